```python
import math
import numpy as np
import jax
import jax.numpy as jnp
from jax import lax

D_MODEL = 1024
BATCH = 8
SEQ = 16384
DEPTH = 1

ATTN_PATTERNS = ((128, 1), (512, 4), (2048, 16))
N_GROUPS_A = 3
HEADS_PER_GROUP = 4
N_HEADS_A = N_GROUPS_A * HEADS_PER_GROUP
HEAD_DIM_A = 128
D_ATTN = N_HEADS_A * HEAD_DIM_A
D_ATTN_OUT = HEADS_PER_GROUP * HEAD_DIM_A
BLOCK_A = 128
N_HEADS_B = 8
HEAD_DIM_K = 128
HEAD_DIM_V = 128
D_KEY_B = N_HEADS_B * HEAD_DIM_K
D_VAL_B = N_HEADS_B * HEAD_DIM_V
CONV_WIDTH = 4
CHUNK = 64
N_BRANCHES = 2
D_FF = ((8 * D_MODEL // 3 + 255) // 256) * 256
EPS = 1e-6
IN_SPLITS = (D_ATTN, D_ATTN, D_ATTN, D_KEY_B, D_KEY_B, D_VAL_B, D_VAL_B, N_HEADS_B, N_HEADS_B, N_BRANCHES * D_MODEL)
D_IN = 3 * D_ATTN + 2 * D_KEY_B + 2 * D_VAL_B + 2 * N_HEADS_B + N_BRANCHES * D_MODEL

kernel_name = 'hybrid_dilated_swa_gated_deltanet'


def rmsnorm(x, w):
    xf = x.astype(jnp.float32)
    xf = xf * lax.rsqrt(jnp.mean(xf * xf, axis=-1, keepdims=True) + EPS)
    return xf.astype(x.dtype) * w


def l2norm(t):
    return t * lax.rsqrt(jnp.sum(t * t, axis=-1, keepdims=True) + EPS)


def alibi_slopes():
    return jnp.exp2(-8.0 * jnp.arange(1, N_HEADS_A + 1, dtype=jnp.float32) / N_HEADS_A)


def dilated_window_attention(q, k, v, slopes, window, dilation):
    b, s, h, dh = q.shape
    f32 = jnp.float32
    w_sub = window // dilation
    span = BLOCK_A * dilation
    s_pad = -(-s // span) * span
    nb = s_pad // span

    def to_blocks(t):
        t = jnp.pad(t, ((0, 0), (0, s_pad - s), (0, 0), (0, 0)))
        return t.reshape(b, nb, BLOCK_A, dilation, h, dh)

    qb, kb, vb = to_blocks(q), to_blocks(k), to_blocks(v)

    def with_prev(t):
        prev = jnp.pad(t[:, :-1], ((0, 0), (1, 0), (0, 0), (0, 0), (0, 0), (0, 0)))
        return jnp.concatenate([prev, t], axis=2)

    kc, vc = with_prev(kb), with_prev(vb)
    scores = jnp.einsum('bnirhd,bnjrhd->bnrhij', qb, kc).astype(f32) * (dh ** -0.5)
    i = jnp.arange(BLOCK_A)[:, None]
    j = jnp.arange(2 * BLOCK_A)[None, :]
    delta = BLOCK_A + i - j
    band = (delta >= 0) & (delta <= w_sub)
    has_prev = (jnp.arange(nb)[:, None, None] > 0) | (j >= BLOCK_A)[None]
    valid = band[None] & has_prev
    bias = -(slopes.astype(f32) * dilation)[:, None, None] * delta.astype(f32)
    scores = jnp.where(valid[None, :, None, None], scores + bias, -jnp.inf)
    m = jnp.max(scores, axis=-1, keepdims=True)
    p = jnp.exp(scores - m)
    den = jnp.sum(p, axis=-1)
    o = jnp.einsum('bnrhij,bnjrhd->bnirhd', p, vc.astype(f32))
    o = o / jnp.transpose(den, (0, 1, 4, 2, 3))[..., None]
    lse = jnp.transpose(m[..., 0] + jnp.log(den), (0, 1, 4, 2, 3))
    o = o.reshape(b, s_pad, h, dh)[:, :s]
    lse = lse.reshape(b, s_pad, h)[:, :s]
    return o, lse


def causal_depthwise_conv(x, w):
    c = x.shape[-1]
    return lax.conv_general_dilated(x, w[:, None, :].astype(x.dtype), window_strides=(1,),
                                    padding=((CONV_WIDTH - 1, 0),),
                                    dimension_numbers=('NWC', 'WIO', 'NWC'),
                                    feature_group_count=c)


def gated_delta_rule(q, k, v, g, beta):
    b, s, h, dk = q.shape
    dv = v.shape[-1]
    n = s // CHUNK

    def chunks(t):
        return jnp.moveaxis(t.reshape(b, n, CHUNK, h, -1), 3, 1)

    q, k, v = chunks(q), chunks(k), chunks(v)
    g = jnp.moveaxis(g.reshape(b, n, CHUNK, h), 3, 1)
    beta = jnp.moveaxis(beta.reshape(b, n, CHUNK, h), 3, 1)
    gc = jnp.cumsum(g, axis=-1)
    incl = jnp.tril(jnp.ones((CHUNK, CHUNK), dtype=bool))
    strict = jnp.tril(jnp.ones((CHUNK, CHUNK), dtype=bool), -1)
    decay = jnp.exp(jnp.where(incl, gc[..., :, None] - gc[..., None, :], -jnp.inf))
    kk = jnp.einsum('bhnid,bhnjd->bhnij', k, k)
    a = jnp.where(strict, beta[..., :, None] * kk * decay, 0.0)
    eye = jnp.eye(CHUNK, dtype=a.dtype)
    rhs = jnp.concatenate([beta[..., None] * v, (beta * jnp.exp(gc))[..., None] * k], axis=-1)
    sol = lax.linalg.triangular_solve(eye + a, rhs, left_side=True, lower=True, unit_diagonal=True)
    u_bar, w = sol[..., :dv], sol[..., dv:]
    qk = jnp.einsum('bhnid,bhnjd->bhnij', q, k) * decay
    q_dec = q * jnp.exp(gc)[..., None]
    k_dec = k * jnp.exp(gc[..., -1:] - gc)[..., None]
    g_last = jnp.exp(gc[..., -1])

    def step(state, xs):
        u_bar_c, w_c, qk_c, q_c, k_c, gl_c = xs
        u = u_bar_c - jnp.einsum('bhck,bhkv->bhcv', w_c, state)
        o = jnp.einsum('bhck,bhkv->bhcv', q_c, state) + jnp.einsum('bhcj,bhjv->bhcv', qk_c, u)
        state = gl_c[..., None, None] * state + jnp.einsum('bhck,bhcv->bhkv', k_c, u)
        return state, o

    xs = tuple(jnp.moveaxis(t, 2, 0) for t in (u_bar, w, qk, q_dec, k_dec, g_last))
    state0 = jnp.zeros((b, h, dk, dv), jnp.float32)
    _, o = lax.scan(step, state0, xs)
    return jnp.transpose(o, (1, 0, 3, 2, 4)).reshape(b, s, h, dv)


def _fwd_setup_inputs(seed: int = 0) -> dict:
    key = jax.random.key(seed)
    ks = jax.random.split(key, 16)
    f32 = jnp.float32

    def nrm(k, shape, fan_in):
        return jax.random.normal(k, shape, f32) * (fan_in ** -0.5)

    def gain(k, shape):
        return 1.0 + 0.02 * jax.random.normal(k, shape, f32)

    x = jax.random.normal(ks[0], (BATCH, SEQ, D_MODEL), f32)
    a_log = jnp.log(jax.random.uniform(ks[4], (DEPTH, N_HEADS_B), f32, minval=1.0, maxval=16.0))
    dt = jnp.exp(jax.random.uniform(ks[5], (DEPTH, N_HEADS_B), f32, minval=math.log(1e-3), maxval=math.log(1e-1)))
    dt_bias = dt + jnp.log(-jnp.expm1(-dt))
    return {
        'x': x,
        'norm_mix': gain(ks[1], (DEPTH, D_MODEL)),
        'w_in': nrm(ks[2], (DEPTH, D_MODEL, D_IN), D_MODEL),
        'conv_w': nrm(ks[3], (DEPTH, CONV_WIDTH, 2 * D_KEY_B + D_VAL_B), CONV_WIDTH),
        'a_log': a_log,
        'dt_bias': dt_bias,
        'dn_norm': gain(ks[6], (DEPTH, HEAD_DIM_V)),
        'w_proj_attn': nrm(ks[7], (DEPTH, D_ATTN_OUT, D_MODEL), D_ATTN_OUT),
        'w_proj_delta': nrm(ks[8], (DEPTH, D_VAL_B, D_MODEL), D_VAL_B),
        'w_out': nrm(ks[9], (DEPTH, D_MODEL, D_MODEL), D_MODEL),
        'norm_ffn': gain(ks[10], (DEPTH, D_MODEL)),
        'w_gate': nrm(ks[11], (DEPTH, D_MODEL, D_FF), D_MODEL),
        'w_up': nrm(ks[12], (DEPTH, D_MODEL, D_FF), D_MODEL),
        'w_down': nrm(ks[13], (DEPTH, D_FF, D_MODEL), D_FF),
        'norm_final': gain(ks[14], (D_MODEL,)),
    }


def _fwd_reference(x, norm_mix, w_in, conv_w, a_log, dt_bias, dn_norm, w_proj_attn, w_proj_delta,
              w_out, norm_ffn, w_gate, w_up, w_down, norm_final):
    b, s, _ = x.shape
    f32 = jnp.float32
    slopes = alibi_slopes()
    split_at = np.cumsum(IN_SPLITS)[:-1].tolist()
    h = x
    for layer in range(DEPTH):
        u = rmsnorm(h, norm_mix[layer])
        proj = u @ w_in[layer]
        qa, ka, va, qd, kd, vd, z, beta_raw, a_raw, gate_raw = jnp.split(proj, split_at, axis=-1)

        qa = qa.reshape(b, s, N_HEADS_A, HEAD_DIM_A)
        ka = ka.reshape(b, s, N_HEADS_A, HEAD_DIM_A)
        va = va.reshape(b, s, N_HEADS_A, HEAD_DIM_A)
        outs, lses = [], []
        for gi, (window, dilation) in enumerate(ATTN_PATTERNS):
            hs = slice(gi * HEADS_PER_GROUP, (gi + 1) * HEADS_PER_GROUP)
            o, lse = dilated_window_attention(qa[:, :, hs], ka[:, :, hs], va[:, :, hs], slopes[hs], window, dilation)
            outs.append(o)
            lses.append(lse)
        mix_w = jax.nn.softmax(jnp.stack(lses), axis=0)
        y_a = jnp.einsum('gbsh,gbshd->bshd', mix_w, jnp.stack(outs)).reshape(b, s, D_ATTN_OUT).astype(x.dtype)
        y_a = y_a @ w_proj_attn[layer]

        qkv = jax.nn.silu(causal_depthwise_conv(jnp.concatenate([qd, kd, vd], axis=-1), conv_w[layer]))
        qd, kd, vd = jnp.split(qkv, [D_KEY_B, 2 * D_KEY_B], axis=-1)
        qd = l2norm(qd.reshape(b, s, N_HEADS_B, HEAD_DIM_K).astype(f32)) * (HEAD_DIM_K ** -0.5)
        kd = l2norm(kd.reshape(b, s, N_HEADS_B, HEAD_DIM_K).astype(f32))
        vd = vd.reshape(b, s, N_HEADS_B, HEAD_DIM_V).astype(f32)
        beta = jax.nn.sigmoid(beta_raw.astype(f32))
        g = -jnp.exp(a_log[layer].astype(f32)) * jax.nn.softplus(a_raw.astype(f32) + dt_bias[layer].astype(f32))
        o_d = gated_delta_rule(qd, kd, vd, g, beta)
        o_d = rmsnorm(o_d, dn_norm[layer].astype(f32)) * jax.nn.silu(z.reshape(b, s, N_HEADS_B, HEAD_DIM_V).astype(f32))
        y_b = o_d.reshape(b, s, D_VAL_B).astype(x.dtype) @ w_proj_delta[layer]

        gate_a, gate_b = jnp.split(jax.nn.sigmoid(gate_raw), N_BRANCHES, axis=-1)
        h = h + (gate_a * y_a + gate_b * y_b) @ w_out[layer]

        hn = rmsnorm(h, norm_ffn[layer])
        h = h + (jax.nn.silu(hn @ w_gate[layer]) * (hn @ w_up[layer])) @ w_down[layer]
    return rmsnorm(h, norm_final)


import jax as _jax
import jax.numpy as _jnp

TWIN_FORMAT = 'train_step'
FWD_PARAMS = ['x', 'norm_mix', 'w_in', 'conv_w', 'a_log', 'dt_bias', 'dn_norm', 'w_proj_attn', 'w_proj_delta', 'w_out', 'norm_ffn', 'w_gate', 'w_up', 'w_down', 'norm_final']
TWIN_WEIGHTS = ['norm_mix', 'w_in', 'conv_w', 'a_log', 'dt_bias', 'dn_norm', 'w_proj_attn', 'w_proj_delta', 'w_out', 'norm_ffn', 'w_gate', 'w_up', 'w_down', 'norm_final']
TWIN_DIFF_INPUT = 'x'
TWIN_INPUTS = ['x', 'norm_mix', 'w_in', 'conv_w', 'a_log', 'dt_bias', 'dn_norm', 'w_proj_attn', 'w_proj_delta', 'w_out', 'norm_ffn', 'w_gate', 'w_up', 'w_down', 'norm_final', 'loss_target', 'm_norm_mix', 'm_w_in', 'm_conv_w', 'm_a_log', 'm_dt_bias', 'm_dn_norm', 'm_w_proj_attn', 'm_w_proj_delta', 'm_w_out', 'm_norm_ffn', 'm_w_gate', 'm_w_up', 'm_w_down', 'm_norm_final', 'v_norm_mix', 'v_w_in', 'v_conv_w', 'v_a_log', 'v_dt_bias', 'v_dn_norm', 'v_w_proj_attn', 'v_w_proj_delta', 'v_w_out', 'v_norm_ffn', 'v_w_gate', 'v_w_up', 'v_w_down', 'v_norm_final']
TWIN_OUTPUTS = ['loss', 'grad_x', 'grad_norm_mix', 'grad_w_in', 'grad_conv_w', 'grad_a_log', 'grad_dt_bias', 'grad_dn_norm', 'grad_w_proj_attn', 'grad_w_proj_delta', 'grad_w_out', 'grad_norm_ffn', 'grad_w_gate', 'grad_w_up', 'grad_w_down', 'grad_norm_final', 'delta_norm_mix', 'delta_w_in', 'delta_conv_w', 'delta_a_log', 'delta_dt_bias', 'delta_dn_norm', 'delta_w_proj_attn', 'delta_w_proj_delta', 'delta_w_out', 'delta_norm_ffn', 'delta_w_gate', 'delta_w_up', 'delta_w_down', 'delta_norm_final', 'new_m_norm_mix', 'new_m_w_in', 'new_m_conv_w', 'new_m_a_log', 'new_m_dt_bias', 'new_m_dn_norm', 'new_m_w_proj_attn', 'new_m_w_proj_delta', 'new_m_w_out', 'new_m_norm_ffn', 'new_m_w_gate', 'new_m_w_up', 'new_m_w_down', 'new_m_norm_final', 'new_v_norm_mix', 'new_v_w_in', 'new_v_conv_w', 'new_v_a_log', 'new_v_dt_bias', 'new_v_dn_norm', 'new_v_w_proj_attn', 'new_v_w_proj_delta', 'new_v_w_out', 'new_v_norm_ffn', 'new_v_w_gate', 'new_v_w_up', 'new_v_w_down', 'new_v_norm_final']
TWIN_LEAF_KINDS = {'loss': 'loss', 'grad_x': 'grad_x', 'grad_norm_mix': 'grad_w', 'grad_w_in': 'grad_w', 'grad_conv_w': 'grad_w', 'grad_a_log': 'grad_w', 'grad_dt_bias': 'grad_w', 'grad_dn_norm': 'grad_w', 'grad_w_proj_attn': 'grad_w', 'grad_w_proj_delta': 'grad_w', 'grad_w_out': 'grad_w', 'grad_norm_ffn': 'grad_w', 'grad_w_gate': 'grad_w', 'grad_w_up': 'grad_w', 'grad_w_down': 'grad_w', 'grad_norm_final': 'grad_w', 'delta_norm_mix': 'delta_w', 'delta_w_in': 'delta_w', 'delta_conv_w': 'delta_w', 'delta_a_log': 'delta_w', 'delta_dt_bias': 'delta_w', 'delta_dn_norm': 'delta_w', 'delta_w_proj_attn': 'delta_w', 'delta_w_proj_delta': 'delta_w', 'delta_w_out': 'delta_w', 'delta_norm_ffn': 'delta_w', 'delta_w_gate': 'delta_w', 'delta_w_up': 'delta_w', 'delta_w_down': 'delta_w', 'delta_norm_final': 'delta_w', 'new_m_norm_mix': 'new_m', 'new_m_w_in': 'new_m', 'new_m_conv_w': 'new_m', 'new_m_a_log': 'new_m', 'new_m_dt_bias': 'new_m', 'new_m_dn_norm': 'new_m', 'new_m_w_proj_attn': 'new_m', 'new_m_w_proj_delta': 'new_m', 'new_m_w_out': 'new_m', 'new_m_norm_ffn': 'new_m', 'new_m_w_gate': 'new_m', 'new_m_w_up': 'new_m', 'new_m_w_down': 'new_m', 'new_m_norm_final': 'new_m', 'new_v_norm_mix': 'new_v', 'new_v_w_in': 'new_v', 'new_v_conv_w': 'new_v', 'new_v_a_log': 'new_v', 'new_v_dt_bias': 'new_v', 'new_v_dn_norm': 'new_v', 'new_v_w_proj_attn': 'new_v', 'new_v_w_proj_delta': 'new_v', 'new_v_w_out': 'new_v', 'new_v_norm_ffn': 'new_v', 'new_v_w_gate': 'new_v', 'new_v_w_up': 'new_v', 'new_v_w_down': 'new_v', 'new_v_norm_final': 'new_v'}


def _forward(args):
    return _fwd_reference(*[args[k] for k in FWD_PARAMS])


def _output_shape():
    def fwd():
        inp = _fwd_setup_inputs(0)
        return _fwd_reference(*[inp[k] for k in FWD_PARAMS])
    out = _jax.eval_shape(fwd)
    return out.shape, out.dtype

N_MICROBATCH = 1
ADAM_LR = 0.001
ADAM_B1 = 0.9
ADAM_B2 = 0.999
ADAM_EPS = 1e-08
ADAM_WD = 0.01
ADAM_STEP = 10
PER_EXAMPLE_BATCH_AXIS = {'x': 0, 'loss_target': 0}
SHARED_INPUTS = []
_WEIGHT_DTYPES = {'norm_mix': _jnp.float32, 'w_in': _jnp.float32, 'conv_w': _jnp.float32, 'a_log': _jnp.float32, 'dt_bias': _jnp.float32, 'dn_norm': _jnp.float32, 'w_proj_attn': _jnp.float32, 'w_proj_delta': _jnp.float32, 'w_out': _jnp.float32, 'norm_ffn': _jnp.float32, 'w_gate': _jnp.float32, 'w_up': _jnp.float32, 'w_down': _jnp.float32, 'norm_final': _jnp.float32}
MOMENT_SCALE = {'norm_mix': 2.310899e-01, 'w_in': 7.060504e-02, 'conv_w': 9.100567e-02, 'a_log': 5.290163e-01, 'dt_bias': 4.295252e-01, 'dn_norm': 4.036915e-01, 'w_proj_attn': 6.626692e-02, 'w_proj_delta': 1.196795e-01, 'w_out': 1.339639e-01, 'norm_ffn': 2.527557e-01, 'w_gate': 1.087334e-01, 'w_up': 1.053633e-01, 'w_down': 1.753620e-01, 'norm_final': 1.280070e+02}


def _to_microbatches(a, axis):
    t = _jnp.moveaxis(a, axis, 0)
    t = t.reshape((N_MICROBATCH, t.shape[0] // N_MICROBATCH) + t.shape[1:])
    return _jnp.moveaxis(t, 1, axis + 1)


def setup_inputs(seed: int = 0) -> dict:
    inp = _fwd_setup_inputs(seed)
    key = _jax.random.fold_in(_jax.random.key(seed), 7919)
    shape, _ = _output_shape()
    out = dict(inp)
    out["loss_target"] = _jax.random.normal(_jax.random.fold_in(key, 0), shape, _jnp.float32)
    for i, name in enumerate(TWIN_WEIGHTS):
        w = inp[name].astype(_jnp.float32)
        if MOMENT_SCALE is None:
            s = _jnp.sqrt(_jnp.mean(_jnp.square(w)) + 1e-30)
        else:
            s = MOMENT_SCALE[name]
        km, kv = _jax.random.split(_jax.random.fold_in(key, i + 1))
        out[name] = w
        out["m_" + name] = s * _jax.random.normal(km, w.shape, _jnp.float32)
        out["v_" + name] = (s * s) * _jax.random.uniform(kv, w.shape, _jnp.float32, 0.5, 1.5)
    if N_MICROBATCH > 1:
        for name, axis in PER_EXAMPLE_BATCH_AXIS.items():
            out[name] = _to_microbatches(out[name], axis)
    return {'x': out['x'], 'norm_mix': out['norm_mix'], 'w_in': out['w_in'], 'conv_w': out['conv_w'], 'a_log': out['a_log'], 'dt_bias': out['dt_bias'], 'dn_norm': out['dn_norm'], 'w_proj_attn': out['w_proj_attn'], 'w_proj_delta': out['w_proj_delta'], 'w_out': out['w_out'], 'norm_ffn': out['norm_ffn'], 'w_gate': out['w_gate'], 'w_up': out['w_up'], 'w_down': out['w_down'], 'norm_final': out['norm_final'], 'loss_target': out['loss_target'], 'm_norm_mix': out['m_norm_mix'], 'm_w_in': out['m_w_in'], 'm_conv_w': out['m_conv_w'], 'm_a_log': out['m_a_log'], 'm_dt_bias': out['m_dt_bias'], 'm_dn_norm': out['m_dn_norm'], 'm_w_proj_attn': out['m_w_proj_attn'], 'm_w_proj_delta': out['m_w_proj_delta'], 'm_w_out': out['m_w_out'], 'm_norm_ffn': out['m_norm_ffn'], 'm_w_gate': out['m_w_gate'], 'm_w_up': out['m_w_up'], 'm_w_down': out['m_w_down'], 'm_norm_final': out['m_norm_final'], 'v_norm_mix': out['v_norm_mix'], 'v_w_in': out['v_w_in'], 'v_conv_w': out['v_conv_w'], 'v_a_log': out['v_a_log'], 'v_dt_bias': out['v_dt_bias'], 'v_dn_norm': out['v_dn_norm'], 'v_w_proj_attn': out['v_w_proj_attn'], 'v_w_proj_delta': out['v_w_proj_delta'], 'v_w_out': out['v_w_out'], 'v_norm_ffn': out['v_norm_ffn'], 'v_w_gate': out['v_w_gate'], 'v_w_up': out['v_w_up'], 'v_w_down': out['v_w_down'], 'v_norm_final': out['v_norm_final']}


def _loss(weights, diff, rest, loss_target):
    with _jax.named_scope("forward"):
        args = {**rest, TWIN_DIFF_INPUT: diff, **{k: w.astype(_WEIGHT_DTYPES[k]) for k, w in weights.items()}}
        y = _forward(args)
    with _jax.named_scope("loss_head"):
        err = _jnp.square(y.astype(_jnp.float32) - loss_target)
        return 0.5 * _jnp.sum(_jnp.mean(err, axis=-1)) if err.ndim else 0.5 * err


def _adamw(w, g, m, v):
    m = ADAM_B1 * m + (1.0 - ADAM_B1) * g
    v = ADAM_B2 * v + (1.0 - ADAM_B2) * _jnp.square(g)
    m_hat = m / (1.0 - ADAM_B1 ** ADAM_STEP)
    v_hat = v / (1.0 - ADAM_B2 ** ADAM_STEP)
    delta = -ADAM_LR * (m_hat / (_jnp.sqrt(v_hat) + ADAM_EPS) + ADAM_WD * w)
    return delta, m, v


def reference(x, norm_mix, w_in, conv_w, a_log, dt_bias, dn_norm, w_proj_attn, w_proj_delta, w_out, norm_ffn, w_gate, w_up, w_down, norm_final, loss_target, m_norm_mix, m_w_in, m_conv_w, m_a_log, m_dt_bias, m_dn_norm, m_w_proj_attn, m_w_proj_delta, m_w_out, m_norm_ffn, m_w_gate, m_w_up, m_w_down, m_norm_final, v_norm_mix, v_w_in, v_conv_w, v_a_log, v_dt_bias, v_dn_norm, v_w_proj_attn, v_w_proj_delta, v_w_out, v_norm_ffn, v_w_gate, v_w_up, v_w_down, v_norm_final):
    given = dict(x=x, norm_mix=norm_mix, w_in=w_in, conv_w=conv_w, a_log=a_log, dt_bias=dt_bias, dn_norm=dn_norm, w_proj_attn=w_proj_attn, w_proj_delta=w_proj_delta, w_out=w_out, norm_ffn=norm_ffn, w_gate=w_gate, w_up=w_up, w_down=w_down, norm_final=norm_final, loss_target=loss_target, m_norm_mix=m_norm_mix, m_w_in=m_w_in, m_conv_w=m_conv_w, m_a_log=m_a_log, m_dt_bias=m_dt_bias, m_dn_norm=m_dn_norm, m_w_proj_attn=m_w_proj_attn, m_w_proj_delta=m_w_proj_delta, m_w_out=m_w_out, m_norm_ffn=m_norm_ffn, m_w_gate=m_w_gate, m_w_up=m_w_up, m_w_down=m_w_down, m_norm_final=m_norm_final, v_norm_mix=v_norm_mix, v_w_in=v_w_in, v_conv_w=v_conv_w, v_a_log=v_a_log, v_dt_bias=v_dt_bias, v_dn_norm=v_dn_norm, v_w_proj_attn=v_w_proj_attn, v_w_proj_delta=v_w_proj_delta, v_w_out=v_w_out, v_norm_ffn=v_norm_ffn, v_w_gate=v_w_gate, v_w_up=v_w_up, v_w_down=v_w_down, v_norm_final=v_norm_final)
    weights = {n: given[n] for n in TWIN_WEIGHTS}
    shared = {n: given[n] for n in SHARED_INPUTS}
    per_example = {n: given[n] for n in ['x']}
    grad_fn = _jax.value_and_grad(_loss, argnums=(0, 1))

    def one_microbatch(ex, loss_target):
        ex = dict(ex)
        diff = ex.pop(TWIN_DIFF_INPUT)
        return grad_fn(weights, diff, {**shared, **ex}, loss_target)

    if N_MICROBATCH == 1:
        loss, (grad_w, grad_x) = one_microbatch(per_example, given["loss_target"])
    else:
        def body(carry, xs):
            loss_sum, grad_sum = carry
            l_k, (gw_k, gx_k) = one_microbatch(xs[0], xs[1])
            with _jax.named_scope("update"):
                return (loss_sum + l_k, _jax.tree.map(_jnp.add, grad_sum, gw_k)), gx_k

        init = (_jnp.zeros((), _jnp.float32), _jax.tree.map(_jnp.zeros_like, weights))
        (loss, grad_w), grad_x = _jax.lax.scan(body, init, (per_example, given["loss_target"]))
    with _jax.named_scope("update"):
        delta_w, new_m, new_v = {}, {}, {}
        for n in TWIN_WEIGHTS:
            delta_w[n], new_m[n], new_v[n] = _adamw(weights[n], grad_w[n], given["m_" + n], given["v_" + n])
    return (loss, grad_x, *[grad_w[n] for n in TWIN_WEIGHTS], *[delta_w[n] for n in TWIN_WEIGHTS],
            *[new_m[n] for n in TWIN_WEIGHTS], *[new_v[n] for n in TWIN_WEIGHTS])
```

```python
import functools
import math

import jax
import jax.numpy as jnp
from jax import lax
from jax.experimental import pallas as pl
from jax.experimental.pallas import tpu as pltpu

F32 = jnp.float32
BF16 = jnp.bfloat16
MESH = pl.DeviceIdType.MESH

D_MODEL = 1024
N_HEADS_A = 12
HEAD_DIM = 128
BLOCK_A = 128
DILATIONS = (1, 4, 16)
W_SUB = 128
D_ATTN_OUT = 512
N_HEADS_B = 8
D_KEY_B = 1024
CONV_WIDTH = 4
CHUNK = 64
D_FF = 2816
EPS = 1e-6
D_IN = 10768
D_MAIN = 10752
C_QKVD, C_Z, C_GATE, C_QA, C_KA, C_VA = 0, 3072, 4096, 6144, 7680, 9216

ADAM_LR, ADAM_B1, ADAM_B2, ADAM_EPS, ADAM_WD, ADAM_STEP = 0.001, 0.9, 0.999, 1e-08, 0.01, 10

PACK_ROWS = (("w_in", 2692), ("w_proj_attn", 128), ("w_proj_delta", 256), ("w_out", 256),
             ("w_gate", 704), ("w_up", 704), ("w_down", 704), ("conv_w", 3))
PACK_TILE = 512
PACK_R = 5632
PACK_R16 = 5632

VMEM_LIMIT = 48 * 1024 * 1024


def _cparams(n):
    return pltpu.CompilerParams(dimension_semantics=("arbitrary",) * n, vmem_limit_bytes=VMEM_LIMIT)


_ARB = _cparams(1)


def _row_spec(tm, w, cb):
    return pl.BlockSpec((tm, w), lambda i: (i, cb))


def _full_spec(shape):
    nd = len(shape)
    return pl.BlockSpec(shape, lambda i: (0,) * nd)


def _rowwise(name, fn, ins, params, outs, tm, T):
    n_in, n_p = len(ins), len(params)

    def body(*refs):
        xs = [r[...] for r in refs[:n_in + n_p]]
        res = fn(*xs)
        for o_ref, v in zip(refs[n_in + n_p:], res):
            o_ref[...] = v.astype(o_ref.dtype)

    return pl.pallas_call(
        body, name=name, grid=(T // tm,),
        in_specs=[_row_spec(tm, w, cb) for (_, cb, w) in ins] + [_full_spec(p.shape) for p in params],
        out_specs=[_row_spec(tm, w, 0) for (w, _) in outs],
        out_shape=[jax.ShapeDtypeStruct((T, w), dt) for (w, dt) in outs],
        compiler_params=_ARB,
    )(*[a for (a, _, _) in ins], *params)


def _rowwise_vjp(name, fn, ins, params, cts, gdtypes, tm, T, residual=None):
    n_in, n_p, n_ct = len(ins), len(params), len(cts)
    n_res = 0 if residual is None else 1

    def body(*refs):
        i = pl.program_id(0)
        k = 0
        xs = [r[...].astype(F32) for r in refs[k:k + n_in]]
        k += n_in
        ps = [r[...] for r in refs[k:k + n_p]]
        k += n_p
        gs = tuple(r[...].astype(F32) for r in refs[k:k + n_ct])
        k += n_ct
        res_refs = refs[k:k + n_res]
        k += n_res
        gi_refs = refs[k:k + n_in]
        gp_refs = refs[k + n_in:]
        _, vjp = jax.vjp(fn, *xs, *ps)
        g = vjp(gs)
        for j, (ref, v) in enumerate(zip(gi_refs, g[:n_in])):
            if j == 0 and n_res:
                v = v + res_refs[0][...].astype(F32)
            ref[...] = v.astype(ref.dtype)

        @pl.when(i == 0)
        def _():
            for ref in gp_refs:
                ref[...] = jnp.zeros_like(ref)

        for ref, v in zip(gp_refs, g[n_in:]):
            ref[...] += v

    res_in = [] if residual is None else [residual]
    return pl.pallas_call(
        body, name=name, grid=(T // tm,),
        in_specs=([_row_spec(tm, w, cb) for (_, cb, w) in ins] + [_full_spec(p.shape) for p in params]
                  + [_row_spec(tm, w, cb) for (_, cb, w) in cts] + [_row_spec(tm, w, cb) for (_, cb, w) in res_in]),
        out_specs=[_row_spec(tm, w, 0) for (_, _, w) in ins] + [_full_spec(p.shape) for p in params],
        out_shape=([jax.ShapeDtypeStruct((T, w), dt) for (_, _, w), dt in zip(ins, gdtypes)]
                   + [jax.ShapeDtypeStruct(p.shape, F32) for p in params]),
        compiler_params=_ARB,
    )(*[a for (a, _, _) in ins], *params, *[a for (a, _, _) in cts], *[a for (a, _, _) in res_in])


def _matmul(name, a, b, mode, M, N, K, tm, tn, tk, out_dtype, a_off=0, b_off=0, residual=None):
    nk = K // tk
    assert M % tm == 0 and N % tn == 0 and K % tk == 0
    if mode == "nn":
        assert a_off % tk == 0 and b_off % tn == 0
        a_spec = pl.BlockSpec((tm, tk), lambda i, j, k: (i, a_off // tk + k))
        b_spec = pl.BlockSpec((tk, tn), lambda i, j, k: (k, b_off // tn + j))
        dims = (((1,), (0,)), ((), ()))
    elif mode == "nt":
        assert a_off % tk == 0 and b_off == 0
        a_spec = pl.BlockSpec((tm, tk), lambda i, j, k: (i, a_off // tk + k))
        b_spec = pl.BlockSpec((tn, tk), lambda i, j, k: (j, k))
        dims = (((1,), (1,)), ((), ()))
    else:
        assert a_off % tm == 0 and b_off % tn == 0
        a_spec = pl.BlockSpec((tk, tm), lambda i, j, k: (k, a_off // tm + i))
        b_spec = pl.BlockSpec((tk, tn), lambda i, j, k: (k, b_off // tn + j))
        dims = (((0,), (0,)), ((), ()))
    has_res = residual is not None

    def body(*refs):
        a_ref, b_ref = refs[0], refs[1]
        r_ref = refs[2] if has_res else None
        o_ref, acc_ref = refs[-2], refs[-1]
        k = pl.program_id(2)

        @pl.when(k == 0)
        def _():
            acc_ref[...] = jnp.zeros_like(acc_ref)

        acc_ref[...] += lax.dot_general(a_ref[...].astype(BF16), b_ref[...].astype(BF16), dims,
                                        preferred_element_type=F32)

        @pl.when(k == nk - 1)
        def _():
            acc = acc_ref[...]
            if has_res:
                acc = acc + r_ref[...].astype(F32)
            o_ref[...] = acc.astype(o_ref.dtype)

    in_specs = [a_spec, b_spec]
    args = [a, b]
    if has_res:
        in_specs.append(pl.BlockSpec((tm, tn), lambda i, j, k: (i, j)))
        args.append(residual)
    return pl.pallas_call(
        body, name=name, grid=(M // tm, N // tn, nk),
        in_specs=in_specs,
        out_specs=pl.BlockSpec((tm, tn), lambda i, j, k: (i, j)),
        out_shape=jax.ShapeDtypeStruct((M, N), out_dtype),
        scratch_shapes=[pltpu.VMEM((tm, tn), F32)],
        compiler_params=pltpu.CompilerParams(dimension_semantics=("parallel", "parallel", "arbitrary"),
                                             vmem_limit_bytes=VMEM_LIMIT),
    )(*args)


def _f_rms(x, w):
    return (x * lax.rsqrt(jnp.mean(x * x, axis=-1, keepdims=True) + EPS) * w,)


def _silu(x):
    return x * jax.nn.sigmoid(x)


def _softplus(x):
    return jnp.maximum(x, 0.0) + jnp.log1p(jnp.exp(-jnp.abs(x)))


def _heads(x, n):
    return [x[:, HEAD_DIM * h:HEAD_DIM * (h + 1)] for h in range(n)]


def _f_dnpoint(c, ba, alog_v, dtb_v):
    cs = _silu(c)
    q, k, v = cs[:, :D_KEY_B], cs[:, D_KEY_B:2 * D_KEY_B], cs[:, 2 * D_KEY_B:]
    qn = jnp.concatenate([t * lax.rsqrt(jnp.sum(t * t, axis=-1, keepdims=True) + EPS) * (HEAD_DIM ** -0.5)
                          for t in _heads(q, N_HEADS_B)], axis=1)
    kn = jnp.concatenate([t * lax.rsqrt(jnp.sum(t * t, axis=-1, keepdims=True) + EPS)
                          for t in _heads(k, N_HEADS_B)], axis=1)
    lane = lax.broadcasted_iota(jnp.int32, ba.shape, 1)
    beta = jax.nn.sigmoid(ba)
    g = -jnp.exp(alog_v) * _softplus(ba + dtb_v)
    bg = jnp.where(lane < N_HEADS_B, beta, jnp.where(lane < 2 * N_HEADS_B, g, 0.0))
    return qn, kn, v, bg


def _f_postdn(o, z, w):
    outs = []
    for oh, zh in zip(_heads(o, N_HEADS_B), _heads(z, N_HEADS_B)):
        outs.append(oh * lax.rsqrt(jnp.mean(oh * oh, axis=-1, keepdims=True) + EPS) * w * _silu(zh))
    return (jnp.concatenate(outs, axis=1),)


def _f_merge(o0, o1, o2, l0, l1, l2):
    m = lax.stop_gradient(jnp.maximum(jnp.maximum(l0, l1), l2))
    e0, e1, e2 = jnp.exp(l0 - m), jnp.exp(l1 - m), jnp.exp(l2 - m)
    return ((e0 * o0 + e1 * o1 + e2 * o2) / (e0 + e1 + e2),)


def _f_mix(ya, yb, gr):
    return (jax.nn.sigmoid(gr[:, :D_MODEL]) * ya + jax.nn.sigmoid(gr[:, D_MODEL:]) * yb,)


def _f_act(gu):
    return (_silu(gu[:, :D_FF]) * gu[:, D_FF:],)


def _attn_block(q, kp, kc, vp, vc, slope_d, first_key):
    k2 = jnp.concatenate([kp, kc], axis=0).astype(BF16)
    v2 = jnp.concatenate([vp, vc], axis=0).astype(BF16)
    s = lax.dot_general(q.astype(BF16), k2, (((1,), (1,)), ((), ())), preferred_element_type=F32)
    s = s * (HEAD_DIM ** -0.5)
    i = lax.broadcasted_iota(jnp.int32, (BLOCK_A, 2 * BLOCK_A), 0)
    j = lax.broadcasted_iota(jnp.int32, (BLOCK_A, 2 * BLOCK_A), 1)
    delta = BLOCK_A + i - j
    valid = (delta >= 0) & (delta <= W_SUB) & (j >= first_key)
    s = jnp.where(valid, s - slope_d * delta.astype(F32), -jnp.inf)
    m = lax.stop_gradient(jnp.max(s, axis=-1, keepdims=True))
    p = jnp.exp(s - m)
    den = jnp.sum(p, axis=-1, keepdims=True)
    o = jnp.dot(p.astype(BF16), v2, preferred_element_type=F32) / den
    lse = jnp.broadcast_to(m + jnp.log(den), (BLOCK_A, HEAD_DIM))
    return o, lse


def _slope_d(group, h, d):
    hg = (group * 4 + h + 1).astype(F32)
    return jnp.exp(jnp.full((1, 1), -8.0 * math.log(2.0) / N_HEADS_A, F32) * hg) * float(d)


def _attn_fwd(name, proj, group, T):
    d = DILATIONS[group]
    span = BLOCK_A * d
    nb = T // span
    qb, kb, vb = C_QA // 128 + 4 * group, C_KA // 128 + 4 * group, C_VA // 128 + 4 * group

    def body(q_ref, kc_ref, kp_ref, vc_ref, vp_ref, o_ref, l_ref):
        h, n = pl.program_id(0), pl.program_id(1)
        sl = _slope_d(group, h, d)
        first_key = jnp.where(n > 0, 0, BLOCK_A)

        def one(r, carry):
            idx = pl.ds(r, BLOCK_A, stride=d) if d > 1 else pl.ds(0, BLOCK_A)
            o, lse = _attn_block(q_ref[idx, :], kp_ref[idx, :], kc_ref[idx, :], vp_ref[idx, :], vc_ref[idx, :],
                                 sl, first_key)
            o_ref[idx, :] = o
            l_ref[idx, :] = lse
            return carry

        lax.fori_loop(0, d, one, 0)

    def cur(cb):
        return pl.BlockSpec((span, 128), lambda h, n: (n, cb + h))

    def prev(cb):
        return pl.BlockSpec((span, 128), lambda h, n: (jnp.maximum(n - 1, 0), cb + h))

    return pl.pallas_call(
        body, name=name, grid=(4, nb),
        in_specs=[cur(qb), cur(kb), prev(kb), cur(vb), prev(vb)],
        out_specs=[pl.BlockSpec((span, 128), lambda h, n: (n, h))] * 2,
        out_shape=[jax.ShapeDtypeStruct((T, D_ATTN_OUT), F32)] * 2,
        compiler_params=_cparams(2),
    )(proj, proj, proj, proj, proj)


def _attn_bwd(name, proj, do, dl, group, T):
    d = DILATIONS[group]
    span = BLOCK_A * d
    nb = T // span
    qb, kb, vb = C_QA // 128 + 4 * group, C_KA // 128 + 4 * group, C_VA // 128 + 4 * group

    def body(q_ref, kc_ref, kp_ref, vc_ref, vp_ref, do_ref, dl_ref, dq_ref, dk_ref, dv_ref, ck_ref, cv_ref):
        h, n = pl.program_id(0), pl.program_id(1)
        sl = _slope_d(group, h, d)
        first_key = jnp.where(n > 0, 0, BLOCK_A)

        @pl.when(n == 0)
        def _():
            ck_ref[...] = jnp.zeros_like(ck_ref)
            cv_ref[...] = jnp.zeros_like(cv_ref)

        def idx_of(r):
            return pl.ds(r, BLOCK_A, stride=d) if d > 1 else pl.ds(0, BLOCK_A)

        @pl.when(n < nb)
        def _():
            def one(r, carry):
                idx = idx_of(r)
                rows = pl.ds(pl.multiple_of(r * BLOCK_A, BLOCK_A), BLOCK_A)
                f = functools.partial(_attn_block, slope_d=sl, first_key=first_key)
                _, vjp = jax.vjp(f, q_ref[idx, :], kp_ref[idx, :], kc_ref[idx, :], vp_ref[idx, :], vc_ref[idx, :])
                dq, dkp, dkc, dvp, dvc = vjp((do_ref[idx, :], dl_ref[idx, :]))
                dq_ref[idx, :] = dq
                dk_ref[idx, :] = ck_ref[rows, :] + dkp
                dv_ref[idx, :] = cv_ref[rows, :] + dvp
                ck_ref[rows, :] = dkc
                cv_ref[rows, :] = dvc
                return carry

            lax.fori_loop(0, d, one, 0)

        @pl.when(n == nb)
        def _():
            def one(r, carry):
                idx = idx_of(r)
                rows = pl.ds(pl.multiple_of(r * BLOCK_A, BLOCK_A), BLOCK_A)
                dk_ref[idx, :] = ck_ref[rows, :]
                dv_ref[idx, :] = cv_ref[rows, :]
                return carry

            lax.fori_loop(0, d, one, 0)

    def cur(cb):
        return pl.BlockSpec((span, 128), lambda h, n: (jnp.minimum(n, nb - 1), cb + h))

    def prev(cb):
        return pl.BlockSpec((span, 128), lambda h, n: (jnp.maximum(jnp.minimum(n, nb - 1) - 1, 0), cb + h))

    own = pl.BlockSpec((span, 128), lambda h, n: (jnp.minimum(n, nb - 1), h))
    late = pl.BlockSpec((span, 128), lambda h, n: (jnp.maximum(n - 1, 0), h))
    return pl.pallas_call(
        body, name=name, grid=(4, nb + 1),
        in_specs=[cur(qb), cur(kb), prev(kb), cur(vb), prev(vb), own, own],
        out_specs=[own, late, late],
        out_shape=[jax.ShapeDtypeStruct((T, D_ATTN_OUT), F32)] * 3,
        scratch_shapes=[pltpu.VMEM((span, 128), F32), pltpu.VMEM((span, 128), F32)],
        compiler_params=_cparams(2),
    )(proj, proj, proj, proj, proj, do, dl)


HALO = 8


def _conv_taps(ext, w):
    c = ext[HALO:, :] * w[CONV_WIDTH - 1]
    for k in range(1, CONV_WIDTH):
        c = c + pltpu.roll(ext, k, 0)[HALO:, :] * w[CONV_WIDTH - 1 - k]
    return c


def _taps(w_ref):
    return [w_ref[j:j + 1, :] for j in range(CONV_WIDTH)]


def _dn_pre_fwd(name, proj, ba, conv_w, alog_v, dtb_v, tm, T):
    C = 3 * D_KEY_B

    def body(x_ref, ba_ref, w_ref, al_ref, dt_ref, q_ref, k_ref, v_ref, bg_ref, halo_ref):
        i = pl.program_id(0)

        @pl.when(i == 0)
        def _():
            halo_ref[...] = jnp.zeros_like(halo_ref)

        x = x_ref[...]
        ext = jnp.concatenate([halo_ref[...], x], axis=0)
        c = _conv_taps(ext, _taps(w_ref))
        halo_ref[...] = x[tm - HALO:, :]
        qn, kn, v, bg = _f_dnpoint(c, ba_ref[...], al_ref[...], dt_ref[...])
        q_ref[...] = qn
        k_ref[...] = kn
        v_ref[...] = v
        bg_ref[...] = bg

    return pl.pallas_call(
        body, name=name, grid=(T // tm,),
        in_specs=[_row_spec(tm, C, 0), _row_spec(tm, 128, 0), _full_spec(conv_w.shape), _full_spec((1, 128)),
                  _full_spec((1, 128))],
        out_specs=[_row_spec(tm, D_KEY_B, 0)] * 3 + [_row_spec(tm, 128, 0)],
        out_shape=[jax.ShapeDtypeStruct((T, D_KEY_B), F32)] * 3 + [jax.ShapeDtypeStruct((T, 128), F32)],
        scratch_shapes=[pltpu.VMEM((HALO, C), F32)],
        compiler_params=_ARB,
    )(proj, ba, conv_w, alog_v, dtb_v)


def _dn_pre_bwd(name, proj, ba, conv_w, alog_v, dtb_v, dq, dk, dv, dbg, tm, T):
    C = 3 * D_KEY_B
    nt = T // tm
    hb = tm // HALO

    def body(x_ref, xh_ref, ba_ref, w_ref, al_ref, dt_ref, dq_ref, dk_ref, dv_ref, dbg_ref,
             dx_ref, dba_ref, dw_ref, dal_ref, ddt_ref, nxt_ref):
        s = pl.program_id(0)
        i = nt - 1 - s

        @pl.when(s == 0)
        def _():
            nxt_ref[...] = jnp.zeros_like(nxt_ref)
            dw_ref[...] = jnp.zeros_like(dw_ref)
            dal_ref[...] = jnp.zeros_like(dal_ref)
            ddt_ref[...] = jnp.zeros_like(ddt_ref)

        w = _taps(w_ref)
        halo = jnp.where(i > 0, xh_ref[...], 0.0)
        ext = jnp.concatenate([halo, x_ref[...]], axis=0)
        c = _conv_taps(ext, w)
        _, vjp = jax.vjp(_f_dnpoint, c, ba_ref[...], al_ref[...], dt_ref[...])
        dc, dba, dal, ddt = vjp((dq_ref[...], dk_ref[...], dv_ref[...], dbg_ref[...]))
        dba_ref[...] = dba
        dal_ref[...] += dal
        ddt_ref[...] += ddt
        ext2 = jnp.concatenate([dc, nxt_ref[...]], axis=0)
        dx = dc * w[CONV_WIDTH - 1]
        for m in range(1, CONV_WIDTH):
            dx = dx + pltpu.roll(ext2, tm + HALO - m, 0)[:tm, :] * w[CONV_WIDTH - 1 - m]
        dx_ref[...] = dx
        nxt_ref[...] = dc[:HALO, :]
        for j in range(CONV_WIDTH):
            k = CONV_WIDTH - 1 - j
            xs = ext[HALO:, :] if k == 0 else pltpu.roll(ext, k, 0)[HALO:, :]
            dw_ref[j:j + 1, :] += jnp.sum(xs * dc, axis=0, keepdims=True)

    def rev(w):
        return pl.BlockSpec((tm, w), lambda s: (nt - 1 - s, 0))

    return pl.pallas_call(
        body, name=name, grid=(nt,),
        in_specs=[rev(C), pl.BlockSpec((HALO, C), lambda s: (jnp.maximum((nt - 1 - s) * hb - 1, 0), 0)), rev(128),
                  _full_spec(conv_w.shape), _full_spec((1, 128)), _full_spec((1, 128)),
                  rev(D_KEY_B), rev(D_KEY_B), rev(D_KEY_B), rev(128)],
        out_specs=[rev(C), rev(128), _full_spec((8, C)), _full_spec((1, 128)), _full_spec((1, 128))],
        out_shape=[jax.ShapeDtypeStruct((T, C), F32), jax.ShapeDtypeStruct((T, 128), F32),
                   jax.ShapeDtypeStruct((8, C), F32), jax.ShapeDtypeStruct((1, 128), F32),
                   jax.ShapeDtypeStruct((1, 128), F32)],
        scratch_shapes=[pltpu.VMEM((HALO, C), F32)],
        compiler_params=_ARB,
    )(proj, proj, ba, conv_w, alog_v, dtb_v, dq, dk, dv, dbg)


def _bdot(a, b, dims, exact=False):
    if exact:
        return lax.dot_general(a, b, dims, preferred_element_type=F32, precision=lax.Precision.HIGHEST)
    return lax.dot_general(a.astype(BF16), b.astype(BF16), dims, preferred_element_type=F32)


_B_NN = (((2,), (1,)), ((0,), (0,)))
_B_NT = (((2,), (2,)), ((0,), (0,)))
_B_TN = (((1,), (1,)), ((0,), (0,)))


def _delta_chunk(S, q, k, v, bg):
    H, C = N_HEADS_B, CHUNK
    r = lax.broadcasted_iota(jnp.int32, (C, C), 0)
    c = lax.broadcasted_iota(jnp.int32, (C, C), 1)
    incl, strict, eye = r >= c, r > c, r == c
    gc_all = jnp.dot(incl.astype(F32), bg, preferred_element_type=F32, precision=lax.Precision.HIGHEST)
    lane = lax.broadcasted_iota(jnp.int32, (C, 128), 1)

    def col(x, l):
        return jnp.sum(jnp.where(lane == l, x, 0.0), axis=1, keepdims=True)

    beta = jnp.stack([col(bg, h) for h in range(H)])
    gc = jnp.stack([col(gc_all, H + h) for h in range(H)])
    gc_row = jnp.sum(jnp.where(eye[None], jnp.broadcast_to(gc, (H, C, C)), 0.0), axis=1, keepdims=True)
    row = lax.broadcasted_iota(jnp.int32, (H, C, 1), 1)
    gc_last = jnp.sum(jnp.where(row == C - 1, gc, 0.0), axis=1, keepdims=True)
    decay = jnp.exp(jnp.where(incl[None], gc - gc_row, -jnp.inf))
    qh = jnp.stack(_heads(q, H))
    kh = jnp.stack(_heads(k, H))
    vh = jnp.stack(_heads(v, H))
    kk = _bdot(kh, kh, _B_NT, exact=True)
    a = jnp.where(strict[None], beta * kk * decay, 0.0)
    pw = -a
    t = jnp.where(eye[None], 1.0, 0.0) + pw
    for _ in range(5):
        pw = _bdot(pw, pw, _B_NN, exact=True)
        t = t + _bdot(t, pw, _B_NN, exact=True)
    egc = jnp.exp(gc)
    rhs = jnp.concatenate([beta * vh, (beta * egc) * kh], axis=-1)
    sol = _bdot(t, rhs, _B_NN, exact=True)
    u_bar, w = sol[..., :HEAD_DIM], sol[..., HEAD_DIM:]
    qk = _bdot(qh, kh, _B_NT) * decay
    q_dec = qh * egc
    k_dec = kh * jnp.exp(gc_last - gc)
    u = u_bar - _bdot(w, S, _B_NN)
    o = _bdot(q_dec, S, _B_NN) + _bdot(qk, u, _B_NN)
    S_new = jnp.exp(gc_last) * S + _bdot(k_dec, u, _B_TN)
    return jnp.concatenate([o[h] for h in range(H)], axis=1), S_new


def _delta_fwd(name, q, k, v, bg, T):
    nc = T // CHUNK
    H = N_HEADS_B

    def body(q_ref, k_ref, v_ref, bg_ref, o_ref, s_out_ref, s_ref):
        @pl.when(pl.program_id(0) == 0)
        def _():
            s_ref[...] = jnp.zeros_like(s_ref)

        S = s_ref[...]
        s_out_ref[0] = S
        o, S_new = _delta_chunk(S, q_ref[...], k_ref[...], v_ref[...], bg_ref[...])
        o_ref[...] = o
        s_ref[...] = S_new

    return pl.pallas_call(
        body, name=name, grid=(nc,),
        in_specs=[_row_spec(CHUNK, D_KEY_B, 0)] * 3 + [_row_spec(CHUNK, 128, 0)],
        out_specs=[_row_spec(CHUNK, D_KEY_B, 0), pl.BlockSpec((1, H, HEAD_DIM, HEAD_DIM), lambda i: (i, 0, 0, 0))],
        out_shape=[jax.ShapeDtypeStruct((T, D_KEY_B), F32), jax.ShapeDtypeStruct((nc, H, HEAD_DIM, HEAD_DIM), F32)],
        scratch_shapes=[pltpu.VMEM((H, HEAD_DIM, HEAD_DIM), F32)],
        compiler_params=_ARB,
    )(q, k, v, bg)


def _delta_bwd(name, q, k, v, bg, s_all, do, T):
    nc = T // CHUNK
    H = N_HEADS_B

    def body(q_ref, k_ref, v_ref, bg_ref, s_in_ref, do_ref, dq_ref, dk_ref, dv_ref, dbg_ref, ds_ref):
        @pl.when(pl.program_id(0) == 0)
        def _():
            ds_ref[...] = jnp.zeros_like(ds_ref)

        _, vjp = jax.vjp(_delta_chunk, s_in_ref[0], q_ref[...], k_ref[...], v_ref[...], bg_ref[...])
        dS, dq, dk, dv, dbg = vjp((do_ref[...], ds_ref[...]))
        dq_ref[...] = dq
        dk_ref[...] = dk
        dv_ref[...] = dv
        dbg_ref[...] = dbg
        ds_ref[...] = dS

    def rev(w):
        return pl.BlockSpec((CHUNK, w), lambda s: (nc - 1 - s, 0))

    return pl.pallas_call(
        body, name=name, grid=(nc,),
        in_specs=[rev(D_KEY_B)] * 3 + [rev(128), pl.BlockSpec((1, H, HEAD_DIM, HEAD_DIM), lambda s: (nc - 1 - s, 0, 0, 0)),
                                       rev(D_KEY_B)],
        out_specs=[rev(D_KEY_B)] * 3 + [rev(128)],
        out_shape=[jax.ShapeDtypeStruct((T, D_KEY_B), F32)] * 3 + [jax.ShapeDtypeStruct((T, 128), F32)],
        scratch_shapes=[pltpu.VMEM((H, HEAD_DIM, HEAD_DIM), F32)],
        compiler_params=_ARB,
    )(q, k, v, bg, s_all, do)


def _loss_fb(name, h2, w, tgt, tm, T):
    def tile_loss(h, wv, t):
        y = _f_rms(h, wv)[0]
        e = y - t
        return 0.5 * jnp.sum(jnp.mean(e * e, axis=-1))

    def body(h_ref, w_ref, t_ref, dh_ref, loss_ref, dw_ref):
        @pl.when(pl.program_id(0) == 0)
        def _():
            loss_ref[...] = jnp.zeros_like(loss_ref)
            dw_ref[...] = jnp.zeros_like(dw_ref)

        val, (dh, dw) = jax.value_and_grad(tile_loss, argnums=(0, 1))(h_ref[...], w_ref[...], t_ref[...])
        dh_ref[...] = dh
        dw_ref[...] += dw
        loss_ref[...] += jnp.reshape(val, (1, 1))

    return pl.pallas_call(
        body, name=name, grid=(T // tm,),
        in_specs=[_row_spec(tm, D_MODEL, 0), _full_spec((1, D_MODEL)), _row_spec(tm, D_MODEL, 0)],
        out_specs=[_row_spec(tm, D_MODEL, 0), _full_spec((1, 1)), _full_spec((1, D_MODEL))],
        out_shape=[jax.ShapeDtypeStruct((T, D_MODEL), F32), jax.ShapeDtypeStruct((1, 1), F32),
                   jax.ShapeDtypeStruct((1, D_MODEL), F32)],
        compiler_params=_ARB,
    )(h2, w, tgt)


def _adamw(name, w, g_a, g_b, m, v, tile):
    R = w.shape[0]

    def body(w_ref, ga_ref, gb_ref, m_ref, v_ref, g_out, d_out, m_out, v_out):
        g = ga_ref[...] + gb_ref[...]
        mm = ADAM_B1 * m_ref[...] + (1.0 - ADAM_B1) * g
        vv = ADAM_B2 * v_ref[...] + (1.0 - ADAM_B2) * jnp.square(g)
        m_hat = mm / (1.0 - ADAM_B1 ** ADAM_STEP)
        v_hat = vv / (1.0 - ADAM_B2 ** ADAM_STEP)
        g_out[...] = g
        d_out[...] = -ADAM_LR * (m_hat / (jnp.sqrt(v_hat) + ADAM_EPS) + ADAM_WD * w_ref[...])
        m_out[...] = mm
        v_out[...] = vv

    spec = pl.BlockSpec((tile, D_MODEL), lambda i: (i, 0))
    return pl.pallas_call(
        body, name=name, grid=(R // tile,), in_specs=[spec] * 5, out_specs=[spec] * 4,
        out_shape=[jax.ShapeDtypeStruct((R, D_MODEL), F32)] * 4, compiler_params=_ARB,
    )(w, g_a, g_b, m, v)


def _sum4(name, r):
    R = r.shape[1]

    def body(r_ref, o_ref):
        o_ref[...] = ((r_ref[0] + r_ref[1]) + r_ref[2]) + r_ref[3]

    return pl.pallas_call(
        body, name=name, grid=(R // PACK_TILE,),
        in_specs=[pl.BlockSpec((4, PACK_TILE, D_MODEL), lambda i: (0, i, 0))],
        out_specs=pl.BlockSpec((PACK_TILE, D_MODEL), lambda i: (i, 0)),
        out_shape=jax.ShapeDtypeStruct((R, D_MODEL), F32), compiler_params=_ARB,
    )(r)


_ANY = pl.BlockSpec(memory_space=pl.ANY)


def _place():
    return lax.axis_index("x"), lax.axis_index("y"), lax.axis_index("c")


def _gather_chips(name, shard):
    def body(s_ref, out_ref, send_sems, recv_sems, local_sem):
        x, y, c = _place()
        me = 2 * x + y
        chips = [(1 - x, y), (x, 1 - y), (1 - x, 1 - y)]
        mine = pltpu.make_async_copy(s_ref, out_ref.at[me], local_sem)
        mine.start()

        def copy(k, slot, px, py):
            return pltpu.make_async_remote_copy(src_ref=s_ref, dst_ref=out_ref.at[slot], send_sem=send_sems.at[k],
                                                recv_sem=recv_sems.at[k], device_id=(px, py, c), device_id_type=MESH)

        sends = [copy(k, me, px, py) for k, (px, py) in enumerate(chips)]
        for cp in sends:
            cp.start()
        for k, (px, py) in enumerate(chips):
            copy(k, 2 * px + py, px, py).wait_recv()
        for cp in sends:
            cp.wait_send()
        mine.wait()

    return pl.pallas_call(
        body, name=name, in_specs=[_ANY], out_specs=_ANY,
        out_shape=jax.ShapeDtypeStruct((4,) + shard.shape, shard.dtype),
        scratch_shapes=[pltpu.SemaphoreType.DMA((3,)), pltpu.SemaphoreType.DMA((3,)), pltpu.SemaphoreType.DMA],
    )(shard)


def _scatter_chips(name, g):
    def body(g_ref, out_ref, send_sems, recv_sems, local_sem):
        x, y, c = _place()
        me = 2 * x + y
        chips = [(1 - x, y), (x, 1 - y), (1 - x, 1 - y)]
        mine = pltpu.make_async_copy(g_ref.at[me], out_ref.at[me], local_sem)
        mine.start()

        def copy(k, src_slot, dst_slot, px, py):
            return pltpu.make_async_remote_copy(src_ref=g_ref.at[src_slot], dst_ref=out_ref.at[dst_slot],
                                                send_sem=send_sems.at[k], recv_sem=recv_sems.at[k],
                                                device_id=(px, py, c), device_id_type=MESH)

        sends = [copy(k, 2 * px + py, me, px, py) for k, (px, py) in enumerate(chips)]
        for cp in sends:
            cp.start()
        for k, (px, py) in enumerate(chips):
            copy(k, me, 2 * px + py, px, py).wait_recv()
        for cp in sends:
            cp.wait_send()
        mine.wait()

    return pl.pallas_call(
        body, name=name, in_specs=[_ANY], out_specs=_ANY,
        out_shape=jax.ShapeDtypeStruct(g.shape, g.dtype),
        scratch_shapes=[pltpu.SemaphoreType.DMA((3,)), pltpu.SemaphoreType.DMA((3,)), pltpu.SemaphoreType.DMA],
    )(g)


def _swap_sibling(name, p):
    def body(p_ref, out_ref, send_sem, recv_sem):
        x, y, c = _place()
        cp = pltpu.make_async_remote_copy(src_ref=p_ref, dst_ref=out_ref, send_sem=send_sem, recv_sem=recv_sem,
                                          device_id=(x, y, 1 - c), device_id_type=MESH)
        cp.start()
        cp.wait()

    return pl.pallas_call(
        body, name=name, in_specs=[_ANY], out_specs=_ANY,
        out_shape=jax.ShapeDtypeStruct(p.shape, p.dtype),
        scratch_shapes=[pltpu.SemaphoreType.DMA, pltpu.SemaphoreType.DMA],
    )(p)


def _allsum_small(name, pack):
    def body(p_ref, out_ref, slots, send_sems, recv_sems):
        x, y, c = _place()
        me = 4 * x + 2 * y + c
        slots[me] = p_ref[...]
        sends = []
        for k in range(1, 8):
            px = 1 - x if k & 4 else x
            py = 1 - y if k & 2 else y
            pc = 1 - c if k & 1 else c
            cp = pltpu.make_async_remote_copy(src_ref=p_ref, dst_ref=slots.at[me], send_sem=send_sems.at[k - 1],
                                              recv_sem=recv_sems.at[k - 1], device_id=(px, py, pc), device_id_type=MESH)
            cp.start()
            sends.append(cp)
        for k in range(1, 8):
            px = 1 - x if k & 4 else x
            py = 1 - y if k & 2 else y
            pc = 1 - c if k & 1 else c
            pltpu.make_async_remote_copy(src_ref=p_ref, dst_ref=slots.at[4 * px + 2 * py + pc],
                                         send_sem=send_sems.at[k - 1], recv_sem=recv_sems.at[k - 1],
                                         device_id=(px, py, pc), device_id_type=MESH).wait_recv()
        for cp in sends:
            cp.wait_send()
        acc = slots[0]
        for s in range(1, 8):
            acc = acc + slots[s]
        out_ref[...] = acc

    vm = pl.BlockSpec(memory_space=pltpu.VMEM)
    return pl.pallas_call(
        body, name=name, in_specs=[vm], out_specs=vm,
        out_shape=jax.ShapeDtypeStruct(pack.shape, pack.dtype),
        scratch_shapes=[pltpu.VMEM((8,) + pack.shape, pack.dtype), pltpu.SemaphoreType.DMA((7,)),
                        pltpu.SemaphoreType.DMA((7,))],
    )(pack)


def _pack_shards(parts, rows_total):
    flat = [p.reshape(-1, D_MODEL) for p in parts]
    used = sum(f.shape[0] for f in flat)
    return jnp.concatenate(flat + [jnp.zeros((rows_total - used, D_MODEL), flat[0].dtype)], axis=0)


def _unpack_shards(buf, shapes):
    out, r0 = [], 0
    for shp in shapes:
        n = math.prod(shp) // D_MODEL
        out.append(buf[r0:r0 + n].reshape(shp))
        r0 += n
    return out


def kernel(x, norm_mix, w_in, conv_w, a_log, dt_bias, dn_norm, w_proj_attn, w_proj_delta, w_out, norm_ffn, w_gate, w_up, w_down, norm_final, loss_target, m_norm_mix, m_w_in, m_conv_w, m_a_log, m_dt_bias, m_dn_norm, m_w_proj_attn, m_w_proj_delta, m_w_out, m_norm_ffn, m_w_gate, m_w_up, m_w_down, m_norm_final, v_norm_mix, v_w_in, v_conv_w, v_a_log, v_dt_bias, v_dn_norm, v_w_proj_attn, v_w_proj_delta, v_w_out, v_norm_ffn, v_w_gate, v_w_up, v_w_down, v_norm_final):
    T = x.shape[1]
    D = D_MODEL
    tm = 256
    mt = 1024 if T % 1024 == 0 else 512
    x2 = x.reshape(T, D)
    tgt = loss_target.reshape(T, D)

    sharded = (w_in, w_proj_attn, w_proj_delta, w_out, w_gate, w_up, w_down)
    conv_bits = lax.bitcast_convert_type(conv_w[0], BF16)
    w16 = _pack_shards([w[0].astype(BF16) for w in sharded] + [conv_bits], PACK_R16)
    wg = _gather_chips("gather_weights", w16)

    def seg(r0, n):
        return wg[:, r0:r0 + n, :]

    r = 0
    Win = seg(r, 2692).reshape(4, D, 2692).transpose(1, 0, 2).reshape(D, D_IN); r += 2692
    Wpa = seg(r, 128).reshape(4, D_ATTN_OUT, 256).transpose(1, 0, 2).reshape(D_ATTN_OUT, D); r += 128
    Wpd = seg(r, 256).reshape(D, D); r += 256
    Wout = seg(r, 256).reshape(D, D); r += 256
    Wg = seg(r, 704).reshape(4, D, 704).transpose(1, 0, 2).reshape(D, D_FF); r += 704
    Wu = seg(r, 704).reshape(4, D, 704).transpose(1, 0, 2).reshape(D, D_FF); r += 704
    Wd = seg(r, 704).reshape(D_FF, D); r += 704
    convw = lax.bitcast_convert_type(seg(r, 6).reshape(4, CONV_WIDTH, 768, 2), F32)
    convw = convw.transpose(1, 0, 2).reshape(CONV_WIDTH, 3 * D_KEY_B)
    Wmain = jnp.concatenate([Win[:, 4608:8704], Win[:, 8720:], Win[:, :4608]], axis=1)
    Wba = jnp.pad(Win[:, 8704:8720], ((0, 0), (0, 112)))
    Wgu = jnp.concatenate([Wg, Wu], axis=1)
    alog_v = jnp.pad(a_log, ((0, 0), (8, 112)))
    dtb_v = jnp.pad(dt_bias, ((0, 0), (8, 112)))

    (u,) = _rowwise("rms_mix", _f_rms, [(x2, 0, D)], [norm_mix], [(D, BF16)], tm, T)
    proj = _matmul("mm_in", u, Wmain, "nn", T, D_MAIN, D, mt, 1536, D, F32)
    ba = _matmul("mm_ba", u, Wba, "nn", T, 128, D, mt, 128, D, F32)

    att = [_attn_fwd(f"attn_fwd{g}", proj, g, T) for g in range(3)]
    (ya_m,) = _rowwise("merge", _f_merge, [(att[g][0], 0, 512) for g in range(3)] + [(att[g][1], 0, 512) for g in range(3)],
                       [], [(512, BF16)], tm, T)
    qn, kn, vd, bg = _dn_pre_fwd("dn_pre_fwd", proj, ba, convw, alog_v, dtb_v, tm, T)
    o_d, s_all = _delta_fwd("delta_fwd", qn, kn, vd, bg, T)
    (od,) = _rowwise("post_dn", _f_postdn, [(o_d, 0, D), (proj, C_Z // D, D)], [dn_norm], [(D, BF16)], tm, T)
    y_a = _matmul("mm_pa", ya_m, Wpa, "nn", T, D, 512, mt, D, 512, F32)
    y_b = _matmul("mm_pd", od, Wpd, "nn", T, D, D, mt, D, D, F32)
    (mix,) = _rowwise("mix", _f_mix, [(y_a, 0, D), (y_b, 0, D), (proj, C_GATE // 2048, 2048)], [], [(D, BF16)], tm, T)
    h1 = _matmul("mm_out", mix, Wout, "nn", T, D, D, mt, D, D, F32, residual=x2)
    (hn,) = _rowwise("rms_ffn", _f_rms, [(h1, 0, D)], [norm_ffn], [(D, BF16)], tm, T)
    gu = _matmul("mm_gu", hn, Wgu, "nn", T, 2 * D_FF, D, mt, 1408, D, F32)
    (act,) = _rowwise("act", _f_act, [(gu, 0, 2 * D_FF)], [], [(D_FF, BF16)], tm, T)
    h2 = _matmul("mm_down", act, Wd, "nn", T, D, D_FF, mt, D, 1408, F32, residual=h1)
    dh2, loss, g_norm_final = _loss_fb("loss", h2, norm_final.reshape(1, D), tgt, tm, T)

    dact = _matmul("mm_dact", dh2, Wd, "nt", T, D_FF, D, mt, 1408, D, F32)
    gWd = _matmul("mm_gwd", act, dh2, "tn", D_FF, D, T, 1408, D, mt, F32)
    (dgu,) = _rowwise_vjp("act_bwd", _f_act, [(gu, 0, 2 * D_FF)], [], [(dact, 0, D_FF)], [BF16], tm, T)
    dhn = _matmul("mm_dhn", dgu, Wgu, "nt", T, D, 2 * D_FF, mt, D, 1408, F32)
    gWgu = _matmul("mm_gwgu", hn, dgu, "tn", D, 2 * D_FF, T, D, 1408, mt, F32)
    dh1, g_norm_ffn = _rowwise_vjp("rms_ffn_bwd", _f_rms, [(h1, 0, D)], [norm_ffn], [(dhn, 0, D)], [F32], tm, T,
                                   residual=(dh2, 0, D))
    dmix = _matmul("mm_dmix", dh1, Wout, "nt", T, D, D, mt, D, D, F32)
    gWout = _matmul("mm_gwout", mix, dh1, "tn", D, D, T, D, D, mt, F32)
    dya, dyb, dgate = _rowwise_vjp("mix_bwd", _f_mix, [(y_a, 0, D), (y_b, 0, D), (proj, C_GATE // 2048, 2048)], [],
                                   [(dmix, 0, D)], [BF16, BF16, BF16], tm, T)
    dya_m = _matmul("mm_dyam", dya, Wpa, "nt", T, 512, D, mt, 512, D, F32)
    gWpa = _matmul("mm_gwpa", ya_m, dya, "tn", 512, D, T, 512, D, mt, F32)
    dod = _matmul("mm_dod", dyb, Wpd, "nt", T, D, D, mt, D, D, F32)
    gWpd = _matmul("mm_gwpd", od, dyb, "tn", D, D, T, D, D, mt, F32)
    do_d, dz, g_dn_norm = _rowwise_vjp("post_dn_bwd", _f_postdn, [(o_d, 0, D), (proj, C_Z // D, D)], [dn_norm],
                                       [(dod, 0, D)], [F32, BF16], tm, T)
    dqn, dkn, dvd, dbg = _delta_bwd("delta_bwd", qn, kn, vd, bg, s_all, do_d, T)
    dqkvd, dba, g_conv8, g_alog_v, g_dtb_v = _dn_pre_bwd("dn_pre_bwd", proj, ba, convw, alog_v, dtb_v,
                                                         dqn, dkn, dvd, dbg, tm, T)
    mg = _rowwise_vjp("merge_bwd", _f_merge,
                      [(att[g][0], 0, 512) for g in range(3)] + [(att[g][1], 0, 512) for g in range(3)], [],
                      [(dya_m, 0, 512)], [F32] * 6, tm, T)
    datt = [_attn_bwd(f"attn_bwd{g}", proj, mg[g], mg[3 + g], g, T) for g in range(3)]
    dproj = jnp.concatenate([dqkvd.astype(BF16), dz, dgate]
                            + [datt[g][0].astype(BF16) for g in range(3)]
                            + [datt[g][1].astype(BF16) for g in range(3)]
                            + [datt[g][2].astype(BF16) for g in range(3)], axis=1)
    gWmain = _matmul("mm_gwmain", u, dproj, "tn", D, D_MAIN, T, D, 1536, mt, F32)
    gWba = _matmul("mm_gwba", u, dba, "tn", D, 128, T, D, 128, mt, F32)
    du0 = _matmul("mm_du_ba", dba, Wba, "nt", T, D, 128, mt, D, 128, F32)
    du = _matmul("mm_du", dproj, Wmain, "nt", T, D, D_MAIN, mt, D, 1536, F32, residual=du0)
    dx, g_norm_mix = _rowwise_vjp("rms_mix_bwd", _f_rms, [(x2, 0, D)], [norm_mix], [(du, 0, D)], [F32], tm, T,
                                  residual=(dh1, 0, D))

    gWin = jnp.concatenate([gWmain[:, C_QA:], gWmain[:, :C_GATE], gWba[:, :16], gWmain[:, C_GATE:C_QA]], axis=1)

    def cols4(g, n):
        return g.reshape(g.shape[0], 4, n).transpose(1, 0, 2).reshape(4, -1, D)

    parts = [cols4(gWin, 2692), cols4(gWpa, 256), gWpd.reshape(4, 256, D), gWout.reshape(4, 256, D),
             cols4(gWgu[:, :D_FF], 704), cols4(gWgu[:, D_FF:], 704), gWd.reshape(4, 704, D), cols4(g_conv8[:CONV_WIDTH], 768)]
    used = sum(p.shape[1] for p in parts)
    g_pack = jnp.concatenate(parts + [jnp.zeros((4, PACK_R - used, D), F32)], axis=1)
    recv = _scatter_chips("scatter_grads", g_pack)
    part = _sum4("sum_chips", recv)
    other = _swap_sibling("swap_cores", part)

    locals_ = (w_in, w_proj_attn, w_proj_delta, w_out, w_gate, w_up, w_down, conv_w)
    ms = (m_w_in, m_w_proj_attn, m_w_proj_delta, m_w_out, m_w_gate, m_w_up, m_w_down, m_conv_w)
    vs = (v_w_in, v_w_proj_attn, v_w_proj_delta, v_w_out, v_w_gate, v_w_up, v_w_down, v_conv_w)
    big = _adamw("adamw", _pack_shards(locals_, PACK_R), part, other, _pack_shards(ms, PACK_R),
                 _pack_shards(vs, PACK_R), PACK_TILE)
    big = [_unpack_shards(b, [w.shape for w in locals_]) for b in big]

    def small_pack(nm, nf, nfin, dn, al, dt, extra):
        row3 = jnp.concatenate([dn.reshape(1, 128), al.reshape(1, 8), dt.reshape(1, 8), extra.reshape(1, 1),
                                jnp.zeros((1, D - 145), F32)], axis=1)
        return jnp.concatenate([nm.reshape(1, D), nf.reshape(1, D), nfin.reshape(1, D), row3, jnp.zeros((4, D), F32)], axis=0)

    zero1 = jnp.zeros((1, 1), F32)
    sp = small_pack(g_norm_mix, g_norm_ffn, g_norm_final, g_dn_norm, g_alog_v[:, 8:16], g_dtb_v[:, 8:16], loss)
    tot = _allsum_small("allsum_small", sp)
    sm = _adamw("adamw_small", small_pack(norm_mix, norm_ffn, norm_final, dn_norm, a_log, dt_bias, zero1), tot,
                jnp.zeros_like(tot), small_pack(m_norm_mix, m_norm_ffn, m_norm_final, m_dn_norm, m_a_log, m_dt_bias, zero1),
                small_pack(v_norm_mix, v_norm_ffn, v_norm_final, v_dn_norm, v_a_log, v_dt_bias, zero1), 8)

    def small_unpack(p):
        return {"norm_mix": p[0:1], "norm_ffn": p[1:2], "norm_final": p[2], "dn_norm": p[3:4, :128],
                "a_log": p[3:4, 128:136], "dt_bias": p[3:4, 136:144]}

    sm = [small_unpack(p) for p in sm]
    loss_out = tot[3, 144]

    order = ("norm_mix", "w_in", "conv_w", "a_log", "dt_bias", "dn_norm", "w_proj_attn", "w_proj_delta", "w_out",
             "norm_ffn", "w_gate", "w_up", "w_down", "norm_final")
    big_names = ("w_in", "w_proj_attn", "w_proj_delta", "w_out", "w_gate", "w_up", "w_down", "conv_w")
    outs = [loss_out, dx.reshape(x.shape)]
    for kind in range(4):
        for n in order:
            outs.append(big[kind][big_names.index(n)] if n in big_names else sm[kind][n])
    return tuple(outs)
```

```python
import functools
import math

import jax
import jax.numpy as jnp
from jax import lax
from jax.experimental import pallas as pl
from jax.experimental.pallas import tpu as pltpu

F32 = jnp.float32
BF16 = jnp.bfloat16
MESH = pl.DeviceIdType.MESH

D_MODEL = 1024
N_HEADS_A = 12
HEAD_DIM = 128
BLOCK_A = 128
DILATIONS = (1, 4, 16)
W_SUB = 128
D_ATTN_OUT = 512
N_HEADS_B = 8
D_KEY_B = 1024
CONV_WIDTH = 4
CHUNK = 64
D_FF = 2816
EPS = 1e-6
D_IN = 10768
D_MAIN = 10752
C_QKVD, C_Z, C_GATE, C_QA, C_KA, C_VA = 0, 3072, 4096, 6144, 7680, 9216

ADAM_LR, ADAM_B1, ADAM_B2, ADAM_EPS, ADAM_WD, ADAM_STEP = 0.001, 0.9, 0.999, 1e-08, 0.01, 10

PACK_ROWS = (("w_in", 2692), ("w_proj_attn", 128), ("w_proj_delta", 256), ("w_out", 256),
             ("w_gate", 704), ("w_up", 704), ("w_down", 704), ("conv_w", 3))
PACK_TILE = 512
PACK_R = 5632
PACK_R16 = 5632
HALF = PACK_R // 2
HALF_TILE = 256

VMEM_LIMIT = 48 * 1024 * 1024


def _cparams(n):
    return pltpu.CompilerParams(dimension_semantics=("arbitrary",) * n, vmem_limit_bytes=VMEM_LIMIT)


_ARB = _cparams(1)


def _row_spec(tm, w, cb):
    return pl.BlockSpec((tm, w), lambda i: (i, cb))


def _full_spec(shape):
    nd = len(shape)
    return pl.BlockSpec(shape, lambda i: (0,) * nd)


def _rowwise(name, fn, ins, params, outs, tm, T):
    n_in, n_p = len(ins), len(params)

    def body(*refs):
        xs = [r[...] for r in refs[:n_in + n_p]]
        res = fn(*xs)
        for o_ref, v in zip(refs[n_in + n_p:], res):
            o_ref[...] = v.astype(o_ref.dtype)

    return pl.pallas_call(
        body, name=name, grid=(T // tm,),
        in_specs=[_row_spec(tm, w, cb) for (_, cb, w) in ins] + [_full_spec(p.shape) for p in params],
        out_specs=[_row_spec(tm, w, 0) for (w, _) in outs],
        out_shape=[jax.ShapeDtypeStruct((T, w), dt) for (w, dt) in outs],
        compiler_params=_ARB,
    )(*[a for (a, _, _) in ins], *params)


def _rowwise_vjp(name, fn, ins, params, cts, gdtypes, tm, T, residual=None):
    n_in, n_p, n_ct = len(ins), len(params), len(cts)
    n_res = 0 if residual is None else 1

    def body(*refs):
        i = pl.program_id(0)
        k = 0
        xs = [r[...].astype(F32) for r in refs[k:k + n_in]]
        k += n_in
        ps = [r[...] for r in refs[k:k + n_p]]
        k += n_p
        gs = tuple(r[...].astype(F32) for r in refs[k:k + n_ct])
        k += n_ct
        res_refs = refs[k:k + n_res]
        k += n_res
        gi_refs = refs[k:k + n_in]
        gp_refs = refs[k + n_in:]
        _, vjp = jax.vjp(fn, *xs, *ps)
        g = vjp(gs)
        for j, (ref, v) in enumerate(zip(gi_refs, g[:n_in])):
            if j == 0 and n_res:
                v = v + res_refs[0][...].astype(F32)
            ref[...] = v.astype(ref.dtype)

        @pl.when(i == 0)
        def _():
            for ref in gp_refs:
                ref[...] = jnp.zeros_like(ref)

        for ref, v in zip(gp_refs, g[n_in:]):
            ref[...] += v

    res_in = [] if residual is None else [residual]
    return pl.pallas_call(
        body, name=name, grid=(T // tm,),
        in_specs=([_row_spec(tm, w, cb) for (_, cb, w) in ins] + [_full_spec(p.shape) for p in params]
                  + [_row_spec(tm, w, cb) for (_, cb, w) in cts] + [_row_spec(tm, w, cb) for (_, cb, w) in res_in]),
        out_specs=[_row_spec(tm, w, 0) for (_, _, w) in ins] + [_full_spec(p.shape) for p in params],
        out_shape=([jax.ShapeDtypeStruct((T, w), dt) for (_, _, w), dt in zip(ins, gdtypes)]
                   + [jax.ShapeDtypeStruct(p.shape, F32) for p in params]),
        compiler_params=_ARB,
    )(*[a for (a, _, _) in ins], *params, *[a for (a, _, _) in cts], *[a for (a, _, _) in res_in])


def _matmul(name, a, b, mode, M, N, K, tm, tn, tk, out_dtype, a_off=0, b_off=0, residual=None):
    nk = K // tk
    assert M % tm == 0 and N % tn == 0 and K % tk == 0
    if mode == "nn":
        assert a_off % tk == 0 and b_off % tn == 0
        a_spec = pl.BlockSpec((tm, tk), lambda i, j, k: (i, a_off // tk + k))
        b_spec = pl.BlockSpec((tk, tn), lambda i, j, k: (k, b_off // tn + j))
        dims = (((1,), (0,)), ((), ()))
    elif mode == "nt":
        assert a_off % tk == 0 and b_off == 0
        a_spec = pl.BlockSpec((tm, tk), lambda i, j, k: (i, a_off // tk + k))
        b_spec = pl.BlockSpec((tn, tk), lambda i, j, k: (j, k))
        dims = (((1,), (1,)), ((), ()))
    else:
        assert a_off % tm == 0 and b_off % tn == 0
        a_spec = pl.BlockSpec((tk, tm), lambda i, j, k: (k, a_off // tm + i))
        b_spec = pl.BlockSpec((tk, tn), lambda i, j, k: (k, b_off // tn + j))
        dims = (((0,), (0,)), ((), ()))
    has_res = residual is not None

    def body(*refs):
        a_ref, b_ref = refs[0], refs[1]
        r_ref = refs[2] if has_res else None
        o_ref, acc_ref = refs[-2], refs[-1]
        k = pl.program_id(2)

        @pl.when(k == 0)
        def _():
            acc_ref[...] = jnp.zeros_like(acc_ref)

        acc_ref[...] += lax.dot_general(a_ref[...].astype(BF16), b_ref[...].astype(BF16), dims,
                                        preferred_element_type=F32)

        @pl.when(k == nk - 1)
        def _():
            acc = acc_ref[...]
            if has_res:
                acc = acc + r_ref[...].astype(F32)
            o_ref[...] = acc.astype(o_ref.dtype)

    in_specs = [a_spec, b_spec]
    args = [a, b]
    if has_res:
        in_specs.append(pl.BlockSpec((tm, tn), lambda i, j, k: (i, j)))
        args.append(residual)
    return pl.pallas_call(
        body, name=name, grid=(M // tm, N // tn, nk),
        in_specs=in_specs,
        out_specs=pl.BlockSpec((tm, tn), lambda i, j, k: (i, j)),
        out_shape=jax.ShapeDtypeStruct((M, N), out_dtype),
        scratch_shapes=[pltpu.VMEM((tm, tn), F32)],
        compiler_params=pltpu.CompilerParams(dimension_semantics=("parallel", "parallel", "arbitrary"),
                                             vmem_limit_bytes=VMEM_LIMIT),
    )(*args)


def _f_rms(x, w):
    return (x * lax.rsqrt(jnp.mean(x * x, axis=-1, keepdims=True) + EPS) * w,)


def _silu(x):
    return x * jax.nn.sigmoid(x)


def _softplus(x):
    return jnp.maximum(x, 0.0) + jnp.log1p(jnp.exp(-jnp.abs(x)))


def _heads(x, n):
    return [x[:, HEAD_DIM * h:HEAD_DIM * (h + 1)] for h in range(n)]


def _f_dnpoint(c, ba, alog_v, dtb_v):
    cs = _silu(c)
    q, k, v = cs[:, :D_KEY_B], cs[:, D_KEY_B:2 * D_KEY_B], cs[:, 2 * D_KEY_B:]
    qn = jnp.concatenate([t * lax.rsqrt(jnp.sum(t * t, axis=-1, keepdims=True) + EPS) * (HEAD_DIM ** -0.5)
                          for t in _heads(q, N_HEADS_B)], axis=1)
    kn = jnp.concatenate([t * lax.rsqrt(jnp.sum(t * t, axis=-1, keepdims=True) + EPS)
                          for t in _heads(k, N_HEADS_B)], axis=1)
    lane = lax.broadcasted_iota(jnp.int32, ba.shape, 1)
    beta = jax.nn.sigmoid(ba)
    g = -jnp.exp(alog_v) * _softplus(ba + dtb_v)
    bg = jnp.where(lane < N_HEADS_B, beta, jnp.where(lane < 2 * N_HEADS_B, g, 0.0))
    return qn, kn, v, bg


def _f_postdn(o, z, w):
    outs = []
    for oh, zh in zip(_heads(o, N_HEADS_B), _heads(z, N_HEADS_B)):
        outs.append(oh * lax.rsqrt(jnp.mean(oh * oh, axis=-1, keepdims=True) + EPS) * w * _silu(zh))
    return (jnp.concatenate(outs, axis=1),)


def _f_merge(o0, o1, o2, l0, l1, l2):
    m = lax.stop_gradient(jnp.maximum(jnp.maximum(l0, l1), l2))
    e0, e1, e2 = jnp.exp(l0 - m), jnp.exp(l1 - m), jnp.exp(l2 - m)
    return ((e0 * o0 + e1 * o1 + e2 * o2) / (e0 + e1 + e2),)


def _f_mix(ya, yb, gr):
    return (jax.nn.sigmoid(gr[:, :D_MODEL]) * ya + jax.nn.sigmoid(gr[:, D_MODEL:]) * yb,)


def _f_act(gu):
    return (_silu(gu[:, :D_FF]) * gu[:, D_FF:],)


def _attn_block(q, kp, kc, vp, vc, slope_d, first_key):
    k2 = jnp.concatenate([kp, kc], axis=0).astype(BF16)
    v2 = jnp.concatenate([vp, vc], axis=0).astype(BF16)
    s = lax.dot_general(q.astype(BF16), k2, (((1,), (1,)), ((), ())), preferred_element_type=F32)
    s = s * (HEAD_DIM ** -0.5)
    i = lax.broadcasted_iota(jnp.int32, (BLOCK_A, 2 * BLOCK_A), 0)
    j = lax.broadcasted_iota(jnp.int32, (BLOCK_A, 2 * BLOCK_A), 1)
    delta = BLOCK_A + i - j
    valid = (delta >= 0) & (delta <= W_SUB) & (j >= first_key)
    s = jnp.where(valid, s - slope_d * delta.astype(F32), -jnp.inf)
    m = lax.stop_gradient(jnp.max(s, axis=-1, keepdims=True))
    p = jnp.exp(s - m)
    den = jnp.sum(p, axis=-1, keepdims=True)
    o = jnp.dot(p.astype(BF16), v2, preferred_element_type=F32) / den
    lse = jnp.broadcast_to(m + jnp.log(den), (BLOCK_A, HEAD_DIM))
    return o, lse


def _slope_d(group, h, d):
    hg = (group * 4 + h + 1).astype(F32)
    return jnp.exp(jnp.full((1, 1), -8.0 * math.log(2.0) / N_HEADS_A, F32) * hg) * float(d)


def _attn_fwd(name, proj, group, T):
    d = DILATIONS[group]
    span = BLOCK_A * d
    nb = T // span
    qb, kb, vb = C_QA // 128 + 4 * group, C_KA // 128 + 4 * group, C_VA // 128 + 4 * group

    def body(q_ref, kc_ref, kp_ref, vc_ref, vp_ref, o_ref, l_ref):
        h, n = pl.program_id(0), pl.program_id(1)
        sl = _slope_d(group, h, d)
        first_key = jnp.where(n > 0, 0, BLOCK_A)

        def one(r, carry):
            idx = pl.ds(r, BLOCK_A, stride=d) if d > 1 else pl.ds(0, BLOCK_A)
            o, lse = _attn_block(q_ref[idx, :], kp_ref[idx, :], kc_ref[idx, :], vp_ref[idx, :], vc_ref[idx, :],
                                 sl, first_key)
            o_ref[idx, :] = o
            l_ref[idx, :] = lse
            return carry

        lax.fori_loop(0, d, one, 0)

    def cur(cb):
        return pl.BlockSpec((span, 128), lambda h, n: (n, cb + h))

    def prev(cb):
        return pl.BlockSpec((span, 128), lambda h, n: (jnp.maximum(n - 1, 0), cb + h))

    return pl.pallas_call(
        body, name=name, grid=(4, nb),
        in_specs=[cur(qb), cur(kb), prev(kb), cur(vb), prev(vb)],
        out_specs=[pl.BlockSpec((span, 128), lambda h, n: (n, h))] * 2,
        out_shape=[jax.ShapeDtypeStruct((T, D_ATTN_OUT), F32)] * 2,
        compiler_params=_cparams(2),
    )(proj, proj, proj, proj, proj)


def _attn_bwd(name, proj, do, dl, group, T):
    d = DILATIONS[group]
    span = BLOCK_A * d
    nb = T // span
    qb, kb, vb = C_QA // 128 + 4 * group, C_KA // 128 + 4 * group, C_VA // 128 + 4 * group

    def body(q_ref, kc_ref, kp_ref, vc_ref, vp_ref, do_ref, dl_ref, dq_ref, dk_ref, dv_ref, ck_ref, cv_ref):
        h, n = pl.program_id(0), pl.program_id(1)
        sl = _slope_d(group, h, d)
        first_key = jnp.where(n > 0, 0, BLOCK_A)

        @pl.when(n == 0)
        def _():
            ck_ref[...] = jnp.zeros_like(ck_ref)
            cv_ref[...] = jnp.zeros_like(cv_ref)

        def idx_of(r):
            return pl.ds(r, BLOCK_A, stride=d) if d > 1 else pl.ds(0, BLOCK_A)

        @pl.when(n < nb)
        def _():
            def one(r, carry):
                idx = idx_of(r)
                rows = pl.ds(pl.multiple_of(r * BLOCK_A, BLOCK_A), BLOCK_A)
                f = functools.partial(_attn_block, slope_d=sl, first_key=first_key)
                _, vjp = jax.vjp(f, q_ref[idx, :], kp_ref[idx, :], kc_ref[idx, :], vp_ref[idx, :], vc_ref[idx, :])
                dq, dkp, dkc, dvp, dvc = vjp((do_ref[idx, :], dl_ref[idx, :]))
                dq_ref[idx, :] = dq
                dk_ref[idx, :] = ck_ref[rows, :] + dkp
                dv_ref[idx, :] = cv_ref[rows, :] + dvp
                ck_ref[rows, :] = dkc
                cv_ref[rows, :] = dvc
                return carry

            lax.fori_loop(0, d, one, 0)

        @pl.when(n == nb)
        def _():
            def one(r, carry):
                idx = idx_of(r)
                rows = pl.ds(pl.multiple_of(r * BLOCK_A, BLOCK_A), BLOCK_A)
                dk_ref[idx, :] = ck_ref[rows, :]
                dv_ref[idx, :] = cv_ref[rows, :]
                return carry

            lax.fori_loop(0, d, one, 0)

    def cur(cb):
        return pl.BlockSpec((span, 128), lambda h, n: (jnp.minimum(n, nb - 1), cb + h))

    def prev(cb):
        return pl.BlockSpec((span, 128), lambda h, n: (jnp.maximum(jnp.minimum(n, nb - 1) - 1, 0), cb + h))

    own = pl.BlockSpec((span, 128), lambda h, n: (jnp.minimum(n, nb - 1), h))
    late = pl.BlockSpec((span, 128), lambda h, n: (jnp.maximum(n - 1, 0), h))
    return pl.pallas_call(
        body, name=name, grid=(4, nb + 1),
        in_specs=[cur(qb), cur(kb), prev(kb), cur(vb), prev(vb), own, own],
        out_specs=[own, late, late],
        out_shape=[jax.ShapeDtypeStruct((T, D_ATTN_OUT), F32)] * 3,
        scratch_shapes=[pltpu.VMEM((span, 128), F32), pltpu.VMEM((span, 128), F32)],
        compiler_params=_cparams(2),
    )(proj, proj, proj, proj, proj, do, dl)


HALO = 8


def _conv_taps(ext, w):
    c = ext[HALO:, :] * w[CONV_WIDTH - 1]
    for k in range(1, CONV_WIDTH):
        c = c + pltpu.roll(ext, k, 0)[HALO:, :] * w[CONV_WIDTH - 1 - k]
    return c


def _taps(w_ref):
    return [w_ref[j:j + 1, :] for j in range(CONV_WIDTH)]


def _dn_pre_fwd(name, proj, ba, conv_w, alog_v, dtb_v, tm, T):
    C = 3 * D_KEY_B

    def body(x_ref, ba_ref, w_ref, al_ref, dt_ref, q_ref, k_ref, v_ref, bg_ref, halo_ref):
        i = pl.program_id(0)

        @pl.when(i == 0)
        def _():
            halo_ref[...] = jnp.zeros_like(halo_ref)

        x = x_ref[...]
        ext = jnp.concatenate([halo_ref[...], x], axis=0)
        c = _conv_taps(ext, _taps(w_ref))
        halo_ref[...] = x[tm - HALO:, :]
        qn, kn, v, bg = _f_dnpoint(c, ba_ref[...], al_ref[...], dt_ref[...])
        q_ref[...] = qn
        k_ref[...] = kn
        v_ref[...] = v
        bg_ref[...] = bg

    return pl.pallas_call(
        body, name=name, grid=(T // tm,),
        in_specs=[_row_spec(tm, C, 0), _row_spec(tm, 128, 0), _full_spec(conv_w.shape), _full_spec((1, 128)),
                  _full_spec((1, 128))],
        out_specs=[_row_spec(tm, D_KEY_B, 0)] * 3 + [_row_spec(tm, 128, 0)],
        out_shape=[jax.ShapeDtypeStruct((T, D_KEY_B), F32)] * 3 + [jax.ShapeDtypeStruct((T, 128), F32)],
        scratch_shapes=[pltpu.VMEM((HALO, C), F32)],
        compiler_params=_ARB,
    )(proj, ba, conv_w, alog_v, dtb_v)


def _dn_pre_bwd(name, proj, ba, conv_w, alog_v, dtb_v, dq, dk, dv, dbg, tm, T):
    C = 3 * D_KEY_B
    nt = T // tm
    hb = tm // HALO

    def body(x_ref, xh_ref, ba_ref, w_ref, al_ref, dt_ref, dq_ref, dk_ref, dv_ref, dbg_ref,
             dx_ref, dba_ref, dw_ref, dal_ref, ddt_ref, nxt_ref):
        s = pl.program_id(0)
        i = nt - 1 - s

        @pl.when(s == 0)
        def _():
            nxt_ref[...] = jnp.zeros_like(nxt_ref)
            dw_ref[...] = jnp.zeros_like(dw_ref)
            dal_ref[...] = jnp.zeros_like(dal_ref)
            ddt_ref[...] = jnp.zeros_like(ddt_ref)

        w = _taps(w_ref)
        halo = jnp.where(i > 0, xh_ref[...], 0.0)
        ext = jnp.concatenate([halo, x_ref[...]], axis=0)
        c = _conv_taps(ext, w)
        _, vjp = jax.vjp(_f_dnpoint, c, ba_ref[...], al_ref[...], dt_ref[...])
        dc, dba, dal, ddt = vjp((dq_ref[...], dk_ref[...], dv_ref[...], dbg_ref[...]))
        dba_ref[...] = dba
        dal_ref[...] += dal
        ddt_ref[...] += ddt
        ext2 = jnp.concatenate([dc, nxt_ref[...]], axis=0)
        dx = dc * w[CONV_WIDTH - 1]
        for m in range(1, CONV_WIDTH):
            dx = dx + pltpu.roll(ext2, tm + HALO - m, 0)[:tm, :] * w[CONV_WIDTH - 1 - m]
        dx_ref[...] = dx
        nxt_ref[...] = dc[:HALO, :]
        for j in range(CONV_WIDTH):
            k = CONV_WIDTH - 1 - j
            xs = ext[HALO:, :] if k == 0 else pltpu.roll(ext, k, 0)[HALO:, :]
            dw_ref[j:j + 1, :] += jnp.sum(xs * dc, axis=0, keepdims=True)

    def rev(w):
        return pl.BlockSpec((tm, w), lambda s: (nt - 1 - s, 0))

    return pl.pallas_call(
        body, name=name, grid=(nt,),
        in_specs=[rev(C), pl.BlockSpec((HALO, C), lambda s: (jnp.maximum((nt - 1 - s) * hb - 1, 0), 0)), rev(128),
                  _full_spec(conv_w.shape), _full_spec((1, 128)), _full_spec((1, 128)),
                  rev(D_KEY_B), rev(D_KEY_B), rev(D_KEY_B), rev(128)],
        out_specs=[rev(C), rev(128), _full_spec((8, C)), _full_spec((1, 128)), _full_spec((1, 128))],
        out_shape=[jax.ShapeDtypeStruct((T, C), F32), jax.ShapeDtypeStruct((T, 128), F32),
                   jax.ShapeDtypeStruct((8, C), F32), jax.ShapeDtypeStruct((1, 128), F32),
                   jax.ShapeDtypeStruct((1, 128), F32)],
        scratch_shapes=[pltpu.VMEM((HALO, C), F32)],
        compiler_params=_ARB,
    )(proj, proj, ba, conv_w, alog_v, dtb_v, dq, dk, dv, dbg)


def _bdot(a, b, dims, exact=False):
    if exact:
        return lax.dot_general(a, b, dims, preferred_element_type=F32, precision=lax.Precision.HIGH)
    return lax.dot_general(a.astype(BF16), b.astype(BF16), dims, preferred_element_type=F32)


_B_NN = (((2,), (1,)), ((0,), (0,)))
_B_NT = (((2,), (2,)), ((0,), (0,)))
_B_TN = (((1,), (1,)), ((0,), (0,)))


def _unit_lower_inverse(a):
    C = a.shape[-1]
    r = lax.broadcasted_iota(jnp.int32, (C, C), 0)
    c = lax.broadcasted_iota(jnp.int32, (C, C), 1)
    pw = -a
    t = jnp.where((r == c)[None], 1.0, 0.0) + pw
    for _ in range(int(math.log2(C)) - 1):
        pw = _bdot(pw, pw, _B_NN, exact=True)
        t = t + _bdot(t, pw, _B_NN, exact=True)
    return t


@jax.custom_vjp
def _solve_given_inverse(t, a, rhs):
    return _bdot(t, rhs, _B_NN, exact=True)


def _solve_fwd(t, a, rhs):
    sol = _bdot(t, rhs, _B_NN, exact=True)
    return sol, (t, sol)


def _solve_bwd(res, dsol):
    t, sol = res
    drhs = _bdot(t, dsol, _B_TN, exact=True)
    return jnp.zeros_like(t), -_bdot(drhs, sol, _B_NT, exact=True), drhs


_solve_given_inverse.defvjp(_solve_fwd, _solve_bwd)


def _stack(xs):
    return jnp.concatenate([x[None] for x in xs], axis=0)


def _delta_chunk(S, q, k, v, bg, t_inv=None):
    H, C = N_HEADS_B, CHUNK
    r = lax.broadcasted_iota(jnp.int32, (C, C), 0)
    c = lax.broadcasted_iota(jnp.int32, (C, C), 1)
    incl, strict, eye = r >= c, r > c, r == c
    gc_all = jnp.dot(incl.astype(F32), bg, preferred_element_type=F32, precision=lax.Precision.HIGHEST)
    lane = lax.broadcasted_iota(jnp.int32, (C, 128), 1)

    def col(x, l):
        return jnp.sum(jnp.where(lane == l, x, 0.0), axis=1, keepdims=True)

    beta = _stack([col(bg, h) for h in range(H)])
    gc = _stack([col(gc_all, H + h) for h in range(H)])
    gc_row = jnp.sum(jnp.where(eye[None], jnp.broadcast_to(gc, (H, C, C)), 0.0), axis=1, keepdims=True)
    row = lax.broadcasted_iota(jnp.int32, (H, C, 1), 1)
    gc_last = jnp.sum(jnp.where(row == C - 1, gc, 0.0), axis=1, keepdims=True)
    decay = jnp.exp(jnp.where(incl[None], gc - gc_row, -jnp.inf))
    qh = _stack(_heads(q, H))
    kh = _stack(_heads(k, H))
    vh = _stack(_heads(v, H))
    kk = _bdot(kh, kh, _B_NT, exact=True)
    a = jnp.where(strict[None], beta * kk * decay, 0.0)
    egc = jnp.exp(gc)
    rhs = jnp.concatenate([beta * vh, (beta * egc) * kh], axis=-1)
    if t_inv is None:
        t_inv = _unit_lower_inverse(a)
        sol = _bdot(t_inv, rhs, _B_NN, exact=True)
    else:
        sol = _solve_given_inverse(t_inv, a, rhs)
    u_bar, w = sol[..., :HEAD_DIM], sol[..., HEAD_DIM:]
    qk = _bdot(qh, kh, _B_NT) * decay
    q_dec = qh * egc
    k_dec = kh * jnp.exp(gc_last - gc)
    u = u_bar - _bdot(w, S, _B_NN)
    o = _bdot(q_dec, S, _B_NN) + _bdot(qk, u, _B_NN)
    S_new = jnp.exp(gc_last) * S + _bdot(k_dec, u, _B_TN)
    o2 = jnp.concatenate([lax.index_in_dim(o, h, 0, keepdims=False) for h in range(H)], axis=1)
    return o2, S_new, t_inv


def _delta_fwd(name, q, k, v, bg, T):
    nc = T // CHUNK
    H = N_HEADS_B

    def body(q_ref, k_ref, v_ref, bg_ref, o_ref, s_out_ref, t_out_ref, s_ref):
        @pl.when(pl.program_id(0) == 0)
        def _():
            s_ref[...] = jnp.zeros_like(s_ref)

        S = s_ref[...]
        s_out_ref[0] = S
        o, S_new, t_inv = _delta_chunk(S, q_ref[...], k_ref[...], v_ref[...], bg_ref[...])
        o_ref[...] = o
        t_out_ref[0] = t_inv
        s_ref[...] = S_new

    return pl.pallas_call(
        body, name=name, grid=(nc,),
        in_specs=[_row_spec(CHUNK, D_KEY_B, 0)] * 3 + [_row_spec(CHUNK, 128, 0)],
        out_specs=[_row_spec(CHUNK, D_KEY_B, 0), pl.BlockSpec((1, H, HEAD_DIM, HEAD_DIM), lambda i: (i, 0, 0, 0)),
                   pl.BlockSpec((1, H, CHUNK, CHUNK), lambda i: (i, 0, 0, 0))],
        out_shape=[jax.ShapeDtypeStruct((T, D_KEY_B), F32), jax.ShapeDtypeStruct((nc, H, HEAD_DIM, HEAD_DIM), F32),
                   jax.ShapeDtypeStruct((nc, H, CHUNK, CHUNK), F32)],
        scratch_shapes=[pltpu.VMEM((H, HEAD_DIM, HEAD_DIM), F32)],
        compiler_params=_ARB,
    )(q, k, v, bg)


def _delta_bwd(name, q, k, v, bg, s_all, t_all, do, T):
    nc = T // CHUNK
    H = N_HEADS_B

    def body(q_ref, k_ref, v_ref, bg_ref, s_in_ref, t_ref, do_ref, dq_ref, dk_ref, dv_ref, dbg_ref, ds_ref):
        @pl.when(pl.program_id(0) == 0)
        def _():
            ds_ref[...] = jnp.zeros_like(ds_ref)

        t_inv = t_ref[0]

        def f(S, qq, kk, vv, bb):
            return _delta_chunk(S, qq, kk, vv, bb, t_inv)[:2]

        _, vjp = jax.vjp(f, s_in_ref[0], q_ref[...], k_ref[...], v_ref[...], bg_ref[...])
        dS, dq, dk, dv, dbg = vjp((do_ref[...], ds_ref[...]))
        dq_ref[...] = dq
        dk_ref[...] = dk
        dv_ref[...] = dv
        dbg_ref[...] = dbg
        ds_ref[...] = dS

    def rev(w):
        return pl.BlockSpec((CHUNK, w), lambda s: (nc - 1 - s, 0))

    def rev4(n):
        return pl.BlockSpec((1, H, n, n), lambda s: (nc - 1 - s, 0, 0, 0))

    return pl.pallas_call(
        body, name=name, grid=(nc,),
        in_specs=[rev(D_KEY_B)] * 3 + [rev(128), rev4(HEAD_DIM), rev4(CHUNK), rev(D_KEY_B)],
        out_specs=[rev(D_KEY_B)] * 3 + [rev(128)],
        out_shape=[jax.ShapeDtypeStruct((T, D_KEY_B), F32)] * 3 + [jax.ShapeDtypeStruct((T, 128), F32)],
        scratch_shapes=[pltpu.VMEM((H, HEAD_DIM, HEAD_DIM), F32)],
        compiler_params=_ARB,
    )(q, k, v, bg, s_all, t_all, do)


def _loss_fb(name, h2, w, tgt, tm, T):
    def tile_loss(h, wv, t):
        y = _f_rms(h, wv)[0]
        e = y - t
        return 0.5 * jnp.sum(jnp.mean(e * e, axis=-1))

    def body(h_ref, w_ref, t_ref, dh_ref, loss_ref, dw_ref):
        @pl.when(pl.program_id(0) == 0)
        def _():
            loss_ref[...] = jnp.zeros_like(loss_ref)
            dw_ref[...] = jnp.zeros_like(dw_ref)

        val, (dh, dw) = jax.value_and_grad(tile_loss, argnums=(0, 1))(h_ref[...], w_ref[...], t_ref[...])
        dh_ref[...] = dh
        dw_ref[...] += dw
        loss_ref[...] += jnp.reshape(val, (1, 1))

    return pl.pallas_call(
        body, name=name, grid=(T // tm,),
        in_specs=[_row_spec(tm, D_MODEL, 0), _full_spec((1, D_MODEL)), _row_spec(tm, D_MODEL, 0)],
        out_specs=[_row_spec(tm, D_MODEL, 0), _full_spec((1, 1)), _full_spec((1, D_MODEL))],
        out_shape=[jax.ShapeDtypeStruct((T, D_MODEL), F32), jax.ShapeDtypeStruct((1, 1), F32),
                   jax.ShapeDtypeStruct((1, D_MODEL), F32)],
        compiler_params=_ARB,
    )(h2, w, tgt)


def _adamw(name, w, g, m, v, tile):
    R = w.shape[0]

    def body(w_ref, g_ref, m_ref, v_ref, d_out, m_out, v_out):
        g = g_ref[...]
        mm = ADAM_B1 * m_ref[...] + (1.0 - ADAM_B1) * g
        vv = ADAM_B2 * v_ref[...] + (1.0 - ADAM_B2) * jnp.square(g)
        m_hat = mm / (1.0 - ADAM_B1 ** ADAM_STEP)
        v_hat = vv / (1.0 - ADAM_B2 ** ADAM_STEP)
        d_out[...] = -ADAM_LR * (m_hat / (jnp.sqrt(v_hat) + ADAM_EPS) + ADAM_WD * w_ref[...])
        m_out[...] = mm
        v_out[...] = vv

    spec = pl.BlockSpec((tile, D_MODEL), lambda i: (i, 0))
    return pl.pallas_call(
        body, name=name, grid=(R // tile,), in_specs=[spec] * 4, out_specs=[spec] * 3,
        out_shape=[jax.ShapeDtypeStruct((R, D_MODEL), F32)] * 3, compiler_params=_ARB,
    )(w, g, m, v)


def _sum4(name, r):
    R = r.shape[1]

    def body(r_ref, o_ref):
        o_ref[...] = ((r_ref[0].astype(F32) + r_ref[1].astype(F32)) + r_ref[2].astype(F32)) + r_ref[3].astype(F32)

    return pl.pallas_call(
        body, name=name, grid=(R // HALF_TILE,),
        in_specs=[pl.BlockSpec((4, HALF_TILE, D_MODEL), lambda i: (0, i, 0))],
        out_specs=pl.BlockSpec((HALF_TILE, D_MODEL), lambda i: (i, 0)),
        out_shape=jax.ShapeDtypeStruct((R, D_MODEL), F32), compiler_params=_ARB,
    )(r)


def _add2(name, a, b):
    R = a.shape[0]

    def body(a_ref, b_ref, o_ref):
        o_ref[...] = (a_ref[...].astype(F32) + b_ref[...].astype(F32)).astype(BF16)

    spec = pl.BlockSpec((HALF_TILE, D_MODEL), lambda i: (i, 0))
    return pl.pallas_call(
        body, name=name, grid=(R // HALF_TILE,), in_specs=[spec, spec], out_specs=spec,
        out_shape=jax.ShapeDtypeStruct((R, D_MODEL), BF16), compiler_params=_ARB,
    )(a, b)


_ANY = pl.BlockSpec(memory_space=pl.ANY)


def _place():
    return lax.axis_index("x"), lax.axis_index("y"), lax.axis_index("c")


def _gather_chips(name, shard):
    def body(s_ref, out_ref, send_sems, recv_sems, local_sem):
        x, y, c = _place()
        me = 2 * x + y
        chips = [(1 - x, y), (x, 1 - y), (1 - x, 1 - y)]
        mine = pltpu.make_async_copy(s_ref, out_ref.at[me], local_sem)
        mine.start()

        def copy(k, slot, px, py):
            return pltpu.make_async_remote_copy(src_ref=s_ref, dst_ref=out_ref.at[slot], send_sem=send_sems.at[k],
                                                recv_sem=recv_sems.at[k], device_id=(px, py, c), device_id_type=MESH)

        sends = [copy(k, me, px, py) for k, (px, py) in enumerate(chips)]
        for cp in sends:
            cp.start()
        for k, (px, py) in enumerate(chips):
            copy(k, 2 * px + py, px, py).wait_recv()
        for cp in sends:
            cp.wait_send()
        mine.wait()

    return pl.pallas_call(
        body, name=name, in_specs=[_ANY], out_specs=_ANY,
        out_shape=jax.ShapeDtypeStruct((4,) + shard.shape, shard.dtype),
        scratch_shapes=[pltpu.SemaphoreType.DMA((3,)), pltpu.SemaphoreType.DMA((3,)), pltpu.SemaphoreType.DMA],
    )(shard)


def _scatter_chips(name, g):
    def body(g_ref, out_ref, send_sems, recv_sems, local_sem):
        x, y, c = _place()
        me = 2 * x + y
        chips = [(1 - x, y), (x, 1 - y), (1 - x, 1 - y)]
        mine = pltpu.make_async_copy(g_ref.at[me], out_ref.at[me], local_sem)
        mine.start()

        def copy(k, src_slot, dst_slot, px, py):
            return pltpu.make_async_remote_copy(src_ref=g_ref.at[src_slot], dst_ref=out_ref.at[dst_slot],
                                                send_sem=send_sems.at[k], recv_sem=recv_sems.at[k],
                                                device_id=(px, py, c), device_id_type=MESH)

        sends = [copy(k, 2 * px + py, me, px, py) for k, (px, py) in enumerate(chips)]
        for cp in sends:
            cp.start()
        for k, (px, py) in enumerate(chips):
            copy(k, me, 2 * px + py, px, py).wait_recv()
        for cp in sends:
            cp.wait_send()
        mine.wait()

    return pl.pallas_call(
        body, name=name, in_specs=[_ANY], out_specs=_ANY,
        out_shape=jax.ShapeDtypeStruct(g.shape, g.dtype),
        scratch_shapes=[pltpu.SemaphoreType.DMA((3,)), pltpu.SemaphoreType.DMA((3,)), pltpu.SemaphoreType.DMA],
    )(g)


def _swap_sibling(name, p):
    def body(p_ref, out_ref, send_sem, recv_sem):
        x, y, c = _place()
        cp = pltpu.make_async_remote_copy(src_ref=p_ref, dst_ref=out_ref, send_sem=send_sem, recv_sem=recv_sem,
                                          device_id=(x, y, 1 - c), device_id_type=MESH)
        cp.start()
        cp.wait()

    return pl.pallas_call(
        body, name=name, in_specs=[_ANY], out_specs=_ANY,
        out_shape=jax.ShapeDtypeStruct(p.shape, p.dtype),
        scratch_shapes=[pltpu.SemaphoreType.DMA, pltpu.SemaphoreType.DMA],
    )(p)


def _allsum_small(name, pack):
    def body(p_ref, out_ref, slots, send_sems, recv_sems):
        x, y, c = _place()
        me = 4 * x + 2 * y + c
        slots[me] = p_ref[...]
        sends = []
        for k in range(1, 8):
            px = 1 - x if k & 4 else x
            py = 1 - y if k & 2 else y
            pc = 1 - c if k & 1 else c
            cp = pltpu.make_async_remote_copy(src_ref=p_ref, dst_ref=slots.at[me], send_sem=send_sems.at[k - 1],
                                              recv_sem=recv_sems.at[k - 1], device_id=(px, py, pc), device_id_type=MESH)
            cp.start()
            sends.append(cp)
        for k in range(1, 8):
            px = 1 - x if k & 4 else x
            py = 1 - y if k & 2 else y
            pc = 1 - c if k & 1 else c
            pltpu.make_async_remote_copy(src_ref=p_ref, dst_ref=slots.at[4 * px + 2 * py + pc],
                                         send_sem=send_sems.at[k - 1], recv_sem=recv_sems.at[k - 1],
                                         device_id=(px, py, pc), device_id_type=MESH).wait_recv()
        for cp in sends:
            cp.wait_send()
        acc = slots[0]
        for s in range(1, 8):
            acc = acc + slots[s]
        out_ref[...] = acc

    vm = pl.BlockSpec(memory_space=pltpu.VMEM)
    return pl.pallas_call(
        body, name=name, in_specs=[vm], out_specs=vm,
        out_shape=jax.ShapeDtypeStruct(pack.shape, pack.dtype),
        scratch_shapes=[pltpu.VMEM((8,) + pack.shape, pack.dtype), pltpu.SemaphoreType.DMA((7,)),
                        pltpu.SemaphoreType.DMA((7,))],
    )(pack)


def _pack_shards(parts, rows_total):
    flat = [p.reshape(-1, D_MODEL) for p in parts]
    used = sum(f.shape[0] for f in flat)
    return jnp.concatenate(flat + [jnp.zeros((rows_total - used, D_MODEL), flat[0].dtype)], axis=0)


def _unpack_shards(buf, shapes):
    out, r0 = [], 0
    for shp in shapes:
        n = math.prod(shp) // D_MODEL
        out.append(buf[r0:r0 + n].reshape(shp))
        r0 += n
    return out


def kernel(x, norm_mix, w_in, conv_w, a_log, dt_bias, dn_norm, w_proj_attn, w_proj_delta, w_out, norm_ffn, w_gate, w_up, w_down, norm_final, loss_target, m_norm_mix, m_w_in, m_conv_w, m_a_log, m_dt_bias, m_dn_norm, m_w_proj_attn, m_w_proj_delta, m_w_out, m_norm_ffn, m_w_gate, m_w_up, m_w_down, m_norm_final, v_norm_mix, v_w_in, v_conv_w, v_a_log, v_dt_bias, v_dn_norm, v_w_proj_attn, v_w_proj_delta, v_w_out, v_norm_ffn, v_w_gate, v_w_up, v_w_down, v_norm_final):
    T = x.shape[1]
    D = D_MODEL
    tm = 256
    mt = 1024 if T % 1024 == 0 else 512
    x2 = x.reshape(T, D)
    tgt = loss_target.reshape(T, D)

    sharded = (w_in, w_proj_attn, w_proj_delta, w_out, w_gate, w_up, w_down)
    conv_bits = lax.bitcast_convert_type(conv_w[0], BF16)
    w16 = _pack_shards([w[0].astype(BF16) for w in sharded] + [conv_bits], PACK_R16)
    my_c = lax.axis_index("c")
    w_mine = _gather_chips("gather_weights", lax.dynamic_slice_in_dim(w16, my_c * HALF, HALF, axis=0))
    w_other = _swap_sibling("swap_weights", w_mine)
    wg = jnp.concatenate([jnp.where(my_c == 0, w_mine, w_other), jnp.where(my_c == 0, w_other, w_mine)], axis=1)

    def seg(r0, n):
        return wg[:, r0:r0 + n, :]

    r = 0
    Win = seg(r, 2692).reshape(4, D, 2692).transpose(1, 0, 2).reshape(D, D_IN); r += 2692
    Wpa = seg(r, 128).reshape(4, D_ATTN_OUT, 256).transpose(1, 0, 2).reshape(D_ATTN_OUT, D); r += 128
    Wpd = seg(r, 256).reshape(D, D); r += 256
    Wout = seg(r, 256).reshape(D, D); r += 256
    Wg = seg(r, 704).reshape(4, D, 704).transpose(1, 0, 2).reshape(D, D_FF); r += 704
    Wu = seg(r, 704).reshape(4, D, 704).transpose(1, 0, 2).reshape(D, D_FF); r += 704
    Wd = seg(r, 704).reshape(D_FF, D); r += 704
    convw = lax.bitcast_convert_type(seg(r, 6).reshape(4, CONV_WIDTH, 768, 2), F32)
    convw = convw.transpose(1, 0, 2).reshape(CONV_WIDTH, 3 * D_KEY_B)
    Wmain = jnp.concatenate([Win[:, 4608:8704], Win[:, 8720:], Win[:, :4608]], axis=1)
    Wba = jnp.pad(Win[:, 8704:8720], ((0, 0), (0, 112)))
    Wgu = jnp.concatenate([Wg, Wu], axis=1)
    alog_v = jnp.pad(a_log, ((0, 0), (8, 112)))
    dtb_v = jnp.pad(dt_bias, ((0, 0), (8, 112)))

    (u,) = _rowwise("rms_mix", _f_rms, [(x2, 0, D)], [norm_mix], [(D, BF16)], tm, T)
    proj = _matmul("mm_in", u, Wmain, "nn", T, D_MAIN, D, mt, 1536, D, F32)
    ba = _matmul("mm_ba", u, Wba, "nn", T, 128, D, mt, 128, D, F32)

    att = [_attn_fwd(f"attn_fwd{g}", proj, g, T) for g in range(3)]
    (ya_m,) = _rowwise("merge", _f_merge, [(att[g][0], 0, 512) for g in range(3)] + [(att[g][1], 0, 512) for g in range(3)],
                       [], [(512, BF16)], tm, T)
    qn, kn, vd, bg = _dn_pre_fwd("dn_pre_fwd", proj, ba, convw, alog_v, dtb_v, tm, T)
    o_d, s_all, t_all = _delta_fwd("delta_fwd", qn, kn, vd, bg, T)
    (od,) = _rowwise("post_dn", _f_postdn, [(o_d, 0, D), (proj, C_Z // D, D)], [dn_norm], [(D, BF16)], tm, T)
    y_a = _matmul("mm_pa", ya_m, Wpa, "nn", T, D, 512, mt, D, 512, F32)
    y_b = _matmul("mm_pd", od, Wpd, "nn", T, D, D, mt, D, D, F32)
    (mix,) = _rowwise("mix", _f_mix, [(y_a, 0, D), (y_b, 0, D), (proj, C_GATE // 2048, 2048)], [], [(D, BF16)], tm, T)
    h1 = _matmul("mm_out", mix, Wout, "nn", T, D, D, mt, D, D, F32, residual=x2)
    (hn,) = _rowwise("rms_ffn", _f_rms, [(h1, 0, D)], [norm_ffn], [(D, BF16)], tm, T)
    gu = _matmul("mm_gu", hn, Wgu, "nn", T, 2 * D_FF, D, mt, 1408, D, F32)
    (act,) = _rowwise("act", _f_act, [(gu, 0, 2 * D_FF)], [], [(D_FF, BF16)], tm, T)
    h2 = _matmul("mm_down", act, Wd, "nn", T, D, D_FF, mt, D, 1408, F32, residual=h1)
    dh2, loss, g_norm_final = _loss_fb("loss", h2, norm_final.reshape(1, D), tgt, tm, T)

    dact = _matmul("mm_dact", dh2, Wd, "nt", T, D_FF, D, mt, 1408, D, F32)
    gWd = _matmul("mm_gwd", act, dh2, "tn", D_FF, D, T, 1408, D, mt, F32)
    (dgu,) = _rowwise_vjp("act_bwd", _f_act, [(gu, 0, 2 * D_FF)], [], [(dact, 0, D_FF)], [BF16], tm, T)
    dhn = _matmul("mm_dhn", dgu, Wgu, "nt", T, D, 2 * D_FF, mt, D, 1408, F32)
    gWgu = _matmul("mm_gwgu", hn, dgu, "tn", D, 2 * D_FF, T, D, 1408, mt, F32)
    dh1, g_norm_ffn = _rowwise_vjp("rms_ffn_bwd", _f_rms, [(h1, 0, D)], [norm_ffn], [(dhn, 0, D)], [F32], tm, T,
                                   residual=(dh2, 0, D))
    dmix = _matmul("mm_dmix", dh1, Wout, "nt", T, D, D, mt, D, D, F32)
    gWout = _matmul("mm_gwout", mix, dh1, "tn", D, D, T, D, D, mt, F32)
    dya, dyb, dgate = _rowwise_vjp("mix_bwd", _f_mix, [(y_a, 0, D), (y_b, 0, D), (proj, C_GATE // 2048, 2048)], [],
                                   [(dmix, 0, D)], [BF16, BF16, BF16], tm, T)
    dya_m = _matmul("mm_dyam", dya, Wpa, "nt", T, 512, D, mt, 512, D, F32)
    gWpa = _matmul("mm_gwpa", ya_m, dya, "tn", 512, D, T, 512, D, mt, F32)
    dod = _matmul("mm_dod", dyb, Wpd, "nt", T, D, D, mt, D, D, F32)
    gWpd = _matmul("mm_gwpd", od, dyb, "tn", D, D, T, D, D, mt, F32)
    do_d, dz, g_dn_norm = _rowwise_vjp("post_dn_bwd", _f_postdn, [(o_d, 0, D), (proj, C_Z // D, D)], [dn_norm],
                                       [(dod, 0, D)], [F32, BF16], tm, T)
    dqn, dkn, dvd, dbg = _delta_bwd("delta_bwd", qn, kn, vd, bg, s_all, t_all, do_d, T)
    dqkvd, dba, g_conv8, g_alog_v, g_dtb_v = _dn_pre_bwd("dn_pre_bwd", proj, ba, convw, alog_v, dtb_v,
                                                         dqn, dkn, dvd, dbg, tm, T)
    mg = _rowwise_vjp("merge_bwd", _f_merge,
                      [(att[g][0], 0, 512) for g in range(3)] + [(att[g][1], 0, 512) for g in range(3)], [],
                      [(dya_m, 0, 512)], [F32] * 6, tm, T)
    datt = [_attn_bwd(f"attn_bwd{g}", proj, mg[g], mg[3 + g], g, T) for g in range(3)]
    dproj = jnp.concatenate([dqkvd.astype(BF16), dz, dgate]
                            + [datt[g][0].astype(BF16) for g in range(3)]
                            + [datt[g][1].astype(BF16) for g in range(3)]
                            + [datt[g][2].astype(BF16) for g in range(3)], axis=1)
    gWmain = _matmul("mm_gwmain", u, dproj, "tn", D, D_MAIN, T, D, 1536, mt, F32)
    gWba = _matmul("mm_gwba", u, dba, "tn", D, 128, T, D, 128, mt, F32)
    du0 = _matmul("mm_du_ba", dba, Wba, "nt", T, D, 128, mt, D, 128, F32)
    du = _matmul("mm_du", dproj, Wmain, "nt", T, D, D_MAIN, mt, D, 1536, F32, residual=du0)
    dx, g_norm_mix = _rowwise_vjp("rms_mix_bwd", _f_rms, [(x2, 0, D)], [norm_mix], [(du, 0, D)], [F32], tm, T,
                                  residual=(dh1, 0, D))

    gWin = jnp.concatenate([gWmain[:, C_QA:], gWmain[:, :C_GATE], gWba[:, :16], gWmain[:, C_GATE:C_QA]], axis=1)

    def cols4(g, n):
        return g.reshape(g.shape[0], 4, n).transpose(1, 0, 2).reshape(4, -1, D)

    parts = [cols4(gWin, 2692), cols4(gWpa, 256), gWpd.reshape(4, 256, D), gWout.reshape(4, 256, D),
             cols4(gWgu[:, :D_FF], 704), cols4(gWgu[:, D_FF:], 704), gWd.reshape(4, 704, D), cols4(g_conv8[:CONV_WIDTH], 768)]
    used = sum(p.shape[1] for p in parts)
    g_pack = jnp.concatenate([p.astype(BF16) for p in parts] + [jnp.zeros((4, PACK_R - used, D), BF16)], axis=1)
    g_keep = lax.dynamic_slice_in_dim(g_pack, my_c * HALF, HALF, axis=1)
    g_give = lax.dynamic_slice_in_dim(g_pack, (1 - my_c) * HALF, HALF, axis=1)
    g_got = _swap_sibling("swap_grads", g_give)
    g_pair = _add2("add_cores", g_keep.reshape(4 * HALF, D), g_got.reshape(4 * HALF, D)).reshape(4, HALF, D)
    recv = _scatter_chips("scatter_grads", g_pair)
    g_mine = _sum4("sum_chips", recv)
    g_other = _swap_sibling("swap_sums", g_mine)
    g_all = jnp.concatenate([jnp.where(my_c == 0, g_mine, g_other), jnp.where(my_c == 0, g_other, g_mine)], axis=0)

    locals_ = (w_in, w_proj_attn, w_proj_delta, w_out, w_gate, w_up, w_down, conv_w)
    ms = (m_w_in, m_w_proj_attn, m_w_proj_delta, m_w_out, m_w_gate, m_w_up, m_w_down, m_conv_w)
    vs = (v_w_in, v_w_proj_attn, v_w_proj_delta, v_w_out, v_w_gate, v_w_up, v_w_down, v_conv_w)
    big = _adamw("adamw", _pack_shards(locals_, PACK_R), g_all, _pack_shards(ms, PACK_R), _pack_shards(vs, PACK_R), PACK_TILE)
    big = [_unpack_shards(b, [w.shape for w in locals_]) for b in [g_all] + list(big)]

    def small_pack(nm, nf, nfin, dn, al, dt, extra):
        row3 = jnp.concatenate([dn.reshape(1, 128), al.reshape(1, 8), dt.reshape(1, 8), extra.reshape(1, 1),
                                jnp.zeros((1, D - 145), F32)], axis=1)
        return jnp.concatenate([nm.reshape(1, D), nf.reshape(1, D), nfin.reshape(1, D), row3, jnp.zeros((4, D), F32)], axis=0)

    zero1 = jnp.zeros((1, 1), F32)
    sp = small_pack(g_norm_mix, g_norm_ffn, g_norm_final, g_dn_norm, g_alog_v[:, 8:16], g_dtb_v[:, 8:16], loss)
    tot = _allsum_small("allsum_small", sp)
    sm = _adamw("adamw_small", small_pack(norm_mix, norm_ffn, norm_final, dn_norm, a_log, dt_bias, zero1), tot,
                small_pack(m_norm_mix, m_norm_ffn, m_norm_final, m_dn_norm, m_a_log, m_dt_bias, zero1),
                small_pack(v_norm_mix, v_norm_ffn, v_norm_final, v_dn_norm, v_a_log, v_dt_bias, zero1), 8)
    sm = [tot] + list(sm)

    def small_unpack(p):
        return {"norm_mix": p[0:1], "norm_ffn": p[1:2], "norm_final": p[2], "dn_norm": p[3:4, :128],
                "a_log": p[3:4, 128:136], "dt_bias": p[3:4, 136:144]}

    sm = [small_unpack(p) for p in sm]
    loss_out = tot[3, 144]

    order = ("norm_mix", "w_in", "conv_w", "a_log", "dt_bias", "dn_norm", "w_proj_attn", "w_proj_delta", "w_out",
             "norm_ffn", "w_gate", "w_up", "w_down", "norm_final")
    big_names = ("w_in", "w_proj_attn", "w_proj_delta", "w_out", "w_gate", "w_up", "w_down", "conv_w")
    outs = [loss_out, dx.reshape(x.shape)]
    for kind in range(4):
        for n in order:
            outs.append(big[kind][big_names.index(n)] if n in big_names else sm[kind][n])
    return tuple(outs)
```

```python
import functools
import math

import jax
import jax.numpy as jnp
from jax import lax
from jax.experimental import pallas as pl
from jax.experimental.pallas import tpu as pltpu

F32 = jnp.float32
BF16 = jnp.bfloat16
MESH = pl.DeviceIdType.MESH

D_MODEL = 1024
N_HEADS_A = 12
HEAD_DIM = 128
BLOCK_A = 128
DILATIONS = (1, 4, 16)
W_SUB = 128
D_ATTN_OUT = 512
N_HEADS_B = 8
D_KEY_B = 1024
CONV_WIDTH = 4
CHUNK = 64
D_FF = 2816
EPS = 1e-6
D_IN = 10768
D_MAIN = 10752
C_QKVD, C_Z, C_GATE, C_QA, C_KA, C_VA = 0, 3072, 4096, 6144, 7680, 9216

ADAM_LR, ADAM_B1, ADAM_B2, ADAM_EPS, ADAM_WD, ADAM_STEP = 0.001, 0.9, 0.999, 1e-08, 0.01, 10

PACK_ROWS = (("w_in", 2692), ("w_proj_attn", 128), ("w_proj_delta", 256), ("w_out", 256),
             ("w_gate", 704), ("w_up", 704), ("w_down", 704), ("conv_w", 3))
PACK_TILE = 512
PACK_R = 5632
PACK_R16 = 5632
HALF = PACK_R // 2
HALF_TILE = 256

VMEM_LIMIT = 48 * 1024 * 1024


def _cparams(n):
    return pltpu.CompilerParams(dimension_semantics=("arbitrary",) * n, vmem_limit_bytes=VMEM_LIMIT)


_ARB = _cparams(1)


def _row_spec(tm, w, cb):
    return pl.BlockSpec((tm, w), lambda i: (i, cb))


def _full_spec(shape):
    nd = len(shape)
    return pl.BlockSpec(shape, lambda i: (0,) * nd)


def _rowwise(name, fn, ins, params, outs, tm, T):
    n_in, n_p = len(ins), len(params)

    def body(*refs):
        xs = [r[...] for r in refs[:n_in + n_p]]
        res = fn(*xs)
        for o_ref, v in zip(refs[n_in + n_p:], res):
            o_ref[...] = v.astype(o_ref.dtype)

    return pl.pallas_call(
        body, name=name, grid=(T // tm,),
        in_specs=[_row_spec(tm, w, cb) for (_, cb, w) in ins] + [_full_spec(p.shape) for p in params],
        out_specs=[_row_spec(tm, w, 0) for (w, _) in outs],
        out_shape=[jax.ShapeDtypeStruct((T, w), dt) for (w, dt) in outs],
        compiler_params=_ARB,
    )(*[a for (a, _, _) in ins], *params)


def _rowwise_vjp(name, fn, ins, params, cts, gdtypes, tm, T, residual=None):
    n_in, n_p, n_ct = len(ins), len(params), len(cts)
    n_res = 0 if residual is None else 1

    def body(*refs):
        i = pl.program_id(0)
        k = 0
        xs = [r[...].astype(F32) for r in refs[k:k + n_in]]
        k += n_in
        ps = [r[...] for r in refs[k:k + n_p]]
        k += n_p
        gs = tuple(r[...].astype(F32) for r in refs[k:k + n_ct])
        k += n_ct
        res_refs = refs[k:k + n_res]
        k += n_res
        gi_refs = refs[k:k + n_in]
        gp_refs = refs[k + n_in:]
        _, vjp = jax.vjp(fn, *xs, *ps)
        g = vjp(gs)
        for j, (ref, v) in enumerate(zip(gi_refs, g[:n_in])):
            if j == 0 and n_res:
                v = v + res_refs[0][...].astype(F32)
            ref[...] = v.astype(ref.dtype)

        @pl.when(i == 0)
        def _():
            for ref in gp_refs:
                ref[...] = jnp.zeros_like(ref)

        for ref, v in zip(gp_refs, g[n_in:]):
            ref[...] += v

    res_in = [] if residual is None else [residual]
    return pl.pallas_call(
        body, name=name, grid=(T // tm,),
        in_specs=([_row_spec(tm, w, cb) for (_, cb, w) in ins] + [_full_spec(p.shape) for p in params]
                  + [_row_spec(tm, w, cb) for (_, cb, w) in cts] + [_row_spec(tm, w, cb) for (_, cb, w) in res_in]),
        out_specs=[_row_spec(tm, w, 0) for (_, _, w) in ins] + [_full_spec(p.shape) for p in params],
        out_shape=([jax.ShapeDtypeStruct((T, w), dt) for (_, _, w), dt in zip(ins, gdtypes)]
                   + [jax.ShapeDtypeStruct(p.shape, F32) for p in params]),
        compiler_params=_ARB,
    )(*[a for (a, _, _) in ins], *params, *[a for (a, _, _) in cts], *[a for (a, _, _) in res_in])


def _matmul(name, a, b, mode, M, N, K, tm, tn, tk, out_dtype, a_off=0, b_off=0, residual=None):
    nk = K // tk
    assert M % tm == 0 and N % tn == 0 and K % tk == 0
    if mode == "nn":
        assert a_off % tk == 0 and b_off % tn == 0
        a_spec = pl.BlockSpec((tm, tk), lambda i, j, k: (i, a_off // tk + k))
        b_spec = pl.BlockSpec((tk, tn), lambda i, j, k: (k, b_off // tn + j))
        dims = (((1,), (0,)), ((), ()))
    elif mode == "nt":
        assert a_off % tk == 0 and b_off == 0
        a_spec = pl.BlockSpec((tm, tk), lambda i, j, k: (i, a_off // tk + k))
        b_spec = pl.BlockSpec((tn, tk), lambda i, j, k: (j, k))
        dims = (((1,), (1,)), ((), ()))
    else:
        assert a_off % tm == 0 and b_off % tn == 0
        a_spec = pl.BlockSpec((tk, tm), lambda i, j, k: (k, a_off // tm + i))
        b_spec = pl.BlockSpec((tk, tn), lambda i, j, k: (k, b_off // tn + j))
        dims = (((0,), (0,)), ((), ()))
    has_res = residual is not None

    def body(*refs):
        a_ref, b_ref = refs[0], refs[1]
        r_ref = refs[2] if has_res else None
        o_ref, acc_ref = refs[-2], refs[-1]
        k = pl.program_id(2)

        @pl.when(k == 0)
        def _():
            acc_ref[...] = jnp.zeros_like(acc_ref)

        acc_ref[...] += lax.dot_general(a_ref[...].astype(BF16), b_ref[...].astype(BF16), dims,
                                        preferred_element_type=F32)

        @pl.when(k == nk - 1)
        def _():
            acc = acc_ref[...]
            if has_res:
                acc = acc + r_ref[...].astype(F32)
            o_ref[...] = acc.astype(o_ref.dtype)

    in_specs = [a_spec, b_spec]
    args = [a, b]
    if has_res:
        in_specs.append(pl.BlockSpec((tm, tn), lambda i, j, k: (i, j)))
        args.append(residual)
    return pl.pallas_call(
        body, name=name, grid=(M // tm, N // tn, nk),
        in_specs=in_specs,
        out_specs=pl.BlockSpec((tm, tn), lambda i, j, k: (i, j)),
        out_shape=jax.ShapeDtypeStruct((M, N), out_dtype),
        scratch_shapes=[pltpu.VMEM((tm, tn), F32)],
        compiler_params=pltpu.CompilerParams(dimension_semantics=("parallel", "parallel", "arbitrary"),
                                             vmem_limit_bytes=VMEM_LIMIT),
    )(*args)


def _f_rms(x, w):
    return (x * lax.rsqrt(jnp.mean(x * x, axis=-1, keepdims=True) + EPS) * w,)


def _silu(x):
    return x * jax.nn.sigmoid(x)


def _softplus(x):
    return jnp.maximum(x, 0.0) + jnp.log1p(jnp.exp(-jnp.abs(x)))


def _heads(x, n):
    return [x[:, HEAD_DIM * h:HEAD_DIM * (h + 1)] for h in range(n)]


def _f_dnpoint(c, ba, alog_v, dtb_v):
    cs = _silu(c)
    q, k, v = cs[:, :D_KEY_B], cs[:, D_KEY_B:2 * D_KEY_B], cs[:, 2 * D_KEY_B:]
    qn = jnp.concatenate([t * lax.rsqrt(jnp.sum(t * t, axis=-1, keepdims=True) + EPS) * (HEAD_DIM ** -0.5)
                          for t in _heads(q, N_HEADS_B)], axis=1)
    kn = jnp.concatenate([t * lax.rsqrt(jnp.sum(t * t, axis=-1, keepdims=True) + EPS)
                          for t in _heads(k, N_HEADS_B)], axis=1)
    lane = lax.broadcasted_iota(jnp.int32, ba.shape, 1)
    beta = jax.nn.sigmoid(ba)
    g = -jnp.exp(alog_v) * _softplus(ba + dtb_v)
    bg = jnp.where(lane < N_HEADS_B, beta, jnp.where(lane < 2 * N_HEADS_B, g, 0.0))
    return qn, kn, v, bg


def _f_postdn(o, z, w):
    outs = []
    for oh, zh in zip(_heads(o, N_HEADS_B), _heads(z, N_HEADS_B)):
        outs.append(oh * lax.rsqrt(jnp.mean(oh * oh, axis=-1, keepdims=True) + EPS) * w * _silu(zh))
    return (jnp.concatenate(outs, axis=1),)


def _f_merge(o0, o1, o2, l0, l1, l2):
    m = lax.stop_gradient(jnp.maximum(jnp.maximum(l0, l1), l2))
    e0, e1, e2 = jnp.exp(l0 - m), jnp.exp(l1 - m), jnp.exp(l2 - m)
    return ((e0 * o0 + e1 * o1 + e2 * o2) / (e0 + e1 + e2),)


def _f_mix(ya, yb, gr):
    return (jax.nn.sigmoid(gr[:, :D_MODEL]) * ya + jax.nn.sigmoid(gr[:, D_MODEL:]) * yb,)


def _f_act(gu):
    return (_silu(gu[:, :D_FF]) * gu[:, D_FF:],)


def _attn_block(q, kp, kc, vp, vc, slope_d, first_key):
    k2 = jnp.concatenate([kp, kc], axis=0).astype(BF16)
    v2 = jnp.concatenate([vp, vc], axis=0).astype(BF16)
    s = lax.dot_general(q.astype(BF16), k2, (((1,), (1,)), ((), ())), preferred_element_type=F32)
    s = s * (HEAD_DIM ** -0.5)
    i = lax.broadcasted_iota(jnp.int32, (BLOCK_A, 2 * BLOCK_A), 0)
    j = lax.broadcasted_iota(jnp.int32, (BLOCK_A, 2 * BLOCK_A), 1)
    delta = BLOCK_A + i - j
    valid = (delta >= 0) & (delta <= W_SUB) & (j >= first_key)
    s = jnp.where(valid, s - slope_d * delta.astype(F32), -jnp.inf)
    m = lax.stop_gradient(jnp.max(s, axis=-1, keepdims=True))
    p = jnp.exp(s - m)
    den = jnp.sum(p, axis=-1, keepdims=True)
    o = jnp.dot(p.astype(BF16), v2, preferred_element_type=F32) / den
    lse = jnp.broadcast_to(m + jnp.log(den), (BLOCK_A, HEAD_DIM))
    return o, lse


def _slope_d(group, h, d):
    hg = (group * 4 + h + 1).astype(F32)
    return jnp.exp(jnp.full((1, 1), -8.0 * math.log(2.0) / N_HEADS_A, F32) * hg) * float(d)


ATT_CFG = ((1, 4, 4), (4, 2, 1), (16, 1, 1))
R_UNROLL = 4


def _attn_geometry(group, T):
    d, m, hp = ATT_CFG[group]
    span = BLOCK_A * d
    tile = m * span
    assert T % tile == 0 and 4 % hp == 0
    return d, m, hp, span, tile, T // tile, HEAD_DIM * hp


def _sub_rows(b, span, r, d):
    return pl.ds(b * span + r, BLOCK_A, stride=d) if d > 1 else pl.ds(b * span, BLOCK_A)


def _for_each_r(d, fn):
    if d <= R_UNROLL:
        for r in range(d):
            fn(r)
    else:
        def step(r, carry):
            fn(r)
            return carry

        lax.fori_loop(0, d, step, 0, unroll=R_UNROLL)


def _attn_fwd(name, proj, group, T):
    d, m, hp, span, tile, nt, lanes = _attn_geometry(group, T)
    qb, kb, vb = [(c + 512 * group) // lanes for c in (C_QA, C_KA, C_VA)]

    def body(q_ref, kc_ref, kp_ref, vc_ref, vp_ref, o_ref, l_ref):
        hb, n = pl.program_id(0), pl.program_id(1)
        first_key = jnp.where(n > 0, 0, BLOCK_A)
        for hh in range(hp):
            ls = slice(HEAD_DIM * hh, HEAD_DIM * (hh + 1))
            sl = _slope_d(group, hb * hp + hh, d)
            for b in range(m):
                def one(r, b=b, ls=ls, sl=sl):
                    rq = _sub_rows(b, span, r, d)
                    if b == 0:
                        rp = _sub_rows(0, span, r, d)
                        kp, vp, fk = kp_ref[rp, ls], vp_ref[rp, ls], first_key
                    else:
                        rp = _sub_rows(b - 1, span, r, d)
                        kp, vp, fk = kc_ref[rp, ls], vc_ref[rp, ls], 0
                    o, lse = _attn_block(q_ref[rq, ls], kp, kc_ref[rq, ls], vp, vc_ref[rq, ls], sl, fk)
                    o_ref[rq, ls] = o
                    l_ref[rq, ls] = lse

                _for_each_r(d, one)

    def cur(cb):
        return pl.BlockSpec((tile, lanes), lambda hb, n: (n, cb + hb))

    def prev(cb):
        return pl.BlockSpec((span, lanes), lambda hb, n: (jnp.maximum(m * n - 1, 0), cb + hb))

    return pl.pallas_call(
        body, name=name, grid=(4 // hp, nt),
        in_specs=[cur(qb), cur(kb), prev(kb), cur(vb), prev(vb)],
        out_specs=[pl.BlockSpec((tile, lanes), lambda hb, n: (n, hb))] * 2,
        out_shape=[jax.ShapeDtypeStruct((T, D_ATTN_OUT), F32)] * 2,
        compiler_params=_cparams(2),
    )(proj, proj, proj, proj, proj)


def _attn_bwd(name, proj, do, dl, group, T):
    d, m, hp, span, tile, nt, lanes = _attn_geometry(group, T)
    qb, kb, vb = [(c + 512 * group) // lanes for c in (C_QA, C_KA, C_VA)]

    def body(q_ref, kc_ref, kp_ref, vc_ref, vp_ref, do_ref, dl_ref, dq_ref, dk_ref, dv_ref, ck_ref, cv_ref):
        hb, n = pl.program_id(0), pl.program_id(1)
        first_key = jnp.where(n > 0, 0, BLOCK_A)

        @pl.when(n == 0)
        def _():
            ck_ref[...] = jnp.zeros_like(ck_ref)
            cv_ref[...] = jnp.zeros_like(cv_ref)

        dk_ref[...] = ck_ref[...]
        dv_ref[...] = cv_ref[...]

        @pl.when(n < nt)
        def _():
            for hh in range(hp):
                ls = slice(HEAD_DIM * hh, HEAD_DIM * (hh + 1))
                sl = _slope_d(group, hb * hp + hh, d)
                for b in range(m):
                    def one(r, b=b, ls=ls, sl=sl):
                        rq = _sub_rows(b, span, r, d)
                        if b == 0:
                            rp = _sub_rows(0, span, r, d)
                            kp, vp, fk = kp_ref[rp, ls], vp_ref[rp, ls], first_key
                        else:
                            rp = _sub_rows(b - 1, span, r, d)
                            kp, vp, fk = kc_ref[rp, ls], vc_ref[rp, ls], 0
                        f = functools.partial(_attn_block, slope_d=sl, first_key=fk)
                        _, vjp = jax.vjp(f, q_ref[rq, ls], kp, kc_ref[rq, ls], vp, vc_ref[rq, ls])
                        dq, dkp, dkc, dvp, dvc = vjp((do_ref[rq, ls], dl_ref[rq, ls]))
                        dq_ref[rq, ls] = dq
                        ck_ref[rq, ls] = dkc
                        cv_ref[rq, ls] = dvc
                        if b == 0:
                            last = _sub_rows(m - 1, span, r, d)
                            dk_ref[last, ls] += dkp
                            dv_ref[last, ls] += dvp
                        else:
                            ck_ref[rp, ls] += dkp
                            cv_ref[rp, ls] += dvp

                    _for_each_r(d, one)

    def cur(cb):
        return pl.BlockSpec((tile, lanes), lambda hb, n: (jnp.minimum(n, nt - 1), cb + hb))

    def prev(cb):
        return pl.BlockSpec((span, lanes), lambda hb, n: (jnp.maximum(m * jnp.minimum(n, nt - 1) - 1, 0), cb + hb))

    own = pl.BlockSpec((tile, lanes), lambda hb, n: (jnp.minimum(n, nt - 1), hb))
    late = pl.BlockSpec((tile, lanes), lambda hb, n: (jnp.maximum(n - 1, 0), hb))
    return pl.pallas_call(
        body, name=name, grid=(4 // hp, nt + 1),
        in_specs=[cur(qb), cur(kb), prev(kb), cur(vb), prev(vb), own, own],
        out_specs=[own, late, late],
        out_shape=[jax.ShapeDtypeStruct((T, D_ATTN_OUT), F32)] * 3,
        scratch_shapes=[pltpu.VMEM((tile, lanes), F32), pltpu.VMEM((tile, lanes), F32)],
        compiler_params=_cparams(2),
    )(proj, proj, proj, proj, proj, do, dl)


HALO = 8


def _conv_taps(ext, w):
    c = ext[HALO:, :] * w[CONV_WIDTH - 1]
    for k in range(1, CONV_WIDTH):
        c = c + pltpu.roll(ext, k, 0)[HALO:, :] * w[CONV_WIDTH - 1 - k]
    return c


def _taps(w_ref):
    return [w_ref[j:j + 1, :] for j in range(CONV_WIDTH)]


def _dn_pre_fwd(name, proj, ba, conv_w, alog_v, dtb_v, tm, T):
    C = 3 * D_KEY_B

    def body(x_ref, ba_ref, w_ref, al_ref, dt_ref, q_ref, k_ref, v_ref, bg_ref, halo_ref):
        i = pl.program_id(0)

        @pl.when(i == 0)
        def _():
            halo_ref[...] = jnp.zeros_like(halo_ref)

        x = x_ref[...]
        ext = jnp.concatenate([halo_ref[...], x], axis=0)
        c = _conv_taps(ext, _taps(w_ref))
        halo_ref[...] = x[tm - HALO:, :]
        qn, kn, v, bg = _f_dnpoint(c, ba_ref[...], al_ref[...], dt_ref[...])
        q_ref[...] = qn
        k_ref[...] = kn
        v_ref[...] = v
        bg_ref[...] = bg

    return pl.pallas_call(
        body, name=name, grid=(T // tm,),
        in_specs=[_row_spec(tm, C, 0), _row_spec(tm, 128, 0), _full_spec(conv_w.shape), _full_spec((1, 128)),
                  _full_spec((1, 128))],
        out_specs=[_row_spec(tm, D_KEY_B, 0)] * 3 + [_row_spec(tm, 128, 0)],
        out_shape=[jax.ShapeDtypeStruct((T, D_KEY_B), F32)] * 3 + [jax.ShapeDtypeStruct((T, 128), F32)],
        scratch_shapes=[pltpu.VMEM((HALO, C), F32)],
        compiler_params=_ARB,
    )(proj, ba, conv_w, alog_v, dtb_v)


def _dn_pre_bwd(name, proj, ba, conv_w, alog_v, dtb_v, dq, dk, dv, dbg, tm, T):
    C = 3 * D_KEY_B
    nt = T // tm
    hb = tm // HALO

    def body(x_ref, xh_ref, ba_ref, w_ref, al_ref, dt_ref, dq_ref, dk_ref, dv_ref, dbg_ref,
             dx_ref, dba_ref, dw_ref, dal_ref, ddt_ref, nxt_ref):
        s = pl.program_id(0)
        i = nt - 1 - s

        @pl.when(s == 0)
        def _():
            nxt_ref[...] = jnp.zeros_like(nxt_ref)
            dw_ref[...] = jnp.zeros_like(dw_ref)
            dal_ref[...] = jnp.zeros_like(dal_ref)
            ddt_ref[...] = jnp.zeros_like(ddt_ref)

        w = _taps(w_ref)
        halo = jnp.where(i > 0, xh_ref[...], 0.0)
        ext = jnp.concatenate([halo, x_ref[...]], axis=0)
        c = _conv_taps(ext, w)
        _, vjp = jax.vjp(_f_dnpoint, c, ba_ref[...], al_ref[...], dt_ref[...])
        dc, dba, dal, ddt = vjp((dq_ref[...], dk_ref[...], dv_ref[...], dbg_ref[...]))
        dba_ref[...] = dba
        dal_ref[...] += dal
        ddt_ref[...] += ddt
        ext2 = jnp.concatenate([dc, nxt_ref[...]], axis=0)
        dx = dc * w[CONV_WIDTH - 1]
        for m in range(1, CONV_WIDTH):
            dx = dx + pltpu.roll(ext2, tm + HALO - m, 0)[:tm, :] * w[CONV_WIDTH - 1 - m]
        dx_ref[...] = dx
        nxt_ref[...] = dc[:HALO, :]
        for j in range(CONV_WIDTH):
            k = CONV_WIDTH - 1 - j
            xs = ext[HALO:, :] if k == 0 else pltpu.roll(ext, k, 0)[HALO:, :]
            dw_ref[j:j + 1, :] += jnp.sum(xs * dc, axis=0, keepdims=True)

    def rev(w):
        return pl.BlockSpec((tm, w), lambda s: (nt - 1 - s, 0))

    return pl.pallas_call(
        body, name=name, grid=(nt,),
        in_specs=[rev(C), pl.BlockSpec((HALO, C), lambda s: (jnp.maximum((nt - 1 - s) * hb - 1, 0), 0)), rev(128),
                  _full_spec(conv_w.shape), _full_spec((1, 128)), _full_spec((1, 128)),
                  rev(D_KEY_B), rev(D_KEY_B), rev(D_KEY_B), rev(128)],
        out_specs=[rev(C), rev(128), _full_spec((8, C)), _full_spec((1, 128)), _full_spec((1, 128))],
        out_shape=[jax.ShapeDtypeStruct((T, C), F32), jax.ShapeDtypeStruct((T, 128), F32),
                   jax.ShapeDtypeStruct((8, C), F32), jax.ShapeDtypeStruct((1, 128), F32),
                   jax.ShapeDtypeStruct((1, 128), F32)],
        scratch_shapes=[pltpu.VMEM((HALO, C), F32)],
        compiler_params=_ARB,
    )(proj, proj, ba, conv_w, alog_v, dtb_v, dq, dk, dv, dbg)


def _bdot(a, b, dims, exact=False):
    if exact:
        return lax.dot_general(a, b, dims, preferred_element_type=F32, precision=lax.Precision.HIGH)
    return lax.dot_general(a.astype(BF16), b.astype(BF16), dims, preferred_element_type=F32)


_B_NN = (((2,), (1,)), ((0,), (0,)))
_B_NT = (((2,), (2,)), ((0,), (0,)))
_B_TN = (((1,), (1,)), ((0,), (0,)))


def _unit_lower_inverse(a):
    C = a.shape[-1]
    r = lax.broadcasted_iota(jnp.int32, (C, C), 0)
    c = lax.broadcasted_iota(jnp.int32, (C, C), 1)
    pw = -a
    t = jnp.where((r == c)[None], 1.0, 0.0) + pw
    for _ in range(int(math.log2(C)) - 1):
        pw = _bdot(pw, pw, _B_NN, exact=True)
        t = t + _bdot(t, pw, _B_NN, exact=True)
    return t


@jax.custom_vjp
def _solve_given_inverse(t, a, rhs):
    return _bdot(t, rhs, _B_NN, exact=True)


def _solve_fwd(t, a, rhs):
    sol = _bdot(t, rhs, _B_NN, exact=True)
    return sol, (t, sol)


def _solve_bwd(res, dsol):
    t, sol = res
    drhs = _bdot(t, dsol, _B_TN, exact=True)
    return jnp.zeros_like(t), -_bdot(drhs, sol, _B_NT, exact=True), drhs


_solve_given_inverse.defvjp(_solve_fwd, _solve_bwd)


def _stack(xs):
    return jnp.concatenate([x[None] for x in xs], axis=0)


def _delta_chunk(S, q, k, v, bg, t_inv=None):
    H, C = N_HEADS_B, CHUNK
    r = lax.broadcasted_iota(jnp.int32, (C, C), 0)
    c = lax.broadcasted_iota(jnp.int32, (C, C), 1)
    incl, strict, eye = r >= c, r > c, r == c
    gc_all = jnp.dot(incl.astype(F32), bg, preferred_element_type=F32, precision=lax.Precision.HIGHEST)
    lane = lax.broadcasted_iota(jnp.int32, (C, 128), 1)

    def col(x, l):
        return jnp.sum(jnp.where(lane == l, x, 0.0), axis=1, keepdims=True)

    beta = _stack([col(bg, h) for h in range(H)])
    gc = _stack([col(gc_all, H + h) for h in range(H)])
    gc_row = jnp.sum(jnp.where(eye[None], jnp.broadcast_to(gc, (H, C, C)), 0.0), axis=1, keepdims=True)
    row = lax.broadcasted_iota(jnp.int32, (H, C, 1), 1)
    gc_last = jnp.sum(jnp.where(row == C - 1, gc, 0.0), axis=1, keepdims=True)
    decay = jnp.exp(jnp.where(incl[None], gc - gc_row, -jnp.inf))
    qh = _stack(_heads(q, H))
    kh = _stack(_heads(k, H))
    vh = _stack(_heads(v, H))
    kk = _bdot(kh, kh, _B_NT, exact=True)
    a = jnp.where(strict[None], beta * kk * decay, 0.0)
    egc = jnp.exp(gc)
    rhs = jnp.concatenate([beta * vh, (beta * egc) * kh], axis=-1)
    if t_inv is None:
        t_inv = _unit_lower_inverse(a)
        sol = _bdot(t_inv, rhs, _B_NN, exact=True)
    else:
        sol = _solve_given_inverse(t_inv, a, rhs)
    u_bar, w = sol[..., :HEAD_DIM], sol[..., HEAD_DIM:]
    qk = _bdot(qh, kh, _B_NT) * decay
    q_dec = qh * egc
    k_dec = kh * jnp.exp(gc_last - gc)
    u = u_bar - _bdot(w, S, _B_NN)
    o = _bdot(q_dec, S, _B_NN) + _bdot(qk, u, _B_NN)
    S_new = jnp.exp(gc_last) * S + _bdot(k_dec, u, _B_TN)
    o2 = jnp.concatenate([lax.index_in_dim(o, h, 0, keepdims=False) for h in range(H)], axis=1)
    return o2, S_new, t_inv


def _delta_fwd(name, q, k, v, bg, T):
    nc = T // CHUNK
    H = N_HEADS_B

    def body(q_ref, k_ref, v_ref, bg_ref, o_ref, s_out_ref, t_out_ref, s_ref):
        @pl.when(pl.program_id(0) == 0)
        def _():
            s_ref[...] = jnp.zeros_like(s_ref)

        S = s_ref[...]
        s_out_ref[0] = S
        o, S_new, t_inv = _delta_chunk(S, q_ref[...], k_ref[...], v_ref[...], bg_ref[...])
        o_ref[...] = o
        t_out_ref[0] = t_inv
        s_ref[...] = S_new

    return pl.pallas_call(
        body, name=name, grid=(nc,),
        in_specs=[_row_spec(CHUNK, D_KEY_B, 0)] * 3 + [_row_spec(CHUNK, 128, 0)],
        out_specs=[_row_spec(CHUNK, D_KEY_B, 0), pl.BlockSpec((1, H, HEAD_DIM, HEAD_DIM), lambda i: (i, 0, 0, 0)),
                   pl.BlockSpec((1, H, CHUNK, CHUNK), lambda i: (i, 0, 0, 0))],
        out_shape=[jax.ShapeDtypeStruct((T, D_KEY_B), F32), jax.ShapeDtypeStruct((nc, H, HEAD_DIM, HEAD_DIM), F32),
                   jax.ShapeDtypeStruct((nc, H, CHUNK, CHUNK), F32)],
        scratch_shapes=[pltpu.VMEM((H, HEAD_DIM, HEAD_DIM), F32)],
        compiler_params=_ARB,
    )(q, k, v, bg)


def _delta_bwd(name, q, k, v, bg, s_all, t_all, do, T):
    nc = T // CHUNK
    H = N_HEADS_B

    def body(q_ref, k_ref, v_ref, bg_ref, s_in_ref, t_ref, do_ref, dq_ref, dk_ref, dv_ref, dbg_ref, ds_ref):
        @pl.when(pl.program_id(0) == 0)
        def _():
            ds_ref[...] = jnp.zeros_like(ds_ref)

        t_inv = t_ref[0]

        def f(S, qq, kk, vv, bb):
            return _delta_chunk(S, qq, kk, vv, bb, t_inv)[:2]

        _, vjp = jax.vjp(f, s_in_ref[0], q_ref[...], k_ref[...], v_ref[...], bg_ref[...])
        dS, dq, dk, dv, dbg = vjp((do_ref[...], ds_ref[...]))
        dq_ref[...] = dq
        dk_ref[...] = dk
        dv_ref[...] = dv
        dbg_ref[...] = dbg
        ds_ref[...] = dS

    def rev(w):
        return pl.BlockSpec((CHUNK, w), lambda s: (nc - 1 - s, 0))

    def rev4(n):
        return pl.BlockSpec((1, H, n, n), lambda s: (nc - 1 - s, 0, 0, 0))

    return pl.pallas_call(
        body, name=name, grid=(nc,),
        in_specs=[rev(D_KEY_B)] * 3 + [rev(128), rev4(HEAD_DIM), rev4(CHUNK), rev(D_KEY_B)],
        out_specs=[rev(D_KEY_B)] * 3 + [rev(128)],
        out_shape=[jax.ShapeDtypeStruct((T, D_KEY_B), F32)] * 3 + [jax.ShapeDtypeStruct((T, 128), F32)],
        scratch_shapes=[pltpu.VMEM((H, HEAD_DIM, HEAD_DIM), F32)],
        compiler_params=_ARB,
    )(q, k, v, bg, s_all, t_all, do)


def _loss_fb(name, h2, w, tgt, tm, T):
    def tile_loss(h, wv, t):
        y = _f_rms(h, wv)[0]
        e = y - t
        return 0.5 * jnp.sum(jnp.mean(e * e, axis=-1))

    def body(h_ref, w_ref, t_ref, dh_ref, loss_ref, dw_ref):
        @pl.when(pl.program_id(0) == 0)
        def _():
            loss_ref[...] = jnp.zeros_like(loss_ref)
            dw_ref[...] = jnp.zeros_like(dw_ref)

        val, (dh, dw) = jax.value_and_grad(tile_loss, argnums=(0, 1))(h_ref[...], w_ref[...], t_ref[...])
        dh_ref[...] = dh
        dw_ref[...] += dw
        loss_ref[...] += jnp.reshape(val, (1, 1))

    return pl.pallas_call(
        body, name=name, grid=(T // tm,),
        in_specs=[_row_spec(tm, D_MODEL, 0), _full_spec((1, D_MODEL)), _row_spec(tm, D_MODEL, 0)],
        out_specs=[_row_spec(tm, D_MODEL, 0), _full_spec((1, 1)), _full_spec((1, D_MODEL))],
        out_shape=[jax.ShapeDtypeStruct((T, D_MODEL), F32), jax.ShapeDtypeStruct((1, 1), F32),
                   jax.ShapeDtypeStruct((1, D_MODEL), F32)],
        compiler_params=_ARB,
    )(h2, w, tgt)


def _adamw(name, w, g, m, v, tile):
    R = w.shape[0]

    def body(w_ref, g_ref, m_ref, v_ref, d_out, m_out, v_out):
        g = g_ref[...]
        mm = ADAM_B1 * m_ref[...] + (1.0 - ADAM_B1) * g
        vv = ADAM_B2 * v_ref[...] + (1.0 - ADAM_B2) * jnp.square(g)
        m_hat = mm / (1.0 - ADAM_B1 ** ADAM_STEP)
        v_hat = vv / (1.0 - ADAM_B2 ** ADAM_STEP)
        d_out[...] = -ADAM_LR * (m_hat / (jnp.sqrt(v_hat) + ADAM_EPS) + ADAM_WD * w_ref[...])
        m_out[...] = mm
        v_out[...] = vv

    assert R % tile == 0
    spec = pl.BlockSpec((tile, w.shape[1]), lambda i: (i, 0))
    return pl.pallas_call(
        body, name=name, grid=(R // tile,), in_specs=[spec] * 4, out_specs=[spec] * 3,
        out_shape=[jax.ShapeDtypeStruct(w.shape, F32)] * 3, compiler_params=_ARB,
    )(w, g, m, v)


def _sum4(name, r):
    R = r.shape[1]

    def body(r_ref, o_ref):
        o_ref[...] = ((r_ref[0].astype(F32) + r_ref[1].astype(F32)) + r_ref[2].astype(F32)) + r_ref[3].astype(F32)

    return pl.pallas_call(
        body, name=name, grid=(R // HALF_TILE,),
        in_specs=[pl.BlockSpec((4, HALF_TILE, D_MODEL), lambda i: (0, i, 0))],
        out_specs=pl.BlockSpec((HALF_TILE, D_MODEL), lambda i: (i, 0)),
        out_shape=jax.ShapeDtypeStruct((R, D_MODEL), F32), compiler_params=_ARB,
    )(r)


def _add2(name, a, b):
    R = a.shape[0]

    def body(a_ref, b_ref, o_ref):
        o_ref[...] = (a_ref[...].astype(F32) + b_ref[...].astype(F32)).astype(BF16)

    spec = pl.BlockSpec((HALF_TILE, D_MODEL), lambda i: (i, 0))
    return pl.pallas_call(
        body, name=name, grid=(R // HALF_TILE,), in_specs=[spec, spec], out_specs=spec,
        out_shape=jax.ShapeDtypeStruct((R, D_MODEL), BF16), compiler_params=_ARB,
    )(a, b)


_ANY = pl.BlockSpec(memory_space=pl.ANY)


def _place():
    return lax.axis_index("x"), lax.axis_index("y"), lax.axis_index("c")


def _gather_chips(name, shard):
    def body(s_ref, out_ref, send_sems, recv_sems, local_sem):
        x, y, c = _place()
        me = 2 * x + y
        chips = [(1 - x, y), (x, 1 - y), (1 - x, 1 - y)]
        mine = pltpu.make_async_copy(s_ref, out_ref.at[me], local_sem)
        mine.start()

        def copy(k, slot, px, py):
            return pltpu.make_async_remote_copy(src_ref=s_ref, dst_ref=out_ref.at[slot], send_sem=send_sems.at[k],
                                                recv_sem=recv_sems.at[k], device_id=(px, py, c), device_id_type=MESH)

        sends = [copy(k, me, px, py) for k, (px, py) in enumerate(chips)]
        for cp in sends:
            cp.start()
        for k, (px, py) in enumerate(chips):
            copy(k, 2 * px + py, px, py).wait_recv()
        for cp in sends:
            cp.wait_send()
        mine.wait()

    return pl.pallas_call(
        body, name=name, in_specs=[_ANY], out_specs=_ANY,
        out_shape=jax.ShapeDtypeStruct((4,) + shard.shape, shard.dtype),
        scratch_shapes=[pltpu.SemaphoreType.DMA((3,)), pltpu.SemaphoreType.DMA((3,)), pltpu.SemaphoreType.DMA],
    )(shard)


def _scatter_chips(name, g):
    def body(g_ref, out_ref, send_sems, recv_sems, local_sem):
        x, y, c = _place()
        me = 2 * x + y
        chips = [(1 - x, y), (x, 1 - y), (1 - x, 1 - y)]
        mine = pltpu.make_async_copy(g_ref.at[me], out_ref.at[me], local_sem)
        mine.start()

        def copy(k, src_slot, dst_slot, px, py):
            return pltpu.make_async_remote_copy(src_ref=g_ref.at[src_slot], dst_ref=out_ref.at[dst_slot],
                                                send_sem=send_sems.at[k], recv_sem=recv_sems.at[k],
                                                device_id=(px, py, c), device_id_type=MESH)

        sends = [copy(k, 2 * px + py, me, px, py) for k, (px, py) in enumerate(chips)]
        for cp in sends:
            cp.start()
        for k, (px, py) in enumerate(chips):
            copy(k, me, 2 * px + py, px, py).wait_recv()
        for cp in sends:
            cp.wait_send()
        mine.wait()

    return pl.pallas_call(
        body, name=name, in_specs=[_ANY], out_specs=_ANY,
        out_shape=jax.ShapeDtypeStruct(g.shape, g.dtype),
        scratch_shapes=[pltpu.SemaphoreType.DMA((3,)), pltpu.SemaphoreType.DMA((3,)), pltpu.SemaphoreType.DMA],
    )(g)


def _swap_sibling(name, p):
    def body(p_ref, out_ref, send_sem, recv_sem):
        x, y, c = _place()
        cp = pltpu.make_async_remote_copy(src_ref=p_ref, dst_ref=out_ref, send_sem=send_sem, recv_sem=recv_sem,
                                          device_id=(x, y, 1 - c), device_id_type=MESH)
        cp.start()
        cp.wait()

    return pl.pallas_call(
        body, name=name, in_specs=[_ANY], out_specs=_ANY,
        out_shape=jax.ShapeDtypeStruct(p.shape, p.dtype),
        scratch_shapes=[pltpu.SemaphoreType.DMA, pltpu.SemaphoreType.DMA],
    )(p)


def _allsum_small(name, pack):
    def body(p_ref, out_ref, slots, send_sems, recv_sems):
        x, y, c = _place()
        me = 4 * x + 2 * y + c
        slots[me] = p_ref[...]
        sends = []
        for k in range(1, 8):
            px = 1 - x if k & 4 else x
            py = 1 - y if k & 2 else y
            pc = 1 - c if k & 1 else c
            cp = pltpu.make_async_remote_copy(src_ref=p_ref, dst_ref=slots.at[me], send_sem=send_sems.at[k - 1],
                                              recv_sem=recv_sems.at[k - 1], device_id=(px, py, pc), device_id_type=MESH)
            cp.start()
            sends.append(cp)
        for k in range(1, 8):
            px = 1 - x if k & 4 else x
            py = 1 - y if k & 2 else y
            pc = 1 - c if k & 1 else c
            pltpu.make_async_remote_copy(src_ref=p_ref, dst_ref=slots.at[4 * px + 2 * py + pc],
                                         send_sem=send_sems.at[k - 1], recv_sem=recv_sems.at[k - 1],
                                         device_id=(px, py, pc), device_id_type=MESH).wait_recv()
        for cp in sends:
            cp.wait_send()
        acc = slots[0]
        for s in range(1, 8):
            acc = acc + slots[s]
        out_ref[...] = acc

    vm = pl.BlockSpec(memory_space=pltpu.VMEM)
    return pl.pallas_call(
        body, name=name, in_specs=[vm], out_specs=vm,
        out_shape=jax.ShapeDtypeStruct(pack.shape, pack.dtype),
        scratch_shapes=[pltpu.VMEM((8,) + pack.shape, pack.dtype), pltpu.SemaphoreType.DMA((7,)),
                        pltpu.SemaphoreType.DMA((7,))],
    )(pack)


def _pack_shards(parts, rows_total):
    flat = [p.reshape(-1, D_MODEL) for p in parts]
    used = sum(f.shape[0] for f in flat)
    return jnp.concatenate(flat + [jnp.zeros((rows_total - used, D_MODEL), flat[0].dtype)], axis=0)


def _unpack_shards(buf, shapes):
    out, r0 = [], 0
    for shp in shapes:
        n = math.prod(shp) // D_MODEL
        out.append(buf[r0:r0 + n].reshape(shp))
        r0 += n
    return out


def kernel(x, norm_mix, w_in, conv_w, a_log, dt_bias, dn_norm, w_proj_attn, w_proj_delta, w_out, norm_ffn, w_gate, w_up, w_down, norm_final, loss_target, m_norm_mix, m_w_in, m_conv_w, m_a_log, m_dt_bias, m_dn_norm, m_w_proj_attn, m_w_proj_delta, m_w_out, m_norm_ffn, m_w_gate, m_w_up, m_w_down, m_norm_final, v_norm_mix, v_w_in, v_conv_w, v_a_log, v_dt_bias, v_dn_norm, v_w_proj_attn, v_w_proj_delta, v_w_out, v_norm_ffn, v_w_gate, v_w_up, v_w_down, v_norm_final):
    T = x.shape[1]
    D = D_MODEL
    tm = 256
    mt = 1024 if T % 1024 == 0 else 512
    x2 = x.reshape(T, D)
    tgt = loss_target.reshape(T, D)

    sharded = (w_in, w_proj_attn, w_proj_delta, w_out, w_gate, w_up, w_down)
    conv_bits = lax.bitcast_convert_type(conv_w[0], BF16)
    w16 = _pack_shards([w[0].astype(BF16) for w in sharded] + [conv_bits], PACK_R16)
    my_c = lax.axis_index("c")
    w_mine = _gather_chips("gather_weights", lax.dynamic_slice_in_dim(w16, my_c * HALF, HALF, axis=0))
    w_other = _swap_sibling("swap_weights", w_mine)
    wg = jnp.concatenate([jnp.where(my_c == 0, w_mine, w_other), jnp.where(my_c == 0, w_other, w_mine)], axis=1)

    def seg(r0, n):
        return wg[:, r0:r0 + n, :]

    r = 0
    Win = seg(r, 2692).reshape(4, D, 2692).transpose(1, 0, 2).reshape(D, D_IN); r += 2692
    Wpa = seg(r, 128).reshape(4, D_ATTN_OUT, 256).transpose(1, 0, 2).reshape(D_ATTN_OUT, D); r += 128
    Wpd = seg(r, 256).reshape(D, D); r += 256
    Wout = seg(r, 256).reshape(D, D); r += 256
    Wg = seg(r, 704).reshape(4, D, 704).transpose(1, 0, 2).reshape(D, D_FF); r += 704
    Wu = seg(r, 704).reshape(4, D, 704).transpose(1, 0, 2).reshape(D, D_FF); r += 704
    Wd = seg(r, 704).reshape(D_FF, D); r += 704
    convw = lax.bitcast_convert_type(seg(r, 6).reshape(4, CONV_WIDTH, 768, 2), F32)
    convw = convw.transpose(1, 0, 2).reshape(CONV_WIDTH, 3 * D_KEY_B)
    Wmain = jnp.concatenate([Win[:, 4608:8704], Win[:, 8720:], Win[:, :4608]], axis=1)
    Wba = jnp.pad(Win[:, 8704:8720], ((0, 0), (0, 112)))
    Wgu = jnp.concatenate([Wg, Wu], axis=1)
    alog_v = jnp.pad(a_log, ((0, 0), (8, 112)))
    dtb_v = jnp.pad(dt_bias, ((0, 0), (8, 112)))

    (u,) = _rowwise("rms_mix", _f_rms, [(x2, 0, D)], [norm_mix], [(D, BF16)], tm, T)
    proj = _matmul("mm_in", u, Wmain, "nn", T, D_MAIN, D, mt, 1536, D, F32)
    ba = _matmul("mm_ba", u, Wba, "nn", T, 128, D, mt, 128, D, F32)

    att = [_attn_fwd(f"attn_fwd{g}", proj, g, T) for g in range(3)]
    (ya_m,) = _rowwise("merge", _f_merge, [(att[g][0], 0, 512) for g in range(3)] + [(att[g][1], 0, 512) for g in range(3)],
                       [], [(512, BF16)], tm, T)
    qn, kn, vd, bg = _dn_pre_fwd("dn_pre_fwd", proj, ba, convw, alog_v, dtb_v, tm, T)
    o_d, s_all, t_all = _delta_fwd("delta_fwd", qn, kn, vd, bg, T)
    (od,) = _rowwise("post_dn", _f_postdn, [(o_d, 0, D), (proj, C_Z // D, D)], [dn_norm], [(D, BF16)], tm, T)
    y_a = _matmul("mm_pa", ya_m, Wpa, "nn", T, D, 512, mt, D, 512, F32)
    y_b = _matmul("mm_pd", od, Wpd, "nn", T, D, D, mt, D, D, F32)
    (mix,) = _rowwise("mix", _f_mix, [(y_a, 0, D), (y_b, 0, D), (proj, C_GATE // 2048, 2048)], [], [(D, BF16)], tm, T)
    h1 = _matmul("mm_out", mix, Wout, "nn", T, D, D, mt, D, D, F32, residual=x2)
    (hn,) = _rowwise("rms_ffn", _f_rms, [(h1, 0, D)], [norm_ffn], [(D, BF16)], tm, T)
    gu = _matmul("mm_gu", hn, Wgu, "nn", T, 2 * D_FF, D, mt, 1408, D, F32)
    (act,) = _rowwise("act", _f_act, [(gu, 0, 2 * D_FF)], [], [(D_FF, BF16)], tm, T)
    h2 = _matmul("mm_down", act, Wd, "nn", T, D, D_FF, mt, D, 1408, F32, residual=h1)
    dh2, loss, g_norm_final = _loss_fb("loss", h2, norm_final.reshape(1, D), tgt, tm, T)

    dact = _matmul("mm_dact", dh2, Wd, "nt", T, D_FF, D, mt, 1408, D, F32)
    gWd = _matmul("mm_gwd", act, dh2, "tn", D_FF, D, T, 1408, D, mt, BF16)
    (dgu,) = _rowwise_vjp("act_bwd", _f_act, [(gu, 0, 2 * D_FF)], [], [(dact, 0, D_FF)], [BF16], tm, T)
    dhn = _matmul("mm_dhn", dgu, Wgu, "nt", T, D, 2 * D_FF, mt, D, 1408, F32)
    gWgu = _matmul("mm_gwgu", hn, dgu, "tn", D, 2 * D_FF, T, D, 1408, mt, BF16)
    dh1, g_norm_ffn = _rowwise_vjp("rms_ffn_bwd", _f_rms, [(h1, 0, D)], [norm_ffn], [(dhn, 0, D)], [F32], tm, T,
                                   residual=(dh2, 0, D))
    dmix = _matmul("mm_dmix", dh1, Wout, "nt", T, D, D, mt, D, D, F32)
    gWout = _matmul("mm_gwout", mix, dh1, "tn", D, D, T, D, D, mt, BF16)
    dya, dyb, dgate = _rowwise_vjp("mix_bwd", _f_mix, [(y_a, 0, D), (y_b, 0, D), (proj, C_GATE // 2048, 2048)], [],
                                   [(dmix, 0, D)], [BF16, BF16, BF16], tm, T)
    dya_m = _matmul("mm_dyam", dya, Wpa, "nt", T, 512, D, mt, 512, D, F32)
    gWpa = _matmul("mm_gwpa", ya_m, dya, "tn", 512, D, T, 512, D, mt, BF16)
    dod = _matmul("mm_dod", dyb, Wpd, "nt", T, D, D, mt, D, D, F32)
    gWpd = _matmul("mm_gwpd", od, dyb, "tn", D, D, T, D, D, mt, BF16)
    do_d, dz, g_dn_norm = _rowwise_vjp("post_dn_bwd", _f_postdn, [(o_d, 0, D), (proj, C_Z // D, D)], [dn_norm],
                                       [(dod, 0, D)], [F32, BF16], tm, T)
    dqn, dkn, dvd, dbg = _delta_bwd("delta_bwd", qn, kn, vd, bg, s_all, t_all, do_d, T)
    dqkvd, dba, g_conv8, g_alog_v, g_dtb_v = _dn_pre_bwd("dn_pre_bwd", proj, ba, convw, alog_v, dtb_v,
                                                         dqn, dkn, dvd, dbg, tm, T)
    mg = _rowwise_vjp("merge_bwd", _f_merge,
                      [(att[g][0], 0, 512) for g in range(3)] + [(att[g][1], 0, 512) for g in range(3)], [],
                      [(dya_m, 0, 512)], [F32] * 6, tm, T)
    datt = [_attn_bwd(f"attn_bwd{g}", proj, mg[g], mg[3 + g], g, T) for g in range(3)]
    dproj = jnp.concatenate([dqkvd.astype(BF16), dz, dgate]
                            + [datt[g][0].astype(BF16) for g in range(3)]
                            + [datt[g][1].astype(BF16) for g in range(3)]
                            + [datt[g][2].astype(BF16) for g in range(3)], axis=1)
    gWmain = _matmul("mm_gwmain", u, dproj, "tn", D, D_MAIN, T, D, 1536, mt, BF16)
    gWba = _matmul("mm_gwba", u, dba, "tn", D, 128, T, D, 128, mt, BF16)
    du0 = _matmul("mm_du_ba", dba, Wba, "nt", T, D, 128, mt, D, 128, F32)
    du = _matmul("mm_du", dproj, Wmain, "nt", T, D, D_MAIN, mt, D, 1536, F32, residual=du0)
    dx, g_norm_mix = _rowwise_vjp("rms_mix_bwd", _f_rms, [(x2, 0, D)], [norm_mix], [(du, 0, D)], [F32], tm, T,
                                  residual=(dh1, 0, D))

    gWin = jnp.concatenate([gWmain[:, C_QA:], gWmain[:, :C_GATE], gWba[:, :16], gWmain[:, C_GATE:C_QA]], axis=1)

    def cols4(g, n):
        return g.reshape(g.shape[0], 4, n).transpose(1, 0, 2).reshape(4, -1, D)

    parts = [cols4(gWin, 2692), cols4(gWpa, 256), gWpd.reshape(4, 256, D), gWout.reshape(4, 256, D),
             cols4(gWgu[:, :D_FF], 704), cols4(gWgu[:, D_FF:], 704), gWd.reshape(4, 704, D), cols4(g_conv8[:CONV_WIDTH], 768)]
    used = sum(p.shape[1] for p in parts)
    g_pack = jnp.concatenate([p.astype(BF16) for p in parts] + [jnp.zeros((4, PACK_R - used, D), BF16)], axis=1)
    g_keep = lax.dynamic_slice_in_dim(g_pack, my_c * HALF, HALF, axis=1)
    g_give = lax.dynamic_slice_in_dim(g_pack, (1 - my_c) * HALF, HALF, axis=1)
    g_got = _swap_sibling("swap_grads", g_give)
    g_pair = _add2("add_cores", g_keep.reshape(4 * HALF, D), g_got.reshape(4 * HALF, D)).reshape(4, HALF, D)
    recv = _scatter_chips("scatter_grads", g_pair)
    g_mine = _sum4("sum_chips", recv)
    g_other = _swap_sibling("swap_sums", g_mine)
    g_all = jnp.concatenate([jnp.where(my_c == 0, g_mine, g_other), jnp.where(my_c == 0, g_other, g_mine)], axis=0)

    locals_ = (w_in, w_proj_attn, w_proj_delta, w_out, w_gate, w_up, w_down, conv_w)
    ms = (m_w_in, m_w_proj_attn, m_w_proj_delta, m_w_out, m_w_gate, m_w_up, m_w_down, m_conv_w)
    vs = (v_w_in, v_w_proj_attn, v_w_proj_delta, v_w_out, v_w_gate, v_w_up, v_w_down, v_conv_w)
    big_names = ("w_in", "w_proj_attn", "w_proj_delta", "w_out", "w_gate", "w_up", "w_down", "conv_w")
    adam_tiles = (128, 512, 256, 256, 256, 256, 352, CONV_WIDTH)
    g_shards = _unpack_shards(g_all, [w.shape for w in locals_])
    big = [g_shards, [], [], []]
    for nm, w, g, m_, v_, tl in zip(big_names, locals_, g_shards, ms, vs, adam_tiles):
        res = _adamw("adamw_" + nm, w[0], g[0], m_[0], v_[0], tl)
        for kind in range(3):
            big[kind + 1].append(res[kind].reshape(w.shape))

    def small_pack(nm, nf, nfin, dn, al, dt, extra):
        row3 = jnp.concatenate([dn.reshape(1, 128), al.reshape(1, 8), dt.reshape(1, 8), extra.reshape(1, 1),
                                jnp.zeros((1, D - 145), F32)], axis=1)
        return jnp.concatenate([nm.reshape(1, D), nf.reshape(1, D), nfin.reshape(1, D), row3, jnp.zeros((4, D), F32)], axis=0)

    zero1 = jnp.zeros((1, 1), F32)
    sp = small_pack(g_norm_mix, g_norm_ffn, g_norm_final, g_dn_norm, g_alog_v[:, 8:16], g_dtb_v[:, 8:16], loss)
    tot = _allsum_small("allsum_small", sp)
    sm = _adamw("adamw_small", small_pack(norm_mix, norm_ffn, norm_final, dn_norm, a_log, dt_bias, zero1), tot,
                small_pack(m_norm_mix, m_norm_ffn, m_norm_final, m_dn_norm, m_a_log, m_dt_bias, zero1),
                small_pack(v_norm_mix, v_norm_ffn, v_norm_final, v_dn_norm, v_a_log, v_dt_bias, zero1), 8)
    sm = [tot] + list(sm)

    def small_unpack(p):
        return {"norm_mix": p[0:1], "norm_ffn": p[1:2], "norm_final": p[2], "dn_norm": p[3:4, :128],
                "a_log": p[3:4, 128:136], "dt_bias": p[3:4, 136:144]}

    sm = [small_unpack(p) for p in sm]
    loss_out = tot[3, 144]

    order = ("norm_mix", "w_in", "conv_w", "a_log", "dt_bias", "dn_norm", "w_proj_attn", "w_proj_delta", "w_out",
             "norm_ffn", "w_gate", "w_up", "w_down", "norm_final")
    big_names = ("w_in", "w_proj_attn", "w_proj_delta", "w_out", "w_gate", "w_up", "w_down", "conv_w")
    outs = [loss_out, dx.reshape(x.shape)]
    for kind in range(4):
        for n in order:
            outs.append(big[kind][big_names.index(n)] if n in big_names else sm[kind][n])
    return tuple(outs)
```

```python
import functools
import math

import jax
import jax.numpy as jnp
from jax import lax
from jax.experimental import pallas as pl
from jax.experimental.pallas import tpu as pltpu

F32 = jnp.float32
BF16 = jnp.bfloat16
MESH = pl.DeviceIdType.MESH

D_MODEL = 1024
N_HEADS_A = 12
HEAD_DIM = 128
BLOCK_A = 128
DILATIONS = (1, 4, 16)
W_SUB = 128
D_ATTN_OUT = 512
N_HEADS_B = 8
D_KEY_B = 1024
CONV_WIDTH = 4
CHUNK = 64
D_FF = 2816
EPS = 1e-6
D_IN = 10768
D_PA = 4608
D_PB = 6144
C_Z, C_GATE = 3072, 4096
C_QA, C_KA, C_VA = 0, 1536, 3072

ADAM_LR, ADAM_B1, ADAM_B2, ADAM_EPS, ADAM_WD, ADAM_STEP = 0.001, 0.9, 0.999, 1e-08, 0.01, 10

PACK_R = 5632
PACK_R16 = 5632
HALF = PACK_R // 2
HALF_TILE = 256

VMEM_LIMIT = 48 * 1024 * 1024


def _cparams(n):
    return pltpu.CompilerParams(dimension_semantics=("arbitrary",) * n, vmem_limit_bytes=VMEM_LIMIT)


_ARB = _cparams(1)


def _row_spec(tm, w, cb):
    return pl.BlockSpec((tm, w), lambda i: (i, cb))


def _full_spec(shape):
    nd = len(shape)
    return pl.BlockSpec(shape, lambda i: (0,) * nd)


def _rowwise(name, fn, ins, params, outs, tm, T):
    n_in, n_p = len(ins), len(params)

    def body(*refs):
        xs = [r[...].astype(F32) for r in refs[:n_in]] + [r[...] for r in refs[n_in:n_in + n_p]]
        res = fn(*xs)
        for o_ref, v in zip(refs[n_in + n_p:], res):
            o_ref[...] = v.astype(o_ref.dtype)

    return pl.pallas_call(
        body, name=name, grid=(T // tm,),
        in_specs=[_row_spec(tm, w, cb) for (_, cb, w) in ins] + [_full_spec(p.shape) for p in params],
        out_specs=[_row_spec(tm, w, 0) for (w, _) in outs],
        out_shape=[jax.ShapeDtypeStruct((T, w), dt) for (w, dt) in outs],
        compiler_params=_ARB,
    )(*[a for (a, _, _) in ins], *params)


def _rowwise_vjp(name, fn, ins, params, cts, gdtypes, tm, T, residual=None):
    n_in, n_p, n_ct = len(ins), len(params), len(cts)
    n_res = 0 if residual is None else 1

    def body(*refs):
        i = pl.program_id(0)
        k = 0
        xs = [r[...].astype(F32) for r in refs[k:k + n_in]]
        k += n_in
        ps = [r[...] for r in refs[k:k + n_p]]
        k += n_p
        gs = tuple(r[...].astype(F32) for r in refs[k:k + n_ct])
        k += n_ct
        res_refs = refs[k:k + n_res]
        k += n_res
        gi_refs = refs[k:k + n_in]
        gp_refs = refs[k + n_in:]
        _, vjp = jax.vjp(fn, *xs, *ps)
        g = vjp(gs)
        for j, (ref, v) in enumerate(zip(gi_refs, g[:n_in])):
            if j == 0 and n_res:
                v = v + res_refs[0][...].astype(F32)
            ref[...] = v.astype(ref.dtype)

        @pl.when(i == 0)
        def _():
            for ref in gp_refs:
                ref[...] = jnp.zeros_like(ref)

        for ref, v in zip(gp_refs, g[n_in:]):
            ref[...] += v

    res_in = [] if residual is None else [residual]
    return pl.pallas_call(
        body, name=name, grid=(T // tm,),
        in_specs=([_row_spec(tm, w, cb) for (_, cb, w) in ins] + [_full_spec(p.shape) for p in params]
                  + [_row_spec(tm, w, cb) for (_, cb, w) in cts] + [_row_spec(tm, w, cb) for (_, cb, w) in res_in]),
        out_specs=[_row_spec(tm, w, 0) for (_, _, w) in ins] + [_full_spec(p.shape) for p in params],
        out_shape=([jax.ShapeDtypeStruct((T, w), dt) for (_, _, w), dt in zip(ins, gdtypes)]
                   + [jax.ShapeDtypeStruct(p.shape, F32) for p in params]),
        compiler_params=_ARB,
    )(*[a for (a, _, _) in ins], *params, *[a for (a, _, _) in cts], *[a for (a, _, _) in res_in])


def _matmul(name, a, b, mode, M, N, K, tm, tn, tk, out_dtype, a_off=0, b_off=0, residual=None):
    nk = K // tk
    assert M % tm == 0 and N % tn == 0 and K % tk == 0
    if mode == "nn":
        assert a_off % tk == 0 and b_off % tn == 0
        a_spec = pl.BlockSpec((tm, tk), lambda i, j, k: (i, a_off // tk + k))
        b_spec = pl.BlockSpec((tk, tn), lambda i, j, k: (k, b_off // tn + j))
        dims = (((1,), (0,)), ((), ()))
    elif mode == "nt":
        assert a_off % tk == 0 and b_off == 0
        a_spec = pl.BlockSpec((tm, tk), lambda i, j, k: (i, a_off // tk + k))
        b_spec = pl.BlockSpec((tn, tk), lambda i, j, k: (j, k))
        dims = (((1,), (1,)), ((), ()))
    else:
        assert a_off % tm == 0 and b_off % tn == 0
        a_spec = pl.BlockSpec((tk, tm), lambda i, j, k: (k, a_off // tm + i))
        b_spec = pl.BlockSpec((tk, tn), lambda i, j, k: (k, b_off // tn + j))
        dims = (((0,), (0,)), ((), ()))
    has_res = residual is not None

    def body(*refs):
        a_ref, b_ref = refs[0], refs[1]
        r_ref = refs[2] if has_res else None
        o_ref, acc_ref = refs[-2], refs[-1]
        k = pl.program_id(2)

        @pl.when(k == 0)
        def _():
            acc_ref[...] = jnp.zeros_like(acc_ref)

        acc_ref[...] += lax.dot_general(a_ref[...].astype(BF16), b_ref[...].astype(BF16), dims,
                                        preferred_element_type=F32)

        @pl.when(k == nk - 1)
        def _():
            acc = acc_ref[...]
            if has_res:
                acc = acc + r_ref[...].astype(F32)
            o_ref[...] = acc.astype(o_ref.dtype)

    in_specs = [a_spec, b_spec]
    args = [a, b]
    if has_res:
        in_specs.append(pl.BlockSpec((tm, tn), lambda i, j, k: (i, j)))
        args.append(residual)
    return pl.pallas_call(
        body, name=name, grid=(M // tm, N // tn, nk),
        in_specs=in_specs,
        out_specs=pl.BlockSpec((tm, tn), lambda i, j, k: (i, j)),
        out_shape=jax.ShapeDtypeStruct((M, N), out_dtype),
        scratch_shapes=[pltpu.VMEM((tm, tn), F32)],
        compiler_params=pltpu.CompilerParams(dimension_semantics=("parallel", "parallel", "arbitrary"),
                                             vmem_limit_bytes=VMEM_LIMIT),
    )(*args)


def _f_rms(x, w):
    return (x * lax.rsqrt(jnp.mean(x * x, axis=-1, keepdims=True) + EPS) * w,)


def _silu(x):
    return x * jax.nn.sigmoid(x)


def _softplus(x):
    return jnp.maximum(x, 0.0) + jnp.log1p(jnp.exp(-jnp.abs(x)))


def _heads(x, n):
    return [x[:, HEAD_DIM * h:HEAD_DIM * (h + 1)] for h in range(n)]


def _f_dnpoint(c, ba, alog_v, dtb_v):
    cs = _silu(c)
    q, k, v = cs[:, :D_KEY_B], cs[:, D_KEY_B:2 * D_KEY_B], cs[:, 2 * D_KEY_B:]
    qn = jnp.concatenate([t * lax.rsqrt(jnp.sum(t * t, axis=-1, keepdims=True) + EPS) * (HEAD_DIM ** -0.5)
                          for t in _heads(q, N_HEADS_B)], axis=1)
    kn = jnp.concatenate([t * lax.rsqrt(jnp.sum(t * t, axis=-1, keepdims=True) + EPS)
                          for t in _heads(k, N_HEADS_B)], axis=1)
    lane = lax.broadcasted_iota(jnp.int32, ba.shape, 1)
    beta = jax.nn.sigmoid(ba)
    g = -jnp.exp(alog_v) * _softplus(ba + dtb_v)
    bg = jnp.where(lane < N_HEADS_B, beta, jnp.where(lane < 2 * N_HEADS_B, g, 0.0))
    return qn, kn, v, bg


def _f_postdn(o, z, w):
    outs = []
    for oh, zh in zip(_heads(o, N_HEADS_B), _heads(z, N_HEADS_B)):
        outs.append(oh * lax.rsqrt(jnp.mean(oh * oh, axis=-1, keepdims=True) + EPS) * w * _silu(zh))
    return (jnp.concatenate(outs, axis=1),)


def _f_merge(o0, o1, o2, l0, l1, l2):
    m = lax.stop_gradient(jnp.maximum(jnp.maximum(l0, l1), l2))
    e0, e1, e2 = jnp.exp(l0 - m), jnp.exp(l1 - m), jnp.exp(l2 - m)
    return ((e0 * o0 + e1 * o1 + e2 * o2) / (e0 + e1 + e2),)


def _f_mix(ya, yb, gr):
    return (jax.nn.sigmoid(gr[:, :D_MODEL]) * ya + jax.nn.sigmoid(gr[:, D_MODEL:]) * yb,)


def _f_act(gu):
    return (_silu(gu[:, :D_FF]) * gu[:, D_FF:],)


def _attn_block(q, kp, kc, vp, vc, slope_d, first_key):
    k2 = jnp.concatenate([kp, kc], axis=0).astype(BF16)
    v2 = jnp.concatenate([vp, vc], axis=0).astype(BF16)
    s = lax.dot_general(q.astype(BF16), k2, (((1,), (1,)), ((), ())), preferred_element_type=F32)
    s = s * (HEAD_DIM ** -0.5)
    i = lax.broadcasted_iota(jnp.int32, (BLOCK_A, 2 * BLOCK_A), 0)
    j = lax.broadcasted_iota(jnp.int32, (BLOCK_A, 2 * BLOCK_A), 1)
    delta = BLOCK_A + i - j
    valid = (delta >= 0) & (delta <= W_SUB) & (j >= first_key)
    s = jnp.where(valid, s - slope_d * delta.astype(F32), -jnp.inf)
    m = lax.stop_gradient(jnp.max(s, axis=-1, keepdims=True))
    p = jnp.exp(s - m)
    den = jnp.sum(p, axis=-1, keepdims=True)
    o = jnp.dot(p.astype(BF16), v2, preferred_element_type=F32) / den
    lse = jnp.broadcast_to(m + jnp.log(den), (BLOCK_A, HEAD_DIM))
    return o, lse


def _slope_d(group, h, d):
    hg = (group * 4 + h + 1).astype(F32)
    return jnp.exp(jnp.full((1, 1), -8.0 * math.log(2.0) / N_HEADS_A, F32) * hg) * float(d)


ATT_CFG = ((1, 4, 4), (4, 2, 1), (16, 1, 1))
R_UNROLL = 4


def _attn_geometry(group, T):
    d, m, hp = ATT_CFG[group]
    span = BLOCK_A * d
    tile = m * span
    assert T % tile == 0 and 4 % hp == 0
    return d, m, hp, span, tile, T // tile, HEAD_DIM * hp


def _sub_rows(b, span, r, d):
    return pl.ds(b * span + r, BLOCK_A, stride=d) if d > 1 else pl.ds(b * span, BLOCK_A)


def _for_each_r(d, fn):
    if d <= R_UNROLL:
        for r in range(d):
            fn(r)
    else:
        def step(r, carry):
            fn(r)
            return carry

        lax.fori_loop(0, d, step, 0, unroll=R_UNROLL)


def _attn_fwd(name, proj, group, T):
    d, m, hp, span, tile, nt, lanes = _attn_geometry(group, T)
    qb, kb, vb = [(c + 512 * group) // lanes for c in (C_QA, C_KA, C_VA)]

    def body(q_ref, kc_ref, kp_ref, vc_ref, vp_ref, o_ref, l_ref):
        hb, n = pl.program_id(0), pl.program_id(1)
        first_key = jnp.where(n > 0, 0, BLOCK_A)
        for hh in range(hp):
            ls = slice(HEAD_DIM * hh, HEAD_DIM * (hh + 1))
            sl = _slope_d(group, hb * hp + hh, d)
            for b in range(m):
                def one(r, b=b, ls=ls, sl=sl):
                    rq = _sub_rows(b, span, r, d)
                    if b == 0:
                        rp = _sub_rows(0, span, r, d)
                        kp, vp, fk = kp_ref[rp, ls], vp_ref[rp, ls], first_key
                    else:
                        rp = _sub_rows(b - 1, span, r, d)
                        kp, vp, fk = kc_ref[rp, ls], vc_ref[rp, ls], 0
                    o, lse = _attn_block(q_ref[rq, ls], kp, kc_ref[rq, ls], vp, vc_ref[rq, ls], sl, fk)
                    o_ref[rq, ls] = o
                    l_ref[rq, ls] = lse

                _for_each_r(d, one)

    def cur(cb):
        return pl.BlockSpec((tile, lanes), lambda hb, n: (n, cb + hb))

    def prev(cb):
        return pl.BlockSpec((span, lanes), lambda hb, n: (jnp.maximum(m * n - 1, 0), cb + hb))

    return pl.pallas_call(
        body, name=name, grid=(4 // hp, nt),
        in_specs=[cur(qb), cur(kb), prev(kb), cur(vb), prev(vb)],
        out_specs=[pl.BlockSpec((tile, lanes), lambda hb, n: (n, hb))] * 2,
        out_shape=[jax.ShapeDtypeStruct((T, D_ATTN_OUT), F32)] * 2,
        compiler_params=_cparams(2),
    )(proj, proj, proj, proj, proj)


def _attn_bwd(name, proj, do, dl, group, T):
    d, m, hp, span, tile, nt, lanes = _attn_geometry(group, T)
    qb, kb, vb = [(c + 512 * group) // lanes for c in (C_QA, C_KA, C_VA)]

    def body(q_ref, kc_ref, kp_ref, vc_ref, vp_ref, do_ref, dl_ref, dq_ref, dk_ref, dv_ref, ck_ref, cv_ref):
        hb, n = pl.program_id(0), pl.program_id(1)
        first_key = jnp.where(n > 0, 0, BLOCK_A)

        @pl.when(n == 0)
        def _():
            ck_ref[...] = jnp.zeros_like(ck_ref)
            cv_ref[...] = jnp.zeros_like(cv_ref)

        dk_ref[...] = ck_ref[...]
        dv_ref[...] = cv_ref[...]

        @pl.when(n < nt)
        def _():
            for hh in range(hp):
                ls = slice(HEAD_DIM * hh, HEAD_DIM * (hh + 1))
                sl = _slope_d(group, hb * hp + hh, d)
                for b in range(m):
                    def one(r, b=b, ls=ls, sl=sl):
                        rq = _sub_rows(b, span, r, d)
                        if b == 0:
                            rp = _sub_rows(0, span, r, d)
                            kp, vp, fk = kp_ref[rp, ls], vp_ref[rp, ls], first_key
                        else:
                            rp = _sub_rows(b - 1, span, r, d)
                            kp, vp, fk = kc_ref[rp, ls], vc_ref[rp, ls], 0
                        f = functools.partial(_attn_block, slope_d=sl, first_key=fk)
                        _, vjp = jax.vjp(f, q_ref[rq, ls], kp, kc_ref[rq, ls], vp, vc_ref[rq, ls])
                        dq, dkp, dkc, dvp, dvc = vjp((do_ref[rq, ls], dl_ref[rq, ls]))
                        dq_ref[rq, ls] = dq
                        ck_ref[rq, ls] = dkc
                        cv_ref[rq, ls] = dvc
                        if b == 0:
                            last = _sub_rows(m - 1, span, r, d)
                            dk_ref[last, ls] += dkp
                            dv_ref[last, ls] += dvp
                        else:
                            ck_ref[rp, ls] += dkp
                            cv_ref[rp, ls] += dvp

                    _for_each_r(d, one)

    def cur(cb):
        return pl.BlockSpec((tile, lanes), lambda hb, n: (jnp.minimum(n, nt - 1), cb + hb))

    def prev(cb):
        return pl.BlockSpec((span, lanes), lambda hb, n: (jnp.maximum(m * jnp.minimum(n, nt - 1) - 1, 0), cb + hb))

    own = pl.BlockSpec((tile, lanes), lambda hb, n: (jnp.minimum(n, nt - 1), hb))
    late = pl.BlockSpec((tile, lanes), lambda hb, n: (jnp.maximum(n - 1, 0), hb))
    return pl.pallas_call(
        body, name=name, grid=(4 // hp, nt + 1),
        in_specs=[cur(qb), cur(kb), prev(kb), cur(vb), prev(vb), own, own],
        out_specs=[own, late, late],
        out_shape=[jax.ShapeDtypeStruct((T, D_ATTN_OUT), F32)] * 3,
        scratch_shapes=[pltpu.VMEM((tile, lanes), F32), pltpu.VMEM((tile, lanes), F32)],
        compiler_params=_cparams(2),
    )(proj, proj, proj, proj, proj, do, dl)


HALO = 8
HALO_IN = 16


def _conv_taps(ext, w):
    c = ext[HALO:, :] * w[CONV_WIDTH - 1]
    for k in range(1, CONV_WIDTH):
        c = c + pltpu.roll(ext, k, 0)[HALO:, :] * w[CONV_WIDTH - 1 - k]
    return c


def _taps(w_ref):
    return [w_ref[j:j + 1, :] for j in range(CONV_WIDTH)]


def _dn_pre_fwd(name, proj, ba, conv_w, alog_v, dtb_v, tm, T):
    C = 3 * D_KEY_B

    def body(x_ref, ba_ref, w_ref, al_ref, dt_ref, q_ref, k_ref, v_ref, bg_ref, halo_ref):
        i = pl.program_id(0)

        @pl.when(i == 0)
        def _():
            halo_ref[...] = jnp.zeros_like(halo_ref)

        x = x_ref[...].astype(F32)
        ext = jnp.concatenate([halo_ref[...], x], axis=0)
        c = _conv_taps(ext, _taps(w_ref))
        halo_ref[...] = x[tm - HALO:, :]
        qn, kn, v, bg = _f_dnpoint(c, ba_ref[...], al_ref[...], dt_ref[...])
        q_ref[...] = qn
        k_ref[...] = kn
        v_ref[...] = v
        bg_ref[...] = bg

    return pl.pallas_call(
        body, name=name, grid=(T // tm,),
        in_specs=[_row_spec(tm, C, 0), _row_spec(tm, 128, 0), _full_spec(conv_w.shape), _full_spec((1, 128)),
                  _full_spec((1, 128))],
        out_specs=[_row_spec(tm, D_KEY_B, 0)] * 3 + [_row_spec(tm, 128, 0)],
        out_shape=[jax.ShapeDtypeStruct((T, D_KEY_B), F32)] * 3 + [jax.ShapeDtypeStruct((T, 128), F32)],
        scratch_shapes=[pltpu.VMEM((HALO, C), F32)],
        compiler_params=_ARB,
    )(proj, ba, conv_w, alog_v, dtb_v)


def _dn_pre_bwd(name, proj, ba, conv_w, alog_v, dtb_v, dq, dk, dv, dbg, tm, T):
    C = 3 * D_KEY_B
    nt = T // tm
    hb = tm // HALO_IN

    def body(x_ref, xh_ref, ba_ref, w_ref, al_ref, dt_ref, dq_ref, dk_ref, dv_ref, dbg_ref,
             dx_ref, dba_ref, dw_ref, dal_ref, ddt_ref, nxt_ref):
        s = pl.program_id(0)
        i = nt - 1 - s

        @pl.when(s == 0)
        def _():
            nxt_ref[...] = jnp.zeros_like(nxt_ref)
            dw_ref[...] = jnp.zeros_like(dw_ref)
            dal_ref[...] = jnp.zeros_like(dal_ref)
            ddt_ref[...] = jnp.zeros_like(ddt_ref)

        w = _taps(w_ref)
        halo = jnp.where(i > 0, xh_ref[...].astype(F32)[HALO_IN - HALO:, :], 0.0)
        ext = jnp.concatenate([halo, x_ref[...].astype(F32)], axis=0)
        c = _conv_taps(ext, w)
        _, vjp = jax.vjp(_f_dnpoint, c, ba_ref[...], al_ref[...], dt_ref[...])
        dc, dba, dal, ddt = vjp((dq_ref[...], dk_ref[...], dv_ref[...], dbg_ref[...]))
        dba_ref[...] = dba
        dal_ref[...] += dal
        ddt_ref[...] += ddt
        ext2 = jnp.concatenate([dc, nxt_ref[...]], axis=0)
        dx = dc * w[CONV_WIDTH - 1]
        for m in range(1, CONV_WIDTH):
            dx = dx + pltpu.roll(ext2, tm + HALO - m, 0)[:tm, :] * w[CONV_WIDTH - 1 - m]
        dx_ref[...] = dx.astype(dx_ref.dtype)
        nxt_ref[...] = dc[:HALO, :]
        for j in range(CONV_WIDTH):
            k = CONV_WIDTH - 1 - j
            xs = ext[HALO:, :] if k == 0 else pltpu.roll(ext, k, 0)[HALO:, :]
            dw_ref[j:j + 1, :] += jnp.sum(xs * dc, axis=0, keepdims=True)

    def rev(w):
        return pl.BlockSpec((tm, w), lambda s: (nt - 1 - s, 0))

    return pl.pallas_call(
        body, name=name, grid=(nt,),
        in_specs=[rev(C), pl.BlockSpec((HALO_IN, C), lambda s: (jnp.maximum((nt - 1 - s) * hb - 1, 0), 0)), rev(128),
                  _full_spec(conv_w.shape), _full_spec((1, 128)), _full_spec((1, 128)),
                  rev(D_KEY_B), rev(D_KEY_B), rev(D_KEY_B), rev(128)],
        out_specs=[rev(C), rev(128), _full_spec((8, C)), _full_spec((1, 128)), _full_spec((1, 128))],
        out_shape=[jax.ShapeDtypeStruct((T, C), BF16), jax.ShapeDtypeStruct((T, 128), F32),
                   jax.ShapeDtypeStruct((8, C), F32), jax.ShapeDtypeStruct((1, 128), F32),
                   jax.ShapeDtypeStruct((1, 128), F32)],
        scratch_shapes=[pltpu.VMEM((HALO, C), F32)],
        compiler_params=_ARB,
    )(proj, proj, ba, conv_w, alog_v, dtb_v, dq, dk, dv, dbg)


def _bdot(a, b, dims, exact=False):
    if exact:
        return lax.dot_general(a, b, dims, preferred_element_type=F32, precision=lax.Precision.HIGH)
    return lax.dot_general(a.astype(BF16), b.astype(BF16), dims, preferred_element_type=F32)


_B_NN = (((2,), (1,)), ((0,), (0,)))
_B_NT = (((2,), (2,)), ((0,), (0,)))
_B_TN = (((1,), (1,)), ((0,), (0,)))


def _unit_lower_inverse(a):
    C = a.shape[-1]
    r = lax.broadcasted_iota(jnp.int32, (C, C), 0)
    c = lax.broadcasted_iota(jnp.int32, (C, C), 1)
    pw = -a
    t = jnp.where((r == c)[None], 1.0, 0.0) + pw
    for _ in range(int(math.log2(C)) - 1):
        pw = _bdot(pw, pw, _B_NN, exact=True)
        t = t + _bdot(t, pw, _B_NN, exact=True)
    return t


@jax.custom_vjp
def _solve_given_inverse(t, a, rhs):
    return _bdot(t, rhs, _B_NN, exact=True)


def _solve_fwd(t, a, rhs):
    sol = _bdot(t, rhs, _B_NN, exact=True)
    return sol, (t, sol)


def _solve_bwd(res, dsol):
    t, sol = res
    drhs = _bdot(t, dsol, _B_TN, exact=True)
    return jnp.zeros_like(t), -_bdot(drhs, sol, _B_NT, exact=True), drhs


_solve_given_inverse.defvjp(_solve_fwd, _solve_bwd)


def _stack(xs):
    return jnp.concatenate([x[None] for x in xs], axis=0)


def _delta_chunk(S, q, k, v, bg, t_inv=None):
    H, C = N_HEADS_B, CHUNK
    r = lax.broadcasted_iota(jnp.int32, (C, C), 0)
    c = lax.broadcasted_iota(jnp.int32, (C, C), 1)
    incl, strict, eye = r >= c, r > c, r == c
    gc_all = jnp.dot(incl.astype(F32), bg, preferred_element_type=F32, precision=lax.Precision.HIGHEST)
    lane = lax.broadcasted_iota(jnp.int32, (C, 128), 1)

    def col(x, l):
        return jnp.sum(jnp.where(lane == l, x, 0.0), axis=1, keepdims=True)

    beta = _stack([col(bg, h) for h in range(H)])
    gc = _stack([col(gc_all, H + h) for h in range(H)])
    gc_row = jnp.sum(jnp.where(eye[None], jnp.broadcast_to(gc, (H, C, C)), 0.0), axis=1, keepdims=True)
    row = lax.broadcasted_iota(jnp.int32, (H, C, 1), 1)
    gc_last = jnp.sum(jnp.where(row == C - 1, gc, 0.0), axis=1, keepdims=True)
    decay = jnp.exp(jnp.where(incl[None], gc - gc_row, -jnp.inf))
    qh = _stack(_heads(q, H))
    kh = _stack(_heads(k, H))
    vh = _stack(_heads(v, H))
    kk = _bdot(kh, kh, _B_NT, exact=True)
    a = jnp.where(strict[None], beta * kk * decay, 0.0)
    egc = jnp.exp(gc)
    rhs = jnp.concatenate([beta * vh, (beta * egc) * kh], axis=-1)
    if t_inv is None:
        t_inv = _unit_lower_inverse(a)
        sol = _bdot(t_inv, rhs, _B_NN, exact=True)
    else:
        sol = _solve_given_inverse(t_inv, a, rhs)
    u_bar, w = sol[..., :HEAD_DIM], sol[..., HEAD_DIM:]
    qk = _bdot(qh, kh, _B_NT) * decay
    q_dec = qh * egc
    k_dec = kh * jnp.exp(gc_last - gc)
    u = u_bar - _bdot(w, S, _B_NN)
    o = _bdot(q_dec, S, _B_NN) + _bdot(qk, u, _B_NN)
    S_new = jnp.exp(gc_last) * S + _bdot(k_dec, u, _B_TN)
    o2 = jnp.concatenate([lax.index_in_dim(o, h, 0, keepdims=False) for h in range(H)], axis=1)
    return o2, S_new, t_inv


def _delta_fwd(name, q, k, v, bg, T):
    nc = T // CHUNK
    H = N_HEADS_B

    def body(q_ref, k_ref, v_ref, bg_ref, o_ref, s_out_ref, t_out_ref, s_ref):
        @pl.when(pl.program_id(0) == 0)
        def _():
            s_ref[...] = jnp.zeros_like(s_ref)

        S = s_ref[...]
        s_out_ref[0] = S
        o, S_new, t_inv = _delta_chunk(S, q_ref[...], k_ref[...], v_ref[...], bg_ref[...])
        o_ref[...] = o
        t_out_ref[0] = t_inv
        s_ref[...] = S_new

    return pl.pallas_call(
        body, name=name, grid=(nc,),
        in_specs=[_row_spec(CHUNK, D_KEY_B, 0)] * 3 + [_row_spec(CHUNK, 128, 0)],
        out_specs=[_row_spec(CHUNK, D_KEY_B, 0), pl.BlockSpec((1, H, HEAD_DIM, HEAD_DIM), lambda i: (i, 0, 0, 0)),
                   pl.BlockSpec((1, H, CHUNK, CHUNK), lambda i: (i, 0, 0, 0))],
        out_shape=[jax.ShapeDtypeStruct((T, D_KEY_B), F32), jax.ShapeDtypeStruct((nc, H, HEAD_DIM, HEAD_DIM), F32),
                   jax.ShapeDtypeStruct((nc, H, CHUNK, CHUNK), F32)],
        scratch_shapes=[pltpu.VMEM((H, HEAD_DIM, HEAD_DIM), F32)],
        compiler_params=_ARB,
    )(q, k, v, bg)


def _delta_bwd(name, q, k, v, bg, s_all, t_all, do, T):
    nc = T // CHUNK
    H = N_HEADS_B

    def body(q_ref, k_ref, v_ref, bg_ref, s_in_ref, t_ref, do_ref, dq_ref, dk_ref, dv_ref, dbg_ref, ds_ref):
        @pl.when(pl.program_id(0) == 0)
        def _():
            ds_ref[...] = jnp.zeros_like(ds_ref)

        t_inv = t_ref[0]

        def f(S, qq, kk, vv, bb):
            return _delta_chunk(S, qq, kk, vv, bb, t_inv)[:2]

        _, vjp = jax.vjp(f, s_in_ref[0], q_ref[...], k_ref[...], v_ref[...], bg_ref[...])
        dS, dq, dk, dv, dbg = vjp((do_ref[...], ds_ref[...]))
        dq_ref[...] = dq
        dk_ref[...] = dk
        dv_ref[...] = dv
        dbg_ref[...] = dbg
        ds_ref[...] = dS

    def rev(w):
        return pl.BlockSpec((CHUNK, w), lambda s: (nc - 1 - s, 0))

    def rev4(n):
        return pl.BlockSpec((1, H, n, n), lambda s: (nc - 1 - s, 0, 0, 0))

    return pl.pallas_call(
        body, name=name, grid=(nc,),
        in_specs=[rev(D_KEY_B)] * 3 + [rev(128), rev4(HEAD_DIM), rev4(CHUNK), rev(D_KEY_B)],
        out_specs=[rev(D_KEY_B)] * 3 + [rev(128)],
        out_shape=[jax.ShapeDtypeStruct((T, D_KEY_B), F32)] * 3 + [jax.ShapeDtypeStruct((T, 128), F32)],
        scratch_shapes=[pltpu.VMEM((H, HEAD_DIM, HEAD_DIM), F32)],
        compiler_params=_ARB,
    )(q, k, v, bg, s_all, t_all, do)


def _loss_fb(name, h2, w, tgt, tm, T):
    def tile_loss(h, wv, t):
        y = _f_rms(h, wv)[0]
        e = y - t
        return 0.5 * jnp.sum(jnp.mean(e * e, axis=-1))

    def body(h_ref, w_ref, t_ref, dh_ref, loss_ref, dw_ref):
        @pl.when(pl.program_id(0) == 0)
        def _():
            loss_ref[...] = jnp.zeros_like(loss_ref)
            dw_ref[...] = jnp.zeros_like(dw_ref)

        val, (dh, dw) = jax.value_and_grad(tile_loss, argnums=(0, 1))(h_ref[...], w_ref[...], t_ref[...])
        dh_ref[...] = dh
        dw_ref[...] += dw
        loss_ref[...] += jnp.reshape(val, (1, 1))

    return pl.pallas_call(
        body, name=name, grid=(T // tm,),
        in_specs=[_row_spec(tm, D_MODEL, 0), _full_spec((1, D_MODEL)), _row_spec(tm, D_MODEL, 0)],
        out_specs=[_row_spec(tm, D_MODEL, 0), _full_spec((1, 1)), _full_spec((1, D_MODEL))],
        out_shape=[jax.ShapeDtypeStruct((T, D_MODEL), F32), jax.ShapeDtypeStruct((1, 1), F32),
                   jax.ShapeDtypeStruct((1, D_MODEL), F32)],
        compiler_params=_ARB,
    )(h2, w, tgt)


def _adamw(name, w, g, m, v, tile):
    R = w.shape[0]

    def body(w_ref, g_ref, m_ref, v_ref, d_out, m_out, v_out):
        g = g_ref[...]
        mm = ADAM_B1 * m_ref[...] + (1.0 - ADAM_B1) * g
        vv = ADAM_B2 * v_ref[...] + (1.0 - ADAM_B2) * jnp.square(g)
        m_hat = mm / (1.0 - ADAM_B1 ** ADAM_STEP)
        v_hat = vv / (1.0 - ADAM_B2 ** ADAM_STEP)
        d_out[...] = -ADAM_LR * (m_hat / (jnp.sqrt(v_hat) + ADAM_EPS) + ADAM_WD * w_ref[...])
        m_out[...] = mm
        v_out[...] = vv

    assert R % tile == 0
    spec = pl.BlockSpec((tile, w.shape[1]), lambda i: (i, 0))
    return pl.pallas_call(
        body, name=name, grid=(R // tile,), in_specs=[spec] * 4, out_specs=[spec] * 3,
        out_shape=[jax.ShapeDtypeStruct(w.shape, F32)] * 3, compiler_params=_ARB,
    )(w, g, m, v)


def _sum4(name, r):
    R = r.shape[1]

    def body(r_ref, o_ref):
        o_ref[...] = ((r_ref[0].astype(F32) + r_ref[1].astype(F32)) + r_ref[2].astype(F32)) + r_ref[3].astype(F32)

    return pl.pallas_call(
        body, name=name, grid=(R // HALF_TILE,),
        in_specs=[pl.BlockSpec((4, HALF_TILE, D_MODEL), lambda i: (0, i, 0))],
        out_specs=pl.BlockSpec((HALF_TILE, D_MODEL), lambda i: (i, 0)),
        out_shape=jax.ShapeDtypeStruct((R, D_MODEL), F32), compiler_params=_ARB,
    )(r)


def _add2(name, a, b):
    R = a.shape[0]

    def body(a_ref, b_ref, o_ref):
        o_ref[...] = (a_ref[...].astype(F32) + b_ref[...].astype(F32)).astype(BF16)

    spec = pl.BlockSpec((HALF_TILE, D_MODEL), lambda i: (i, 0))
    return pl.pallas_call(
        body, name=name, grid=(R // HALF_TILE,), in_specs=[spec, spec], out_specs=spec,
        out_shape=jax.ShapeDtypeStruct((R, D_MODEL), BF16), compiler_params=_ARB,
    )(a, b)


_ANY = pl.BlockSpec(memory_space=pl.ANY)


def _place():
    return lax.axis_index("x"), lax.axis_index("y"), lax.axis_index("c")


def _gather_chips(name, shard):
    def body(s_ref, out_ref, send_sems, recv_sems, local_sem):
        x, y, c = _place()
        me = 2 * x + y
        chips = [(1 - x, y), (x, 1 - y), (1 - x, 1 - y)]
        mine = pltpu.make_async_copy(s_ref, out_ref.at[me], local_sem)
        mine.start()

        def copy(k, slot, px, py):
            return pltpu.make_async_remote_copy(src_ref=s_ref, dst_ref=out_ref.at[slot], send_sem=send_sems.at[k],
                                                recv_sem=recv_sems.at[k], device_id=(px, py, c), device_id_type=MESH)

        sends = [copy(k, me, px, py) for k, (px, py) in enumerate(chips)]
        for cp in sends:
            cp.start()
        for k, (px, py) in enumerate(chips):
            copy(k, 2 * px + py, px, py).wait_recv()
        for cp in sends:
            cp.wait_send()
        mine.wait()

    return pl.pallas_call(
        body, name=name, in_specs=[_ANY], out_specs=_ANY,
        out_shape=jax.ShapeDtypeStruct((4,) + shard.shape, shard.dtype),
        scratch_shapes=[pltpu.SemaphoreType.DMA((3,)), pltpu.SemaphoreType.DMA((3,)), pltpu.SemaphoreType.DMA],
    )(shard)


def _scatter_chips(name, g):
    def body(g_ref, out_ref, send_sems, recv_sems, local_sem):
        x, y, c = _place()
        me = 2 * x + y
        chips = [(1 - x, y), (x, 1 - y), (1 - x, 1 - y)]
        mine = pltpu.make_async_copy(g_ref.at[me], out_ref.at[me], local_sem)
        mine.start()

        def copy(k, src_slot, dst_slot, px, py):
            return pltpu.make_async_remote_copy(src_ref=g_ref.at[src_slot], dst_ref=out_ref.at[dst_slot],
                                                send_sem=send_sems.at[k], recv_sem=recv_sems.at[k],
                                                device_id=(px, py, c), device_id_type=MESH)

        sends = [copy(k, 2 * px + py, me, px, py) for k, (px, py) in enumerate(chips)]
        for cp in sends:
            cp.start()
        for k, (px, py) in enumerate(chips):
            copy(k, me, 2 * px + py, px, py).wait_recv()
        for cp in sends:
            cp.wait_send()
        mine.wait()

    return pl.pallas_call(
        body, name=name, in_specs=[_ANY], out_specs=_ANY,
        out_shape=jax.ShapeDtypeStruct(g.shape, g.dtype),
        scratch_shapes=[pltpu.SemaphoreType.DMA((3,)), pltpu.SemaphoreType.DMA((3,)), pltpu.SemaphoreType.DMA],
    )(g)


def _swap_sibling(name, p):
    def body(p_ref, out_ref, send_sem, recv_sem):
        x, y, c = _place()
        cp = pltpu.make_async_remote_copy(src_ref=p_ref, dst_ref=out_ref, send_sem=send_sem, recv_sem=recv_sem,
                                          device_id=(x, y, 1 - c), device_id_type=MESH)
        cp.start()
        cp.wait()

    return pl.pallas_call(
        body, name=name, in_specs=[_ANY], out_specs=_ANY,
        out_shape=jax.ShapeDtypeStruct(p.shape, p.dtype),
        scratch_shapes=[pltpu.SemaphoreType.DMA, pltpu.SemaphoreType.DMA],
    )(p)


def _allsum_small(name, pack):
    def body(p_ref, out_ref, slots, send_sems, recv_sems):
        x, y, c = _place()
        me = 4 * x + 2 * y + c
        slots[me] = p_ref[...]
        sends = []
        for k in range(1, 8):
            px = 1 - x if k & 4 else x
            py = 1 - y if k & 2 else y
            pc = 1 - c if k & 1 else c
            cp = pltpu.make_async_remote_copy(src_ref=p_ref, dst_ref=slots.at[me], send_sem=send_sems.at[k - 1],
                                              recv_sem=recv_sems.at[k - 1], device_id=(px, py, pc), device_id_type=MESH)
            cp.start()
            sends.append(cp)
        for k in range(1, 8):
            px = 1 - x if k & 4 else x
            py = 1 - y if k & 2 else y
            pc = 1 - c if k & 1 else c
            pltpu.make_async_remote_copy(src_ref=p_ref, dst_ref=slots.at[4 * px + 2 * py + pc],
                                         send_sem=send_sems.at[k - 1], recv_sem=recv_sems.at[k - 1],
                                         device_id=(px, py, pc), device_id_type=MESH).wait_recv()
        for cp in sends:
            cp.wait_send()
        acc = slots[0]
        for s in range(1, 8):
            acc = acc + slots[s]
        out_ref[...] = acc

    vm = pl.BlockSpec(memory_space=pltpu.VMEM)
    return pl.pallas_call(
        body, name=name, in_specs=[vm], out_specs=vm,
        out_shape=jax.ShapeDtypeStruct(pack.shape, pack.dtype),
        scratch_shapes=[pltpu.VMEM((8,) + pack.shape, pack.dtype), pltpu.SemaphoreType.DMA((7,)),
                        pltpu.SemaphoreType.DMA((7,))],
    )(pack)


def _pack_shards(parts, rows_total):
    flat = [p.reshape(-1, D_MODEL) for p in parts]
    used = sum(f.shape[0] for f in flat)
    return jnp.concatenate(flat + [jnp.zeros((rows_total - used, D_MODEL), flat[0].dtype)], axis=0)


def _unpack_shards(buf, shapes):
    out, r0 = [], 0
    for shp in shapes:
        n = math.prod(shp) // D_MODEL
        out.append(buf[r0:r0 + n].reshape(shp))
        r0 += n
    return out


def kernel(x, norm_mix, w_in, conv_w, a_log, dt_bias, dn_norm, w_proj_attn, w_proj_delta, w_out, norm_ffn, w_gate, w_up, w_down, norm_final, loss_target, m_norm_mix, m_w_in, m_conv_w, m_a_log, m_dt_bias, m_dn_norm, m_w_proj_attn, m_w_proj_delta, m_w_out, m_norm_ffn, m_w_gate, m_w_up, m_w_down, m_norm_final, v_norm_mix, v_w_in, v_conv_w, v_a_log, v_dt_bias, v_dn_norm, v_w_proj_attn, v_w_proj_delta, v_w_out, v_norm_ffn, v_w_gate, v_w_up, v_w_down, v_norm_final):
    T = x.shape[1]
    D = D_MODEL
    tm = 512
    tw = 256
    mt = 1024 if T % 1024 == 0 else 512
    x2 = x.reshape(T, D)
    tgt = loss_target.reshape(T, D)

    sharded = (w_proj_attn, w_proj_delta, w_out, w_gate, w_up, w_down, w_in)
    conv_bits = lax.bitcast_convert_type(conv_w[0], BF16)
    w16 = _pack_shards([w[0].astype(BF16) for w in sharded] + [conv_bits], PACK_R16)
    my_c = lax.axis_index("c")
    w_mine = _gather_chips("gather_weights", lax.dynamic_slice_in_dim(w16, my_c * HALF, HALF, axis=0))
    w_other = _swap_sibling("swap_weights", w_mine)
    wg = jnp.concatenate([jnp.where(my_c == 0, w_mine, w_other), jnp.where(my_c == 0, w_other, w_mine)], axis=1)

    def seg(r0, n):
        return wg[:, r0:r0 + n, :]

    r = 0
    Wpa = seg(r, 128).reshape(4, D_ATTN_OUT, 256).transpose(1, 0, 2).reshape(D_ATTN_OUT, D); r += 128
    Wpd = seg(r, 256).reshape(D, D); r += 256
    Wout = seg(r, 256).reshape(D, D); r += 256
    Wg = seg(r, 704).reshape(4, D, 704).transpose(1, 0, 2).reshape(D, D_FF); r += 704
    Wu = seg(r, 704).reshape(4, D, 704).transpose(1, 0, 2).reshape(D, D_FF); r += 704
    Wd = seg(r, 704).reshape(D_FF, D); r += 704
    Win = seg(r, 2692).reshape(4, D, 2692).transpose(1, 0, 2).reshape(D, D_IN); r += 2692
    convw = lax.bitcast_convert_type(seg(r, 6).reshape(4, CONV_WIDTH, 768, 2), F32)
    convw = convw.transpose(1, 0, 2).reshape(CONV_WIDTH, 3 * D_KEY_B)
    Wa = Win[:, :D_PA]
    Wb = jnp.concatenate([Win[:, 4608:8704], Win[:, 8720:]], axis=1)
    Wba = jnp.pad(Win[:, 8704:8720], ((0, 0), (0, 112)))
    Wgu = jnp.concatenate([Wg, Wu], axis=1)
    alog_v = jnp.pad(a_log, ((0, 0), (8, 112)))
    dtb_v = jnp.pad(dt_bias, ((0, 0), (8, 112)))

    (u,) = _rowwise("rms_mix", _f_rms, [(x2, 0, D)], [norm_mix], [(D, BF16)], tm, T)
    proj_a = _matmul("mm_in_a", u, Wa, "nn", T, D_PA, D, mt, 1536, D, F32)
    proj_b = _matmul("mm_in_b", u, Wb, "nn", T, D_PB, D, mt, 1536, D, BF16)
    ba = _matmul("mm_ba", u, Wba, "nn", T, 128, D, mt, 128, D, F32)

    att = [_attn_fwd(f"attn_fwd{g}", proj_a, g, T) for g in range(3)]
    (ya_m,) = _rowwise("merge", _f_merge, [(att[g][0], 0, 512) for g in range(3)] + [(att[g][1], 0, 512) for g in range(3)],
                       [], [(512, BF16)], tm, T)
    qn, kn, vd, bg = _dn_pre_fwd("dn_pre_fwd", proj_b, ba, convw, alog_v, dtb_v, tw, T)
    o_d, s_all, t_all = _delta_fwd("delta_fwd", qn, kn, vd, bg, T)
    (od,) = _rowwise("post_dn", _f_postdn, [(o_d, 0, D), (proj_b, C_Z // D, D)], [dn_norm], [(D, BF16)], tm, T)
    y_a = _matmul("mm_pa", ya_m, Wpa, "nn", T, D, 512, mt, D, 512, F32)
    y_b = _matmul("mm_pd", od, Wpd, "nn", T, D, D, mt, D, D, F32)
    (mix,) = _rowwise("mix", _f_mix, [(y_a, 0, D), (y_b, 0, D), (proj_b, C_GATE // 2048, 2048)], [], [(D, BF16)], tm, T)
    h1 = _matmul("mm_out", mix, Wout, "nn", T, D, D, mt, D, D, F32, residual=x2)
    (hn,) = _rowwise("rms_ffn", _f_rms, [(h1, 0, D)], [norm_ffn], [(D, BF16)], tm, T)
    gu = _matmul("mm_gu", hn, Wgu, "nn", T, 2 * D_FF, D, mt, 1408, D, BF16)
    (act,) = _rowwise("act", _f_act, [(gu, 0, 2 * D_FF)], [], [(D_FF, BF16)], tw, T)
    h2 = _matmul("mm_down", act, Wd, "nn", T, D, D_FF, mt, D, 1408, F32, residual=h1)
    dh2, loss, g_norm_final = _loss_fb("loss", h2, norm_final.reshape(1, D), tgt, tm, T)

    dact = _matmul("mm_dact", dh2, Wd, "nt", T, D_FF, D, mt, 1408, D, BF16)
    gWd = _matmul("mm_gwd", act, dh2, "tn", D_FF, D, T, 1408, D, mt, BF16)
    (dgu,) = _rowwise_vjp("act_bwd", _f_act, [(gu, 0, 2 * D_FF)], [], [(dact, 0, D_FF)], [BF16], tw, T)
    dhn = _matmul("mm_dhn", dgu, Wgu, "nt", T, D, 2 * D_FF, mt, D, 1408, F32)
    gWgu = _matmul("mm_gwgu", hn, dgu, "tn", D, 2 * D_FF, T, D, 1408, mt, BF16)
    dh1, g_norm_ffn = _rowwise_vjp("rms_ffn_bwd", _f_rms, [(h1, 0, D)], [norm_ffn], [(dhn, 0, D)], [F32], tm, T,
                                   residual=(dh2, 0, D))
    dmix = _matmul("mm_dmix", dh1, Wout, "nt", T, D, D, mt, D, D, F32)
    gWout = _matmul("mm_gwout", mix, dh1, "tn", D, D, T, D, D, mt, BF16)
    dya, dyb, dgate = _rowwise_vjp("mix_bwd", _f_mix, [(y_a, 0, D), (y_b, 0, D), (proj_b, C_GATE // 2048, 2048)], [],
                                   [(dmix, 0, D)], [BF16, BF16, BF16], tm, T)
    dya_m = _matmul("mm_dyam", dya, Wpa, "nt", T, 512, D, mt, 512, D, F32)
    gWpa = _matmul("mm_gwpa", ya_m, dya, "tn", 512, D, T, 512, D, mt, BF16)
    dod = _matmul("mm_dod", dyb, Wpd, "nt", T, D, D, mt, D, D, F32)
    gWpd = _matmul("mm_gwpd", od, dyb, "tn", D, D, T, D, D, mt, BF16)
    do_d, dz, g_dn_norm = _rowwise_vjp("post_dn_bwd", _f_postdn, [(o_d, 0, D), (proj_b, C_Z // D, D)], [dn_norm],
                                       [(dod, 0, D)], [F32, BF16], tm, T)
    dqn, dkn, dvd, dbg = _delta_bwd("delta_bwd", qn, kn, vd, bg, s_all, t_all, do_d, T)
    dqkvd, dba, g_conv8, g_alog_v, g_dtb_v = _dn_pre_bwd("dn_pre_bwd", proj_b, ba, convw, alog_v, dtb_v,
                                                         dqn, dkn, dvd, dbg, tw, T)
    mg = _rowwise_vjp("merge_bwd", _f_merge,
                      [(att[g][0], 0, 512) for g in range(3)] + [(att[g][1], 0, 512) for g in range(3)], [],
                      [(dya_m, 0, 512)], [F32] * 6, tm, T)
    datt = [_attn_bwd(f"attn_bwd{g}", proj_a, mg[g], mg[3 + g], g, T) for g in range(3)]
    dproj_a = jnp.concatenate([datt[g][j].astype(BF16) for j in range(3) for g in range(3)], axis=1)
    dproj_b = jnp.concatenate([dqkvd, dz, dgate], axis=1)
    gWa = _matmul("mm_gwa", u, dproj_a, "tn", D, D_PA, T, D, 1536, mt, BF16)
    gWb = _matmul("mm_gwb", u, dproj_b, "tn", D, D_PB, T, D, 1536, mt, BF16)
    gWba = _matmul("mm_gwba", u, dba, "tn", D, 128, T, D, 128, mt, BF16)
    du0 = _matmul("mm_du_ba", dba, Wba, "nt", T, D, 128, mt, D, 128, F32)
    du1 = _matmul("mm_du_b", dproj_b, Wb, "nt", T, D, D_PB, mt, D, 1536, F32, residual=du0)
    du = _matmul("mm_du_a", dproj_a, Wa, "nt", T, D, D_PA, mt, D, 1536, F32, residual=du1)
    dx, g_norm_mix = _rowwise_vjp("rms_mix_bwd", _f_rms, [(x2, 0, D)], [norm_mix], [(du, 0, D)], [F32], tm, T,
                                  residual=(dh1, 0, D))

    gWin = jnp.concatenate([gWa, gWb[:, :C_GATE], gWba[:, :16], gWb[:, C_GATE:]], axis=1)

    def cols4(g, n):
        return g.reshape(g.shape[0], 4, n).transpose(1, 0, 2).reshape(4, -1, D)

    parts = [cols4(gWpa, 256), gWpd.reshape(4, 256, D), gWout.reshape(4, 256, D), cols4(gWgu[:, :D_FF], 704),
             cols4(gWgu[:, D_FF:], 704), gWd.reshape(4, 704, D), cols4(gWin, 2692), cols4(g_conv8[:CONV_WIDTH], 768)]
    used = sum(p.shape[1] for p in parts)
    g_pack = jnp.concatenate([p.astype(BF16) for p in parts] + [jnp.zeros((4, PACK_R - used, D), BF16)], axis=1)
    g_keep = lax.dynamic_slice_in_dim(g_pack, my_c * HALF, HALF, axis=1)
    g_give = lax.dynamic_slice_in_dim(g_pack, (1 - my_c) * HALF, HALF, axis=1)
    g_got = _swap_sibling("swap_grads", g_give)
    g_pair = _add2("add_cores", g_keep.reshape(4 * HALF, D), g_got.reshape(4 * HALF, D)).reshape(4, HALF, D)
    recv = _scatter_chips("scatter_grads", g_pair)
    g_mine = _sum4("sum_chips", recv)
    g_other = _swap_sibling("swap_sums", g_mine)
    g_all = jnp.concatenate([jnp.where(my_c == 0, g_mine, g_other), jnp.where(my_c == 0, g_other, g_mine)], axis=0)

    locals_ = (w_proj_attn, w_proj_delta, w_out, w_gate, w_up, w_down, w_in, conv_w)
    ms = (m_w_proj_attn, m_w_proj_delta, m_w_out, m_w_gate, m_w_up, m_w_down, m_w_in, m_conv_w)
    vs = (v_w_proj_attn, v_w_proj_delta, v_w_out, v_w_gate, v_w_up, v_w_down, v_w_in, v_conv_w)
    big_names = ("w_proj_attn", "w_proj_delta", "w_out", "w_gate", "w_up", "w_down", "w_in", "conv_w")
    adam_tiles = (512, 256, 256, 256, 256, 352, 128, CONV_WIDTH)
    g_shards = _unpack_shards(g_all, [w.shape for w in locals_])
    big = [g_shards, [], [], []]
    for nm, w, g, m_, v_, tl in zip(big_names, locals_, g_shards, ms, vs, adam_tiles):
        res = _adamw("adamw_" + nm, w[0], g[0], m_[0], v_[0], tl)
        for kind in range(3):
            big[kind + 1].append(res[kind].reshape(w.shape))

    def small_pack(nm, nf, nfin, dn, al, dt, extra):
        row3 = jnp.concatenate([dn.reshape(1, 128), al.reshape(1, 8), dt.reshape(1, 8), extra.reshape(1, 1),
                                jnp.zeros((1, D - 145), F32)], axis=1)
        return jnp.concatenate([nm.reshape(1, D), nf.reshape(1, D), nfin.reshape(1, D), row3, jnp.zeros((4, D), F32)], axis=0)

    zero1 = jnp.zeros((1, 1), F32)
    sp = small_pack(g_norm_mix, g_norm_ffn, g_norm_final, g_dn_norm, g_alog_v[:, 8:16], g_dtb_v[:, 8:16], loss)
    tot = _allsum_small("allsum_small", sp)
    sm = _adamw("adamw_small", small_pack(norm_mix, norm_ffn, norm_final, dn_norm, a_log, dt_bias, zero1), tot,
                small_pack(m_norm_mix, m_norm_ffn, m_norm_final, m_dn_norm, m_a_log, m_dt_bias, zero1),
                small_pack(v_norm_mix, v_norm_ffn, v_norm_final, v_dn_norm, v_a_log, v_dt_bias, zero1), 8)
    sm = [tot] + list(sm)

    def small_unpack(p):
        return {"norm_mix": p[0:1], "norm_ffn": p[1:2], "norm_final": p[2], "dn_norm": p[3:4, :128],
                "a_log": p[3:4, 128:136], "dt_bias": p[3:4, 136:144]}

    sm = [small_unpack(p) for p in sm]
    loss_out = tot[3, 144]

    order = ("norm_mix", "w_in", "conv_w", "a_log", "dt_bias", "dn_norm", "w_proj_attn", "w_proj_delta", "w_out",
             "norm_ffn", "w_gate", "w_up", "w_down", "norm_final")
    outs = [loss_out, dx.reshape(x.shape)]
    for kind in range(4):
        for n in order:
            outs.append(big[kind][big_names.index(n)] if n in big_names else sm[kind][n])
    return tuple(outs)
```

```python
import functools
import math

import jax
import jax.numpy as jnp
from jax import lax
from jax.experimental import pallas as pl
from jax.experimental.pallas import tpu as pltpu

F32 = jnp.float32
BF16 = jnp.bfloat16
MESH = pl.DeviceIdType.MESH

D_MODEL = 1024
N_HEADS_A = 12
HEAD_DIM = 128
BLOCK_A = 128
DILATIONS = (1, 4, 16)
W_SUB = 128
D_ATTN_OUT = 512
N_HEADS_B = 8
D_KEY_B = 1024
CONV_WIDTH = 4
CHUNK = 64
D_FF = 2816
EPS = 1e-6
D_IN = 10768
D_PA = 4608
D_PB = 6144
C_Z, C_GATE = 3072, 4096
C_QA, C_KA, C_VA = 0, 1536, 3072

ADAM_LR, ADAM_B1, ADAM_B2, ADAM_EPS, ADAM_WD, ADAM_STEP = 0.001, 0.9, 0.999, 1e-08, 0.01, 10

PACK_R = 2816
HALF = PACK_R // 2
HALF_TILE = 128
W_IN_COLS = 2692
W_IN_HALF = D_MODEL // 2

VMEM_LIMIT = 48 * 1024 * 1024


def _cparams(n):
    return pltpu.CompilerParams(dimension_semantics=("arbitrary",) * n, vmem_limit_bytes=VMEM_LIMIT)


_ARB = _cparams(1)


def _row_spec(tm, w, cb):
    return pl.BlockSpec((tm, w), lambda i: (i, cb))


def _full_spec(shape):
    nd = len(shape)
    return pl.BlockSpec(shape, lambda i: (0,) * nd)


def _rowwise(name, fn, ins, params, outs, tm, T):
    n_in, n_p = len(ins), len(params)

    def body(*refs):
        xs = [r[...].astype(F32) for r in refs[:n_in]] + [r[...] for r in refs[n_in:n_in + n_p]]
        res = fn(*xs)
        for o_ref, v in zip(refs[n_in + n_p:], res):
            o_ref[...] = v.astype(o_ref.dtype)

    return pl.pallas_call(
        body, name=name, grid=(T // tm,),
        in_specs=[_row_spec(tm, w, cb) for (_, cb, w) in ins] + [_full_spec(p.shape) for p in params],
        out_specs=[_row_spec(tm, w, 0) for (w, _) in outs],
        out_shape=[jax.ShapeDtypeStruct((T, w), dt) for (w, dt) in outs],
        compiler_params=_ARB,
    )(*[a for (a, _, _) in ins], *params)


def _rowwise_vjp(name, fn, ins, params, cts, gdtypes, tm, T, residual=None):
    n_in, n_p, n_ct = len(ins), len(params), len(cts)
    n_res = 0 if residual is None else 1

    def body(*refs):
        i = pl.program_id(0)
        k = 0
        xs = [r[...].astype(F32) for r in refs[k:k + n_in]]
        k += n_in
        ps = [r[...] for r in refs[k:k + n_p]]
        k += n_p
        gs = tuple(r[...].astype(F32) for r in refs[k:k + n_ct])
        k += n_ct
        res_refs = refs[k:k + n_res]
        k += n_res
        gi_refs = refs[k:k + n_in]
        gp_refs = refs[k + n_in:]
        _, vjp = jax.vjp(fn, *xs, *ps)
        g = vjp(gs)
        for j, (ref, v) in enumerate(zip(gi_refs, g[:n_in])):
            if j == 0 and n_res:
                v = v + res_refs[0][...].astype(F32)
            ref[...] = v.astype(ref.dtype)

        @pl.when(i == 0)
        def _():
            for ref in gp_refs:
                ref[...] = jnp.zeros_like(ref)

        for ref, v in zip(gp_refs, g[n_in:]):
            ref[...] += v

    res_in = [] if residual is None else [residual]
    return pl.pallas_call(
        body, name=name, grid=(T // tm,),
        in_specs=([_row_spec(tm, w, cb) for (_, cb, w) in ins] + [_full_spec(p.shape) for p in params]
                  + [_row_spec(tm, w, cb) for (_, cb, w) in cts] + [_row_spec(tm, w, cb) for (_, cb, w) in res_in]),
        out_specs=[_row_spec(tm, w, 0) for (_, _, w) in ins] + [_full_spec(p.shape) for p in params],
        out_shape=([jax.ShapeDtypeStruct((T, w), dt) for (_, _, w), dt in zip(ins, gdtypes)]
                   + [jax.ShapeDtypeStruct(p.shape, F32) for p in params]),
        compiler_params=_ARB,
    )(*[a for (a, _, _) in ins], *params, *[a for (a, _, _) in cts], *[a for (a, _, _) in res_in])


def _matmul(name, a, b, mode, M, N, K, tm, tn, tk, out_dtype, a_off=0, b_off=0, residual=None):
    nk = K // tk
    assert M % tm == 0 and N % tn == 0 and K % tk == 0
    if mode == "nn":
        assert a_off % tk == 0 and b_off % tn == 0
        a_spec = pl.BlockSpec((tm, tk), lambda i, j, k: (i, a_off // tk + k))
        b_spec = pl.BlockSpec((tk, tn), lambda i, j, k: (k, b_off // tn + j))
        dims = (((1,), (0,)), ((), ()))
    elif mode == "nt":
        assert a_off % tk == 0 and b_off == 0
        a_spec = pl.BlockSpec((tm, tk), lambda i, j, k: (i, a_off // tk + k))
        b_spec = pl.BlockSpec((tn, tk), lambda i, j, k: (j, k))
        dims = (((1,), (1,)), ((), ()))
    else:
        assert a_off % tm == 0 and b_off % tn == 0
        a_spec = pl.BlockSpec((tk, tm), lambda i, j, k: (k, a_off // tm + i))
        b_spec = pl.BlockSpec((tk, tn), lambda i, j, k: (k, b_off // tn + j))
        dims = (((0,), (0,)), ((), ()))
    has_res = residual is not None

    def body(*refs):
        a_ref, b_ref = refs[0], refs[1]
        r_ref = refs[2] if has_res else None
        o_ref, acc_ref = refs[-2], refs[-1]
        k = pl.program_id(2)

        @pl.when(k == 0)
        def _():
            acc_ref[...] = jnp.zeros_like(acc_ref)

        acc_ref[...] += lax.dot_general(a_ref[...].astype(BF16), b_ref[...].astype(BF16), dims,
                                        preferred_element_type=F32)

        @pl.when(k == nk - 1)
        def _():
            acc = acc_ref[...]
            if has_res:
                acc = acc + r_ref[...].astype(F32)
            o_ref[...] = acc.astype(o_ref.dtype)

    in_specs = [a_spec, b_spec]
    args = [a, b]
    if has_res:
        in_specs.append(pl.BlockSpec((tm, tn), lambda i, j, k: (i, j)))
        args.append(residual)
    return pl.pallas_call(
        body, name=name, grid=(M // tm, N // tn, nk),
        in_specs=in_specs,
        out_specs=pl.BlockSpec((tm, tn), lambda i, j, k: (i, j)),
        out_shape=jax.ShapeDtypeStruct((M, N), out_dtype),
        scratch_shapes=[pltpu.VMEM((tm, tn), F32)],
        compiler_params=pltpu.CompilerParams(dimension_semantics=("parallel", "parallel", "arbitrary"),
                                             vmem_limit_bytes=VMEM_LIMIT),
    )(*args)


def _f_rms(x, w):
    return (x * lax.rsqrt(jnp.mean(x * x, axis=-1, keepdims=True) + EPS) * w,)


def _silu(x):
    return x * jax.nn.sigmoid(x)


def _softplus(x):
    return jnp.maximum(x, 0.0) + jnp.log1p(jnp.exp(-jnp.abs(x)))


def _heads(x, n):
    return [x[:, HEAD_DIM * h:HEAD_DIM * (h + 1)] for h in range(n)]


def _f_dnpoint(c, ba, alog_v, dtb_v):
    cs = _silu(c)
    q, k, v = cs[:, :D_KEY_B], cs[:, D_KEY_B:2 * D_KEY_B], cs[:, 2 * D_KEY_B:]
    qn = jnp.concatenate([t * lax.rsqrt(jnp.sum(t * t, axis=-1, keepdims=True) + EPS) * (HEAD_DIM ** -0.5)
                          for t in _heads(q, N_HEADS_B)], axis=1)
    kn = jnp.concatenate([t * lax.rsqrt(jnp.sum(t * t, axis=-1, keepdims=True) + EPS)
                          for t in _heads(k, N_HEADS_B)], axis=1)
    lane = lax.broadcasted_iota(jnp.int32, ba.shape, 1)
    beta = jax.nn.sigmoid(ba)
    g = -jnp.exp(alog_v) * _softplus(ba + dtb_v)
    bg = jnp.where(lane < N_HEADS_B, beta, jnp.where(lane < 2 * N_HEADS_B, g, 0.0))
    return qn, kn, v, bg


def _f_postdn(o, z, w):
    outs = []
    for oh, zh in zip(_heads(o, N_HEADS_B), _heads(z, N_HEADS_B)):
        outs.append(oh * lax.rsqrt(jnp.mean(oh * oh, axis=-1, keepdims=True) + EPS) * w * _silu(zh))
    return (jnp.concatenate(outs, axis=1),)


def _f_merge(o0, o1, o2, l0, l1, l2):
    m = lax.stop_gradient(jnp.maximum(jnp.maximum(l0, l1), l2))
    e0, e1, e2 = jnp.exp(l0 - m), jnp.exp(l1 - m), jnp.exp(l2 - m)
    return ((e0 * o0 + e1 * o1 + e2 * o2) / (e0 + e1 + e2),)


def _f_mix(ya, yb, gr):
    return (jax.nn.sigmoid(gr[:, :D_MODEL]) * ya + jax.nn.sigmoid(gr[:, D_MODEL:]) * yb,)


def _f_act(gu):
    return (_silu(gu[:, :D_FF]) * gu[:, D_FF:],)


def _attn_block(q, kp, kc, vp, vc, slope_d, first_key):
    k2 = jnp.concatenate([kp, kc], axis=0).astype(BF16)
    v2 = jnp.concatenate([vp, vc], axis=0).astype(BF16)
    s = lax.dot_general(q.astype(BF16), k2, (((1,), (1,)), ((), ())), preferred_element_type=F32)
    s = s * (HEAD_DIM ** -0.5)
    i = lax.broadcasted_iota(jnp.int32, (BLOCK_A, 2 * BLOCK_A), 0)
    j = lax.broadcasted_iota(jnp.int32, (BLOCK_A, 2 * BLOCK_A), 1)
    delta = BLOCK_A + i - j
    valid = (delta >= 0) & (delta <= W_SUB) & (j >= first_key)
    s = jnp.where(valid, s - slope_d * delta.astype(F32), -jnp.inf)
    m = lax.stop_gradient(jnp.max(s, axis=-1, keepdims=True))
    p = jnp.exp(s - m)
    den = jnp.sum(p, axis=-1, keepdims=True)
    o = jnp.dot(p.astype(BF16), v2, preferred_element_type=F32) / den
    lse = jnp.broadcast_to(m + jnp.log(den), (BLOCK_A, HEAD_DIM))
    return o, lse


def _slope_d(group, h, d):
    hg = (group * 4 + h + 1).astype(F32)
    return jnp.exp(jnp.full((1, 1), -8.0 * math.log(2.0) / N_HEADS_A, F32) * hg) * float(d)


ATT_CFG = ((1, 4, 4), (4, 2, 1), (16, 1, 1))
R_UNROLL = 4


def _attn_geometry(group, T):
    d, m, hp = ATT_CFG[group]
    span = BLOCK_A * d
    tile = m * span
    assert T % tile == 0 and 4 % hp == 0
    return d, m, hp, span, tile, T // tile, HEAD_DIM * hp


def _sub_rows(b, span, r, d):
    return pl.ds(b * span + r, BLOCK_A, stride=d) if d > 1 else pl.ds(b * span, BLOCK_A)


def _for_each_r(d, fn):
    if d <= R_UNROLL:
        for r in range(d):
            fn(r)
    else:
        def step(r, carry):
            fn(r)
            return carry

        lax.fori_loop(0, d, step, 0, unroll=R_UNROLL)


def _attn_fwd(name, proj, group, T):
    d, m, hp, span, tile, nt, lanes = _attn_geometry(group, T)
    qb, kb, vb = [(c + 512 * group) // lanes for c in (C_QA, C_KA, C_VA)]

    def body(q_ref, kc_ref, kp_ref, vc_ref, vp_ref, o_ref, l_ref):
        hb, n = pl.program_id(0), pl.program_id(1)
        first_key = jnp.where(n > 0, 0, BLOCK_A)
        for hh in range(hp):
            ls = slice(HEAD_DIM * hh, HEAD_DIM * (hh + 1))
            sl = _slope_d(group, hb * hp + hh, d)
            for b in range(m):
                def one(r, b=b, ls=ls, sl=sl):
                    rq = _sub_rows(b, span, r, d)
                    if b == 0:
                        rp = _sub_rows(0, span, r, d)
                        kp, vp, fk = kp_ref[rp, ls], vp_ref[rp, ls], first_key
                    else:
                        rp = _sub_rows(b - 1, span, r, d)
                        kp, vp, fk = kc_ref[rp, ls], vc_ref[rp, ls], 0
                    o, lse = _attn_block(q_ref[rq, ls], kp, kc_ref[rq, ls], vp, vc_ref[rq, ls], sl, fk)
                    o_ref[rq, ls] = o
                    l_ref[rq, ls] = lse

                _for_each_r(d, one)

    def cur(cb):
        return pl.BlockSpec((tile, lanes), lambda hb, n: (n, cb + hb))

    def prev(cb):
        return pl.BlockSpec((span, lanes), lambda hb, n: (jnp.maximum(m * n - 1, 0), cb + hb))

    return pl.pallas_call(
        body, name=name, grid=(4 // hp, nt),
        in_specs=[cur(qb), cur(kb), prev(kb), cur(vb), prev(vb)],
        out_specs=[pl.BlockSpec((tile, lanes), lambda hb, n: (n, hb))] * 2,
        out_shape=[jax.ShapeDtypeStruct((T, D_ATTN_OUT), F32)] * 2,
        compiler_params=_cparams(2),
    )(proj, proj, proj, proj, proj)


def _attn_block_bwd(q, kp, kc, vp, vc, o, lse, do, dl, slope_d, first_key):
    scale = HEAD_DIM ** -0.5
    nt_dims = (((1,), (1,)), ((), ()))
    tn_dims = (((0,), (0,)), ((), ()))
    qb = q.astype(BF16)
    k2 = jnp.concatenate([kp, kc], axis=0).astype(BF16)
    v2 = jnp.concatenate([vp, vc], axis=0).astype(BF16)
    dob = do.astype(BF16)
    s = lax.dot_general(qb, k2, nt_dims, preferred_element_type=F32) * scale
    i = lax.broadcasted_iota(jnp.int32, (BLOCK_A, 2 * BLOCK_A), 0)
    j = lax.broadcasted_iota(jnp.int32, (BLOCK_A, 2 * BLOCK_A), 1)
    delta = BLOCK_A + i - j
    valid = (delta >= 0) & (delta <= W_SUB) & (j >= first_key)
    lse_col = jnp.max(lse, axis=-1, keepdims=True)
    p = jnp.exp(jnp.where(valid, s - slope_d * delta.astype(F32), -jnp.inf) - lse_col)
    dp = lax.dot_general(dob, v2, nt_dims, preferred_element_type=F32)
    row = jnp.sum(dl, axis=-1, keepdims=True) - jnp.sum(do * o, axis=-1, keepdims=True)
    ds = (p * (dp + row)).astype(BF16)
    dv2 = lax.dot_general(p.astype(BF16), dob, tn_dims, preferred_element_type=F32)
    dq = jnp.dot(ds, k2, preferred_element_type=F32) * scale
    dk2 = lax.dot_general(ds, qb, tn_dims, preferred_element_type=F32) * scale
    return dq, dk2[:BLOCK_A], dk2[BLOCK_A:], dv2[:BLOCK_A], dv2[BLOCK_A:]


def _attn_bwd(name, proj, o, lse, do, dl, group, T):
    d, m, hp, span, tile, nt, lanes = _attn_geometry(group, T)
    qb, kb, vb = [(c + 512 * group) // lanes for c in (C_QA, C_KA, C_VA)]

    def body(q_ref, kc_ref, kp_ref, vc_ref, vp_ref, o_ref, l_ref, do_ref, dl_ref, dq_ref, dk_ref, dv_ref, ck_ref, cv_ref):
        hb, n = pl.program_id(0), pl.program_id(1)
        first_key = jnp.where(n > 0, 0, BLOCK_A)

        @pl.when(n == 0)
        def _():
            ck_ref[...] = jnp.zeros_like(ck_ref)
            cv_ref[...] = jnp.zeros_like(cv_ref)

        dk_ref[...] = ck_ref[...]
        dv_ref[...] = cv_ref[...]

        @pl.when(n < nt)
        def _():
            for hh in range(hp):
                ls = slice(HEAD_DIM * hh, HEAD_DIM * (hh + 1))
                sl = _slope_d(group, hb * hp + hh, d)
                for b in range(m):
                    def one(r, b=b, ls=ls, sl=sl):
                        rq = _sub_rows(b, span, r, d)
                        if b == 0:
                            rp = _sub_rows(0, span, r, d)
                            kp, vp, fk = kp_ref[rp, ls], vp_ref[rp, ls], first_key
                        else:
                            rp = _sub_rows(b - 1, span, r, d)
                            kp, vp, fk = kc_ref[rp, ls], vc_ref[rp, ls], 0
                        dq, dkp, dkc, dvp, dvc = _attn_block_bwd(
                            q_ref[rq, ls], kp, kc_ref[rq, ls], vp, vc_ref[rq, ls], o_ref[rq, ls], l_ref[rq, ls],
                            do_ref[rq, ls], dl_ref[rq, ls], sl, fk)
                        dq_ref[rq, ls] = dq
                        ck_ref[rq, ls] = dkc
                        cv_ref[rq, ls] = dvc
                        if b == 0:
                            last = _sub_rows(m - 1, span, r, d)
                            dk_ref[last, ls] += dkp
                            dv_ref[last, ls] += dvp
                        else:
                            ck_ref[rp, ls] += dkp
                            cv_ref[rp, ls] += dvp

                    _for_each_r(d, one)

    def cur(cb):
        return pl.BlockSpec((tile, lanes), lambda hb, n: (jnp.minimum(n, nt - 1), cb + hb))

    def prev(cb):
        return pl.BlockSpec((span, lanes), lambda hb, n: (jnp.maximum(m * jnp.minimum(n, nt - 1) - 1, 0), cb + hb))

    own = pl.BlockSpec((tile, lanes), lambda hb, n: (jnp.minimum(n, nt - 1), hb))
    late = pl.BlockSpec((tile, lanes), lambda hb, n: (jnp.maximum(n - 1, 0), hb))
    return pl.pallas_call(
        body, name=name, grid=(4 // hp, nt + 1),
        in_specs=[cur(qb), cur(kb), prev(kb), cur(vb), prev(vb), own, own, own, own],
        out_specs=[own, late, late],
        out_shape=[jax.ShapeDtypeStruct((T, D_ATTN_OUT), F32)] * 3,
        scratch_shapes=[pltpu.VMEM((tile, lanes), F32), pltpu.VMEM((tile, lanes), F32)],
        compiler_params=_cparams(2),
    )(proj, proj, proj, proj, proj, o, lse, do, dl)


HALO = 8
HALO_IN = 16


def _conv_taps(ext, w):
    c = ext[HALO:, :] * w[CONV_WIDTH - 1]
    for k in range(1, CONV_WIDTH):
        c = c + pltpu.roll(ext, k, 0)[HALO:, :] * w[CONV_WIDTH - 1 - k]
    return c


def _taps(w_ref):
    return [w_ref[j:j + 1, :] for j in range(CONV_WIDTH)]


def _dn_pre_fwd(name, proj, ba, conv_w, alog_v, dtb_v, tm, T):
    C = 3 * D_KEY_B

    def body(x_ref, ba_ref, w_ref, al_ref, dt_ref, q_ref, k_ref, v_ref, bg_ref, halo_ref):
        i = pl.program_id(0)

        @pl.when(i == 0)
        def _():
            halo_ref[...] = jnp.zeros_like(halo_ref)

        x = x_ref[...].astype(F32)
        ext = jnp.concatenate([halo_ref[...], x], axis=0)
        c = _conv_taps(ext, _taps(w_ref))
        halo_ref[...] = x[tm - HALO:, :]
        qn, kn, v, bg = _f_dnpoint(c, ba_ref[...], al_ref[...], dt_ref[...])
        q_ref[...] = qn
        k_ref[...] = kn
        v_ref[...] = v
        bg_ref[...] = bg

    return pl.pallas_call(
        body, name=name, grid=(T // tm,),
        in_specs=[_row_spec(tm, C, 0), _row_spec(tm, 128, 0), _full_spec(conv_w.shape), _full_spec((1, 128)),
                  _full_spec((1, 128))],
        out_specs=[_row_spec(tm, D_KEY_B, 0)] * 3 + [_row_spec(tm, 128, 0)],
        out_shape=[jax.ShapeDtypeStruct((T, D_KEY_B), F32)] * 3 + [jax.ShapeDtypeStruct((T, 128), F32)],
        scratch_shapes=[pltpu.VMEM((HALO, C), F32)],
        compiler_params=_ARB,
    )(proj, ba, conv_w, alog_v, dtb_v)


def _dn_pre_bwd(name, proj, ba, conv_w, alog_v, dtb_v, dq, dk, dv, dbg, tm, T):
    C = 3 * D_KEY_B
    nt = T // tm
    hb = tm // HALO_IN

    def body(x_ref, xh_ref, ba_ref, w_ref, al_ref, dt_ref, dq_ref, dk_ref, dv_ref, dbg_ref,
             dx_ref, dba_ref, dw_ref, dal_ref, ddt_ref, nxt_ref):
        s = pl.program_id(0)
        i = nt - 1 - s

        @pl.when(s == 0)
        def _():
            nxt_ref[...] = jnp.zeros_like(nxt_ref)
            dw_ref[...] = jnp.zeros_like(dw_ref)
            dal_ref[...] = jnp.zeros_like(dal_ref)
            ddt_ref[...] = jnp.zeros_like(ddt_ref)

        w = _taps(w_ref)
        halo = jnp.where(i > 0, xh_ref[...].astype(F32)[HALO_IN - HALO:, :], 0.0)
        ext = jnp.concatenate([halo, x_ref[...].astype(F32)], axis=0)
        c = _conv_taps(ext, w)
        _, vjp = jax.vjp(_f_dnpoint, c, ba_ref[...], al_ref[...], dt_ref[...])
        dc, dba, dal, ddt = vjp((dq_ref[...], dk_ref[...], dv_ref[...], dbg_ref[...]))
        dba_ref[...] = dba
        dal_ref[...] += dal
        ddt_ref[...] += ddt
        ext2 = jnp.concatenate([dc, nxt_ref[...]], axis=0)
        dx = dc * w[CONV_WIDTH - 1]
        for m in range(1, CONV_WIDTH):
            dx = dx + pltpu.roll(ext2, tm + HALO - m, 0)[:tm, :] * w[CONV_WIDTH - 1 - m]
        dx_ref[...] = dx.astype(dx_ref.dtype)
        nxt_ref[...] = dc[:HALO, :]
        for j in range(CONV_WIDTH):
            k = CONV_WIDTH - 1 - j
            xs = ext[HALO:, :] if k == 0 else pltpu.roll(ext, k, 0)[HALO:, :]
            dw_ref[j:j + 1, :] += jnp.sum(xs * dc, axis=0, keepdims=True)

    def rev(w):
        return pl.BlockSpec((tm, w), lambda s: (nt - 1 - s, 0))

    return pl.pallas_call(
        body, name=name, grid=(nt,),
        in_specs=[rev(C), pl.BlockSpec((HALO_IN, C), lambda s: (jnp.maximum((nt - 1 - s) * hb - 1, 0), 0)), rev(128),
                  _full_spec(conv_w.shape), _full_spec((1, 128)), _full_spec((1, 128)),
                  rev(D_KEY_B), rev(D_KEY_B), rev(D_KEY_B), rev(128)],
        out_specs=[rev(C), rev(128), _full_spec((8, C)), _full_spec((1, 128)), _full_spec((1, 128))],
        out_shape=[jax.ShapeDtypeStruct((T, C), BF16), jax.ShapeDtypeStruct((T, 128), F32),
                   jax.ShapeDtypeStruct((8, C), F32), jax.ShapeDtypeStruct((1, 128), F32),
                   jax.ShapeDtypeStruct((1, 128), F32)],
        scratch_shapes=[pltpu.VMEM((HALO, C), F32)],
        compiler_params=_ARB,
    )(proj, proj, ba, conv_w, alog_v, dtb_v, dq, dk, dv, dbg)


def _bdot(a, b, dims, exact=False):
    if exact:
        return lax.dot_general(a, b, dims, preferred_element_type=F32, precision=lax.Precision.HIGH)
    return lax.dot_general(a.astype(BF16), b.astype(BF16), dims, preferred_element_type=F32)


_B_NN = (((2,), (1,)), ((0,), (0,)))
_B_NT = (((2,), (2,)), ((0,), (0,)))
_B_TN = (((1,), (1,)), ((0,), (0,)))


def _unit_lower_inverse(a):
    C = a.shape[-1]
    r = lax.broadcasted_iota(jnp.int32, (C, C), 0)
    c = lax.broadcasted_iota(jnp.int32, (C, C), 1)
    pw = -a
    t = jnp.where((r == c)[None], 1.0, 0.0) + pw
    for _ in range(int(math.log2(C)) - 1):
        pw = _bdot(pw, pw, _B_NN, exact=True)
        t = t + _bdot(t, pw, _B_NN, exact=True)
    return t


@jax.custom_vjp
def _solve_given_inverse(t, a, rhs):
    return _bdot(t, rhs, _B_NN, exact=True)


def _solve_fwd(t, a, rhs):
    sol = _bdot(t, rhs, _B_NN, exact=True)
    return sol, (t, sol)


def _solve_bwd(res, dsol):
    t, sol = res
    drhs = _bdot(t, dsol, _B_TN, exact=True)
    return jnp.zeros_like(t), -_bdot(drhs, sol, _B_NT, exact=True), drhs


_solve_given_inverse.defvjp(_solve_fwd, _solve_bwd)


def _stack(xs):
    return jnp.concatenate([x[None] for x in xs], axis=0)


def _delta_chunk(S, q, k, v, bg, t_inv=None):
    H, C = N_HEADS_B, CHUNK
    r = lax.broadcasted_iota(jnp.int32, (C, C), 0)
    c = lax.broadcasted_iota(jnp.int32, (C, C), 1)
    incl, strict, eye = r >= c, r > c, r == c
    gc_all = jnp.dot(incl.astype(F32), bg, preferred_element_type=F32, precision=lax.Precision.HIGHEST)
    lane = lax.broadcasted_iota(jnp.int32, (C, 128), 1)

    def col(x, l):
        return jnp.sum(jnp.where(lane == l, x, 0.0), axis=1, keepdims=True)

    beta = _stack([col(bg, h) for h in range(H)])
    gc = _stack([col(gc_all, H + h) for h in range(H)])
    gc_row = jnp.sum(jnp.where(eye[None], jnp.broadcast_to(gc, (H, C, C)), 0.0), axis=1, keepdims=True)
    row = lax.broadcasted_iota(jnp.int32, (H, C, 1), 1)
    gc_last = jnp.sum(jnp.where(row == C - 1, gc, 0.0), axis=1, keepdims=True)
    decay = jnp.exp(jnp.where(incl[None], gc - gc_row, -jnp.inf))
    qh = _stack(_heads(q, H))
    kh = _stack(_heads(k, H))
    vh = _stack(_heads(v, H))
    kk = _bdot(kh, kh, _B_NT, exact=True)
    a = jnp.where(strict[None], beta * kk * decay, 0.0)
    egc = jnp.exp(gc)
    rhs = jnp.concatenate([beta * vh, (beta * egc) * kh], axis=-1)
    if t_inv is None:
        t_inv = _unit_lower_inverse(a)
        sol = _bdot(t_inv, rhs, _B_NN, exact=True)
    else:
        sol = _solve_given_inverse(t_inv, a, rhs)
    u_bar, w = sol[..., :HEAD_DIM], sol[..., HEAD_DIM:]
    qk = _bdot(qh, kh, _B_NT) * decay
    q_dec = qh * egc
    k_dec = kh * jnp.exp(gc_last - gc)
    u = u_bar - _bdot(w, S, _B_NN)
    o = _bdot(q_dec, S, _B_NN) + _bdot(qk, u, _B_NN)
    S_new = jnp.exp(gc_last) * S + _bdot(k_dec, u, _B_TN)
    o2 = jnp.concatenate([lax.index_in_dim(o, h, 0, keepdims=False) for h in range(H)], axis=1)
    return o2, S_new, t_inv


def _delta_fwd(name, q, k, v, bg, T):
    nc = T // CHUNK
    H = N_HEADS_B

    def body(q_ref, k_ref, v_ref, bg_ref, o_ref, s_out_ref, t_out_ref, s_ref):
        @pl.when(pl.program_id(0) == 0)
        def _():
            s_ref[...] = jnp.zeros_like(s_ref)

        S = s_ref[...]
        s_out_ref[0] = S
        o, S_new, t_inv = _delta_chunk(S, q_ref[...], k_ref[...], v_ref[...], bg_ref[...])
        o_ref[...] = o
        t_out_ref[0] = t_inv
        s_ref[...] = S_new

    return pl.pallas_call(
        body, name=name, grid=(nc,),
        in_specs=[_row_spec(CHUNK, D_KEY_B, 0)] * 3 + [_row_spec(CHUNK, 128, 0)],
        out_specs=[_row_spec(CHUNK, D_KEY_B, 0), pl.BlockSpec((1, H, HEAD_DIM, HEAD_DIM), lambda i: (i, 0, 0, 0)),
                   pl.BlockSpec((1, H, CHUNK, CHUNK), lambda i: (i, 0, 0, 0))],
        out_shape=[jax.ShapeDtypeStruct((T, D_KEY_B), F32), jax.ShapeDtypeStruct((nc, H, HEAD_DIM, HEAD_DIM), F32),
                   jax.ShapeDtypeStruct((nc, H, CHUNK, CHUNK), F32)],
        scratch_shapes=[pltpu.VMEM((H, HEAD_DIM, HEAD_DIM), F32)],
        compiler_params=_ARB,
    )(q, k, v, bg)


def _delta_bwd(name, q, k, v, bg, s_all, t_all, do, T):
    nc = T // CHUNK
    H = N_HEADS_B

    def body(q_ref, k_ref, v_ref, bg_ref, s_in_ref, t_ref, do_ref, dq_ref, dk_ref, dv_ref, dbg_ref, ds_ref):
        @pl.when(pl.program_id(0) == 0)
        def _():
            ds_ref[...] = jnp.zeros_like(ds_ref)

        t_inv = t_ref[0]

        def f(S, qq, kk, vv, bb):
            return _delta_chunk(S, qq, kk, vv, bb, t_inv)[:2]

        _, vjp = jax.vjp(f, s_in_ref[0], q_ref[...], k_ref[...], v_ref[...], bg_ref[...])
        dS, dq, dk, dv, dbg = vjp((do_ref[...], ds_ref[...]))
        dq_ref[...] = dq
        dk_ref[...] = dk
        dv_ref[...] = dv
        dbg_ref[...] = dbg
        ds_ref[...] = dS

    def rev(w):
        return pl.BlockSpec((CHUNK, w), lambda s: (nc - 1 - s, 0))

    def rev4(n):
        return pl.BlockSpec((1, H, n, n), lambda s: (nc - 1 - s, 0, 0, 0))

    return pl.pallas_call(
        body, name=name, grid=(nc,),
        in_specs=[rev(D_KEY_B)] * 3 + [rev(128), rev4(HEAD_DIM), rev4(CHUNK), rev(D_KEY_B)],
        out_specs=[rev(D_KEY_B)] * 3 + [rev(128)],
        out_shape=[jax.ShapeDtypeStruct((T, D_KEY_B), F32)] * 3 + [jax.ShapeDtypeStruct((T, 128), F32)],
        scratch_shapes=[pltpu.VMEM((H, HEAD_DIM, HEAD_DIM), F32)],
        compiler_params=_ARB,
    )(q, k, v, bg, s_all, t_all, do)


def _loss_fb(name, h2, w, tgt, tm, T):
    def tile_loss(h, wv, t):
        y = _f_rms(h, wv)[0]
        e = y - t
        return 0.5 * jnp.sum(jnp.mean(e * e, axis=-1))

    def body(h_ref, w_ref, t_ref, dh_ref, loss_ref, dw_ref):
        @pl.when(pl.program_id(0) == 0)
        def _():
            loss_ref[...] = jnp.zeros_like(loss_ref)
            dw_ref[...] = jnp.zeros_like(dw_ref)

        val, (dh, dw) = jax.value_and_grad(tile_loss, argnums=(0, 1))(h_ref[...], w_ref[...], t_ref[...])
        dh_ref[...] = dh
        dw_ref[...] += dw
        loss_ref[...] += jnp.reshape(val, (1, 1))

    return pl.pallas_call(
        body, name=name, grid=(T // tm,),
        in_specs=[_row_spec(tm, D_MODEL, 0), _full_spec((1, D_MODEL)), _row_spec(tm, D_MODEL, 0)],
        out_specs=[_row_spec(tm, D_MODEL, 0), _full_spec((1, 1)), _full_spec((1, D_MODEL))],
        out_shape=[jax.ShapeDtypeStruct((T, D_MODEL), F32), jax.ShapeDtypeStruct((1, 1), F32),
                   jax.ShapeDtypeStruct((1, D_MODEL), F32)],
        compiler_params=_ARB,
    )(h2, w, tgt)


def _adamw(name, w, g, m, v, tile):
    R = w.shape[0]

    def body(w_ref, g_ref, m_ref, v_ref, d_out, m_out, v_out):
        g = g_ref[...]
        mm = ADAM_B1 * m_ref[...] + (1.0 - ADAM_B1) * g
        vv = ADAM_B2 * v_ref[...] + (1.0 - ADAM_B2) * jnp.square(g)
        m_hat = mm / (1.0 - ADAM_B1 ** ADAM_STEP)
        v_hat = vv / (1.0 - ADAM_B2 ** ADAM_STEP)
        d_out[...] = -ADAM_LR * (m_hat / (jnp.sqrt(v_hat) + ADAM_EPS) + ADAM_WD * w_ref[...])
        m_out[...] = mm
        v_out[...] = vv

    assert R % tile == 0
    spec = pl.BlockSpec((tile, w.shape[1]), lambda i: (i, 0))
    return pl.pallas_call(
        body, name=name, grid=(R // tile,), in_specs=[spec] * 4, out_specs=[spec] * 3,
        out_shape=[jax.ShapeDtypeStruct(w.shape, F32)] * 3, compiler_params=_ARB,
    )(w, g, m, v)


def _sum4(name, r, tile):
    _, R, C = r.shape
    assert R % tile == 0

    def body(r_ref, o_ref):
        o_ref[...] = ((r_ref[0].astype(F32) + r_ref[1].astype(F32)) + r_ref[2].astype(F32)) + r_ref[3].astype(F32)

    return pl.pallas_call(
        body, name=name, grid=(R // tile,),
        in_specs=[pl.BlockSpec((4, tile, C), lambda i: (0, i, 0))],
        out_specs=pl.BlockSpec((tile, C), lambda i: (i, 0)),
        out_shape=jax.ShapeDtypeStruct((R, C), F32), compiler_params=_ARB,
    )(r)


def _add2(name, a, b, tile):
    R, C = a.shape
    assert R % tile == 0

    def body(a_ref, b_ref, o_ref):
        o_ref[...] = (a_ref[...].astype(F32) + b_ref[...].astype(F32)).astype(BF16)

    spec = pl.BlockSpec((tile, C), lambda i: (i, 0))
    return pl.pallas_call(
        body, name=name, grid=(R // tile,), in_specs=[spec, spec], out_specs=spec,
        out_shape=jax.ShapeDtypeStruct((R, C), BF16), compiler_params=_ARB,
    )(a, b)


_ANY = pl.BlockSpec(memory_space=pl.ANY)


def _place():
    return lax.axis_index("x"), lax.axis_index("y"), lax.axis_index("c")


def _exchange_chips(name, arrays, gather):
    n = len(arrays)

    def body(*refs):
        in_refs, out_refs = refs[:n], refs[n:2 * n]
        send_sems, recv_sems, local_sems = refs[2 * n:]
        x, y, c = _place()
        me = 2 * x + y
        chips = [(1 - x, y), (x, 1 - y), (1 - x, 1 - y)]

        def src(a, slot):
            return in_refs[a] if gather else in_refs[a].at[slot]

        def copy(a, k, src_slot, dst_slot, px, py):
            return pltpu.make_async_remote_copy(src_ref=src(a, src_slot), dst_ref=out_refs[a].at[dst_slot],
                                                send_sem=send_sems.at[3 * a + k], recv_sem=recv_sems.at[3 * a + k],
                                                device_id=(px, py, c), device_id_type=MESH)

        mine = [pltpu.make_async_copy(src(a, me), out_refs[a].at[me], local_sems.at[a]) for a in range(n)]
        for cp in mine:
            cp.start()
        sends = [copy(a, k, 2 * px + py, me, px, py) for a in range(n) for k, (px, py) in enumerate(chips)]
        for cp in sends:
            cp.start()
        for a in range(n):
            for k, (px, py) in enumerate(chips):
                copy(a, k, me, 2 * px + py, px, py).wait_recv()
        for cp in sends:
            cp.wait_send()
        for cp in mine:
            cp.wait()

    return pl.pallas_call(
        body, name=name, in_specs=[_ANY] * n, out_specs=[_ANY] * n,
        out_shape=[jax.ShapeDtypeStruct(((4,) + a.shape) if gather else a.shape, a.dtype) for a in arrays],
        scratch_shapes=[pltpu.SemaphoreType.DMA((3 * n,)), pltpu.SemaphoreType.DMA((3 * n,)), pltpu.SemaphoreType.DMA((n,))],
    )(*arrays)


def _swap_sibling(name, arrays):
    n = len(arrays)

    def body(*refs):
        in_refs, out_refs, send_sems, recv_sems = refs[:n], refs[n:2 * n], refs[2 * n], refs[2 * n + 1]
        x, y, c = _place()
        cps = [pltpu.make_async_remote_copy(src_ref=in_refs[a], dst_ref=out_refs[a], send_sem=send_sems.at[a],
                                            recv_sem=recv_sems.at[a], device_id=(x, y, 1 - c), device_id_type=MESH)
               for a in range(n)]
        for cp in cps:
            cp.start()
        for cp in cps:
            cp.wait()

    return pl.pallas_call(
        body, name=name, in_specs=[_ANY] * n, out_specs=[_ANY] * n,
        out_shape=[jax.ShapeDtypeStruct(a.shape, a.dtype) for a in arrays],
        scratch_shapes=[pltpu.SemaphoreType.DMA((n,)), pltpu.SemaphoreType.DMA((n,))],
    )(*arrays)


def _allsum_small(name, pack):
    def body(p_ref, out_ref, slots, send_sems, recv_sems):
        x, y, c = _place()
        me = 4 * x + 2 * y + c
        slots[me] = p_ref[...]
        sends = []
        for k in range(1, 8):
            px = 1 - x if k & 4 else x
            py = 1 - y if k & 2 else y
            pc = 1 - c if k & 1 else c
            cp = pltpu.make_async_remote_copy(src_ref=p_ref, dst_ref=slots.at[me], send_sem=send_sems.at[k - 1],
                                              recv_sem=recv_sems.at[k - 1], device_id=(px, py, pc), device_id_type=MESH)
            cp.start()
            sends.append(cp)
        for k in range(1, 8):
            px = 1 - x if k & 4 else x
            py = 1 - y if k & 2 else y
            pc = 1 - c if k & 1 else c
            pltpu.make_async_remote_copy(src_ref=p_ref, dst_ref=slots.at[4 * px + 2 * py + pc],
                                         send_sem=send_sems.at[k - 1], recv_sem=recv_sems.at[k - 1],
                                         device_id=(px, py, pc), device_id_type=MESH).wait_recv()
        for cp in sends:
            cp.wait_send()
        acc = slots[0]
        for s in range(1, 8):
            acc = acc + slots[s]
        out_ref[...] = acc

    vm = pl.BlockSpec(memory_space=pltpu.VMEM)
    return pl.pallas_call(
        body, name=name, in_specs=[vm], out_specs=vm,
        out_shape=jax.ShapeDtypeStruct(pack.shape, pack.dtype),
        scratch_shapes=[pltpu.VMEM((8,) + pack.shape, pack.dtype), pltpu.SemaphoreType.DMA((7,)),
                        pltpu.SemaphoreType.DMA((7,))],
    )(pack)


def _pack_shards(parts, rows_total):
    flat = [p.reshape(-1, D_MODEL) for p in parts]
    used = sum(f.shape[0] for f in flat)
    return jnp.concatenate(flat + [jnp.zeros((rows_total - used, D_MODEL), flat[0].dtype)], axis=0)


def _unpack_shards(buf, shapes):
    out, r0 = [], 0
    for shp in shapes:
        n = math.prod(shp) // D_MODEL
        out.append(buf[r0:r0 + n].reshape(shp))
        r0 += n
    return out


def kernel(x, norm_mix, w_in, conv_w, a_log, dt_bias, dn_norm, w_proj_attn, w_proj_delta, w_out, norm_ffn, w_gate, w_up, w_down, norm_final, loss_target, m_norm_mix, m_w_in, m_conv_w, m_a_log, m_dt_bias, m_dn_norm, m_w_proj_attn, m_w_proj_delta, m_w_out, m_norm_ffn, m_w_gate, m_w_up, m_w_down, m_norm_final, v_norm_mix, v_w_in, v_conv_w, v_a_log, v_dt_bias, v_dn_norm, v_w_proj_attn, v_w_proj_delta, v_w_out, v_norm_ffn, v_w_gate, v_w_up, v_w_down, v_norm_final):
    T = x.shape[1]
    D = D_MODEL
    tm = 512
    tw = 256
    mt = 1024 if T % 1024 == 0 else 512
    x2 = x.reshape(T, D)
    tgt = loss_target.reshape(T, D)

    packed = (w_proj_attn, w_proj_delta, w_out, w_gate, w_up, w_down)
    conv_bits = lax.bitcast_convert_type(conv_w[0], BF16)
    w16 = _pack_shards([w[0].astype(BF16) for w in packed] + [conv_bits], PACK_R)
    win16 = w_in[0].astype(BF16)
    my_c = lax.axis_index("c")

    def both_halves(mine, other, axis):
        return jnp.concatenate([jnp.where(my_c == 0, mine, other), jnp.where(my_c == 0, other, mine)], axis=axis)

    w_mine = _exchange_chips("gather_weights", [lax.dynamic_slice_in_dim(w16, my_c * HALF, HALF, axis=0),
                                                lax.dynamic_slice_in_dim(win16, my_c * W_IN_HALF, W_IN_HALF, axis=0)], True)
    w_other = _swap_sibling("swap_weights", w_mine)
    wg = both_halves(w_mine[0], w_other[0], 1)
    win4 = both_halves(w_mine[1], w_other[1], 1)

    def seg(r0, n):
        return wg[:, r0:r0 + n, :]

    def win_cols(g0, g1):
        out = []
        for j in range(4):
            lo, hi = max(g0, W_IN_COLS * j), min(g1, W_IN_COLS * (j + 1))
            if lo < hi:
                out.append(win4[j][:, lo - W_IN_COLS * j:hi - W_IN_COLS * j])
        return out

    r = 0
    Wpa = seg(r, 128).reshape(4, D_ATTN_OUT, 256).transpose(1, 0, 2).reshape(D_ATTN_OUT, D); r += 128
    Wpd = seg(r, 256).reshape(D, D); r += 256
    Wout = seg(r, 256).reshape(D, D); r += 256
    Wg = seg(r, 704).reshape(4, D, 704).transpose(1, 0, 2).reshape(D, D_FF); r += 704
    Wu = seg(r, 704).reshape(4, D, 704).transpose(1, 0, 2).reshape(D, D_FF); r += 704
    Wd = seg(r, 704).reshape(D_FF, D); r += 704
    convw = lax.bitcast_convert_type(seg(r, 6).reshape(4, CONV_WIDTH, 768, 2), F32)
    convw = convw.transpose(1, 0, 2).reshape(CONV_WIDTH, 3 * D_KEY_B)
    Wa = jnp.concatenate(win_cols(0, D_PA), axis=1)
    Wb = jnp.concatenate(win_cols(4608, 8704) + win_cols(8720, D_IN), axis=1)
    Wba = jnp.pad(jnp.concatenate(win_cols(8704, 8720), axis=1), ((0, 0), (0, 112)))
    Wgu = jnp.concatenate([Wg, Wu], axis=1)
    alog_v = jnp.pad(a_log, ((0, 0), (8, 112)))
    dtb_v = jnp.pad(dt_bias, ((0, 0), (8, 112)))

    (u,) = _rowwise("rms_mix", _f_rms, [(x2, 0, D)], [norm_mix], [(D, BF16)], tm, T)
    proj_a = _matmul("mm_in_a", u, Wa, "nn", T, D_PA, D, mt, 1536, D, F32)
    proj_b = _matmul("mm_in_b", u, Wb, "nn", T, D_PB, D, mt, 1536, D, BF16)
    ba = _matmul("mm_ba", u, Wba, "nn", T, 128, D, mt, 128, D, F32)

    att = [_attn_fwd(f"attn_fwd{g}", proj_a, g, T) for g in range(3)]
    (ya_m,) = _rowwise("merge", _f_merge, [(att[g][0], 0, 512) for g in range(3)] + [(att[g][1], 0, 512) for g in range(3)],
                       [], [(512, BF16)], tm, T)
    qn, kn, vd, bg = _dn_pre_fwd("dn_pre_fwd", proj_b, ba, convw, alog_v, dtb_v, tw, T)
    o_d, s_all, t_all = _delta_fwd("delta_fwd", qn, kn, vd, bg, T)
    (od,) = _rowwise("post_dn", _f_postdn, [(o_d, 0, D), (proj_b, C_Z // D, D)], [dn_norm], [(D, BF16)], tm, T)
    y_a = _matmul("mm_pa", ya_m, Wpa, "nn", T, D, 512, mt, D, 512, F32)
    y_b = _matmul("mm_pd", od, Wpd, "nn", T, D, D, mt, D, D, F32)
    (mix,) = _rowwise("mix", _f_mix, [(y_a, 0, D), (y_b, 0, D), (proj_b, C_GATE // 2048, 2048)], [], [(D, BF16)], tm, T)
    h1 = _matmul("mm_out", mix, Wout, "nn", T, D, D, mt, D, D, F32, residual=x2)
    (hn,) = _rowwise("rms_ffn", _f_rms, [(h1, 0, D)], [norm_ffn], [(D, BF16)], tm, T)
    gu = _matmul("mm_gu", hn, Wgu, "nn", T, 2 * D_FF, D, mt, 1408, D, BF16)
    (act,) = _rowwise("act", _f_act, [(gu, 0, 2 * D_FF)], [], [(D_FF, BF16)], tw, T)
    h2 = _matmul("mm_down", act, Wd, "nn", T, D, D_FF, mt, D, 1408, F32, residual=h1)
    dh2, loss, g_norm_final = _loss_fb("loss", h2, norm_final.reshape(1, D), tgt, tm, T)

    dact = _matmul("mm_dact", dh2, Wd, "nt", T, D_FF, D, mt, 1408, D, BF16)
    gWd = _matmul("mm_gwd", act, dh2, "tn", D_FF, D, T, 1408, D, mt, BF16)
    (dgu,) = _rowwise_vjp("act_bwd", _f_act, [(gu, 0, 2 * D_FF)], [], [(dact, 0, D_FF)], [BF16], tw, T)
    dhn = _matmul("mm_dhn", dgu, Wgu, "nt", T, D, 2 * D_FF, mt, D, 1408, F32)
    gWgu = _matmul("mm_gwgu", hn, dgu, "tn", D, 2 * D_FF, T, D, 1408, mt, BF16)
    dh1, g_norm_ffn = _rowwise_vjp("rms_ffn_bwd", _f_rms, [(h1, 0, D)], [norm_ffn], [(dhn, 0, D)], [F32], tm, T,
                                   residual=(dh2, 0, D))
    dmix = _matmul("mm_dmix", dh1, Wout, "nt", T, D, D, mt, D, D, F32)
    gWout = _matmul("mm_gwout", mix, dh1, "tn", D, D, T, D, D, mt, BF16)
    dya, dyb, dgate = _rowwise_vjp("mix_bwd", _f_mix, [(y_a, 0, D), (y_b, 0, D), (proj_b, C_GATE // 2048, 2048)], [],
                                   [(dmix, 0, D)], [BF16, BF16, BF16], tm, T)
    dya_m = _matmul("mm_dyam", dya, Wpa, "nt", T, 512, D, mt, 512, D, F32)
    gWpa = _matmul("mm_gwpa", ya_m, dya, "tn", 512, D, T, 512, D, mt, BF16)
    dod = _matmul("mm_dod", dyb, Wpd, "nt", T, D, D, mt, D, D, F32)
    gWpd = _matmul("mm_gwpd", od, dyb, "tn", D, D, T, D, D, mt, BF16)
    do_d, dz, g_dn_norm = _rowwise_vjp("post_dn_bwd", _f_postdn, [(o_d, 0, D), (proj_b, C_Z // D, D)], [dn_norm],
                                       [(dod, 0, D)], [F32, BF16], tm, T)
    dqn, dkn, dvd, dbg = _delta_bwd("delta_bwd", qn, kn, vd, bg, s_all, t_all, do_d, T)
    dqkvd, dba, g_conv8, g_alog_v, g_dtb_v = _dn_pre_bwd("dn_pre_bwd", proj_b, ba, convw, alog_v, dtb_v,
                                                         dqn, dkn, dvd, dbg, tw, T)
    mg = _rowwise_vjp("merge_bwd", _f_merge,
                      [(att[g][0], 0, 512) for g in range(3)] + [(att[g][1], 0, 512) for g in range(3)], [],
                      [(dya_m, 0, 512)], [F32] * 6, tm, T)
    datt = [_attn_bwd(f"attn_bwd{g}", proj_a, att[g][0], att[g][1], mg[g], mg[3 + g], g, T) for g in range(3)]
    dproj_a = jnp.concatenate([datt[g][j].astype(BF16) for j in range(3) for g in range(3)], axis=1)
    dproj_b = jnp.concatenate([dqkvd, dz, dgate], axis=1)
    gWa = _matmul("mm_gwa", u, dproj_a, "tn", D, D_PA, T, D, 1536, mt, BF16)
    gWb = _matmul("mm_gwb", u, dproj_b, "tn", D, D_PB, T, D, 1536, mt, BF16)
    gWba = _matmul("mm_gwba", u, dba, "tn", D, 128, T, D, 128, mt, BF16)
    du0 = _matmul("mm_du_ba", dba, Wba, "nt", T, D, 128, mt, D, 128, F32)
    du1 = _matmul("mm_du_b", dproj_b, Wb, "nt", T, D, D_PB, mt, D, 1536, F32, residual=du0)
    du = _matmul("mm_du_a", dproj_a, Wa, "nt", T, D, D_PA, mt, D, 1536, F32, residual=du1)
    dx, g_norm_mix = _rowwise_vjp("rms_mix_bwd", _f_rms, [(x2, 0, D)], [norm_mix], [(du, 0, D)], [F32], tm, T,
                                  residual=(dh1, 0, D))

    pieces = ((gWa, 0), (gWb[:, :C_GATE], D_PA), (gWba[:, :16], 8704), (gWb[:, C_GATE:], 8720))

    def win_shard(j):
        out = []
        for arr, g0 in pieces:
            lo, hi = max(g0, W_IN_COLS * j), min(g0 + arr.shape[1], W_IN_COLS * (j + 1))
            if lo < hi:
                out.append(arr[:, lo - g0:hi - g0])
        return jnp.concatenate(out, axis=1)

    g_win4 = jnp.concatenate([win_shard(j)[None] for j in range(4)], axis=0)

    def cols4(g, n):
        return g.reshape(g.shape[0], 4, n).transpose(1, 0, 2).reshape(4, -1, D)

    parts = [cols4(gWpa, 256), gWpd.reshape(4, 256, D), gWout.reshape(4, 256, D), cols4(gWgu[:, :D_FF], 704),
             cols4(gWgu[:, D_FF:], 704), gWd.reshape(4, 704, D), cols4(g_conv8[:CONV_WIDTH], 768)]
    used = sum(p.shape[1] for p in parts)
    g_pack = jnp.concatenate([p.astype(BF16) for p in parts] + [jnp.zeros((4, PACK_R - used, D), BF16)], axis=1)
    halves = ((g_pack, HALF), (g_win4, W_IN_HALF))
    g_keep = [lax.dynamic_slice_in_dim(g, my_c * h, h, axis=1) for g, h in halves]
    g_give = [lax.dynamic_slice_in_dim(g, (1 - my_c) * h, h, axis=1) for g, h in halves]
    g_got = _swap_sibling("swap_grads", g_give)
    g_pair = [_add2(f"add_cores{i}", a.reshape(-1, a.shape[2]), b.reshape(-1, a.shape[2]), HALF_TILE).reshape(a.shape)
              for i, (a, b) in enumerate(zip(g_keep, g_got))]
    recv = _exchange_chips("scatter_grads", g_pair, False)
    g_mine = [_sum4(f"sum_chips{i}", rv, HALF_TILE) for i, rv in enumerate(recv)]
    g_other = _swap_sibling("swap_sums", g_mine)
    g_all = both_halves(g_mine[0], g_other[0], 0)
    g_win = both_halves(g_mine[1], g_other[1], 0)

    locals_ = (w_proj_attn, w_proj_delta, w_out, w_gate, w_up, w_down, conv_w, w_in)
    ms = (m_w_proj_attn, m_w_proj_delta, m_w_out, m_w_gate, m_w_up, m_w_down, m_conv_w, m_w_in)
    vs = (v_w_proj_attn, v_w_proj_delta, v_w_out, v_w_gate, v_w_up, v_w_down, v_conv_w, v_w_in)
    big_names = ("w_proj_attn", "w_proj_delta", "w_out", "w_gate", "w_up", "w_down", "conv_w", "w_in")
    adam_tiles = (512, 256, 256, 256, 256, 352, CONV_WIDTH, 128)
    g_shards = _unpack_shards(g_all, [w.shape for w in locals_[:-1]]) + [g_win.reshape(w_in.shape)]
    big = [g_shards, [], [], []]
    for nm, w, g, m_, v_, tl in zip(big_names, locals_, g_shards, ms, vs, adam_tiles):
        res = _adamw("adamw_" + nm, w[0], g[0], m_[0], v_[0], tl)
        for kind in range(3):
            big[kind + 1].append(res[kind].reshape(w.shape))

    def small_pack(nm, nf, nfin, dn, al, dt, extra):
        row3 = jnp.concatenate([dn.reshape(1, 128), al.reshape(1, 8), dt.reshape(1, 8), extra.reshape(1, 1),
                                jnp.zeros((1, D - 145), F32)], axis=1)
        return jnp.concatenate([nm.reshape(1, D), nf.reshape(1, D), nfin.reshape(1, D), row3, jnp.zeros((4, D), F32)], axis=0)

    zero1 = jnp.zeros((1, 1), F32)
    sp = small_pack(g_norm_mix, g_norm_ffn, g_norm_final, g_dn_norm, g_alog_v[:, 8:16], g_dtb_v[:, 8:16], loss)
    tot = _allsum_small("allsum_small", sp)
    sm = _adamw("adamw_small", small_pack(norm_mix, norm_ffn, norm_final, dn_norm, a_log, dt_bias, zero1), tot,
                small_pack(m_norm_mix, m_norm_ffn, m_norm_final, m_dn_norm, m_a_log, m_dt_bias, zero1),
                small_pack(v_norm_mix, v_norm_ffn, v_norm_final, v_dn_norm, v_a_log, v_dt_bias, zero1), 8)
    sm = [tot] + list(sm)

    def small_unpack(p):
        return {"norm_mix": p[0:1], "norm_ffn": p[1:2], "norm_final": p[2], "dn_norm": p[3:4, :128],
                "a_log": p[3:4, 128:136], "dt_bias": p[3:4, 136:144]}

    sm = [small_unpack(p) for p in sm]
    loss_out = tot[3, 144]

    order = ("norm_mix", "w_in", "conv_w", "a_log", "dt_bias", "dn_norm", "w_proj_attn", "w_proj_delta", "w_out",
             "norm_ffn", "w_gate", "w_up", "w_down", "norm_final")
    outs = [loss_out, dx.reshape(x.shape)]
    for kind in range(4):
        for n in order:
            outs.append(big[kind][big_names.index(n)] if n in big_names else sm[kind][n])
    return tuple(outs)
```

```python
import functools
import math

import jax
import jax.numpy as jnp
from jax import lax
from jax.experimental import pallas as pl
from jax.experimental.pallas import tpu as pltpu

F32 = jnp.float32
BF16 = jnp.bfloat16
MESH = pl.DeviceIdType.MESH

D_MODEL = 1024
N_HEADS_A = 12
HEAD_DIM = 128
BLOCK_A = 128
DILATIONS = (1, 4, 16)
W_SUB = 128
D_ATTN_OUT = 512
N_HEADS_B = 8
D_KEY_B = 1024
CONV_WIDTH = 4
CHUNK = 64
D_FF = 2816
EPS = 1e-6
D_IN = 10768
D_PA = 4608
D_PB = 6144
C_Z, C_GATE = 3072, 4096

ADAM_LR, ADAM_B1, ADAM_B2, ADAM_EPS, ADAM_WD, ADAM_STEP = 0.001, 0.9, 0.999, 1e-08, 0.01, 10

PACK_R = 2816
HALF = PACK_R // 2
HALF_TILE = 128
W_IN_COLS = 2692
W_IN_HALF = D_MODEL // 2

VMEM_LIMIT = 48 * 1024 * 1024


def _cparams(n):
    return pltpu.CompilerParams(dimension_semantics=("arbitrary",) * n, vmem_limit_bytes=VMEM_LIMIT)


_ARB = _cparams(1)


def _row_spec(tm, w, cb):
    return pl.BlockSpec((tm, w), lambda i: (i, cb))


def _full_spec(shape):
    nd = len(shape)
    return pl.BlockSpec(shape, lambda i: (0,) * nd)


def _rowwise(name, fn, ins, params, outs, tm, T):
    n_in, n_p = len(ins), len(params)

    def body(*refs):
        xs = [r[...].astype(F32) for r in refs[:n_in]] + [r[...] for r in refs[n_in:n_in + n_p]]
        res = fn(*xs)
        for o_ref, v in zip(refs[n_in + n_p:], res):
            o_ref[...] = v.astype(o_ref.dtype)

    return pl.pallas_call(
        body, name=name, grid=(T // tm,),
        in_specs=[_row_spec(tm, w, cb) for (_, cb, w) in ins] + [_full_spec(p.shape) for p in params],
        out_specs=[_row_spec(tm, w, 0) for (w, _) in outs],
        out_shape=[jax.ShapeDtypeStruct((T, w), dt) for (w, dt) in outs],
        compiler_params=_ARB,
    )(*[a for (a, _, _) in ins], *params)


def _rowwise_vjp(name, fn, ins, params, cts, gdtypes, tm, T, residual=None):
    n_in, n_p, n_ct = len(ins), len(params), len(cts)
    n_res = 0 if residual is None else 1

    def body(*refs):
        i = pl.program_id(0)
        k = 0
        xs = [r[...].astype(F32) for r in refs[k:k + n_in]]
        k += n_in
        ps = [r[...] for r in refs[k:k + n_p]]
        k += n_p
        gs = tuple(r[...].astype(F32) for r in refs[k:k + n_ct])
        k += n_ct
        res_refs = refs[k:k + n_res]
        k += n_res
        gi_refs = refs[k:k + n_in]
        gp_refs = refs[k + n_in:]
        _, vjp = jax.vjp(fn, *xs, *ps)
        g = vjp(gs)
        for j, (ref, v) in enumerate(zip(gi_refs, g[:n_in])):
            if j == 0 and n_res:
                v = v + res_refs[0][...].astype(F32)
            ref[...] = v.astype(ref.dtype)

        @pl.when(i == 0)
        def _():
            for ref in gp_refs:
                ref[...] = jnp.zeros_like(ref)

        for ref, v in zip(gp_refs, g[n_in:]):
            ref[...] += v

    res_in = [] if residual is None else [residual]
    return pl.pallas_call(
        body, name=name, grid=(T // tm,),
        in_specs=([_row_spec(tm, w, cb) for (_, cb, w) in ins] + [_full_spec(p.shape) for p in params]
                  + [_row_spec(tm, w, cb) for (_, cb, w) in cts] + [_row_spec(tm, w, cb) for (_, cb, w) in res_in]),
        out_specs=[_row_spec(tm, w, 0) for (_, _, w) in ins] + [_full_spec(p.shape) for p in params],
        out_shape=([jax.ShapeDtypeStruct((T, w), dt) for (_, _, w), dt in zip(ins, gdtypes)]
                   + [jax.ShapeDtypeStruct(p.shape, F32) for p in params]),
        compiler_params=_ARB,
    )(*[a for (a, _, _) in ins], *params, *[a for (a, _, _) in cts], *[a for (a, _, _) in res_in])


def _matmul(name, a, b, mode, M, N, K, tm, tn, tk, out_dtype, a_off=0, b_off=0, residual=None):
    nk = K // tk
    assert M % tm == 0 and N % tn == 0 and K % tk == 0
    if mode == "nn":
        assert a_off % tk == 0 and b_off % tn == 0
        a_spec = pl.BlockSpec((tm, tk), lambda i, j, k: (i, a_off // tk + k))
        b_spec = pl.BlockSpec((tk, tn), lambda i, j, k: (k, b_off // tn + j))
        dims = (((1,), (0,)), ((), ()))
    elif mode == "nt":
        assert a_off % tk == 0 and b_off == 0
        a_spec = pl.BlockSpec((tm, tk), lambda i, j, k: (i, a_off // tk + k))
        b_spec = pl.BlockSpec((tn, tk), lambda i, j, k: (j, k))
        dims = (((1,), (1,)), ((), ()))
    else:
        assert a_off % tm == 0 and b_off % tn == 0
        a_spec = pl.BlockSpec((tk, tm), lambda i, j, k: (k, a_off // tm + i))
        b_spec = pl.BlockSpec((tk, tn), lambda i, j, k: (k, b_off // tn + j))
        dims = (((0,), (0,)), ((), ()))
    has_res = residual is not None

    def body(*refs):
        a_ref, b_ref = refs[0], refs[1]
        r_ref = refs[2] if has_res else None
        o_ref, acc_ref = refs[-2], refs[-1]
        k = pl.program_id(2)

        @pl.when(k == 0)
        def _():
            acc_ref[...] = jnp.zeros_like(acc_ref)

        acc_ref[...] += lax.dot_general(a_ref[...].astype(BF16), b_ref[...].astype(BF16), dims,
                                        preferred_element_type=F32)

        @pl.when(k == nk - 1)
        def _():
            acc = acc_ref[...]
            if has_res:
                acc = acc + r_ref[...].astype(F32)
            o_ref[...] = acc.astype(o_ref.dtype)

    in_specs = [a_spec, b_spec]
    args = [a, b]
    if has_res:
        in_specs.append(pl.BlockSpec((tm, tn), lambda i, j, k: (i, j)))
        args.append(residual)
    return pl.pallas_call(
        body, name=name, grid=(M // tm, N // tn, nk),
        in_specs=in_specs,
        out_specs=pl.BlockSpec((tm, tn), lambda i, j, k: (i, j)),
        out_shape=jax.ShapeDtypeStruct((M, N), out_dtype),
        scratch_shapes=[pltpu.VMEM((tm, tn), F32)],
        compiler_params=pltpu.CompilerParams(dimension_semantics=("parallel", "parallel", "arbitrary"),
                                             vmem_limit_bytes=VMEM_LIMIT),
    )(*args)


def _matmul_pieces(name, mode, pieces, other, M, N, K, tm, tn, tk, out_dtype, residual=None):
    nk = K // tk
    blk = tn if mode == "tn" else tk
    counts = [p.shape[1] // blk for p in pieces]
    assert all(p.shape[1] % blk == 0 for p in pieces) and sum(counts) == (N if mode == "tn" else K) // blk
    starts = [sum(counts[:p]) for p in range(len(pieces))]
    n_p = len(pieces)
    has_res = residual is not None

    def piece_spec(s, n):
        if mode == "tn":
            return pl.BlockSpec((tk, tn), lambda i, j, k: (jnp.where((j >= s) & (j < s + n), k, 0), jnp.clip(j - s, 0, n - 1)))
        return pl.BlockSpec((tm, tk), lambda i, j, k: (i, jnp.clip(k - s, 0, n - 1)))

    if mode == "tn":
        other_spec = pl.BlockSpec((tk, tm), lambda i, j, k: (k, i))
    else:
        other_spec = pl.BlockSpec((tn, tk), lambda i, j, k: (j, k))

    def body(*refs):
        other_ref, piece_refs = refs[0], refs[1:1 + n_p]
        r_ref = refs[1 + n_p] if has_res else None
        o_ref, acc_ref = refs[-2], refs[-1]
        k = pl.program_id(2)
        sel = pl.program_id(1) if mode == "tn" else k

        @pl.when(k == 0)
        def _():
            acc_ref[...] = jnp.zeros_like(acc_ref)

        for p_ref, s, n in zip(piece_refs, starts, counts):
            @pl.when((sel >= s) & (sel < s + n))
            def _(p_ref=p_ref):
                if mode == "tn":
                    acc_ref[...] += lax.dot_general(other_ref[...].astype(BF16), p_ref[...].astype(BF16),
                                                    (((0,), (0,)), ((), ())), preferred_element_type=F32)
                else:
                    acc_ref[...] += lax.dot_general(p_ref[...].astype(BF16), other_ref[...].astype(BF16),
                                                    (((1,), (1,)), ((), ())), preferred_element_type=F32)

        @pl.when(k == nk - 1)
        def _():
            acc = acc_ref[...]
            if has_res:
                acc = acc + r_ref[...].astype(F32)
            o_ref[...] = acc.astype(o_ref.dtype)

    in_specs = [other_spec] + [piece_spec(s, n) for s, n in zip(starts, counts)]
    args = [other] + list(pieces)
    if has_res:
        in_specs.append(pl.BlockSpec((tm, tn), lambda i, j, k: (i, j)))
        args.append(residual)
    return pl.pallas_call(
        body, name=name, grid=(M // tm, N // tn, nk),
        in_specs=in_specs,
        out_specs=pl.BlockSpec((tm, tn), lambda i, j, k: (i, j)),
        out_shape=jax.ShapeDtypeStruct((M, N), out_dtype),
        scratch_shapes=[pltpu.VMEM((tm, tn), F32)],
        compiler_params=pltpu.CompilerParams(dimension_semantics=("parallel", "parallel", "arbitrary"),
                                             vmem_limit_bytes=VMEM_LIMIT),
    )(*args)


def _f_rms(x, w):
    return (x * lax.rsqrt(jnp.mean(x * x, axis=-1, keepdims=True) + EPS) * w,)


def _silu(x):
    return x * jax.nn.sigmoid(x)


def _softplus(x):
    return jnp.maximum(x, 0.0) + jnp.log1p(jnp.exp(-jnp.abs(x)))


def _heads(x, n):
    return [x[:, HEAD_DIM * h:HEAD_DIM * (h + 1)] for h in range(n)]


def _f_dnpoint(c, ba, alog_v, dtb_v):
    cs = _silu(c)
    q, k, v = cs[:, :D_KEY_B], cs[:, D_KEY_B:2 * D_KEY_B], cs[:, 2 * D_KEY_B:]
    qn = jnp.concatenate([t * lax.rsqrt(jnp.sum(t * t, axis=-1, keepdims=True) + EPS) * (HEAD_DIM ** -0.5)
                          for t in _heads(q, N_HEADS_B)], axis=1)
    kn = jnp.concatenate([t * lax.rsqrt(jnp.sum(t * t, axis=-1, keepdims=True) + EPS)
                          for t in _heads(k, N_HEADS_B)], axis=1)
    lane = lax.broadcasted_iota(jnp.int32, ba.shape, 1)
    beta = jax.nn.sigmoid(ba)
    g = -jnp.exp(alog_v) * _softplus(ba + dtb_v)
    bg = jnp.where(lane < N_HEADS_B, beta, jnp.where(lane < 2 * N_HEADS_B, g, 0.0))
    return qn, kn, v, bg


def _f_postdn(o, z, w):
    outs = []
    for oh, zh in zip(_heads(o, N_HEADS_B), _heads(z, N_HEADS_B)):
        outs.append(oh * lax.rsqrt(jnp.mean(oh * oh, axis=-1, keepdims=True) + EPS) * w * _silu(zh))
    return (jnp.concatenate(outs, axis=1),)


def _f_merge(o0, o1, o2, l0, l1, l2):
    m = lax.stop_gradient(jnp.maximum(jnp.maximum(l0, l1), l2))
    e0, e1, e2 = jnp.exp(l0 - m), jnp.exp(l1 - m), jnp.exp(l2 - m)
    return ((e0 * o0 + e1 * o1 + e2 * o2) / (e0 + e1 + e2),)


def _f_mix(ya, yb, gr):
    return (jax.nn.sigmoid(gr[:, :D_MODEL]) * ya + jax.nn.sigmoid(gr[:, D_MODEL:]) * yb,)


def _f_act(gu):
    return (_silu(gu[:, :D_FF]) * gu[:, D_FF:],)


def _attn_block(q, kp, kc, vp, vc, slope_d, first_key):
    k2 = jnp.concatenate([kp, kc], axis=0).astype(BF16)
    v2 = jnp.concatenate([vp, vc], axis=0).astype(BF16)
    s = lax.dot_general(q.astype(BF16), k2, (((1,), (1,)), ((), ())), preferred_element_type=F32)
    s = s * (HEAD_DIM ** -0.5)
    i = lax.broadcasted_iota(jnp.int32, (BLOCK_A, 2 * BLOCK_A), 0)
    j = lax.broadcasted_iota(jnp.int32, (BLOCK_A, 2 * BLOCK_A), 1)
    delta = BLOCK_A + i - j
    valid = (delta >= 0) & (delta <= W_SUB) & (j >= first_key)
    s = jnp.where(valid, s - slope_d * delta.astype(F32), -jnp.inf)
    m = lax.stop_gradient(jnp.max(s, axis=-1, keepdims=True))
    p = jnp.exp(s - m)
    den = jnp.sum(p, axis=-1, keepdims=True)
    o = jnp.dot(p.astype(BF16), v2, preferred_element_type=F32) / den
    lse = jnp.broadcast_to(m + jnp.log(den), (BLOCK_A, HEAD_DIM))
    return o, lse


def _slope_d(group, h, d):
    hg = (group * 4 + h + 1).astype(F32)
    return jnp.exp(jnp.full((1, 1), -8.0 * math.log(2.0) / N_HEADS_A, F32) * hg) * float(d)


ATT_CFG = ((1, 4, 4), (4, 2, 1), (16, 1, 1))
D_GROUP = 1536


def _att_columns():
    out = []
    for g, (_, _, hp) in enumerate(ATT_CFG):
        lanes = HEAD_DIM * hp
        for hb in range(D_ATTN_OUT // lanes):
            for which in range(3):
                out.append((N_HEADS_A * HEAD_DIM * which + D_ATTN_OUT * g + lanes * hb, lanes))
    return out
R_UNROLL = 4


def _attn_geometry(group, T):
    d, m, hp = ATT_CFG[group]
    span = BLOCK_A * d
    tile = m * span
    assert T % tile == 0 and 4 % hp == 0
    return d, m, hp, span, tile, T // tile, HEAD_DIM * hp


def _sub_rows(b, span, r, d):
    return pl.ds(b * span + r, BLOCK_A, stride=d) if d > 1 else pl.ds(b * span, BLOCK_A)


def _for_each_r(d, fn):
    if d <= R_UNROLL:
        for r in range(d):
            fn(r)
    else:
        def step(r, carry):
            fn(r)
            return carry

        lax.fori_loop(0, d, step, 0, unroll=R_UNROLL)


def _attn_fwd(name, proj, group, T):
    d, m, hp, span, tile, nt, lanes = _attn_geometry(group, T)
    qb, kb, vb = [(D_GROUP * group) // lanes + which for which in range(3)]

    def body(q_ref, kc_ref, kp_ref, vc_ref, vp_ref, o_ref, l_ref):
        hb, n = pl.program_id(0), pl.program_id(1)
        first_key = jnp.where(n > 0, 0, BLOCK_A)
        for hh in range(hp):
            ls = slice(HEAD_DIM * hh, HEAD_DIM * (hh + 1))
            sl = _slope_d(group, hb * hp + hh, d)
            for b in range(m):
                def one(r, b=b, ls=ls, sl=sl):
                    rq = _sub_rows(b, span, r, d)
                    if b == 0:
                        rp = _sub_rows(0, span, r, d)
                        kp, vp, fk = kp_ref[rp, ls], vp_ref[rp, ls], first_key
                    else:
                        rp = _sub_rows(b - 1, span, r, d)
                        kp, vp, fk = kc_ref[rp, ls], vc_ref[rp, ls], 0
                    o, lse = _attn_block(q_ref[rq, ls], kp, kc_ref[rq, ls], vp, vc_ref[rq, ls], sl, fk)
                    o_ref[rq, ls] = o
                    l_ref[rq, ls] = lse

                _for_each_r(d, one)

    def cur(cb):
        return pl.BlockSpec((tile, lanes), lambda hb, n: (n, cb + 3 * hb))

    def prev(cb):
        return pl.BlockSpec((span, lanes), lambda hb, n: (jnp.maximum(m * n - 1, 0), cb + 3 * hb))

    return pl.pallas_call(
        body, name=name, grid=(4 // hp, nt),
        in_specs=[cur(qb), cur(kb), prev(kb), cur(vb), prev(vb)],
        out_specs=[pl.BlockSpec((tile, lanes), lambda hb, n: (n, hb))] * 2,
        out_shape=[jax.ShapeDtypeStruct((T, D_ATTN_OUT), F32)] * 2,
        compiler_params=_cparams(2),
    )(proj, proj, proj, proj, proj)


def _attn_block_bwd(q, kp, kc, vp, vc, o, lse, do, dl, slope_d, first_key):
    scale = HEAD_DIM ** -0.5
    nt_dims = (((1,), (1,)), ((), ()))
    tn_dims = (((0,), (0,)), ((), ()))
    qb = q.astype(BF16)
    k2 = jnp.concatenate([kp, kc], axis=0).astype(BF16)
    v2 = jnp.concatenate([vp, vc], axis=0).astype(BF16)
    dob = do.astype(BF16)
    s = lax.dot_general(qb, k2, nt_dims, preferred_element_type=F32) * scale
    i = lax.broadcasted_iota(jnp.int32, (BLOCK_A, 2 * BLOCK_A), 0)
    j = lax.broadcasted_iota(jnp.int32, (BLOCK_A, 2 * BLOCK_A), 1)
    delta = BLOCK_A + i - j
    valid = (delta >= 0) & (delta <= W_SUB) & (j >= first_key)
    lse_col = jnp.max(lse, axis=-1, keepdims=True)
    p = jnp.exp(jnp.where(valid, s - slope_d * delta.astype(F32), -jnp.inf) - lse_col)
    dp = lax.dot_general(dob, v2, nt_dims, preferred_element_type=F32)
    row = jnp.sum(dl, axis=-1, keepdims=True) - jnp.sum(do * o, axis=-1, keepdims=True)
    ds = (p * (dp + row)).astype(BF16)
    dv2 = lax.dot_general(p.astype(BF16), dob, tn_dims, preferred_element_type=F32)
    dq = jnp.dot(ds, k2, preferred_element_type=F32) * scale
    dk2 = lax.dot_general(ds, qb, tn_dims, preferred_element_type=F32) * scale
    return dq, dk2[:BLOCK_A], dk2[BLOCK_A:], dv2[:BLOCK_A], dv2[BLOCK_A:]


def _attn_bwd(name, proj, o, lse, do, dl, group, T, dest):
    d, m, hp, span, tile, nt, lanes = _attn_geometry(group, T)
    qb, kb, vb = [(D_GROUP * group) // lanes + which for which in range(3)]
    n_in = 9 if dest is None else 10

    def body(*refs):
        q_ref, kc_ref, kp_ref, vc_ref, vp_ref, o_ref, l_ref, do_ref, dl_ref = refs[:9]
        out_ref, aq_ref, ak_ref, av_ref = refs[n_in:]
        hb, n = pl.program_id(0), pl.program_id(1)
        first_key = jnp.where(n > 0, 0, BLOCK_A)
        cur_slot = n % 2
        old_slot = 1 - cur_slot

        @pl.when(n == 0)
        def _():
            aq_ref[...] = jnp.zeros_like(aq_ref)
            ak_ref[...] = jnp.zeros_like(ak_ref)
            av_ref[...] = jnp.zeros_like(av_ref)

        @pl.when(n < nt)
        def _():
            for hh in range(hp):
                ls = slice(HEAD_DIM * hh, HEAD_DIM * (hh + 1))
                sl = _slope_d(group, hb * hp + hh, d)
                for b in range(m):
                    def one(r, b=b, ls=ls, sl=sl):
                        rq = _sub_rows(b, span, r, d)
                        if b == 0:
                            rp = _sub_rows(0, span, r, d)
                            kp, vp, fk = kp_ref[rp, ls], vp_ref[rp, ls], first_key
                        else:
                            rp = _sub_rows(b - 1, span, r, d)
                            kp, vp, fk = kc_ref[rp, ls], vc_ref[rp, ls], 0
                        dq, dkp, dkc, dvp, dvc = _attn_block_bwd(
                            q_ref[rq, ls], kp, kc_ref[rq, ls], vp, vc_ref[rq, ls], o_ref[rq, ls], l_ref[rq, ls],
                            do_ref[rq, ls], dl_ref[rq, ls], sl, fk)
                        aq_ref[cur_slot, rq, ls] = dq
                        ak_ref[cur_slot, rq, ls] = dkc
                        av_ref[cur_slot, rq, ls] = dvc
                        if b == 0:
                            last = _sub_rows(m - 1, span, r, d)
                            ak_ref[old_slot, last, ls] += dkp
                            av_ref[old_slot, last, ls] += dvp
                        else:
                            ak_ref[cur_slot, rp, ls] += dkp
                            av_ref[cur_slot, rp, ls] += dvp

                    _for_each_r(d, one)

        out_ref[:, 0:lanes] = aq_ref[old_slot].astype(BF16)
        out_ref[:, lanes:2 * lanes] = ak_ref[old_slot].astype(BF16)
        out_ref[:, 2 * lanes:3 * lanes] = av_ref[old_slot].astype(BF16)

    def cur(cb):
        return pl.BlockSpec((tile, lanes), lambda hb, n: (jnp.minimum(n, nt - 1), cb + 3 * hb))

    def prev(cb):
        return pl.BlockSpec((span, lanes), lambda hb, n: (jnp.maximum(m * jnp.minimum(n, nt - 1) - 1, 0), cb + 3 * hb))

    own = pl.BlockSpec((tile, lanes), lambda hb, n: (jnp.minimum(n, nt - 1), hb))
    late = pl.BlockSpec((tile, 3 * lanes), lambda hb, n: (jnp.maximum(n - 1, 0), (D_GROUP * group) // (3 * lanes) + hb))
    acc = pltpu.VMEM((2, tile, lanes), F32)
    return pl.pallas_call(
        body, name=name, grid=(4 // hp, nt + 1),
        in_specs=[cur(qb), cur(kb), prev(kb), cur(vb), prev(vb), own, own, own, own] + ([] if dest is None else [_ANY]),
        out_specs=late,
        out_shape=jax.ShapeDtypeStruct((T, D_PA), BF16),
        input_output_aliases={} if dest is None else {9: 0},
        scratch_shapes=[acc, acc, acc],
        compiler_params=_cparams(2),
    )(proj, proj, proj, proj, proj, o, lse, do, dl, *([] if dest is None else [dest]))


HALO = 8
HALO_IN = 16


def _shifted(ext):
    return [ext[HALO:, :]] + [pltpu.roll(ext, k, 0)[HALO:, :] for k in range(1, CONV_WIDTH)]


def _conv_taps(sh, w):
    c = sh[0] * w[CONV_WIDTH - 1]
    for k in range(1, CONV_WIDTH):
        c = c + sh[k] * w[CONV_WIDTH - 1 - k]
    return c


def _taps(w_ref):
    return [w_ref[j:j + 1, :] for j in range(CONV_WIDTH)]


def _dn_pre_fwd(name, proj, ba, conv_w, alog_v, dtb_v, tm, T):
    C = 3 * D_KEY_B

    def body(x_ref, ba_ref, w_ref, al_ref, dt_ref, q_ref, k_ref, v_ref, bg_ref, halo_ref):
        i = pl.program_id(0)

        @pl.when(i == 0)
        def _():
            halo_ref[...] = jnp.zeros_like(halo_ref)

        x = x_ref[...].astype(F32)
        ext = jnp.concatenate([halo_ref[...], x], axis=0)
        c = _conv_taps(_shifted(ext), _taps(w_ref))
        halo_ref[...] = x[tm - HALO:, :]
        qn, kn, v, bg = _f_dnpoint(c, ba_ref[...], al_ref[...], dt_ref[...])
        q_ref[...] = qn
        k_ref[...] = kn
        v_ref[...] = v
        bg_ref[...] = bg

    return pl.pallas_call(
        body, name=name, grid=(T // tm,),
        in_specs=[_row_spec(tm, C, 0), _row_spec(tm, 128, 0), _full_spec(conv_w.shape), _full_spec((1, 128)),
                  _full_spec((1, 128))],
        out_specs=[_row_spec(tm, D_KEY_B, 0)] * 3 + [_row_spec(tm, 128, 0)],
        out_shape=[jax.ShapeDtypeStruct((T, D_KEY_B), F32)] * 3 + [jax.ShapeDtypeStruct((T, 128), F32)],
        scratch_shapes=[pltpu.VMEM((HALO, C), F32)],
        compiler_params=_ARB,
    )(proj, ba, conv_w, alog_v, dtb_v)


def _dn_pre_bwd(name, proj, ba, conv_w, alog_v, dtb_v, dq, dk, dv, dbg, tm, T):
    C = 3 * D_KEY_B
    nt = T // tm
    hb = tm // HALO_IN

    def body(x_ref, xh_ref, ba_ref, w_ref, al_ref, dt_ref, dq_ref, dk_ref, dv_ref, dbg_ref,
             dx_ref, dba_ref, dw_ref, dal_ref, ddt_ref, nxt_ref):
        s = pl.program_id(0)
        i = nt - 1 - s

        @pl.when(s == 0)
        def _():
            nxt_ref[...] = jnp.zeros_like(nxt_ref)
            dw_ref[...] = jnp.zeros_like(dw_ref)
            dal_ref[...] = jnp.zeros_like(dal_ref)
            ddt_ref[...] = jnp.zeros_like(ddt_ref)

        w = _taps(w_ref)
        halo = jnp.where(i > 0, xh_ref[...].astype(F32)[HALO_IN - HALO:, :], 0.0)
        ext = jnp.concatenate([halo, x_ref[...].astype(F32)], axis=0)
        sh = _shifted(ext)
        c = _conv_taps(sh, w)
        _, vjp = jax.vjp(_f_dnpoint, c, ba_ref[...], al_ref[...], dt_ref[...])
        dc, dba, dal, ddt = vjp((dq_ref[...], dk_ref[...], dv_ref[...], dbg_ref[...]))
        dba_ref[...] = dba
        dal_ref[...] += dal
        ddt_ref[...] += ddt
        ext2 = jnp.concatenate([dc, nxt_ref[...]], axis=0)
        dx = dc * w[CONV_WIDTH - 1]
        for m in range(1, CONV_WIDTH):
            dx = dx + pltpu.roll(ext2, tm + HALO - m, 0)[:tm, :] * w[CONV_WIDTH - 1 - m]
        dx_ref[...] = dx.astype(dx_ref.dtype)
        nxt_ref[...] = dc[:HALO, :]
        for j in range(CONV_WIDTH):
            dw_ref[j:j + 1, :] += jnp.sum(sh[CONV_WIDTH - 1 - j] * dc, axis=0, keepdims=True)

    def rev(w):
        return pl.BlockSpec((tm, w), lambda s: (nt - 1 - s, 0))

    return pl.pallas_call(
        body, name=name, grid=(nt,),
        in_specs=[rev(C), pl.BlockSpec((HALO_IN, C), lambda s: (jnp.maximum((nt - 1 - s) * hb - 1, 0), 0)), rev(128),
                  _full_spec(conv_w.shape), _full_spec((1, 128)), _full_spec((1, 128)),
                  rev(D_KEY_B), rev(D_KEY_B), rev(D_KEY_B), rev(128)],
        out_specs=[rev(C), rev(128), _full_spec((8, C)), _full_spec((1, 128)), _full_spec((1, 128))],
        out_shape=[jax.ShapeDtypeStruct((T, C), BF16), jax.ShapeDtypeStruct((T, 128), F32),
                   jax.ShapeDtypeStruct((8, C), F32), jax.ShapeDtypeStruct((1, 128), F32),
                   jax.ShapeDtypeStruct((1, 128), F32)],
        scratch_shapes=[pltpu.VMEM((HALO, C), F32)],
        compiler_params=_ARB,
    )(proj, proj, ba, conv_w, alog_v, dtb_v, dq, dk, dv, dbg)


def _bdot(a, b, dims, exact=False):
    if exact:
        return lax.dot_general(a, b, dims, preferred_element_type=F32, precision=lax.Precision.HIGH)
    return lax.dot_general(a.astype(BF16), b.astype(BF16), dims, preferred_element_type=F32)


_B_NN = (((2,), (1,)), ((0,), (0,)))
_B_NT = (((2,), (2,)), ((0,), (0,)))
_B_TN = (((1,), (1,)), ((0,), (0,)))


def _unit_lower_inverse(a):
    C = a.shape[-1]
    r = lax.broadcasted_iota(jnp.int32, (C, C), 0)
    c = lax.broadcasted_iota(jnp.int32, (C, C), 1)
    pw = -a
    t = jnp.where((r == c)[None], 1.0, 0.0) + pw
    for _ in range(int(math.log2(C)) - 1):
        pw = _bdot(pw, pw, _B_NN, exact=True)
        t = t + _bdot(t, pw, _B_NN, exact=True)
    return t


@jax.custom_vjp
def _solve_given_inverse(t, a, rhs):
    return _bdot(t, rhs, _B_NN, exact=True)


def _solve_fwd(t, a, rhs):
    sol = _bdot(t, rhs, _B_NN, exact=True)
    return sol, (t, sol)


def _solve_bwd(res, dsol):
    t, sol = res
    drhs = _bdot(t, dsol, _B_TN, exact=True)
    return jnp.zeros_like(t), -_bdot(drhs, sol, _B_NT, exact=True), drhs


_solve_given_inverse.defvjp(_solve_fwd, _solve_bwd)


def _stack(xs):
    return jnp.concatenate([x[None] for x in xs], axis=0)


def _delta_chunk(S, q, k, v, bg, t_inv=None):
    H, C = N_HEADS_B, CHUNK
    r = lax.broadcasted_iota(jnp.int32, (C, C), 0)
    c = lax.broadcasted_iota(jnp.int32, (C, C), 1)
    incl, strict, eye = r >= c, r > c, r == c
    gc_all = jnp.dot(incl.astype(F32), bg, preferred_element_type=F32, precision=lax.Precision.HIGHEST)
    lane = lax.broadcasted_iota(jnp.int32, (C, 128), 1)

    def col(x, l):
        return jnp.sum(jnp.where(lane == l, x, 0.0), axis=1, keepdims=True)

    beta = _stack([col(bg, h) for h in range(H)])
    gc = _stack([col(gc_all, H + h) for h in range(H)])
    gc_row = jnp.sum(jnp.where(eye[None], jnp.broadcast_to(gc, (H, C, C)), 0.0), axis=1, keepdims=True)
    row = lax.broadcasted_iota(jnp.int32, (H, C, 1), 1)
    gc_last = jnp.sum(jnp.where(row == C - 1, gc, 0.0), axis=1, keepdims=True)
    decay = jnp.exp(jnp.where(incl[None], gc - gc_row, -jnp.inf))
    qh = _stack(_heads(q, H))
    kh = _stack(_heads(k, H))
    vh = _stack(_heads(v, H))
    kk = _bdot(kh, kh, _B_NT, exact=True)
    a = jnp.where(strict[None], beta * kk * decay, 0.0)
    egc = jnp.exp(gc)
    rhs = jnp.concatenate([beta * vh, (beta * egc) * kh], axis=-1)
    if t_inv is None:
        t_inv = _unit_lower_inverse(a)
        sol = _bdot(t_inv, rhs, _B_NN, exact=True)
    else:
        sol = _solve_given_inverse(t_inv, a, rhs)
    u_bar, w = sol[..., :HEAD_DIM], sol[..., HEAD_DIM:]
    qk = _bdot(qh, kh, _B_NT) * decay
    q_dec = qh * egc
    k_dec = kh * jnp.exp(gc_last - gc)
    u = u_bar - _bdot(w, S, _B_NN)
    o = _bdot(q_dec, S, _B_NN) + _bdot(qk, u, _B_NN)
    S_new = jnp.exp(gc_last) * S + _bdot(k_dec, u, _B_TN)
    o2 = jnp.concatenate([lax.index_in_dim(o, h, 0, keepdims=False) for h in range(H)], axis=1)
    return o2, S_new, t_inv


def _delta_fwd(name, q, k, v, bg, T):
    nc = T // CHUNK
    H = N_HEADS_B

    def body(q_ref, k_ref, v_ref, bg_ref, o_ref, s_out_ref, t_out_ref, s_ref):
        @pl.when(pl.program_id(0) == 0)
        def _():
            s_ref[...] = jnp.zeros_like(s_ref)

        S = s_ref[...]
        s_out_ref[0] = S
        o, S_new, t_inv = _delta_chunk(S, q_ref[...], k_ref[...], v_ref[...], bg_ref[...])
        o_ref[...] = o
        t_out_ref[0] = t_inv
        s_ref[...] = S_new

    return pl.pallas_call(
        body, name=name, grid=(nc,),
        in_specs=[_row_spec(CHUNK, D_KEY_B, 0)] * 3 + [_row_spec(CHUNK, 128, 0)],
        out_specs=[_row_spec(CHUNK, D_KEY_B, 0), pl.BlockSpec((1, H, HEAD_DIM, HEAD_DIM), lambda i: (i, 0, 0, 0)),
                   pl.BlockSpec((1, H, CHUNK, CHUNK), lambda i: (i, 0, 0, 0))],
        out_shape=[jax.ShapeDtypeStruct((T, D_KEY_B), F32), jax.ShapeDtypeStruct((nc, H, HEAD_DIM, HEAD_DIM), F32),
                   jax.ShapeDtypeStruct((nc, H, CHUNK, CHUNK), F32)],
        scratch_shapes=[pltpu.VMEM((H, HEAD_DIM, HEAD_DIM), F32)],
        compiler_params=_ARB,
    )(q, k, v, bg)


def _delta_bwd(name, q, k, v, bg, s_all, t_all, do, T):
    nc = T // CHUNK
    H = N_HEADS_B

    def body(q_ref, k_ref, v_ref, bg_ref, s_in_ref, t_ref, do_ref, dq_ref, dk_ref, dv_ref, dbg_ref, ds_ref):
        @pl.when(pl.program_id(0) == 0)
        def _():
            ds_ref[...] = jnp.zeros_like(ds_ref)

        t_inv = t_ref[0]

        def f(S, qq, kk, vv, bb):
            return _delta_chunk(S, qq, kk, vv, bb, t_inv)[:2]

        _, vjp = jax.vjp(f, s_in_ref[0], q_ref[...], k_ref[...], v_ref[...], bg_ref[...])
        dS, dq, dk, dv, dbg = vjp((do_ref[...], ds_ref[...]))
        dq_ref[...] = dq
        dk_ref[...] = dk
        dv_ref[...] = dv
        dbg_ref[...] = dbg
        ds_ref[...] = dS

    def rev(w):
        return pl.BlockSpec((CHUNK, w), lambda s: (nc - 1 - s, 0))

    def rev4(n):
        return pl.BlockSpec((1, H, n, n), lambda s: (nc - 1 - s, 0, 0, 0))

    return pl.pallas_call(
        body, name=name, grid=(nc,),
        in_specs=[rev(D_KEY_B)] * 3 + [rev(128), rev4(HEAD_DIM), rev4(CHUNK), rev(D_KEY_B)],
        out_specs=[rev(D_KEY_B)] * 3 + [rev(128)],
        out_shape=[jax.ShapeDtypeStruct((T, D_KEY_B), F32)] * 3 + [jax.ShapeDtypeStruct((T, 128), F32)],
        scratch_shapes=[pltpu.VMEM((H, HEAD_DIM, HEAD_DIM), F32)],
        compiler_params=_ARB,
    )(q, k, v, bg, s_all, t_all, do)


def _loss_fb(name, h2, w, tgt, tm, T):
    def tile_loss(h, wv, t):
        y = _f_rms(h, wv)[0]
        e = y - t
        return 0.5 * jnp.sum(jnp.mean(e * e, axis=-1))

    def body(h_ref, w_ref, t_ref, dh_ref, loss_ref, dw_ref):
        @pl.when(pl.program_id(0) == 0)
        def _():
            loss_ref[...] = jnp.zeros_like(loss_ref)
            dw_ref[...] = jnp.zeros_like(dw_ref)

        val, (dh, dw) = jax.value_and_grad(tile_loss, argnums=(0, 1))(h_ref[...], w_ref[...], t_ref[...])
        dh_ref[...] = dh
        dw_ref[...] += dw
        loss_ref[...] += jnp.reshape(val, (1, 1))

    return pl.pallas_call(
        body, name=name, grid=(T // tm,),
        in_specs=[_row_spec(tm, D_MODEL, 0), _full_spec((1, D_MODEL)), _row_spec(tm, D_MODEL, 0)],
        out_specs=[_row_spec(tm, D_MODEL, 0), _full_spec((1, 1)), _full_spec((1, D_MODEL))],
        out_shape=[jax.ShapeDtypeStruct((T, D_MODEL), F32), jax.ShapeDtypeStruct((1, 1), F32),
                   jax.ShapeDtypeStruct((1, D_MODEL), F32)],
        compiler_params=_ARB,
    )(h2, w, tgt)


def _adamw(name, w, g, m, v, tile):
    R = w.shape[0]

    def body(w_ref, g_ref, m_ref, v_ref, d_out, m_out, v_out):
        g = g_ref[...]
        mm = ADAM_B1 * m_ref[...] + (1.0 - ADAM_B1) * g
        vv = ADAM_B2 * v_ref[...] + (1.0 - ADAM_B2) * jnp.square(g)
        m_hat = mm / (1.0 - ADAM_B1 ** ADAM_STEP)
        v_hat = vv / (1.0 - ADAM_B2 ** ADAM_STEP)
        d_out[...] = -ADAM_LR * (m_hat / (jnp.sqrt(v_hat) + ADAM_EPS) + ADAM_WD * w_ref[...])
        m_out[...] = mm
        v_out[...] = vv

    assert R % tile == 0
    spec = pl.BlockSpec((tile, w.shape[1]), lambda i: (i, 0))
    return pl.pallas_call(
        body, name=name, grid=(R // tile,), in_specs=[spec] * 4, out_specs=[spec] * 3,
        out_shape=[jax.ShapeDtypeStruct(w.shape, F32)] * 3, compiler_params=_ARB,
    )(w, g, m, v)


def _sum4(name, r, tile):
    _, R, C = r.shape
    assert R % tile == 0

    def body(r_ref, o_ref):
        o_ref[...] = ((r_ref[0].astype(F32) + r_ref[1].astype(F32)) + r_ref[2].astype(F32)) + r_ref[3].astype(F32)

    return pl.pallas_call(
        body, name=name, grid=(R // tile,),
        in_specs=[pl.BlockSpec((4, tile, C), lambda i: (0, i, 0))],
        out_specs=pl.BlockSpec((tile, C), lambda i: (i, 0)),
        out_shape=jax.ShapeDtypeStruct((R, C), F32), compiler_params=_ARB,
    )(r)


def _add2(name, a, b, tile):
    R, C = a.shape
    assert R % tile == 0

    def body(a_ref, b_ref, o_ref):
        o_ref[...] = (a_ref[...].astype(F32) + b_ref[...].astype(F32)).astype(BF16)

    spec = pl.BlockSpec((tile, C), lambda i: (i, 0))
    return pl.pallas_call(
        body, name=name, grid=(R // tile,), in_specs=[spec, spec], out_specs=spec,
        out_shape=jax.ShapeDtypeStruct((R, C), BF16), compiler_params=_ARB,
    )(a, b)


_ANY = pl.BlockSpec(memory_space=pl.ANY)


def _place():
    return lax.axis_index("x"), lax.axis_index("y"), lax.axis_index("c")


def _exchange_chips(name, arrays, gather):
    n = len(arrays)

    def body(*refs):
        in_refs, out_refs = refs[:n], refs[n:2 * n]
        send_sems, recv_sems, local_sems = refs[2 * n:]
        x, y, c = _place()
        me = 2 * x + y
        chips = [(1 - x, y), (x, 1 - y), (1 - x, 1 - y)]

        def src(a, slot):
            return in_refs[a] if gather else in_refs[a].at[slot]

        def copy(a, k, src_slot, dst_slot, px, py):
            return pltpu.make_async_remote_copy(src_ref=src(a, src_slot), dst_ref=out_refs[a].at[dst_slot],
                                                send_sem=send_sems.at[3 * a + k], recv_sem=recv_sems.at[3 * a + k],
                                                device_id=(px, py, c), device_id_type=MESH)

        mine = [pltpu.make_async_copy(src(a, me), out_refs[a].at[me], local_sems.at[a]) for a in range(n)]
        for cp in mine:
            cp.start()
        sends = [copy(a, k, 2 * px + py, me, px, py) for a in range(n) for k, (px, py) in enumerate(chips)]
        for cp in sends:
            cp.start()
        for a in range(n):
            for k, (px, py) in enumerate(chips):
                copy(a, k, me, 2 * px + py, px, py).wait_recv()
        for cp in sends:
            cp.wait_send()
        for cp in mine:
            cp.wait()

    return pl.pallas_call(
        body, name=name, in_specs=[_ANY] * n, out_specs=[_ANY] * n,
        out_shape=[jax.ShapeDtypeStruct(((4,) + a.shape) if gather else a.shape, a.dtype) for a in arrays],
        scratch_shapes=[pltpu.SemaphoreType.DMA((3 * n,)), pltpu.SemaphoreType.DMA((3 * n,)), pltpu.SemaphoreType.DMA((n,))],
    )(*arrays)


def _swap_sibling(name, arrays):
    n = len(arrays)

    def body(*refs):
        in_refs, out_refs, send_sems, recv_sems = refs[:n], refs[n:2 * n], refs[2 * n], refs[2 * n + 1]
        x, y, c = _place()
        cps = [pltpu.make_async_remote_copy(src_ref=in_refs[a], dst_ref=out_refs[a], send_sem=send_sems.at[a],
                                            recv_sem=recv_sems.at[a], device_id=(x, y, 1 - c), device_id_type=MESH)
               for a in range(n)]
        for cp in cps:
            cp.start()
        for cp in cps:
            cp.wait()

    return pl.pallas_call(
        body, name=name, in_specs=[_ANY] * n, out_specs=[_ANY] * n,
        out_shape=[jax.ShapeDtypeStruct(a.shape, a.dtype) for a in arrays],
        scratch_shapes=[pltpu.SemaphoreType.DMA((n,)), pltpu.SemaphoreType.DMA((n,))],
    )(*arrays)


def _allsum_small(name, pack):
    def body(p_ref, out_ref, slots, send_sems, recv_sems):
        x, y, c = _place()
        me = 4 * x + 2 * y + c
        slots[me] = p_ref[...]
        sends = []
        for k in range(1, 8):
            px = 1 - x if k & 4 else x
            py = 1 - y if k & 2 else y
            pc = 1 - c if k & 1 else c
            cp = pltpu.make_async_remote_copy(src_ref=p_ref, dst_ref=slots.at[me], send_sem=send_sems.at[k - 1],
                                              recv_sem=recv_sems.at[k - 1], device_id=(px, py, pc), device_id_type=MESH)
            cp.start()
            sends.append(cp)
        for k in range(1, 8):
            px = 1 - x if k & 4 else x
            py = 1 - y if k & 2 else y
            pc = 1 - c if k & 1 else c
            pltpu.make_async_remote_copy(src_ref=p_ref, dst_ref=slots.at[4 * px + 2 * py + pc],
                                         send_sem=send_sems.at[k - 1], recv_sem=recv_sems.at[k - 1],
                                         device_id=(px, py, pc), device_id_type=MESH).wait_recv()
        for cp in sends:
            cp.wait_send()
        acc = slots[0]
        for s in range(1, 8):
            acc = acc + slots[s]
        out_ref[...] = acc

    vm = pl.BlockSpec(memory_space=pltpu.VMEM)
    return pl.pallas_call(
        body, name=name, in_specs=[vm], out_specs=vm,
        out_shape=jax.ShapeDtypeStruct(pack.shape, pack.dtype),
        scratch_shapes=[pltpu.VMEM((8,) + pack.shape, pack.dtype), pltpu.SemaphoreType.DMA((7,)),
                        pltpu.SemaphoreType.DMA((7,))],
    )(pack)


def _pack_shards(parts, rows_total):
    flat = [p.reshape(-1, D_MODEL) for p in parts]
    used = sum(f.shape[0] for f in flat)
    return jnp.concatenate(flat + [jnp.zeros((rows_total - used, D_MODEL), flat[0].dtype)], axis=0)


def _unpack_shards(buf, shapes):
    out, r0 = [], 0
    for shp in shapes:
        n = math.prod(shp) // D_MODEL
        out.append(buf[r0:r0 + n].reshape(shp))
        r0 += n
    return out


def kernel(x, norm_mix, w_in, conv_w, a_log, dt_bias, dn_norm, w_proj_attn, w_proj_delta, w_out, norm_ffn, w_gate, w_up, w_down, norm_final, loss_target, m_norm_mix, m_w_in, m_conv_w, m_a_log, m_dt_bias, m_dn_norm, m_w_proj_attn, m_w_proj_delta, m_w_out, m_norm_ffn, m_w_gate, m_w_up, m_w_down, m_norm_final, v_norm_mix, v_w_in, v_conv_w, v_a_log, v_dt_bias, v_dn_norm, v_w_proj_attn, v_w_proj_delta, v_w_out, v_norm_ffn, v_w_gate, v_w_up, v_w_down, v_norm_final):
    T = x.shape[1]
    D = D_MODEL
    tm = 512
    tw = 256
    mt = 1024 if T % 1024 == 0 else 512
    x2 = x.reshape(T, D)
    tgt = loss_target.reshape(T, D)

    packed = (w_proj_attn, w_proj_delta, w_out, w_gate, w_up, w_down)
    conv_bits = lax.bitcast_convert_type(conv_w[0], BF16)
    w16 = _pack_shards([w[0].astype(BF16) for w in packed] + [conv_bits], PACK_R)
    win16 = w_in[0].astype(BF16)
    my_c = lax.axis_index("c")

    def both_halves(mine, other, axis):
        return jnp.concatenate([jnp.where(my_c == 0, mine, other), jnp.where(my_c == 0, other, mine)], axis=axis)

    w_mine = _exchange_chips("gather_weights", [lax.dynamic_slice_in_dim(w16, my_c * HALF, HALF, axis=0),
                                                lax.dynamic_slice_in_dim(win16, my_c * W_IN_HALF, W_IN_HALF, axis=0)], True)
    w_other = _swap_sibling("swap_weights", w_mine)
    wg = both_halves(w_mine[0], w_other[0], 1)
    win4 = both_halves(w_mine[1], w_other[1], 1)

    def seg(r0, n):
        return wg[:, r0:r0 + n, :]

    def win_cols(g0, g1):
        out = []
        for j in range(4):
            lo, hi = max(g0, W_IN_COLS * j), min(g1, W_IN_COLS * (j + 1))
            if lo < hi:
                out.append(win4[j][:, lo - W_IN_COLS * j:hi - W_IN_COLS * j])
        return out

    r = 0
    Wpa = seg(r, 128).reshape(4, D_ATTN_OUT, 256).transpose(1, 0, 2).reshape(D_ATTN_OUT, D); r += 128
    Wpd = seg(r, 256).reshape(D, D); r += 256
    Wout = seg(r, 256).reshape(D, D); r += 256
    Wg = seg(r, 704).reshape(4, D, 704).transpose(1, 0, 2).reshape(D, D_FF); r += 704
    Wu = seg(r, 704).reshape(4, D, 704).transpose(1, 0, 2).reshape(D, D_FF); r += 704
    Wd = seg(r, 704).reshape(D_FF, D); r += 704
    convw = lax.bitcast_convert_type(seg(r, 6).reshape(4, CONV_WIDTH, 768, 2), F32)
    convw = convw.transpose(1, 0, 2).reshape(CONV_WIDTH, 3 * D_KEY_B)
    Wa = jnp.concatenate([p for c0, w in _att_columns() for p in win_cols(c0, c0 + w)], axis=1)
    Wb = jnp.concatenate(win_cols(4608, 8704) + win_cols(8720, D_IN), axis=1)
    Wba = jnp.pad(jnp.concatenate(win_cols(8704, 8720), axis=1), ((0, 0), (0, 112)))
    Wgu = jnp.concatenate([Wg, Wu], axis=1)
    alog_v = jnp.pad(a_log, ((0, 0), (8, 112)))
    dtb_v = jnp.pad(dt_bias, ((0, 0), (8, 112)))

    (u,) = _rowwise("rms_mix", _f_rms, [(x2, 0, D)], [norm_mix], [(D, BF16)], tm, T)
    proj_a = _matmul("mm_in_a", u, Wa, "nn", T, D_PA, D, mt, 1536, D, F32)
    proj_b = _matmul("mm_in_b", u, Wb, "nn", T, D_PB, D, mt, 1536, D, BF16)
    ba = _matmul("mm_ba", u, Wba, "nn", T, 128, D, mt, 128, D, F32)

    att = [_attn_fwd(f"attn_fwd{g}", proj_a, g, T) for g in range(3)]
    (ya_m,) = _rowwise("merge", _f_merge, [(att[g][0], 0, 512) for g in range(3)] + [(att[g][1], 0, 512) for g in range(3)],
                       [], [(512, BF16)], tm, T)
    qn, kn, vd, bg = _dn_pre_fwd("dn_pre_fwd", proj_b, ba, convw, alog_v, dtb_v, tw, T)
    o_d, s_all, t_all = _delta_fwd("delta_fwd", qn, kn, vd, bg, T)
    (od,) = _rowwise("post_dn", _f_postdn, [(o_d, 0, D), (proj_b, C_Z // D, D)], [dn_norm], [(D, BF16)], tm, T)
    y_a = _matmul("mm_pa", ya_m, Wpa, "nn", T, D, 512, mt, D, 512, F32)
    y_b = _matmul("mm_pd", od, Wpd, "nn", T, D, D, mt, D, D, F32)
    (mix,) = _rowwise("mix", _f_mix, [(y_a, 0, D), (y_b, 0, D), (proj_b, C_GATE // 2048, 2048)], [], [(D, BF16)], tm, T)
    h1 = _matmul("mm_out", mix, Wout, "nn", T, D, D, mt, D, D, F32, residual=x2)
    (hn,) = _rowwise("rms_ffn", _f_rms, [(h1, 0, D)], [norm_ffn], [(D, BF16)], tm, T)
    gu = _matmul("mm_gu", hn, Wgu, "nn", T, 2 * D_FF, D, mt, 1408, D, BF16)
    (act,) = _rowwise("act", _f_act, [(gu, 0, 2 * D_FF)], [], [(D_FF, BF16)], tw, T)
    h2 = _matmul("mm_down", act, Wd, "nn", T, D, D_FF, mt, D, 1408, F32, residual=h1)
    dh2, loss, g_norm_final = _loss_fb("loss", h2, norm_final.reshape(1, D), tgt, tm, T)

    dact = _matmul("mm_dact", dh2, Wd, "nt", T, D_FF, D, mt, 1408, D, BF16)
    gWd = _matmul("mm_gwd", act, dh2, "tn", D_FF, D, T, 1408, D, mt, BF16)
    (dgu,) = _rowwise_vjp("act_bwd", _f_act, [(gu, 0, 2 * D_FF)], [], [(dact, 0, D_FF)], [BF16], tw, T)
    dhn = _matmul("mm_dhn", dgu, Wgu, "nt", T, D, 2 * D_FF, mt, D, 1408, F32)
    gWgu = _matmul("mm_gwgu", hn, dgu, "tn", D, 2 * D_FF, T, D, 1408, mt, BF16)
    dh1, g_norm_ffn = _rowwise_vjp("rms_ffn_bwd", _f_rms, [(h1, 0, D)], [norm_ffn], [(dhn, 0, D)], [F32], tm, T,
                                   residual=(dh2, 0, D))
    dmix = _matmul("mm_dmix", dh1, Wout, "nt", T, D, D, mt, D, D, F32)
    gWout = _matmul("mm_gwout", mix, dh1, "tn", D, D, T, D, D, mt, BF16)
    dya, dyb, dgate = _rowwise_vjp("mix_bwd", _f_mix, [(y_a, 0, D), (y_b, 0, D), (proj_b, C_GATE // 2048, 2048)], [],
                                   [(dmix, 0, D)], [BF16, BF16, BF16], tm, T)
    dya_m = _matmul("mm_dyam", dya, Wpa, "nt", T, 512, D, mt, 512, D, F32)
    gWpa = _matmul("mm_gwpa", ya_m, dya, "tn", 512, D, T, 512, D, mt, BF16)
    dod = _matmul("mm_dod", dyb, Wpd, "nt", T, D, D, mt, D, D, F32)
    gWpd = _matmul("mm_gwpd", od, dyb, "tn", D, D, T, D, D, mt, BF16)
    do_d, dz, g_dn_norm = _rowwise_vjp("post_dn_bwd", _f_postdn, [(o_d, 0, D), (proj_b, C_Z // D, D)], [dn_norm],
                                       [(dod, 0, D)], [F32, BF16], tm, T)
    dqn, dkn, dvd, dbg = _delta_bwd("delta_bwd", qn, kn, vd, bg, s_all, t_all, do_d, T)
    dqkvd, dba, g_conv8, g_alog_v, g_dtb_v = _dn_pre_bwd("dn_pre_bwd", proj_b, ba, convw, alog_v, dtb_v,
                                                         dqn, dkn, dvd, dbg, tw, T)
    mg = _rowwise_vjp("merge_bwd", _f_merge,
                      [(att[g][0], 0, 512) for g in range(3)] + [(att[g][1], 0, 512) for g in range(3)], [],
                      [(dya_m, 0, 512)], [F32] * 6, tm, T)
    dproj_a = None
    for g in range(3):
        dproj_a = _attn_bwd(f"attn_bwd{g}", proj_a, att[g][0], att[g][1], mg[g], mg[3 + g], g, T, dproj_a)
    dproj_b = [dqkvd, dz, dgate]
    gWa = _matmul("mm_gwa", u, dproj_a, "tn", D, D_PA, T, D, 1536, mt, BF16)
    gWb = _matmul_pieces("mm_gwb", "tn", dproj_b, u, D, D_PB, T, D, 1024, mt, BF16)
    gWba = _matmul("mm_gwba", u, dba, "tn", D, 128, T, D, 128, mt, BF16)
    du0 = _matmul("mm_du_ba", dba, Wba, "nt", T, D, 128, mt, D, 128, F32)
    du1 = _matmul_pieces("mm_du_b", "nt", dproj_b, Wb, T, D, D_PB, mt, D, 1024, F32, residual=du0)
    du = _matmul("mm_du_a", dproj_a, Wa, "nt", T, D, D_PA, mt, D, 1536, F32, residual=du1)
    dx, g_norm_mix = _rowwise_vjp("rms_mix_bwd", _f_rms, [(x2, 0, D)], [norm_mix], [(du, 0, D)], [F32], tm, T,
                                  residual=(dh1, 0, D))

    pieces, off = [(gWb[:, :C_GATE], D_PA), (gWba[:, :16], 8704), (gWb[:, C_GATE:], 8720)], 0
    for c0, w in _att_columns():
        pieces.append((gWa[:, off:off + w], c0))
        off += w
    pieces.sort(key=lambda t: t[1])

    def win_shard(j):
        out = []
        for arr, g0 in pieces:
            lo, hi = max(g0, W_IN_COLS * j), min(g0 + arr.shape[1], W_IN_COLS * (j + 1))
            if lo < hi:
                out.append(arr[:, lo - g0:hi - g0])
        return jnp.concatenate(out, axis=1)

    g_win4 = jnp.concatenate([win_shard(j)[None] for j in range(4)], axis=0)

    def cols4(g, n):
        return g.reshape(g.shape[0], 4, n).transpose(1, 0, 2).reshape(4, -1, D)

    parts = [cols4(gWpa, 256), gWpd.reshape(4, 256, D), gWout.reshape(4, 256, D), cols4(gWgu[:, :D_FF], 704),
             cols4(gWgu[:, D_FF:], 704), gWd.reshape(4, 704, D), cols4(g_conv8[:CONV_WIDTH], 768)]
    used = sum(p.shape[1] for p in parts)
    g_pack = jnp.concatenate([p.astype(BF16) for p in parts] + [jnp.zeros((4, PACK_R - used, D), BF16)], axis=1)
    halves = ((g_pack, HALF), (g_win4, W_IN_HALF))
    g_keep = [lax.dynamic_slice_in_dim(g, my_c * h, h, axis=1) for g, h in halves]
    g_give = [lax.dynamic_slice_in_dim(g, (1 - my_c) * h, h, axis=1) for g, h in halves]
    g_got = _swap_sibling("swap_grads", g_give)
    g_pair = [_add2(f"add_cores{i}", a.reshape(-1, a.shape[2]), b.reshape(-1, a.shape[2]), HALF_TILE).reshape(a.shape)
              for i, (a, b) in enumerate(zip(g_keep, g_got))]
    recv = _exchange_chips("scatter_grads", g_pair, False)
    g_mine = [_sum4(f"sum_chips{i}", rv, HALF_TILE) for i, rv in enumerate(recv)]
    g_other = _swap_sibling("swap_sums", g_mine)
    g_all = both_halves(g_mine[0], g_other[0], 0)
    g_win = both_halves(g_mine[1], g_other[1], 0)

    locals_ = (w_proj_attn, w_proj_delta, w_out, w_gate, w_up, w_down, conv_w, w_in)
    ms = (m_w_proj_attn, m_w_proj_delta, m_w_out, m_w_gate, m_w_up, m_w_down, m_conv_w, m_w_in)
    vs = (v_w_proj_attn, v_w_proj_delta, v_w_out, v_w_gate, v_w_up, v_w_down, v_conv_w, v_w_in)
    big_names = ("w_proj_attn", "w_proj_delta", "w_out", "w_gate", "w_up", "w_down", "conv_w", "w_in")
    adam_tiles = (512, 256, 256, 256, 256, 352, CONV_WIDTH, 128)
    g_shards = _unpack_shards(g_all, [w.shape for w in locals_[:-1]]) + [g_win.reshape(w_in.shape)]
    big = [g_shards, [], [], []]
    for nm, w, g, m_, v_, tl in zip(big_names, locals_, g_shards, ms, vs, adam_tiles):
        res = _adamw("adamw_" + nm, w[0], g[0], m_[0], v_[0], tl)
        for kind in range(3):
            big[kind + 1].append(res[kind].reshape(w.shape))

    def small_pack(nm, nf, nfin, dn, al, dt, extra):
        row3 = jnp.concatenate([dn.reshape(1, 128), al.reshape(1, 8), dt.reshape(1, 8), extra.reshape(1, 1),
                                jnp.zeros((1, D - 145), F32)], axis=1)
        return jnp.concatenate([nm.reshape(1, D), nf.reshape(1, D), nfin.reshape(1, D), row3, jnp.zeros((4, D), F32)], axis=0)

    zero1 = jnp.zeros((1, 1), F32)
    sp = small_pack(g_norm_mix, g_norm_ffn, g_norm_final, g_dn_norm, g_alog_v[:, 8:16], g_dtb_v[:, 8:16], loss)
    tot = _allsum_small("allsum_small", sp)
    sm = _adamw("adamw_small", small_pack(norm_mix, norm_ffn, norm_final, dn_norm, a_log, dt_bias, zero1), tot,
                small_pack(m_norm_mix, m_norm_ffn, m_norm_final, m_dn_norm, m_a_log, m_dt_bias, zero1),
                small_pack(v_norm_mix, v_norm_ffn, v_norm_final, v_dn_norm, v_a_log, v_dt_bias, zero1), 8)
    sm = [tot] + list(sm)

    def small_unpack(p):
        return {"norm_mix": p[0:1], "norm_ffn": p[1:2], "norm_final": p[2], "dn_norm": p[3:4, :128],
                "a_log": p[3:4, 128:136], "dt_bias": p[3:4, 136:144]}

    sm = [small_unpack(p) for p in sm]
    loss_out = tot[3, 144]

    order = ("norm_mix", "w_in", "conv_w", "a_log", "dt_bias", "dn_norm", "w_proj_attn", "w_proj_delta", "w_out",
             "norm_ffn", "w_gate", "w_up", "w_down", "norm_final")
    outs = [loss_out, dx.reshape(x.shape)]
    for kind in range(4):
        for n in order:
            outs.append(big[kind][big_names.index(n)] if n in big_names else sm[kind][n])
    return tuple(outs)
```

```python
import functools
import math

import jax
import jax.numpy as jnp
from jax import lax
from jax.experimental import pallas as pl
from jax.experimental.pallas import tpu as pltpu

F32 = jnp.float32
BF16 = jnp.bfloat16
MESH = pl.DeviceIdType.MESH

D_MODEL = 1024
N_HEADS_A = 12
HEAD_DIM = 128
BLOCK_A = 128
DILATIONS = (1, 4, 16)
W_SUB = 128
D_ATTN_OUT = 512
N_HEADS_B = 8
D_KEY_B = 1024
CONV_WIDTH = 4
CHUNK = 64
D_FF = 2816
EPS = 1e-6
D_IN = 10768
D_PA = 4608
D_PB = 6144
C_Z, C_GATE = 3072, 4096

ADAM_LR, ADAM_B1, ADAM_B2, ADAM_EPS, ADAM_WD, ADAM_STEP = 0.001, 0.9, 0.999, 1e-08, 0.01, 10

PACK_R = 2816
HALF = PACK_R // 2
HALF_TILE = 128
W_IN_COLS = 2692
W_IN_HALF = D_MODEL // 2

VMEM_LIMIT = 48 * 1024 * 1024


def _cparams(n):
    return pltpu.CompilerParams(dimension_semantics=("arbitrary",) * n, vmem_limit_bytes=VMEM_LIMIT)


_ARB = _cparams(1)


def _row_spec(tm, w, cb):
    return pl.BlockSpec((tm, w), lambda i: (i, cb))


def _full_spec(shape):
    nd = len(shape)
    return pl.BlockSpec(shape, lambda i: (0,) * nd)


def _rowwise(name, fn, ins, params, outs, tm, T):
    n_in, n_p = len(ins), len(params)

    def body(*refs):
        xs = [r[...].astype(F32) for r in refs[:n_in]] + [r[...] for r in refs[n_in:n_in + n_p]]
        res = fn(*xs)
        for o_ref, v in zip(refs[n_in + n_p:], res):
            o_ref[...] = v.astype(o_ref.dtype)

    return pl.pallas_call(
        body, name=name, grid=(T // tm,),
        in_specs=[_row_spec(tm, w, cb) for (_, cb, w) in ins] + [_full_spec(p.shape) for p in params],
        out_specs=[_row_spec(tm, w, 0) for (w, _) in outs],
        out_shape=[jax.ShapeDtypeStruct((T, w), dt) for (w, dt) in outs],
        compiler_params=_ARB,
    )(*[a for (a, _, _) in ins], *params)


def _rowwise_vjp(name, fn, ins, params, cts, gdtypes, tm, T, residual=None):
    n_in, n_p, n_ct = len(ins), len(params), len(cts)
    n_res = 0 if residual is None else 1

    def body(*refs):
        i = pl.program_id(0)
        k = 0
        xs = [r[...].astype(F32) for r in refs[k:k + n_in]]
        k += n_in
        ps = [r[...] for r in refs[k:k + n_p]]
        k += n_p
        gs = tuple(r[...].astype(F32) for r in refs[k:k + n_ct])
        k += n_ct
        res_refs = refs[k:k + n_res]
        k += n_res
        gi_refs = refs[k:k + n_in]
        gp_refs = refs[k + n_in:]
        _, vjp = jax.vjp(fn, *xs, *ps)
        g = vjp(gs)
        for j, (ref, v) in enumerate(zip(gi_refs, g[:n_in])):
            if j == 0 and n_res:
                v = v + res_refs[0][...].astype(F32)
            ref[...] = v.astype(ref.dtype)

        @pl.when(i == 0)
        def _():
            for ref in gp_refs:
                ref[...] = jnp.zeros_like(ref)

        for ref, v in zip(gp_refs, g[n_in:]):
            ref[...] += v

    res_in = [] if residual is None else [residual]
    return pl.pallas_call(
        body, name=name, grid=(T // tm,),
        in_specs=([_row_spec(tm, w, cb) for (_, cb, w) in ins] + [_full_spec(p.shape) for p in params]
                  + [_row_spec(tm, w, cb) for (_, cb, w) in cts] + [_row_spec(tm, w, cb) for (_, cb, w) in res_in]),
        out_specs=[_row_spec(tm, w, 0) for (_, _, w) in ins] + [_full_spec(p.shape) for p in params],
        out_shape=([jax.ShapeDtypeStruct((T, w), dt) for (_, _, w), dt in zip(ins, gdtypes)]
                   + [jax.ShapeDtypeStruct(p.shape, F32) for p in params]),
        compiler_params=_ARB,
    )(*[a for (a, _, _) in ins], *params, *[a for (a, _, _) in cts], *[a for (a, _, _) in res_in])


def _matmul(name, a, b, mode, M, N, K, tm, tn, tk, out_dtype, a_off=0, b_off=0, residual=None, side=None):
    nk = K // tk
    n_side = 0 if side is None else len(side[0])
    assert M % tm == 0 and N % tn == 0 and K % tk == 0
    if mode == "nn":
        assert a_off % tk == 0 and b_off % tn == 0
        a_spec = pl.BlockSpec((tm, tk), lambda i, j, k: (i, a_off // tk + k))
        b_spec = pl.BlockSpec((tk, tn), lambda i, j, k: (k, b_off // tn + j))
        dims = (((1,), (0,)), ((), ()))
    elif mode == "nt":
        assert a_off % tk == 0 and b_off == 0
        a_spec = pl.BlockSpec((tm, tk), lambda i, j, k: (i, a_off // tk + k))
        b_spec = pl.BlockSpec((tn, tk), lambda i, j, k: (j, k))
        dims = (((1,), (1,)), ((), ()))
    else:
        assert a_off % tm == 0 and b_off % tn == 0
        a_spec = pl.BlockSpec((tk, tm), lambda i, j, k: (k, a_off // tm + i))
        b_spec = pl.BlockSpec((tk, tn), lambda i, j, k: (k, b_off // tn + j))
        dims = (((0,), (0,)), ((), ()))
    has_res = residual is not None

    n_main = 3 if has_res else 2
    grid = (M // tm, N // tn, nk)

    def body(*refs):
        a_ref, b_ref = refs[0], refs[1]
        r_ref = refs[2] if has_res else None
        o_ref = refs[n_main + n_side]
        acc_ref = refs[n_main + 2 * n_side + 1]
        i, j, k = pl.program_id(0), pl.program_id(1), pl.program_id(2)
        if n_side:
            start, finish = _chip_exchange(refs[n_main:n_main + n_side], refs[n_main + n_side + 1:n_main + 2 * n_side + 1],
                                           *refs[n_main + 2 * n_side + 2:], gather=side[1])
            pl.when((i == 0) & (j == 0) & (k == 0))(start)

        @pl.when(k == 0)
        def _():
            acc_ref[...] = jnp.zeros_like(acc_ref)

        acc_ref[...] += lax.dot_general(a_ref[...].astype(BF16), b_ref[...].astype(BF16), dims,
                                        preferred_element_type=F32)

        @pl.when(k == nk - 1)
        def _():
            acc = acc_ref[...]
            if has_res:
                acc = acc + r_ref[...].astype(F32)
            o_ref[...] = acc.astype(o_ref.dtype)

        if n_side:
            pl.when((i == grid[0] - 1) & (j == grid[1] - 1) & (k == nk - 1))(finish)

    in_specs = [a_spec, b_spec]
    args = [a, b]
    if has_res:
        in_specs.append(pl.BlockSpec((tm, tn), lambda i, j, k: (i, j)))
        args.append(residual)
    out_specs = pl.BlockSpec((tm, tn), lambda i, j, k: (i, j))
    out_shape = jax.ShapeDtypeStruct((M, N), out_dtype)
    scratch = [pltpu.VMEM((tm, tn), F32)]
    semantics = ("parallel", "parallel", "arbitrary")
    if n_side:
        in_specs += [_ANY] * n_side
        args += list(side[0])
        out_specs = [out_specs] + [_ANY] * n_side
        out_shape = [out_shape] + _chip_exchange_shapes(*side)
        scratch += _chip_exchange_sems(n_side)
        semantics = ("arbitrary",) * 3
    return pl.pallas_call(
        body, name=name, grid=grid, in_specs=in_specs, out_specs=out_specs, out_shape=out_shape, scratch_shapes=scratch,
        compiler_params=pltpu.CompilerParams(dimension_semantics=semantics, vmem_limit_bytes=VMEM_LIMIT),
    )(*args)


def _matmul_pieces(name, mode, pieces, other, M, N, K, tm, tn, tk, out_dtype, residual=None):
    nk = K // tk
    blk = tn if mode == "tn" else tk
    counts = [p.shape[1] // blk for p in pieces]
    assert all(p.shape[1] % blk == 0 for p in pieces) and sum(counts) == (N if mode == "tn" else K) // blk
    starts = [sum(counts[:p]) for p in range(len(pieces))]
    n_p = len(pieces)
    has_res = residual is not None

    def piece_spec(s, n):
        if mode == "tn":
            return pl.BlockSpec((tk, tn), lambda i, j, k: (jnp.where((j >= s) & (j < s + n), k, 0), jnp.clip(j - s, 0, n - 1)))
        return pl.BlockSpec((tm, tk), lambda i, j, k: (i, jnp.clip(k - s, 0, n - 1)))

    if mode == "tn":
        other_spec = pl.BlockSpec((tk, tm), lambda i, j, k: (k, i))
    else:
        other_spec = pl.BlockSpec((tn, tk), lambda i, j, k: (j, k))

    def body(*refs):
        other_ref, piece_refs = refs[0], refs[1:1 + n_p]
        r_ref = refs[1 + n_p] if has_res else None
        o_ref, acc_ref = refs[-2], refs[-1]
        k = pl.program_id(2)
        sel = pl.program_id(1) if mode == "tn" else k

        @pl.when(k == 0)
        def _():
            acc_ref[...] = jnp.zeros_like(acc_ref)

        for p_ref, s, n in zip(piece_refs, starts, counts):
            @pl.when((sel >= s) & (sel < s + n))
            def _(p_ref=p_ref):
                if mode == "tn":
                    acc_ref[...] += lax.dot_general(other_ref[...].astype(BF16), p_ref[...].astype(BF16),
                                                    (((0,), (0,)), ((), ())), preferred_element_type=F32)
                else:
                    acc_ref[...] += lax.dot_general(p_ref[...].astype(BF16), other_ref[...].astype(BF16),
                                                    (((1,), (1,)), ((), ())), preferred_element_type=F32)

        @pl.when(k == nk - 1)
        def _():
            acc = acc_ref[...]
            if has_res:
                acc = acc + r_ref[...].astype(F32)
            o_ref[...] = acc.astype(o_ref.dtype)

    in_specs = [other_spec] + [piece_spec(s, n) for s, n in zip(starts, counts)]
    args = [other] + list(pieces)
    if has_res:
        in_specs.append(pl.BlockSpec((tm, tn), lambda i, j, k: (i, j)))
        args.append(residual)
    return pl.pallas_call(
        body, name=name, grid=(M // tm, N // tn, nk),
        in_specs=in_specs,
        out_specs=pl.BlockSpec((tm, tn), lambda i, j, k: (i, j)),
        out_shape=jax.ShapeDtypeStruct((M, N), out_dtype),
        scratch_shapes=[pltpu.VMEM((tm, tn), F32)],
        compiler_params=pltpu.CompilerParams(dimension_semantics=("parallel", "parallel", "arbitrary"),
                                             vmem_limit_bytes=VMEM_LIMIT),
    )(*args)


def _f_rms(x, w):
    return (x * lax.rsqrt(jnp.mean(x * x, axis=-1, keepdims=True) + EPS) * w,)


def _silu(x):
    return x * jax.nn.sigmoid(x)


def _softplus(x):
    return jnp.maximum(x, 0.0) + jnp.log1p(jnp.exp(-jnp.abs(x)))


def _heads(x, n):
    return [x[:, HEAD_DIM * h:HEAD_DIM * (h + 1)] for h in range(n)]


def _f_dnpoint(c, ba, alog_v, dtb_v):
    cs = _silu(c)
    q, k, v = cs[:, :D_KEY_B], cs[:, D_KEY_B:2 * D_KEY_B], cs[:, 2 * D_KEY_B:]
    qn = jnp.concatenate([t * lax.rsqrt(jnp.sum(t * t, axis=-1, keepdims=True) + EPS) * (HEAD_DIM ** -0.5)
                          for t in _heads(q, N_HEADS_B)], axis=1)
    kn = jnp.concatenate([t * lax.rsqrt(jnp.sum(t * t, axis=-1, keepdims=True) + EPS)
                          for t in _heads(k, N_HEADS_B)], axis=1)
    lane = lax.broadcasted_iota(jnp.int32, ba.shape, 1)
    beta = jax.nn.sigmoid(ba)
    g = -jnp.exp(alog_v) * _softplus(ba + dtb_v)
    bg = jnp.where(lane < N_HEADS_B, beta, jnp.where(lane < 2 * N_HEADS_B, g, 0.0))
    return qn, kn, v, bg


def _f_postdn(o, z, w):
    outs = []
    for oh, zh in zip(_heads(o, N_HEADS_B), _heads(z, N_HEADS_B)):
        outs.append(oh * lax.rsqrt(jnp.mean(oh * oh, axis=-1, keepdims=True) + EPS) * w * _silu(zh))
    return (jnp.concatenate(outs, axis=1),)


def _f_merge(o0, o1, o2, l0, l1, l2):
    m = lax.stop_gradient(jnp.maximum(jnp.maximum(l0, l1), l2))
    e0, e1, e2 = jnp.exp(l0 - m), jnp.exp(l1 - m), jnp.exp(l2 - m)
    return ((e0 * o0 + e1 * o1 + e2 * o2) / (e0 + e1 + e2),)


def _f_mix(ya, yb, gr):
    return (jax.nn.sigmoid(gr[:, :D_MODEL]) * ya + jax.nn.sigmoid(gr[:, D_MODEL:]) * yb,)


def _f_act(gu):
    return (_silu(gu[:, :D_FF]) * gu[:, D_FF:],)


def _attn_block(q, kp, kc, vp, vc, slope_d, first_key):
    k2 = jnp.concatenate([kp, kc], axis=0).astype(BF16)
    v2 = jnp.concatenate([vp, vc], axis=0).astype(BF16)
    s = lax.dot_general(q.astype(BF16), k2, (((1,), (1,)), ((), ())), preferred_element_type=F32)
    s = s * (HEAD_DIM ** -0.5)
    i = lax.broadcasted_iota(jnp.int32, (BLOCK_A, 2 * BLOCK_A), 0)
    j = lax.broadcasted_iota(jnp.int32, (BLOCK_A, 2 * BLOCK_A), 1)
    delta = BLOCK_A + i - j
    valid = (delta >= 0) & (delta <= W_SUB) & (j >= first_key)
    s = jnp.where(valid, s - slope_d * delta.astype(F32), -jnp.inf)
    m = lax.stop_gradient(jnp.max(s, axis=-1, keepdims=True))
    p = jnp.exp(s - m)
    den = jnp.sum(p, axis=-1, keepdims=True)
    o = jnp.dot(p.astype(BF16), v2, preferred_element_type=F32) / den
    lse = jnp.broadcast_to(m + jnp.log(den), (BLOCK_A, HEAD_DIM))
    return o, lse


def _slope_d(group, h, d):
    hg = (group * 4 + h + 1).astype(F32)
    return jnp.exp(jnp.full((1, 1), -8.0 * math.log(2.0) / N_HEADS_A, F32) * hg) * float(d)


ATT_CFG = ((1, 4, 4), (4, 2, 1), (16, 1, 1))
D_GROUP = 1536


def _att_columns():
    out = []
    for g, (_, _, hp) in enumerate(ATT_CFG):
        lanes = HEAD_DIM * hp
        for hb in range(D_ATTN_OUT // lanes):
            for which in range(3):
                out.append((N_HEADS_A * HEAD_DIM * which + D_ATTN_OUT * g + lanes * hb, lanes))
    return out
R_UNROLL = 4


def _attn_geometry(group, T):
    d, m, hp = ATT_CFG[group]
    span = BLOCK_A * d
    tile = m * span
    assert T % tile == 0 and 4 % hp == 0
    return d, m, hp, span, tile, T // tile, HEAD_DIM * hp


def _sub_rows(b, span, r, d):
    return pl.ds(b * span + r, BLOCK_A, stride=d) if d > 1 else pl.ds(b * span, BLOCK_A)


def _for_each_r(d, fn):
    if d <= R_UNROLL:
        for r in range(d):
            fn(r)
    else:
        def step(r, carry):
            fn(r)
            return carry

        lax.fori_loop(0, d, step, 0, unroll=R_UNROLL)


def _attn_fwd(name, proj, group, T):
    d, m, hp, span, tile, nt, lanes = _attn_geometry(group, T)
    qb, kb, vb = [(D_GROUP * group) // lanes + which for which in range(3)]

    def body(q_ref, kc_ref, kp_ref, vc_ref, vp_ref, o_ref, l_ref):
        hb, n = pl.program_id(0), pl.program_id(1)
        first_key = jnp.where(n > 0, 0, BLOCK_A)
        for hh in range(hp):
            ls = slice(HEAD_DIM * hh, HEAD_DIM * (hh + 1))
            sl = _slope_d(group, hb * hp + hh, d)
            for b in range(m):
                def one(r, b=b, ls=ls, sl=sl):
                    rq = _sub_rows(b, span, r, d)
                    if b == 0:
                        rp = _sub_rows(0, span, r, d)
                        kp, vp, fk = kp_ref[rp, ls], vp_ref[rp, ls], first_key
                    else:
                        rp = _sub_rows(b - 1, span, r, d)
                        kp, vp, fk = kc_ref[rp, ls], vc_ref[rp, ls], 0
                    o, lse = _attn_block(q_ref[rq, ls], kp, kc_ref[rq, ls], vp, vc_ref[rq, ls], sl, fk)
                    o_ref[rq, ls] = o
                    l_ref[rq, ls] = lse

                _for_each_r(d, one)

    def cur(cb):
        return pl.BlockSpec((tile, lanes), lambda hb, n: (n, cb + 3 * hb))

    def prev(cb):
        return pl.BlockSpec((span, lanes), lambda hb, n: (jnp.maximum(m * n - 1, 0), cb + 3 * hb))

    return pl.pallas_call(
        body, name=name, grid=(4 // hp, nt),
        in_specs=[cur(qb), cur(kb), prev(kb), cur(vb), prev(vb)],
        out_specs=[pl.BlockSpec((tile, lanes), lambda hb, n: (n, hb))] * 2,
        out_shape=[jax.ShapeDtypeStruct((T, D_ATTN_OUT), F32)] * 2,
        compiler_params=_cparams(2),
    )(proj, proj, proj, proj, proj)


def _attn_block_bwd(q, kp, kc, vp, vc, o, lse, do, dl, slope_d, first_key):
    scale = HEAD_DIM ** -0.5
    nt_dims = (((1,), (1,)), ((), ()))
    tn_dims = (((0,), (0,)), ((), ()))
    qb = q.astype(BF16)
    k2 = jnp.concatenate([kp, kc], axis=0).astype(BF16)
    v2 = jnp.concatenate([vp, vc], axis=0).astype(BF16)
    dob = do.astype(BF16)
    s = lax.dot_general(qb, k2, nt_dims, preferred_element_type=F32) * scale
    i = lax.broadcasted_iota(jnp.int32, (BLOCK_A, 2 * BLOCK_A), 0)
    j = lax.broadcasted_iota(jnp.int32, (BLOCK_A, 2 * BLOCK_A), 1)
    delta = BLOCK_A + i - j
    valid = (delta >= 0) & (delta <= W_SUB) & (j >= first_key)
    lse_col = jnp.max(lse, axis=-1, keepdims=True)
    p = jnp.exp(jnp.where(valid, s - slope_d * delta.astype(F32), -jnp.inf) - lse_col)
    dp = lax.dot_general(dob, v2, nt_dims, preferred_element_type=F32)
    row = jnp.sum(dl, axis=-1, keepdims=True) - jnp.sum(do * o, axis=-1, keepdims=True)
    ds = (p * (dp + row)).astype(BF16)
    dv2 = lax.dot_general(p.astype(BF16), dob, tn_dims, preferred_element_type=F32)
    dq = jnp.dot(ds, k2, preferred_element_type=F32) * scale
    dk2 = lax.dot_general(ds, qb, tn_dims, preferred_element_type=F32) * scale
    return dq, dk2[:BLOCK_A], dk2[BLOCK_A:], dv2[:BLOCK_A], dv2[BLOCK_A:]


def _attn_bwd(name, proj, o, lse, do, dl, group, T, dest):
    d, m, hp, span, tile, nt, lanes = _attn_geometry(group, T)
    qb, kb, vb = [(D_GROUP * group) // lanes + which for which in range(3)]
    n_in = 9 if dest is None else 10

    def body(*refs):
        q_ref, kc_ref, kp_ref, vc_ref, vp_ref, o_ref, l_ref, do_ref, dl_ref = refs[:9]
        out_ref, aq_ref, ak_ref, av_ref = refs[n_in:]
        hb, n = pl.program_id(0), pl.program_id(1)
        first_key = jnp.where(n > 0, 0, BLOCK_A)
        cur_slot = n % 2
        old_slot = 1 - cur_slot

        @pl.when(n == 0)
        def _():
            aq_ref[...] = jnp.zeros_like(aq_ref)
            ak_ref[...] = jnp.zeros_like(ak_ref)
            av_ref[...] = jnp.zeros_like(av_ref)

        @pl.when(n < nt)
        def _():
            for hh in range(hp):
                ls = slice(HEAD_DIM * hh, HEAD_DIM * (hh + 1))
                sl = _slope_d(group, hb * hp + hh, d)
                for b in range(m):
                    def one(r, b=b, ls=ls, sl=sl):
                        rq = _sub_rows(b, span, r, d)
                        if b == 0:
                            rp = _sub_rows(0, span, r, d)
                            kp, vp, fk = kp_ref[rp, ls], vp_ref[rp, ls], first_key
                        else:
                            rp = _sub_rows(b - 1, span, r, d)
                            kp, vp, fk = kc_ref[rp, ls], vc_ref[rp, ls], 0
                        dq, dkp, dkc, dvp, dvc = _attn_block_bwd(
                            q_ref[rq, ls], kp, kc_ref[rq, ls], vp, vc_ref[rq, ls], o_ref[rq, ls], l_ref[rq, ls],
                            do_ref[rq, ls], dl_ref[rq, ls], sl, fk)
                        aq_ref[cur_slot, rq, ls] = dq
                        ak_ref[cur_slot, rq, ls] = dkc
                        av_ref[cur_slot, rq, ls] = dvc
                        if b == 0:
                            last = _sub_rows(m - 1, span, r, d)
                            ak_ref[old_slot, last, ls] += dkp
                            av_ref[old_slot, last, ls] += dvp
                        else:
                            ak_ref[cur_slot, rp, ls] += dkp
                            av_ref[cur_slot, rp, ls] += dvp

                    _for_each_r(d, one)

        out_ref[:, 0:lanes] = aq_ref[old_slot].astype(BF16)
        out_ref[:, lanes:2 * lanes] = ak_ref[old_slot].astype(BF16)
        out_ref[:, 2 * lanes:3 * lanes] = av_ref[old_slot].astype(BF16)

    def cur(cb):
        return pl.BlockSpec((tile, lanes), lambda hb, n: (jnp.minimum(n, nt - 1), cb + 3 * hb))

    def prev(cb):
        return pl.BlockSpec((span, lanes), lambda hb, n: (jnp.maximum(m * jnp.minimum(n, nt - 1) - 1, 0), cb + 3 * hb))

    own = pl.BlockSpec((tile, lanes), lambda hb, n: (jnp.minimum(n, nt - 1), hb))
    late = pl.BlockSpec((tile, 3 * lanes), lambda hb, n: (jnp.maximum(n - 1, 0), (D_GROUP * group) // (3 * lanes) + hb))
    acc = pltpu.VMEM((2, tile, lanes), F32)
    return pl.pallas_call(
        body, name=name, grid=(4 // hp, nt + 1),
        in_specs=[cur(qb), cur(kb), prev(kb), cur(vb), prev(vb), own, own, own, own] + ([] if dest is None else [_ANY]),
        out_specs=late,
        out_shape=jax.ShapeDtypeStruct((T, D_PA), BF16),
        input_output_aliases={} if dest is None else {9: 0},
        scratch_shapes=[acc, acc, acc],
        compiler_params=_cparams(2),
    )(proj, proj, proj, proj, proj, o, lse, do, dl, *([] if dest is None else [dest]))


HALO = 8
HALO_IN = 16


def _shifted(ext):
    return [ext[HALO:, :]] + [pltpu.roll(ext, k, 0)[HALO:, :] for k in range(1, CONV_WIDTH)]


def _conv_taps(sh, w):
    c = sh[0] * w[CONV_WIDTH - 1]
    for k in range(1, CONV_WIDTH):
        c = c + sh[k] * w[CONV_WIDTH - 1 - k]
    return c


def _taps(w_ref):
    return [w_ref[j:j + 1, :] for j in range(CONV_WIDTH)]


def _dn_pre_fwd(name, proj, ba, conv_w, alog_v, dtb_v, tm, T):
    C = 3 * D_KEY_B

    def body(x_ref, ba_ref, w_ref, al_ref, dt_ref, q_ref, k_ref, v_ref, bg_ref, halo_ref):
        i = pl.program_id(0)

        @pl.when(i == 0)
        def _():
            halo_ref[...] = jnp.zeros_like(halo_ref)

        x = x_ref[...].astype(F32)
        ext = jnp.concatenate([halo_ref[...], x], axis=0)
        c = _conv_taps(_shifted(ext), _taps(w_ref))
        halo_ref[...] = x[tm - HALO:, :]
        qn, kn, v, bg = _f_dnpoint(c, ba_ref[...], al_ref[...], dt_ref[...])
        q_ref[...] = qn
        k_ref[...] = kn
        v_ref[...] = v
        bg_ref[...] = bg

    return pl.pallas_call(
        body, name=name, grid=(T // tm,),
        in_specs=[_row_spec(tm, C, 0), _row_spec(tm, 128, 0), _full_spec(conv_w.shape), _full_spec((1, 128)),
                  _full_spec((1, 128))],
        out_specs=[_row_spec(tm, D_KEY_B, 0)] * 3 + [_row_spec(tm, 128, 0)],
        out_shape=[jax.ShapeDtypeStruct((T, D_KEY_B), F32)] * 3 + [jax.ShapeDtypeStruct((T, 128), F32)],
        scratch_shapes=[pltpu.VMEM((HALO, C), F32)],
        compiler_params=_ARB,
    )(proj, ba, conv_w, alog_v, dtb_v)


def _dn_pre_bwd(name, proj, ba, conv_w, alog_v, dtb_v, dq, dk, dv, dbg, tm, T):
    C = 3 * D_KEY_B
    nt = T // tm
    hb = tm // HALO_IN

    def body(x_ref, xh_ref, ba_ref, w_ref, al_ref, dt_ref, dq_ref, dk_ref, dv_ref, dbg_ref,
             dx_ref, dba_ref, dw_ref, dal_ref, ddt_ref, nxt_ref):
        s = pl.program_id(0)
        i = nt - 1 - s

        @pl.when(s == 0)
        def _():
            nxt_ref[...] = jnp.zeros_like(nxt_ref)
            dw_ref[...] = jnp.zeros_like(dw_ref)
            dal_ref[...] = jnp.zeros_like(dal_ref)
            ddt_ref[...] = jnp.zeros_like(ddt_ref)

        w = _taps(w_ref)
        halo = jnp.where(i > 0, xh_ref[...].astype(F32)[HALO_IN - HALO:, :], 0.0)
        ext = jnp.concatenate([halo, x_ref[...].astype(F32)], axis=0)
        sh = _shifted(ext)
        c = _conv_taps(sh, w)
        _, vjp = jax.vjp(_f_dnpoint, c, ba_ref[...], al_ref[...], dt_ref[...])
        dc, dba, dal, ddt = vjp((dq_ref[...], dk_ref[...], dv_ref[...], dbg_ref[...]))
        dba_ref[...] = dba
        dal_ref[...] += dal
        ddt_ref[...] += ddt
        ext2 = jnp.concatenate([dc, nxt_ref[...]], axis=0)
        dx = dc * w[CONV_WIDTH - 1]
        for m in range(1, CONV_WIDTH):
            dx = dx + pltpu.roll(ext2, tm + HALO - m, 0)[:tm, :] * w[CONV_WIDTH - 1 - m]
        dx_ref[...] = dx.astype(dx_ref.dtype)
        nxt_ref[...] = dc[:HALO, :]
        for j in range(CONV_WIDTH):
            dw_ref[j:j + 1, :] += jnp.sum(sh[CONV_WIDTH - 1 - j] * dc, axis=0, keepdims=True)

    def rev(w):
        return pl.BlockSpec((tm, w), lambda s: (nt - 1 - s, 0))

    return pl.pallas_call(
        body, name=name, grid=(nt,),
        in_specs=[rev(C), pl.BlockSpec((HALO_IN, C), lambda s: (jnp.maximum((nt - 1 - s) * hb - 1, 0), 0)), rev(128),
                  _full_spec(conv_w.shape), _full_spec((1, 128)), _full_spec((1, 128)),
                  rev(D_KEY_B), rev(D_KEY_B), rev(D_KEY_B), rev(128)],
        out_specs=[rev(C), rev(128), _full_spec((8, C)), _full_spec((1, 128)), _full_spec((1, 128))],
        out_shape=[jax.ShapeDtypeStruct((T, C), BF16), jax.ShapeDtypeStruct((T, 128), F32),
                   jax.ShapeDtypeStruct((8, C), F32), jax.ShapeDtypeStruct((1, 128), F32),
                   jax.ShapeDtypeStruct((1, 128), F32)],
        scratch_shapes=[pltpu.VMEM((HALO, C), F32)],
        compiler_params=_ARB,
    )(proj, proj, ba, conv_w, alog_v, dtb_v, dq, dk, dv, dbg)


def _bdot(a, b, dims, exact=False):
    if exact:
        return lax.dot_general(a, b, dims, preferred_element_type=F32, precision=lax.Precision.HIGH)
    return lax.dot_general(a.astype(BF16), b.astype(BF16), dims, preferred_element_type=F32)


_B_NN = (((2,), (1,)), ((0,), (0,)))
_B_NT = (((2,), (2,)), ((0,), (0,)))
_B_TN = (((1,), (1,)), ((0,), (0,)))


def _unit_lower_inverse(a):
    C = a.shape[-1]
    r = lax.broadcasted_iota(jnp.int32, (C, C), 0)
    c = lax.broadcasted_iota(jnp.int32, (C, C), 1)
    pw = -a
    t = jnp.where((r == c)[None], 1.0, 0.0) + pw
    for _ in range(int(math.log2(C)) - 1):
        pw = _bdot(pw, pw, _B_NN, exact=True)
        t = t + _bdot(t, pw, _B_NN, exact=True)
    return t


@jax.custom_vjp
def _solve_given_inverse(t, a, rhs):
    return _bdot(t, rhs, _B_NN, exact=True)


def _solve_fwd(t, a, rhs):
    sol = _bdot(t, rhs, _B_NN, exact=True)
    return sol, (t, sol)


def _solve_bwd(res, dsol):
    t, sol = res
    drhs = _bdot(t, dsol, _B_TN, exact=True)
    return jnp.zeros_like(t), -_bdot(drhs, sol, _B_NT, exact=True), drhs


_solve_given_inverse.defvjp(_solve_fwd, _solve_bwd)


def _stack(xs):
    return jnp.concatenate([x[None] for x in xs], axis=0)


def _delta_chunk(S, q, k, v, bg, t_inv=None):
    H, C = N_HEADS_B, CHUNK
    r = lax.broadcasted_iota(jnp.int32, (C, C), 0)
    c = lax.broadcasted_iota(jnp.int32, (C, C), 1)
    incl, strict, eye = r >= c, r > c, r == c
    gc_all = jnp.dot(incl.astype(F32), bg, preferred_element_type=F32, precision=lax.Precision.HIGHEST)
    lane = lax.broadcasted_iota(jnp.int32, (C, 128), 1)

    def col(x, l):
        return jnp.sum(jnp.where(lane == l, x, 0.0), axis=1, keepdims=True)

    beta = _stack([col(bg, h) for h in range(H)])
    gc = _stack([col(gc_all, H + h) for h in range(H)])
    gc_row = jnp.sum(jnp.where(eye[None], jnp.broadcast_to(gc, (H, C, C)), 0.0), axis=1, keepdims=True)
    row = lax.broadcasted_iota(jnp.int32, (H, C, 1), 1)
    gc_last = jnp.sum(jnp.where(row == C - 1, gc, 0.0), axis=1, keepdims=True)
    decay = jnp.exp(jnp.where(incl[None], gc - gc_row, -jnp.inf))
    qh = _stack(_heads(q, H))
    kh = _stack(_heads(k, H))
    vh = _stack(_heads(v, H))
    kk = _bdot(kh, kh, _B_NT, exact=True)
    a = jnp.where(strict[None], beta * kk * decay, 0.0)
    egc = jnp.exp(gc)
    rhs = jnp.concatenate([beta * vh, (beta * egc) * kh], axis=-1)
    if t_inv is None:
        t_inv = _unit_lower_inverse(a)
        sol = _bdot(t_inv, rhs, _B_NN, exact=True)
    else:
        sol = _solve_given_inverse(t_inv, a, rhs)
    u_bar, w = sol[..., :HEAD_DIM], sol[..., HEAD_DIM:]
    qk = _bdot(qh, kh, _B_NT) * decay
    q_dec = qh * egc
    k_dec = kh * jnp.exp(gc_last - gc)
    u = u_bar - _bdot(w, S, _B_NN)
    o = _bdot(q_dec, S, _B_NN) + _bdot(qk, u, _B_NN)
    S_new = jnp.exp(gc_last) * S + _bdot(k_dec, u, _B_TN)
    o2 = jnp.concatenate([lax.index_in_dim(o, h, 0, keepdims=False) for h in range(H)], axis=1)
    return o2, S_new, t_inv


def _delta_fwd(name, q, k, v, bg, T):
    nc = T // CHUNK
    H = N_HEADS_B

    def body(q_ref, k_ref, v_ref, bg_ref, o_ref, s_out_ref, t_out_ref, s_ref):
        @pl.when(pl.program_id(0) == 0)
        def _():
            s_ref[...] = jnp.zeros_like(s_ref)

        S = s_ref[...]
        s_out_ref[0] = S
        o, S_new, t_inv = _delta_chunk(S, q_ref[...], k_ref[...], v_ref[...], bg_ref[...])
        o_ref[...] = o
        t_out_ref[0] = t_inv
        s_ref[...] = S_new

    return pl.pallas_call(
        body, name=name, grid=(nc,),
        in_specs=[_row_spec(CHUNK, D_KEY_B, 0)] * 3 + [_row_spec(CHUNK, 128, 0)],
        out_specs=[_row_spec(CHUNK, D_KEY_B, 0), pl.BlockSpec((1, H, HEAD_DIM, HEAD_DIM), lambda i: (i, 0, 0, 0)),
                   pl.BlockSpec((1, H, CHUNK, CHUNK), lambda i: (i, 0, 0, 0))],
        out_shape=[jax.ShapeDtypeStruct((T, D_KEY_B), F32), jax.ShapeDtypeStruct((nc, H, HEAD_DIM, HEAD_DIM), F32),
                   jax.ShapeDtypeStruct((nc, H, CHUNK, CHUNK), F32)],
        scratch_shapes=[pltpu.VMEM((H, HEAD_DIM, HEAD_DIM), F32)],
        compiler_params=_ARB,
    )(q, k, v, bg)


def _delta_bwd(name, q, k, v, bg, s_all, t_all, do, T):
    nc = T // CHUNK
    H = N_HEADS_B

    def body(q_ref, k_ref, v_ref, bg_ref, s_in_ref, t_ref, do_ref, dq_ref, dk_ref, dv_ref, dbg_ref, ds_ref):
        @pl.when(pl.program_id(0) == 0)
        def _():
            ds_ref[...] = jnp.zeros_like(ds_ref)

        t_inv = t_ref[0]

        def f(S, qq, kk, vv, bb):
            return _delta_chunk(S, qq, kk, vv, bb, t_inv)[:2]

        _, vjp = jax.vjp(f, s_in_ref[0], q_ref[...], k_ref[...], v_ref[...], bg_ref[...])
        dS, dq, dk, dv, dbg = vjp((do_ref[...], ds_ref[...]))
        dq_ref[...] = dq
        dk_ref[...] = dk
        dv_ref[...] = dv
        dbg_ref[...] = dbg
        ds_ref[...] = dS

    def rev(w):
        return pl.BlockSpec((CHUNK, w), lambda s: (nc - 1 - s, 0))

    def rev4(n):
        return pl.BlockSpec((1, H, n, n), lambda s: (nc - 1 - s, 0, 0, 0))

    return pl.pallas_call(
        body, name=name, grid=(nc,),
        in_specs=[rev(D_KEY_B)] * 3 + [rev(128), rev4(HEAD_DIM), rev4(CHUNK), rev(D_KEY_B)],
        out_specs=[rev(D_KEY_B)] * 3 + [rev(128)],
        out_shape=[jax.ShapeDtypeStruct((T, D_KEY_B), F32)] * 3 + [jax.ShapeDtypeStruct((T, 128), F32)],
        scratch_shapes=[pltpu.VMEM((H, HEAD_DIM, HEAD_DIM), F32)],
        compiler_params=_ARB,
    )(q, k, v, bg, s_all, t_all, do)


def _loss_fb(name, h2, w, tgt, tm, T):
    def tile_loss(h, wv, t):
        y = _f_rms(h, wv)[0]
        e = y - t
        return 0.5 * jnp.sum(jnp.mean(e * e, axis=-1))

    def body(h_ref, w_ref, t_ref, dh_ref, loss_ref, dw_ref):
        @pl.when(pl.program_id(0) == 0)
        def _():
            loss_ref[...] = jnp.zeros_like(loss_ref)
            dw_ref[...] = jnp.zeros_like(dw_ref)

        val, (dh, dw) = jax.value_and_grad(tile_loss, argnums=(0, 1))(h_ref[...], w_ref[...], t_ref[...])
        dh_ref[...] = dh
        dw_ref[...] += dw
        loss_ref[...] += jnp.reshape(val, (1, 1))

    return pl.pallas_call(
        body, name=name, grid=(T // tm,),
        in_specs=[_row_spec(tm, D_MODEL, 0), _full_spec((1, D_MODEL)), _row_spec(tm, D_MODEL, 0)],
        out_specs=[_row_spec(tm, D_MODEL, 0), _full_spec((1, 1)), _full_spec((1, D_MODEL))],
        out_shape=[jax.ShapeDtypeStruct((T, D_MODEL), F32), jax.ShapeDtypeStruct((1, 1), F32),
                   jax.ShapeDtypeStruct((1, D_MODEL), F32)],
        compiler_params=_ARB,
    )(h2, w, tgt)


def _adamw(name, w, g, m, v, tile):
    R = w.shape[0]

    def body(w_ref, g_ref, m_ref, v_ref, d_out, m_out, v_out):
        g = g_ref[...]
        mm = ADAM_B1 * m_ref[...] + (1.0 - ADAM_B1) * g
        vv = ADAM_B2 * v_ref[...] + (1.0 - ADAM_B2) * jnp.square(g)
        m_hat = mm / (1.0 - ADAM_B1 ** ADAM_STEP)
        v_hat = vv / (1.0 - ADAM_B2 ** ADAM_STEP)
        d_out[...] = -ADAM_LR * (m_hat / (jnp.sqrt(v_hat) + ADAM_EPS) + ADAM_WD * w_ref[...])
        m_out[...] = mm
        v_out[...] = vv

    assert R % tile == 0
    spec = pl.BlockSpec((tile, w.shape[1]), lambda i: (i, 0))
    return pl.pallas_call(
        body, name=name, grid=(R // tile,), in_specs=[spec] * 4, out_specs=[spec] * 3,
        out_shape=[jax.ShapeDtypeStruct(w.shape, F32)] * 3, compiler_params=_ARB,
    )(w, g, m, v)


def _sum4(name, r, tile):
    _, R, C = r.shape
    assert R % tile == 0

    def body(r_ref, o_ref):
        o_ref[...] = ((r_ref[0].astype(F32) + r_ref[1].astype(F32)) + r_ref[2].astype(F32)) + r_ref[3].astype(F32)

    return pl.pallas_call(
        body, name=name, grid=(R // tile,),
        in_specs=[pl.BlockSpec((4, tile, C), lambda i: (0, i, 0))],
        out_specs=pl.BlockSpec((tile, C), lambda i: (i, 0)),
        out_shape=jax.ShapeDtypeStruct((R, C), F32), compiler_params=_ARB,
    )(r)


def _add2(name, a, b, tile):
    R, C = a.shape
    assert R % tile == 0

    def body(a_ref, b_ref, o_ref):
        o_ref[...] = (a_ref[...].astype(F32) + b_ref[...].astype(F32)).astype(BF16)

    spec = pl.BlockSpec((tile, C), lambda i: (i, 0))
    return pl.pallas_call(
        body, name=name, grid=(R // tile,), in_specs=[spec, spec], out_specs=spec,
        out_shape=jax.ShapeDtypeStruct((R, C), BF16), compiler_params=_ARB,
    )(a, b)


_ANY = pl.BlockSpec(memory_space=pl.ANY)


def _place():
    return lax.axis_index("x"), lax.axis_index("y"), lax.axis_index("c")


def _chip_exchange(in_refs, out_refs, send_sems, recv_sems, local_sems, gather):
    n = len(in_refs)
    x, y, c = _place()
    me = 2 * x + y
    chips = [(1 - x, y), (x, 1 - y), (1 - x, 1 - y)]

    def src(a, slot):
        return in_refs[a] if gather else in_refs[a].at[slot]

    def copy(a, k, src_slot, dst_slot, px, py):
        return pltpu.make_async_remote_copy(src_ref=src(a, src_slot), dst_ref=out_refs[a].at[dst_slot],
                                            send_sem=send_sems.at[3 * a + k], recv_sem=recv_sems.at[3 * a + k],
                                            device_id=(px, py, c), device_id_type=MESH)

    def mine():
        return [pltpu.make_async_copy(src(a, me), out_refs[a].at[me], local_sems.at[a]) for a in range(n)]

    def sends():
        return [copy(a, k, 2 * px + py, me, px, py) for a in range(n) for k, (px, py) in enumerate(chips)]

    def start():
        for cp in mine() + sends():
            cp.start()

    def finish():
        for a in range(n):
            for k, (px, py) in enumerate(chips):
                copy(a, k, me, 2 * px + py, px, py).wait_recv()
        for cp in sends():
            cp.wait_send()
        for cp in mine():
            cp.wait()

    return start, finish


def _chip_exchange_shapes(arrays, gather):
    return [jax.ShapeDtypeStruct(((4,) + a.shape) if gather else a.shape, a.dtype) for a in arrays]


def _chip_exchange_sems(n):
    return [pltpu.SemaphoreType.DMA((3 * n,)), pltpu.SemaphoreType.DMA((3 * n,)), pltpu.SemaphoreType.DMA((n,))]


def _exchange_chips(name, arrays, gather):
    n = len(arrays)

    def body(*refs):
        start, finish = _chip_exchange(refs[:n], refs[n:2 * n], *refs[2 * n:], gather=gather)
        start()
        finish()

    return pl.pallas_call(
        body, name=name, in_specs=[_ANY] * n, out_specs=[_ANY] * n,
        out_shape=_chip_exchange_shapes(arrays, gather), scratch_shapes=_chip_exchange_sems(n),
    )(*arrays)


def _swap_sibling(name, arrays):
    n = len(arrays)

    def body(*refs):
        in_refs, out_refs, send_sems, recv_sems = refs[:n], refs[n:2 * n], refs[2 * n], refs[2 * n + 1]
        x, y, c = _place()
        cps = [pltpu.make_async_remote_copy(src_ref=in_refs[a], dst_ref=out_refs[a], send_sem=send_sems.at[a],
                                            recv_sem=recv_sems.at[a], device_id=(x, y, 1 - c), device_id_type=MESH)
               for a in range(n)]
        for cp in cps:
            cp.start()
        for cp in cps:
            cp.wait()

    return pl.pallas_call(
        body, name=name, in_specs=[_ANY] * n, out_specs=[_ANY] * n,
        out_shape=[jax.ShapeDtypeStruct(a.shape, a.dtype) for a in arrays],
        scratch_shapes=[pltpu.SemaphoreType.DMA((n,)), pltpu.SemaphoreType.DMA((n,))],
    )(*arrays)


def _allsum_small(name, pack):
    def body(p_ref, out_ref, slots, send_sems, recv_sems):
        x, y, c = _place()
        me = 4 * x + 2 * y + c
        slots[me] = p_ref[...]
        sends = []
        for k in range(1, 8):
            px = 1 - x if k & 4 else x
            py = 1 - y if k & 2 else y
            pc = 1 - c if k & 1 else c
            cp = pltpu.make_async_remote_copy(src_ref=p_ref, dst_ref=slots.at[me], send_sem=send_sems.at[k - 1],
                                              recv_sem=recv_sems.at[k - 1], device_id=(px, py, pc), device_id_type=MESH)
            cp.start()
            sends.append(cp)
        for k in range(1, 8):
            px = 1 - x if k & 4 else x
            py = 1 - y if k & 2 else y
            pc = 1 - c if k & 1 else c
            pltpu.make_async_remote_copy(src_ref=p_ref, dst_ref=slots.at[4 * px + 2 * py + pc],
                                         send_sem=send_sems.at[k - 1], recv_sem=recv_sems.at[k - 1],
                                         device_id=(px, py, pc), device_id_type=MESH).wait_recv()
        for cp in sends:
            cp.wait_send()
        acc = slots[0]
        for s in range(1, 8):
            acc = acc + slots[s]
        out_ref[...] = acc

    vm = pl.BlockSpec(memory_space=pltpu.VMEM)
    return pl.pallas_call(
        body, name=name, in_specs=[vm], out_specs=vm,
        out_shape=jax.ShapeDtypeStruct(pack.shape, pack.dtype),
        scratch_shapes=[pltpu.VMEM((8,) + pack.shape, pack.dtype), pltpu.SemaphoreType.DMA((7,)),
                        pltpu.SemaphoreType.DMA((7,))],
    )(pack)


def _pack_shards(parts, rows_total):
    flat = [p.reshape(-1, D_MODEL) for p in parts]
    used = sum(f.shape[0] for f in flat)
    return jnp.concatenate(flat + [jnp.zeros((rows_total - used, D_MODEL), flat[0].dtype)], axis=0)


def _unpack_shards(buf, shapes):
    out, r0 = [], 0
    for shp in shapes:
        n = math.prod(shp) // D_MODEL
        out.append(buf[r0:r0 + n].reshape(shp))
        r0 += n
    return out


def kernel(x, norm_mix, w_in, conv_w, a_log, dt_bias, dn_norm, w_proj_attn, w_proj_delta, w_out, norm_ffn, w_gate, w_up, w_down, norm_final, loss_target, m_norm_mix, m_w_in, m_conv_w, m_a_log, m_dt_bias, m_dn_norm, m_w_proj_attn, m_w_proj_delta, m_w_out, m_norm_ffn, m_w_gate, m_w_up, m_w_down, m_norm_final, v_norm_mix, v_w_in, v_conv_w, v_a_log, v_dt_bias, v_dn_norm, v_w_proj_attn, v_w_proj_delta, v_w_out, v_norm_ffn, v_w_gate, v_w_up, v_w_down, v_norm_final):
    T = x.shape[1]
    D = D_MODEL
    tm = 512
    tw = 256
    mt = 1024 if T % 1024 == 0 else 512
    x2 = x.reshape(T, D)
    tgt = loss_target.reshape(T, D)

    packed = (w_proj_attn, w_proj_delta, w_out, w_gate, w_up, w_down)
    conv_bits = lax.bitcast_convert_type(conv_w[0], BF16)
    w16 = _pack_shards([w[0].astype(BF16) for w in packed] + [conv_bits], PACK_R)
    win16 = w_in[0].astype(BF16)
    my_c = lax.axis_index("c")

    def both_halves(mine, other, axis):
        return jnp.concatenate([jnp.where(my_c == 0, mine, other), jnp.where(my_c == 0, other, mine)], axis=axis)

    pack_half = lax.dynamic_slice_in_dim(w16, my_c * HALF, HALF, axis=0)
    (win_mine,) = _exchange_chips("gather_w_in", [lax.dynamic_slice_in_dim(win16, my_c * W_IN_HALF, W_IN_HALF, axis=0)], True)
    (win_other,) = _swap_sibling("swap_w_in", [win_mine])
    win4 = both_halves(win_mine, win_other, 1)

    def win_cols(g0, g1):
        out = []
        for j in range(4):
            lo, hi = max(g0, W_IN_COLS * j), min(g1, W_IN_COLS * (j + 1))
            if lo < hi:
                out.append(win4[j][:, lo - W_IN_COLS * j:hi - W_IN_COLS * j])
        return out

    Wa = jnp.concatenate([p for c0, w in _att_columns() for p in win_cols(c0, c0 + w)], axis=1)
    Wb = jnp.concatenate(win_cols(4608, 8704) + win_cols(8720, D_IN), axis=1)
    Wba = jnp.pad(jnp.concatenate(win_cols(8704, 8720), axis=1), ((0, 0), (0, 112)))
    alog_v = jnp.pad(a_log, ((0, 0), (8, 112)))
    dtb_v = jnp.pad(dt_bias, ((0, 0), (8, 112)))

    (u,) = _rowwise("rms_mix", _f_rms, [(x2, 0, D)], [norm_mix], [(D, BF16)], tm, T)
    proj_a, pack_mine = _matmul("mm_in_a", u, Wa, "nn", T, D_PA, D, mt, 1536, D, F32, side=([pack_half], True))
    (pack_other,) = _swap_sibling("swap_pack", [pack_mine])
    wg = both_halves(pack_mine, pack_other, 1)

    def seg(r0, n):
        return wg[:, r0:r0 + n, :]

    r = 0
    Wpa = seg(r, 128).reshape(4, D_ATTN_OUT, 256).transpose(1, 0, 2).reshape(D_ATTN_OUT, D); r += 128
    Wpd = seg(r, 256).reshape(D, D); r += 256
    Wout = seg(r, 256).reshape(D, D); r += 256
    Wg = seg(r, 704).reshape(4, D, 704).transpose(1, 0, 2).reshape(D, D_FF); r += 704
    Wu = seg(r, 704).reshape(4, D, 704).transpose(1, 0, 2).reshape(D, D_FF); r += 704
    Wd = seg(r, 704).reshape(D_FF, D); r += 704
    convw = lax.bitcast_convert_type(seg(r, 6).reshape(4, CONV_WIDTH, 768, 2), F32)
    convw = convw.transpose(1, 0, 2).reshape(CONV_WIDTH, 3 * D_KEY_B)
    Wgu = jnp.concatenate([Wg, Wu], axis=1)
    proj_b = _matmul("mm_in_b", u, Wb, "nn", T, D_PB, D, mt, 1536, D, BF16)
    ba = _matmul("mm_ba", u, Wba, "nn", T, 128, D, mt, 128, D, F32)

    att = [_attn_fwd(f"attn_fwd{g}", proj_a, g, T) for g in range(3)]
    (ya_m,) = _rowwise("merge", _f_merge, [(att[g][0], 0, 512) for g in range(3)] + [(att[g][1], 0, 512) for g in range(3)],
                       [], [(512, BF16)], tm, T)
    qn, kn, vd, bg = _dn_pre_fwd("dn_pre_fwd", proj_b, ba, convw, alog_v, dtb_v, tw, T)
    o_d, s_all, t_all = _delta_fwd("delta_fwd", qn, kn, vd, bg, T)
    (od,) = _rowwise("post_dn", _f_postdn, [(o_d, 0, D), (proj_b, C_Z // D, D)], [dn_norm], [(D, BF16)], tm, T)
    y_a = _matmul("mm_pa", ya_m, Wpa, "nn", T, D, 512, mt, D, 512, F32)
    y_b = _matmul("mm_pd", od, Wpd, "nn", T, D, D, mt, D, D, F32)
    (mix,) = _rowwise("mix", _f_mix, [(y_a, 0, D), (y_b, 0, D), (proj_b, C_GATE // 2048, 2048)], [], [(D, BF16)], tm, T)
    h1 = _matmul("mm_out", mix, Wout, "nn", T, D, D, mt, D, D, F32, residual=x2)
    (hn,) = _rowwise("rms_ffn", _f_rms, [(h1, 0, D)], [norm_ffn], [(D, BF16)], tm, T)
    gu = _matmul("mm_gu", hn, Wgu, "nn", T, 2 * D_FF, D, mt, 1408, D, BF16)
    (act,) = _rowwise("act", _f_act, [(gu, 0, 2 * D_FF)], [], [(D_FF, BF16)], tw, T)
    h2 = _matmul("mm_down", act, Wd, "nn", T, D, D_FF, mt, D, 1408, F32, residual=h1)
    dh2, loss, g_norm_final = _loss_fb("loss", h2, norm_final.reshape(1, D), tgt, tm, T)

    dact = _matmul("mm_dact", dh2, Wd, "nt", T, D_FF, D, mt, 1408, D, BF16)
    gWd = _matmul("mm_gwd", act, dh2, "tn", D_FF, D, T, 1408, D, mt, BF16)
    (dgu,) = _rowwise_vjp("act_bwd", _f_act, [(gu, 0, 2 * D_FF)], [], [(dact, 0, D_FF)], [BF16], tw, T)
    dhn = _matmul("mm_dhn", dgu, Wgu, "nt", T, D, 2 * D_FF, mt, D, 1408, F32)
    gWgu = _matmul("mm_gwgu", hn, dgu, "tn", D, 2 * D_FF, T, D, 1408, mt, BF16)
    dh1, g_norm_ffn = _rowwise_vjp("rms_ffn_bwd", _f_rms, [(h1, 0, D)], [norm_ffn], [(dhn, 0, D)], [F32], tm, T,
                                   residual=(dh2, 0, D))
    dmix = _matmul("mm_dmix", dh1, Wout, "nt", T, D, D, mt, D, D, F32)
    gWout = _matmul("mm_gwout", mix, dh1, "tn", D, D, T, D, D, mt, BF16)
    dya, dyb, dgate = _rowwise_vjp("mix_bwd", _f_mix, [(y_a, 0, D), (y_b, 0, D), (proj_b, C_GATE // 2048, 2048)], [],
                                   [(dmix, 0, D)], [BF16, BF16, BF16], tm, T)
    dya_m = _matmul("mm_dyam", dya, Wpa, "nt", T, 512, D, mt, 512, D, F32)
    gWpa = _matmul("mm_gwpa", ya_m, dya, "tn", 512, D, T, 512, D, mt, BF16)
    dod = _matmul("mm_dod", dyb, Wpd, "nt", T, D, D, mt, D, D, F32)
    gWpd = _matmul("mm_gwpd", od, dyb, "tn", D, D, T, D, D, mt, BF16)
    do_d, dz, g_dn_norm = _rowwise_vjp("post_dn_bwd", _f_postdn, [(o_d, 0, D), (proj_b, C_Z // D, D)], [dn_norm],
                                       [(dod, 0, D)], [F32, BF16], tm, T)
    dqn, dkn, dvd, dbg = _delta_bwd("delta_bwd", qn, kn, vd, bg, s_all, t_all, do_d, T)
    dqkvd, dba, g_conv8, g_alog_v, g_dtb_v = _dn_pre_bwd("dn_pre_bwd", proj_b, ba, convw, alog_v, dtb_v,
                                                         dqn, dkn, dvd, dbg, tw, T)
    def pair_sum(tag, g, h):
        keep = lax.dynamic_slice_in_dim(g, my_c * h, h, axis=1)
        give = lax.dynamic_slice_in_dim(g, (1 - my_c) * h, h, axis=1)
        (got,) = _swap_sibling("swap_grads_" + tag, [give])
        return _add2("add_cores_" + tag, keep.reshape(-1, g.shape[2]), got.reshape(-1, g.shape[2]), HALF_TILE).reshape(keep.shape)

    def cols4(g, n):
        return g.reshape(g.shape[0], 4, n).transpose(1, 0, 2).reshape(4, -1, D)

    parts = [cols4(gWpa, 256), gWpd.reshape(4, 256, D), gWout.reshape(4, 256, D), cols4(gWgu[:, :D_FF], 704),
             cols4(gWgu[:, D_FF:], 704), gWd.reshape(4, 704, D), cols4(g_conv8[:CONV_WIDTH], 768)]
    used = sum(p.shape[1] for p in parts)
    g_pack = jnp.concatenate([p.astype(BF16) for p in parts] + [jnp.zeros((4, PACK_R - used, D), BF16)], axis=1)
    pair_pack = pair_sum("pack", g_pack, HALF)

    mg = _rowwise_vjp("merge_bwd", _f_merge,
                      [(att[g][0], 0, 512) for g in range(3)] + [(att[g][1], 0, 512) for g in range(3)], [],
                      [(dya_m, 0, 512)], [F32] * 6, tm, T)
    dproj_a = None
    for g in range(3):
        dproj_a = _attn_bwd(f"attn_bwd{g}", proj_a, att[g][0], att[g][1], mg[g], mg[3 + g], g, T, dproj_a)
    dproj_b = [dqkvd, dz, dgate]
    gWa, recv_pack = _matmul("mm_gwa", u, dproj_a, "tn", D, D_PA, T, D, 1536, mt, BF16, side=([pair_pack], False))
    gWb = _matmul_pieces("mm_gwb", "tn", dproj_b, u, D, D_PB, T, D, 1024, mt, BF16)
    gWba = _matmul("mm_gwba", u, dba, "tn", D, 128, T, D, 128, mt, BF16)

    pieces, off = [(gWb[:, :C_GATE], D_PA), (gWba[:, :16], 8704), (gWb[:, C_GATE:], 8720)], 0
    for c0, w in _att_columns():
        pieces.append((gWa[:, off:off + w], c0))
        off += w
    pieces.sort(key=lambda t: t[1])

    def win_shard(j):
        out = []
        for arr, g0 in pieces:
            lo, hi = max(g0, W_IN_COLS * j), min(g0 + arr.shape[1], W_IN_COLS * (j + 1))
            if lo < hi:
                out.append(arr[:, lo - g0:hi - g0])
        return jnp.concatenate(out, axis=1)

    g_win4 = jnp.concatenate([win_shard(j)[None] for j in range(4)], axis=0)
    pair_win = pair_sum("w_in", g_win4, W_IN_HALF)

    du0 = _matmul("mm_du_ba", dba, Wba, "nt", T, D, 128, mt, D, 128, F32)
    du1 = _matmul_pieces("mm_du_b", "nt", dproj_b, Wb, T, D, D_PB, mt, D, 1024, F32, residual=du0)
    du, recv_win = _matmul("mm_du_a", dproj_a, Wa, "nt", T, D, D_PA, mt, D, 1536, F32, residual=du1, side=([pair_win], False))
    dx, g_norm_mix = _rowwise_vjp("rms_mix_bwd", _f_rms, [(x2, 0, D)], [norm_mix], [(du, 0, D)], [F32], tm, T,
                                  residual=(dh1, 0, D))

    g_mine = [_sum4("sum_chips_pack", recv_pack, HALF_TILE), _sum4("sum_chips_w_in", recv_win, HALF_TILE)]
    g_other = _swap_sibling("swap_sums", g_mine)
    g_all = both_halves(g_mine[0], g_other[0], 0)
    g_win = both_halves(g_mine[1], g_other[1], 0)

    locals_ = (w_proj_attn, w_proj_delta, w_out, w_gate, w_up, w_down, conv_w, w_in)
    ms = (m_w_proj_attn, m_w_proj_delta, m_w_out, m_w_gate, m_w_up, m_w_down, m_conv_w, m_w_in)
    vs = (v_w_proj_attn, v_w_proj_delta, v_w_out, v_w_gate, v_w_up, v_w_down, v_conv_w, v_w_in)
    big_names = ("w_proj_attn", "w_proj_delta", "w_out", "w_gate", "w_up", "w_down", "conv_w", "w_in")
    adam_tiles = (512, 256, 256, 256, 256, 352, CONV_WIDTH, 128)
    g_shards = _unpack_shards(g_all, [w.shape for w in locals_[:-1]]) + [g_win.reshape(w_in.shape)]
    big = [g_shards, [], [], []]
    for nm, w, g, m_, v_, tl in zip(big_names, locals_, g_shards, ms, vs, adam_tiles):
        res = _adamw("adamw_" + nm, w[0], g[0], m_[0], v_[0], tl)
        for kind in range(3):
            big[kind + 1].append(res[kind].reshape(w.shape))

    def small_pack(nm, nf, nfin, dn, al, dt, extra):
        row3 = jnp.concatenate([dn.reshape(1, 128), al.reshape(1, 8), dt.reshape(1, 8), extra.reshape(1, 1),
                                jnp.zeros((1, D - 145), F32)], axis=1)
        return jnp.concatenate([nm.reshape(1, D), nf.reshape(1, D), nfin.reshape(1, D), row3, jnp.zeros((4, D), F32)], axis=0)

    zero1 = jnp.zeros((1, 1), F32)
    sp = small_pack(g_norm_mix, g_norm_ffn, g_norm_final, g_dn_norm, g_alog_v[:, 8:16], g_dtb_v[:, 8:16], loss)
    tot = _allsum_small("allsum_small", sp)
    sm = _adamw("adamw_small", small_pack(norm_mix, norm_ffn, norm_final, dn_norm, a_log, dt_bias, zero1), tot,
                small_pack(m_norm_mix, m_norm_ffn, m_norm_final, m_dn_norm, m_a_log, m_dt_bias, zero1),
                small_pack(v_norm_mix, v_norm_ffn, v_norm_final, v_dn_norm, v_a_log, v_dt_bias, zero1), 8)
    sm = [tot] + list(sm)

    def small_unpack(p):
        return {"norm_mix": p[0:1], "norm_ffn": p[1:2], "norm_final": p[2], "dn_norm": p[3:4, :128],
                "a_log": p[3:4, 128:136], "dt_bias": p[3:4, 136:144]}

    sm = [small_unpack(p) for p in sm]
    loss_out = tot[3, 144]

    order = ("norm_mix", "w_in", "conv_w", "a_log", "dt_bias", "dn_norm", "w_proj_attn", "w_proj_delta", "w_out",
             "norm_ffn", "w_gate", "w_up", "w_down", "norm_final")
    outs = [loss_out, dx.reshape(x.shape)]
    for kind in range(4):
        for n in order:
            outs.append(big[kind][big_names.index(n)] if n in big_names else sm[kind][n])
    return tuple(outs)
```

```python
import functools
import math

import jax
import jax.numpy as jnp
from jax import lax
from jax.experimental import pallas as pl
from jax.experimental.pallas import tpu as pltpu

F32 = jnp.float32
BF16 = jnp.bfloat16
MESH = pl.DeviceIdType.MESH

D_MODEL = 1024
N_HEADS_A = 12
HEAD_DIM = 128
BLOCK_A = 128
DILATIONS = (1, 4, 16)
W_SUB = 128
D_ATTN_OUT = 512
N_HEADS_B = 8
D_KEY_B = 1024
CONV_WIDTH = 4
CHUNK = 64
D_FF = 2816
EPS = 1e-6
D_IN = 10768
D_PA = 4608
D_PB = 6144
C_Z, C_GATE = 3072, 4096

ADAM_LR, ADAM_B1, ADAM_B2, ADAM_EPS, ADAM_WD, ADAM_STEP = 0.001, 0.9, 0.999, 1e-08, 0.01, 10

PACK_R = 2816
HALF = PACK_R // 2
HALF_TILE = 128
W_IN_COLS = 2692
W_IN_HALF = D_MODEL // 2

VMEM_LIMIT = 48 * 1024 * 1024


def _cparams(n):
    return pltpu.CompilerParams(dimension_semantics=("arbitrary",) * n, vmem_limit_bytes=VMEM_LIMIT)


_ARB = _cparams(1)


def _row_spec(tm, w, cb):
    return pl.BlockSpec((tm, w), lambda i: (i, cb))


def _full_spec(shape):
    nd = len(shape)
    return pl.BlockSpec(shape, lambda i: (0,) * nd)


def _rowwise(name, fn, ins, params, outs, tm, T):
    n_in, n_p = len(ins), len(params)

    def body(*refs):
        xs = [r[...].astype(F32) for r in refs[:n_in]] + [r[...] for r in refs[n_in:n_in + n_p]]
        res = fn(*xs)
        for o_ref, v in zip(refs[n_in + n_p:], res):
            o_ref[...] = v.astype(o_ref.dtype)

    return pl.pallas_call(
        body, name=name, grid=(T // tm,),
        in_specs=[_row_spec(tm, w, cb) for (_, cb, w) in ins] + [_full_spec(p.shape) for p in params],
        out_specs=[_row_spec(tm, w, 0) for (w, _) in outs],
        out_shape=[jax.ShapeDtypeStruct((T, w), dt) for (w, dt) in outs],
        compiler_params=_ARB,
    )(*[a for (a, _, _) in ins], *params)


def _rowwise_vjp(name, fn, ins, params, cts, gdtypes, tm, T, residual=None):
    n_in, n_p, n_ct = len(ins), len(params), len(cts)
    n_res = 0 if residual is None else 1

    def body(*refs):
        i = pl.program_id(0)
        k = 0
        xs = [r[...].astype(F32) for r in refs[k:k + n_in]]
        k += n_in
        ps = [r[...] for r in refs[k:k + n_p]]
        k += n_p
        gs = tuple(r[...].astype(F32) for r in refs[k:k + n_ct])
        k += n_ct
        res_refs = refs[k:k + n_res]
        k += n_res
        gi_refs = refs[k:k + n_in]
        gp_refs = refs[k + n_in:]
        _, vjp = jax.vjp(fn, *xs, *ps)
        g = vjp(gs)
        for j, (ref, v) in enumerate(zip(gi_refs, g[:n_in])):
            if j == 0 and n_res:
                v = v + res_refs[0][...].astype(F32)
            ref[...] = v.astype(ref.dtype)

        @pl.when(i == 0)
        def _():
            for ref in gp_refs:
                ref[...] = jnp.zeros_like(ref)

        for ref, v in zip(gp_refs, g[n_in:]):
            ref[...] += v

    res_in = [] if residual is None else [residual]
    return pl.pallas_call(
        body, name=name, grid=(T // tm,),
        in_specs=([_row_spec(tm, w, cb) for (_, cb, w) in ins] + [_full_spec(p.shape) for p in params]
                  + [_row_spec(tm, w, cb) for (_, cb, w) in cts] + [_row_spec(tm, w, cb) for (_, cb, w) in res_in]),
        out_specs=[_row_spec(tm, w, 0) for (_, _, w) in ins] + [_full_spec(p.shape) for p in params],
        out_shape=([jax.ShapeDtypeStruct((T, w), dt) for (_, _, w), dt in zip(ins, gdtypes)]
                   + [jax.ShapeDtypeStruct(p.shape, F32) for p in params]),
        compiler_params=_ARB,
    )(*[a for (a, _, _) in ins], *params, *[a for (a, _, _) in cts], *[a for (a, _, _) in res_in])


def _matmul(name, a, b, mode, M, N, K, tm, tn, tk, out_dtype, a_off=0, b_off=0, residual=None, side=None):
    nk = K // tk
    n_side = 0 if side is None else len(side[0])
    assert M % tm == 0 and N % tn == 0 and K % tk == 0
    if mode == "nn":
        assert a_off % tk == 0 and b_off % tn == 0
        a_spec = pl.BlockSpec((tm, tk), lambda i, j, k: (i, a_off // tk + k))
        b_spec = pl.BlockSpec((tk, tn), lambda i, j, k: (k, b_off // tn + j))
        dims = (((1,), (0,)), ((), ()))
    elif mode == "nt":
        assert a_off % tk == 0 and b_off == 0
        a_spec = pl.BlockSpec((tm, tk), lambda i, j, k: (i, a_off // tk + k))
        b_spec = pl.BlockSpec((tn, tk), lambda i, j, k: (j, k))
        dims = (((1,), (1,)), ((), ()))
    else:
        assert a_off % tm == 0 and b_off % tn == 0
        a_spec = pl.BlockSpec((tk, tm), lambda i, j, k: (k, a_off // tm + i))
        b_spec = pl.BlockSpec((tk, tn), lambda i, j, k: (k, b_off // tn + j))
        dims = (((0,), (0,)), ((), ()))
    has_res = residual is not None

    n_main = 3 if has_res else 2
    grid = (M // tm, N // tn, nk)

    def body(*refs):
        a_ref, b_ref = refs[0], refs[1]
        r_ref = refs[2] if has_res else None
        o_ref = refs[n_main + n_side]
        n_acc = 1 if nk > 1 else 0
        i, j, k = pl.program_id(0), pl.program_id(1), pl.program_id(2)
        if n_side:
            start, finish = _chip_exchange(refs[n_main:n_main + n_side], refs[n_main + n_side + 1:n_main + 2 * n_side + 1],
                                           *refs[n_main + 2 * n_side + 1 + n_acc:], gather=side[1])
            pl.when((i == 0) & (j == 0) & (k == 0))(start)

        part = lax.dot_general(a_ref[...].astype(BF16), b_ref[...].astype(BF16), dims, preferred_element_type=F32)
        if nk == 1:
            if has_res:
                part = part + r_ref[...].astype(F32)
            o_ref[...] = part.astype(o_ref.dtype)
        else:
            acc_ref = refs[n_main + 2 * n_side + 1]

            @pl.when(k == 0)
            def _():
                acc_ref[...] = part

            @pl.when(k > 0)
            def _():
                acc_ref[...] += part

            @pl.when(k == nk - 1)
            def _():
                acc = acc_ref[...]
                if has_res:
                    acc = acc + r_ref[...].astype(F32)
                o_ref[...] = acc.astype(o_ref.dtype)

        if n_side:
            pl.when((i == grid[0] - 1) & (j == grid[1] - 1) & (k == nk - 1))(finish)

    in_specs = [a_spec, b_spec]
    args = [a, b]
    if has_res:
        in_specs.append(pl.BlockSpec((tm, tn), lambda i, j, k: (i, j)))
        args.append(residual)
    out_specs = pl.BlockSpec((tm, tn), lambda i, j, k: (i, j))
    out_shape = jax.ShapeDtypeStruct((M, N), out_dtype)
    scratch = [pltpu.VMEM((tm, tn), F32)] if nk > 1 else []
    semantics = ("parallel", "parallel", "arbitrary")
    if n_side:
        in_specs += [_ANY] * n_side
        args += list(side[0])
        out_specs = [out_specs] + [_ANY] * n_side
        out_shape = [out_shape] + _chip_exchange_shapes(*side)
        scratch += _chip_exchange_sems(n_side)
        semantics = ("arbitrary",) * 3
    return pl.pallas_call(
        body, name=name, grid=grid, in_specs=in_specs, out_specs=out_specs, out_shape=out_shape, scratch_shapes=scratch,
        compiler_params=pltpu.CompilerParams(dimension_semantics=semantics, vmem_limit_bytes=VMEM_LIMIT),
    )(*args)


def _matmul_pieces(name, mode, pieces, other, M, N, K, tm, tn, tk, out_dtype, residual=None):
    nk = K // tk
    blk = tn if mode == "tn" else tk
    counts = [p.shape[1] // blk for p in pieces]
    assert all(p.shape[1] % blk == 0 for p in pieces) and sum(counts) == (N if mode == "tn" else K) // blk
    starts = [sum(counts[:p]) for p in range(len(pieces))]
    n_p = len(pieces)
    has_res = residual is not None

    def piece_spec(s, n):
        if mode == "tn":
            return pl.BlockSpec((tk, tn), lambda i, j, k: (jnp.where((j >= s) & (j < s + n), k, 0), jnp.clip(j - s, 0, n - 1)))
        return pl.BlockSpec((tm, tk), lambda i, j, k: (i, jnp.clip(k - s, 0, n - 1)))

    if mode == "tn":
        other_spec = pl.BlockSpec((tk, tm), lambda i, j, k: (k, i))
    else:
        other_spec = pl.BlockSpec((tn, tk), lambda i, j, k: (j, k))

    def body(*refs):
        other_ref, piece_refs = refs[0], refs[1:1 + n_p]
        r_ref = refs[1 + n_p] if has_res else None
        o_ref, acc_ref = refs[-2], refs[-1]
        k = pl.program_id(2)
        sel = pl.program_id(1) if mode == "tn" else k

        @pl.when(k == 0)
        def _():
            acc_ref[...] = jnp.zeros_like(acc_ref)

        for p_ref, s, n in zip(piece_refs, starts, counts):
            @pl.when((sel >= s) & (sel < s + n))
            def _(p_ref=p_ref):
                if mode == "tn":
                    acc_ref[...] += lax.dot_general(other_ref[...].astype(BF16), p_ref[...].astype(BF16),
                                                    (((0,), (0,)), ((), ())), preferred_element_type=F32)
                else:
                    acc_ref[...] += lax.dot_general(p_ref[...].astype(BF16), other_ref[...].astype(BF16),
                                                    (((1,), (1,)), ((), ())), preferred_element_type=F32)

        @pl.when(k == nk - 1)
        def _():
            acc = acc_ref[...]
            if has_res:
                acc = acc + r_ref[...].astype(F32)
            o_ref[...] = acc.astype(o_ref.dtype)

    in_specs = [other_spec] + [piece_spec(s, n) for s, n in zip(starts, counts)]
    args = [other] + list(pieces)
    if has_res:
        in_specs.append(pl.BlockSpec((tm, tn), lambda i, j, k: (i, j)))
        args.append(residual)
    return pl.pallas_call(
        body, name=name, grid=(M // tm, N // tn, nk),
        in_specs=in_specs,
        out_specs=pl.BlockSpec((tm, tn), lambda i, j, k: (i, j)),
        out_shape=jax.ShapeDtypeStruct((M, N), out_dtype),
        scratch_shapes=[pltpu.VMEM((tm, tn), F32)],
        compiler_params=pltpu.CompilerParams(dimension_semantics=("parallel", "parallel", "arbitrary"),
                                             vmem_limit_bytes=VMEM_LIMIT),
    )(*args)


def _f_rms(x, w):
    return (x * lax.rsqrt(jnp.mean(x * x, axis=-1, keepdims=True) + EPS) * w,)


def _silu(x):
    return x * jax.nn.sigmoid(x)


def _softplus(x):
    return jnp.maximum(x, 0.0) + jnp.log1p(jnp.exp(-jnp.abs(x)))


def _heads(x, n):
    return [x[:, HEAD_DIM * h:HEAD_DIM * (h + 1)] for h in range(n)]


def _f_dnpoint(c, ba, alog_v, dtb_v):
    cs = _silu(c)
    q, k, v = cs[:, :D_KEY_B], cs[:, D_KEY_B:2 * D_KEY_B], cs[:, 2 * D_KEY_B:]
    qn = jnp.concatenate([t * lax.rsqrt(jnp.sum(t * t, axis=-1, keepdims=True) + EPS) * (HEAD_DIM ** -0.5)
                          for t in _heads(q, N_HEADS_B)], axis=1)
    kn = jnp.concatenate([t * lax.rsqrt(jnp.sum(t * t, axis=-1, keepdims=True) + EPS)
                          for t in _heads(k, N_HEADS_B)], axis=1)
    lane = lax.broadcasted_iota(jnp.int32, ba.shape, 1)
    beta = jax.nn.sigmoid(ba)
    g = -jnp.exp(alog_v) * _softplus(ba + dtb_v)
    bg = jnp.where(lane < N_HEADS_B, beta, jnp.where(lane < 2 * N_HEADS_B, g, 0.0))
    return qn, kn, v, bg


def _f_postdn(o, z, w):
    outs = []
    for oh, zh in zip(_heads(o, N_HEADS_B), _heads(z, N_HEADS_B)):
        outs.append(oh * lax.rsqrt(jnp.mean(oh * oh, axis=-1, keepdims=True) + EPS) * w * _silu(zh))
    return (jnp.concatenate(outs, axis=1),)


def _f_merge(o0, o1, o2, l0, l1, l2):
    m = lax.stop_gradient(jnp.maximum(jnp.maximum(l0, l1), l2))
    e0, e1, e2 = jnp.exp(l0 - m), jnp.exp(l1 - m), jnp.exp(l2 - m)
    return ((e0 * o0 + e1 * o1 + e2 * o2) / (e0 + e1 + e2),)


def _f_mix(ya, yb, gr):
    return (jax.nn.sigmoid(gr[:, :D_MODEL]) * ya + jax.nn.sigmoid(gr[:, D_MODEL:]) * yb,)


def _f_act(gu):
    return (_silu(gu[:, :D_FF]) * gu[:, D_FF:],)


def _attn_block(q, kp, kc, vp, vc, slope_d, first_key):
    k2 = jnp.concatenate([kp, kc], axis=0).astype(BF16)
    v2 = jnp.concatenate([vp, vc], axis=0).astype(BF16)
    s = lax.dot_general(q.astype(BF16), k2, (((1,), (1,)), ((), ())), preferred_element_type=F32)
    s = s * (HEAD_DIM ** -0.5)
    i = lax.broadcasted_iota(jnp.int32, (BLOCK_A, 2 * BLOCK_A), 0)
    j = lax.broadcasted_iota(jnp.int32, (BLOCK_A, 2 * BLOCK_A), 1)
    delta = BLOCK_A + i - j
    valid = (delta >= 0) & (delta <= W_SUB) & (j >= first_key)
    s = jnp.where(valid, s - slope_d * delta.astype(F32), -jnp.inf)
    m = lax.stop_gradient(jnp.max(s, axis=-1, keepdims=True))
    p = jnp.exp(s - m)
    den = jnp.sum(p, axis=-1, keepdims=True)
    o = jnp.dot(p.astype(BF16), v2, preferred_element_type=F32) / den
    lse = jnp.broadcast_to(m + jnp.log(den), (BLOCK_A, HEAD_DIM))
    return o, lse


def _slope_d(group, h, d):
    hg = (group * 4 + h + 1).astype(F32)
    return jnp.exp(jnp.full((1, 1), -8.0 * math.log(2.0) / N_HEADS_A, F32) * hg) * float(d)


ATT_CFG = ((1, 4, 4), (4, 2, 1), (16, 1, 1))
D_GROUP = 1536


def _att_columns():
    out = []
    for g, (_, _, hp) in enumerate(ATT_CFG):
        lanes = HEAD_DIM * hp
        for hb in range(D_ATTN_OUT // lanes):
            for which in range(3):
                out.append((N_HEADS_A * HEAD_DIM * which + D_ATTN_OUT * g + lanes * hb, lanes))
    return out
R_UNROLL = 4


def _attn_geometry(group, T):
    d, m, hp = ATT_CFG[group]
    span = BLOCK_A * d
    tile = m * span
    assert T % tile == 0 and 4 % hp == 0
    return d, m, hp, span, tile, T // tile, HEAD_DIM * hp


def _sub_rows(b, span, r, d):
    return pl.ds(b * span + r, BLOCK_A, stride=d) if d > 1 else pl.ds(b * span, BLOCK_A)


def _for_each_r(d, fn):
    if d <= R_UNROLL:
        for r in range(d):
            fn(r)
    else:
        def step(r, carry):
            fn(r)
            return carry

        lax.fori_loop(0, d, step, 0, unroll=R_UNROLL)


def _attn_fwd(name, proj, group, T):
    d, m, hp, span, tile, nt, lanes = _attn_geometry(group, T)
    qb, kb, vb = [(D_GROUP * group) // lanes + which for which in range(3)]

    def body(q_ref, kc_ref, kp_ref, vc_ref, vp_ref, o_ref, l_ref):
        hb, n = pl.program_id(0), pl.program_id(1)
        first_key = jnp.where(n > 0, 0, BLOCK_A)
        for hh in range(hp):
            ls = slice(HEAD_DIM * hh, HEAD_DIM * (hh + 1))
            sl = _slope_d(group, hb * hp + hh, d)
            for b in range(m):
                def one(r, b=b, ls=ls, sl=sl):
                    rq = _sub_rows(b, span, r, d)
                    if b == 0:
                        rp = _sub_rows(0, span, r, d)
                        kp, vp, fk = kp_ref[rp, ls], vp_ref[rp, ls], first_key
                    else:
                        rp = _sub_rows(b - 1, span, r, d)
                        kp, vp, fk = kc_ref[rp, ls], vc_ref[rp, ls], 0
                    o, lse = _attn_block(q_ref[rq, ls], kp, kc_ref[rq, ls], vp, vc_ref[rq, ls], sl, fk)
                    o_ref[rq, ls] = o
                    l_ref[rq, ls] = lse

                _for_each_r(d, one)

    def cur(cb):
        return pl.BlockSpec((tile, lanes), lambda hb, n: (n, cb + 3 * hb))

    def prev(cb):
        return pl.BlockSpec((span, lanes), lambda hb, n: (jnp.maximum(m * n - 1, 0), cb + 3 * hb))

    return pl.pallas_call(
        body, name=name, grid=(4 // hp, nt),
        in_specs=[cur(qb), cur(kb), prev(kb), cur(vb), prev(vb)],
        out_specs=[pl.BlockSpec((tile, lanes), lambda hb, n: (n, hb))] * 2,
        out_shape=[jax.ShapeDtypeStruct((T, D_ATTN_OUT), F32)] * 2,
        compiler_params=_cparams(2),
    )(proj, proj, proj, proj, proj)


def _attn_block_bwd(q, kp, kc, vp, vc, o, lse, do, dl, slope_d, first_key):
    scale = HEAD_DIM ** -0.5
    nt_dims = (((1,), (1,)), ((), ()))
    tn_dims = (((0,), (0,)), ((), ()))
    qb = q.astype(BF16)
    k2 = jnp.concatenate([kp, kc], axis=0).astype(BF16)
    v2 = jnp.concatenate([vp, vc], axis=0).astype(BF16)
    dob = do.astype(BF16)
    s = lax.dot_general(qb, k2, nt_dims, preferred_element_type=F32) * scale
    i = lax.broadcasted_iota(jnp.int32, (BLOCK_A, 2 * BLOCK_A), 0)
    j = lax.broadcasted_iota(jnp.int32, (BLOCK_A, 2 * BLOCK_A), 1)
    delta = BLOCK_A + i - j
    valid = (delta >= 0) & (delta <= W_SUB) & (j >= first_key)
    lse_col = jnp.max(lse, axis=-1, keepdims=True)
    p = jnp.exp(jnp.where(valid, s - slope_d * delta.astype(F32), -jnp.inf) - lse_col)
    dp = lax.dot_general(dob, v2, nt_dims, preferred_element_type=F32)
    row = jnp.sum(dl, axis=-1, keepdims=True) - jnp.sum(do * o, axis=-1, keepdims=True)
    ds = (p * (dp + row)).astype(BF16)
    dv2 = lax.dot_general(p.astype(BF16), dob, tn_dims, preferred_element_type=F32)
    dq = jnp.dot(ds, k2, preferred_element_type=F32) * scale
    dk2 = lax.dot_general(ds, qb, tn_dims, preferred_element_type=F32) * scale
    return dq, dk2[:BLOCK_A], dk2[BLOCK_A:], dv2[:BLOCK_A], dv2[BLOCK_A:]


def _attn_bwd(name, proj, o, lse, do, dl, group, T, dest):
    d, m, hp, span, tile, nt, lanes = _attn_geometry(group, T)
    qb, kb, vb = [(D_GROUP * group) // lanes + which for which in range(3)]
    n_in = 9 if dest is None else 10

    def body(*refs):
        q_ref, kc_ref, kp_ref, vc_ref, vp_ref, o_ref, l_ref, do_ref, dl_ref = refs[:9]
        out_ref, aq_ref, ak_ref, av_ref = refs[n_in:]
        hb, n = pl.program_id(0), pl.program_id(1)
        first_key = jnp.where(n > 0, 0, BLOCK_A)
        cur_slot = n % 2
        old_slot = 1 - cur_slot

        @pl.when(n == 0)
        def _():
            aq_ref[...] = jnp.zeros_like(aq_ref)
            ak_ref[...] = jnp.zeros_like(ak_ref)
            av_ref[...] = jnp.zeros_like(av_ref)

        @pl.when(n < nt)
        def _():
            for hh in range(hp):
                ls = slice(HEAD_DIM * hh, HEAD_DIM * (hh + 1))
                sl = _slope_d(group, hb * hp + hh, d)
                for b in range(m):
                    def one(r, b=b, ls=ls, sl=sl):
                        rq = _sub_rows(b, span, r, d)
                        if b == 0:
                            rp = _sub_rows(0, span, r, d)
                            kp, vp, fk = kp_ref[rp, ls], vp_ref[rp, ls], first_key
                        else:
                            rp = _sub_rows(b - 1, span, r, d)
                            kp, vp, fk = kc_ref[rp, ls], vc_ref[rp, ls], 0
                        dq, dkp, dkc, dvp, dvc = _attn_block_bwd(
                            q_ref[rq, ls], kp, kc_ref[rq, ls], vp, vc_ref[rq, ls], o_ref[rq, ls], l_ref[rq, ls],
                            do_ref[rq, ls], dl_ref[rq, ls], sl, fk)
                        aq_ref[cur_slot, rq, ls] = dq
                        ak_ref[cur_slot, rq, ls] = dkc
                        av_ref[cur_slot, rq, ls] = dvc
                        if b == 0:
                            last = _sub_rows(m - 1, span, r, d)
                            ak_ref[old_slot, last, ls] += dkp
                            av_ref[old_slot, last, ls] += dvp
                        else:
                            ak_ref[cur_slot, rp, ls] += dkp
                            av_ref[cur_slot, rp, ls] += dvp

                    _for_each_r(d, one)

        out_ref[:, 0:lanes] = aq_ref[old_slot].astype(BF16)
        out_ref[:, lanes:2 * lanes] = ak_ref[old_slot].astype(BF16)
        out_ref[:, 2 * lanes:3 * lanes] = av_ref[old_slot].astype(BF16)

    def cur(cb):
        return pl.BlockSpec((tile, lanes), lambda hb, n: (jnp.minimum(n, nt - 1), cb + 3 * hb))

    def prev(cb):
        return pl.BlockSpec((span, lanes), lambda hb, n: (jnp.maximum(m * jnp.minimum(n, nt - 1) - 1, 0), cb + 3 * hb))

    own = pl.BlockSpec((tile, lanes), lambda hb, n: (jnp.minimum(n, nt - 1), hb))
    late = pl.BlockSpec((tile, 3 * lanes), lambda hb, n: (jnp.maximum(n - 1, 0), (D_GROUP * group) // (3 * lanes) + hb))
    acc = pltpu.VMEM((2, tile, lanes), F32)
    return pl.pallas_call(
        body, name=name, grid=(4 // hp, nt + 1),
        in_specs=[cur(qb), cur(kb), prev(kb), cur(vb), prev(vb), own, own, own, own] + ([] if dest is None else [_ANY]),
        out_specs=late,
        out_shape=jax.ShapeDtypeStruct((T, D_PA), BF16),
        input_output_aliases={} if dest is None else {9: 0},
        scratch_shapes=[acc, acc, acc],
        compiler_params=_cparams(2),
    )(proj, proj, proj, proj, proj, o, lse, do, dl, *([] if dest is None else [dest]))


HALO = 8
HALO_IN = 16


def _shifted(ext):
    return [ext[HALO:, :]] + [pltpu.roll(ext, k, 0)[HALO:, :] for k in range(1, CONV_WIDTH)]


def _conv_taps(sh, w):
    c = sh[0] * w[CONV_WIDTH - 1]
    for k in range(1, CONV_WIDTH):
        c = c + sh[k] * w[CONV_WIDTH - 1 - k]
    return c


def _taps(w_ref):
    return [w_ref[j:j + 1, :] for j in range(CONV_WIDTH)]


def _dn_pre_fwd(name, proj, ba, conv_w, alog_v, dtb_v, tm, T):
    C = 3 * D_KEY_B

    def body(x_ref, ba_ref, w_ref, al_ref, dt_ref, q_ref, k_ref, v_ref, bg_ref, halo_ref):
        i = pl.program_id(0)

        @pl.when(i == 0)
        def _():
            halo_ref[...] = jnp.zeros_like(halo_ref)

        x = x_ref[...].astype(F32)
        ext = jnp.concatenate([halo_ref[...], x], axis=0)
        c = _conv_taps(_shifted(ext), _taps(w_ref))
        halo_ref[...] = x[tm - HALO:, :]
        qn, kn, v, bg = _f_dnpoint(c, ba_ref[...], al_ref[...], dt_ref[...])
        q_ref[...] = qn
        k_ref[...] = kn
        v_ref[...] = v
        bg_ref[...] = bg

    return pl.pallas_call(
        body, name=name, grid=(T // tm,),
        in_specs=[_row_spec(tm, C, 0), _row_spec(tm, 128, 0), _full_spec(conv_w.shape), _full_spec((1, 128)),
                  _full_spec((1, 128))],
        out_specs=[_row_spec(tm, D_KEY_B, 0)] * 3 + [_row_spec(tm, 128, 0)],
        out_shape=[jax.ShapeDtypeStruct((T, D_KEY_B), F32)] * 3 + [jax.ShapeDtypeStruct((T, 128), F32)],
        scratch_shapes=[pltpu.VMEM((HALO, C), F32)],
        compiler_params=_ARB,
    )(proj, ba, conv_w, alog_v, dtb_v)


def _dn_pre_bwd(name, proj, ba, conv_w, alog_v, dtb_v, dq, dk, dv, dbg, tm, T):
    C = 3 * D_KEY_B
    nt = T // tm
    hb = tm // HALO_IN

    def body(x_ref, xh_ref, ba_ref, w_ref, al_ref, dt_ref, dq_ref, dk_ref, dv_ref, dbg_ref,
             dx_ref, dba_ref, dw_ref, dal_ref, ddt_ref, nxt_ref):
        s = pl.program_id(0)
        i = nt - 1 - s

        @pl.when(s == 0)
        def _():
            nxt_ref[...] = jnp.zeros_like(nxt_ref)
            dw_ref[...] = jnp.zeros_like(dw_ref)
            dal_ref[...] = jnp.zeros_like(dal_ref)
            ddt_ref[...] = jnp.zeros_like(ddt_ref)

        w = _taps(w_ref)
        halo = jnp.where(i > 0, xh_ref[...].astype(F32)[HALO_IN - HALO:, :], 0.0)
        ext = jnp.concatenate([halo, x_ref[...].astype(F32)], axis=0)
        sh = _shifted(ext)
        c = _conv_taps(sh, w)
        _, vjp = jax.vjp(_f_dnpoint, c, ba_ref[...], al_ref[...], dt_ref[...])
        dc, dba, dal, ddt = vjp((dq_ref[...], dk_ref[...], dv_ref[...], dbg_ref[...]))
        dba_ref[...] = dba
        dal_ref[...] += dal
        ddt_ref[...] += ddt
        ext2 = jnp.concatenate([dc, nxt_ref[...]], axis=0)
        dx = dc * w[CONV_WIDTH - 1]
        for m in range(1, CONV_WIDTH):
            dx = dx + pltpu.roll(ext2, tm + HALO - m, 0)[:tm, :] * w[CONV_WIDTH - 1 - m]
        dx_ref[...] = dx.astype(dx_ref.dtype)
        nxt_ref[...] = dc[:HALO, :]
        for j in range(CONV_WIDTH):
            dw_ref[j:j + 1, :] += jnp.sum(sh[CONV_WIDTH - 1 - j] * dc, axis=0, keepdims=True)

    def rev(w):
        return pl.BlockSpec((tm, w), lambda s: (nt - 1 - s, 0))

    return pl.pallas_call(
        body, name=name, grid=(nt,),
        in_specs=[rev(C), pl.BlockSpec((HALO_IN, C), lambda s: (jnp.maximum((nt - 1 - s) * hb - 1, 0), 0)), rev(128),
                  _full_spec(conv_w.shape), _full_spec((1, 128)), _full_spec((1, 128)),
                  rev(D_KEY_B), rev(D_KEY_B), rev(D_KEY_B), rev(128)],
        out_specs=[rev(C), rev(128), _full_spec((8, C)), _full_spec((1, 128)), _full_spec((1, 128))],
        out_shape=[jax.ShapeDtypeStruct((T, C), BF16), jax.ShapeDtypeStruct((T, 128), F32),
                   jax.ShapeDtypeStruct((8, C), F32), jax.ShapeDtypeStruct((1, 128), F32),
                   jax.ShapeDtypeStruct((1, 128), F32)],
        scratch_shapes=[pltpu.VMEM((HALO, C), F32)],
        compiler_params=_ARB,
    )(proj, proj, ba, conv_w, alog_v, dtb_v, dq, dk, dv, dbg)


def _bdot(a, b, dims, exact=False):
    if exact:
        return lax.dot_general(a, b, dims, preferred_element_type=F32, precision=lax.Precision.HIGH)
    return lax.dot_general(a.astype(BF16), b.astype(BF16), dims, preferred_element_type=F32)


_B_NN = (((2,), (1,)), ((0,), (0,)))
_B_NT = (((2,), (2,)), ((0,), (0,)))
_B_TN = (((1,), (1,)), ((0,), (0,)))


def _unit_lower_inverse(a):
    C = a.shape[-1]
    r = lax.broadcasted_iota(jnp.int32, (C, C), 0)
    c = lax.broadcasted_iota(jnp.int32, (C, C), 1)
    pw = -a
    t = jnp.where((r == c)[None], 1.0, 0.0) + pw
    for _ in range(int(math.log2(C)) - 1):
        pw = _bdot(pw, pw, _B_NN, exact=True)
        t = t + _bdot(t, pw, _B_NN, exact=True)
    return t


@jax.custom_vjp
def _solve_given_inverse(t, a, rhs):
    return _bdot(t, rhs, _B_NN, exact=True)


def _solve_fwd(t, a, rhs):
    sol = _bdot(t, rhs, _B_NN, exact=True)
    return sol, (t, sol)


def _solve_bwd(res, dsol):
    t, sol = res
    drhs = _bdot(t, dsol, _B_TN, exact=True)
    return jnp.zeros_like(t), -_bdot(drhs, sol, _B_NT, exact=True), drhs


_solve_given_inverse.defvjp(_solve_fwd, _solve_bwd)


def _stack(xs):
    return jnp.concatenate([x[None] for x in xs], axis=0)


def _delta_chunk(S, q, k, v, bg, t_inv=None):
    H, C = N_HEADS_B, CHUNK
    r = lax.broadcasted_iota(jnp.int32, (C, C), 0)
    c = lax.broadcasted_iota(jnp.int32, (C, C), 1)
    incl, strict, eye = r >= c, r > c, r == c
    gc_all = jnp.dot(incl.astype(F32), bg, preferred_element_type=F32, precision=lax.Precision.HIGHEST)
    lane = lax.broadcasted_iota(jnp.int32, (C, 128), 1)

    def col(x, l):
        return jnp.sum(jnp.where(lane == l, x, 0.0), axis=1, keepdims=True)

    beta = _stack([col(bg, h) for h in range(H)])
    gc = _stack([col(gc_all, H + h) for h in range(H)])
    gc_row = jnp.sum(jnp.where(eye[None], jnp.broadcast_to(gc, (H, C, C)), 0.0), axis=1, keepdims=True)
    row = lax.broadcasted_iota(jnp.int32, (H, C, 1), 1)
    gc_last = jnp.sum(jnp.where(row == C - 1, gc, 0.0), axis=1, keepdims=True)
    decay = jnp.exp(jnp.where(incl[None], gc - gc_row, -jnp.inf))
    qh = _stack(_heads(q, H))
    kh = _stack(_heads(k, H))
    vh = _stack(_heads(v, H))
    kk = _bdot(kh, kh, _B_NT, exact=True)
    a = jnp.where(strict[None], beta * kk * decay, 0.0)
    egc = jnp.exp(gc)
    rhs = jnp.concatenate([beta * vh, (beta * egc) * kh], axis=-1)
    if t_inv is None:
        t_inv = _unit_lower_inverse(a)
        sol = _bdot(t_inv, rhs, _B_NN, exact=True)
    else:
        sol = _solve_given_inverse(t_inv, a, rhs)
    u_bar, w = sol[..., :HEAD_DIM], sol[..., HEAD_DIM:]
    qk = _bdot(qh, kh, _B_NT) * decay
    q_dec = qh * egc
    k_dec = kh * jnp.exp(gc_last - gc)
    u = u_bar - _bdot(w, S, _B_NN)
    o = _bdot(q_dec, S, _B_NN) + _bdot(qk, u, _B_NN)
    S_new = jnp.exp(gc_last) * S + _bdot(k_dec, u, _B_TN)
    o2 = jnp.concatenate([lax.index_in_dim(o, h, 0, keepdims=False) for h in range(H)], axis=1)
    return o2, S_new, t_inv


def _delta_fwd(name, q, k, v, bg, T):
    nc = T // CHUNK
    H = N_HEADS_B

    def body(q_ref, k_ref, v_ref, bg_ref, o_ref, s_out_ref, t_out_ref, s_ref):
        @pl.when(pl.program_id(0) == 0)
        def _():
            s_ref[...] = jnp.zeros_like(s_ref)

        S = s_ref[...]
        s_out_ref[0] = S
        o, S_new, t_inv = _delta_chunk(S, q_ref[...], k_ref[...], v_ref[...], bg_ref[...])
        o_ref[...] = o
        t_out_ref[0] = t_inv
        s_ref[...] = S_new

    return pl.pallas_call(
        body, name=name, grid=(nc,),
        in_specs=[_row_spec(CHUNK, D_KEY_B, 0)] * 3 + [_row_spec(CHUNK, 128, 0)],
        out_specs=[_row_spec(CHUNK, D_KEY_B, 0), pl.BlockSpec((1, H, HEAD_DIM, HEAD_DIM), lambda i: (i, 0, 0, 0)),
                   pl.BlockSpec((1, H, CHUNK, CHUNK), lambda i: (i, 0, 0, 0))],
        out_shape=[jax.ShapeDtypeStruct((T, D_KEY_B), F32), jax.ShapeDtypeStruct((nc, H, HEAD_DIM, HEAD_DIM), F32),
                   jax.ShapeDtypeStruct((nc, H, CHUNK, CHUNK), F32)],
        scratch_shapes=[pltpu.VMEM((H, HEAD_DIM, HEAD_DIM), F32)],
        compiler_params=_ARB,
    )(q, k, v, bg)


def _delta_bwd(name, q, k, v, bg, s_all, t_all, do, T):
    nc = T // CHUNK
    H = N_HEADS_B

    def body(q_ref, k_ref, v_ref, bg_ref, s_in_ref, t_ref, do_ref, dq_ref, dk_ref, dv_ref, dbg_ref, ds_ref):
        @pl.when(pl.program_id(0) == 0)
        def _():
            ds_ref[...] = jnp.zeros_like(ds_ref)

        t_inv = t_ref[0]

        def f(S, qq, kk, vv, bb):
            return _delta_chunk(S, qq, kk, vv, bb, t_inv)[:2]

        _, vjp = jax.vjp(f, s_in_ref[0], q_ref[...], k_ref[...], v_ref[...], bg_ref[...])
        dS, dq, dk, dv, dbg = vjp((do_ref[...], ds_ref[...]))
        dq_ref[...] = dq
        dk_ref[...] = dk
        dv_ref[...] = dv
        dbg_ref[...] = dbg
        ds_ref[...] = dS

    def rev(w):
        return pl.BlockSpec((CHUNK, w), lambda s: (nc - 1 - s, 0))

    def rev4(n):
        return pl.BlockSpec((1, H, n, n), lambda s: (nc - 1 - s, 0, 0, 0))

    return pl.pallas_call(
        body, name=name, grid=(nc,),
        in_specs=[rev(D_KEY_B)] * 3 + [rev(128), rev4(HEAD_DIM), rev4(CHUNK), rev(D_KEY_B)],
        out_specs=[rev(D_KEY_B)] * 3 + [rev(128)],
        out_shape=[jax.ShapeDtypeStruct((T, D_KEY_B), F32)] * 3 + [jax.ShapeDtypeStruct((T, 128), F32)],
        scratch_shapes=[pltpu.VMEM((H, HEAD_DIM, HEAD_DIM), F32)],
        compiler_params=_ARB,
    )(q, k, v, bg, s_all, t_all, do)


def _loss_fb(name, h2, w, tgt, tm, T):
    def tile_loss(h, wv, t):
        y = _f_rms(h, wv)[0]
        e = y - t
        return 0.5 * jnp.sum(jnp.mean(e * e, axis=-1))

    def body(h_ref, w_ref, t_ref, dh_ref, loss_ref, dw_ref):
        @pl.when(pl.program_id(0) == 0)
        def _():
            loss_ref[...] = jnp.zeros_like(loss_ref)
            dw_ref[...] = jnp.zeros_like(dw_ref)

        val, (dh, dw) = jax.value_and_grad(tile_loss, argnums=(0, 1))(h_ref[...], w_ref[...], t_ref[...])
        dh_ref[...] = dh
        dw_ref[...] += dw
        loss_ref[...] += jnp.reshape(val, (1, 1))

    return pl.pallas_call(
        body, name=name, grid=(T // tm,),
        in_specs=[_row_spec(tm, D_MODEL, 0), _full_spec((1, D_MODEL)), _row_spec(tm, D_MODEL, 0)],
        out_specs=[_row_spec(tm, D_MODEL, 0), _full_spec((1, 1)), _full_spec((1, D_MODEL))],
        out_shape=[jax.ShapeDtypeStruct((T, D_MODEL), F32), jax.ShapeDtypeStruct((1, 1), F32),
                   jax.ShapeDtypeStruct((1, D_MODEL), F32)],
        compiler_params=_ARB,
    )(h2, w, tgt)


def _adamw(name, w, g, m, v, tile):
    R = w.shape[0]

    def body(w_ref, g_ref, m_ref, v_ref, d_out, m_out, v_out):
        g = g_ref[...]
        mm = ADAM_B1 * m_ref[...] + (1.0 - ADAM_B1) * g
        vv = ADAM_B2 * v_ref[...] + (1.0 - ADAM_B2) * jnp.square(g)
        m_hat = mm / (1.0 - ADAM_B1 ** ADAM_STEP)
        v_hat = vv / (1.0 - ADAM_B2 ** ADAM_STEP)
        d_out[...] = -ADAM_LR * (m_hat / (jnp.sqrt(v_hat) + ADAM_EPS) + ADAM_WD * w_ref[...])
        m_out[...] = mm
        v_out[...] = vv

    assert R % tile == 0
    spec = pl.BlockSpec((tile, w.shape[1]), lambda i: (i, 0))
    return pl.pallas_call(
        body, name=name, grid=(R // tile,), in_specs=[spec] * 4, out_specs=[spec] * 3,
        out_shape=[jax.ShapeDtypeStruct(w.shape, F32)] * 3, compiler_params=_ARB,
    )(w, g, m, v)


def _sum4(name, r, tile):
    _, R, C = r.shape
    assert R % tile == 0

    def body(r_ref, o_ref):
        o_ref[...] = ((r_ref[0].astype(F32) + r_ref[1].astype(F32)) + r_ref[2].astype(F32)) + r_ref[3].astype(F32)

    return pl.pallas_call(
        body, name=name, grid=(R // tile,),
        in_specs=[pl.BlockSpec((4, tile, C), lambda i: (0, i, 0))],
        out_specs=pl.BlockSpec((tile, C), lambda i: (i, 0)),
        out_shape=jax.ShapeDtypeStruct((R, C), F32), compiler_params=_ARB,
    )(r)


def _add2(name, a, b, tile):
    R, C = a.shape
    assert R % tile == 0

    def body(a_ref, b_ref, o_ref):
        o_ref[...] = (a_ref[...].astype(F32) + b_ref[...].astype(F32)).astype(BF16)

    spec = pl.BlockSpec((tile, C), lambda i: (i, 0))
    return pl.pallas_call(
        body, name=name, grid=(R // tile,), in_specs=[spec, spec], out_specs=spec,
        out_shape=jax.ShapeDtypeStruct((R, C), BF16), compiler_params=_ARB,
    )(a, b)


_ANY = pl.BlockSpec(memory_space=pl.ANY)


def _place():
    return lax.axis_index("x"), lax.axis_index("y"), lax.axis_index("c")


def _chip_exchange(in_refs, out_refs, send_sems, recv_sems, local_sems, gather):
    n = len(in_refs)
    x, y, c = _place()
    me = 2 * x + y
    chips = [(1 - x, y), (x, 1 - y), (1 - x, 1 - y)]

    def src(a, slot):
        return in_refs[a] if gather else in_refs[a].at[slot]

    def copy(a, k, src_slot, dst_slot, px, py):
        return pltpu.make_async_remote_copy(src_ref=src(a, src_slot), dst_ref=out_refs[a].at[dst_slot],
                                            send_sem=send_sems.at[3 * a + k], recv_sem=recv_sems.at[3 * a + k],
                                            device_id=(px, py, c), device_id_type=MESH)

    def mine():
        return [pltpu.make_async_copy(src(a, me), out_refs[a].at[me], local_sems.at[a]) for a in range(n)]

    def sends():
        return [copy(a, k, 2 * px + py, me, px, py) for a in range(n) for k, (px, py) in enumerate(chips)]

    def start():
        for cp in mine() + sends():
            cp.start()

    def finish():
        for a in range(n):
            for k, (px, py) in enumerate(chips):
                copy(a, k, me, 2 * px + py, px, py).wait_recv()
        for cp in sends():
            cp.wait_send()
        for cp in mine():
            cp.wait()

    return start, finish


def _chip_exchange_shapes(arrays, gather):
    return [jax.ShapeDtypeStruct(((4,) + a.shape) if gather else a.shape, a.dtype) for a in arrays]


def _chip_exchange_sems(n):
    return [pltpu.SemaphoreType.DMA((3 * n,)), pltpu.SemaphoreType.DMA((3 * n,)), pltpu.SemaphoreType.DMA((n,))]


def _exchange_chips(name, arrays, gather):
    n = len(arrays)

    def body(*refs):
        start, finish = _chip_exchange(refs[:n], refs[n:2 * n], *refs[2 * n:], gather=gather)
        start()
        finish()

    return pl.pallas_call(
        body, name=name, in_specs=[_ANY] * n, out_specs=[_ANY] * n,
        out_shape=_chip_exchange_shapes(arrays, gather), scratch_shapes=_chip_exchange_sems(n),
    )(*arrays)


def _swap_sibling(name, arrays):
    n = len(arrays)

    def body(*refs):
        in_refs, out_refs, send_sems, recv_sems = refs[:n], refs[n:2 * n], refs[2 * n], refs[2 * n + 1]
        x, y, c = _place()
        cps = [pltpu.make_async_remote_copy(src_ref=in_refs[a], dst_ref=out_refs[a], send_sem=send_sems.at[a],
                                            recv_sem=recv_sems.at[a], device_id=(x, y, 1 - c), device_id_type=MESH)
               for a in range(n)]
        for cp in cps:
            cp.start()
        for cp in cps:
            cp.wait()

    return pl.pallas_call(
        body, name=name, in_specs=[_ANY] * n, out_specs=[_ANY] * n,
        out_shape=[jax.ShapeDtypeStruct(a.shape, a.dtype) for a in arrays],
        scratch_shapes=[pltpu.SemaphoreType.DMA((n,)), pltpu.SemaphoreType.DMA((n,))],
    )(*arrays)


def _allsum_small(name, pack):
    def body(p_ref, out_ref, slots, send_sems, recv_sems):
        x, y, c = _place()
        me = 4 * x + 2 * y + c
        slots[me] = p_ref[...]
        sends = []
        for k in range(1, 8):
            px = 1 - x if k & 4 else x
            py = 1 - y if k & 2 else y
            pc = 1 - c if k & 1 else c
            cp = pltpu.make_async_remote_copy(src_ref=p_ref, dst_ref=slots.at[me], send_sem=send_sems.at[k - 1],
                                              recv_sem=recv_sems.at[k - 1], device_id=(px, py, pc), device_id_type=MESH)
            cp.start()
            sends.append(cp)
        for k in range(1, 8):
            px = 1 - x if k & 4 else x
            py = 1 - y if k & 2 else y
            pc = 1 - c if k & 1 else c
            pltpu.make_async_remote_copy(src_ref=p_ref, dst_ref=slots.at[4 * px + 2 * py + pc],
                                         send_sem=send_sems.at[k - 1], recv_sem=recv_sems.at[k - 1],
                                         device_id=(px, py, pc), device_id_type=MESH).wait_recv()
        for cp in sends:
            cp.wait_send()
        acc = slots[0]
        for s in range(1, 8):
            acc = acc + slots[s]
        out_ref[...] = acc

    vm = pl.BlockSpec(memory_space=pltpu.VMEM)
    return pl.pallas_call(
        body, name=name, in_specs=[vm], out_specs=vm,
        out_shape=jax.ShapeDtypeStruct(pack.shape, pack.dtype),
        scratch_shapes=[pltpu.VMEM((8,) + pack.shape, pack.dtype), pltpu.SemaphoreType.DMA((7,)),
                        pltpu.SemaphoreType.DMA((7,))],
    )(pack)


def _pack_shards(parts, rows_total):
    flat = [p.reshape(-1, D_MODEL) for p in parts]
    used = sum(f.shape[0] for f in flat)
    return jnp.concatenate(flat + [jnp.zeros((rows_total - used, D_MODEL), flat[0].dtype)], axis=0)


def _unpack_shards(buf, shapes):
    out, r0 = [], 0
    for shp in shapes:
        n = math.prod(shp) // D_MODEL
        out.append(buf[r0:r0 + n].reshape(shp))
        r0 += n
    return out


def kernel(x, norm_mix, w_in, conv_w, a_log, dt_bias, dn_norm, w_proj_attn, w_proj_delta, w_out, norm_ffn, w_gate, w_up, w_down, norm_final, loss_target, m_norm_mix, m_w_in, m_conv_w, m_a_log, m_dt_bias, m_dn_norm, m_w_proj_attn, m_w_proj_delta, m_w_out, m_norm_ffn, m_w_gate, m_w_up, m_w_down, m_norm_final, v_norm_mix, v_w_in, v_conv_w, v_a_log, v_dt_bias, v_dn_norm, v_w_proj_attn, v_w_proj_delta, v_w_out, v_norm_ffn, v_w_gate, v_w_up, v_w_down, v_norm_final):
    T = x.shape[1]
    D = D_MODEL
    tm = 512
    tw = 256
    mt = 1024 if T % 1024 == 0 else 512
    x2 = x.reshape(T, D)
    tgt = loss_target.reshape(T, D)

    packed = (w_proj_attn, w_proj_delta, w_out, w_gate, w_up, w_down)
    conv_bits = lax.bitcast_convert_type(conv_w[0], BF16)
    w16 = _pack_shards([w[0].astype(BF16) for w in packed] + [conv_bits], PACK_R)
    win16 = w_in[0].astype(BF16)
    my_c = lax.axis_index("c")

    def both_halves(mine, other, axis):
        return jnp.concatenate([jnp.where(my_c == 0, mine, other), jnp.where(my_c == 0, other, mine)], axis=axis)

    pack_half = lax.dynamic_slice_in_dim(w16, my_c * HALF, HALF, axis=0)
    (win_mine,) = _exchange_chips("gather_w_in", [lax.dynamic_slice_in_dim(win16, my_c * W_IN_HALF, W_IN_HALF, axis=0)], True)
    (win_other,) = _swap_sibling("swap_w_in", [win_mine])
    win4 = both_halves(win_mine, win_other, 1)

    def win_cols(g0, g1):
        out = []
        for j in range(4):
            lo, hi = max(g0, W_IN_COLS * j), min(g1, W_IN_COLS * (j + 1))
            if lo < hi:
                out.append(win4[j][:, lo - W_IN_COLS * j:hi - W_IN_COLS * j])
        return out

    Wa = jnp.concatenate([p for c0, w in _att_columns() for p in win_cols(c0, c0 + w)], axis=1)
    Wb = jnp.concatenate(win_cols(4608, 8704) + win_cols(8720, D_IN), axis=1)
    Wba = jnp.pad(jnp.concatenate(win_cols(8704, 8720), axis=1), ((0, 0), (0, 112)))
    alog_v = jnp.pad(a_log, ((0, 0), (8, 112)))
    dtb_v = jnp.pad(dt_bias, ((0, 0), (8, 112)))

    (u,) = _rowwise("rms_mix", _f_rms, [(x2, 0, D)], [norm_mix], [(D, BF16)], tm, T)
    proj_a, pack_mine = _matmul("mm_in_a", u, Wa, "nn", T, D_PA, D, mt, 1536, D, F32, side=([pack_half], True))
    (pack_other,) = _swap_sibling("swap_pack", [pack_mine])
    wg = both_halves(pack_mine, pack_other, 1)

    def seg(r0, n):
        return wg[:, r0:r0 + n, :]

    r = 0
    Wpa = seg(r, 128).reshape(4, D_ATTN_OUT, 256).transpose(1, 0, 2).reshape(D_ATTN_OUT, D); r += 128
    Wpd = seg(r, 256).reshape(D, D); r += 256
    Wout = seg(r, 256).reshape(D, D); r += 256
    Wg = seg(r, 704).reshape(4, D, 704).transpose(1, 0, 2).reshape(D, D_FF); r += 704
    Wu = seg(r, 704).reshape(4, D, 704).transpose(1, 0, 2).reshape(D, D_FF); r += 704
    Wd = seg(r, 704).reshape(D_FF, D); r += 704
    convw = lax.bitcast_convert_type(seg(r, 6).reshape(4, CONV_WIDTH, 768, 2), F32)
    convw = convw.transpose(1, 0, 2).reshape(CONV_WIDTH, 3 * D_KEY_B)
    Wgu = jnp.concatenate([Wg, Wu], axis=1)
    proj_b = _matmul("mm_in_b", u, Wb, "nn", T, D_PB, D, mt, 1536, D, BF16)
    ba = _matmul("mm_ba", u, Wba, "nn", T, 128, D, mt, 128, D, F32)

    att = [_attn_fwd(f"attn_fwd{g}", proj_a, g, T) for g in range(3)]
    (ya_m,) = _rowwise("merge", _f_merge, [(att[g][0], 0, 512) for g in range(3)] + [(att[g][1], 0, 512) for g in range(3)],
                       [], [(512, BF16)], tm, T)
    qn, kn, vd, bg = _dn_pre_fwd("dn_pre_fwd", proj_b, ba, convw, alog_v, dtb_v, tw, T)
    o_d, s_all, t_all = _delta_fwd("delta_fwd", qn, kn, vd, bg, T)
    (od,) = _rowwise("post_dn", _f_postdn, [(o_d, 0, D), (proj_b, C_Z // D, D)], [dn_norm], [(D, BF16)], tm, T)
    y_a = _matmul("mm_pa", ya_m, Wpa, "nn", T, D, 512, mt, D, 512, F32)
    y_b = _matmul("mm_pd", od, Wpd, "nn", T, D, D, mt, D, D, F32)
    (mix,) = _rowwise("mix", _f_mix, [(y_a, 0, D), (y_b, 0, D), (proj_b, C_GATE // 2048, 2048)], [], [(D, BF16)], tm, T)
    h1 = _matmul("mm_out", mix, Wout, "nn", T, D, D, mt, D, D, F32, residual=x2)
    (hn,) = _rowwise("rms_ffn", _f_rms, [(h1, 0, D)], [norm_ffn], [(D, BF16)], tm, T)
    gu = _matmul("mm_gu", hn, Wgu, "nn", T, 2 * D_FF, D, mt, 1408, D, BF16)
    (act,) = _rowwise("act", _f_act, [(gu, 0, 2 * D_FF)], [], [(D_FF, BF16)], tw, T)
    h2 = _matmul("mm_down", act, Wd, "nn", T, D, D_FF, mt, D, 1408, F32, residual=h1)
    dh2, loss, g_norm_final = _loss_fb("loss", h2, norm_final.reshape(1, D), tgt, tm, T)

    dact = _matmul("mm_dact", dh2, Wd, "nt", T, D_FF, D, mt, 1408, D, BF16)
    gWd = _matmul("mm_gwd", act, dh2, "tn", D_FF, D, T, 1408, D, mt, BF16)
    (dgu,) = _rowwise_vjp("act_bwd", _f_act, [(gu, 0, 2 * D_FF)], [], [(dact, 0, D_FF)], [BF16], tw, T)
    dhn = _matmul("mm_dhn", dgu, Wgu, "nt", T, D, 2 * D_FF, mt, D, 1408, F32)
    gWgu = _matmul("mm_gwgu", hn, dgu, "tn", D, 2 * D_FF, T, D, 1408, mt, BF16)
    dh1, g_norm_ffn = _rowwise_vjp("rms_ffn_bwd", _f_rms, [(h1, 0, D)], [norm_ffn], [(dhn, 0, D)], [F32], tm, T,
                                   residual=(dh2, 0, D))
    dmix = _matmul("mm_dmix", dh1, Wout, "nt", T, D, D, mt, D, D, F32)
    gWout = _matmul("mm_gwout", mix, dh1, "tn", D, D, T, D, D, mt, BF16)
    dya, dyb, dgate = _rowwise_vjp("mix_bwd", _f_mix, [(y_a, 0, D), (y_b, 0, D), (proj_b, C_GATE // 2048, 2048)], [],
                                   [(dmix, 0, D)], [BF16, BF16, BF16], tm, T)
    dya_m = _matmul("mm_dyam", dya, Wpa, "nt", T, 512, D, mt, 512, D, F32)
    gWpa = _matmul("mm_gwpa", ya_m, dya, "tn", 512, D, T, 512, D, mt, BF16)
    dod = _matmul("mm_dod", dyb, Wpd, "nt", T, D, D, mt, D, D, F32)
    gWpd = _matmul("mm_gwpd", od, dyb, "tn", D, D, T, D, D, mt, BF16)
    do_d, dz, g_dn_norm = _rowwise_vjp("post_dn_bwd", _f_postdn, [(o_d, 0, D), (proj_b, C_Z // D, D)], [dn_norm],
                                       [(dod, 0, D)], [F32, BF16], tm, T)
    dqn, dkn, dvd, dbg = _delta_bwd("delta_bwd", qn, kn, vd, bg, s_all, t_all, do_d, T)
    dqkvd, dba, g_conv8, g_alog_v, g_dtb_v = _dn_pre_bwd("dn_pre_bwd", proj_b, ba, convw, alog_v, dtb_v,
                                                         dqn, dkn, dvd, dbg, tw, T)
    def pair_sum(tag, g, h):
        keep = lax.dynamic_slice_in_dim(g, my_c * h, h, axis=1)
        give = lax.dynamic_slice_in_dim(g, (1 - my_c) * h, h, axis=1)
        (got,) = _swap_sibling("swap_grads_" + tag, [give])
        return _add2("add_cores_" + tag, keep.reshape(-1, g.shape[2]), got.reshape(-1, g.shape[2]), HALF_TILE).reshape(keep.shape)

    def cols4(g, n):
        return g.reshape(g.shape[0], 4, n).transpose(1, 0, 2).reshape(4, -1, D)

    parts = [cols4(gWpa, 256), gWpd.reshape(4, 256, D), gWout.reshape(4, 256, D), cols4(gWgu[:, :D_FF], 704),
             cols4(gWgu[:, D_FF:], 704), gWd.reshape(4, 704, D), cols4(g_conv8[:CONV_WIDTH], 768)]
    used = sum(p.shape[1] for p in parts)
    g_pack = jnp.concatenate([p.astype(BF16) for p in parts] + [jnp.zeros((4, PACK_R - used, D), BF16)], axis=1)
    pair_pack = pair_sum("pack", g_pack, HALF)

    mg = _rowwise_vjp("merge_bwd", _f_merge,
                      [(att[g][0], 0, 512) for g in range(3)] + [(att[g][1], 0, 512) for g in range(3)], [],
                      [(dya_m, 0, 512)], [F32] * 6, tm, T)
    dproj_a = None
    for g in range(3):
        dproj_a = _attn_bwd(f"attn_bwd{g}", proj_a, att[g][0], att[g][1], mg[g], mg[3 + g], g, T, dproj_a)
    dproj_b = [dqkvd, dz, dgate]
    gWa, recv_pack = _matmul("mm_gwa", u, dproj_a, "tn", D, D_PA, T, D, 1536, mt, BF16, side=([pair_pack], False))
    gWb = _matmul_pieces("mm_gwb", "tn", dproj_b, u, D, D_PB, T, D, 1024, mt, BF16)
    gWba = _matmul("mm_gwba", u, dba, "tn", D, 128, T, D, 128, mt, BF16)

    pieces, off = [(gWb[:, :C_GATE], D_PA), (gWba[:, :16], 8704), (gWb[:, C_GATE:], 8720)], 0
    for c0, w in _att_columns():
        pieces.append((gWa[:, off:off + w], c0))
        off += w
    pieces.sort(key=lambda t: t[1])

    def win_shard(j):
        out = []
        for arr, g0 in pieces:
            lo, hi = max(g0, W_IN_COLS * j), min(g0 + arr.shape[1], W_IN_COLS * (j + 1))
            if lo < hi:
                out.append(arr[:, lo - g0:hi - g0])
        return jnp.concatenate(out, axis=1)

    g_win4 = jnp.concatenate([win_shard(j)[None] for j in range(4)], axis=0)
    pair_win = pair_sum("w_in", g_win4, W_IN_HALF)

    du0 = _matmul("mm_du_ba", dba, Wba, "nt", T, D, 128, mt, D, 128, F32)
    du1 = _matmul_pieces("mm_du_b", "nt", dproj_b, Wb, T, D, D_PB, mt, D, 1024, F32, residual=du0)
    du, recv_win = _matmul("mm_du_a", dproj_a, Wa, "nt", T, D, D_PA, mt, D, 1536, F32, residual=du1, side=([pair_win], False))
    dx, g_norm_mix = _rowwise_vjp("rms_mix_bwd", _f_rms, [(x2, 0, D)], [norm_mix], [(du, 0, D)], [F32], tm, T,
                                  residual=(dh1, 0, D))

    g_mine = [_sum4("sum_chips_pack", recv_pack, HALF_TILE), _sum4("sum_chips_w_in", recv_win, HALF_TILE)]
    g_other = _swap_sibling("swap_sums", g_mine)
    g_all = both_halves(g_mine[0], g_other[0], 0)
    g_win = both_halves(g_mine[1], g_other[1], 0)

    locals_ = (w_proj_attn, w_proj_delta, w_out, w_gate, w_up, w_down, conv_w, w_in)
    ms = (m_w_proj_attn, m_w_proj_delta, m_w_out, m_w_gate, m_w_up, m_w_down, m_conv_w, m_w_in)
    vs = (v_w_proj_attn, v_w_proj_delta, v_w_out, v_w_gate, v_w_up, v_w_down, v_conv_w, v_w_in)
    big_names = ("w_proj_attn", "w_proj_delta", "w_out", "w_gate", "w_up", "w_down", "conv_w", "w_in")
    adam_tiles = (512, 256, 256, 256, 256, 352, CONV_WIDTH, 128)
    g_shards = _unpack_shards(g_all, [w.shape for w in locals_[:-1]]) + [g_win.reshape(w_in.shape)]
    big = [g_shards, [], [], []]
    for nm, w, g, m_, v_, tl in zip(big_names, locals_, g_shards, ms, vs, adam_tiles):
        res = _adamw("adamw_" + nm, w[0], g[0], m_[0], v_[0], tl)
        for kind in range(3):
            big[kind + 1].append(res[kind].reshape(w.shape))

    def small_pack(nm, nf, nfin, dn, al, dt, extra):
        row3 = jnp.concatenate([dn.reshape(1, 128), al.reshape(1, 8), dt.reshape(1, 8), extra.reshape(1, 1),
                                jnp.zeros((1, D - 145), F32)], axis=1)
        return jnp.concatenate([nm.reshape(1, D), nf.reshape(1, D), nfin.reshape(1, D), row3, jnp.zeros((4, D), F32)], axis=0)

    zero1 = jnp.zeros((1, 1), F32)
    sp = small_pack(g_norm_mix, g_norm_ffn, g_norm_final, g_dn_norm, g_alog_v[:, 8:16], g_dtb_v[:, 8:16], loss)
    tot = _allsum_small("allsum_small", sp)
    sm = _adamw("adamw_small", small_pack(norm_mix, norm_ffn, norm_final, dn_norm, a_log, dt_bias, zero1), tot,
                small_pack(m_norm_mix, m_norm_ffn, m_norm_final, m_dn_norm, m_a_log, m_dt_bias, zero1),
                small_pack(v_norm_mix, v_norm_ffn, v_norm_final, v_dn_norm, v_a_log, v_dt_bias, zero1), 8)
    sm = [tot] + list(sm)

    def small_unpack(p):
        return {"norm_mix": p[0:1], "norm_ffn": p[1:2], "norm_final": p[2], "dn_norm": p[3:4, :128],
                "a_log": p[3:4, 128:136], "dt_bias": p[3:4, 136:144]}

    sm = [small_unpack(p) for p in sm]
    loss_out = tot[3, 144]

    order = ("norm_mix", "w_in", "conv_w", "a_log", "dt_bias", "dn_norm", "w_proj_attn", "w_proj_delta", "w_out",
             "norm_ffn", "w_gate", "w_up", "w_down", "norm_final")
    outs = [loss_out, dx.reshape(x.shape)]
    for kind in range(4):
        for n in order:
            outs.append(big[kind][big_names.index(n)] if n in big_names else sm[kind][n])
    return tuple(outs)
```

```python
import functools
import math

import jax
import jax.numpy as jnp
from jax import lax
from jax.experimental import pallas as pl
from jax.experimental.pallas import tpu as pltpu

F32 = jnp.float32
BF16 = jnp.bfloat16
MESH = pl.DeviceIdType.MESH

D_MODEL = 1024
N_HEADS_A = 12
HEAD_DIM = 128
BLOCK_A = 128
DILATIONS = (1, 4, 16)
W_SUB = 128
D_ATTN_OUT = 512
N_HEADS_B = 8
D_KEY_B = 1024
CONV_WIDTH = 4
CHUNK = 64
D_FF = 2816
EPS = 1e-6
D_IN = 10768
D_PA = 4608
D_PB = 6144
C_Z, C_GATE = 3072, 4096

ADAM_LR, ADAM_B1, ADAM_B2, ADAM_EPS, ADAM_WD, ADAM_STEP = 0.001, 0.9, 0.999, 1e-08, 0.01, 10

PACK_R = 2816
HALF = PACK_R // 2
HALF_TILE = 128
W_IN_COLS = 2692
W_IN_HALF = D_MODEL // 2

VMEM_LIMIT = 48 * 1024 * 1024


def _cparams(n):
    return pltpu.CompilerParams(dimension_semantics=("arbitrary",) * n, vmem_limit_bytes=VMEM_LIMIT)


_ARB = _cparams(1)


def _row_spec(tm, w, cb):
    return pl.BlockSpec((tm, w), lambda i: (i, cb))


def _full_spec(shape):
    nd = len(shape)
    return pl.BlockSpec(shape, lambda i: (0,) * nd)


def _rowwise(name, fn, ins, params, outs, tm, T):
    n_in, n_p = len(ins), len(params)

    def body(*refs):
        xs = [r[...].astype(F32) for r in refs[:n_in]] + [r[...] for r in refs[n_in:n_in + n_p]]
        res = fn(*xs)
        for o_ref, v in zip(refs[n_in + n_p:], res):
            o_ref[...] = v.astype(o_ref.dtype)

    return pl.pallas_call(
        body, name=name, grid=(T // tm,),
        in_specs=[_row_spec(tm, w, cb) for (_, cb, w) in ins] + [_full_spec(p.shape) for p in params],
        out_specs=[_row_spec(tm, w, 0) for (w, _) in outs],
        out_shape=[jax.ShapeDtypeStruct((T, w), dt) for (w, dt) in outs],
        compiler_params=_ARB,
    )(*[a for (a, _, _) in ins], *params)


def _rowwise_vjp(name, fn, ins, params, cts, gdtypes, tm, T, residual=None):
    n_in, n_p, n_ct = len(ins), len(params), len(cts)
    n_res = 0 if residual is None else 1

    def body(*refs):
        i = pl.program_id(0)
        k = 0
        xs = [r[...].astype(F32) for r in refs[k:k + n_in]]
        k += n_in
        ps = [r[...] for r in refs[k:k + n_p]]
        k += n_p
        gs = tuple(r[...].astype(F32) for r in refs[k:k + n_ct])
        k += n_ct
        res_refs = refs[k:k + n_res]
        k += n_res
        gi_refs = refs[k:k + n_in]
        gp_refs = refs[k + n_in:]
        _, vjp = jax.vjp(fn, *xs, *ps)
        g = vjp(gs)
        for j, (ref, v) in enumerate(zip(gi_refs, g[:n_in])):
            if j == 0 and n_res:
                v = v + res_refs[0][...].astype(F32)
            ref[...] = v.astype(ref.dtype)

        @pl.when(i == 0)
        def _():
            for ref in gp_refs:
                ref[...] = jnp.zeros_like(ref)

        for ref, v in zip(gp_refs, g[n_in:]):
            ref[...] += v

    res_in = [] if residual is None else [residual]
    return pl.pallas_call(
        body, name=name, grid=(T // tm,),
        in_specs=([_row_spec(tm, w, cb) for (_, cb, w) in ins] + [_full_spec(p.shape) for p in params]
                  + [_row_spec(tm, w, cb) for (_, cb, w) in cts] + [_row_spec(tm, w, cb) for (_, cb, w) in res_in]),
        out_specs=[_row_spec(tm, w, 0) for (_, _, w) in ins] + [_full_spec(p.shape) for p in params],
        out_shape=([jax.ShapeDtypeStruct((T, w), dt) for (_, _, w), dt in zip(ins, gdtypes)]
                   + [jax.ShapeDtypeStruct(p.shape, F32) for p in params]),
        compiler_params=_ARB,
    )(*[a for (a, _, _) in ins], *params, *[a for (a, _, _) in cts], *[a for (a, _, _) in res_in])


def _matmul(name, a, b, mode, M, N, K, tm, tn, tk, out_dtype, a_off=0, b_off=0, residual=None, side=None):
    nk = K // tk
    n_side = 0 if side is None else len(side[0])
    assert M % tm == 0 and N % tn == 0 and K % tk == 0
    if mode == "nn":
        assert a_off % tk == 0 and b_off % tn == 0
        a_spec = pl.BlockSpec((tm, tk), lambda i, j, k: (i, a_off // tk + k))
        b_spec = pl.BlockSpec((tk, tn), lambda i, j, k: (k, b_off // tn + j))
        dims = (((1,), (0,)), ((), ()))
    elif mode == "nt":
        assert a_off % tk == 0 and b_off == 0
        a_spec = pl.BlockSpec((tm, tk), lambda i, j, k: (i, a_off // tk + k))
        b_spec = pl.BlockSpec((tn, tk), lambda i, j, k: (j, k))
        dims = (((1,), (1,)), ((), ()))
    else:
        assert a_off % tm == 0 and b_off % tn == 0
        a_spec = pl.BlockSpec((tk, tm), lambda i, j, k: (k, a_off // tm + i))
        b_spec = pl.BlockSpec((tk, tn), lambda i, j, k: (k, b_off // tn + j))
        dims = (((0,), (0,)), ((), ()))
    has_res = residual is not None

    n_main = 3 if has_res else 2
    grid = (M // tm, N // tn, nk)

    def body(*refs):
        a_ref, b_ref = refs[0], refs[1]
        r_ref = refs[2] if has_res else None
        o_ref = refs[n_main + n_side]
        acc_ref = refs[n_main + 2 * n_side + 1]
        i, j, k = pl.program_id(0), pl.program_id(1), pl.program_id(2)
        if n_side:
            start, finish = _chip_exchange(refs[n_main:n_main + n_side], refs[n_main + n_side + 1:n_main + 2 * n_side + 1],
                                           *refs[n_main + 2 * n_side + 2:], gather=side[1])
            pl.when((i == 0) & (j == 0) & (k == 0))(start)

        @pl.when(k == 0)
        def _():
            acc_ref[...] = jnp.zeros_like(acc_ref)

        acc_ref[...] += lax.dot_general(a_ref[...].astype(BF16), b_ref[...].astype(BF16), dims,
                                        preferred_element_type=F32)

        @pl.when(k == nk - 1)
        def _():
            acc = acc_ref[...]
            if has_res:
                acc = acc + r_ref[...].astype(F32)
            o_ref[...] = acc.astype(o_ref.dtype)

        if n_side:
            pl.when((i == grid[0] - 1) & (j == grid[1] - 1) & (k == nk - 1))(finish)

    in_specs = [a_spec, b_spec]
    args = [a, b]
    if has_res:
        in_specs.append(pl.BlockSpec((tm, tn), lambda i, j, k: (i, j)))
        args.append(residual)
    out_specs = pl.BlockSpec((tm, tn), lambda i, j, k: (i, j))
    out_shape = jax.ShapeDtypeStruct((M, N), out_dtype)
    scratch = [pltpu.VMEM((tm, tn), F32)]
    semantics = ("parallel", "parallel", "arbitrary")
    if n_side:
        in_specs += [_ANY] * n_side
        args += list(side[0])
        out_specs = [out_specs] + [_ANY] * n_side
        out_shape = [out_shape] + _chip_exchange_shapes(*side)
        scratch += _chip_exchange_sems(n_side)
        semantics = ("arbitrary",) * 3
    return pl.pallas_call(
        body, name=name, grid=grid, in_specs=in_specs, out_specs=out_specs, out_shape=out_shape, scratch_shapes=scratch,
        compiler_params=pltpu.CompilerParams(dimension_semantics=semantics, vmem_limit_bytes=VMEM_LIMIT),
    )(*args)


def _matmul_pieces(name, mode, pieces, other, M, N, K, tm, tn, tk, out_dtype, residual=None):
    nk = K // tk
    blk = tn if mode == "tn" else tk
    counts = [p.shape[1] // blk for p in pieces]
    assert all(p.shape[1] % blk == 0 for p in pieces) and sum(counts) == (N if mode == "tn" else K) // blk
    starts = [sum(counts[:p]) for p in range(len(pieces))]
    n_p = len(pieces)
    has_res = residual is not None

    def piece_spec(s, n):
        if mode == "tn":
            return pl.BlockSpec((tk, tn), lambda i, j, k: (jnp.where((j >= s) & (j < s + n), k, 0), jnp.clip(j - s, 0, n - 1)))
        return pl.BlockSpec((tm, tk), lambda i, j, k: (i, jnp.clip(k - s, 0, n - 1)))

    if mode == "tn":
        other_spec = pl.BlockSpec((tk, tm), lambda i, j, k: (k, i))
    else:
        other_spec = pl.BlockSpec((tn, tk), lambda i, j, k: (j, k))

    def body(*refs):
        other_ref, piece_refs = refs[0], refs[1:1 + n_p]
        r_ref = refs[1 + n_p] if has_res else None
        o_ref, acc_ref = refs[-2], refs[-1]
        k = pl.program_id(2)
        sel = pl.program_id(1) if mode == "tn" else k

        @pl.when(k == 0)
        def _():
            acc_ref[...] = jnp.zeros_like(acc_ref)

        for p_ref, s, n in zip(piece_refs, starts, counts):
            @pl.when((sel >= s) & (sel < s + n))
            def _(p_ref=p_ref):
                if mode == "tn":
                    acc_ref[...] += lax.dot_general(other_ref[...].astype(BF16), p_ref[...].astype(BF16),
                                                    (((0,), (0,)), ((), ())), preferred_element_type=F32)
                else:
                    acc_ref[...] += lax.dot_general(p_ref[...].astype(BF16), other_ref[...].astype(BF16),
                                                    (((1,), (1,)), ((), ())), preferred_element_type=F32)

        @pl.when(k == nk - 1)
        def _():
            acc = acc_ref[...]
            if has_res:
                acc = acc + r_ref[...].astype(F32)
            o_ref[...] = acc.astype(o_ref.dtype)

    in_specs = [other_spec] + [piece_spec(s, n) for s, n in zip(starts, counts)]
    args = [other] + list(pieces)
    if has_res:
        in_specs.append(pl.BlockSpec((tm, tn), lambda i, j, k: (i, j)))
        args.append(residual)
    return pl.pallas_call(
        body, name=name, grid=(M // tm, N // tn, nk),
        in_specs=in_specs,
        out_specs=pl.BlockSpec((tm, tn), lambda i, j, k: (i, j)),
        out_shape=jax.ShapeDtypeStruct((M, N), out_dtype),
        scratch_shapes=[pltpu.VMEM((tm, tn), F32)],
        compiler_params=pltpu.CompilerParams(dimension_semantics=("parallel", "parallel", "arbitrary"),
                                             vmem_limit_bytes=VMEM_LIMIT),
    )(*args)


def _f_rms(x, w):
    return (x * lax.rsqrt(jnp.mean(x * x, axis=-1, keepdims=True) + EPS) * w,)


def _silu(x):
    return x * jax.nn.sigmoid(x)


def _softplus(x):
    return jnp.maximum(x, 0.0) + jnp.log1p(jnp.exp(-jnp.abs(x)))


def _heads(x, n):
    return [x[:, HEAD_DIM * h:HEAD_DIM * (h + 1)] for h in range(n)]


def _f_dnpoint(c, ba, alog_v, dtb_v):
    cs = _silu(c)
    q, k, v = cs[:, :D_KEY_B], cs[:, D_KEY_B:2 * D_KEY_B], cs[:, 2 * D_KEY_B:]
    qn = jnp.concatenate([t * lax.rsqrt(jnp.sum(t * t, axis=-1, keepdims=True) + EPS) * (HEAD_DIM ** -0.5)
                          for t in _heads(q, N_HEADS_B)], axis=1)
    kn = jnp.concatenate([t * lax.rsqrt(jnp.sum(t * t, axis=-1, keepdims=True) + EPS)
                          for t in _heads(k, N_HEADS_B)], axis=1)
    lane = lax.broadcasted_iota(jnp.int32, ba.shape, 1)
    beta = jax.nn.sigmoid(ba)
    g = -jnp.exp(alog_v) * _softplus(ba + dtb_v)
    bg = jnp.where(lane < N_HEADS_B, beta, jnp.where(lane < 2 * N_HEADS_B, g, 0.0))
    return qn, kn, v, bg


def _f_postdn(o, z, w):
    outs = []
    for oh, zh in zip(_heads(o, N_HEADS_B), _heads(z, N_HEADS_B)):
        outs.append(oh * lax.rsqrt(jnp.mean(oh * oh, axis=-1, keepdims=True) + EPS) * w * _silu(zh))
    return (jnp.concatenate(outs, axis=1),)


def _f_merge(o0, o1, o2, l0, l1, l2):
    m = lax.stop_gradient(jnp.maximum(jnp.maximum(l0, l1), l2))
    e0, e1, e2 = jnp.exp(l0 - m), jnp.exp(l1 - m), jnp.exp(l2 - m)
    return ((e0 * o0 + e1 * o1 + e2 * o2) / (e0 + e1 + e2),)


def _f_mix(ya, yb, gr):
    return (jax.nn.sigmoid(gr[:, :D_MODEL]) * ya + jax.nn.sigmoid(gr[:, D_MODEL:]) * yb,)


def _f_act(gu):
    return (_silu(gu[:, :D_FF]) * gu[:, D_FF:],)


def _attn_block(q, kp, kc, vp, vc, slope_d, first_key):
    k2 = jnp.concatenate([kp, kc], axis=0).astype(BF16)
    v2 = jnp.concatenate([vp, vc], axis=0).astype(BF16)
    s = lax.dot_general(q.astype(BF16), k2, (((1,), (1,)), ((), ())), preferred_element_type=F32)
    s = s * (HEAD_DIM ** -0.5)
    i = lax.broadcasted_iota(jnp.int32, (BLOCK_A, 2 * BLOCK_A), 0)
    j = lax.broadcasted_iota(jnp.int32, (BLOCK_A, 2 * BLOCK_A), 1)
    delta = BLOCK_A + i - j
    valid = (delta >= 0) & (delta <= W_SUB) & (j >= first_key)
    s = jnp.where(valid, s - slope_d * delta.astype(F32), -jnp.inf)
    m = lax.stop_gradient(jnp.max(s, axis=-1, keepdims=True))
    p = jnp.exp(s - m)
    den = jnp.sum(p, axis=-1, keepdims=True)
    o = jnp.dot(p.astype(BF16), v2, preferred_element_type=F32) / den
    lse = jnp.broadcast_to(m + jnp.log(den), (BLOCK_A, HEAD_DIM))
    return o, lse


def _slope_d(group, h, d):
    hg = (group * 4 + h + 1).astype(F32)
    return jnp.exp(jnp.full((1, 1), -8.0 * math.log(2.0) / N_HEADS_A, F32) * hg) * float(d)


ATT_CFG = ((1, 4, 4), (4, 2, 1), (16, 1, 1))
D_GROUP = 1536


def _att_columns():
    out = []
    for g, (_, _, hp) in enumerate(ATT_CFG):
        lanes = HEAD_DIM * hp
        for hb in range(D_ATTN_OUT // lanes):
            for which in range(3):
                out.append((N_HEADS_A * HEAD_DIM * which + D_ATTN_OUT * g + lanes * hb, lanes))
    return out
R_UNROLL = 4


def _attn_geometry(group, T):
    d, m, hp = ATT_CFG[group]
    span = BLOCK_A * d
    tile = m * span
    assert T % tile == 0 and 4 % hp == 0
    return d, m, hp, span, tile, T // tile, HEAD_DIM * hp


def _sub_rows(b, span, r, d):
    return pl.ds(b * span + r, BLOCK_A, stride=d) if d > 1 else pl.ds(b * span, BLOCK_A)


def _for_each_r(d, fn):
    if d <= R_UNROLL:
        for r in range(d):
            fn(r)
    else:
        def step(r, carry):
            fn(r)
            return carry

        lax.fori_loop(0, d, step, 0, unroll=R_UNROLL)


def _attn_fwd(name, proj, group, T):
    d, m, hp, span, tile, nt, lanes = _attn_geometry(group, T)
    qb, kb, vb = [(D_GROUP * group) // lanes + which for which in range(3)]

    def body(q_ref, kc_ref, kp_ref, vc_ref, vp_ref, o_ref, l_ref):
        hb, n = pl.program_id(0), pl.program_id(1)
        first_key = jnp.where(n > 0, 0, BLOCK_A)
        for hh in range(hp):
            ls = slice(HEAD_DIM * hh, HEAD_DIM * (hh + 1))
            sl = _slope_d(group, hb * hp + hh, d)
            for b in range(m):
                def one(r, b=b, ls=ls, sl=sl):
                    rq = _sub_rows(b, span, r, d)
                    if b == 0:
                        rp = _sub_rows(0, span, r, d)
                        kp, vp, fk = kp_ref[rp, ls], vp_ref[rp, ls], first_key
                    else:
                        rp = _sub_rows(b - 1, span, r, d)
                        kp, vp, fk = kc_ref[rp, ls], vc_ref[rp, ls], 0
                    o, lse = _attn_block(q_ref[rq, ls], kp, kc_ref[rq, ls], vp, vc_ref[rq, ls], sl, fk)
                    o_ref[rq, ls] = o
                    l_ref[rq, ls] = lse

                _for_each_r(d, one)

    def cur(cb):
        return pl.BlockSpec((tile, lanes), lambda hb, n: (n, cb + 3 * hb))

    def prev(cb):
        return pl.BlockSpec((span, lanes), lambda hb, n: (jnp.maximum(m * n - 1, 0), cb + 3 * hb))

    return pl.pallas_call(
        body, name=name, grid=(4 // hp, nt),
        in_specs=[cur(qb), cur(kb), prev(kb), cur(vb), prev(vb)],
        out_specs=[pl.BlockSpec((tile, lanes), lambda hb, n: (n, hb))] * 2,
        out_shape=[jax.ShapeDtypeStruct((T, D_ATTN_OUT), F32)] * 2,
        compiler_params=_cparams(2),
    )(proj, proj, proj, proj, proj)


def _attn_block_bwd(q, kp, kc, vp, vc, o, lse, do, dl, slope_d, first_key):
    scale = HEAD_DIM ** -0.5
    nt_dims = (((1,), (1,)), ((), ()))
    tn_dims = (((0,), (0,)), ((), ()))
    qb = q.astype(BF16)
    k2 = jnp.concatenate([kp, kc], axis=0).astype(BF16)
    v2 = jnp.concatenate([vp, vc], axis=0).astype(BF16)
    dob = do.astype(BF16)
    s = lax.dot_general(qb, k2, nt_dims, preferred_element_type=F32) * scale
    i = lax.broadcasted_iota(jnp.int32, (BLOCK_A, 2 * BLOCK_A), 0)
    j = lax.broadcasted_iota(jnp.int32, (BLOCK_A, 2 * BLOCK_A), 1)
    delta = BLOCK_A + i - j
    valid = (delta >= 0) & (delta <= W_SUB) & (j >= first_key)
    lse_col = jnp.max(lse, axis=-1, keepdims=True)
    p = jnp.exp(jnp.where(valid, s - slope_d * delta.astype(F32), -jnp.inf) - lse_col)
    dp = lax.dot_general(dob, v2, nt_dims, preferred_element_type=F32)
    row = jnp.sum(dl, axis=-1, keepdims=True) - jnp.sum(do * o, axis=-1, keepdims=True)
    ds = (p * (dp + row)).astype(BF16)
    dv2 = lax.dot_general(p.astype(BF16), dob, tn_dims, preferred_element_type=F32)
    dq = jnp.dot(ds, k2, preferred_element_type=F32) * scale
    dk2 = lax.dot_general(ds, qb, tn_dims, preferred_element_type=F32) * scale
    return dq, dk2[:BLOCK_A], dk2[BLOCK_A:], dv2[:BLOCK_A], dv2[BLOCK_A:]


def _attn_bwd(name, proj, o, lse, do, dl, group, T, dest):
    d, m, hp, span, tile, nt, lanes = _attn_geometry(group, T)
    qb, kb, vb = [(D_GROUP * group) // lanes + which for which in range(3)]
    n_in = 9 if dest is None else 10

    def body(*refs):
        q_ref, kc_ref, kp_ref, vc_ref, vp_ref, o_ref, l_ref, do_ref, dl_ref = refs[:9]
        out_ref, aq_ref, ak_ref, av_ref = refs[n_in:]
        hb, n = pl.program_id(0), pl.program_id(1)
        first_key = jnp.where(n > 0, 0, BLOCK_A)
        cur_slot = n % 2
        old_slot = 1 - cur_slot

        @pl.when(n == 0)
        def _():
            aq_ref[...] = jnp.zeros_like(aq_ref)
            ak_ref[...] = jnp.zeros_like(ak_ref)
            av_ref[...] = jnp.zeros_like(av_ref)

        @pl.when(n < nt)
        def _():
            for hh in range(hp):
                ls = slice(HEAD_DIM * hh, HEAD_DIM * (hh + 1))
                sl = _slope_d(group, hb * hp + hh, d)
                for b in range(m):
                    def one(r, b=b, ls=ls, sl=sl):
                        rq = _sub_rows(b, span, r, d)
                        if b == 0:
                            rp = _sub_rows(0, span, r, d)
                            kp, vp, fk = kp_ref[rp, ls], vp_ref[rp, ls], first_key
                        else:
                            rp = _sub_rows(b - 1, span, r, d)
                            kp, vp, fk = kc_ref[rp, ls], vc_ref[rp, ls], 0
                        dq, dkp, dkc, dvp, dvc = _attn_block_bwd(
                            q_ref[rq, ls], kp, kc_ref[rq, ls], vp, vc_ref[rq, ls], o_ref[rq, ls], l_ref[rq, ls],
                            do_ref[rq, ls], dl_ref[rq, ls], sl, fk)
                        aq_ref[cur_slot, rq, ls] = dq
                        ak_ref[cur_slot, rq, ls] = dkc
                        av_ref[cur_slot, rq, ls] = dvc
                        if b == 0:
                            last = _sub_rows(m - 1, span, r, d)
                            ak_ref[old_slot, last, ls] += dkp
                            av_ref[old_slot, last, ls] += dvp
                        else:
                            ak_ref[cur_slot, rp, ls] += dkp
                            av_ref[cur_slot, rp, ls] += dvp

                    _for_each_r(d, one)

        out_ref[:, 0:lanes] = aq_ref[old_slot].astype(BF16)
        out_ref[:, lanes:2 * lanes] = ak_ref[old_slot].astype(BF16)
        out_ref[:, 2 * lanes:3 * lanes] = av_ref[old_slot].astype(BF16)

    def cur(cb):
        return pl.BlockSpec((tile, lanes), lambda hb, n: (jnp.minimum(n, nt - 1), cb + 3 * hb))

    def prev(cb):
        return pl.BlockSpec((span, lanes), lambda hb, n: (jnp.maximum(m * jnp.minimum(n, nt - 1) - 1, 0), cb + 3 * hb))

    own = pl.BlockSpec((tile, lanes), lambda hb, n: (jnp.minimum(n, nt - 1), hb))
    late = pl.BlockSpec((tile, 3 * lanes), lambda hb, n: (jnp.maximum(n - 1, 0), (D_GROUP * group) // (3 * lanes) + hb))
    acc = pltpu.VMEM((2, tile, lanes), F32)
    return pl.pallas_call(
        body, name=name, grid=(4 // hp, nt + 1),
        in_specs=[cur(qb), cur(kb), prev(kb), cur(vb), prev(vb), own, own, own, own] + ([] if dest is None else [_ANY]),
        out_specs=late,
        out_shape=jax.ShapeDtypeStruct((T, D_PA), BF16),
        input_output_aliases={} if dest is None else {9: 0},
        scratch_shapes=[acc, acc, acc],
        compiler_params=_cparams(2),
    )(proj, proj, proj, proj, proj, o, lse, do, dl, *([] if dest is None else [dest]))


HALO = 8
HALO_IN = 16


def _shifted(ext):
    return [ext[HALO:, :]] + [pltpu.roll(ext, k, 0)[HALO:, :] for k in range(1, CONV_WIDTH)]


def _conv_taps(sh, w):
    c = sh[0] * w[CONV_WIDTH - 1]
    for k in range(1, CONV_WIDTH):
        c = c + sh[k] * w[CONV_WIDTH - 1 - k]
    return c


def _taps(w_ref):
    return [w_ref[j:j + 1, :] for j in range(CONV_WIDTH)]


def _dn_pre_fwd(name, proj, ba, conv_w, alog_v, dtb_v, tm, T):
    C = 3 * D_KEY_B

    def body(x_ref, ba_ref, w_ref, al_ref, dt_ref, q_ref, k_ref, v_ref, bg_ref, halo_ref):
        i = pl.program_id(0)

        @pl.when(i == 0)
        def _():
            halo_ref[...] = jnp.zeros_like(halo_ref)

        x = x_ref[...].astype(F32)
        ext = jnp.concatenate([halo_ref[...], x], axis=0)
        c = _conv_taps(_shifted(ext), _taps(w_ref))
        halo_ref[...] = x[tm - HALO:, :]
        qn, kn, v, bg = _f_dnpoint(c, ba_ref[...], al_ref[...], dt_ref[...])
        q_ref[...] = qn
        k_ref[...] = kn
        v_ref[...] = v
        bg_ref[...] = bg

    return pl.pallas_call(
        body, name=name, grid=(T // tm,),
        in_specs=[_row_spec(tm, C, 0), _row_spec(tm, 128, 0), _full_spec(conv_w.shape), _full_spec((1, 128)),
                  _full_spec((1, 128))],
        out_specs=[_row_spec(tm, D_KEY_B, 0)] * 3 + [_row_spec(tm, 128, 0)],
        out_shape=[jax.ShapeDtypeStruct((T, D_KEY_B), F32)] * 3 + [jax.ShapeDtypeStruct((T, 128), F32)],
        scratch_shapes=[pltpu.VMEM((HALO, C), F32)],
        compiler_params=_ARB,
    )(proj, ba, conv_w, alog_v, dtb_v)


def _dn_pre_bwd(name, proj, ba, conv_w, alog_v, dtb_v, dq, dk, dv, dbg, tm, T):
    C = 3 * D_KEY_B
    nt = T // tm
    hb = tm // HALO_IN

    def body(x_ref, xh_ref, ba_ref, w_ref, al_ref, dt_ref, dq_ref, dk_ref, dv_ref, dbg_ref,
             dx_ref, dba_ref, dw_ref, dal_ref, ddt_ref, nxt_ref):
        s = pl.program_id(0)
        i = nt - 1 - s

        @pl.when(s == 0)
        def _():
            nxt_ref[...] = jnp.zeros_like(nxt_ref)
            dw_ref[...] = jnp.zeros_like(dw_ref)
            dal_ref[...] = jnp.zeros_like(dal_ref)
            ddt_ref[...] = jnp.zeros_like(ddt_ref)

        w = _taps(w_ref)
        halo = jnp.where(i > 0, xh_ref[...].astype(F32)[HALO_IN - HALO:, :], 0.0)
        ext = jnp.concatenate([halo, x_ref[...].astype(F32)], axis=0)
        sh = _shifted(ext)
        c = _conv_taps(sh, w)
        _, vjp = jax.vjp(_f_dnpoint, c, ba_ref[...], al_ref[...], dt_ref[...])
        dc, dba, dal, ddt = vjp((dq_ref[...], dk_ref[...], dv_ref[...], dbg_ref[...]))
        dba_ref[...] = dba
        dal_ref[...] += dal
        ddt_ref[...] += ddt
        ext2 = jnp.concatenate([dc, nxt_ref[...]], axis=0)
        dx = dc * w[CONV_WIDTH - 1]
        for m in range(1, CONV_WIDTH):
            dx = dx + pltpu.roll(ext2, tm + HALO - m, 0)[:tm, :] * w[CONV_WIDTH - 1 - m]
        dx_ref[...] = dx.astype(dx_ref.dtype)
        nxt_ref[...] = dc[:HALO, :]
        for j in range(CONV_WIDTH):
            dw_ref[j:j + 1, :] += jnp.sum(sh[CONV_WIDTH - 1 - j] * dc, axis=0, keepdims=True)

    def rev(w):
        return pl.BlockSpec((tm, w), lambda s: (nt - 1 - s, 0))

    return pl.pallas_call(
        body, name=name, grid=(nt,),
        in_specs=[rev(C), pl.BlockSpec((HALO_IN, C), lambda s: (jnp.maximum((nt - 1 - s) * hb - 1, 0), 0)), rev(128),
                  _full_spec(conv_w.shape), _full_spec((1, 128)), _full_spec((1, 128)),
                  rev(D_KEY_B), rev(D_KEY_B), rev(D_KEY_B), rev(128)],
        out_specs=[rev(C), rev(128), _full_spec((8, C)), _full_spec((1, 128)), _full_spec((1, 128))],
        out_shape=[jax.ShapeDtypeStruct((T, C), BF16), jax.ShapeDtypeStruct((T, 128), F32),
                   jax.ShapeDtypeStruct((8, C), F32), jax.ShapeDtypeStruct((1, 128), F32),
                   jax.ShapeDtypeStruct((1, 128), F32)],
        scratch_shapes=[pltpu.VMEM((HALO, C), F32)],
        compiler_params=_ARB,
    )(proj, proj, ba, conv_w, alog_v, dtb_v, dq, dk, dv, dbg)


def _bdot(a, b, dims, exact=False):
    if exact:
        return lax.dot_general(a, b, dims, preferred_element_type=F32, precision=lax.Precision.HIGH)
    return lax.dot_general(a.astype(BF16), b.astype(BF16), dims, preferred_element_type=F32)


_B_NN = (((2,), (1,)), ((0,), (0,)))
_B_NT = (((2,), (2,)), ((0,), (0,)))
_B_TN = (((1,), (1,)), ((0,), (0,)))


def _unit_lower_inverse(a):
    C = a.shape[-1]
    r = lax.broadcasted_iota(jnp.int32, (C, C), 0)
    c = lax.broadcasted_iota(jnp.int32, (C, C), 1)
    pw = -a
    t = jnp.where((r == c)[None], 1.0, 0.0) + pw
    for _ in range(int(math.log2(C)) - 1):
        pw = _bdot(pw, pw, _B_NN, exact=True)
        t = t + _bdot(t, pw, _B_NN, exact=True)
    return t


@jax.custom_vjp
def _solve_given_inverse(t, a, rhs):
    return _bdot(t, rhs, _B_NN, exact=True)


def _solve_fwd(t, a, rhs):
    sol = _bdot(t, rhs, _B_NN, exact=True)
    return sol, (t, sol)


def _solve_bwd(res, dsol):
    t, sol = res
    drhs = _bdot(t, dsol, _B_TN, exact=True)
    return jnp.zeros_like(t), -_bdot(drhs, sol, _B_NT, exact=True), drhs


_solve_given_inverse.defvjp(_solve_fwd, _solve_bwd)


def _stack(xs):
    return jnp.concatenate([x[None] for x in xs], axis=0)


def _delta_chunk(S, q, k, v, bg, t_inv=None):
    H, C = N_HEADS_B, CHUNK
    r = lax.broadcasted_iota(jnp.int32, (C, C), 0)
    c = lax.broadcasted_iota(jnp.int32, (C, C), 1)
    incl, strict, eye = r >= c, r > c, r == c
    gc_all = jnp.dot(incl.astype(F32), bg, preferred_element_type=F32, precision=lax.Precision.HIGHEST)
    lane = lax.broadcasted_iota(jnp.int32, (C, 128), 1)

    def col(x, l):
        return jnp.sum(jnp.where(lane == l, x, 0.0), axis=1, keepdims=True)

    beta = _stack([col(bg, h) for h in range(H)])
    gc = _stack([col(gc_all, H + h) for h in range(H)])
    gc_row = jnp.sum(jnp.where(eye[None], jnp.broadcast_to(gc, (H, C, C)), 0.0), axis=1, keepdims=True)
    row = lax.broadcasted_iota(jnp.int32, (H, C, 1), 1)
    gc_last = jnp.sum(jnp.where(row == C - 1, gc, 0.0), axis=1, keepdims=True)
    decay = jnp.exp(jnp.where(incl[None], gc - gc_row, -jnp.inf))
    qh = _stack(_heads(q, H))
    kh = _stack(_heads(k, H))
    vh = _stack(_heads(v, H))
    kk = _bdot(kh, kh, _B_NT, exact=True)
    a = jnp.where(strict[None], beta * kk * decay, 0.0)
    egc = jnp.exp(gc)
    rhs = jnp.concatenate([beta * vh, (beta * egc) * kh], axis=-1)
    if t_inv is None:
        t_inv = _unit_lower_inverse(a)
        sol = _bdot(t_inv, rhs, _B_NN, exact=True)
    else:
        sol = _solve_given_inverse(t_inv, a, rhs)
    u_bar, w = sol[..., :HEAD_DIM], sol[..., HEAD_DIM:]
    qk = _bdot(qh, kh, _B_NT) * decay
    q_dec = qh * egc
    k_dec = kh * jnp.exp(gc_last - gc)
    u = u_bar - _bdot(w, S, _B_NN)
    o = _bdot(q_dec, S, _B_NN) + _bdot(qk, u, _B_NN)
    S_new = jnp.exp(gc_last) * S + _bdot(k_dec, u, _B_TN)
    o2 = jnp.concatenate([lax.index_in_dim(o, h, 0, keepdims=False) for h in range(H)], axis=1)
    return o2, S_new, t_inv


def _delta_fwd(name, q, k, v, bg, T):
    nc = T // CHUNK
    H = N_HEADS_B

    def body(q_ref, k_ref, v_ref, bg_ref, o_ref, s_out_ref, t_out_ref, s_ref):
        @pl.when(pl.program_id(0) == 0)
        def _():
            s_ref[...] = jnp.zeros_like(s_ref)

        S = s_ref[...]
        s_out_ref[0] = S
        o, S_new, t_inv = _delta_chunk(S, q_ref[...], k_ref[...], v_ref[...], bg_ref[...])
        o_ref[...] = o
        t_out_ref[0] = t_inv
        s_ref[...] = S_new

    return pl.pallas_call(
        body, name=name, grid=(nc,),
        in_specs=[_row_spec(CHUNK, D_KEY_B, 0)] * 3 + [_row_spec(CHUNK, 128, 0)],
        out_specs=[_row_spec(CHUNK, D_KEY_B, 0), pl.BlockSpec((1, H, HEAD_DIM, HEAD_DIM), lambda i: (i, 0, 0, 0)),
                   pl.BlockSpec((1, H, CHUNK, CHUNK), lambda i: (i, 0, 0, 0))],
        out_shape=[jax.ShapeDtypeStruct((T, D_KEY_B), F32), jax.ShapeDtypeStruct((nc, H, HEAD_DIM, HEAD_DIM), F32),
                   jax.ShapeDtypeStruct((nc, H, CHUNK, CHUNK), F32)],
        scratch_shapes=[pltpu.VMEM((H, HEAD_DIM, HEAD_DIM), F32)],
        compiler_params=_ARB,
    )(q, k, v, bg)


def _delta_bwd(name, q, k, v, bg, s_all, t_all, do, T):
    nc = T // CHUNK
    H = N_HEADS_B

    def body(q_ref, k_ref, v_ref, bg_ref, s_in_ref, t_ref, do_ref, dq_ref, dk_ref, dv_ref, dbg_ref, ds_ref):
        @pl.when(pl.program_id(0) == 0)
        def _():
            ds_ref[...] = jnp.zeros_like(ds_ref)

        t_inv = t_ref[0]

        def f(S, qq, kk, vv, bb):
            return _delta_chunk(S, qq, kk, vv, bb, t_inv)[:2]

        _, vjp = jax.vjp(f, s_in_ref[0], q_ref[...], k_ref[...], v_ref[...], bg_ref[...])
        dS, dq, dk, dv, dbg = vjp((do_ref[...], ds_ref[...]))
        dq_ref[...] = dq
        dk_ref[...] = dk
        dv_ref[...] = dv
        dbg_ref[...] = dbg
        ds_ref[...] = dS

    def rev(w):
        return pl.BlockSpec((CHUNK, w), lambda s: (nc - 1 - s, 0))

    def rev4(n):
        return pl.BlockSpec((1, H, n, n), lambda s: (nc - 1 - s, 0, 0, 0))

    return pl.pallas_call(
        body, name=name, grid=(nc,),
        in_specs=[rev(D_KEY_B)] * 3 + [rev(128), rev4(HEAD_DIM), rev4(CHUNK), rev(D_KEY_B)],
        out_specs=[rev(D_KEY_B)] * 3 + [rev(128)],
        out_shape=[jax.ShapeDtypeStruct((T, D_KEY_B), F32)] * 3 + [jax.ShapeDtypeStruct((T, 128), F32)],
        scratch_shapes=[pltpu.VMEM((H, HEAD_DIM, HEAD_DIM), F32)],
        compiler_params=_ARB,
    )(q, k, v, bg, s_all, t_all, do)


def _loss_fb(name, h2, w, tgt, tm, T):
    def tile_loss(h, wv, t):
        y = _f_rms(h, wv)[0]
        e = y - t
        return 0.5 * jnp.sum(jnp.mean(e * e, axis=-1))

    def body(h_ref, w_ref, t_ref, dh_ref, loss_ref, dw_ref):
        @pl.when(pl.program_id(0) == 0)
        def _():
            loss_ref[...] = jnp.zeros_like(loss_ref)
            dw_ref[...] = jnp.zeros_like(dw_ref)

        val, (dh, dw) = jax.value_and_grad(tile_loss, argnums=(0, 1))(h_ref[...], w_ref[...], t_ref[...])
        dh_ref[...] = dh
        dw_ref[...] += dw
        loss_ref[...] += jnp.reshape(val, (1, 1))

    return pl.pallas_call(
        body, name=name, grid=(T // tm,),
        in_specs=[_row_spec(tm, D_MODEL, 0), _full_spec((1, D_MODEL)), _row_spec(tm, D_MODEL, 0)],
        out_specs=[_row_spec(tm, D_MODEL, 0), _full_spec((1, 1)), _full_spec((1, D_MODEL))],
        out_shape=[jax.ShapeDtypeStruct((T, D_MODEL), F32), jax.ShapeDtypeStruct((1, 1), F32),
                   jax.ShapeDtypeStruct((1, D_MODEL), F32)],
        compiler_params=_ARB,
    )(h2, w, tgt)


def _adamw(name, w, g, m, v, tile):
    R = w.shape[0]

    def body(w_ref, g_ref, m_ref, v_ref, d_out, m_out, v_out):
        g = g_ref[...]
        mm = ADAM_B1 * m_ref[...] + (1.0 - ADAM_B1) * g
        vv = ADAM_B2 * v_ref[...] + (1.0 - ADAM_B2) * jnp.square(g)
        m_hat = mm / (1.0 - ADAM_B1 ** ADAM_STEP)
        v_hat = vv / (1.0 - ADAM_B2 ** ADAM_STEP)
        d_out[...] = -ADAM_LR * (m_hat / (jnp.sqrt(v_hat) + ADAM_EPS) + ADAM_WD * w_ref[...])
        m_out[...] = mm
        v_out[...] = vv

    assert R % tile == 0
    spec = pl.BlockSpec((tile, w.shape[1]), lambda i: (i, 0))
    return pl.pallas_call(
        body, name=name, grid=(R // tile,), in_specs=[spec] * 4, out_specs=[spec] * 3,
        out_shape=[jax.ShapeDtypeStruct(w.shape, F32)] * 3, compiler_params=_ARB,
    )(w, g, m, v)


def _sum4(name, r, tile):
    _, R, C = r.shape
    assert R % tile == 0

    def body(r_ref, o_ref):
        o_ref[...] = ((r_ref[0].astype(F32) + r_ref[1].astype(F32)) + r_ref[2].astype(F32)) + r_ref[3].astype(F32)

    return pl.pallas_call(
        body, name=name, grid=(R // tile,),
        in_specs=[pl.BlockSpec((4, tile, C), lambda i: (0, i, 0))],
        out_specs=pl.BlockSpec((tile, C), lambda i: (i, 0)),
        out_shape=jax.ShapeDtypeStruct((R, C), F32), compiler_params=_ARB,
    )(r)


def _add2(name, a, b, tile):
    R, C = a.shape
    assert R % tile == 0

    def body(a_ref, b_ref, o_ref):
        o_ref[...] = (a_ref[...].astype(F32) + b_ref[...].astype(F32)).astype(BF16)

    spec = pl.BlockSpec((tile, C), lambda i: (i, 0))
    return pl.pallas_call(
        body, name=name, grid=(R // tile,), in_specs=[spec, spec], out_specs=spec,
        out_shape=jax.ShapeDtypeStruct((R, C), BF16), compiler_params=_ARB,
    )(a, b)


_ANY = pl.BlockSpec(memory_space=pl.ANY)


def _place():
    return lax.axis_index("x"), lax.axis_index("y"), lax.axis_index("c")


def _chip_exchange(in_refs, out_refs, send_sems, recv_sems, local_sems, gather):
    n = len(in_refs)
    x, y, c = _place()
    me = 2 * x + y
    chips = [(1 - x, y), (x, 1 - y), (1 - x, 1 - y)]

    def src(a, slot):
        return in_refs[a] if gather else in_refs[a].at[slot]

    def copy(a, k, src_slot, dst_slot, px, py):
        return pltpu.make_async_remote_copy(src_ref=src(a, src_slot), dst_ref=out_refs[a].at[dst_slot],
                                            send_sem=send_sems.at[3 * a + k], recv_sem=recv_sems.at[3 * a + k],
                                            device_id=(px, py, c), device_id_type=MESH)

    def mine():
        return [pltpu.make_async_copy(src(a, me), out_refs[a].at[me], local_sems.at[a]) for a in range(n)]

    def sends():
        return [copy(a, k, 2 * px + py, me, px, py) for a in range(n) for k, (px, py) in enumerate(chips)]

    def start():
        for cp in mine() + sends():
            cp.start()

    def finish():
        for a in range(n):
            for k, (px, py) in enumerate(chips):
                copy(a, k, me, 2 * px + py, px, py).wait_recv()
        for cp in sends():
            cp.wait_send()
        for cp in mine():
            cp.wait()

    return start, finish


def _chip_exchange_shapes(arrays, gather):
    return [jax.ShapeDtypeStruct(((4,) + a.shape) if gather else a.shape, a.dtype) for a in arrays]


def _chip_exchange_sems(n):
    return [pltpu.SemaphoreType.DMA((3 * n,)), pltpu.SemaphoreType.DMA((3 * n,)), pltpu.SemaphoreType.DMA((n,))]


def _exchange_chips(name, arrays, gather):
    n = len(arrays)

    def body(*refs):
        start, finish = _chip_exchange(refs[:n], refs[n:2 * n], *refs[2 * n:], gather=gather)
        start()
        finish()

    return pl.pallas_call(
        body, name=name, in_specs=[_ANY] * n, out_specs=[_ANY] * n,
        out_shape=_chip_exchange_shapes(arrays, gather), scratch_shapes=_chip_exchange_sems(n),
    )(*arrays)


def _swap_sibling(name, arrays):
    n = len(arrays)

    def body(*refs):
        in_refs, out_refs, send_sems, recv_sems = refs[:n], refs[n:2 * n], refs[2 * n], refs[2 * n + 1]
        x, y, c = _place()
        cps = [pltpu.make_async_remote_copy(src_ref=in_refs[a], dst_ref=out_refs[a], send_sem=send_sems.at[a],
                                            recv_sem=recv_sems.at[a], device_id=(x, y, 1 - c), device_id_type=MESH)
               for a in range(n)]
        for cp in cps:
            cp.start()
        for cp in cps:
            cp.wait()

    return pl.pallas_call(
        body, name=name, in_specs=[_ANY] * n, out_specs=[_ANY] * n,
        out_shape=[jax.ShapeDtypeStruct(a.shape, a.dtype) for a in arrays],
        scratch_shapes=[pltpu.SemaphoreType.DMA((n,)), pltpu.SemaphoreType.DMA((n,))],
    )(*arrays)


def _allsum_small(name, pack):
    def body(p_ref, out_ref, slots, send_sems, recv_sems):
        x, y, c = _place()
        me = 4 * x + 2 * y + c
        slots[me] = p_ref[...]
        sends = []
        for k in range(1, 8):
            px = 1 - x if k & 4 else x
            py = 1 - y if k & 2 else y
            pc = 1 - c if k & 1 else c
            cp = pltpu.make_async_remote_copy(src_ref=p_ref, dst_ref=slots.at[me], send_sem=send_sems.at[k - 1],
                                              recv_sem=recv_sems.at[k - 1], device_id=(px, py, pc), device_id_type=MESH)
            cp.start()
            sends.append(cp)
        for k in range(1, 8):
            px = 1 - x if k & 4 else x
            py = 1 - y if k & 2 else y
            pc = 1 - c if k & 1 else c
            pltpu.make_async_remote_copy(src_ref=p_ref, dst_ref=slots.at[4 * px + 2 * py + pc],
                                         send_sem=send_sems.at[k - 1], recv_sem=recv_sems.at[k - 1],
                                         device_id=(px, py, pc), device_id_type=MESH).wait_recv()
        for cp in sends:
            cp.wait_send()
        acc = slots[0]
        for s in range(1, 8):
            acc = acc + slots[s]
        out_ref[...] = acc

    vm = pl.BlockSpec(memory_space=pltpu.VMEM)
    return pl.pallas_call(
        body, name=name, in_specs=[vm], out_specs=vm,
        out_shape=jax.ShapeDtypeStruct(pack.shape, pack.dtype),
        scratch_shapes=[pltpu.VMEM((8,) + pack.shape, pack.dtype), pltpu.SemaphoreType.DMA((7,)),
                        pltpu.SemaphoreType.DMA((7,))],
    )(pack)


def _pack_shards(parts, rows_total):
    flat = [p.reshape(-1, D_MODEL) for p in parts]
    used = sum(f.shape[0] for f in flat)
    return jnp.concatenate(flat + [jnp.zeros((rows_total - used, D_MODEL), flat[0].dtype)], axis=0)


def _unpack_shards(buf, shapes):
    out, r0 = [], 0
    for shp in shapes:
        n = math.prod(shp) // D_MODEL
        out.append(buf[r0:r0 + n].reshape(shp))
        r0 += n
    return out


def kernel(x, norm_mix, w_in, conv_w, a_log, dt_bias, dn_norm, w_proj_attn, w_proj_delta, w_out, norm_ffn, w_gate, w_up, w_down, norm_final, loss_target, m_norm_mix, m_w_in, m_conv_w, m_a_log, m_dt_bias, m_dn_norm, m_w_proj_attn, m_w_proj_delta, m_w_out, m_norm_ffn, m_w_gate, m_w_up, m_w_down, m_norm_final, v_norm_mix, v_w_in, v_conv_w, v_a_log, v_dt_bias, v_dn_norm, v_w_proj_attn, v_w_proj_delta, v_w_out, v_norm_ffn, v_w_gate, v_w_up, v_w_down, v_norm_final):
    T = x.shape[1]
    D = D_MODEL
    tm = 512
    tw = 256
    mt = 1024 if T % 1024 == 0 else 512
    x2 = x.reshape(T, D)
    tgt = loss_target.reshape(T, D)

    packed = (w_proj_attn, w_proj_delta, w_out, w_gate, w_up, w_down)
    conv_bits = lax.bitcast_convert_type(conv_w[0], BF16)
    w16 = _pack_shards([w[0].astype(BF16) for w in packed] + [conv_bits], PACK_R)
    win16 = w_in[0].astype(BF16)
    my_c = lax.axis_index("c")

    def both_halves(mine, other, axis):
        return jnp.concatenate([jnp.where(my_c == 0, mine, other), jnp.where(my_c == 0, other, mine)], axis=axis)

    pack_half = lax.dynamic_slice_in_dim(w16, my_c * HALF, HALF, axis=0)
    (win_mine,) = _exchange_chips("gather_w_in", [lax.dynamic_slice_in_dim(win16, my_c * W_IN_HALF, W_IN_HALF, axis=0)], True)
    (win_other,) = _swap_sibling("swap_w_in", [win_mine])
    win4 = both_halves(win_mine, win_other, 1)

    def win_cols(g0, g1):
        out = []
        for j in range(4):
            lo, hi = max(g0, W_IN_COLS * j), min(g1, W_IN_COLS * (j + 1))
            if lo < hi:
                out.append(win4[j][:, lo - W_IN_COLS * j:hi - W_IN_COLS * j])
        return out

    Wa = jnp.concatenate([p for c0, w in _att_columns() for p in win_cols(c0, c0 + w)], axis=1)
    Wb = jnp.concatenate(win_cols(4608, 8704) + win_cols(8720, D_IN), axis=1)
    Wba = jnp.pad(jnp.concatenate(win_cols(8704, 8720), axis=1), ((0, 0), (0, 112)))
    alog_v = jnp.pad(a_log, ((0, 0), (8, 112)))
    dtb_v = jnp.pad(dt_bias, ((0, 0), (8, 112)))

    (u,) = _rowwise("rms_mix", _f_rms, [(x2, 0, D)], [norm_mix], [(D, BF16)], tm, T)
    proj_a, pack_mine = _matmul("mm_in_a", u, Wa, "nn", T, D_PA, D, mt, 2304, D, F32, side=([pack_half], True))
    (pack_other,) = _swap_sibling("swap_pack", [pack_mine])
    wg = both_halves(pack_mine, pack_other, 1)

    def seg(r0, n):
        return wg[:, r0:r0 + n, :]

    r = 0
    Wpa = seg(r, 128).reshape(4, D_ATTN_OUT, 256).transpose(1, 0, 2).reshape(D_ATTN_OUT, D); r += 128
    Wpd = seg(r, 256).reshape(D, D); r += 256
    Wout = seg(r, 256).reshape(D, D); r += 256
    Wg = seg(r, 704).reshape(4, D, 704).transpose(1, 0, 2).reshape(D, D_FF); r += 704
    Wu = seg(r, 704).reshape(4, D, 704).transpose(1, 0, 2).reshape(D, D_FF); r += 704
    Wd = seg(r, 704).reshape(D_FF, D); r += 704
    convw = lax.bitcast_convert_type(seg(r, 6).reshape(4, CONV_WIDTH, 768, 2), F32)
    convw = convw.transpose(1, 0, 2).reshape(CONV_WIDTH, 3 * D_KEY_B)
    Wgu = jnp.concatenate([Wg, Wu], axis=1)
    proj_b = _matmul("mm_in_b", u, Wb, "nn", T, D_PB, D, mt, 3072, D, BF16)
    ba = _matmul("mm_ba", u, Wba, "nn", T, 128, D, mt, 128, D, F32)

    att = [_attn_fwd(f"attn_fwd{g}", proj_a, g, T) for g in range(3)]
    (ya_m,) = _rowwise("merge", _f_merge, [(att[g][0], 0, 512) for g in range(3)] + [(att[g][1], 0, 512) for g in range(3)],
                       [], [(512, BF16)], tm, T)
    qn, kn, vd, bg = _dn_pre_fwd("dn_pre_fwd", proj_b, ba, convw, alog_v, dtb_v, tw, T)
    o_d, s_all, t_all = _delta_fwd("delta_fwd", qn, kn, vd, bg, T)
    (od,) = _rowwise("post_dn", _f_postdn, [(o_d, 0, D), (proj_b, C_Z // D, D)], [dn_norm], [(D, BF16)], tm, T)
    y_a = _matmul("mm_pa", ya_m, Wpa, "nn", T, D, 512, mt, D, 512, F32)
    y_b = _matmul("mm_pd", od, Wpd, "nn", T, D, D, mt, D, D, F32)
    (mix,) = _rowwise("mix", _f_mix, [(y_a, 0, D), (y_b, 0, D), (proj_b, C_GATE // 2048, 2048)], [], [(D, BF16)], tm, T)
    h1 = _matmul("mm_out", mix, Wout, "nn", T, D, D, mt, D, D, F32, residual=x2)
    (hn,) = _rowwise("rms_ffn", _f_rms, [(h1, 0, D)], [norm_ffn], [(D, BF16)], tm, T)
    gu = _matmul("mm_gu", hn, Wgu, "nn", T, 2 * D_FF, D, mt, 2816, D, BF16)
    (act,) = _rowwise("act", _f_act, [(gu, 0, 2 * D_FF)], [], [(D_FF, BF16)], tw, T)
    h2 = _matmul("mm_down", act, Wd, "nn", T, D, D_FF, mt, D, 1408, F32, residual=h1)
    dh2, loss, g_norm_final = _loss_fb("loss", h2, norm_final.reshape(1, D), tgt, tm, T)

    dact = _matmul("mm_dact", dh2, Wd, "nt", T, D_FF, D, mt, 1408, D, BF16)
    gWd = _matmul("mm_gwd", act, dh2, "tn", D_FF, D, T, 1408, D, mt, BF16)
    (dgu,) = _rowwise_vjp("act_bwd", _f_act, [(gu, 0, 2 * D_FF)], [], [(dact, 0, D_FF)], [BF16], tw, T)
    dhn = _matmul("mm_dhn", dgu, Wgu, "nt", T, D, 2 * D_FF, mt, D, 2816, F32)
    gWgu = _matmul("mm_gwgu", hn, dgu, "tn", D, 2 * D_FF, T, D, 1408, 2 * mt, BF16)
    dh1, g_norm_ffn = _rowwise_vjp("rms_ffn_bwd", _f_rms, [(h1, 0, D)], [norm_ffn], [(dhn, 0, D)], [F32], tm, T,
                                   residual=(dh2, 0, D))
    dmix = _matmul("mm_dmix", dh1, Wout, "nt", T, D, D, mt, D, D, F32)
    gWout = _matmul("mm_gwout", mix, dh1, "tn", D, D, T, D, D, mt, BF16)
    dya, dyb, dgate = _rowwise_vjp("mix_bwd", _f_mix, [(y_a, 0, D), (y_b, 0, D), (proj_b, C_GATE // 2048, 2048)], [],
                                   [(dmix, 0, D)], [BF16, BF16, BF16], tm, T)
    dya_m = _matmul("mm_dyam", dya, Wpa, "nt", T, 512, D, mt, 512, D, F32)
    gWpa = _matmul("mm_gwpa", ya_m, dya, "tn", 512, D, T, 512, D, mt, BF16)
    dod = _matmul("mm_dod", dyb, Wpd, "nt", T, D, D, mt, D, D, F32)
    gWpd = _matmul("mm_gwpd", od, dyb, "tn", D, D, T, D, D, mt, BF16)
    do_d, dz, g_dn_norm = _rowwise_vjp("post_dn_bwd", _f_postdn, [(o_d, 0, D), (proj_b, C_Z // D, D)], [dn_norm],
                                       [(dod, 0, D)], [F32, BF16], tm, T)
    dqn, dkn, dvd, dbg = _delta_bwd("delta_bwd", qn, kn, vd, bg, s_all, t_all, do_d, T)
    dqkvd, dba, g_conv8, g_alog_v, g_dtb_v = _dn_pre_bwd("dn_pre_bwd", proj_b, ba, convw, alog_v, dtb_v,
                                                         dqn, dkn, dvd, dbg, tw, T)
    def pair_sum(tag, g, h):
        keep = lax.dynamic_slice_in_dim(g, my_c * h, h, axis=1)
        give = lax.dynamic_slice_in_dim(g, (1 - my_c) * h, h, axis=1)
        (got,) = _swap_sibling("swap_grads_" + tag, [give])
        return _add2("add_cores_" + tag, keep.reshape(-1, g.shape[2]), got.reshape(-1, g.shape[2]), HALF_TILE).reshape(keep.shape)

    def cols4(g, n):
        return g.reshape(g.shape[0], 4, n).transpose(1, 0, 2).reshape(4, -1, D)

    parts = [cols4(gWpa, 256), gWpd.reshape(4, 256, D), gWout.reshape(4, 256, D), cols4(gWgu[:, :D_FF], 704),
             cols4(gWgu[:, D_FF:], 704), gWd.reshape(4, 704, D), cols4(g_conv8[:CONV_WIDTH], 768)]
    used = sum(p.shape[1] for p in parts)
    g_pack = jnp.concatenate([p.astype(BF16) for p in parts] + [jnp.zeros((4, PACK_R - used, D), BF16)], axis=1)
    pair_pack = pair_sum("pack", g_pack, HALF)

    mg = _rowwise_vjp("merge_bwd", _f_merge,
                      [(att[g][0], 0, 512) for g in range(3)] + [(att[g][1], 0, 512) for g in range(3)], [],
                      [(dya_m, 0, 512)], [F32] * 6, tm, T)
    dproj_a = None
    for g in range(3):
        dproj_a = _attn_bwd(f"attn_bwd{g}", proj_a, att[g][0], att[g][1], mg[g], mg[3 + g], g, T, dproj_a)
    dproj_b = [dqkvd, dz, dgate]
    gWa, recv_pack = _matmul("mm_gwa", u, dproj_a, "tn", D, D_PA, T, D, 1536, 2 * mt, BF16, side=([pair_pack], False))
    gWb = _matmul_pieces("mm_gwb", "tn", dproj_b, u, D, D_PB, T, D, 1024, mt, BF16)
    gWba = _matmul("mm_gwba", u, dba, "tn", D, 128, T, D, 128, mt, BF16)

    pieces, off = [(gWb[:, :C_GATE], D_PA), (gWba[:, :16], 8704), (gWb[:, C_GATE:], 8720)], 0
    for c0, w in _att_columns():
        pieces.append((gWa[:, off:off + w], c0))
        off += w
    pieces.sort(key=lambda t: t[1])

    def win_shard(j):
        out = []
        for arr, g0 in pieces:
            lo, hi = max(g0, W_IN_COLS * j), min(g0 + arr.shape[1], W_IN_COLS * (j + 1))
            if lo < hi:
                out.append(arr[:, lo - g0:hi - g0])
        return jnp.concatenate(out, axis=1)

    g_win4 = jnp.concatenate([win_shard(j)[None] for j in range(4)], axis=0)
    pair_win = pair_sum("w_in", g_win4, W_IN_HALF)

    du0 = _matmul("mm_du_ba", dba, Wba, "nt", T, D, 128, mt, D, 128, F32)
    du1 = _matmul_pieces("mm_du_b", "nt", dproj_b, Wb, T, D, D_PB, mt, D, 1024, F32, residual=du0)
    du, recv_win = _matmul("mm_du_a", dproj_a, Wa, "nt", T, D, D_PA, mt, D, 2304, F32, residual=du1, side=([pair_win], False))
    dx, g_norm_mix = _rowwise_vjp("rms_mix_bwd", _f_rms, [(x2, 0, D)], [norm_mix], [(du, 0, D)], [F32], tm, T,
                                  residual=(dh1, 0, D))

    g_mine = [_sum4("sum_chips_pack", recv_pack, HALF_TILE), _sum4("sum_chips_w_in", recv_win, HALF_TILE)]
    g_other = _swap_sibling("swap_sums", g_mine)
    g_all = both_halves(g_mine[0], g_other[0], 0)
    g_win = both_halves(g_mine[1], g_other[1], 0)

    locals_ = (w_proj_attn, w_proj_delta, w_out, w_gate, w_up, w_down, conv_w, w_in)
    ms = (m_w_proj_attn, m_w_proj_delta, m_w_out, m_w_gate, m_w_up, m_w_down, m_conv_w, m_w_in)
    vs = (v_w_proj_attn, v_w_proj_delta, v_w_out, v_w_gate, v_w_up, v_w_down, v_conv_w, v_w_in)
    big_names = ("w_proj_attn", "w_proj_delta", "w_out", "w_gate", "w_up", "w_down", "conv_w", "w_in")
    adam_tiles = (512, 256, 256, 256, 256, 352, CONV_WIDTH, 128)
    g_shards = _unpack_shards(g_all, [w.shape for w in locals_[:-1]]) + [g_win.reshape(w_in.shape)]
    big = [g_shards, [], [], []]
    for nm, w, g, m_, v_, tl in zip(big_names, locals_, g_shards, ms, vs, adam_tiles):
        res = _adamw("adamw_" + nm, w[0], g[0], m_[0], v_[0], tl)
        for kind in range(3):
            big[kind + 1].append(res[kind].reshape(w.shape))

    def small_pack(nm, nf, nfin, dn, al, dt, extra):
        row3 = jnp.concatenate([dn.reshape(1, 128), al.reshape(1, 8), dt.reshape(1, 8), extra.reshape(1, 1),
                                jnp.zeros((1, D - 145), F32)], axis=1)
        return jnp.concatenate([nm.reshape(1, D), nf.reshape(1, D), nfin.reshape(1, D), row3, jnp.zeros((4, D), F32)], axis=0)

    zero1 = jnp.zeros((1, 1), F32)
    sp = small_pack(g_norm_mix, g_norm_ffn, g_norm_final, g_dn_norm, g_alog_v[:, 8:16], g_dtb_v[:, 8:16], loss)
    tot = _allsum_small("allsum_small", sp)
    sm = _adamw("adamw_small", small_pack(norm_mix, norm_ffn, norm_final, dn_norm, a_log, dt_bias, zero1), tot,
                small_pack(m_norm_mix, m_norm_ffn, m_norm_final, m_dn_norm, m_a_log, m_dt_bias, zero1),
                small_pack(v_norm_mix, v_norm_ffn, v_norm_final, v_dn_norm, v_a_log, v_dt_bias, zero1), 8)
    sm = [tot] + list(sm)

    def small_unpack(p):
        return {"norm_mix": p[0:1], "norm_ffn": p[1:2], "norm_final": p[2], "dn_norm": p[3:4, :128],
                "a_log": p[3:4, 128:136], "dt_bias": p[3:4, 136:144]}

    sm = [small_unpack(p) for p in sm]
    loss_out = tot[3, 144]

    order = ("norm_mix", "w_in", "conv_w", "a_log", "dt_bias", "dn_norm", "w_proj_attn", "w_proj_delta", "w_out",
             "norm_ffn", "w_gate", "w_up", "w_down", "norm_final")
    outs = [loss_out, dx.reshape(x.shape)]
    for kind in range(4):
        for n in order:
            outs.append(big[kind][big_names.index(n)] if n in big_names else sm[kind][n])
    return tuple(outs)
```

```python
import functools
import math

import jax
import jax.numpy as jnp
from jax import lax
from jax.experimental import pallas as pl
from jax.experimental.pallas import tpu as pltpu

F32 = jnp.float32
BF16 = jnp.bfloat16
MESH = pl.DeviceIdType.MESH

D_MODEL = 1024
N_HEADS_A = 12
HEAD_DIM = 128
BLOCK_A = 128
DILATIONS = (1, 4, 16)
W_SUB = 128
D_ATTN_OUT = 512
N_HEADS_B = 8
D_KEY_B = 1024
CONV_WIDTH = 4
CHUNK = 64
CHUNKS_FWD, CHUNKS_BWD = 2, 1
D_FF = 2816
EPS = 1e-6
D_IN = 10768
D_PA = 4608
D_PB = 6144
C_Z, C_GATE = 3072, 4096

ADAM_LR, ADAM_B1, ADAM_B2, ADAM_EPS, ADAM_WD, ADAM_STEP = 0.001, 0.9, 0.999, 1e-08, 0.01, 10

PACK_R = 2816
HALF = PACK_R // 2
HALF_TILE = 128
W_IN_COLS = 2692
W_IN_HALF = D_MODEL // 2

VMEM_LIMIT = 48 * 1024 * 1024


def _cparams(n):
    return pltpu.CompilerParams(dimension_semantics=("arbitrary",) * n, vmem_limit_bytes=VMEM_LIMIT)


_ARB = _cparams(1)


def _row_spec(tm, w, cb):
    return pl.BlockSpec((tm, w), lambda i: (i, cb))


def _full_spec(shape):
    nd = len(shape)
    return pl.BlockSpec(shape, lambda i: (0,) * nd)


def _rowwise(name, fn, ins, params, outs, tm, T):
    n_in, n_p = len(ins), len(params)

    def body(*refs):
        xs = [r[...].astype(F32) for r in refs[:n_in]] + [r[...] for r in refs[n_in:n_in + n_p]]
        res = fn(*xs)
        for o_ref, v in zip(refs[n_in + n_p:], res):
            o_ref[...] = v.astype(o_ref.dtype)

    return pl.pallas_call(
        body, name=name, grid=(T // tm,),
        in_specs=[_row_spec(tm, w, cb) for (_, cb, w) in ins] + [_full_spec(p.shape) for p in params],
        out_specs=[_row_spec(tm, w, 0) for (w, _) in outs],
        out_shape=[jax.ShapeDtypeStruct((T, w), dt) for (w, dt) in outs],
        compiler_params=_ARB,
    )(*[a for (a, _, _) in ins], *params)


def _rowwise_vjp(name, fn, ins, params, cts, gdtypes, tm, T, residual=None):
    n_in, n_p, n_ct = len(ins), len(params), len(cts)
    n_res = 0 if residual is None else 1

    def body(*refs):
        i = pl.program_id(0)
        k = 0
        xs = [r[...].astype(F32) for r in refs[k:k + n_in]]
        k += n_in
        ps = [r[...] for r in refs[k:k + n_p]]
        k += n_p
        gs = tuple(r[...].astype(F32) for r in refs[k:k + n_ct])
        k += n_ct
        res_refs = refs[k:k + n_res]
        k += n_res
        gi_refs = refs[k:k + n_in]
        gp_refs = refs[k + n_in:]
        _, vjp = jax.vjp(fn, *xs, *ps)
        g = vjp(gs)
        for j, (ref, v) in enumerate(zip(gi_refs, g[:n_in])):
            if j == 0 and n_res:
                v = v + res_refs[0][...].astype(F32)
            ref[...] = v.astype(ref.dtype)

        @pl.when(i == 0)
        def _():
            for ref in gp_refs:
                ref[...] = jnp.zeros_like(ref)

        for ref, v in zip(gp_refs, g[n_in:]):
            ref[...] += v

    res_in = [] if residual is None else [residual]
    return pl.pallas_call(
        body, name=name, grid=(T // tm,),
        in_specs=([_row_spec(tm, w, cb) for (_, cb, w) in ins] + [_full_spec(p.shape) for p in params]
                  + [_row_spec(tm, w, cb) for (_, cb, w) in cts] + [_row_spec(tm, w, cb) for (_, cb, w) in res_in]),
        out_specs=[_row_spec(tm, w, 0) for (_, _, w) in ins] + [_full_spec(p.shape) for p in params],
        out_shape=([jax.ShapeDtypeStruct((T, w), dt) for (_, _, w), dt in zip(ins, gdtypes)]
                   + [jax.ShapeDtypeStruct(p.shape, F32) for p in params]),
        compiler_params=_ARB,
    )(*[a for (a, _, _) in ins], *params, *[a for (a, _, _) in cts], *[a for (a, _, _) in res_in])


def _matmul(name, a, b, mode, M, N, K, tm, tn, tk, out_dtype, a_off=0, b_off=0, residual=None, side=None, a_swiglu=None):
    nk = K // tk
    n_side = 0 if side is None else len(side[0])
    assert M % tm == 0 and N % tn == 0 and K % tk == 0
    if mode == "nn":
        assert a_off % tk == 0 and b_off % tn == 0
        a_spec = pl.BlockSpec((tm, tk), lambda i, j, k: (i, a_off // tk + k))
        b_spec = pl.BlockSpec((tk, tn), lambda i, j, k: (k, b_off // tn + j))
        dims = (((1,), (0,)), ((), ()))
    elif mode == "nt":
        assert a_off % tk == 0 and b_off == 0
        a_spec = pl.BlockSpec((tm, tk), lambda i, j, k: (i, a_off // tk + k))
        b_spec = pl.BlockSpec((tn, tk), lambda i, j, k: (j, k))
        dims = (((1,), (1,)), ((), ()))
    else:
        assert a_off % tm == 0 and b_off % tn == 0
        a_spec = pl.BlockSpec((tk, tm), lambda i, j, k: (k, a_off // tm + i))
        b_spec = pl.BlockSpec((tk, tn), lambda i, j, k: (k, b_off // tn + j))
        dims = (((0,), (0,)), ((), ()))
    has_res = residual is not None
    has_pair = a_swiglu is not None
    if has_pair:
        blk = tm if mode == "tn" else tk
        assert mode in ("nn", "tn") and a_swiglu % blk == 0
        shift = a_swiglu // blk
        a2_spec = (pl.BlockSpec((tm, tk), lambda i, j, k: (i, a_off // tk + shift + k)) if mode == "nn" else
                   pl.BlockSpec((tk, tm), lambda i, j, k: (k, a_off // tm + shift + i)))

    n_main = 2 + has_res + has_pair
    grid = (M // tm, N // tn, nk)

    def a_tile(refs):
        if not has_pair:
            return refs[0][...].astype(BF16)
        g, up = refs[0][...].astype(F32), refs[n_main - 1][...].astype(F32)
        return (_silu(g) * up).astype(BF16)

    def body(*refs):
        b_ref = refs[1]
        r_ref = refs[2] if has_res else None
        o_ref = refs[n_main + n_side]
        acc_ref = refs[n_main + 2 * n_side + 1]
        i, j, k = pl.program_id(0), pl.program_id(1), pl.program_id(2)
        if n_side:
            start, finish = _chip_exchange(refs[n_main:n_main + n_side], refs[n_main + n_side + 1:n_main + 2 * n_side + 1],
                                           *refs[n_main + 2 * n_side + 2:], gather=side[1])
            pl.when((i == 0) & (j == 0) & (k == 0))(start)

        @pl.when(k == 0)
        def _():
            acc_ref[...] = jnp.zeros_like(acc_ref)

        acc_ref[...] += lax.dot_general(a_tile(refs), b_ref[...].astype(BF16), dims, preferred_element_type=F32)

        @pl.when(k == nk - 1)
        def _():
            acc = acc_ref[...]
            if has_res:
                acc = acc + r_ref[...].astype(F32)
            o_ref[...] = acc.astype(o_ref.dtype)

        if n_side:
            pl.when((i == grid[0] - 1) & (j == grid[1] - 1) & (k == nk - 1))(finish)

    in_specs = [a_spec, b_spec]
    args = [a, b]
    if has_res:
        in_specs.append(pl.BlockSpec((tm, tn), lambda i, j, k: (i, j)))
        args.append(residual)
    if has_pair:
        in_specs.append(a2_spec)
        args.append(a)
    out_specs = pl.BlockSpec((tm, tn), lambda i, j, k: (i, j))
    out_shape = jax.ShapeDtypeStruct((M, N), out_dtype)
    scratch = [pltpu.VMEM((tm, tn), F32)]
    semantics = ("parallel", "parallel", "arbitrary")
    if n_side:
        in_specs += [_ANY] * n_side
        args += list(side[0])
        out_specs = [out_specs] + [_ANY] * n_side
        out_shape = [out_shape] + _chip_exchange_shapes(*side)
        scratch += _chip_exchange_sems(n_side)
        semantics = ("arbitrary",) * 3
    return pl.pallas_call(
        body, name=name, grid=grid, in_specs=in_specs, out_specs=out_specs, out_shape=out_shape, scratch_shapes=scratch,
        compiler_params=pltpu.CompilerParams(dimension_semantics=semantics, vmem_limit_bytes=VMEM_LIMIT),
    )(*args)


def _matmul_pieces(name, mode, pieces, other, M, N, K, tm, tn, tk, out_dtype, residual=None):
    nk = K // tk
    blk = tn if mode == "tn" else tk
    counts = [p.shape[1] // blk for p in pieces]
    assert all(p.shape[1] % blk == 0 for p in pieces) and sum(counts) == (N if mode == "tn" else K) // blk
    starts = [sum(counts[:p]) for p in range(len(pieces))]
    n_p = len(pieces)
    has_res = residual is not None

    def piece_spec(s, n):
        if mode == "tn":
            return pl.BlockSpec((tk, tn), lambda i, j, k: (jnp.where((j >= s) & (j < s + n), k, 0), jnp.clip(j - s, 0, n - 1)))
        return pl.BlockSpec((tm, tk), lambda i, j, k: (i, jnp.clip(k - s, 0, n - 1)))

    if mode == "tn":
        other_spec = pl.BlockSpec((tk, tm), lambda i, j, k: (k, i))
    else:
        other_spec = pl.BlockSpec((tn, tk), lambda i, j, k: (j, k))

    def body(*refs):
        other_ref, piece_refs = refs[0], refs[1:1 + n_p]
        r_ref = refs[1 + n_p] if has_res else None
        o_ref, acc_ref = refs[-2], refs[-1]
        k = pl.program_id(2)
        sel = pl.program_id(1) if mode == "tn" else k

        @pl.when(k == 0)
        def _():
            acc_ref[...] = jnp.zeros_like(acc_ref)

        for p_ref, s, n in zip(piece_refs, starts, counts):
            @pl.when((sel >= s) & (sel < s + n))
            def _(p_ref=p_ref):
                if mode == "tn":
                    acc_ref[...] += lax.dot_general(other_ref[...].astype(BF16), p_ref[...].astype(BF16),
                                                    (((0,), (0,)), ((), ())), preferred_element_type=F32)
                else:
                    acc_ref[...] += lax.dot_general(p_ref[...].astype(BF16), other_ref[...].astype(BF16),
                                                    (((1,), (1,)), ((), ())), preferred_element_type=F32)

        @pl.when(k == nk - 1)
        def _():
            acc = acc_ref[...]
            if has_res:
                acc = acc + r_ref[...].astype(F32)
            o_ref[...] = acc.astype(o_ref.dtype)

    in_specs = [other_spec] + [piece_spec(s, n) for s, n in zip(starts, counts)]
    args = [other] + list(pieces)
    if has_res:
        in_specs.append(pl.BlockSpec((tm, tn), lambda i, j, k: (i, j)))
        args.append(residual)
    return pl.pallas_call(
        body, name=name, grid=(M // tm, N // tn, nk),
        in_specs=in_specs,
        out_specs=pl.BlockSpec((tm, tn), lambda i, j, k: (i, j)),
        out_shape=jax.ShapeDtypeStruct((M, N), out_dtype),
        scratch_shapes=[pltpu.VMEM((tm, tn), F32)],
        compiler_params=pltpu.CompilerParams(dimension_semantics=("parallel", "parallel", "arbitrary"),
                                             vmem_limit_bytes=VMEM_LIMIT),
    )(*args)


def _f_rms(x, w):
    return (x * lax.rsqrt(jnp.mean(x * x, axis=-1, keepdims=True) + EPS) * w,)


def _silu(x):
    return x * jax.nn.sigmoid(x)


def _softplus(x):
    return jnp.maximum(x, 0.0) + jnp.log1p(jnp.exp(-jnp.abs(x)))


def _heads(x, n):
    return [x[:, HEAD_DIM * h:HEAD_DIM * (h + 1)] for h in range(n)]


def _f_dnpoint(c, ba, alog_v, dtb_v):
    cs = _silu(c)
    q, k, v = cs[:, :D_KEY_B], cs[:, D_KEY_B:2 * D_KEY_B], cs[:, 2 * D_KEY_B:]
    qn = jnp.concatenate([t * lax.rsqrt(jnp.sum(t * t, axis=-1, keepdims=True) + EPS) * (HEAD_DIM ** -0.5)
                          for t in _heads(q, N_HEADS_B)], axis=1)
    kn = jnp.concatenate([t * lax.rsqrt(jnp.sum(t * t, axis=-1, keepdims=True) + EPS)
                          for t in _heads(k, N_HEADS_B)], axis=1)
    lane = lax.broadcasted_iota(jnp.int32, ba.shape, 1)
    beta = jax.nn.sigmoid(ba)
    g = -jnp.exp(alog_v) * _softplus(ba + dtb_v)
    bg = jnp.where(lane < N_HEADS_B, beta, jnp.where(lane < 2 * N_HEADS_B, g, 0.0))
    return qn, kn, v, bg


def _f_postdn(o, z, w):
    outs = []
    for oh, zh in zip(_heads(o, N_HEADS_B), _heads(z, N_HEADS_B)):
        outs.append(oh * lax.rsqrt(jnp.mean(oh * oh, axis=-1, keepdims=True) + EPS) * w * _silu(zh))
    return (jnp.concatenate(outs, axis=1),)


def _f_merge(o0, o1, o2, l0, l1, l2):
    m = lax.stop_gradient(jnp.maximum(jnp.maximum(l0, l1), l2))
    e0, e1, e2 = jnp.exp(l0 - m), jnp.exp(l1 - m), jnp.exp(l2 - m)
    return ((e0 * o0 + e1 * o1 + e2 * o2) / (e0 + e1 + e2),)


def _f_mix(ya, yb, gr):
    return (jax.nn.sigmoid(gr[:, :D_MODEL]) * ya + jax.nn.sigmoid(gr[:, D_MODEL:]) * yb,)


def _f_act(gu):
    return (_silu(gu[:, :D_FF]) * gu[:, D_FF:],)


def _attn_block(q, kp, kc, vp, vc, slope_d, first_key):
    k2 = jnp.concatenate([kp, kc], axis=0).astype(BF16)
    v2 = jnp.concatenate([vp, vc], axis=0).astype(BF16)
    s = lax.dot_general(q.astype(BF16), k2, (((1,), (1,)), ((), ())), preferred_element_type=F32)
    s = s * (HEAD_DIM ** -0.5)
    i = lax.broadcasted_iota(jnp.int32, (BLOCK_A, 2 * BLOCK_A), 0)
    j = lax.broadcasted_iota(jnp.int32, (BLOCK_A, 2 * BLOCK_A), 1)
    delta = BLOCK_A + i - j
    valid = (delta >= 0) & (delta <= W_SUB) & (j >= first_key)
    s = jnp.where(valid, s - slope_d * delta.astype(F32), -jnp.inf)
    m = lax.stop_gradient(jnp.max(s, axis=-1, keepdims=True))
    p = jnp.exp(s - m)
    den = jnp.sum(p, axis=-1, keepdims=True)
    o = jnp.dot(p.astype(BF16), v2, preferred_element_type=F32) / den
    lse = jnp.broadcast_to(m + jnp.log(den), (BLOCK_A, HEAD_DIM))
    return o, lse


def _slope_d(group, h, d):
    hg = (group * 4 + h + 1).astype(F32)
    return jnp.exp(jnp.full((1, 1), -8.0 * math.log(2.0) / N_HEADS_A, F32) * hg) * float(d)


ATT_CFG = ((1, 4, 4), (4, 2, 1), (16, 1, 1))
D_GROUP = 1536


def _att_columns():
    out = []
    for g, (_, _, hp) in enumerate(ATT_CFG):
        lanes = HEAD_DIM * hp
        for hb in range(D_ATTN_OUT // lanes):
            for which in range(3):
                out.append((N_HEADS_A * HEAD_DIM * which + D_ATTN_OUT * g + lanes * hb, lanes))
    return out
R_UNROLL = 4


def _attn_geometry(group, T):
    d, m, hp = ATT_CFG[group]
    span = BLOCK_A * d
    tile = m * span
    assert T % tile == 0 and 4 % hp == 0
    return d, m, hp, span, tile, T // tile, HEAD_DIM * hp


def _sub_rows(b, span, r, d):
    return pl.ds(b * span + r, BLOCK_A, stride=d) if d > 1 else pl.ds(b * span, BLOCK_A)


def _for_each_r(d, fn):
    if d <= R_UNROLL:
        for r in range(d):
            fn(r)
    else:
        def step(r, carry):
            fn(r)
            return carry

        lax.fori_loop(0, d, step, 0, unroll=R_UNROLL)


def _attn_fwd(name, proj, group, T):
    d, m, hp, span, tile, nt, lanes = _attn_geometry(group, T)
    qb, kb, vb = [(D_GROUP * group) // lanes + which for which in range(3)]

    def body(q_ref, kc_ref, kp_ref, vc_ref, vp_ref, o_ref, l_ref):
        hb, n = pl.program_id(0), pl.program_id(1)
        first_key = jnp.where(n > 0, 0, BLOCK_A)
        for hh in range(hp):
            ls = slice(HEAD_DIM * hh, HEAD_DIM * (hh + 1))
            sl = _slope_d(group, hb * hp + hh, d)
            for b in range(m):
                def one(r, b=b, ls=ls, sl=sl):
                    rq = _sub_rows(b, span, r, d)
                    if b == 0:
                        rp = _sub_rows(0, span, r, d)
                        kp, vp, fk = kp_ref[rp, ls], vp_ref[rp, ls], first_key
                    else:
                        rp = _sub_rows(b - 1, span, r, d)
                        kp, vp, fk = kc_ref[rp, ls], vc_ref[rp, ls], 0
                    o, lse = _attn_block(q_ref[rq, ls], kp, kc_ref[rq, ls], vp, vc_ref[rq, ls], sl, fk)
                    o_ref[rq, ls] = o
                    l_ref[rq, ls] = lse

                _for_each_r(d, one)

    def cur(cb):
        return pl.BlockSpec((tile, lanes), lambda hb, n: (n, cb + 3 * hb))

    def prev(cb):
        return pl.BlockSpec((span, lanes), lambda hb, n: (jnp.maximum(m * n - 1, 0), cb + 3 * hb))

    return pl.pallas_call(
        body, name=name, grid=(4 // hp, nt),
        in_specs=[cur(qb), cur(kb), prev(kb), cur(vb), prev(vb)],
        out_specs=[pl.BlockSpec((tile, lanes), lambda hb, n: (n, hb))] * 2,
        out_shape=[jax.ShapeDtypeStruct((T, D_ATTN_OUT), F32)] * 2,
        compiler_params=_cparams(2),
    )(proj, proj, proj, proj, proj)


def _attn_block_bwd(q, kp, kc, vp, vc, o, lse, do, dl, slope_d, first_key):
    scale = HEAD_DIM ** -0.5
    nt_dims = (((1,), (1,)), ((), ()))
    tn_dims = (((0,), (0,)), ((), ()))
    qb = q.astype(BF16)
    k2 = jnp.concatenate([kp, kc], axis=0).astype(BF16)
    v2 = jnp.concatenate([vp, vc], axis=0).astype(BF16)
    dob = do.astype(BF16)
    s = lax.dot_general(qb, k2, nt_dims, preferred_element_type=F32) * scale
    i = lax.broadcasted_iota(jnp.int32, (BLOCK_A, 2 * BLOCK_A), 0)
    j = lax.broadcasted_iota(jnp.int32, (BLOCK_A, 2 * BLOCK_A), 1)
    delta = BLOCK_A + i - j
    valid = (delta >= 0) & (delta <= W_SUB) & (j >= first_key)
    lse_col = jnp.max(lse, axis=-1, keepdims=True)
    p = jnp.exp(jnp.where(valid, s - slope_d * delta.astype(F32), -jnp.inf) - lse_col)
    dp = lax.dot_general(dob, v2, nt_dims, preferred_element_type=F32)
    row = jnp.sum(dl, axis=-1, keepdims=True) - jnp.sum(do * o, axis=-1, keepdims=True)
    ds = (p * (dp + row)).astype(BF16)
    dv2 = lax.dot_general(p.astype(BF16), dob, tn_dims, preferred_element_type=F32)
    dq = jnp.dot(ds, k2, preferred_element_type=F32) * scale
    dk2 = lax.dot_general(ds, qb, tn_dims, preferred_element_type=F32) * scale
    return dq, dk2[:BLOCK_A], dk2[BLOCK_A:], dv2[:BLOCK_A], dv2[BLOCK_A:]


def _attn_bwd(name, proj, o, lse, do, dl, group, T, dest):
    d, m, hp, span, tile, nt, lanes = _attn_geometry(group, T)
    qb, kb, vb = [(D_GROUP * group) // lanes + which for which in range(3)]
    n_in = 9 if dest is None else 10

    def body(*refs):
        q_ref, kc_ref, kp_ref, vc_ref, vp_ref, o_ref, l_ref, do_ref, dl_ref = refs[:9]
        out_ref, aq_ref, ak_ref, av_ref = refs[n_in:]
        hb, n = pl.program_id(0), pl.program_id(1)
        first_key = jnp.where(n > 0, 0, BLOCK_A)
        cur_slot = n % 2
        old_slot = 1 - cur_slot

        @pl.when(n == 0)
        def _():
            aq_ref[...] = jnp.zeros_like(aq_ref)
            ak_ref[...] = jnp.zeros_like(ak_ref)
            av_ref[...] = jnp.zeros_like(av_ref)

        @pl.when(n < nt)
        def _():
            for hh in range(hp):
                ls = slice(HEAD_DIM * hh, HEAD_DIM * (hh + 1))
                sl = _slope_d(group, hb * hp + hh, d)
                for b in range(m):
                    def one(r, b=b, ls=ls, sl=sl):
                        rq = _sub_rows(b, span, r, d)
                        if b == 0:
                            rp = _sub_rows(0, span, r, d)
                            kp, vp, fk = kp_ref[rp, ls], vp_ref[rp, ls], first_key
                        else:
                            rp = _sub_rows(b - 1, span, r, d)
                            kp, vp, fk = kc_ref[rp, ls], vc_ref[rp, ls], 0
                        dq, dkp, dkc, dvp, dvc = _attn_block_bwd(
                            q_ref[rq, ls], kp, kc_ref[rq, ls], vp, vc_ref[rq, ls], o_ref[rq, ls], l_ref[rq, ls],
                            do_ref[rq, ls], dl_ref[rq, ls], sl, fk)
                        aq_ref[cur_slot, rq, ls] = dq
                        ak_ref[cur_slot, rq, ls] = dkc
                        av_ref[cur_slot, rq, ls] = dvc
                        if b == 0:
                            last = _sub_rows(m - 1, span, r, d)
                            ak_ref[old_slot, last, ls] += dkp
                            av_ref[old_slot, last, ls] += dvp
                        else:
                            ak_ref[cur_slot, rp, ls] += dkp
                            av_ref[cur_slot, rp, ls] += dvp

                    _for_each_r(d, one)

        out_ref[:, 0:lanes] = aq_ref[old_slot].astype(BF16)
        out_ref[:, lanes:2 * lanes] = ak_ref[old_slot].astype(BF16)
        out_ref[:, 2 * lanes:3 * lanes] = av_ref[old_slot].astype(BF16)

    def cur(cb):
        return pl.BlockSpec((tile, lanes), lambda hb, n: (jnp.minimum(n, nt - 1), cb + 3 * hb))

    def prev(cb):
        return pl.BlockSpec((span, lanes), lambda hb, n: (jnp.maximum(m * jnp.minimum(n, nt - 1) - 1, 0), cb + 3 * hb))

    own = pl.BlockSpec((tile, lanes), lambda hb, n: (jnp.minimum(n, nt - 1), hb))
    late = pl.BlockSpec((tile, 3 * lanes), lambda hb, n: (jnp.maximum(n - 1, 0), (D_GROUP * group) // (3 * lanes) + hb))
    acc = pltpu.VMEM((2, tile, lanes), F32)
    return pl.pallas_call(
        body, name=name, grid=(4 // hp, nt + 1),
        in_specs=[cur(qb), cur(kb), prev(kb), cur(vb), prev(vb), own, own, own, own] + ([] if dest is None else [_ANY]),
        out_specs=late,
        out_shape=jax.ShapeDtypeStruct((T, D_PA), BF16),
        input_output_aliases={} if dest is None else {9: 0},
        scratch_shapes=[acc, acc, acc],
        compiler_params=_cparams(2),
    )(proj, proj, proj, proj, proj, o, lse, do, dl, *([] if dest is None else [dest]))


HALO = 8
HALO_IN = 16


def _shifted(ext):
    return [ext[HALO:, :]] + [pltpu.roll(ext, k, 0)[HALO:, :] for k in range(1, CONV_WIDTH)]


def _conv_taps(sh, w):
    c = sh[0] * w[CONV_WIDTH - 1]
    for k in range(1, CONV_WIDTH):
        c = c + sh[k] * w[CONV_WIDTH - 1 - k]
    return c


def _taps(w_ref):
    return [w_ref[j:j + 1, :] for j in range(CONV_WIDTH)]


def _dn_pre_fwd(name, proj, ba, conv_w, alog_v, dtb_v, tm, T):
    C = 3 * D_KEY_B

    def body(x_ref, ba_ref, w_ref, al_ref, dt_ref, q_ref, k_ref, v_ref, bg_ref, halo_ref):
        i = pl.program_id(0)

        @pl.when(i == 0)
        def _():
            halo_ref[...] = jnp.zeros_like(halo_ref)

        x = x_ref[...].astype(F32)
        ext = jnp.concatenate([halo_ref[...], x], axis=0)
        c = _conv_taps(_shifted(ext), _taps(w_ref))
        halo_ref[...] = x[tm - HALO:, :]
        qn, kn, v, bg = _f_dnpoint(c, ba_ref[...], al_ref[...], dt_ref[...])
        q_ref[...] = qn
        k_ref[...] = kn
        v_ref[...] = v
        bg_ref[...] = bg

    return pl.pallas_call(
        body, name=name, grid=(T // tm,),
        in_specs=[_row_spec(tm, C, 0), _row_spec(tm, 128, 0), _full_spec(conv_w.shape), _full_spec((1, 128)),
                  _full_spec((1, 128))],
        out_specs=[_row_spec(tm, D_KEY_B, 0)] * 3 + [_row_spec(tm, 128, 0)],
        out_shape=[jax.ShapeDtypeStruct((T, D_KEY_B), F32)] * 3 + [jax.ShapeDtypeStruct((T, 128), F32)],
        scratch_shapes=[pltpu.VMEM((HALO, C), F32)],
        compiler_params=_ARB,
    )(proj, ba, conv_w, alog_v, dtb_v)


def _dn_pre_bwd(name, proj, ba, conv_w, alog_v, dtb_v, dq, dk, dv, dbg, tm, T):
    C = 3 * D_KEY_B
    nt = T // tm
    hb = tm // HALO_IN

    def body(x_ref, xh_ref, ba_ref, w_ref, al_ref, dt_ref, dq_ref, dk_ref, dv_ref, dbg_ref,
             dx_ref, dba_ref, dw_ref, dal_ref, ddt_ref, nxt_ref):
        s = pl.program_id(0)
        i = nt - 1 - s

        @pl.when(s == 0)
        def _():
            nxt_ref[...] = jnp.zeros_like(nxt_ref)
            dw_ref[...] = jnp.zeros_like(dw_ref)
            dal_ref[...] = jnp.zeros_like(dal_ref)
            ddt_ref[...] = jnp.zeros_like(ddt_ref)

        w = _taps(w_ref)
        halo = jnp.where(i > 0, xh_ref[...].astype(F32)[HALO_IN - HALO:, :], 0.0)
        ext = jnp.concatenate([halo, x_ref[...].astype(F32)], axis=0)
        sh = _shifted(ext)
        c = _conv_taps(sh, w)
        _, vjp = jax.vjp(_f_dnpoint, c, ba_ref[...], al_ref[...], dt_ref[...])
        dc, dba, dal, ddt = vjp((dq_ref[...], dk_ref[...], dv_ref[...], dbg_ref[...]))
        dba_ref[...] = dba
        dal_ref[...] += dal
        ddt_ref[...] += ddt
        ext2 = jnp.concatenate([dc, nxt_ref[...]], axis=0)
        dx = dc * w[CONV_WIDTH - 1]
        for m in range(1, CONV_WIDTH):
            dx = dx + pltpu.roll(ext2, tm + HALO - m, 0)[:tm, :] * w[CONV_WIDTH - 1 - m]
        dx_ref[...] = dx.astype(dx_ref.dtype)
        nxt_ref[...] = dc[:HALO, :]
        for j in range(CONV_WIDTH):
            dw_ref[j:j + 1, :] += jnp.sum(sh[CONV_WIDTH - 1 - j] * dc, axis=0, keepdims=True)

    def rev(w):
        return pl.BlockSpec((tm, w), lambda s: (nt - 1 - s, 0))

    return pl.pallas_call(
        body, name=name, grid=(nt,),
        in_specs=[rev(C), pl.BlockSpec((HALO_IN, C), lambda s: (jnp.maximum((nt - 1 - s) * hb - 1, 0), 0)), rev(128),
                  _full_spec(conv_w.shape), _full_spec((1, 128)), _full_spec((1, 128)),
                  rev(D_KEY_B), rev(D_KEY_B), rev(D_KEY_B), rev(128)],
        out_specs=[rev(C), rev(128), _full_spec((8, C)), _full_spec((1, 128)), _full_spec((1, 128))],
        out_shape=[jax.ShapeDtypeStruct((T, C), BF16), jax.ShapeDtypeStruct((T, 128), F32),
                   jax.ShapeDtypeStruct((8, C), F32), jax.ShapeDtypeStruct((1, 128), F32),
                   jax.ShapeDtypeStruct((1, 128), F32)],
        scratch_shapes=[pltpu.VMEM((HALO, C), F32)],
        compiler_params=_ARB,
    )(proj, proj, ba, conv_w, alog_v, dtb_v, dq, dk, dv, dbg)


def _bdot(a, b, dims, exact=False):
    if exact:
        return lax.dot_general(a, b, dims, preferred_element_type=F32, precision=lax.Precision.HIGH)
    return lax.dot_general(a.astype(BF16), b.astype(BF16), dims, preferred_element_type=F32)


_B_NN = (((2,), (1,)), ((0,), (0,)))
_B_NT = (((2,), (2,)), ((0,), (0,)))
_B_TN = (((1,), (1,)), ((0,), (0,)))


def _unit_lower_inverse(a):
    C = a.shape[-1]
    r = lax.broadcasted_iota(jnp.int32, (C, C), 0)
    c = lax.broadcasted_iota(jnp.int32, (C, C), 1)
    pw = -a
    t = jnp.where((r == c)[None], 1.0, 0.0) + pw
    for _ in range(int(math.log2(C)) - 1):
        pw = _bdot(pw, pw, _B_NN, exact=True)
        t = t + _bdot(t, pw, _B_NN, exact=True)
    return t


@jax.custom_vjp
def _solve_given_inverse(t, a, rhs):
    return _bdot(t, rhs, _B_NN, exact=True)


def _solve_fwd(t, a, rhs):
    sol = _bdot(t, rhs, _B_NN, exact=True)
    return sol, (t, sol)


def _solve_bwd(res, dsol):
    t, sol = res
    drhs = _bdot(t, dsol, _B_TN, exact=True)
    return jnp.zeros_like(t), -_bdot(drhs, sol, _B_NT, exact=True), drhs


_solve_given_inverse.defvjp(_solve_fwd, _solve_bwd)


def _stack(xs):
    return jnp.concatenate([x[None] for x in xs], axis=0)


def _delta_chunk(S, q, k, v, bg, t_inv=None):
    H, C = N_HEADS_B, CHUNK
    r = lax.broadcasted_iota(jnp.int32, (C, C), 0)
    c = lax.broadcasted_iota(jnp.int32, (C, C), 1)
    incl, strict, eye = r >= c, r > c, r == c
    gc_all = jnp.dot(incl.astype(F32), bg, preferred_element_type=F32, precision=lax.Precision.HIGHEST)
    lane = lax.broadcasted_iota(jnp.int32, (C, 128), 1)

    def col(x, l):
        return jnp.sum(jnp.where(lane == l, x, 0.0), axis=1, keepdims=True)

    beta = _stack([col(bg, h) for h in range(H)])
    gc = _stack([col(gc_all, H + h) for h in range(H)])
    gc_row = jnp.sum(jnp.where(eye[None], jnp.broadcast_to(gc, (H, C, C)), 0.0), axis=1, keepdims=True)
    row = lax.broadcasted_iota(jnp.int32, (H, C, 1), 1)
    gc_last = jnp.sum(jnp.where(row == C - 1, gc, 0.0), axis=1, keepdims=True)
    decay = jnp.exp(jnp.where(incl[None], gc - gc_row, -jnp.inf))
    qh = _stack(_heads(q, H))
    kh = _stack(_heads(k, H))
    vh = _stack(_heads(v, H))
    kk = _bdot(kh, kh, _B_NT, exact=True)
    a = jnp.where(strict[None], beta * kk * decay, 0.0)
    egc = jnp.exp(gc)
    rhs = jnp.concatenate([beta * vh, (beta * egc) * kh], axis=-1)
    if t_inv is None:
        t_inv = _unit_lower_inverse(a)
        sol = _bdot(t_inv, rhs, _B_NN, exact=True)
    else:
        sol = _solve_given_inverse(t_inv, a, rhs)
    u_bar, w = sol[..., :HEAD_DIM], sol[..., HEAD_DIM:]
    qk = _bdot(qh, kh, _B_NT) * decay
    q_dec = qh * egc
    k_dec = kh * jnp.exp(gc_last - gc)
    u = u_bar - _bdot(w, S, _B_NN)
    o = _bdot(q_dec, S, _B_NN) + _bdot(qk, u, _B_NN)
    S_new = jnp.exp(gc_last) * S + _bdot(k_dec, u, _B_TN)
    o2 = jnp.concatenate([lax.index_in_dim(o, h, 0, keepdims=False) for h in range(H)], axis=1)
    return o2, S_new, t_inv


def _delta_fwd(name, q, k, v, bg, T):
    nc = T // CHUNK
    H = N_HEADS_B
    G, R = CHUNKS_FWD, CHUNKS_FWD * CHUNK

    def body(q_ref, k_ref, v_ref, bg_ref, o_ref, s_out_ref, t_out_ref, s_ref):
        @pl.when(pl.program_id(0) == 0)
        def _():
            s_ref[...] = jnp.zeros_like(s_ref)

        S = s_ref[...]
        for c in range(G):
            rows = slice(c * CHUNK, (c + 1) * CHUNK)
            s_out_ref[c] = S
            o, S, t_inv = _delta_chunk(S, q_ref[rows, :], k_ref[rows, :], v_ref[rows, :], bg_ref[rows, :])
            o_ref[rows, :] = o
            t_out_ref[c] = t_inv
        s_ref[...] = S

    return pl.pallas_call(
        body, name=name, grid=(nc // G,),
        in_specs=[_row_spec(R, D_KEY_B, 0)] * 3 + [_row_spec(R, 128, 0)],
        out_specs=[_row_spec(R, D_KEY_B, 0), pl.BlockSpec((G, H, HEAD_DIM, HEAD_DIM), lambda i: (i, 0, 0, 0)),
                   pl.BlockSpec((G, H, CHUNK, CHUNK), lambda i: (i, 0, 0, 0))],
        out_shape=[jax.ShapeDtypeStruct((T, D_KEY_B), F32), jax.ShapeDtypeStruct((nc, H, HEAD_DIM, HEAD_DIM), F32),
                   jax.ShapeDtypeStruct((nc, H, CHUNK, CHUNK), F32)],
        scratch_shapes=[pltpu.VMEM((H, HEAD_DIM, HEAD_DIM), F32)],
        compiler_params=_ARB,
    )(q, k, v, bg)


def _delta_bwd(name, q, k, v, bg, s_all, t_all, do, T):
    nc = T // CHUNK
    H = N_HEADS_B
    G, R = CHUNKS_BWD, CHUNKS_BWD * CHUNK
    ns = nc // G

    def body(q_ref, k_ref, v_ref, bg_ref, s_in_ref, t_ref, do_ref, dq_ref, dk_ref, dv_ref, dbg_ref, ds_ref):
        @pl.when(pl.program_id(0) == 0)
        def _():
            ds_ref[...] = jnp.zeros_like(ds_ref)

        dS = ds_ref[...]
        for c in reversed(range(G)):
            rows = slice(c * CHUNK, (c + 1) * CHUNK)
            t_inv = t_ref[c]

            def f(S, qq, kk, vv, bb, t_inv=t_inv):
                return _delta_chunk(S, qq, kk, vv, bb, t_inv)[:2]

            _, vjp = jax.vjp(f, s_in_ref[c], q_ref[rows, :], k_ref[rows, :], v_ref[rows, :], bg_ref[rows, :])
            dS, dq, dk, dv, dbg = vjp((do_ref[rows, :], dS))
            dq_ref[rows, :] = dq
            dk_ref[rows, :] = dk
            dv_ref[rows, :] = dv
            dbg_ref[rows, :] = dbg
        ds_ref[...] = dS

    def rev(w):
        return pl.BlockSpec((R, w), lambda s: (ns - 1 - s, 0))

    def rev4(n):
        return pl.BlockSpec((G, H, n, n), lambda s: (ns - 1 - s, 0, 0, 0))

    return pl.pallas_call(
        body, name=name, grid=(ns,),
        in_specs=[rev(D_KEY_B)] * 3 + [rev(128), rev4(HEAD_DIM), rev4(CHUNK), rev(D_KEY_B)],
        out_specs=[rev(D_KEY_B)] * 3 + [rev(128)],
        out_shape=[jax.ShapeDtypeStruct((T, D_KEY_B), F32)] * 3 + [jax.ShapeDtypeStruct((T, 128), F32)],
        scratch_shapes=[pltpu.VMEM((H, HEAD_DIM, HEAD_DIM), F32)],
        compiler_params=_ARB,
    )(q, k, v, bg, s_all, t_all, do)


def _loss_fb(name, h2, w, tgt, tm, T):
    def tile_loss(h, wv, t):
        y = _f_rms(h, wv)[0]
        e = y - t
        return 0.5 * jnp.sum(jnp.mean(e * e, axis=-1))

    def body(h_ref, w_ref, t_ref, dh_ref, loss_ref, dw_ref):
        @pl.when(pl.program_id(0) == 0)
        def _():
            loss_ref[...] = jnp.zeros_like(loss_ref)
            dw_ref[...] = jnp.zeros_like(dw_ref)

        val, (dh, dw) = jax.value_and_grad(tile_loss, argnums=(0, 1))(h_ref[...], w_ref[...], t_ref[...])
        dh_ref[...] = dh
        dw_ref[...] += dw
        loss_ref[...] += jnp.reshape(val, (1, 1))

    return pl.pallas_call(
        body, name=name, grid=(T // tm,),
        in_specs=[_row_spec(tm, D_MODEL, 0), _full_spec((1, D_MODEL)), _row_spec(tm, D_MODEL, 0)],
        out_specs=[_row_spec(tm, D_MODEL, 0), _full_spec((1, 1)), _full_spec((1, D_MODEL))],
        out_shape=[jax.ShapeDtypeStruct((T, D_MODEL), F32), jax.ShapeDtypeStruct((1, 1), F32),
                   jax.ShapeDtypeStruct((1, D_MODEL), F32)],
        compiler_params=_ARB,
    )(h2, w, tgt)


def _adamw(name, w, g, m, v, tile):
    R = w.shape[0]

    def body(w_ref, g_ref, m_ref, v_ref, d_out, m_out, v_out):
        g = g_ref[...]
        mm = ADAM_B1 * m_ref[...] + (1.0 - ADAM_B1) * g
        vv = ADAM_B2 * v_ref[...] + (1.0 - ADAM_B2) * jnp.square(g)
        m_hat = mm / (1.0 - ADAM_B1 ** ADAM_STEP)
        v_hat = vv / (1.0 - ADAM_B2 ** ADAM_STEP)
        d_out[...] = -ADAM_LR * (m_hat / (jnp.sqrt(v_hat) + ADAM_EPS) + ADAM_WD * w_ref[...])
        m_out[...] = mm
        v_out[...] = vv

    assert R % tile == 0
    spec = pl.BlockSpec((tile, w.shape[1]), lambda i: (i, 0))
    return pl.pallas_call(
        body, name=name, grid=(R // tile,), in_specs=[spec] * 4, out_specs=[spec] * 3,
        out_shape=[jax.ShapeDtypeStruct(w.shape, F32)] * 3, compiler_params=_ARB,
    )(w, g, m, v)


def _sum4(name, r, tile):
    _, R, C = r.shape
    assert R % tile == 0

    def body(r_ref, o_ref):
        o_ref[...] = ((r_ref[0].astype(F32) + r_ref[1].astype(F32)) + r_ref[2].astype(F32)) + r_ref[3].astype(F32)

    return pl.pallas_call(
        body, name=name, grid=(R // tile,),
        in_specs=[pl.BlockSpec((4, tile, C), lambda i: (0, i, 0))],
        out_specs=pl.BlockSpec((tile, C), lambda i: (i, 0)),
        out_shape=jax.ShapeDtypeStruct((R, C), F32), compiler_params=_ARB,
    )(r)


def _add2(name, a, b, tile):
    R, C = a.shape
    assert R % tile == 0

    def body(a_ref, b_ref, o_ref):
        o_ref[...] = (a_ref[...].astype(F32) + b_ref[...].astype(F32)).astype(BF16)

    spec = pl.BlockSpec((tile, C), lambda i: (i, 0))
    return pl.pallas_call(
        body, name=name, grid=(R // tile,), in_specs=[spec, spec], out_specs=spec,
        out_shape=jax.ShapeDtypeStruct((R, C), BF16), compiler_params=_ARB,
    )(a, b)


_ANY = pl.BlockSpec(memory_space=pl.ANY)


def _place():
    return lax.axis_index("x"), lax.axis_index("y"), lax.axis_index("c")


def _chip_exchange(in_refs, out_refs, send_sems, recv_sems, local_sems, gather):
    n = len(in_refs)
    x, y, c = _place()
    me = 2 * x + y
    chips = [(1 - x, y), (x, 1 - y), (1 - x, 1 - y)]

    def src(a, slot):
        return in_refs[a] if gather else in_refs[a].at[slot]

    def copy(a, k, src_slot, dst_slot, px, py):
        return pltpu.make_async_remote_copy(src_ref=src(a, src_slot), dst_ref=out_refs[a].at[dst_slot],
                                            send_sem=send_sems.at[3 * a + k], recv_sem=recv_sems.at[3 * a + k],
                                            device_id=(px, py, c), device_id_type=MESH)

    def mine():
        return [pltpu.make_async_copy(src(a, me), out_refs[a].at[me], local_sems.at[a]) for a in range(n)]

    def sends():
        return [copy(a, k, 2 * px + py, me, px, py) for a in range(n) for k, (px, py) in enumerate(chips)]

    def start():
        for cp in mine() + sends():
            cp.start()

    def finish():
        for a in range(n):
            for k, (px, py) in enumerate(chips):
                copy(a, k, me, 2 * px + py, px, py).wait_recv()
        for cp in sends():
            cp.wait_send()
        for cp in mine():
            cp.wait()

    return start, finish


def _chip_exchange_shapes(arrays, gather):
    return [jax.ShapeDtypeStruct(((4,) + a.shape) if gather else a.shape, a.dtype) for a in arrays]


def _chip_exchange_sems(n):
    return [pltpu.SemaphoreType.DMA((3 * n,)), pltpu.SemaphoreType.DMA((3 * n,)), pltpu.SemaphoreType.DMA((n,))]


def _exchange_chips(name, arrays, gather):
    n = len(arrays)

    def body(*refs):
        start, finish = _chip_exchange(refs[:n], refs[n:2 * n], *refs[2 * n:], gather=gather)
        start()
        finish()

    return pl.pallas_call(
        body, name=name, in_specs=[_ANY] * n, out_specs=[_ANY] * n,
        out_shape=_chip_exchange_shapes(arrays, gather), scratch_shapes=_chip_exchange_sems(n),
    )(*arrays)


def _swap_sibling(name, arrays):
    n = len(arrays)

    def body(*refs):
        in_refs, out_refs, send_sems, recv_sems = refs[:n], refs[n:2 * n], refs[2 * n], refs[2 * n + 1]
        x, y, c = _place()
        cps = [pltpu.make_async_remote_copy(src_ref=in_refs[a], dst_ref=out_refs[a], send_sem=send_sems.at[a],
                                            recv_sem=recv_sems.at[a], device_id=(x, y, 1 - c), device_id_type=MESH)
               for a in range(n)]
        for cp in cps:
            cp.start()
        for cp in cps:
            cp.wait()

    return pl.pallas_call(
        body, name=name, in_specs=[_ANY] * n, out_specs=[_ANY] * n,
        out_shape=[jax.ShapeDtypeStruct(a.shape, a.dtype) for a in arrays],
        scratch_shapes=[pltpu.SemaphoreType.DMA((n,)), pltpu.SemaphoreType.DMA((n,))],
    )(*arrays)


def _allsum_small(name, pack):
    def body(p_ref, out_ref, slots, send_sems, recv_sems):
        x, y, c = _place()
        me = 4 * x + 2 * y + c
        slots[me] = p_ref[...]
        sends = []
        for k in range(1, 8):
            px = 1 - x if k & 4 else x
            py = 1 - y if k & 2 else y
            pc = 1 - c if k & 1 else c
            cp = pltpu.make_async_remote_copy(src_ref=p_ref, dst_ref=slots.at[me], send_sem=send_sems.at[k - 1],
                                              recv_sem=recv_sems.at[k - 1], device_id=(px, py, pc), device_id_type=MESH)
            cp.start()
            sends.append(cp)
        for k in range(1, 8):
            px = 1 - x if k & 4 else x
            py = 1 - y if k & 2 else y
            pc = 1 - c if k & 1 else c
            pltpu.make_async_remote_copy(src_ref=p_ref, dst_ref=slots.at[4 * px + 2 * py + pc],
                                         send_sem=send_sems.at[k - 1], recv_sem=recv_sems.at[k - 1],
                                         device_id=(px, py, pc), device_id_type=MESH).wait_recv()
        for cp in sends:
            cp.wait_send()
        acc = slots[0]
        for s in range(1, 8):
            acc = acc + slots[s]
        out_ref[...] = acc

    vm = pl.BlockSpec(memory_space=pltpu.VMEM)
    return pl.pallas_call(
        body, name=name, in_specs=[vm], out_specs=vm,
        out_shape=jax.ShapeDtypeStruct(pack.shape, pack.dtype),
        scratch_shapes=[pltpu.VMEM((8,) + pack.shape, pack.dtype), pltpu.SemaphoreType.DMA((7,)),
                        pltpu.SemaphoreType.DMA((7,))],
    )(pack)


def _pack_shards(parts, rows_total):
    flat = [p.reshape(-1, D_MODEL) for p in parts]
    used = sum(f.shape[0] for f in flat)
    return jnp.concatenate(flat + [jnp.zeros((rows_total - used, D_MODEL), flat[0].dtype)], axis=0)


def _unpack_shards(buf, shapes):
    out, r0 = [], 0
    for shp in shapes:
        n = math.prod(shp) // D_MODEL
        out.append(buf[r0:r0 + n].reshape(shp))
        r0 += n
    return out


def kernel(x, norm_mix, w_in, conv_w, a_log, dt_bias, dn_norm, w_proj_attn, w_proj_delta, w_out, norm_ffn, w_gate, w_up, w_down, norm_final, loss_target, m_norm_mix, m_w_in, m_conv_w, m_a_log, m_dt_bias, m_dn_norm, m_w_proj_attn, m_w_proj_delta, m_w_out, m_norm_ffn, m_w_gate, m_w_up, m_w_down, m_norm_final, v_norm_mix, v_w_in, v_conv_w, v_a_log, v_dt_bias, v_dn_norm, v_w_proj_attn, v_w_proj_delta, v_w_out, v_norm_ffn, v_w_gate, v_w_up, v_w_down, v_norm_final):
    T = x.shape[1]
    D = D_MODEL
    tm = 512
    tw = 256
    mt = 1024 if T % 1024 == 0 else 512
    x2 = x.reshape(T, D)
    tgt = loss_target.reshape(T, D)

    packed = (w_proj_attn, w_proj_delta, w_out, w_gate, w_up, w_down)
    conv_bits = lax.bitcast_convert_type(conv_w[0], BF16)
    w16 = _pack_shards([w[0].astype(BF16) for w in packed] + [conv_bits], PACK_R)
    win16 = w_in[0].astype(BF16)
    my_c = lax.axis_index("c")

    def both_halves(mine, other, axis):
        return jnp.concatenate([jnp.where(my_c == 0, mine, other), jnp.where(my_c == 0, other, mine)], axis=axis)

    pack_half = lax.dynamic_slice_in_dim(w16, my_c * HALF, HALF, axis=0)
    (win_mine,) = _exchange_chips("gather_w_in", [lax.dynamic_slice_in_dim(win16, my_c * W_IN_HALF, W_IN_HALF, axis=0)], True)
    (win_other,) = _swap_sibling("swap_w_in", [win_mine])
    win4 = both_halves(win_mine, win_other, 1)

    def win_cols(g0, g1):
        out = []
        for j in range(4):
            lo, hi = max(g0, W_IN_COLS * j), min(g1, W_IN_COLS * (j + 1))
            if lo < hi:
                out.append(win4[j][:, lo - W_IN_COLS * j:hi - W_IN_COLS * j])
        return out

    Wa = jnp.concatenate([p for c0, w in _att_columns() for p in win_cols(c0, c0 + w)], axis=1)
    Wb = jnp.concatenate(win_cols(4608, 8704) + win_cols(8720, D_IN), axis=1)
    Wba = jnp.pad(jnp.concatenate(win_cols(8704, 8720), axis=1), ((0, 0), (0, 112)))
    alog_v = jnp.pad(a_log, ((0, 0), (8, 112)))
    dtb_v = jnp.pad(dt_bias, ((0, 0), (8, 112)))

    (u,) = _rowwise("rms_mix", _f_rms, [(x2, 0, D)], [norm_mix], [(D, BF16)], tm, T)
    proj_a, pack_mine = _matmul("mm_in_a", u, Wa, "nn", T, D_PA, D, mt, 2304, D, F32, side=([pack_half], True))
    (pack_other,) = _swap_sibling("swap_pack", [pack_mine])
    wg = both_halves(pack_mine, pack_other, 1)

    def seg(r0, n):
        return wg[:, r0:r0 + n, :]

    r = 0
    Wpa = seg(r, 128).reshape(4, D_ATTN_OUT, 256).transpose(1, 0, 2).reshape(D_ATTN_OUT, D); r += 128
    Wpd = seg(r, 256).reshape(D, D); r += 256
    Wout = seg(r, 256).reshape(D, D); r += 256
    Wg = seg(r, 704).reshape(4, D, 704).transpose(1, 0, 2).reshape(D, D_FF); r += 704
    Wu = seg(r, 704).reshape(4, D, 704).transpose(1, 0, 2).reshape(D, D_FF); r += 704
    Wd = seg(r, 704).reshape(D_FF, D); r += 704
    convw = lax.bitcast_convert_type(seg(r, 6).reshape(4, CONV_WIDTH, 768, 2), F32)
    convw = convw.transpose(1, 0, 2).reshape(CONV_WIDTH, 3 * D_KEY_B)
    Wgu = jnp.concatenate([Wg, Wu], axis=1)
    proj_b = _matmul("mm_in_b", u, Wb, "nn", T, D_PB, D, mt, 3072, D, BF16)
    ba = _matmul("mm_ba", u, Wba, "nn", T, 128, D, mt, 128, D, F32)

    att = [_attn_fwd(f"attn_fwd{g}", proj_a, g, T) for g in range(3)]
    (ya_m,) = _rowwise("merge", _f_merge, [(att[g][0], 0, 512) for g in range(3)] + [(att[g][1], 0, 512) for g in range(3)],
                       [], [(512, BF16)], tm, T)
    qn, kn, vd, bg = _dn_pre_fwd("dn_pre_fwd", proj_b, ba, convw, alog_v, dtb_v, tw, T)
    o_d, s_all, t_all = _delta_fwd("delta_fwd", qn, kn, vd, bg, T)
    (od,) = _rowwise("post_dn", _f_postdn, [(o_d, 0, D), (proj_b, C_Z // D, D)], [dn_norm], [(D, BF16)], tm, T)
    y_a = _matmul("mm_pa", ya_m, Wpa, "nn", T, D, 512, mt, D, 512, F32)
    y_b = _matmul("mm_pd", od, Wpd, "nn", T, D, D, mt, D, D, F32)
    (mix,) = _rowwise("mix", _f_mix, [(y_a, 0, D), (y_b, 0, D), (proj_b, C_GATE // 2048, 2048)], [], [(D, BF16)], tm, T)
    h1 = _matmul("mm_out", mix, Wout, "nn", T, D, D, mt, D, D, F32, residual=x2)
    (hn,) = _rowwise("rms_ffn", _f_rms, [(h1, 0, D)], [norm_ffn], [(D, BF16)], tm, T)
    gu = _matmul("mm_gu", hn, Wgu, "nn", T, 2 * D_FF, D, mt, 2816, D, BF16)
    h2 = _matmul("mm_down", gu, Wd, "nn", T, D, D_FF, mt, D, 1408, F32, residual=h1, a_swiglu=D_FF)
    dh2, loss, g_norm_final = _loss_fb("loss", h2, norm_final.reshape(1, D), tgt, tm, T)

    dact = _matmul("mm_dact", dh2, Wd, "nt", T, D_FF, D, mt, 1408, D, BF16)
    gWd = _matmul("mm_gwd", gu, dh2, "tn", D_FF, D, T, 1408, D, mt, BF16, a_swiglu=D_FF)
    (dgu,) = _rowwise_vjp("act_bwd", _f_act, [(gu, 0, 2 * D_FF)], [], [(dact, 0, D_FF)], [BF16], tw, T)
    dhn = _matmul("mm_dhn", dgu, Wgu, "nt", T, D, 2 * D_FF, mt, D, 2816, F32)
    gWgu = _matmul("mm_gwgu", hn, dgu, "tn", D, 2 * D_FF, T, D, 1408, 2 * mt, BF16)
    dh1, g_norm_ffn = _rowwise_vjp("rms_ffn_bwd", _f_rms, [(h1, 0, D)], [norm_ffn], [(dhn, 0, D)], [F32], tm, T,
                                   residual=(dh2, 0, D))
    dmix = _matmul("mm_dmix", dh1, Wout, "nt", T, D, D, mt, D, D, F32)
    gWout = _matmul("mm_gwout", mix, dh1, "tn", D, D, T, D, D, mt, BF16)
    dya, dyb, dgate = _rowwise_vjp("mix_bwd", _f_mix, [(y_a, 0, D), (y_b, 0, D), (proj_b, C_GATE // 2048, 2048)], [],
                                   [(dmix, 0, D)], [BF16, BF16, BF16], tm, T)
    dya_m = _matmul("mm_dyam", dya, Wpa, "nt", T, 512, D, mt, 512, D, F32)
    gWpa = _matmul("mm_gwpa", ya_m, dya, "tn", 512, D, T, 512, D, mt, BF16)
    dod = _matmul("mm_dod", dyb, Wpd, "nt", T, D, D, mt, D, D, F32)
    gWpd = _matmul("mm_gwpd", od, dyb, "tn", D, D, T, D, D, mt, BF16)
    do_d, dz, g_dn_norm = _rowwise_vjp("post_dn_bwd", _f_postdn, [(o_d, 0, D), (proj_b, C_Z // D, D)], [dn_norm],
                                       [(dod, 0, D)], [F32, BF16], tm, T)
    dqn, dkn, dvd, dbg = _delta_bwd("delta_bwd", qn, kn, vd, bg, s_all, t_all, do_d, T)
    dqkvd, dba, g_conv8, g_alog_v, g_dtb_v = _dn_pre_bwd("dn_pre_bwd", proj_b, ba, convw, alog_v, dtb_v,
                                                         dqn, dkn, dvd, dbg, tw, T)
    def pair_sum(tag, g, h):
        keep = lax.dynamic_slice_in_dim(g, my_c * h, h, axis=1)
        give = lax.dynamic_slice_in_dim(g, (1 - my_c) * h, h, axis=1)
        (got,) = _swap_sibling("swap_grads_" + tag, [give])
        return _add2("add_cores_" + tag, keep.reshape(-1, g.shape[2]), got.reshape(-1, g.shape[2]), HALF_TILE).reshape(keep.shape)

    def cols4(g, n):
        return g.reshape(g.shape[0], 4, n).transpose(1, 0, 2).reshape(4, -1, D)

    parts = [cols4(gWpa, 256), gWpd.reshape(4, 256, D), gWout.reshape(4, 256, D), cols4(gWgu[:, :D_FF], 704),
             cols4(gWgu[:, D_FF:], 704), gWd.reshape(4, 704, D), cols4(g_conv8[:CONV_WIDTH], 768)]
    used = sum(p.shape[1] for p in parts)
    g_pack = jnp.concatenate([p.astype(BF16) for p in parts] + [jnp.zeros((4, PACK_R - used, D), BF16)], axis=1)
    pair_pack = pair_sum("pack", g_pack, HALF)

    mg = _rowwise_vjp("merge_bwd", _f_merge,
                      [(att[g][0], 0, 512) for g in range(3)] + [(att[g][1], 0, 512) for g in range(3)], [],
                      [(dya_m, 0, 512)], [F32] * 6, tm, T)
    dproj_a = None
    for g in range(3):
        dproj_a = _attn_bwd(f"attn_bwd{g}", proj_a, att[g][0], att[g][1], mg[g], mg[3 + g], g, T, dproj_a)
    dproj_b = [dqkvd, dz, dgate]
    gWa, recv_pack = _matmul("mm_gwa", u, dproj_a, "tn", D, D_PA, T, D, 1536, 2 * mt, BF16, side=([pair_pack], False))
    gWb = _matmul_pieces("mm_gwb", "tn", dproj_b, u, D, D_PB, T, D, 1024, mt, BF16)
    gWba = _matmul("mm_gwba", u, dba, "tn", D, 128, T, D, 128, mt, BF16)

    pieces, off = [(gWb[:, :C_GATE], D_PA), (gWba[:, :16], 8704), (gWb[:, C_GATE:], 8720)], 0
    for c0, w in _att_columns():
        pieces.append((gWa[:, off:off + w], c0))
        off += w
    pieces.sort(key=lambda t: t[1])

    def win_shard(j):
        out = []
        for arr, g0 in pieces:
            lo, hi = max(g0, W_IN_COLS * j), min(g0 + arr.shape[1], W_IN_COLS * (j + 1))
            if lo < hi:
                out.append(arr[:, lo - g0:hi - g0])
        return jnp.concatenate(out, axis=1)

    g_win4 = jnp.concatenate([win_shard(j)[None] for j in range(4)], axis=0)
    pair_win = pair_sum("w_in", g_win4, W_IN_HALF)

    du0 = _matmul("mm_du_ba", dba, Wba, "nt", T, D, 128, mt, D, 128, F32)
    du1 = _matmul_pieces("mm_du_b", "nt", dproj_b, Wb, T, D, D_PB, mt, D, 1024, F32, residual=du0)
    du, recv_win = _matmul("mm_du_a", dproj_a, Wa, "nt", T, D, D_PA, mt, D, 2304, F32, residual=du1, side=([pair_win], False))
    dx, g_norm_mix = _rowwise_vjp("rms_mix_bwd", _f_rms, [(x2, 0, D)], [norm_mix], [(du, 0, D)], [F32], tm, T,
                                  residual=(dh1, 0, D))

    g_mine = [_sum4("sum_chips_pack", recv_pack, HALF_TILE), _sum4("sum_chips_w_in", recv_win, HALF_TILE)]
    g_other = _swap_sibling("swap_sums", g_mine)
    g_all = both_halves(g_mine[0], g_other[0], 0)
    g_win = both_halves(g_mine[1], g_other[1], 0)

    locals_ = (w_proj_attn, w_proj_delta, w_out, w_gate, w_up, w_down, conv_w, w_in)
    ms = (m_w_proj_attn, m_w_proj_delta, m_w_out, m_w_gate, m_w_up, m_w_down, m_conv_w, m_w_in)
    vs = (v_w_proj_attn, v_w_proj_delta, v_w_out, v_w_gate, v_w_up, v_w_down, v_conv_w, v_w_in)
    big_names = ("w_proj_attn", "w_proj_delta", "w_out", "w_gate", "w_up", "w_down", "conv_w", "w_in")
    adam_tiles = (512, 256, 256, 256, 256, 352, CONV_WIDTH, 128)
    g_shards = _unpack_shards(g_all, [w.shape for w in locals_[:-1]]) + [g_win.reshape(w_in.shape)]
    big = [g_shards, [], [], []]
    for nm, w, g, m_, v_, tl in zip(big_names, locals_, g_shards, ms, vs, adam_tiles):
        res = _adamw("adamw_" + nm, w[0], g[0], m_[0], v_[0], tl)
        for kind in range(3):
            big[kind + 1].append(res[kind].reshape(w.shape))

    def small_pack(nm, nf, nfin, dn, al, dt, extra):
        row3 = jnp.concatenate([dn.reshape(1, 128), al.reshape(1, 8), dt.reshape(1, 8), extra.reshape(1, 1),
                                jnp.zeros((1, D - 145), F32)], axis=1)
        return jnp.concatenate([nm.reshape(1, D), nf.reshape(1, D), nfin.reshape(1, D), row3, jnp.zeros((4, D), F32)], axis=0)

    zero1 = jnp.zeros((1, 1), F32)
    sp = small_pack(g_norm_mix, g_norm_ffn, g_norm_final, g_dn_norm, g_alog_v[:, 8:16], g_dtb_v[:, 8:16], loss)
    tot = _allsum_small("allsum_small", sp)
    sm = _adamw("adamw_small", small_pack(norm_mix, norm_ffn, norm_final, dn_norm, a_log, dt_bias, zero1), tot,
                small_pack(m_norm_mix, m_norm_ffn, m_norm_final, m_dn_norm, m_a_log, m_dt_bias, zero1),
                small_pack(v_norm_mix, v_norm_ffn, v_norm_final, v_dn_norm, v_a_log, v_dt_bias, zero1), 8)
    sm = [tot] + list(sm)

    def small_unpack(p):
        return {"norm_mix": p[0:1], "norm_ffn": p[1:2], "norm_final": p[2], "dn_norm": p[3:4, :128],
                "a_log": p[3:4, 128:136], "dt_bias": p[3:4, 136:144]}

    sm = [small_unpack(p) for p in sm]
    loss_out = tot[3, 144]

    order = ("norm_mix", "w_in", "conv_w", "a_log", "dt_bias", "dn_norm", "w_proj_attn", "w_proj_delta", "w_out",
             "norm_ffn", "w_gate", "w_up", "w_down", "norm_final")
    outs = [loss_out, dx.reshape(x.shape)]
    for kind in range(4):
        for n in order:
            outs.append(big[kind][big_names.index(n)] if n in big_names else sm[kind][n])
    return tuple(outs)
```

```python
import functools
import math

import jax
import jax.numpy as jnp
from jax import lax
from jax.experimental import pallas as pl
from jax.experimental.pallas import tpu as pltpu

F32 = jnp.float32
BF16 = jnp.bfloat16
MESH = pl.DeviceIdType.MESH

D_MODEL = 1024
N_HEADS_A = 12
HEAD_DIM = 128
BLOCK_A = 128
DILATIONS = (1, 4, 16)
W_SUB = 128
D_ATTN_OUT = 512
N_HEADS_B = 8
D_KEY_B = 1024
CONV_WIDTH = 4
CHUNK = 64
CHUNKS_FWD, CHUNKS_BWD = 2, 1
D_FF = 2816
EPS = 1e-6
D_IN = 10768
D_PA = 4608
D_PB = 6144
C_Z, C_GATE = 3072, 4096

ADAM_LR, ADAM_B1, ADAM_B2, ADAM_EPS, ADAM_WD, ADAM_STEP = 0.001, 0.9, 0.999, 1e-08, 0.01, 10

PACK_R = 2816
HALF = PACK_R // 2
HALF_TILE = 128
W_IN_COLS = 2692
W_IN_HALF = D_MODEL // 2

VMEM_LIMIT = 48 * 1024 * 1024


def _cparams(n):
    return pltpu.CompilerParams(dimension_semantics=("arbitrary",) * n, vmem_limit_bytes=VMEM_LIMIT)


_ARB = _cparams(1)


def _row_spec(tm, w, cb):
    return pl.BlockSpec((tm, w), lambda i: (i, cb))


def _full_spec(shape):
    nd = len(shape)
    return pl.BlockSpec(shape, lambda i: (0,) * nd)


def _rowwise(name, fn, ins, params, outs, tm, T):
    n_in, n_p = len(ins), len(params)

    def body(*refs):
        xs = [r[...].astype(F32) for r in refs[:n_in]] + [r[...] for r in refs[n_in:n_in + n_p]]
        res = fn(*xs)
        for o_ref, v in zip(refs[n_in + n_p:], res):
            o_ref[...] = v.astype(o_ref.dtype)

    return pl.pallas_call(
        body, name=name, grid=(T // tm,),
        in_specs=[_row_spec(tm, w, cb) for (_, cb, w) in ins] + [_full_spec(p.shape) for p in params],
        out_specs=[_row_spec(tm, w, 0) for (w, _) in outs],
        out_shape=[jax.ShapeDtypeStruct((T, w), dt) for (w, dt) in outs],
        compiler_params=_ARB,
    )(*[a for (a, _, _) in ins], *params)


def _rowwise_vjp(name, fn, ins, params, cts, gdtypes, tm, T, residual=None):
    n_in, n_p, n_ct = len(ins), len(params), len(cts)
    n_res = 0 if residual is None else 1

    def body(*refs):
        i = pl.program_id(0)
        k = 0
        xs = [r[...].astype(F32) for r in refs[k:k + n_in]]
        k += n_in
        ps = [r[...] for r in refs[k:k + n_p]]
        k += n_p
        gs = tuple(r[...].astype(F32) for r in refs[k:k + n_ct])
        k += n_ct
        res_refs = refs[k:k + n_res]
        k += n_res
        gi_refs = refs[k:k + n_in]
        gp_refs = refs[k + n_in:]
        _, vjp = jax.vjp(fn, *xs, *ps)
        g = vjp(gs)
        for j, (ref, v) in enumerate(zip(gi_refs, g[:n_in])):
            if j == 0 and n_res:
                v = v + res_refs[0][...].astype(F32)
            ref[...] = v.astype(ref.dtype)

        @pl.when(i == 0)
        def _():
            for ref in gp_refs:
                ref[...] = jnp.zeros_like(ref)

        for ref, v in zip(gp_refs, g[n_in:]):
            ref[...] += v

    res_in = [] if residual is None else [residual]
    return pl.pallas_call(
        body, name=name, grid=(T // tm,),
        in_specs=([_row_spec(tm, w, cb) for (_, cb, w) in ins] + [_full_spec(p.shape) for p in params]
                  + [_row_spec(tm, w, cb) for (_, cb, w) in cts] + [_row_spec(tm, w, cb) for (_, cb, w) in res_in]),
        out_specs=[_row_spec(tm, w, 0) for (_, _, w) in ins] + [_full_spec(p.shape) for p in params],
        out_shape=([jax.ShapeDtypeStruct((T, w), dt) for (_, _, w), dt in zip(ins, gdtypes)]
                   + [jax.ShapeDtypeStruct(p.shape, F32) for p in params]),
        compiler_params=_ARB,
    )(*[a for (a, _, _) in ins], *params, *[a for (a, _, _) in cts], *[a for (a, _, _) in res_in])


def _matmul(name, a, b, mode, M, N, K, tm, tn, tk, out_dtype, a_off=0, b_off=0, residual=None, side=None, a_swiglu=None):
    nk = K // tk
    n_side = 0 if side is None else len(side[0])
    assert M % tm == 0 and N % tn == 0 and K % tk == 0
    if mode == "nn":
        assert a_off % tk == 0 and b_off % tn == 0
        a_spec = pl.BlockSpec((tm, tk), lambda i, j, k: (i, a_off // tk + k))
        b_spec = pl.BlockSpec((tk, tn), lambda i, j, k: (k, b_off // tn + j))
        dims = (((1,), (0,)), ((), ()))
    elif mode == "nt":
        assert a_off % tk == 0 and b_off == 0
        a_spec = pl.BlockSpec((tm, tk), lambda i, j, k: (i, a_off // tk + k))
        b_spec = pl.BlockSpec((tn, tk), lambda i, j, k: (j, k))
        dims = (((1,), (1,)), ((), ()))
    else:
        assert a_off % tm == 0 and b_off % tn == 0
        a_spec = pl.BlockSpec((tk, tm), lambda i, j, k: (k, a_off // tm + i))
        b_spec = pl.BlockSpec((tk, tn), lambda i, j, k: (k, b_off // tn + j))
        dims = (((0,), (0,)), ((), ()))
    has_res = residual is not None
    has_pair = a_swiglu is not None
    if has_pair:
        blk = tm if mode == "tn" else tk
        assert mode in ("nn", "tn") and a_swiglu % blk == 0
        shift = a_swiglu // blk
        a2_spec = (pl.BlockSpec((tm, tk), lambda i, j, k: (i, a_off // tk + shift + k)) if mode == "nn" else
                   pl.BlockSpec((tk, tm), lambda i, j, k: (k, a_off // tm + shift + i)))

    n_main = 2 + has_res + has_pair
    grid = (M // tm, N // tn, nk)

    def a_tile(refs):
        if not has_pair:
            return refs[0][...].astype(BF16)
        g, up = refs[0][...].astype(F32), refs[n_main - 1][...].astype(F32)
        return (_silu(g) * up).astype(BF16)

    def body(*refs):
        b_ref = refs[1]
        r_ref = refs[2] if has_res else None
        o_ref = refs[n_main + n_side]
        acc_ref = refs[n_main + 2 * n_side + 1]
        i, j, k = pl.program_id(0), pl.program_id(1), pl.program_id(2)
        if n_side:
            start, finish = _chip_exchange(refs[n_main:n_main + n_side], refs[n_main + n_side + 1:n_main + 2 * n_side + 1],
                                           *refs[n_main + 2 * n_side + 2:], gather=side[1])
            pl.when((i == 0) & (j == 0) & (k == 0))(start)

        @pl.when(k == 0)
        def _():
            acc_ref[...] = jnp.zeros_like(acc_ref)

        acc_ref[...] += lax.dot_general(a_tile(refs), b_ref[...].astype(BF16), dims, preferred_element_type=F32)

        @pl.when(k == nk - 1)
        def _():
            acc = acc_ref[...]
            if has_res:
                acc = acc + r_ref[...].astype(F32)
            o_ref[...] = acc.astype(o_ref.dtype)

        if n_side:
            pl.when((i == grid[0] - 1) & (j == grid[1] - 1) & (k == nk - 1))(finish)

    in_specs = [a_spec, b_spec]
    args = [a, b]
    if has_res:
        in_specs.append(pl.BlockSpec((tm, tn), lambda i, j, k: (i, j)))
        args.append(residual)
    if has_pair:
        in_specs.append(a2_spec)
        args.append(a)
    out_specs = pl.BlockSpec((tm, tn), lambda i, j, k: (i, j))
    out_shape = jax.ShapeDtypeStruct((M, N), out_dtype)
    scratch = [pltpu.VMEM((tm, tn), F32)]
    semantics = ("parallel", "parallel", "arbitrary")
    if n_side:
        in_specs += [_ANY] * n_side
        args += list(side[0])
        out_specs = [out_specs] + [_ANY] * n_side
        out_shape = [out_shape] + _chip_exchange_shapes(*side)
        scratch += _chip_exchange_sems(n_side)
        semantics = ("arbitrary",) * 3
    return pl.pallas_call(
        body, name=name, grid=grid, in_specs=in_specs, out_specs=out_specs, out_shape=out_shape, scratch_shapes=scratch,
        compiler_params=pltpu.CompilerParams(dimension_semantics=semantics, vmem_limit_bytes=VMEM_LIMIT),
    )(*args)


def _matmul_pieces(name, mode, pieces, other, M, N, K, tm, tn, tk, out_dtype, residual=None):
    nk = K // tk
    blk = tn if mode == "tn" else tk
    counts = [p.shape[1] // blk for p in pieces]
    assert all(p.shape[1] % blk == 0 for p in pieces) and sum(counts) == (N if mode == "tn" else K) // blk
    starts = [sum(counts[:p]) for p in range(len(pieces))]
    n_p = len(pieces)
    has_res = residual is not None

    def piece_spec(s, n):
        if mode == "tn":
            return pl.BlockSpec((tk, tn), lambda i, j, k: (jnp.where((j >= s) & (j < s + n), k, 0), jnp.clip(j - s, 0, n - 1)))
        return pl.BlockSpec((tm, tk), lambda i, j, k: (i, jnp.clip(k - s, 0, n - 1)))

    if mode == "tn":
        other_spec = pl.BlockSpec((tk, tm), lambda i, j, k: (k, i))
    else:
        other_spec = pl.BlockSpec((tn, tk), lambda i, j, k: (j, k))

    def body(*refs):
        other_ref, piece_refs = refs[0], refs[1:1 + n_p]
        r_ref = refs[1 + n_p] if has_res else None
        o_ref, acc_ref = refs[-2], refs[-1]
        k = pl.program_id(2)
        sel = pl.program_id(1) if mode == "tn" else k

        @pl.when(k == 0)
        def _():
            acc_ref[...] = jnp.zeros_like(acc_ref)

        for p_ref, s, n in zip(piece_refs, starts, counts):
            @pl.when((sel >= s) & (sel < s + n))
            def _(p_ref=p_ref):
                if mode == "tn":
                    acc_ref[...] += lax.dot_general(other_ref[...].astype(BF16), p_ref[...].astype(BF16),
                                                    (((0,), (0,)), ((), ())), preferred_element_type=F32)
                else:
                    acc_ref[...] += lax.dot_general(p_ref[...].astype(BF16), other_ref[...].astype(BF16),
                                                    (((1,), (1,)), ((), ())), preferred_element_type=F32)

        @pl.when(k == nk - 1)
        def _():
            acc = acc_ref[...]
            if has_res:
                acc = acc + r_ref[...].astype(F32)
            o_ref[...] = acc.astype(o_ref.dtype)

    in_specs = [other_spec] + [piece_spec(s, n) for s, n in zip(starts, counts)]
    args = [other] + list(pieces)
    if has_res:
        in_specs.append(pl.BlockSpec((tm, tn), lambda i, j, k: (i, j)))
        args.append(residual)
    return pl.pallas_call(
        body, name=name, grid=(M // tm, N // tn, nk),
        in_specs=in_specs,
        out_specs=pl.BlockSpec((tm, tn), lambda i, j, k: (i, j)),
        out_shape=jax.ShapeDtypeStruct((M, N), out_dtype),
        scratch_shapes=[pltpu.VMEM((tm, tn), F32)],
        compiler_params=pltpu.CompilerParams(dimension_semantics=("parallel", "parallel", "arbitrary"),
                                             vmem_limit_bytes=VMEM_LIMIT),
    )(*args)


def _f_rms(x, w):
    return (x * lax.rsqrt(jnp.mean(x * x, axis=-1, keepdims=True) + EPS) * w,)


def _silu(x):
    return x * jax.nn.sigmoid(x)


def _softplus(x):
    return jnp.maximum(x, 0.0) + jnp.log1p(jnp.exp(-jnp.abs(x)))


def _heads(x, n):
    return [x[:, HEAD_DIM * h:HEAD_DIM * (h + 1)] for h in range(n)]


def _f_dnpoint(c, ba, alog_v, dtb_v):
    cs = _silu(c)
    q, k, v = cs[:, :D_KEY_B], cs[:, D_KEY_B:2 * D_KEY_B], cs[:, 2 * D_KEY_B:]
    qn = jnp.concatenate([t * lax.rsqrt(jnp.sum(t * t, axis=-1, keepdims=True) + EPS) * (HEAD_DIM ** -0.5)
                          for t in _heads(q, N_HEADS_B)], axis=1)
    kn = jnp.concatenate([t * lax.rsqrt(jnp.sum(t * t, axis=-1, keepdims=True) + EPS)
                          for t in _heads(k, N_HEADS_B)], axis=1)
    lane = lax.broadcasted_iota(jnp.int32, ba.shape, 1)
    beta = jax.nn.sigmoid(ba)
    g = -jnp.exp(alog_v) * _softplus(ba + dtb_v)
    bg = jnp.where(lane < N_HEADS_B, beta, jnp.where(lane < 2 * N_HEADS_B, g, 0.0))
    return qn, kn, v, bg


def _dnpoint_bwd(c, ba, alog_v, dtb_v, dq, dk, dv, dbg):
    sig = jax.nn.sigmoid(c)
    cs = c * sig

    def l2_bwd(t, dy, scale):
        outs = []
        for th, dh in zip(_heads(t, N_HEADS_B), _heads(dy, N_HEADS_B)):
            r = lax.rsqrt(jnp.sum(th * th, axis=-1, keepdims=True) + EPS)
            y = th * r
            outs.append((scale * r) * (dh - y * jnp.sum(dh * y, axis=-1, keepdims=True)))
        return outs

    dcs = jnp.concatenate(l2_bwd(cs[:, :D_KEY_B], dq, HEAD_DIM ** -0.5) + l2_bwd(cs[:, D_KEY_B:2 * D_KEY_B], dk, 1.0) + [dv],
                          axis=1)
    dc = dcs * (sig * (1.0 + c * (1.0 - sig)))
    lane = lax.broadcasted_iota(jnp.int32, ba.shape, 1)
    is_beta, is_g = lane < N_HEADS_B, (lane >= N_HEADS_B) & (lane < 2 * N_HEADS_B)
    beta = jax.nn.sigmoid(ba)
    z = ba + dtb_v
    minus_ea = -jnp.exp(alog_v)
    dz = jnp.where(is_g, dbg * minus_ea * jax.nn.sigmoid(z), 0.0)
    dba = jnp.where(is_beta, dbg * beta * (1.0 - beta), dz)
    dal = jnp.sum(jnp.where(is_g, dbg * minus_ea * _softplus(z), 0.0), axis=0, keepdims=True)
    ddt = jnp.sum(dz, axis=0, keepdims=True)
    return dc, dba, dal, ddt


def _f_postdn(o, z, w):
    outs = []
    for oh, zh in zip(_heads(o, N_HEADS_B), _heads(z, N_HEADS_B)):
        outs.append(oh * lax.rsqrt(jnp.mean(oh * oh, axis=-1, keepdims=True) + EPS) * w * _silu(zh))
    return (jnp.concatenate(outs, axis=1),)


def _f_merge(o0, o1, o2, l0, l1, l2):
    m = lax.stop_gradient(jnp.maximum(jnp.maximum(l0, l1), l2))
    e0, e1, e2 = jnp.exp(l0 - m), jnp.exp(l1 - m), jnp.exp(l2 - m)
    return ((e0 * o0 + e1 * o1 + e2 * o2) / (e0 + e1 + e2),)


def _f_mix(ya, yb, gr):
    return (jax.nn.sigmoid(gr[:, :D_MODEL]) * ya + jax.nn.sigmoid(gr[:, D_MODEL:]) * yb,)


def _f_act(gu):
    return (_silu(gu[:, :D_FF]) * gu[:, D_FF:],)


def _attn_block(q, kp, kc, vp, vc, slope_d, first_key):
    k2 = jnp.concatenate([kp, kc], axis=0).astype(BF16)
    v2 = jnp.concatenate([vp, vc], axis=0).astype(BF16)
    s = lax.dot_general(q.astype(BF16), k2, (((1,), (1,)), ((), ())), preferred_element_type=F32)
    s = s * (HEAD_DIM ** -0.5)
    i = lax.broadcasted_iota(jnp.int32, (BLOCK_A, 2 * BLOCK_A), 0)
    j = lax.broadcasted_iota(jnp.int32, (BLOCK_A, 2 * BLOCK_A), 1)
    delta = BLOCK_A + i - j
    valid = (delta >= 0) & (delta <= W_SUB) & (j >= first_key)
    s = jnp.where(valid, s - slope_d * delta.astype(F32), -jnp.inf)
    m = lax.stop_gradient(jnp.max(s, axis=-1, keepdims=True))
    p = jnp.exp(s - m)
    den = jnp.sum(p, axis=-1, keepdims=True)
    o = jnp.dot(p.astype(BF16), v2, preferred_element_type=F32) / den
    lse = jnp.broadcast_to(m + jnp.log(den), (BLOCK_A, HEAD_DIM))
    return o, lse


def _slope_d(group, h, d):
    hg = (group * 4 + h + 1).astype(F32)
    return jnp.exp(jnp.full((1, 1), -8.0 * math.log(2.0) / N_HEADS_A, F32) * hg) * float(d)


ATT_CFG = ((1, 4, 4), (4, 2, 1), (16, 1, 1))
D_GROUP = 1536


def _att_columns():
    out = []
    for g, (_, _, hp) in enumerate(ATT_CFG):
        lanes = HEAD_DIM * hp
        for hb in range(D_ATTN_OUT // lanes):
            for which in range(3):
                out.append((N_HEADS_A * HEAD_DIM * which + D_ATTN_OUT * g + lanes * hb, lanes))
    return out
R_UNROLL = 4


def _attn_geometry(group, T):
    d, m, hp = ATT_CFG[group]
    span = BLOCK_A * d
    tile = m * span
    assert T % tile == 0 and 4 % hp == 0
    return d, m, hp, span, tile, T // tile, HEAD_DIM * hp


def _sub_rows(b, span, r, d):
    return pl.ds(b * span + r, BLOCK_A, stride=d) if d > 1 else pl.ds(b * span, BLOCK_A)


def _for_each_r(d, fn):
    if d <= R_UNROLL:
        for r in range(d):
            fn(r)
    else:
        def step(r, carry):
            fn(r)
            return carry

        lax.fori_loop(0, d, step, 0, unroll=R_UNROLL)


def _attn_fwd(name, proj, group, T):
    d, m, hp, span, tile, nt, lanes = _attn_geometry(group, T)
    qb, kb, vb = [(D_GROUP * group) // lanes + which for which in range(3)]

    def body(q_ref, kc_ref, kp_ref, vc_ref, vp_ref, o_ref, l_ref):
        hb, n = pl.program_id(0), pl.program_id(1)
        first_key = jnp.where(n > 0, 0, BLOCK_A)
        for hh in range(hp):
            ls = slice(HEAD_DIM * hh, HEAD_DIM * (hh + 1))
            sl = _slope_d(group, hb * hp + hh, d)
            for b in range(m):
                def one(r, b=b, ls=ls, sl=sl):
                    rq = _sub_rows(b, span, r, d)
                    if b == 0:
                        rp = _sub_rows(0, span, r, d)
                        kp, vp, fk = kp_ref[rp, ls], vp_ref[rp, ls], first_key
                    else:
                        rp = _sub_rows(b - 1, span, r, d)
                        kp, vp, fk = kc_ref[rp, ls], vc_ref[rp, ls], 0
                    o, lse = _attn_block(q_ref[rq, ls], kp, kc_ref[rq, ls], vp, vc_ref[rq, ls], sl, fk)
                    o_ref[rq, ls] = o
                    l_ref[rq, ls] = lse

                _for_each_r(d, one)

    def cur(cb):
        return pl.BlockSpec((tile, lanes), lambda hb, n: (n, cb + 3 * hb))

    def prev(cb):
        return pl.BlockSpec((span, lanes), lambda hb, n: (jnp.maximum(m * n - 1, 0), cb + 3 * hb))

    return pl.pallas_call(
        body, name=name, grid=(4 // hp, nt),
        in_specs=[cur(qb), cur(kb), prev(kb), cur(vb), prev(vb)],
        out_specs=[pl.BlockSpec((tile, lanes), lambda hb, n: (n, hb))] * 2,
        out_shape=[jax.ShapeDtypeStruct((T, D_ATTN_OUT), F32)] * 2,
        compiler_params=_cparams(2),
    )(proj, proj, proj, proj, proj)


def _attn_block_bwd(q, kp, kc, vp, vc, o, lse, do, dl, slope_d, first_key):
    scale = HEAD_DIM ** -0.5
    nt_dims = (((1,), (1,)), ((), ()))
    tn_dims = (((0,), (0,)), ((), ()))
    qb = q.astype(BF16)
    k2 = jnp.concatenate([kp, kc], axis=0).astype(BF16)
    v2 = jnp.concatenate([vp, vc], axis=0).astype(BF16)
    dob = do.astype(BF16)
    s = lax.dot_general(qb, k2, nt_dims, preferred_element_type=F32) * scale
    i = lax.broadcasted_iota(jnp.int32, (BLOCK_A, 2 * BLOCK_A), 0)
    j = lax.broadcasted_iota(jnp.int32, (BLOCK_A, 2 * BLOCK_A), 1)
    delta = BLOCK_A + i - j
    valid = (delta >= 0) & (delta <= W_SUB) & (j >= first_key)
    lse_col = jnp.max(lse, axis=-1, keepdims=True)
    p = jnp.exp(jnp.where(valid, s - slope_d * delta.astype(F32), -jnp.inf) - lse_col)
    dp = lax.dot_general(dob, v2, nt_dims, preferred_element_type=F32)
    row = jnp.sum(dl, axis=-1, keepdims=True) - jnp.sum(do * o, axis=-1, keepdims=True)
    ds = (p * (dp + row)).astype(BF16)
    dv2 = lax.dot_general(p.astype(BF16), dob, tn_dims, preferred_element_type=F32)
    dq = jnp.dot(ds, k2, preferred_element_type=F32) * scale
    dk2 = lax.dot_general(ds, qb, tn_dims, preferred_element_type=F32) * scale
    return dq, dk2[:BLOCK_A], dk2[BLOCK_A:], dv2[:BLOCK_A], dv2[BLOCK_A:]


def _attn_bwd(name, proj, o, lse, do, dl, group, T, dest):
    d, m, hp, span, tile, nt, lanes = _attn_geometry(group, T)
    qb, kb, vb = [(D_GROUP * group) // lanes + which for which in range(3)]
    n_in = 9 if dest is None else 10

    def body(*refs):
        q_ref, kc_ref, kp_ref, vc_ref, vp_ref, o_ref, l_ref, do_ref, dl_ref = refs[:9]
        out_ref, aq_ref, ak_ref, av_ref = refs[n_in:]
        hb, n = pl.program_id(0), pl.program_id(1)
        first_key = jnp.where(n > 0, 0, BLOCK_A)
        cur_slot = n % 2
        old_slot = 1 - cur_slot

        @pl.when(n == 0)
        def _():
            aq_ref[...] = jnp.zeros_like(aq_ref)
            ak_ref[...] = jnp.zeros_like(ak_ref)
            av_ref[...] = jnp.zeros_like(av_ref)

        @pl.when(n < nt)
        def _():
            for hh in range(hp):
                ls = slice(HEAD_DIM * hh, HEAD_DIM * (hh + 1))
                sl = _slope_d(group, hb * hp + hh, d)
                for b in range(m):
                    def one(r, b=b, ls=ls, sl=sl):
                        rq = _sub_rows(b, span, r, d)
                        if b == 0:
                            rp = _sub_rows(0, span, r, d)
                            kp, vp, fk = kp_ref[rp, ls], vp_ref[rp, ls], first_key
                        else:
                            rp = _sub_rows(b - 1, span, r, d)
                            kp, vp, fk = kc_ref[rp, ls], vc_ref[rp, ls], 0
                        dq, dkp, dkc, dvp, dvc = _attn_block_bwd(
                            q_ref[rq, ls], kp, kc_ref[rq, ls], vp, vc_ref[rq, ls], o_ref[rq, ls], l_ref[rq, ls],
                            do_ref[rq, ls], dl_ref[rq, ls], sl, fk)
                        aq_ref[cur_slot, rq, ls] = dq
                        ak_ref[cur_slot, rq, ls] = dkc
                        av_ref[cur_slot, rq, ls] = dvc
                        if b == 0:
                            last = _sub_rows(m - 1, span, r, d)
                            ak_ref[old_slot, last, ls] += dkp
                            av_ref[old_slot, last, ls] += dvp
                        else:
                            ak_ref[cur_slot, rp, ls] += dkp
                            av_ref[cur_slot, rp, ls] += dvp

                    _for_each_r(d, one)

        out_ref[:, 0:lanes] = aq_ref[old_slot].astype(BF16)
        out_ref[:, lanes:2 * lanes] = ak_ref[old_slot].astype(BF16)
        out_ref[:, 2 * lanes:3 * lanes] = av_ref[old_slot].astype(BF16)

    def cur(cb):
        return pl.BlockSpec((tile, lanes), lambda hb, n: (jnp.minimum(n, nt - 1), cb + 3 * hb))

    def prev(cb):
        return pl.BlockSpec((span, lanes), lambda hb, n: (jnp.maximum(m * jnp.minimum(n, nt - 1) - 1, 0), cb + 3 * hb))

    own = pl.BlockSpec((tile, lanes), lambda hb, n: (jnp.minimum(n, nt - 1), hb))
    late = pl.BlockSpec((tile, 3 * lanes), lambda hb, n: (jnp.maximum(n - 1, 0), (D_GROUP * group) // (3 * lanes) + hb))
    acc = pltpu.VMEM((2, tile, lanes), F32)
    return pl.pallas_call(
        body, name=name, grid=(4 // hp, nt + 1),
        in_specs=[cur(qb), cur(kb), prev(kb), cur(vb), prev(vb), own, own, own, own] + ([] if dest is None else [_ANY]),
        out_specs=late,
        out_shape=jax.ShapeDtypeStruct((T, D_PA), BF16),
        input_output_aliases={} if dest is None else {9: 0},
        scratch_shapes=[acc, acc, acc],
        compiler_params=_cparams(2),
    )(proj, proj, proj, proj, proj, o, lse, do, dl, *([] if dest is None else [dest]))


HALO = 8
HALO_IN = 16


def _shifted(ext):
    return [ext[HALO:, :]] + [pltpu.roll(ext, k, 0)[HALO:, :] for k in range(1, CONV_WIDTH)]


def _conv_taps(sh, w):
    c = sh[0] * w[CONV_WIDTH - 1]
    for k in range(1, CONV_WIDTH):
        c = c + sh[k] * w[CONV_WIDTH - 1 - k]
    return c


def _taps(w_ref):
    return [w_ref[j:j + 1, :] for j in range(CONV_WIDTH)]


def _dn_pre_fwd(name, proj, ba, conv_w, alog_v, dtb_v, tm, T):
    C = 3 * D_KEY_B

    def body(x_ref, ba_ref, w_ref, al_ref, dt_ref, q_ref, k_ref, v_ref, bg_ref, halo_ref):
        i = pl.program_id(0)

        @pl.when(i == 0)
        def _():
            halo_ref[...] = jnp.zeros_like(halo_ref)

        x = x_ref[...].astype(F32)
        ext = jnp.concatenate([halo_ref[...], x], axis=0)
        c = _conv_taps(_shifted(ext), _taps(w_ref))
        halo_ref[...] = x[tm - HALO:, :]
        qn, kn, v, bg = _f_dnpoint(c, ba_ref[...], al_ref[...], dt_ref[...])
        q_ref[...] = qn
        k_ref[...] = kn
        v_ref[...] = v
        bg_ref[...] = bg

    return pl.pallas_call(
        body, name=name, grid=(T // tm,),
        in_specs=[_row_spec(tm, C, 0), _row_spec(tm, 128, 0), _full_spec(conv_w.shape), _full_spec((1, 128)),
                  _full_spec((1, 128))],
        out_specs=[_row_spec(tm, D_KEY_B, 0)] * 3 + [_row_spec(tm, 128, 0)],
        out_shape=[jax.ShapeDtypeStruct((T, D_KEY_B), F32)] * 3 + [jax.ShapeDtypeStruct((T, 128), F32)],
        scratch_shapes=[pltpu.VMEM((HALO, C), F32)],
        compiler_params=_ARB,
    )(proj, ba, conv_w, alog_v, dtb_v)


def _dn_pre_bwd(name, proj, ba, conv_w, alog_v, dtb_v, dq, dk, dv, dbg, tm, T):
    C = 3 * D_KEY_B
    nt = T // tm
    hb = tm // HALO_IN

    def body(x_ref, xh_ref, ba_ref, w_ref, al_ref, dt_ref, dq_ref, dk_ref, dv_ref, dbg_ref,
             dx_ref, dba_ref, dw_ref, dal_ref, ddt_ref, nxt_ref):
        s = pl.program_id(0)
        i = nt - 1 - s

        @pl.when(s == 0)
        def _():
            nxt_ref[...] = jnp.zeros_like(nxt_ref)
            dw_ref[...] = jnp.zeros_like(dw_ref)
            dal_ref[...] = jnp.zeros_like(dal_ref)
            ddt_ref[...] = jnp.zeros_like(ddt_ref)

        w = _taps(w_ref)
        halo = jnp.where(i > 0, xh_ref[...].astype(F32)[HALO_IN - HALO:, :], 0.0)
        ext = jnp.concatenate([halo, x_ref[...].astype(F32)], axis=0)
        sh = _shifted(ext)
        c = _conv_taps(sh, w)
        dc, dba, dal, ddt = _dnpoint_bwd(c, ba_ref[...], al_ref[...], dt_ref[...],
                                         dq_ref[...], dk_ref[...], dv_ref[...], dbg_ref[...])
        dba_ref[...] = dba
        dal_ref[...] += dal
        ddt_ref[...] += ddt
        ext2 = jnp.concatenate([dc, nxt_ref[...]], axis=0)
        dx = dc * w[CONV_WIDTH - 1]
        for m in range(1, CONV_WIDTH):
            dx = dx + pltpu.roll(ext2, tm + HALO - m, 0)[:tm, :] * w[CONV_WIDTH - 1 - m]
        dx_ref[...] = dx.astype(dx_ref.dtype)
        nxt_ref[...] = dc[:HALO, :]
        for j in range(CONV_WIDTH):
            dw_ref[j:j + 1, :] += jnp.sum(sh[CONV_WIDTH - 1 - j] * dc, axis=0, keepdims=True)

    def rev(w):
        return pl.BlockSpec((tm, w), lambda s: (nt - 1 - s, 0))

    return pl.pallas_call(
        body, name=name, grid=(nt,),
        in_specs=[rev(C), pl.BlockSpec((HALO_IN, C), lambda s: (jnp.maximum((nt - 1 - s) * hb - 1, 0), 0)), rev(128),
                  _full_spec(conv_w.shape), _full_spec((1, 128)), _full_spec((1, 128)),
                  rev(D_KEY_B), rev(D_KEY_B), rev(D_KEY_B), rev(128)],
        out_specs=[rev(C), rev(128), _full_spec((8, C)), _full_spec((1, 128)), _full_spec((1, 128))],
        out_shape=[jax.ShapeDtypeStruct((T, C), BF16), jax.ShapeDtypeStruct((T, 128), F32),
                   jax.ShapeDtypeStruct((8, C), F32), jax.ShapeDtypeStruct((1, 128), F32),
                   jax.ShapeDtypeStruct((1, 128), F32)],
        scratch_shapes=[pltpu.VMEM((HALO, C), F32)],
        compiler_params=_ARB,
    )(proj, proj, ba, conv_w, alog_v, dtb_v, dq, dk, dv, dbg)


def _bdot(a, b, dims, exact=False):
    if exact:
        return lax.dot_general(a, b, dims, preferred_element_type=F32, precision=lax.Precision.HIGH)
    return lax.dot_general(a.astype(BF16), b.astype(BF16), dims, preferred_element_type=F32)


_B_NN = (((2,), (1,)), ((0,), (0,)))
_B_NT = (((2,), (2,)), ((0,), (0,)))
_B_TN = (((1,), (1,)), ((0,), (0,)))


def _unit_lower_inverse(a):
    C = a.shape[-1]
    r = lax.broadcasted_iota(jnp.int32, (C, C), 0)
    c = lax.broadcasted_iota(jnp.int32, (C, C), 1)
    pw = -a
    t = jnp.where((r == c)[None], 1.0, 0.0) + pw
    for _ in range(int(math.log2(C)) - 1):
        pw = _bdot(pw, pw, _B_NN, exact=True)
        t = t + _bdot(t, pw, _B_NN, exact=True)
    return t


@jax.custom_vjp
def _solve_given_inverse(t, a, rhs):
    return _bdot(t, rhs, _B_NN, exact=True)


def _solve_fwd(t, a, rhs):
    sol = _bdot(t, rhs, _B_NN, exact=True)
    return sol, (t, sol)


def _solve_bwd(res, dsol):
    t, sol = res
    drhs = _bdot(t, dsol, _B_TN, exact=True)
    return jnp.zeros_like(t), -_bdot(drhs, sol, _B_NT, exact=True), drhs


_solve_given_inverse.defvjp(_solve_fwd, _solve_bwd)


def _stack(xs):
    return jnp.concatenate([x[None] for x in xs], axis=0)


def _delta_chunk(S, q, k, v, bg, t_inv=None):
    H, C = N_HEADS_B, CHUNK
    r = lax.broadcasted_iota(jnp.int32, (C, C), 0)
    c = lax.broadcasted_iota(jnp.int32, (C, C), 1)
    incl, strict, eye = r >= c, r > c, r == c
    gc_all = jnp.dot(incl.astype(F32), bg, preferred_element_type=F32, precision=lax.Precision.HIGHEST)
    lane = lax.broadcasted_iota(jnp.int32, (C, 128), 1)

    def col(x, l):
        return jnp.sum(jnp.where(lane == l, x, 0.0), axis=1, keepdims=True)

    beta = _stack([col(bg, h) for h in range(H)])
    gc = _stack([col(gc_all, H + h) for h in range(H)])
    gc_row = jnp.sum(jnp.where(eye[None], jnp.broadcast_to(gc, (H, C, C)), 0.0), axis=1, keepdims=True)
    row = lax.broadcasted_iota(jnp.int32, (H, C, 1), 1)
    gc_last = jnp.sum(jnp.where(row == C - 1, gc, 0.0), axis=1, keepdims=True)
    decay = jnp.exp(jnp.where(incl[None], gc - gc_row, -jnp.inf))
    qh = _stack(_heads(q, H))
    kh = _stack(_heads(k, H))
    vh = _stack(_heads(v, H))
    kk = _bdot(kh, kh, _B_NT, exact=True)
    a = jnp.where(strict[None], beta * kk * decay, 0.0)
    egc = jnp.exp(gc)
    rhs = jnp.concatenate([beta * vh, (beta * egc) * kh], axis=-1)
    if t_inv is None:
        t_inv = _unit_lower_inverse(a)
        sol = _bdot(t_inv, rhs, _B_NN, exact=True)
    else:
        sol = _solve_given_inverse(t_inv, a, rhs)
    u_bar, w = sol[..., :HEAD_DIM], sol[..., HEAD_DIM:]
    qk = _bdot(qh, kh, _B_NT) * decay
    q_dec = qh * egc
    k_dec = kh * jnp.exp(gc_last - gc)
    u = u_bar - _bdot(w, S, _B_NN)
    o = _bdot(q_dec, S, _B_NN) + _bdot(qk, u, _B_NN)
    S_new = jnp.exp(gc_last) * S + _bdot(k_dec, u, _B_TN)
    o2 = jnp.concatenate([lax.index_in_dim(o, h, 0, keepdims=False) for h in range(H)], axis=1)
    return o2, S_new, t_inv


def _delta_fwd(name, q, k, v, bg, T):
    nc = T // CHUNK
    H = N_HEADS_B
    G, R = CHUNKS_FWD, CHUNKS_FWD * CHUNK

    def body(q_ref, k_ref, v_ref, bg_ref, o_ref, s_out_ref, t_out_ref, s_ref):
        @pl.when(pl.program_id(0) == 0)
        def _():
            s_ref[...] = jnp.zeros_like(s_ref)

        S = s_ref[...]
        for c in range(G):
            rows = slice(c * CHUNK, (c + 1) * CHUNK)
            s_out_ref[c] = S
            o, S, t_inv = _delta_chunk(S, q_ref[rows, :], k_ref[rows, :], v_ref[rows, :], bg_ref[rows, :])
            o_ref[rows, :] = o
            t_out_ref[c] = t_inv
        s_ref[...] = S

    return pl.pallas_call(
        body, name=name, grid=(nc // G,),
        in_specs=[_row_spec(R, D_KEY_B, 0)] * 3 + [_row_spec(R, 128, 0)],
        out_specs=[_row_spec(R, D_KEY_B, 0), pl.BlockSpec((G, H, HEAD_DIM, HEAD_DIM), lambda i: (i, 0, 0, 0)),
                   pl.BlockSpec((G, H, CHUNK, CHUNK), lambda i: (i, 0, 0, 0))],
        out_shape=[jax.ShapeDtypeStruct((T, D_KEY_B), F32), jax.ShapeDtypeStruct((nc, H, HEAD_DIM, HEAD_DIM), F32),
                   jax.ShapeDtypeStruct((nc, H, CHUNK, CHUNK), F32)],
        scratch_shapes=[pltpu.VMEM((H, HEAD_DIM, HEAD_DIM), F32)],
        compiler_params=_ARB,
    )(q, k, v, bg)


def _delta_bwd(name, q, k, v, bg, s_all, t_all, do, T):
    nc = T // CHUNK
    H = N_HEADS_B
    G, R = CHUNKS_BWD, CHUNKS_BWD * CHUNK
    ns = nc // G

    def body(q_ref, k_ref, v_ref, bg_ref, s_in_ref, t_ref, do_ref, dq_ref, dk_ref, dv_ref, dbg_ref, ds_ref):
        @pl.when(pl.program_id(0) == 0)
        def _():
            ds_ref[...] = jnp.zeros_like(ds_ref)

        dS = ds_ref[...]
        for c in reversed(range(G)):
            rows = slice(c * CHUNK, (c + 1) * CHUNK)
            t_inv = t_ref[c]

            def f(S, qq, kk, vv, bb, t_inv=t_inv):
                return _delta_chunk(S, qq, kk, vv, bb, t_inv)[:2]

            _, vjp = jax.vjp(f, s_in_ref[c], q_ref[rows, :], k_ref[rows, :], v_ref[rows, :], bg_ref[rows, :])
            dS, dq, dk, dv, dbg = vjp((do_ref[rows, :], dS))
            dq_ref[rows, :] = dq
            dk_ref[rows, :] = dk
            dv_ref[rows, :] = dv
            dbg_ref[rows, :] = dbg
        ds_ref[...] = dS

    def rev(w):
        return pl.BlockSpec((R, w), lambda s: (ns - 1 - s, 0))

    def rev4(n):
        return pl.BlockSpec((G, H, n, n), lambda s: (ns - 1 - s, 0, 0, 0))

    return pl.pallas_call(
        body, name=name, grid=(ns,),
        in_specs=[rev(D_KEY_B)] * 3 + [rev(128), rev4(HEAD_DIM), rev4(CHUNK), rev(D_KEY_B)],
        out_specs=[rev(D_KEY_B)] * 3 + [rev(128)],
        out_shape=[jax.ShapeDtypeStruct((T, D_KEY_B), F32)] * 3 + [jax.ShapeDtypeStruct((T, 128), F32)],
        scratch_shapes=[pltpu.VMEM((H, HEAD_DIM, HEAD_DIM), F32)],
        compiler_params=_ARB,
    )(q, k, v, bg, s_all, t_all, do)


def _loss_fb(name, h2, w, tgt, tm, T):
    def tile_loss(h, wv, t):
        y = _f_rms(h, wv)[0]
        e = y - t
        return 0.5 * jnp.sum(jnp.mean(e * e, axis=-1))

    def body(h_ref, w_ref, t_ref, dh_ref, loss_ref, dw_ref):
        @pl.when(pl.program_id(0) == 0)
        def _():
            loss_ref[...] = jnp.zeros_like(loss_ref)
            dw_ref[...] = jnp.zeros_like(dw_ref)

        val, (dh, dw) = jax.value_and_grad(tile_loss, argnums=(0, 1))(h_ref[...], w_ref[...], t_ref[...])
        dh_ref[...] = dh
        dw_ref[...] += dw
        loss_ref[...] += jnp.reshape(val, (1, 1))

    return pl.pallas_call(
        body, name=name, grid=(T // tm,),
        in_specs=[_row_spec(tm, D_MODEL, 0), _full_spec((1, D_MODEL)), _row_spec(tm, D_MODEL, 0)],
        out_specs=[_row_spec(tm, D_MODEL, 0), _full_spec((1, 1)), _full_spec((1, D_MODEL))],
        out_shape=[jax.ShapeDtypeStruct((T, D_MODEL), F32), jax.ShapeDtypeStruct((1, 1), F32),
                   jax.ShapeDtypeStruct((1, D_MODEL), F32)],
        compiler_params=_ARB,
    )(h2, w, tgt)


def _adamw(name, w, g, m, v, tile):
    R = w.shape[0]

    def body(w_ref, g_ref, m_ref, v_ref, d_out, m_out, v_out):
        g = g_ref[...]
        mm = ADAM_B1 * m_ref[...] + (1.0 - ADAM_B1) * g
        vv = ADAM_B2 * v_ref[...] + (1.0 - ADAM_B2) * jnp.square(g)
        m_hat = mm / (1.0 - ADAM_B1 ** ADAM_STEP)
        v_hat = vv / (1.0 - ADAM_B2 ** ADAM_STEP)
        d_out[...] = -ADAM_LR * (m_hat / (jnp.sqrt(v_hat) + ADAM_EPS) + ADAM_WD * w_ref[...])
        m_out[...] = mm
        v_out[...] = vv

    assert R % tile == 0
    spec = pl.BlockSpec((tile, w.shape[1]), lambda i: (i, 0))
    return pl.pallas_call(
        body, name=name, grid=(R // tile,), in_specs=[spec] * 4, out_specs=[spec] * 3,
        out_shape=[jax.ShapeDtypeStruct(w.shape, F32)] * 3, compiler_params=_ARB,
    )(w, g, m, v)


def _sum4(name, r, tile):
    _, R, C = r.shape
    assert R % tile == 0

    def body(r_ref, o_ref):
        o_ref[...] = ((r_ref[0].astype(F32) + r_ref[1].astype(F32)) + r_ref[2].astype(F32)) + r_ref[3].astype(F32)

    return pl.pallas_call(
        body, name=name, grid=(R // tile,),
        in_specs=[pl.BlockSpec((4, tile, C), lambda i: (0, i, 0))],
        out_specs=pl.BlockSpec((tile, C), lambda i: (i, 0)),
        out_shape=jax.ShapeDtypeStruct((R, C), F32), compiler_params=_ARB,
    )(r)


def _add2(name, a, b, tile):
    R, C = a.shape
    assert R % tile == 0

    def body(a_ref, b_ref, o_ref):
        o_ref[...] = (a_ref[...].astype(F32) + b_ref[...].astype(F32)).astype(BF16)

    spec = pl.BlockSpec((tile, C), lambda i: (i, 0))
    return pl.pallas_call(
        body, name=name, grid=(R // tile,), in_specs=[spec, spec], out_specs=spec,
        out_shape=jax.ShapeDtypeStruct((R, C), BF16), compiler_params=_ARB,
    )(a, b)


_ANY = pl.BlockSpec(memory_space=pl.ANY)


def _place():
    return lax.axis_index("x"), lax.axis_index("y"), lax.axis_index("c")


def _chip_exchange(in_refs, out_refs, send_sems, recv_sems, local_sems, gather):
    n = len(in_refs)
    x, y, c = _place()
    me = 2 * x + y
    chips = [(1 - x, y), (x, 1 - y), (1 - x, 1 - y)]

    def src(a, slot):
        return in_refs[a] if gather else in_refs[a].at[slot]

    def copy(a, k, src_slot, dst_slot, px, py):
        return pltpu.make_async_remote_copy(src_ref=src(a, src_slot), dst_ref=out_refs[a].at[dst_slot],
                                            send_sem=send_sems.at[3 * a + k], recv_sem=recv_sems.at[3 * a + k],
                                            device_id=(px, py, c), device_id_type=MESH)

    def mine():
        return [pltpu.make_async_copy(src(a, me), out_refs[a].at[me], local_sems.at[a]) for a in range(n)]

    def sends():
        return [copy(a, k, 2 * px + py, me, px, py) for a in range(n) for k, (px, py) in enumerate(chips)]

    def start():
        for cp in mine() + sends():
            cp.start()

    def finish():
        for a in range(n):
            for k, (px, py) in enumerate(chips):
                copy(a, k, me, 2 * px + py, px, py).wait_recv()
        for cp in sends():
            cp.wait_send()
        for cp in mine():
            cp.wait()

    return start, finish


def _chip_exchange_shapes(arrays, gather):
    return [jax.ShapeDtypeStruct(((4,) + a.shape) if gather else a.shape, a.dtype) for a in arrays]


def _chip_exchange_sems(n):
    return [pltpu.SemaphoreType.DMA((3 * n,)), pltpu.SemaphoreType.DMA((3 * n,)), pltpu.SemaphoreType.DMA((n,))]


def _exchange_chips(name, arrays, gather):
    n = len(arrays)

    def body(*refs):
        start, finish = _chip_exchange(refs[:n], refs[n:2 * n], *refs[2 * n:], gather=gather)
        start()
        finish()

    return pl.pallas_call(
        body, name=name, in_specs=[_ANY] * n, out_specs=[_ANY] * n,
        out_shape=_chip_exchange_shapes(arrays, gather), scratch_shapes=_chip_exchange_sems(n),
    )(*arrays)


def _swap_sibling(name, arrays):
    n = len(arrays)

    def body(*refs):
        in_refs, out_refs, send_sems, recv_sems = refs[:n], refs[n:2 * n], refs[2 * n], refs[2 * n + 1]
        x, y, c = _place()
        cps = [pltpu.make_async_remote_copy(src_ref=in_refs[a], dst_ref=out_refs[a], send_sem=send_sems.at[a],
                                            recv_sem=recv_sems.at[a], device_id=(x, y, 1 - c), device_id_type=MESH)
               for a in range(n)]
        for cp in cps:
            cp.start()
        for cp in cps:
            cp.wait()

    return pl.pallas_call(
        body, name=name, in_specs=[_ANY] * n, out_specs=[_ANY] * n,
        out_shape=[jax.ShapeDtypeStruct(a.shape, a.dtype) for a in arrays],
        scratch_shapes=[pltpu.SemaphoreType.DMA((n,)), pltpu.SemaphoreType.DMA((n,))],
    )(*arrays)


def _allsum_small(name, pack):
    def body(p_ref, out_ref, slots, send_sems, recv_sems):
        x, y, c = _place()
        me = 4 * x + 2 * y + c
        slots[me] = p_ref[...]
        sends = []
        for k in range(1, 8):
            px = 1 - x if k & 4 else x
            py = 1 - y if k & 2 else y
            pc = 1 - c if k & 1 else c
            cp = pltpu.make_async_remote_copy(src_ref=p_ref, dst_ref=slots.at[me], send_sem=send_sems.at[k - 1],
                                              recv_sem=recv_sems.at[k - 1], device_id=(px, py, pc), device_id_type=MESH)
            cp.start()
            sends.append(cp)
        for k in range(1, 8):
            px = 1 - x if k & 4 else x
            py = 1 - y if k & 2 else y
            pc = 1 - c if k & 1 else c
            pltpu.make_async_remote_copy(src_ref=p_ref, dst_ref=slots.at[4 * px + 2 * py + pc],
                                         send_sem=send_sems.at[k - 1], recv_sem=recv_sems.at[k - 1],
                                         device_id=(px, py, pc), device_id_type=MESH).wait_recv()
        for cp in sends:
            cp.wait_send()
        acc = slots[0]
        for s in range(1, 8):
            acc = acc + slots[s]
        out_ref[...] = acc

    vm = pl.BlockSpec(memory_space=pltpu.VMEM)
    return pl.pallas_call(
        body, name=name, in_specs=[vm], out_specs=vm,
        out_shape=jax.ShapeDtypeStruct(pack.shape, pack.dtype),
        scratch_shapes=[pltpu.VMEM((8,) + pack.shape, pack.dtype), pltpu.SemaphoreType.DMA((7,)),
                        pltpu.SemaphoreType.DMA((7,))],
    )(pack)


def _pack_shards(parts, rows_total):
    flat = [p.reshape(-1, D_MODEL) for p in parts]
    used = sum(f.shape[0] for f in flat)
    return jnp.concatenate(flat + [jnp.zeros((rows_total - used, D_MODEL), flat[0].dtype)], axis=0)


def _unpack_shards(buf, shapes):
    out, r0 = [], 0
    for shp in shapes:
        n = math.prod(shp) // D_MODEL
        out.append(buf[r0:r0 + n].reshape(shp))
        r0 += n
    return out


def kernel(x, norm_mix, w_in, conv_w, a_log, dt_bias, dn_norm, w_proj_attn, w_proj_delta, w_out, norm_ffn, w_gate, w_up, w_down, norm_final, loss_target, m_norm_mix, m_w_in, m_conv_w, m_a_log, m_dt_bias, m_dn_norm, m_w_proj_attn, m_w_proj_delta, m_w_out, m_norm_ffn, m_w_gate, m_w_up, m_w_down, m_norm_final, v_norm_mix, v_w_in, v_conv_w, v_a_log, v_dt_bias, v_dn_norm, v_w_proj_attn, v_w_proj_delta, v_w_out, v_norm_ffn, v_w_gate, v_w_up, v_w_down, v_norm_final):
    T = x.shape[1]
    D = D_MODEL
    tm = 512
    tw = 256
    mt = 1024 if T % 1024 == 0 else 512
    x2 = x.reshape(T, D)
    tgt = loss_target.reshape(T, D)

    packed = (w_proj_attn, w_proj_delta, w_out, w_gate, w_up, w_down)
    conv_bits = lax.bitcast_convert_type(conv_w[0], BF16)
    w16 = _pack_shards([w[0].astype(BF16) for w in packed] + [conv_bits], PACK_R)
    win16 = w_in[0].astype(BF16)
    my_c = lax.axis_index("c")

    def both_halves(mine, other, axis):
        return jnp.concatenate([jnp.where(my_c == 0, mine, other), jnp.where(my_c == 0, other, mine)], axis=axis)

    pack_half = lax.dynamic_slice_in_dim(w16, my_c * HALF, HALF, axis=0)
    (win_mine,) = _exchange_chips("gather_w_in", [lax.dynamic_slice_in_dim(win16, my_c * W_IN_HALF, W_IN_HALF, axis=0)], True)
    (win_other,) = _swap_sibling("swap_w_in", [win_mine])
    win4 = both_halves(win_mine, win_other, 1)

    def win_cols(g0, g1):
        out = []
        for j in range(4):
            lo, hi = max(g0, W_IN_COLS * j), min(g1, W_IN_COLS * (j + 1))
            if lo < hi:
                out.append(win4[j][:, lo - W_IN_COLS * j:hi - W_IN_COLS * j])
        return out

    Wa = jnp.concatenate([p for c0, w in _att_columns() for p in win_cols(c0, c0 + w)], axis=1)
    Wb = jnp.concatenate(win_cols(4608, 8704) + win_cols(8720, D_IN), axis=1)
    Wba = jnp.pad(jnp.concatenate(win_cols(8704, 8720), axis=1), ((0, 0), (0, 112)))
    alog_v = jnp.pad(a_log, ((0, 0), (8, 112)))
    dtb_v = jnp.pad(dt_bias, ((0, 0), (8, 112)))

    (u,) = _rowwise("rms_mix", _f_rms, [(x2, 0, D)], [norm_mix], [(D, BF16)], tm, T)
    proj_a, pack_mine = _matmul("mm_in_a", u, Wa, "nn", T, D_PA, D, mt, 2304, D, F32, side=([pack_half], True))
    (pack_other,) = _swap_sibling("swap_pack", [pack_mine])
    wg = both_halves(pack_mine, pack_other, 1)

    def seg(r0, n):
        return wg[:, r0:r0 + n, :]

    r = 0
    Wpa = seg(r, 128).reshape(4, D_ATTN_OUT, 256).transpose(1, 0, 2).reshape(D_ATTN_OUT, D); r += 128
    Wpd = seg(r, 256).reshape(D, D); r += 256
    Wout = seg(r, 256).reshape(D, D); r += 256
    Wg = seg(r, 704).reshape(4, D, 704).transpose(1, 0, 2).reshape(D, D_FF); r += 704
    Wu = seg(r, 704).reshape(4, D, 704).transpose(1, 0, 2).reshape(D, D_FF); r += 704
    Wd = seg(r, 704).reshape(D_FF, D); r += 704
    convw = lax.bitcast_convert_type(seg(r, 6).reshape(4, CONV_WIDTH, 768, 2), F32)
    convw = convw.transpose(1, 0, 2).reshape(CONV_WIDTH, 3 * D_KEY_B)
    Wgu = jnp.concatenate([Wg, Wu], axis=1)
    proj_b = _matmul("mm_in_b", u, Wb, "nn", T, D_PB, D, mt, 3072, D, BF16)
    ba = _matmul("mm_ba", u, Wba, "nn", T, 128, D, mt, 128, D, F32)

    att = [_attn_fwd(f"attn_fwd{g}", proj_a, g, T) for g in range(3)]
    (ya_m,) = _rowwise("merge", _f_merge, [(att[g][0], 0, 512) for g in range(3)] + [(att[g][1], 0, 512) for g in range(3)],
                       [], [(512, BF16)], tm, T)
    qn, kn, vd, bg = _dn_pre_fwd("dn_pre_fwd", proj_b, ba, convw, alog_v, dtb_v, tw, T)
    o_d, s_all, t_all = _delta_fwd("delta_fwd", qn, kn, vd, bg, T)
    (od,) = _rowwise("post_dn", _f_postdn, [(o_d, 0, D), (proj_b, C_Z // D, D)], [dn_norm], [(D, BF16)], tm, T)
    y_a = _matmul("mm_pa", ya_m, Wpa, "nn", T, D, 512, mt, D, 512, BF16)
    y_b = _matmul("mm_pd", od, Wpd, "nn", T, D, D, mt, D, D, BF16)
    (mix,) = _rowwise("mix", _f_mix, [(y_a, 0, D), (y_b, 0, D), (proj_b, C_GATE // 2048, 2048)], [], [(D, BF16)], tm, T)
    h1 = _matmul("mm_out", mix, Wout, "nn", T, D, D, mt, D, D, F32, residual=x2)
    (hn,) = _rowwise("rms_ffn", _f_rms, [(h1, 0, D)], [norm_ffn], [(D, BF16)], tm, T)
    gu = _matmul("mm_gu", hn, Wgu, "nn", T, 2 * D_FF, D, mt, 2816, D, BF16)
    h2 = _matmul("mm_down", gu, Wd, "nn", T, D, D_FF, mt, D, 1408, F32, residual=h1, a_swiglu=D_FF)
    dh2, loss, g_norm_final = _loss_fb("loss", h2, norm_final.reshape(1, D), tgt, tm, T)

    dact = _matmul("mm_dact", dh2, Wd, "nt", T, D_FF, D, mt, 1408, D, BF16)
    gWd = _matmul("mm_gwd", gu, dh2, "tn", D_FF, D, T, 1408, D, mt, BF16, a_swiglu=D_FF)
    (dgu,) = _rowwise_vjp("act_bwd", _f_act, [(gu, 0, 2 * D_FF)], [], [(dact, 0, D_FF)], [BF16], tw, T)
    dhn = _matmul("mm_dhn", dgu, Wgu, "nt", T, D, 2 * D_FF, mt, D, 2816, F32)
    gWgu = _matmul("mm_gwgu", hn, dgu, "tn", D, 2 * D_FF, T, D, 1408, 2 * mt, BF16)
    dh1, g_norm_ffn = _rowwise_vjp("rms_ffn_bwd", _f_rms, [(h1, 0, D)], [norm_ffn], [(dhn, 0, D)], [F32], tm, T,
                                   residual=(dh2, 0, D))
    dmix = _matmul("mm_dmix", dh1, Wout, "nt", T, D, D, mt, D, D, BF16)
    gWout = _matmul("mm_gwout", mix, dh1, "tn", D, D, T, D, D, mt, BF16)
    dya, dyb, dgate = _rowwise_vjp("mix_bwd", _f_mix, [(y_a, 0, D), (y_b, 0, D), (proj_b, C_GATE // 2048, 2048)], [],
                                   [(dmix, 0, D)], [BF16, BF16, BF16], tm, T)
    dya_m = _matmul("mm_dyam", dya, Wpa, "nt", T, 512, D, mt, 512, D, BF16)
    gWpa = _matmul("mm_gwpa", ya_m, dya, "tn", 512, D, T, 512, D, mt, BF16)
    dod = _matmul("mm_dod", dyb, Wpd, "nt", T, D, D, mt, D, D, BF16)
    gWpd = _matmul("mm_gwpd", od, dyb, "tn", D, D, T, D, D, mt, BF16)
    do_d, dz, g_dn_norm = _rowwise_vjp("post_dn_bwd", _f_postdn, [(o_d, 0, D), (proj_b, C_Z // D, D)], [dn_norm],
                                       [(dod, 0, D)], [F32, BF16], tm, T)
    dqn, dkn, dvd, dbg = _delta_bwd("delta_bwd", qn, kn, vd, bg, s_all, t_all, do_d, T)
    dqkvd, dba, g_conv8, g_alog_v, g_dtb_v = _dn_pre_bwd("dn_pre_bwd", proj_b, ba, convw, alog_v, dtb_v,
                                                         dqn, dkn, dvd, dbg, tw, T)
    def pair_sum(tag, g, h):
        keep = lax.dynamic_slice_in_dim(g, my_c * h, h, axis=1)
        give = lax.dynamic_slice_in_dim(g, (1 - my_c) * h, h, axis=1)
        (got,) = _swap_sibling("swap_grads_" + tag, [give])
        return _add2("add_cores_" + tag, keep.reshape(-1, g.shape[2]), got.reshape(-1, g.shape[2]), HALF_TILE).reshape(keep.shape)

    def cols4(g, n):
        return g.reshape(g.shape[0], 4, n).transpose(1, 0, 2).reshape(4, -1, D)

    parts = [cols4(gWpa, 256), gWpd.reshape(4, 256, D), gWout.reshape(4, 256, D), cols4(gWgu[:, :D_FF], 704),
             cols4(gWgu[:, D_FF:], 704), gWd.reshape(4, 704, D), cols4(g_conv8[:CONV_WIDTH], 768)]
    used = sum(p.shape[1] for p in parts)
    g_pack = jnp.concatenate([p.astype(BF16) for p in parts] + [jnp.zeros((4, PACK_R - used, D), BF16)], axis=1)
    pair_pack = pair_sum("pack", g_pack, HALF)

    mg = _rowwise_vjp("merge_bwd", _f_merge,
                      [(att[g][0], 0, 512) for g in range(3)] + [(att[g][1], 0, 512) for g in range(3)], [],
                      [(dya_m, 0, 512)], [F32] * 6, tm, T)
    dproj_a = None
    for g in range(3):
        dproj_a = _attn_bwd(f"attn_bwd{g}", proj_a, att[g][0], att[g][1], mg[g], mg[3 + g], g, T, dproj_a)
    dproj_b = [dqkvd, dz, dgate]
    gWa, recv_pack = _matmul("mm_gwa", u, dproj_a, "tn", D, D_PA, T, D, 1536, 2 * mt, BF16, side=([pair_pack], False))
    gWb = _matmul_pieces("mm_gwb", "tn", dproj_b, u, D, D_PB, T, D, 1024, 2 * mt, BF16)
    gWba = _matmul("mm_gwba", u, dba, "tn", D, 128, T, D, 128, mt, BF16)

    pieces, off = [(gWb[:, :C_GATE], D_PA), (gWba[:, :16], 8704), (gWb[:, C_GATE:], 8720)], 0
    for c0, w in _att_columns():
        pieces.append((gWa[:, off:off + w], c0))
        off += w
    pieces.sort(key=lambda t: t[1])

    def win_shard(j):
        out = []
        for arr, g0 in pieces:
            lo, hi = max(g0, W_IN_COLS * j), min(g0 + arr.shape[1], W_IN_COLS * (j + 1))
            if lo < hi:
                out.append(arr[:, lo - g0:hi - g0])
        return jnp.concatenate(out, axis=1)

    g_win4 = jnp.concatenate([win_shard(j)[None] for j in range(4)], axis=0)
    pair_win = pair_sum("w_in", g_win4, W_IN_HALF)

    du0 = _matmul("mm_du_ba", dba, Wba, "nt", T, D, 128, mt, D, 128, F32)
    du1 = _matmul_pieces("mm_du_b", "nt", dproj_b, Wb, T, D, D_PB, mt, D, 1024, F32, residual=du0)
    du, recv_win = _matmul("mm_du_a", dproj_a, Wa, "nt", T, D, D_PA, mt, D, 2304, F32, residual=du1, side=([pair_win], False))
    dx, g_norm_mix = _rowwise_vjp("rms_mix_bwd", _f_rms, [(x2, 0, D)], [norm_mix], [(du, 0, D)], [F32], tm, T,
                                  residual=(dh1, 0, D))

    g_mine = [_sum4("sum_chips_pack", recv_pack, HALF_TILE), _sum4("sum_chips_w_in", recv_win, HALF_TILE)]
    g_other = _swap_sibling("swap_sums", g_mine)
    g_all = both_halves(g_mine[0], g_other[0], 0)
    g_win = both_halves(g_mine[1], g_other[1], 0)

    locals_ = (w_proj_attn, w_proj_delta, w_out, w_gate, w_up, w_down, conv_w, w_in)
    ms = (m_w_proj_attn, m_w_proj_delta, m_w_out, m_w_gate, m_w_up, m_w_down, m_conv_w, m_w_in)
    vs = (v_w_proj_attn, v_w_proj_delta, v_w_out, v_w_gate, v_w_up, v_w_down, v_conv_w, v_w_in)
    big_names = ("w_proj_attn", "w_proj_delta", "w_out", "w_gate", "w_up", "w_down", "conv_w", "w_in")
    adam_tiles = (512, 256, 256, 256, 256, 352, CONV_WIDTH, 128)
    g_shards = _unpack_shards(g_all, [w.shape for w in locals_[:-1]]) + [g_win.reshape(w_in.shape)]
    big = [g_shards, [], [], []]
    for nm, w, g, m_, v_, tl in zip(big_names, locals_, g_shards, ms, vs, adam_tiles):
        res = _adamw("adamw_" + nm, w[0], g[0], m_[0], v_[0], tl)
        for kind in range(3):
            big[kind + 1].append(res[kind].reshape(w.shape))

    def small_pack(nm, nf, nfin, dn, al, dt, extra):
        row3 = jnp.concatenate([dn.reshape(1, 128), al.reshape(1, 8), dt.reshape(1, 8), extra.reshape(1, 1),
                                jnp.zeros((1, D - 145), F32)], axis=1)
        return jnp.concatenate([nm.reshape(1, D), nf.reshape(1, D), nfin.reshape(1, D), row3, jnp.zeros((4, D), F32)], axis=0)

    zero1 = jnp.zeros((1, 1), F32)
    sp = small_pack(g_norm_mix, g_norm_ffn, g_norm_final, g_dn_norm, g_alog_v[:, 8:16], g_dtb_v[:, 8:16], loss)
    tot = _allsum_small("allsum_small", sp)
    sm = _adamw("adamw_small", small_pack(norm_mix, norm_ffn, norm_final, dn_norm, a_log, dt_bias, zero1), tot,
                small_pack(m_norm_mix, m_norm_ffn, m_norm_final, m_dn_norm, m_a_log, m_dt_bias, zero1),
                small_pack(v_norm_mix, v_norm_ffn, v_norm_final, v_dn_norm, v_a_log, v_dt_bias, zero1), 8)
    sm = [tot] + list(sm)

    def small_unpack(p):
        return {"norm_mix": p[0:1], "norm_ffn": p[1:2], "norm_final": p[2], "dn_norm": p[3:4, :128],
                "a_log": p[3:4, 128:136], "dt_bias": p[3:4, 136:144]}

    sm = [small_unpack(p) for p in sm]
    loss_out = tot[3, 144]

    order = ("norm_mix", "w_in", "conv_w", "a_log", "dt_bias", "dn_norm", "w_proj_attn", "w_proj_delta", "w_out",
             "norm_ffn", "w_gate", "w_up", "w_down", "norm_final")
    outs = [loss_out, dx.reshape(x.shape)]
    for kind in range(4):
        for n in order:
            outs.append(big[kind][big_names.index(n)] if n in big_names else sm[kind][n])
    return tuple(outs)
```

```python
import functools
import math

import jax
import jax.numpy as jnp
from jax import lax
from jax.experimental import pallas as pl
from jax.experimental.pallas import tpu as pltpu

F32 = jnp.float32
BF16 = jnp.bfloat16
MESH = pl.DeviceIdType.MESH

D_MODEL = 1024
N_HEADS_A = 12
HEAD_DIM = 128
BLOCK_A = 128
DILATIONS = (1, 4, 16)
W_SUB = 128
D_ATTN_OUT = 512
N_HEADS_B = 8
D_KEY_B = 1024
CONV_WIDTH = 4
CHUNK = 64
CHUNKS_FWD, CHUNKS_BWD = 2, 1
D_FF = 2816
EPS = 1e-6
D_IN = 10768
D_PA = 4608
D_PB = 6144
C_Z, C_GATE = 3072, 4096

ADAM_LR, ADAM_B1, ADAM_B2, ADAM_EPS, ADAM_WD, ADAM_STEP = 0.001, 0.9, 0.999, 1e-08, 0.01, 10

PACK_R = 2816
HALF = PACK_R // 2
HALF_TILE = 128
W_IN_COLS = 2692
W_IN_HALF = D_MODEL // 2

VMEM_LIMIT = 48 * 1024 * 1024


def _cparams(n):
    return pltpu.CompilerParams(dimension_semantics=("arbitrary",) * n, vmem_limit_bytes=VMEM_LIMIT)


_ARB = _cparams(1)


def _row_spec(tm, w, cb):
    return pl.BlockSpec((tm, w), lambda i: (i, cb))


def _full_spec(shape):
    nd = len(shape)
    return pl.BlockSpec(shape, lambda i: (0,) * nd)


def _rowwise(name, fn, ins, params, outs, tm, T, side=None):
    n_in, n_p, n_out = len(ins), len(params), len(outs)
    n_side = 0 if side is None else len(side[0])
    steps = T // tm

    def body(*refs):
        k0 = n_in + n_p + n_side
        if n_side:
            start, finish = _chip_exchange(refs[n_in + n_p:k0], refs[k0 + n_out:k0 + n_out + n_side],
                                           *refs[k0 + n_out + n_side:], gather=side[1])
            pl.when(pl.program_id(0) == 0)(start)
        xs = [r[...].astype(F32) for r in refs[:n_in]] + [r[...] for r in refs[n_in:n_in + n_p]]
        res = fn(*xs)
        for o_ref, v in zip(refs[k0:k0 + n_out], res):
            o_ref[...] = v.astype(o_ref.dtype)
        if n_side:
            pl.when(pl.program_id(0) == steps - 1)(finish)

    side_arrays = [] if side is None else list(side[0])
    return pl.pallas_call(
        body, name=name, grid=(steps,),
        in_specs=([_row_spec(tm, w, cb) for (_, cb, w) in ins] + [_full_spec(p.shape) for p in params] + [_ANY] * n_side),
        out_specs=[_row_spec(tm, w, 0) for (w, _) in outs] + [_ANY] * n_side,
        out_shape=([jax.ShapeDtypeStruct((T, w), dt) for (w, dt) in outs]
                   + ([] if side is None else _chip_exchange_shapes(*side))),
        scratch_shapes=_chip_exchange_sems(n_side) if n_side else [],
        compiler_params=_ARB,
    )(*[a for (a, _, _) in ins], *params, *side_arrays)


def _rowwise_vjp(name, fn, ins, params, cts, gdtypes, tm, T, residual=None):
    n_in, n_p, n_ct = len(ins), len(params), len(cts)
    n_res = 0 if residual is None else 1

    def body(*refs):
        i = pl.program_id(0)
        k = 0
        xs = [r[...].astype(F32) for r in refs[k:k + n_in]]
        k += n_in
        ps = [r[...] for r in refs[k:k + n_p]]
        k += n_p
        gs = tuple(r[...].astype(F32) for r in refs[k:k + n_ct])
        k += n_ct
        res_refs = refs[k:k + n_res]
        k += n_res
        gi_refs = refs[k:k + n_in]
        gp_refs = refs[k + n_in:]
        _, vjp = jax.vjp(fn, *xs, *ps)
        g = vjp(gs)
        for j, (ref, v) in enumerate(zip(gi_refs, g[:n_in])):
            if j == 0 and n_res:
                v = v + res_refs[0][...].astype(F32)
            ref[...] = v.astype(ref.dtype)

        @pl.when(i == 0)
        def _():
            for ref in gp_refs:
                ref[...] = jnp.zeros_like(ref)

        for ref, v in zip(gp_refs, g[n_in:]):
            ref[...] += v

    res_in = [] if residual is None else [residual]
    return pl.pallas_call(
        body, name=name, grid=(T // tm,),
        in_specs=([_row_spec(tm, w, cb) for (_, cb, w) in ins] + [_full_spec(p.shape) for p in params]
                  + [_row_spec(tm, w, cb) for (_, cb, w) in cts] + [_row_spec(tm, w, cb) for (_, cb, w) in res_in]),
        out_specs=[_row_spec(tm, w, 0) for (_, _, w) in ins] + [_full_spec(p.shape) for p in params],
        out_shape=([jax.ShapeDtypeStruct((T, w), dt) for (_, _, w), dt in zip(ins, gdtypes)]
                   + [jax.ShapeDtypeStruct(p.shape, F32) for p in params]),
        compiler_params=_ARB,
    )(*[a for (a, _, _) in ins], *params, *[a for (a, _, _) in cts], *[a for (a, _, _) in res_in])


def _matmul(name, a, b, mode, M, N, K, tm, tn, tk, out_dtype, a_off=0, b_off=0, residual=None, side=None, a_swiglu=None):
    nk = K // tk
    n_side = 0 if side is None else len(side[0])
    assert M % tm == 0 and N % tn == 0 and K % tk == 0
    if mode == "nn":
        assert a_off % tk == 0 and b_off % tn == 0
        a_spec = pl.BlockSpec((tm, tk), lambda i, j, k: (i, a_off // tk + k))
        b_spec = pl.BlockSpec((tk, tn), lambda i, j, k: (k, b_off // tn + j))
        dims = (((1,), (0,)), ((), ()))
    elif mode == "nt":
        assert a_off % tk == 0 and b_off == 0
        a_spec = pl.BlockSpec((tm, tk), lambda i, j, k: (i, a_off // tk + k))
        b_spec = pl.BlockSpec((tn, tk), lambda i, j, k: (j, k))
        dims = (((1,), (1,)), ((), ()))
    else:
        assert a_off % tm == 0 and b_off % tn == 0
        a_spec = pl.BlockSpec((tk, tm), lambda i, j, k: (k, a_off // tm + i))
        b_spec = pl.BlockSpec((tk, tn), lambda i, j, k: (k, b_off // tn + j))
        dims = (((0,), (0,)), ((), ()))
    has_res = residual is not None
    has_pair = a_swiglu is not None
    if has_pair:
        blk = tm if mode == "tn" else tk
        assert mode in ("nn", "tn") and a_swiglu % blk == 0
        shift = a_swiglu // blk
        a2_spec = (pl.BlockSpec((tm, tk), lambda i, j, k: (i, a_off // tk + shift + k)) if mode == "nn" else
                   pl.BlockSpec((tk, tm), lambda i, j, k: (k, a_off // tm + shift + i)))

    n_main = 2 + has_res + has_pair
    grid = (M // tm, N // tn, nk)

    def a_tile(refs):
        if not has_pair:
            return refs[0][...].astype(BF16)
        g, up = refs[0][...].astype(F32), refs[n_main - 1][...].astype(F32)
        return (_silu(g) * up).astype(BF16)

    def body(*refs):
        b_ref = refs[1]
        r_ref = refs[2] if has_res else None
        o_ref = refs[n_main + n_side]
        acc_ref = refs[n_main + 2 * n_side + 1]
        i, j, k = pl.program_id(0), pl.program_id(1), pl.program_id(2)
        if n_side:
            start, finish = _chip_exchange(refs[n_main:n_main + n_side], refs[n_main + n_side + 1:n_main + 2 * n_side + 1],
                                           *refs[n_main + 2 * n_side + 2:], gather=side[1])
            pl.when((i == 0) & (j == 0) & (k == 0))(start)

        @pl.when(k == 0)
        def _():
            acc_ref[...] = jnp.zeros_like(acc_ref)

        acc_ref[...] += lax.dot_general(a_tile(refs), b_ref[...].astype(BF16), dims, preferred_element_type=F32)

        @pl.when(k == nk - 1)
        def _():
            acc = acc_ref[...]
            if has_res:
                acc = acc + r_ref[...].astype(F32)
            o_ref[...] = acc.astype(o_ref.dtype)

        if n_side:
            pl.when((i == grid[0] - 1) & (j == grid[1] - 1) & (k == nk - 1))(finish)

    in_specs = [a_spec, b_spec]
    args = [a, b]
    if has_res:
        in_specs.append(pl.BlockSpec((tm, tn), lambda i, j, k: (i, j)))
        args.append(residual)
    if has_pair:
        in_specs.append(a2_spec)
        args.append(a)
    out_specs = pl.BlockSpec((tm, tn), lambda i, j, k: (i, j))
    out_shape = jax.ShapeDtypeStruct((M, N), out_dtype)
    scratch = [pltpu.VMEM((tm, tn), F32)]
    semantics = ("parallel", "parallel", "arbitrary")
    if n_side:
        in_specs += [_ANY] * n_side
        args += list(side[0])
        out_specs = [out_specs] + [_ANY] * n_side
        out_shape = [out_shape] + _chip_exchange_shapes(*side)
        scratch += _chip_exchange_sems(n_side)
        semantics = ("arbitrary",) * 3
    return pl.pallas_call(
        body, name=name, grid=grid, in_specs=in_specs, out_specs=out_specs, out_shape=out_shape, scratch_shapes=scratch,
        compiler_params=pltpu.CompilerParams(dimension_semantics=semantics, vmem_limit_bytes=VMEM_LIMIT),
    )(*args)


def _matmul_pieces(name, mode, pieces, other, M, N, K, tm, tn, tk, out_dtype, residual=None):
    nk = K // tk
    blk = tn if mode == "tn" else tk
    counts = [p.shape[1] // blk for p in pieces]
    assert all(p.shape[1] % blk == 0 for p in pieces) and sum(counts) == (N if mode == "tn" else K) // blk
    starts = [sum(counts[:p]) for p in range(len(pieces))]
    n_p = len(pieces)
    has_res = residual is not None

    def piece_spec(s, n):
        if mode == "tn":
            return pl.BlockSpec((tk, tn), lambda i, j, k: (jnp.where((j >= s) & (j < s + n), k, 0), jnp.clip(j - s, 0, n - 1)))
        return pl.BlockSpec((tm, tk), lambda i, j, k: (i, jnp.clip(k - s, 0, n - 1)))

    if mode == "tn":
        other_spec = pl.BlockSpec((tk, tm), lambda i, j, k: (k, i))
    else:
        other_spec = pl.BlockSpec((tn, tk), lambda i, j, k: (j, k))

    def body(*refs):
        other_ref, piece_refs = refs[0], refs[1:1 + n_p]
        r_ref = refs[1 + n_p] if has_res else None
        o_ref, acc_ref = refs[-2], refs[-1]
        k = pl.program_id(2)
        sel = pl.program_id(1) if mode == "tn" else k

        @pl.when(k == 0)
        def _():
            acc_ref[...] = jnp.zeros_like(acc_ref)

        for p_ref, s, n in zip(piece_refs, starts, counts):
            @pl.when((sel >= s) & (sel < s + n))
            def _(p_ref=p_ref):
                if mode == "tn":
                    acc_ref[...] += lax.dot_general(other_ref[...].astype(BF16), p_ref[...].astype(BF16),
                                                    (((0,), (0,)), ((), ())), preferred_element_type=F32)
                else:
                    acc_ref[...] += lax.dot_general(p_ref[...].astype(BF16), other_ref[...].astype(BF16),
                                                    (((1,), (1,)), ((), ())), preferred_element_type=F32)

        @pl.when(k == nk - 1)
        def _():
            acc = acc_ref[...]
            if has_res:
                acc = acc + r_ref[...].astype(F32)
            o_ref[...] = acc.astype(o_ref.dtype)

    in_specs = [other_spec] + [piece_spec(s, n) for s, n in zip(starts, counts)]
    args = [other] + list(pieces)
    if has_res:
        in_specs.append(pl.BlockSpec((tm, tn), lambda i, j, k: (i, j)))
        args.append(residual)
    return pl.pallas_call(
        body, name=name, grid=(M // tm, N // tn, nk),
        in_specs=in_specs,
        out_specs=pl.BlockSpec((tm, tn), lambda i, j, k: (i, j)),
        out_shape=jax.ShapeDtypeStruct((M, N), out_dtype),
        scratch_shapes=[pltpu.VMEM((tm, tn), F32)],
        compiler_params=pltpu.CompilerParams(dimension_semantics=("parallel", "parallel", "arbitrary"),
                                             vmem_limit_bytes=VMEM_LIMIT),
    )(*args)


def _f_rms(x, w):
    return (x * lax.rsqrt(jnp.mean(x * x, axis=-1, keepdims=True) + EPS) * w,)


def _silu(x):
    return x * jax.nn.sigmoid(x)


def _softplus(x):
    return jnp.maximum(x, 0.0) + jnp.log1p(jnp.exp(-jnp.abs(x)))


def _heads(x, n):
    return [x[:, HEAD_DIM * h:HEAD_DIM * (h + 1)] for h in range(n)]


def _f_dnpoint(c, ba, alog_v, dtb_v):
    cs = _silu(c)
    q, k, v = cs[:, :D_KEY_B], cs[:, D_KEY_B:2 * D_KEY_B], cs[:, 2 * D_KEY_B:]
    qn = jnp.concatenate([t * lax.rsqrt(jnp.sum(t * t, axis=-1, keepdims=True) + EPS) * (HEAD_DIM ** -0.5)
                          for t in _heads(q, N_HEADS_B)], axis=1)
    kn = jnp.concatenate([t * lax.rsqrt(jnp.sum(t * t, axis=-1, keepdims=True) + EPS)
                          for t in _heads(k, N_HEADS_B)], axis=1)
    lane = lax.broadcasted_iota(jnp.int32, ba.shape, 1)
    beta = jax.nn.sigmoid(ba)
    g = -jnp.exp(alog_v) * _softplus(ba + dtb_v)
    bg = jnp.where(lane < N_HEADS_B, beta, jnp.where(lane < 2 * N_HEADS_B, g, 0.0))
    return qn, kn, v, bg


def _dnpoint_bwd(c, ba, alog_v, dtb_v, dq, dk, dv, dbg):
    sig = jax.nn.sigmoid(c)
    cs = c * sig

    def l2_bwd(t, dy, scale):
        outs = []
        for th, dh in zip(_heads(t, N_HEADS_B), _heads(dy, N_HEADS_B)):
            r = lax.rsqrt(jnp.sum(th * th, axis=-1, keepdims=True) + EPS)
            y = th * r
            outs.append((scale * r) * (dh - y * jnp.sum(dh * y, axis=-1, keepdims=True)))
        return outs

    dcs = jnp.concatenate(l2_bwd(cs[:, :D_KEY_B], dq, HEAD_DIM ** -0.5) + l2_bwd(cs[:, D_KEY_B:2 * D_KEY_B], dk, 1.0) + [dv],
                          axis=1)
    dc = dcs * (sig * (1.0 + c * (1.0 - sig)))
    lane = lax.broadcasted_iota(jnp.int32, ba.shape, 1)
    is_beta, is_g = lane < N_HEADS_B, (lane >= N_HEADS_B) & (lane < 2 * N_HEADS_B)
    beta = jax.nn.sigmoid(ba)
    z = ba + dtb_v
    minus_ea = -jnp.exp(alog_v)
    dz = jnp.where(is_g, dbg * minus_ea * jax.nn.sigmoid(z), 0.0)
    dba = jnp.where(is_beta, dbg * beta * (1.0 - beta), dz)
    dal = jnp.sum(jnp.where(is_g, dbg * minus_ea * _softplus(z), 0.0), axis=0, keepdims=True)
    ddt = jnp.sum(dz, axis=0, keepdims=True)
    return dc, dba, dal, ddt


def _f_postdn(o, z, w):
    outs = []
    for oh, zh in zip(_heads(o, N_HEADS_B), _heads(z, N_HEADS_B)):
        outs.append(oh * lax.rsqrt(jnp.mean(oh * oh, axis=-1, keepdims=True) + EPS) * w * _silu(zh))
    return (jnp.concatenate(outs, axis=1),)


def _f_merge(o0, o1, o2, l0, l1, l2):
    m = lax.stop_gradient(jnp.maximum(jnp.maximum(l0, l1), l2))
    e0, e1, e2 = jnp.exp(l0 - m), jnp.exp(l1 - m), jnp.exp(l2 - m)
    return ((e0 * o0 + e1 * o1 + e2 * o2) / (e0 + e1 + e2),)


def _f_mix(ya, yb, gr):
    return (jax.nn.sigmoid(gr[:, :D_MODEL]) * ya + jax.nn.sigmoid(gr[:, D_MODEL:]) * yb,)


def _f_act(gu):
    return (_silu(gu[:, :D_FF]) * gu[:, D_FF:],)


def _attn_block(q, kp, kc, vp, vc, slope_d, first_key):
    k2 = jnp.concatenate([kp, kc], axis=0).astype(BF16)
    v2 = jnp.concatenate([vp, vc], axis=0).astype(BF16)
    s = lax.dot_general(q.astype(BF16), k2, (((1,), (1,)), ((), ())), preferred_element_type=F32)
    s = s * (HEAD_DIM ** -0.5)
    i = lax.broadcasted_iota(jnp.int32, (BLOCK_A, 2 * BLOCK_A), 0)
    j = lax.broadcasted_iota(jnp.int32, (BLOCK_A, 2 * BLOCK_A), 1)
    delta = BLOCK_A + i - j
    valid = (delta >= 0) & (delta <= W_SUB) & (j >= first_key)
    s = jnp.where(valid, s - slope_d * delta.astype(F32), -jnp.inf)
    m = lax.stop_gradient(jnp.max(s, axis=-1, keepdims=True))
    p = jnp.exp(s - m)
    den = jnp.sum(p, axis=-1, keepdims=True)
    o = jnp.dot(p.astype(BF16), v2, preferred_element_type=F32) / den
    lse = jnp.broadcast_to(m + jnp.log(den), (BLOCK_A, HEAD_DIM))
    return o, lse


def _slope_d(group, h, d):
    hg = (group * 4 + h + 1).astype(F32)
    return jnp.exp(jnp.full((1, 1), -8.0 * math.log(2.0) / N_HEADS_A, F32) * hg) * float(d)


ATT_CFG = ((1, 4, 4), (4, 2, 1), (16, 1, 1))
D_GROUP = 1536


def _att_columns():
    out = []
    for g, (_, _, hp) in enumerate(ATT_CFG):
        lanes = HEAD_DIM * hp
        for hb in range(D_ATTN_OUT // lanes):
            for which in range(3):
                out.append((N_HEADS_A * HEAD_DIM * which + D_ATTN_OUT * g + lanes * hb, lanes))
    return out
R_UNROLL = 4


def _attn_geometry(group, T):
    d, m, hp = ATT_CFG[group]
    span = BLOCK_A * d
    tile = m * span
    assert T % tile == 0 and 4 % hp == 0
    return d, m, hp, span, tile, T // tile, HEAD_DIM * hp


def _sub_rows(b, span, r, d):
    return pl.ds(b * span + r, BLOCK_A, stride=d) if d > 1 else pl.ds(b * span, BLOCK_A)


def _for_each_r(d, fn):
    if d <= R_UNROLL:
        for r in range(d):
            fn(r)
    else:
        def step(r, carry):
            fn(r)
            return carry

        lax.fori_loop(0, d, step, 0, unroll=R_UNROLL)


def _attn_fwd(name, proj, group, T):
    d, m, hp, span, tile, nt, lanes = _attn_geometry(group, T)
    qb, kb, vb = [(D_GROUP * group) // lanes + which for which in range(3)]

    def body(q_ref, kc_ref, kp_ref, vc_ref, vp_ref, o_ref, l_ref):
        hb, n = pl.program_id(0), pl.program_id(1)
        first_key = jnp.where(n > 0, 0, BLOCK_A)
        for hh in range(hp):
            ls = slice(HEAD_DIM * hh, HEAD_DIM * (hh + 1))
            sl = _slope_d(group, hb * hp + hh, d)
            for b in range(m):
                def one(r, b=b, ls=ls, sl=sl):
                    rq = _sub_rows(b, span, r, d)
                    if b == 0:
                        rp = _sub_rows(0, span, r, d)
                        kp, vp, fk = kp_ref[rp, ls], vp_ref[rp, ls], first_key
                    else:
                        rp = _sub_rows(b - 1, span, r, d)
                        kp, vp, fk = kc_ref[rp, ls], vc_ref[rp, ls], 0
                    o, lse = _attn_block(q_ref[rq, ls], kp, kc_ref[rq, ls], vp, vc_ref[rq, ls], sl, fk)
                    o_ref[rq, ls] = o
                    l_ref[rq, ls] = lse

                _for_each_r(d, one)

    def cur(cb):
        return pl.BlockSpec((tile, lanes), lambda hb, n: (n, cb + 3 * hb))

    def prev(cb):
        return pl.BlockSpec((span, lanes), lambda hb, n: (jnp.maximum(m * n - 1, 0), cb + 3 * hb))

    return pl.pallas_call(
        body, name=name, grid=(4 // hp, nt),
        in_specs=[cur(qb), cur(kb), prev(kb), cur(vb), prev(vb)],
        out_specs=[pl.BlockSpec((tile, lanes), lambda hb, n: (n, hb))] * 2,
        out_shape=[jax.ShapeDtypeStruct((T, D_ATTN_OUT), F32)] * 2,
        compiler_params=_cparams(2),
    )(proj, proj, proj, proj, proj)


def _attn_block_bwd(q, kp, kc, vp, vc, o, lse, do, dl, slope_d, first_key):
    scale = HEAD_DIM ** -0.5
    nt_dims = (((1,), (1,)), ((), ()))
    tn_dims = (((0,), (0,)), ((), ()))
    qb = q.astype(BF16)
    k2 = jnp.concatenate([kp, kc], axis=0).astype(BF16)
    v2 = jnp.concatenate([vp, vc], axis=0).astype(BF16)
    dob = do.astype(BF16)
    s = lax.dot_general(qb, k2, nt_dims, preferred_element_type=F32) * scale
    i = lax.broadcasted_iota(jnp.int32, (BLOCK_A, 2 * BLOCK_A), 0)
    j = lax.broadcasted_iota(jnp.int32, (BLOCK_A, 2 * BLOCK_A), 1)
    delta = BLOCK_A + i - j
    valid = (delta >= 0) & (delta <= W_SUB) & (j >= first_key)
    lse_col = jnp.max(lse, axis=-1, keepdims=True)
    p = jnp.exp(jnp.where(valid, s - slope_d * delta.astype(F32), -jnp.inf) - lse_col)
    dp = lax.dot_general(dob, v2, nt_dims, preferred_element_type=F32)
    row = jnp.sum(dl, axis=-1, keepdims=True) - jnp.sum(do * o, axis=-1, keepdims=True)
    ds = (p * (dp + row)).astype(BF16)
    dv2 = lax.dot_general(p.astype(BF16), dob, tn_dims, preferred_element_type=F32)
    dq = jnp.dot(ds, k2, preferred_element_type=F32) * scale
    dk2 = lax.dot_general(ds, qb, tn_dims, preferred_element_type=F32) * scale
    return dq, dk2[:BLOCK_A], dk2[BLOCK_A:], dv2[:BLOCK_A], dv2[BLOCK_A:]


def _attn_bwd(name, proj, o, lse, do, dl, group, T, dest):
    d, m, hp, span, tile, nt, lanes = _attn_geometry(group, T)
    qb, kb, vb = [(D_GROUP * group) // lanes + which for which in range(3)]
    n_in = 9 if dest is None else 10

    def body(*refs):
        q_ref, kc_ref, kp_ref, vc_ref, vp_ref, o_ref, l_ref, do_ref, dl_ref = refs[:9]
        out_ref, aq_ref, ak_ref, av_ref = refs[n_in:]
        hb, n = pl.program_id(0), pl.program_id(1)
        first_key = jnp.where(n > 0, 0, BLOCK_A)
        cur_slot = n % 2
        old_slot = 1 - cur_slot

        @pl.when(n == 0)
        def _():
            aq_ref[...] = jnp.zeros_like(aq_ref)
            ak_ref[...] = jnp.zeros_like(ak_ref)
            av_ref[...] = jnp.zeros_like(av_ref)

        @pl.when(n < nt)
        def _():
            for hh in range(hp):
                ls = slice(HEAD_DIM * hh, HEAD_DIM * (hh + 1))
                sl = _slope_d(group, hb * hp + hh, d)
                for b in range(m):
                    def one(r, b=b, ls=ls, sl=sl):
                        rq = _sub_rows(b, span, r, d)
                        if b == 0:
                            rp = _sub_rows(0, span, r, d)
                            kp, vp, fk = kp_ref[rp, ls], vp_ref[rp, ls], first_key
                        else:
                            rp = _sub_rows(b - 1, span, r, d)
                            kp, vp, fk = kc_ref[rp, ls], vc_ref[rp, ls], 0
                        dq, dkp, dkc, dvp, dvc = _attn_block_bwd(
                            q_ref[rq, ls], kp, kc_ref[rq, ls], vp, vc_ref[rq, ls], o_ref[rq, ls], l_ref[rq, ls],
                            do_ref[rq, ls], dl_ref[rq, ls], sl, fk)
                        aq_ref[cur_slot, rq, ls] = dq
                        ak_ref[cur_slot, rq, ls] = dkc
                        av_ref[cur_slot, rq, ls] = dvc
                        if b == 0:
                            last = _sub_rows(m - 1, span, r, d)
                            ak_ref[old_slot, last, ls] += dkp
                            av_ref[old_slot, last, ls] += dvp
                        else:
                            ak_ref[cur_slot, rp, ls] += dkp
                            av_ref[cur_slot, rp, ls] += dvp

                    _for_each_r(d, one)

        out_ref[:, 0:lanes] = aq_ref[old_slot].astype(BF16)
        out_ref[:, lanes:2 * lanes] = ak_ref[old_slot].astype(BF16)
        out_ref[:, 2 * lanes:3 * lanes] = av_ref[old_slot].astype(BF16)

    def cur(cb):
        return pl.BlockSpec((tile, lanes), lambda hb, n: (jnp.minimum(n, nt - 1), cb + 3 * hb))

    def prev(cb):
        return pl.BlockSpec((span, lanes), lambda hb, n: (jnp.maximum(m * jnp.minimum(n, nt - 1) - 1, 0), cb + 3 * hb))

    own = pl.BlockSpec((tile, lanes), lambda hb, n: (jnp.minimum(n, nt - 1), hb))
    late = pl.BlockSpec((tile, 3 * lanes), lambda hb, n: (jnp.maximum(n - 1, 0), (D_GROUP * group) // (3 * lanes) + hb))
    acc = pltpu.VMEM((2, tile, lanes), F32)
    return pl.pallas_call(
        body, name=name, grid=(4 // hp, nt + 1),
        in_specs=[cur(qb), cur(kb), prev(kb), cur(vb), prev(vb), own, own, own, own] + ([] if dest is None else [_ANY]),
        out_specs=late,
        out_shape=jax.ShapeDtypeStruct((T, D_PA), BF16),
        input_output_aliases={} if dest is None else {9: 0},
        scratch_shapes=[acc, acc, acc],
        compiler_params=_cparams(2),
    )(proj, proj, proj, proj, proj, o, lse, do, dl, *([] if dest is None else [dest]))


HALO = 8
HALO_IN = 16


def _shifted(ext):
    return [ext[HALO:, :]] + [pltpu.roll(ext, k, 0)[HALO:, :] for k in range(1, CONV_WIDTH)]


def _conv_taps(sh, w):
    c = sh[0] * w[CONV_WIDTH - 1]
    for k in range(1, CONV_WIDTH):
        c = c + sh[k] * w[CONV_WIDTH - 1 - k]
    return c


def _taps(w_ref):
    return [w_ref[j:j + 1, :] for j in range(CONV_WIDTH)]


def _dn_pre_fwd(name, proj, ba, conv_w, alog_v, dtb_v, tm, T):
    C = 3 * D_KEY_B

    def body(x_ref, ba_ref, w_ref, al_ref, dt_ref, q_ref, k_ref, v_ref, bg_ref, halo_ref):
        i = pl.program_id(0)

        @pl.when(i == 0)
        def _():
            halo_ref[...] = jnp.zeros_like(halo_ref)

        x = x_ref[...].astype(F32)
        ext = jnp.concatenate([halo_ref[...], x], axis=0)
        c = _conv_taps(_shifted(ext), _taps(w_ref))
        halo_ref[...] = x[tm - HALO:, :]
        qn, kn, v, bg = _f_dnpoint(c, ba_ref[...], al_ref[...], dt_ref[...])
        q_ref[...] = qn
        k_ref[...] = kn
        v_ref[...] = v
        bg_ref[...] = bg

    return pl.pallas_call(
        body, name=name, grid=(T // tm,),
        in_specs=[_row_spec(tm, C, 0), _row_spec(tm, 128, 0), _full_spec(conv_w.shape), _full_spec((1, 128)),
                  _full_spec((1, 128))],
        out_specs=[_row_spec(tm, D_KEY_B, 0)] * 3 + [_row_spec(tm, 128, 0)],
        out_shape=[jax.ShapeDtypeStruct((T, D_KEY_B), F32)] * 3 + [jax.ShapeDtypeStruct((T, 128), F32)],
        scratch_shapes=[pltpu.VMEM((HALO, C), F32)],
        compiler_params=_ARB,
    )(proj, ba, conv_w, alog_v, dtb_v)


def _dn_pre_bwd(name, proj, ba, conv_w, alog_v, dtb_v, dq, dk, dv, dbg, tm, T):
    C = 3 * D_KEY_B
    nt = T // tm
    hb = tm // HALO_IN

    def body(x_ref, xh_ref, ba_ref, w_ref, al_ref, dt_ref, dq_ref, dk_ref, dv_ref, dbg_ref,
             dx_ref, dba_ref, dw_ref, dal_ref, ddt_ref, nxt_ref):
        s = pl.program_id(0)
        i = nt - 1 - s

        @pl.when(s == 0)
        def _():
            nxt_ref[...] = jnp.zeros_like(nxt_ref)
            dw_ref[...] = jnp.zeros_like(dw_ref)
            dal_ref[...] = jnp.zeros_like(dal_ref)
            ddt_ref[...] = jnp.zeros_like(ddt_ref)

        w = _taps(w_ref)
        halo = jnp.where(i > 0, xh_ref[...].astype(F32)[HALO_IN - HALO:, :], 0.0)
        ext = jnp.concatenate([halo, x_ref[...].astype(F32)], axis=0)
        sh = _shifted(ext)
        c = _conv_taps(sh, w)
        dc, dba, dal, ddt = _dnpoint_bwd(c, ba_ref[...], al_ref[...], dt_ref[...],
                                         dq_ref[...].astype(F32), dk_ref[...].astype(F32), dv_ref[...].astype(F32), dbg_ref[...])
        dba_ref[...] = dba
        dal_ref[...] += dal
        ddt_ref[...] += ddt
        ext2 = jnp.concatenate([dc, nxt_ref[...]], axis=0)
        dx = dc * w[CONV_WIDTH - 1]
        for m in range(1, CONV_WIDTH):
            dx = dx + pltpu.roll(ext2, tm + HALO - m, 0)[:tm, :] * w[CONV_WIDTH - 1 - m]
        dx_ref[...] = dx.astype(dx_ref.dtype)
        nxt_ref[...] = dc[:HALO, :]
        for j in range(CONV_WIDTH):
            dw_ref[j:j + 1, :] += jnp.sum(sh[CONV_WIDTH - 1 - j] * dc, axis=0, keepdims=True)

    def rev(w):
        return pl.BlockSpec((tm, w), lambda s: (nt - 1 - s, 0))

    return pl.pallas_call(
        body, name=name, grid=(nt,),
        in_specs=[rev(C), pl.BlockSpec((HALO_IN, C), lambda s: (jnp.maximum((nt - 1 - s) * hb - 1, 0), 0)), rev(128),
                  _full_spec(conv_w.shape), _full_spec((1, 128)), _full_spec((1, 128)),
                  rev(D_KEY_B), rev(D_KEY_B), rev(D_KEY_B), rev(128)],
        out_specs=[rev(C), rev(128), _full_spec((8, C)), _full_spec((1, 128)), _full_spec((1, 128))],
        out_shape=[jax.ShapeDtypeStruct((T, C), BF16), jax.ShapeDtypeStruct((T, 128), F32),
                   jax.ShapeDtypeStruct((8, C), F32), jax.ShapeDtypeStruct((1, 128), F32),
                   jax.ShapeDtypeStruct((1, 128), F32)],
        scratch_shapes=[pltpu.VMEM((HALO, C), F32)],
        compiler_params=_ARB,
    )(proj, proj, ba, conv_w, alog_v, dtb_v, dq, dk, dv, dbg)


def _bdot(a, b, dims, exact=False):
    if exact:
        return lax.dot_general(a, b, dims, preferred_element_type=F32, precision=lax.Precision.HIGH)
    return lax.dot_general(a.astype(BF16), b.astype(BF16), dims, preferred_element_type=F32)


_B_NN = (((2,), (1,)), ((0,), (0,)))
_B_NT = (((2,), (2,)), ((0,), (0,)))
_B_TN = (((1,), (1,)), ((0,), (0,)))


def _unit_lower_inverse(a):
    C = a.shape[-1]
    r = lax.broadcasted_iota(jnp.int32, (C, C), 0)
    c = lax.broadcasted_iota(jnp.int32, (C, C), 1)
    pw = -a
    t = jnp.where((r == c)[None], 1.0, 0.0) + pw
    for _ in range(int(math.log2(C)) - 1):
        pw = _bdot(pw, pw, _B_NN, exact=True)
        t = t + _bdot(t, pw, _B_NN, exact=True)
    return t


@jax.custom_vjp
def _solve_given_inverse(t, a, rhs):
    return _bdot(t, rhs, _B_NN, exact=True)


def _solve_fwd(t, a, rhs):
    sol = _bdot(t, rhs, _B_NN, exact=True)
    return sol, (t, sol)


def _solve_bwd(res, dsol):
    t, sol = res
    drhs = _bdot(t, dsol, _B_TN, exact=True)
    return jnp.zeros_like(t), -_bdot(drhs, sol, _B_NT, exact=True), drhs


_solve_given_inverse.defvjp(_solve_fwd, _solve_bwd)


def _stack(xs):
    return jnp.concatenate([x[None] for x in xs], axis=0)


def _delta_chunk(S, q, k, v, bg, t_inv=None):
    H, C = N_HEADS_B, CHUNK
    r = lax.broadcasted_iota(jnp.int32, (C, C), 0)
    c = lax.broadcasted_iota(jnp.int32, (C, C), 1)
    incl, strict, eye = r >= c, r > c, r == c
    gc_all = jnp.dot(incl.astype(F32), bg, preferred_element_type=F32, precision=lax.Precision.HIGHEST)
    lane = lax.broadcasted_iota(jnp.int32, (C, 128), 1)

    def col(x, l):
        return jnp.sum(jnp.where(lane == l, x, 0.0), axis=1, keepdims=True)

    beta = _stack([col(bg, h) for h in range(H)])
    gc = _stack([col(gc_all, H + h) for h in range(H)])
    gc_row = jnp.sum(jnp.where(eye[None], jnp.broadcast_to(gc, (H, C, C)), 0.0), axis=1, keepdims=True)
    row = lax.broadcasted_iota(jnp.int32, (H, C, 1), 1)
    gc_last = jnp.sum(jnp.where(row == C - 1, gc, 0.0), axis=1, keepdims=True)
    decay = jnp.exp(jnp.where(incl[None], gc - gc_row, -jnp.inf))
    qh = _stack(_heads(q, H))
    kh = _stack(_heads(k, H))
    vh = _stack(_heads(v, H))
    kk = _bdot(kh, kh, _B_NT, exact=True)
    a = jnp.where(strict[None], beta * kk * decay, 0.0)
    egc = jnp.exp(gc)
    rhs = jnp.concatenate([beta * vh, (beta * egc) * kh], axis=-1)
    if t_inv is None:
        t_inv = _unit_lower_inverse(a)
        sol = _bdot(t_inv, rhs, _B_NN, exact=True)
    else:
        sol = _solve_given_inverse(t_inv, a, rhs)
    u_bar, w = sol[..., :HEAD_DIM], sol[..., HEAD_DIM:]
    qk = _bdot(qh, kh, _B_NT) * decay
    q_dec = qh * egc
    k_dec = kh * jnp.exp(gc_last - gc)
    u = u_bar - _bdot(w, S, _B_NN)
    o = _bdot(q_dec, S, _B_NN) + _bdot(qk, u, _B_NN)
    S_new = jnp.exp(gc_last) * S + _bdot(k_dec, u, _B_TN)
    o2 = jnp.concatenate([lax.index_in_dim(o, h, 0, keepdims=False) for h in range(H)], axis=1)
    return o2, S_new, t_inv


def _delta_fwd(name, q, k, v, bg, T):
    nc = T // CHUNK
    H = N_HEADS_B
    G, R = CHUNKS_FWD, CHUNKS_FWD * CHUNK

    def body(q_ref, k_ref, v_ref, bg_ref, o_ref, s_out_ref, t_out_ref, s_ref):
        @pl.when(pl.program_id(0) == 0)
        def _():
            s_ref[...] = jnp.zeros_like(s_ref)

        S = s_ref[...]
        for c in range(G):
            rows = slice(c * CHUNK, (c + 1) * CHUNK)
            s_out_ref[c] = S
            o, S, t_inv = _delta_chunk(S, q_ref[rows, :], k_ref[rows, :], v_ref[rows, :], bg_ref[rows, :])
            o_ref[rows, :] = o.astype(o_ref.dtype)
            t_out_ref[c] = t_inv
        s_ref[...] = S

    return pl.pallas_call(
        body, name=name, grid=(nc // G,),
        in_specs=[_row_spec(R, D_KEY_B, 0)] * 3 + [_row_spec(R, 128, 0)],
        out_specs=[_row_spec(R, D_KEY_B, 0), pl.BlockSpec((G, H, HEAD_DIM, HEAD_DIM), lambda i: (i, 0, 0, 0)),
                   pl.BlockSpec((G, H, CHUNK, CHUNK), lambda i: (i, 0, 0, 0))],
        out_shape=[jax.ShapeDtypeStruct((T, D_KEY_B), BF16), jax.ShapeDtypeStruct((nc, H, HEAD_DIM, HEAD_DIM), F32),
                   jax.ShapeDtypeStruct((nc, H, CHUNK, CHUNK), F32)],
        scratch_shapes=[pltpu.VMEM((H, HEAD_DIM, HEAD_DIM), F32)],
        compiler_params=_ARB,
    )(q, k, v, bg)


def _delta_bwd(name, q, k, v, bg, s_all, t_all, do, T):
    nc = T // CHUNK
    H = N_HEADS_B
    G, R = CHUNKS_BWD, CHUNKS_BWD * CHUNK
    ns = nc // G

    def body(q_ref, k_ref, v_ref, bg_ref, s_in_ref, t_ref, do_ref, dq_ref, dk_ref, dv_ref, dbg_ref, ds_ref):
        @pl.when(pl.program_id(0) == 0)
        def _():
            ds_ref[...] = jnp.zeros_like(ds_ref)

        dS = ds_ref[...]
        for c in reversed(range(G)):
            rows = slice(c * CHUNK, (c + 1) * CHUNK)
            t_inv = t_ref[c]

            def f(S, qq, kk, vv, bb, t_inv=t_inv):
                return _delta_chunk(S, qq, kk, vv, bb, t_inv)[:2]

            _, vjp = jax.vjp(f, s_in_ref[c], q_ref[rows, :], k_ref[rows, :], v_ref[rows, :], bg_ref[rows, :])
            dS, dq, dk, dv, dbg = vjp((do_ref[rows, :].astype(F32), dS))
            dq_ref[rows, :] = dq.astype(dq_ref.dtype)
            dk_ref[rows, :] = dk.astype(dk_ref.dtype)
            dv_ref[rows, :] = dv.astype(dv_ref.dtype)
            dbg_ref[rows, :] = dbg
        ds_ref[...] = dS

    def rev(w):
        return pl.BlockSpec((R, w), lambda s: (ns - 1 - s, 0))

    def rev4(n):
        return pl.BlockSpec((G, H, n, n), lambda s: (ns - 1 - s, 0, 0, 0))

    return pl.pallas_call(
        body, name=name, grid=(ns,),
        in_specs=[rev(D_KEY_B)] * 3 + [rev(128), rev4(HEAD_DIM), rev4(CHUNK), rev(D_KEY_B)],
        out_specs=[rev(D_KEY_B)] * 3 + [rev(128)],
        out_shape=[jax.ShapeDtypeStruct((T, D_KEY_B), BF16)] * 3 + [jax.ShapeDtypeStruct((T, 128), F32)],
        scratch_shapes=[pltpu.VMEM((H, HEAD_DIM, HEAD_DIM), F32)],
        compiler_params=_ARB,
    )(q, k, v, bg, s_all, t_all, do)


def _loss_fb(name, h2, w, tgt, tm, T):
    def tile_loss(h, wv, t):
        y = _f_rms(h, wv)[0]
        e = y - t
        return 0.5 * jnp.sum(jnp.mean(e * e, axis=-1))

    def body(h_ref, w_ref, t_ref, dh_ref, loss_ref, dw_ref):
        @pl.when(pl.program_id(0) == 0)
        def _():
            loss_ref[...] = jnp.zeros_like(loss_ref)
            dw_ref[...] = jnp.zeros_like(dw_ref)

        val, (dh, dw) = jax.value_and_grad(tile_loss, argnums=(0, 1))(h_ref[...], w_ref[...], t_ref[...])
        dh_ref[...] = dh
        dw_ref[...] += dw
        loss_ref[...] += jnp.reshape(val, (1, 1))

    return pl.pallas_call(
        body, name=name, grid=(T // tm,),
        in_specs=[_row_spec(tm, D_MODEL, 0), _full_spec((1, D_MODEL)), _row_spec(tm, D_MODEL, 0)],
        out_specs=[_row_spec(tm, D_MODEL, 0), _full_spec((1, 1)), _full_spec((1, D_MODEL))],
        out_shape=[jax.ShapeDtypeStruct((T, D_MODEL), F32), jax.ShapeDtypeStruct((1, 1), F32),
                   jax.ShapeDtypeStruct((1, D_MODEL), F32)],
        compiler_params=_ARB,
    )(h2, w, tgt)


def _adamw(name, w, g, m, v, tile):
    R = w.shape[0]

    def body(w_ref, g_ref, m_ref, v_ref, d_out, m_out, v_out):
        g = g_ref[...]
        mm = ADAM_B1 * m_ref[...] + (1.0 - ADAM_B1) * g
        vv = ADAM_B2 * v_ref[...] + (1.0 - ADAM_B2) * jnp.square(g)
        m_hat = mm / (1.0 - ADAM_B1 ** ADAM_STEP)
        v_hat = vv / (1.0 - ADAM_B2 ** ADAM_STEP)
        d_out[...] = -ADAM_LR * (m_hat / (jnp.sqrt(v_hat) + ADAM_EPS) + ADAM_WD * w_ref[...])
        m_out[...] = mm
        v_out[...] = vv

    assert R % tile == 0
    spec = pl.BlockSpec((tile, w.shape[1]), lambda i: (i, 0))
    return pl.pallas_call(
        body, name=name, grid=(R // tile,), in_specs=[spec] * 4, out_specs=[spec] * 3,
        out_shape=[jax.ShapeDtypeStruct(w.shape, F32)] * 3, compiler_params=_ARB,
    )(w, g, m, v)


def _sum4(name, r, tile):
    _, R, C = r.shape
    assert R % tile == 0

    def body(r_ref, o_ref):
        o_ref[...] = ((r_ref[0].astype(F32) + r_ref[1].astype(F32)) + r_ref[2].astype(F32)) + r_ref[3].astype(F32)

    return pl.pallas_call(
        body, name=name, grid=(R // tile,),
        in_specs=[pl.BlockSpec((4, tile, C), lambda i: (0, i, 0))],
        out_specs=pl.BlockSpec((tile, C), lambda i: (i, 0)),
        out_shape=jax.ShapeDtypeStruct((R, C), F32), compiler_params=_ARB,
    )(r)


def _add2(name, a, b, tile):
    R, C = a.shape
    assert R % tile == 0

    def body(a_ref, b_ref, o_ref):
        o_ref[...] = (a_ref[...].astype(F32) + b_ref[...].astype(F32)).astype(BF16)

    spec = pl.BlockSpec((tile, C), lambda i: (i, 0))
    return pl.pallas_call(
        body, name=name, grid=(R // tile,), in_specs=[spec, spec], out_specs=spec,
        out_shape=jax.ShapeDtypeStruct((R, C), BF16), compiler_params=_ARB,
    )(a, b)


_ANY = pl.BlockSpec(memory_space=pl.ANY)


def _place():
    return lax.axis_index("x"), lax.axis_index("y"), lax.axis_index("c")


def _chip_exchange(in_refs, out_refs, send_sems, recv_sems, local_sems, gather):
    n = len(in_refs)
    x, y, c = _place()
    me = 2 * x + y
    chips = [(1 - x, y), (x, 1 - y), (1 - x, 1 - y)]

    def src(a, slot):
        return in_refs[a] if gather else in_refs[a].at[slot]

    def copy(a, k, src_slot, dst_slot, px, py):
        return pltpu.make_async_remote_copy(src_ref=src(a, src_slot), dst_ref=out_refs[a].at[dst_slot],
                                            send_sem=send_sems.at[3 * a + k], recv_sem=recv_sems.at[3 * a + k],
                                            device_id=(px, py, c), device_id_type=MESH)

    def mine():
        return [pltpu.make_async_copy(src(a, me), out_refs[a].at[me], local_sems.at[a]) for a in range(n)]

    def sends():
        return [copy(a, k, 2 * px + py, me, px, py) for a in range(n) for k, (px, py) in enumerate(chips)]

    def start():
        for cp in mine() + sends():
            cp.start()

    def finish():
        for a in range(n):
            for k, (px, py) in enumerate(chips):
                copy(a, k, me, 2 * px + py, px, py).wait_recv()
        for cp in sends():
            cp.wait_send()
        for cp in mine():
            cp.wait()

    return start, finish


def _chip_exchange_shapes(arrays, gather):
    return [jax.ShapeDtypeStruct(((4,) + a.shape) if gather else a.shape, a.dtype) for a in arrays]


def _chip_exchange_sems(n):
    return [pltpu.SemaphoreType.DMA((3 * n,)), pltpu.SemaphoreType.DMA((3 * n,)), pltpu.SemaphoreType.DMA((n,))]


def _exchange_chips(name, arrays, gather):
    n = len(arrays)

    def body(*refs):
        start, finish = _chip_exchange(refs[:n], refs[n:2 * n], *refs[2 * n:], gather=gather)
        start()
        finish()

    return pl.pallas_call(
        body, name=name, in_specs=[_ANY] * n, out_specs=[_ANY] * n,
        out_shape=_chip_exchange_shapes(arrays, gather), scratch_shapes=_chip_exchange_sems(n),
    )(*arrays)


def _swap_sibling(name, arrays):
    n = len(arrays)

    def body(*refs):
        in_refs, out_refs, send_sems, recv_sems = refs[:n], refs[n:2 * n], refs[2 * n], refs[2 * n + 1]
        x, y, c = _place()
        cps = [pltpu.make_async_remote_copy(src_ref=in_refs[a], dst_ref=out_refs[a], send_sem=send_sems.at[a],
                                            recv_sem=recv_sems.at[a], device_id=(x, y, 1 - c), device_id_type=MESH)
               for a in range(n)]
        for cp in cps:
            cp.start()
        for cp in cps:
            cp.wait()

    return pl.pallas_call(
        body, name=name, in_specs=[_ANY] * n, out_specs=[_ANY] * n,
        out_shape=[jax.ShapeDtypeStruct(a.shape, a.dtype) for a in arrays],
        scratch_shapes=[pltpu.SemaphoreType.DMA((n,)), pltpu.SemaphoreType.DMA((n,))],
    )(*arrays)


def _allsum_small(name, pack):
    def body(p_ref, out_ref, slots, send_sems, recv_sems):
        x, y, c = _place()
        me = 4 * x + 2 * y + c
        slots[me] = p_ref[...]
        sends = []
        for k in range(1, 8):
            px = 1 - x if k & 4 else x
            py = 1 - y if k & 2 else y
            pc = 1 - c if k & 1 else c
            cp = pltpu.make_async_remote_copy(src_ref=p_ref, dst_ref=slots.at[me], send_sem=send_sems.at[k - 1],
                                              recv_sem=recv_sems.at[k - 1], device_id=(px, py, pc), device_id_type=MESH)
            cp.start()
            sends.append(cp)
        for k in range(1, 8):
            px = 1 - x if k & 4 else x
            py = 1 - y if k & 2 else y
            pc = 1 - c if k & 1 else c
            pltpu.make_async_remote_copy(src_ref=p_ref, dst_ref=slots.at[4 * px + 2 * py + pc],
                                         send_sem=send_sems.at[k - 1], recv_sem=recv_sems.at[k - 1],
                                         device_id=(px, py, pc), device_id_type=MESH).wait_recv()
        for cp in sends:
            cp.wait_send()
        acc = slots[0]
        for s in range(1, 8):
            acc = acc + slots[s]
        out_ref[...] = acc

    vm = pl.BlockSpec(memory_space=pltpu.VMEM)
    return pl.pallas_call(
        body, name=name, in_specs=[vm], out_specs=vm,
        out_shape=jax.ShapeDtypeStruct(pack.shape, pack.dtype),
        scratch_shapes=[pltpu.VMEM((8,) + pack.shape, pack.dtype), pltpu.SemaphoreType.DMA((7,)),
                        pltpu.SemaphoreType.DMA((7,))],
    )(pack)


def _pack_shards(parts, rows_total):
    flat = [p.reshape(-1, D_MODEL) for p in parts]
    used = sum(f.shape[0] for f in flat)
    return jnp.concatenate(flat + [jnp.zeros((rows_total - used, D_MODEL), flat[0].dtype)], axis=0)


def _unpack_shards(buf, shapes):
    out, r0 = [], 0
    for shp in shapes:
        n = math.prod(shp) // D_MODEL
        out.append(buf[r0:r0 + n].reshape(shp))
        r0 += n
    return out


def kernel(x, norm_mix, w_in, conv_w, a_log, dt_bias, dn_norm, w_proj_attn, w_proj_delta, w_out, norm_ffn, w_gate, w_up, w_down, norm_final, loss_target, m_norm_mix, m_w_in, m_conv_w, m_a_log, m_dt_bias, m_dn_norm, m_w_proj_attn, m_w_proj_delta, m_w_out, m_norm_ffn, m_w_gate, m_w_up, m_w_down, m_norm_final, v_norm_mix, v_w_in, v_conv_w, v_a_log, v_dt_bias, v_dn_norm, v_w_proj_attn, v_w_proj_delta, v_w_out, v_norm_ffn, v_w_gate, v_w_up, v_w_down, v_norm_final):
    T = x.shape[1]
    D = D_MODEL
    tm = 512
    tw = 256
    mt = 1024 if T % 1024 == 0 else 512
    x2 = x.reshape(T, D)
    tgt = loss_target.reshape(T, D)

    packed = (w_proj_attn, w_proj_delta, w_out, w_gate, w_up, w_down)
    conv_bits = lax.bitcast_convert_type(conv_w[0], BF16)
    w16 = _pack_shards([w[0].astype(BF16) for w in packed] + [conv_bits], PACK_R)
    win16 = w_in[0].astype(BF16)
    my_c = lax.axis_index("c")

    def both_halves(mine, other, axis):
        return jnp.concatenate([jnp.where(my_c == 0, mine, other), jnp.where(my_c == 0, other, mine)], axis=axis)

    pack_half = lax.dynamic_slice_in_dim(w16, my_c * HALF, HALF, axis=0)
    win_half = lax.dynamic_slice_in_dim(win16, my_c * W_IN_HALF, W_IN_HALF, axis=0)
    u, win_mine = _rowwise("rms_mix", _f_rms, [(x2, 0, D)], [norm_mix], [(D, BF16)], tm, T, side=([win_half], True))
    (win_other,) = _swap_sibling("swap_w_in", [win_mine])
    win4 = both_halves(win_mine, win_other, 1)

    def win_cols(g0, g1):
        out = []
        for j in range(4):
            lo, hi = max(g0, W_IN_COLS * j), min(g1, W_IN_COLS * (j + 1))
            if lo < hi:
                out.append(win4[j][:, lo - W_IN_COLS * j:hi - W_IN_COLS * j])
        return out

    Wa = jnp.concatenate([p for c0, w in _att_columns() for p in win_cols(c0, c0 + w)], axis=1)
    Wb = jnp.concatenate(win_cols(4608, 8704) + win_cols(8720, D_IN), axis=1)
    Wba = jnp.pad(jnp.concatenate(win_cols(8704, 8720), axis=1), ((0, 0), (0, 112)))
    alog_v = jnp.pad(a_log, ((0, 0), (8, 112)))
    dtb_v = jnp.pad(dt_bias, ((0, 0), (8, 112)))

    proj_a, pack_mine = _matmul("mm_in_a", u, Wa, "nn", T, D_PA, D, mt, 2304, D, F32, side=([pack_half], True))
    (pack_other,) = _swap_sibling("swap_pack", [pack_mine])
    wg = both_halves(pack_mine, pack_other, 1)

    def seg(r0, n):
        return wg[:, r0:r0 + n, :]

    r = 0
    Wpa = seg(r, 128).reshape(4, D_ATTN_OUT, 256).transpose(1, 0, 2).reshape(D_ATTN_OUT, D); r += 128
    Wpd = seg(r, 256).reshape(D, D); r += 256
    Wout = seg(r, 256).reshape(D, D); r += 256
    Wg = seg(r, 704).reshape(4, D, 704).transpose(1, 0, 2).reshape(D, D_FF); r += 704
    Wu = seg(r, 704).reshape(4, D, 704).transpose(1, 0, 2).reshape(D, D_FF); r += 704
    Wd = seg(r, 704).reshape(D_FF, D); r += 704
    convw = lax.bitcast_convert_type(seg(r, 6).reshape(4, CONV_WIDTH, 768, 2), F32)
    convw = convw.transpose(1, 0, 2).reshape(CONV_WIDTH, 3 * D_KEY_B)
    Wgu = jnp.concatenate([Wg, Wu], axis=1)
    proj_b = _matmul("mm_in_b", u, Wb, "nn", T, D_PB, D, mt, 3072, D, BF16)
    ba = _matmul("mm_ba", u, Wba, "nn", T, 128, D, mt, 128, D, F32)

    att = [_attn_fwd(f"attn_fwd{g}", proj_a, g, T) for g in range(3)]
    (ya_m,) = _rowwise("merge", _f_merge, [(att[g][0], 0, 512) for g in range(3)] + [(att[g][1], 0, 512) for g in range(3)],
                       [], [(512, BF16)], tm, T)
    qn, kn, vd, bg = _dn_pre_fwd("dn_pre_fwd", proj_b, ba, convw, alog_v, dtb_v, tw, T)
    o_d, s_all, t_all = _delta_fwd("delta_fwd", qn, kn, vd, bg, T)
    (od,) = _rowwise("post_dn", _f_postdn, [(o_d, 0, D), (proj_b, C_Z // D, D)], [dn_norm], [(D, BF16)], tm, T)
    y_a = _matmul("mm_pa", ya_m, Wpa, "nn", T, D, 512, mt, D, 512, BF16)
    y_b = _matmul("mm_pd", od, Wpd, "nn", T, D, D, mt, D, D, BF16)
    (mix,) = _rowwise("mix", _f_mix, [(y_a, 0, D), (y_b, 0, D), (proj_b, C_GATE // 2048, 2048)], [], [(D, BF16)], tm, T)
    h1 = _matmul("mm_out", mix, Wout, "nn", T, D, D, mt, D, D, F32, residual=x2)
    (hn,) = _rowwise("rms_ffn", _f_rms, [(h1, 0, D)], [norm_ffn], [(D, BF16)], tm, T)
    gu = _matmul("mm_gu", hn, Wgu, "nn", T, 2 * D_FF, D, mt, 2816, D, BF16)
    h2 = _matmul("mm_down", gu, Wd, "nn", T, D, D_FF, mt, D, 1408, F32, residual=h1, a_swiglu=D_FF)
    dh2, loss, g_norm_final = _loss_fb("loss", h2, norm_final.reshape(1, D), tgt, tm, T)

    dact = _matmul("mm_dact", dh2, Wd, "nt", T, D_FF, D, mt, 1408, D, BF16)
    gWd = _matmul("mm_gwd", gu, dh2, "tn", D_FF, D, T, 1408, D, mt, BF16, a_swiglu=D_FF)
    (dgu,) = _rowwise_vjp("act_bwd", _f_act, [(gu, 0, 2 * D_FF)], [], [(dact, 0, D_FF)], [BF16], tw, T)
    dhn = _matmul("mm_dhn", dgu, Wgu, "nt", T, D, 2 * D_FF, mt, D, 2816, F32)
    gWgu = _matmul("mm_gwgu", hn, dgu, "tn", D, 2 * D_FF, T, D, 1408, 2 * mt, BF16)
    dh1, g_norm_ffn = _rowwise_vjp("rms_ffn_bwd", _f_rms, [(h1, 0, D)], [norm_ffn], [(dhn, 0, D)], [F32], tm, T,
                                   residual=(dh2, 0, D))
    dmix = _matmul("mm_dmix", dh1, Wout, "nt", T, D, D, mt, D, D, BF16)
    gWout = _matmul("mm_gwout", mix, dh1, "tn", D, D, T, D, D, mt, BF16)
    dya, dyb, dgate = _rowwise_vjp("mix_bwd", _f_mix, [(y_a, 0, D), (y_b, 0, D), (proj_b, C_GATE // 2048, 2048)], [],
                                   [(dmix, 0, D)], [BF16, BF16, BF16], tm, T)
    dya_m = _matmul("mm_dyam", dya, Wpa, "nt", T, 512, D, mt, 512, D, BF16)
    gWpa = _matmul("mm_gwpa", ya_m, dya, "tn", 512, D, T, 512, D, mt, BF16)
    dod = _matmul("mm_dod", dyb, Wpd, "nt", T, D, D, mt, D, D, BF16)
    gWpd = _matmul("mm_gwpd", od, dyb, "tn", D, D, T, D, D, mt, BF16)
    do_d, dz, g_dn_norm = _rowwise_vjp("post_dn_bwd", _f_postdn, [(o_d, 0, D), (proj_b, C_Z // D, D)], [dn_norm],
                                       [(dod, 0, D)], [BF16, BF16], tm, T)
    dqn, dkn, dvd, dbg = _delta_bwd("delta_bwd", qn, kn, vd, bg, s_all, t_all, do_d, T)
    dqkvd, dba, g_conv8, g_alog_v, g_dtb_v = _dn_pre_bwd("dn_pre_bwd", proj_b, ba, convw, alog_v, dtb_v,
                                                         dqn, dkn, dvd, dbg, tw, T)
    def pair_sum(tag, g, h):
        keep = lax.dynamic_slice_in_dim(g, my_c * h, h, axis=1)
        give = lax.dynamic_slice_in_dim(g, (1 - my_c) * h, h, axis=1)
        (got,) = _swap_sibling("swap_grads_" + tag, [give])
        return _add2("add_cores_" + tag, keep.reshape(-1, g.shape[2]), got.reshape(-1, g.shape[2]), HALF_TILE).reshape(keep.shape)

    def cols4(g, n):
        return g.reshape(g.shape[0], 4, n).transpose(1, 0, 2).reshape(4, -1, D)

    parts = [cols4(gWpa, 256), gWpd.reshape(4, 256, D), gWout.reshape(4, 256, D), cols4(gWgu[:, :D_FF], 704),
             cols4(gWgu[:, D_FF:], 704), gWd.reshape(4, 704, D), cols4(g_conv8[:CONV_WIDTH], 768)]
    used = sum(p.shape[1] for p in parts)
    g_pack = jnp.concatenate([p.astype(BF16) for p in parts] + [jnp.zeros((4, PACK_R - used, D), BF16)], axis=1)
    pair_pack = pair_sum("pack", g_pack, HALF)

    mg = _rowwise_vjp("merge_bwd", _f_merge,
                      [(att[g][0], 0, 512) for g in range(3)] + [(att[g][1], 0, 512) for g in range(3)], [],
                      [(dya_m, 0, 512)], [F32] * 6, tm, T)
    dproj_a = None
    for g in range(3):
        dproj_a = _attn_bwd(f"attn_bwd{g}", proj_a, att[g][0], att[g][1], mg[g], mg[3 + g], g, T, dproj_a)
    dproj_b = [dqkvd, dz, dgate]
    gWa, recv_pack = _matmul("mm_gwa", u, dproj_a, "tn", D, D_PA, T, D, 1536, 2 * mt, BF16, side=([pair_pack], False))
    gWb = _matmul_pieces("mm_gwb", "tn", dproj_b, u, D, D_PB, T, D, 1024, 2 * mt, BF16)
    gWba = _matmul("mm_gwba", u, dba, "tn", D, 128, T, D, 128, mt, BF16)

    pieces, off = [(gWb[:, :C_GATE], D_PA), (gWba[:, :16], 8704), (gWb[:, C_GATE:], 8720)], 0
    for c0, w in _att_columns():
        pieces.append((gWa[:, off:off + w], c0))
        off += w
    pieces.sort(key=lambda t: t[1])

    def win_shard(j):
        out = []
        for arr, g0 in pieces:
            lo, hi = max(g0, W_IN_COLS * j), min(g0 + arr.shape[1], W_IN_COLS * (j + 1))
            if lo < hi:
                out.append(arr[:, lo - g0:hi - g0])
        return jnp.concatenate(out, axis=1)

    g_win4 = jnp.concatenate([win_shard(j)[None] for j in range(4)], axis=0)
    pair_win = pair_sum("w_in", g_win4, W_IN_HALF)

    du0 = _matmul("mm_du_ba", dba, Wba, "nt", T, D, 128, mt, D, 128, F32)
    du1 = _matmul_pieces("mm_du_b", "nt", dproj_b, Wb, T, D, D_PB, mt, D, 1024, F32, residual=du0)
    du, recv_win = _matmul("mm_du_a", dproj_a, Wa, "nt", T, D, D_PA, mt, D, 2304, F32, residual=du1, side=([pair_win], False))
    dx, g_norm_mix = _rowwise_vjp("rms_mix_bwd", _f_rms, [(x2, 0, D)], [norm_mix], [(du, 0, D)], [F32], tm, T,
                                  residual=(dh1, 0, D))

    g_mine = [_sum4("sum_chips_pack", recv_pack, HALF_TILE), _sum4("sum_chips_w_in", recv_win, HALF_TILE)]
    g_other = _swap_sibling("swap_sums", g_mine)
    g_all = both_halves(g_mine[0], g_other[0], 0)
    g_win = both_halves(g_mine[1], g_other[1], 0)

    locals_ = (w_proj_attn, w_proj_delta, w_out, w_gate, w_up, w_down, conv_w, w_in)
    ms = (m_w_proj_attn, m_w_proj_delta, m_w_out, m_w_gate, m_w_up, m_w_down, m_conv_w, m_w_in)
    vs = (v_w_proj_attn, v_w_proj_delta, v_w_out, v_w_gate, v_w_up, v_w_down, v_conv_w, v_w_in)
    big_names = ("w_proj_attn", "w_proj_delta", "w_out", "w_gate", "w_up", "w_down", "conv_w", "w_in")
    adam_tiles = (512, 256, 256, 256, 256, 352, CONV_WIDTH, 128)
    g_shards = _unpack_shards(g_all, [w.shape for w in locals_[:-1]]) + [g_win.reshape(w_in.shape)]
    big = [g_shards, [], [], []]
    for nm, w, g, m_, v_, tl in zip(big_names, locals_, g_shards, ms, vs, adam_tiles):
        res = _adamw("adamw_" + nm, w[0], g[0], m_[0], v_[0], tl)
        for kind in range(3):
            big[kind + 1].append(res[kind].reshape(w.shape))

    def small_pack(nm, nf, nfin, dn, al, dt, extra):
        row3 = jnp.concatenate([dn.reshape(1, 128), al.reshape(1, 8), dt.reshape(1, 8), extra.reshape(1, 1),
                                jnp.zeros((1, D - 145), F32)], axis=1)
        return jnp.concatenate([nm.reshape(1, D), nf.reshape(1, D), nfin.reshape(1, D), row3, jnp.zeros((4, D), F32)], axis=0)

    zero1 = jnp.zeros((1, 1), F32)
    sp = small_pack(g_norm_mix, g_norm_ffn, g_norm_final, g_dn_norm, g_alog_v[:, 8:16], g_dtb_v[:, 8:16], loss)
    tot = _allsum_small("allsum_small", sp)
    sm = _adamw("adamw_small", small_pack(norm_mix, norm_ffn, norm_final, dn_norm, a_log, dt_bias, zero1), tot,
                small_pack(m_norm_mix, m_norm_ffn, m_norm_final, m_dn_norm, m_a_log, m_dt_bias, zero1),
                small_pack(v_norm_mix, v_norm_ffn, v_norm_final, v_dn_norm, v_a_log, v_dt_bias, zero1), 8)
    sm = [tot] + list(sm)

    def small_unpack(p):
        return {"norm_mix": p[0:1], "norm_ffn": p[1:2], "norm_final": p[2], "dn_norm": p[3:4, :128],
                "a_log": p[3:4, 128:136], "dt_bias": p[3:4, 136:144]}

    sm = [small_unpack(p) for p in sm]
    loss_out = tot[3, 144]

    order = ("norm_mix", "w_in", "conv_w", "a_log", "dt_bias", "dn_norm", "w_proj_attn", "w_proj_delta", "w_out",
             "norm_ffn", "w_gate", "w_up", "w_down", "norm_final")
    outs = [loss_out, dx.reshape(x.shape)]
    for kind in range(4):
        for n in order:
            outs.append(big[kind][big_names.index(n)] if n in big_names else sm[kind][n])
    return tuple(outs)
```

```python
import functools
import math

import jax
import jax.numpy as jnp
from jax import lax
from jax.experimental import pallas as pl
from jax.experimental.pallas import tpu as pltpu

F32 = jnp.float32
BF16 = jnp.bfloat16
MESH = pl.DeviceIdType.MESH

D_MODEL = 1024
N_HEADS_A = 12
HEAD_DIM = 128
BLOCK_A = 128
DILATIONS = (1, 4, 16)
W_SUB = 128
D_ATTN_OUT = 512
N_HEADS_B = 8
D_KEY_B = 1024
CONV_WIDTH = 4
CHUNK = 64
CHUNKS_FWD, CHUNKS_BWD = 2, 1
D_FF = 2816
EPS = 1e-6
D_IN = 10768
D_PA = 4608
D_PB = 6144
C_Z, C_GATE = 3072, 4096

ADAM_LR, ADAM_B1, ADAM_B2, ADAM_EPS, ADAM_WD, ADAM_STEP = 0.001, 0.9, 0.999, 1e-08, 0.01, 10

PACK_R = 2816
HALF = PACK_R // 2
HALF_TILE = 128
W_IN_COLS = 2692
W_IN_HALF = D_MODEL // 2

VMEM_LIMIT = 48 * 1024 * 1024


def _cparams(n):
    return pltpu.CompilerParams(dimension_semantics=("arbitrary",) * n, vmem_limit_bytes=VMEM_LIMIT)


_ARB = _cparams(1)


def _row_spec(tm, w, cb):
    return pl.BlockSpec((tm, w), lambda i: (i, cb))


def _full_spec(shape):
    nd = len(shape)
    return pl.BlockSpec(shape, lambda i: (0,) * nd)


def _rowwise(name, fn, ins, params, outs, tm, T, side=None):
    n_in, n_p, n_out = len(ins), len(params), len(outs)
    n_side = 0 if side is None else len(side[0])
    steps = T // tm

    def body(*refs):
        k0 = n_in + n_p + n_side
        if n_side:
            start, finish = _chip_exchange(refs[n_in + n_p:k0], refs[k0 + n_out:k0 + n_out + n_side],
                                           *refs[k0 + n_out + n_side:], gather=side[1])
            pl.when(pl.program_id(0) == 0)(start)
        xs = [r[...].astype(F32) for r in refs[:n_in]] + [r[...] for r in refs[n_in:n_in + n_p]]
        res = fn(*xs)
        for o_ref, v in zip(refs[k0:k0 + n_out], res):
            o_ref[...] = v.astype(o_ref.dtype)
        if n_side:
            pl.when(pl.program_id(0) == steps - 1)(finish)

    side_arrays = [] if side is None else list(side[0])
    return pl.pallas_call(
        body, name=name, grid=(steps,),
        in_specs=([_row_spec(tm, w, cb) for (_, cb, w) in ins] + [_full_spec(p.shape) for p in params] + [_ANY] * n_side),
        out_specs=[_row_spec(tm, w, 0) for (w, _) in outs] + [_ANY] * n_side,
        out_shape=([jax.ShapeDtypeStruct((T, w), dt) for (w, dt) in outs]
                   + ([] if side is None else _chip_exchange_shapes(*side))),
        scratch_shapes=_chip_exchange_sems(n_side) if n_side else [],
        compiler_params=_ARB,
    )(*[a for (a, _, _) in ins], *params, *side_arrays)


def _rowwise_vjp(name, fn, ins, params, cts, gdtypes, tm, T, residual=None):
    n_in, n_p, n_ct = len(ins), len(params), len(cts)
    n_res = 0 if residual is None else 1

    def body(*refs):
        i = pl.program_id(0)
        k = 0
        xs = [r[...].astype(F32) for r in refs[k:k + n_in]]
        k += n_in
        ps = [r[...] for r in refs[k:k + n_p]]
        k += n_p
        gs = tuple(r[...].astype(F32) for r in refs[k:k + n_ct])
        k += n_ct
        res_refs = refs[k:k + n_res]
        k += n_res
        gi_refs = refs[k:k + n_in]
        gp_refs = refs[k + n_in:]
        _, vjp = jax.vjp(fn, *xs, *ps)
        g = vjp(gs)
        for j, (ref, v) in enumerate(zip(gi_refs, g[:n_in])):
            if j == 0 and n_res:
                v = v + res_refs[0][...].astype(F32)
            ref[...] = v.astype(ref.dtype)

        @pl.when(i == 0)
        def _():
            for ref in gp_refs:
                ref[...] = jnp.zeros_like(ref)

        for ref, v in zip(gp_refs, g[n_in:]):
            ref[...] += v

    res_in = [] if residual is None else [residual]
    return pl.pallas_call(
        body, name=name, grid=(T // tm,),
        in_specs=([_row_spec(tm, w, cb) for (_, cb, w) in ins] + [_full_spec(p.shape) for p in params]
                  + [_row_spec(tm, w, cb) for (_, cb, w) in cts] + [_row_spec(tm, w, cb) for (_, cb, w) in res_in]),
        out_specs=[_row_spec(tm, w, 0) for (_, _, w) in ins] + [_full_spec(p.shape) for p in params],
        out_shape=([jax.ShapeDtypeStruct((T, w), dt) for (_, _, w), dt in zip(ins, gdtypes)]
                   + [jax.ShapeDtypeStruct(p.shape, F32) for p in params]),
        compiler_params=_ARB,
    )(*[a for (a, _, _) in ins], *params, *[a for (a, _, _) in cts], *[a for (a, _, _) in res_in])


def _matmul(name, a, b, mode, M, N, K, tm, tn, tk, out_dtype, a_off=0, b_off=0, residual=None, side=None, a_swiglu=None):
    nk = K // tk
    n_side = 0 if side is None else len(side[0])
    assert M % tm == 0 and N % tn == 0 and K % tk == 0
    if mode == "nn":
        assert a_off % tk == 0 and b_off % tn == 0
        a_spec = pl.BlockSpec((tm, tk), lambda i, j, k: (i, a_off // tk + k))
        b_spec = pl.BlockSpec((tk, tn), lambda i, j, k: (k, b_off // tn + j))
        dims = (((1,), (0,)), ((), ()))
    elif mode == "nt":
        assert a_off % tk == 0 and b_off == 0
        a_spec = pl.BlockSpec((tm, tk), lambda i, j, k: (i, a_off // tk + k))
        b_spec = pl.BlockSpec((tn, tk), lambda i, j, k: (j, k))
        dims = (((1,), (1,)), ((), ()))
    else:
        assert a_off % tm == 0 and b_off % tn == 0
        a_spec = pl.BlockSpec((tk, tm), lambda i, j, k: (k, a_off // tm + i))
        b_spec = pl.BlockSpec((tk, tn), lambda i, j, k: (k, b_off // tn + j))
        dims = (((0,), (0,)), ((), ()))
    has_res = residual is not None
    has_pair = a_swiglu is not None
    if has_pair:
        blk = tm if mode == "tn" else tk
        assert mode in ("nn", "tn") and a_swiglu % blk == 0
        shift = a_swiglu // blk
        a2_spec = (pl.BlockSpec((tm, tk), lambda i, j, k: (i, a_off // tk + shift + k)) if mode == "nn" else
                   pl.BlockSpec((tk, tm), lambda i, j, k: (k, a_off // tm + shift + i)))

    n_main = 2 + has_res + has_pair
    grid = (M // tm, N // tn, nk)

    def a_tile(refs):
        if not has_pair:
            return refs[0][...].astype(BF16)
        g, up = refs[0][...].astype(F32), refs[n_main - 1][...].astype(F32)
        return (_silu(g) * up).astype(BF16)

    def body(*refs):
        b_ref = refs[1]
        r_ref = refs[2] if has_res else None
        o_ref = refs[n_main + n_side]
        acc_ref = refs[n_main + 2 * n_side + 1]
        i, j, k = pl.program_id(0), pl.program_id(1), pl.program_id(2)
        if n_side:
            start, finish = _chip_exchange(refs[n_main:n_main + n_side], refs[n_main + n_side + 1:n_main + 2 * n_side + 1],
                                           *refs[n_main + 2 * n_side + 2:], gather=side[1])
            pl.when((i == 0) & (j == 0) & (k == 0))(start)

        @pl.when(k == 0)
        def _():
            acc_ref[...] = jnp.zeros_like(acc_ref)

        acc_ref[...] += lax.dot_general(a_tile(refs), b_ref[...].astype(BF16), dims, preferred_element_type=F32)

        @pl.when(k == nk - 1)
        def _():
            acc = acc_ref[...]
            if has_res:
                acc = acc + r_ref[...].astype(F32)
            o_ref[...] = acc.astype(o_ref.dtype)

        if n_side:
            pl.when((i == grid[0] - 1) & (j == grid[1] - 1) & (k == nk - 1))(finish)

    in_specs = [a_spec, b_spec]
    args = [a, b]
    if has_res:
        in_specs.append(pl.BlockSpec((tm, tn), lambda i, j, k: (i, j)))
        args.append(residual)
    if has_pair:
        in_specs.append(a2_spec)
        args.append(a)
    out_specs = pl.BlockSpec((tm, tn), lambda i, j, k: (i, j))
    out_shape = jax.ShapeDtypeStruct((M, N), out_dtype)
    scratch = [pltpu.VMEM((tm, tn), F32)]
    semantics = ("parallel", "parallel", "arbitrary")
    if n_side:
        in_specs += [_ANY] * n_side
        args += list(side[0])
        out_specs = [out_specs] + [_ANY] * n_side
        out_shape = [out_shape] + _chip_exchange_shapes(*side)
        scratch += _chip_exchange_sems(n_side)
        semantics = ("arbitrary",) * 3
    return pl.pallas_call(
        body, name=name, grid=grid, in_specs=in_specs, out_specs=out_specs, out_shape=out_shape, scratch_shapes=scratch,
        compiler_params=pltpu.CompilerParams(dimension_semantics=semantics, vmem_limit_bytes=VMEM_LIMIT),
    )(*args)


def _matmul_pieces(name, mode, pieces, other, M, N, K, tm, tn, tk, out_dtype, residual=None):
    nk = K // tk
    blk = tn if mode == "tn" else tk
    counts = [p.shape[1] // blk for p in pieces]
    assert all(p.shape[1] % blk == 0 for p in pieces) and sum(counts) == (N if mode == "tn" else K) // blk
    starts = [sum(counts[:p]) for p in range(len(pieces))]
    n_p = len(pieces)
    has_res = residual is not None

    def piece_spec(s, n):
        if mode == "tn":
            return pl.BlockSpec((tk, tn), lambda i, j, k: (jnp.where((j >= s) & (j < s + n), k, 0), jnp.clip(j - s, 0, n - 1)))
        return pl.BlockSpec((tm, tk), lambda i, j, k: (i, jnp.clip(k - s, 0, n - 1)))

    if mode == "tn":
        other_spec = pl.BlockSpec((tk, tm), lambda i, j, k: (k, i))
    else:
        other_spec = pl.BlockSpec((tn, tk), lambda i, j, k: (j, k))

    def body(*refs):
        other_ref, piece_refs = refs[0], refs[1:1 + n_p]
        r_ref = refs[1 + n_p] if has_res else None
        o_ref, acc_ref = refs[-2], refs[-1]
        k = pl.program_id(2)
        sel = pl.program_id(1) if mode == "tn" else k

        @pl.when(k == 0)
        def _():
            acc_ref[...] = jnp.zeros_like(acc_ref)

        for p_ref, s, n in zip(piece_refs, starts, counts):
            @pl.when((sel >= s) & (sel < s + n))
            def _(p_ref=p_ref):
                if mode == "tn":
                    acc_ref[...] += lax.dot_general(other_ref[...].astype(BF16), p_ref[...].astype(BF16),
                                                    (((0,), (0,)), ((), ())), preferred_element_type=F32)
                else:
                    acc_ref[...] += lax.dot_general(p_ref[...].astype(BF16), other_ref[...].astype(BF16),
                                                    (((1,), (1,)), ((), ())), preferred_element_type=F32)

        @pl.when(k == nk - 1)
        def _():
            acc = acc_ref[...]
            if has_res:
                acc = acc + r_ref[...].astype(F32)
            o_ref[...] = acc.astype(o_ref.dtype)

    in_specs = [other_spec] + [piece_spec(s, n) for s, n in zip(starts, counts)]
    args = [other] + list(pieces)
    if has_res:
        in_specs.append(pl.BlockSpec((tm, tn), lambda i, j, k: (i, j)))
        args.append(residual)
    return pl.pallas_call(
        body, name=name, grid=(M // tm, N // tn, nk),
        in_specs=in_specs,
        out_specs=pl.BlockSpec((tm, tn), lambda i, j, k: (i, j)),
        out_shape=jax.ShapeDtypeStruct((M, N), out_dtype),
        scratch_shapes=[pltpu.VMEM((tm, tn), F32)],
        compiler_params=pltpu.CompilerParams(dimension_semantics=("parallel", "parallel", "arbitrary"),
                                             vmem_limit_bytes=VMEM_LIMIT),
    )(*args)


def _f_rms(x, w):
    return (x * lax.rsqrt(jnp.mean(x * x, axis=-1, keepdims=True) + EPS) * w,)


def _silu(x):
    return x * jax.nn.sigmoid(x)


def _softplus(x):
    return jnp.maximum(x, 0.0) + jnp.log1p(jnp.exp(-jnp.abs(x)))


def _heads(x, n):
    return [x[:, HEAD_DIM * h:HEAD_DIM * (h + 1)] for h in range(n)]


def _f_dnpoint(c, ba, alog_v, dtb_v):
    cs = _silu(c)
    q, k, v = cs[:, :D_KEY_B], cs[:, D_KEY_B:2 * D_KEY_B], cs[:, 2 * D_KEY_B:]
    qn = jnp.concatenate([t * lax.rsqrt(jnp.sum(t * t, axis=-1, keepdims=True) + EPS) * (HEAD_DIM ** -0.5)
                          for t in _heads(q, N_HEADS_B)], axis=1)
    kn = jnp.concatenate([t * lax.rsqrt(jnp.sum(t * t, axis=-1, keepdims=True) + EPS)
                          for t in _heads(k, N_HEADS_B)], axis=1)
    lane = lax.broadcasted_iota(jnp.int32, ba.shape, 1)
    beta = jax.nn.sigmoid(ba)
    g = -jnp.exp(alog_v) * _softplus(ba + dtb_v)
    bg = jnp.where(lane < N_HEADS_B, beta, jnp.where(lane < 2 * N_HEADS_B, g, 0.0))
    return qn, kn, v, bg


def _dnpoint_bwd(c, ba, alog_v, dtb_v, dq, dk, dv, dbg):
    sig = jax.nn.sigmoid(c)
    cs = c * sig

    def l2_bwd(t, dy, scale):
        outs = []
        for th, dh in zip(_heads(t, N_HEADS_B), _heads(dy, N_HEADS_B)):
            r = lax.rsqrt(jnp.sum(th * th, axis=-1, keepdims=True) + EPS)
            y = th * r
            outs.append((scale * r) * (dh - y * jnp.sum(dh * y, axis=-1, keepdims=True)))
        return outs

    dcs = jnp.concatenate(l2_bwd(cs[:, :D_KEY_B], dq, HEAD_DIM ** -0.5) + l2_bwd(cs[:, D_KEY_B:2 * D_KEY_B], dk, 1.0) + [dv],
                          axis=1)
    dc = dcs * (sig * (1.0 + c * (1.0 - sig)))
    lane = lax.broadcasted_iota(jnp.int32, ba.shape, 1)
    is_beta, is_g = lane < N_HEADS_B, (lane >= N_HEADS_B) & (lane < 2 * N_HEADS_B)
    beta = jax.nn.sigmoid(ba)
    z = ba + dtb_v
    minus_ea = -jnp.exp(alog_v)
    dz = jnp.where(is_g, dbg * minus_ea * jax.nn.sigmoid(z), 0.0)
    dba = jnp.where(is_beta, dbg * beta * (1.0 - beta), dz)
    dal = jnp.sum(jnp.where(is_g, dbg * minus_ea * _softplus(z), 0.0), axis=0, keepdims=True)
    ddt = jnp.sum(dz, axis=0, keepdims=True)
    return dc, dba, dal, ddt


def _f_postdn(o, z, w):
    outs = []
    for oh, zh in zip(_heads(o, N_HEADS_B), _heads(z, N_HEADS_B)):
        outs.append(oh * lax.rsqrt(jnp.mean(oh * oh, axis=-1, keepdims=True) + EPS) * w * _silu(zh))
    return (jnp.concatenate(outs, axis=1),)


def _f_merge(o0, o1, o2, l0, l1, l2):
    m = lax.stop_gradient(jnp.maximum(jnp.maximum(l0, l1), l2))
    e0, e1, e2 = jnp.exp(l0 - m), jnp.exp(l1 - m), jnp.exp(l2 - m)
    return ((e0 * o0 + e1 * o1 + e2 * o2) / (e0 + e1 + e2),)


def _f_mix(ya, yb, gr):
    return (jax.nn.sigmoid(gr[:, :D_MODEL]) * ya + jax.nn.sigmoid(gr[:, D_MODEL:]) * yb,)


def _f_act(gu):
    return (_silu(gu[:, :D_FF]) * gu[:, D_FF:],)


def _attn_block(q, kp, kc, vp, vc, slope_d, first_key):
    k2 = jnp.concatenate([kp, kc], axis=0).astype(BF16)
    v2 = jnp.concatenate([vp, vc], axis=0).astype(BF16)
    s = lax.dot_general(q.astype(BF16), k2, (((1,), (1,)), ((), ())), preferred_element_type=F32)
    s = s * (HEAD_DIM ** -0.5)
    i = lax.broadcasted_iota(jnp.int32, (BLOCK_A, 2 * BLOCK_A), 0)
    j = lax.broadcasted_iota(jnp.int32, (BLOCK_A, 2 * BLOCK_A), 1)
    delta = BLOCK_A + i - j
    valid = (delta >= 0) & (delta <= W_SUB) & (j >= first_key)
    s = jnp.where(valid, s - slope_d * delta.astype(F32), -jnp.inf)
    m = lax.stop_gradient(jnp.max(s, axis=-1, keepdims=True))
    p = jnp.exp(s - m)
    den = jnp.sum(p, axis=-1, keepdims=True)
    o = jnp.dot(p.astype(BF16), v2, preferred_element_type=F32) / den
    lse = jnp.broadcast_to(m + jnp.log(den), (BLOCK_A, HEAD_DIM))
    return o, lse


def _slope_d(group, h, d):
    hg = (group * 4 + h + 1).astype(F32)
    return jnp.exp(jnp.full((1, 1), -8.0 * math.log(2.0) / N_HEADS_A, F32) * hg) * float(d)


ATT_CFG = ((1, 4, 4), (4, 2, 1), (16, 1, 1))
D_GROUP = 1536


def _att_columns():
    out = []
    for g, (_, _, hp) in enumerate(ATT_CFG):
        lanes = HEAD_DIM * hp
        for hb in range(D_ATTN_OUT // lanes):
            for which in range(3):
                out.append((N_HEADS_A * HEAD_DIM * which + D_ATTN_OUT * g + lanes * hb, lanes))
    return out
R_UNROLL = 4


def _attn_geometry(group, T):
    d, m, hp = ATT_CFG[group]
    span = BLOCK_A * d
    tile = m * span
    assert T % tile == 0 and 4 % hp == 0
    return d, m, hp, span, tile, T // tile, HEAD_DIM * hp


def _sub_rows(b, span, r, d):
    return pl.ds(b * span + r, BLOCK_A, stride=d) if d > 1 else pl.ds(b * span, BLOCK_A)


def _for_each_r(d, fn):
    if d <= R_UNROLL:
        for r in range(d):
            fn(r)
    else:
        def step(r, carry):
            fn(r)
            return carry

        lax.fori_loop(0, d, step, 0, unroll=R_UNROLL)


def _attn_fwd(name, proj, group, T):
    d, m, hp, span, tile, nt, lanes = _attn_geometry(group, T)
    qb, kb, vb = [(D_GROUP * group) // lanes + which for which in range(3)]

    def body(q_ref, kc_ref, kp_ref, vc_ref, vp_ref, o_ref, l_ref):
        hb, n = pl.program_id(0), pl.program_id(1)
        first_key = jnp.where(n > 0, 0, BLOCK_A)
        for hh in range(hp):
            ls = slice(HEAD_DIM * hh, HEAD_DIM * (hh + 1))
            sl = _slope_d(group, hb * hp + hh, d)
            for b in range(m):
                def one(r, b=b, ls=ls, sl=sl):
                    rq = _sub_rows(b, span, r, d)
                    if b == 0:
                        rp = _sub_rows(0, span, r, d)
                        kp, vp, fk = kp_ref[rp, ls], vp_ref[rp, ls], first_key
                    else:
                        rp = _sub_rows(b - 1, span, r, d)
                        kp, vp, fk = kc_ref[rp, ls], vc_ref[rp, ls], 0
                    o, lse = _attn_block(q_ref[rq, ls], kp, kc_ref[rq, ls], vp, vc_ref[rq, ls], sl, fk)
                    o_ref[rq, ls] = o
                    l_ref[rq, ls] = lse

                _for_each_r(d, one)

    def cur(cb):
        return pl.BlockSpec((tile, lanes), lambda hb, n: (n, cb + 3 * hb))

    def prev(cb):
        return pl.BlockSpec((span, lanes), lambda hb, n: (jnp.maximum(m * n - 1, 0), cb + 3 * hb))

    return pl.pallas_call(
        body, name=name, grid=(4 // hp, nt),
        in_specs=[cur(qb), cur(kb), prev(kb), cur(vb), prev(vb)],
        out_specs=[pl.BlockSpec((tile, lanes), lambda hb, n: (n, hb))] * 2,
        out_shape=[jax.ShapeDtypeStruct((T, D_ATTN_OUT), F32)] * 2,
        compiler_params=_cparams(2),
    )(proj, proj, proj, proj, proj)


def _attn_block_bwd(q, kp, kc, vp, vc, o, lse, do, dl, slope_d, first_key):
    scale = HEAD_DIM ** -0.5
    nt_dims = (((1,), (1,)), ((), ()))
    tn_dims = (((0,), (0,)), ((), ()))
    qb = q.astype(BF16)
    k2 = jnp.concatenate([kp, kc], axis=0).astype(BF16)
    v2 = jnp.concatenate([vp, vc], axis=0).astype(BF16)
    dob = do.astype(BF16)
    s = lax.dot_general(qb, k2, nt_dims, preferred_element_type=F32) * scale
    i = lax.broadcasted_iota(jnp.int32, (BLOCK_A, 2 * BLOCK_A), 0)
    j = lax.broadcasted_iota(jnp.int32, (BLOCK_A, 2 * BLOCK_A), 1)
    delta = BLOCK_A + i - j
    valid = (delta >= 0) & (delta <= W_SUB) & (j >= first_key)
    lse_col = jnp.max(lse, axis=-1, keepdims=True)
    p = jnp.exp(jnp.where(valid, s - slope_d * delta.astype(F32), -jnp.inf) - lse_col)
    dp = lax.dot_general(dob, v2, nt_dims, preferred_element_type=F32)
    row = jnp.sum(dl, axis=-1, keepdims=True) - jnp.sum(do * o, axis=-1, keepdims=True)
    ds = (p * (dp + row)).astype(BF16)
    dv2 = lax.dot_general(p.astype(BF16), dob, tn_dims, preferred_element_type=F32)
    dq = jnp.dot(ds, k2, preferred_element_type=F32) * scale
    dk2 = lax.dot_general(ds, qb, tn_dims, preferred_element_type=F32) * scale
    return dq, dk2[:BLOCK_A], dk2[BLOCK_A:], dv2[:BLOCK_A], dv2[BLOCK_A:]


def _attn_bwd(name, proj, o, lse, do, dl, group, T, dest):
    d, m, hp, span, tile, nt, lanes = _attn_geometry(group, T)
    qb, kb, vb = [(D_GROUP * group) // lanes + which for which in range(3)]
    n_in = 9 if dest is None else 10

    def body(*refs):
        q_ref, kc_ref, kp_ref, vc_ref, vp_ref, o_ref, l_ref, do_ref, dl_ref = refs[:9]
        out_ref, aq_ref, ak_ref, av_ref = refs[n_in:]
        hb, n = pl.program_id(0), pl.program_id(1)
        first_key = jnp.where(n > 0, 0, BLOCK_A)
        cur_slot = n % 2
        old_slot = 1 - cur_slot

        @pl.when(n == 0)
        def _():
            aq_ref[...] = jnp.zeros_like(aq_ref)
            ak_ref[...] = jnp.zeros_like(ak_ref)
            av_ref[...] = jnp.zeros_like(av_ref)

        @pl.when(n < nt)
        def _():
            for hh in range(hp):
                ls = slice(HEAD_DIM * hh, HEAD_DIM * (hh + 1))
                sl = _slope_d(group, hb * hp + hh, d)
                for b in range(m):
                    def one(r, b=b, ls=ls, sl=sl):
                        rq = _sub_rows(b, span, r, d)
                        if b == 0:
                            rp = _sub_rows(0, span, r, d)
                            kp, vp, fk = kp_ref[rp, ls], vp_ref[rp, ls], first_key
                        else:
                            rp = _sub_rows(b - 1, span, r, d)
                            kp, vp, fk = kc_ref[rp, ls], vc_ref[rp, ls], 0
                        dq, dkp, dkc, dvp, dvc = _attn_block_bwd(
                            q_ref[rq, ls], kp, kc_ref[rq, ls], vp, vc_ref[rq, ls], o_ref[rq, ls], l_ref[rq, ls],
                            do_ref[rq, ls], dl_ref[rq, ls], sl, fk)
                        aq_ref[cur_slot, rq, ls] = dq
                        ak_ref[cur_slot, rq, ls] = dkc
                        av_ref[cur_slot, rq, ls] = dvc
                        if b == 0:
                            last = _sub_rows(m - 1, span, r, d)
                            ak_ref[old_slot, last, ls] += dkp
                            av_ref[old_slot, last, ls] += dvp
                        else:
                            ak_ref[cur_slot, rp, ls] += dkp
                            av_ref[cur_slot, rp, ls] += dvp

                    _for_each_r(d, one)

        out_ref[:, 0:lanes] = aq_ref[old_slot].astype(BF16)
        out_ref[:, lanes:2 * lanes] = ak_ref[old_slot].astype(BF16)
        out_ref[:, 2 * lanes:3 * lanes] = av_ref[old_slot].astype(BF16)

    def cur(cb):
        return pl.BlockSpec((tile, lanes), lambda hb, n: (jnp.minimum(n, nt - 1), cb + 3 * hb))

    def prev(cb):
        return pl.BlockSpec((span, lanes), lambda hb, n: (jnp.maximum(m * jnp.minimum(n, nt - 1) - 1, 0), cb + 3 * hb))

    own = pl.BlockSpec((tile, lanes), lambda hb, n: (jnp.minimum(n, nt - 1), hb))
    late = pl.BlockSpec((tile, 3 * lanes), lambda hb, n: (jnp.maximum(n - 1, 0), (D_GROUP * group) // (3 * lanes) + hb))
    acc = pltpu.VMEM((2, tile, lanes), F32)
    return pl.pallas_call(
        body, name=name, grid=(4 // hp, nt + 1),
        in_specs=[cur(qb), cur(kb), prev(kb), cur(vb), prev(vb), own, own, own, own] + ([] if dest is None else [_ANY]),
        out_specs=late,
        out_shape=jax.ShapeDtypeStruct((T, D_PA), BF16),
        input_output_aliases={} if dest is None else {9: 0},
        scratch_shapes=[acc, acc, acc],
        compiler_params=_cparams(2),
    )(proj, proj, proj, proj, proj, o, lse, do, dl, *([] if dest is None else [dest]))


HALO = 8
HALO_IN = 16


def _shifted(ext):
    return [ext[HALO:, :]] + [pltpu.roll(ext, k, 0)[HALO:, :] for k in range(1, CONV_WIDTH)]


def _conv_taps(sh, w):
    c = sh[0] * w[CONV_WIDTH - 1]
    for k in range(1, CONV_WIDTH):
        c = c + sh[k] * w[CONV_WIDTH - 1 - k]
    return c


def _taps(w_ref):
    return [w_ref[j:j + 1, :] for j in range(CONV_WIDTH)]


def _dn_pre_fwd(name, proj, ba, conv_w, alog_v, dtb_v, tm, T):
    C = 3 * D_KEY_B

    def body(x_ref, ba_ref, w_ref, al_ref, dt_ref, q_ref, k_ref, v_ref, bg_ref, halo_ref):
        i = pl.program_id(0)

        @pl.when(i == 0)
        def _():
            halo_ref[...] = jnp.zeros_like(halo_ref)

        x = x_ref[...].astype(F32)
        ext = jnp.concatenate([halo_ref[...], x], axis=0)
        c = _conv_taps(_shifted(ext), _taps(w_ref))
        halo_ref[...] = x[tm - HALO:, :]
        qn, kn, v, bg = _f_dnpoint(c, ba_ref[...], al_ref[...], dt_ref[...])
        q_ref[...] = qn
        k_ref[...] = kn
        v_ref[...] = v
        bg_ref[...] = bg

    return pl.pallas_call(
        body, name=name, grid=(T // tm,),
        in_specs=[_row_spec(tm, C, 0), _row_spec(tm, 128, 0), _full_spec(conv_w.shape), _full_spec((1, 128)),
                  _full_spec((1, 128))],
        out_specs=[_row_spec(tm, D_KEY_B, 0)] * 3 + [_row_spec(tm, 128, 0)],
        out_shape=[jax.ShapeDtypeStruct((T, D_KEY_B), F32)] * 3 + [jax.ShapeDtypeStruct((T, 128), F32)],
        scratch_shapes=[pltpu.VMEM((HALO, C), F32)],
        compiler_params=_ARB,
    )(proj, ba, conv_w, alog_v, dtb_v)


def _dn_pre_bwd(name, proj, ba, conv_w, alog_v, dtb_v, dq, dk, dv, dbg, tm, T):
    C = 3 * D_KEY_B
    nt = T // tm
    hb = tm // HALO_IN

    def body(x_ref, xh_ref, ba_ref, w_ref, al_ref, dt_ref, dq_ref, dk_ref, dv_ref, dbg_ref,
             dx_ref, dba_ref, dw_ref, dal_ref, ddt_ref, nxt_ref):
        s = pl.program_id(0)
        i = nt - 1 - s

        @pl.when(s == 0)
        def _():
            nxt_ref[...] = jnp.zeros_like(nxt_ref)
            dw_ref[...] = jnp.zeros_like(dw_ref)
            dal_ref[...] = jnp.zeros_like(dal_ref)
            ddt_ref[...] = jnp.zeros_like(ddt_ref)

        w = _taps(w_ref)
        halo = jnp.where(i > 0, xh_ref[...].astype(F32)[HALO_IN - HALO:, :], 0.0)
        ext = jnp.concatenate([halo, x_ref[...].astype(F32)], axis=0)
        sh = _shifted(ext)
        c = _conv_taps(sh, w)
        dc, dba, dal, ddt = _dnpoint_bwd(c, ba_ref[...], al_ref[...], dt_ref[...],
                                         dq_ref[...].astype(F32), dk_ref[...].astype(F32), dv_ref[...].astype(F32), dbg_ref[...])
        dba_ref[...] = dba
        dal_ref[...] += dal
        ddt_ref[...] += ddt
        ext2 = jnp.concatenate([dc, nxt_ref[...]], axis=0)
        dx = dc * w[CONV_WIDTH - 1]
        for m in range(1, CONV_WIDTH):
            dx = dx + pltpu.roll(ext2, tm + HALO - m, 0)[:tm, :] * w[CONV_WIDTH - 1 - m]
        dx_ref[...] = dx.astype(dx_ref.dtype)
        nxt_ref[...] = dc[:HALO, :]
        for j in range(CONV_WIDTH):
            dw_ref[j:j + 1, :] += jnp.sum(sh[CONV_WIDTH - 1 - j] * dc, axis=0, keepdims=True)

    def rev(w):
        return pl.BlockSpec((tm, w), lambda s: (nt - 1 - s, 0))

    return pl.pallas_call(
        body, name=name, grid=(nt,),
        in_specs=[rev(C), pl.BlockSpec((HALO_IN, C), lambda s: (jnp.maximum((nt - 1 - s) * hb - 1, 0), 0)), rev(128),
                  _full_spec(conv_w.shape), _full_spec((1, 128)), _full_spec((1, 128)),
                  rev(D_KEY_B), rev(D_KEY_B), rev(D_KEY_B), rev(128)],
        out_specs=[rev(C), rev(128), _full_spec((8, C)), _full_spec((1, 128)), _full_spec((1, 128))],
        out_shape=[jax.ShapeDtypeStruct((T, C), BF16), jax.ShapeDtypeStruct((T, 128), F32),
                   jax.ShapeDtypeStruct((8, C), F32), jax.ShapeDtypeStruct((1, 128), F32),
                   jax.ShapeDtypeStruct((1, 128), F32)],
        scratch_shapes=[pltpu.VMEM((HALO, C), F32)],
        compiler_params=_ARB,
    )(proj, proj, ba, conv_w, alog_v, dtb_v, dq, dk, dv, dbg)


def _bdot(a, b, dims, exact=False):
    if exact:
        return lax.dot_general(a, b, dims, preferred_element_type=F32, precision=lax.Precision.HIGH)
    return lax.dot_general(a.astype(BF16), b.astype(BF16), dims, preferred_element_type=F32)


_B_NN = (((2,), (1,)), ((0,), (0,)))
_B_NT = (((2,), (2,)), ((0,), (0,)))
_B_TN = (((1,), (1,)), ((0,), (0,)))


def _unit_lower_inverse(a):
    C = a.shape[-1]
    r = lax.broadcasted_iota(jnp.int32, (C, C), 0)
    c = lax.broadcasted_iota(jnp.int32, (C, C), 1)
    pw = -a
    t = jnp.where((r == c)[None], 1.0, 0.0) + pw
    for _ in range(int(math.log2(C)) - 1):
        pw = _bdot(pw, pw, _B_NN, exact=True)
        t = t + _bdot(t, pw, _B_NN, exact=True)
    return t


@jax.custom_vjp
def _solve_given_inverse(t, a, rhs):
    return _bdot(t, rhs, _B_NN, exact=True)


def _solve_fwd(t, a, rhs):
    sol = _bdot(t, rhs, _B_NN, exact=True)
    return sol, (t, sol)


def _solve_bwd(res, dsol):
    t, sol = res
    drhs = _bdot(t, dsol, _B_TN, exact=True)
    return jnp.zeros_like(t), -_bdot(drhs, sol, _B_NT, exact=True), drhs


_solve_given_inverse.defvjp(_solve_fwd, _solve_bwd)


def _stack(xs):
    return jnp.concatenate([x[None] for x in xs], axis=0)


def _delta_chunk(S, q, k, v, bg, t_inv=None):
    H, C = N_HEADS_B, CHUNK
    r = lax.broadcasted_iota(jnp.int32, (C, C), 0)
    c = lax.broadcasted_iota(jnp.int32, (C, C), 1)
    incl, strict, eye = r >= c, r > c, r == c
    gc_all = jnp.dot(incl.astype(F32), bg, preferred_element_type=F32, precision=lax.Precision.HIGHEST)
    lane = lax.broadcasted_iota(jnp.int32, (C, 128), 1)

    def col(x, l):
        return jnp.sum(jnp.where(lane == l, x, 0.0), axis=1, keepdims=True)

    beta = _stack([col(bg, h) for h in range(H)])
    gc = _stack([col(gc_all, H + h) for h in range(H)])
    gc_row = jnp.sum(jnp.where(eye[None], jnp.broadcast_to(gc, (H, C, C)), 0.0), axis=1, keepdims=True)
    row = lax.broadcasted_iota(jnp.int32, (H, C, 1), 1)
    gc_last = jnp.sum(jnp.where(row == C - 1, gc, 0.0), axis=1, keepdims=True)
    decay = jnp.exp(jnp.where(incl[None], gc - gc_row, -jnp.inf))
    qh = _stack(_heads(q, H))
    kh = _stack(_heads(k, H))
    vh = _stack(_heads(v, H))
    kk = _bdot(kh, kh, _B_NT, exact=True)
    a = jnp.where(strict[None], beta * kk * decay, 0.0)
    egc = jnp.exp(gc)
    rhs = jnp.concatenate([beta * vh, (beta * egc) * kh], axis=-1)
    if t_inv is None:
        t_inv = _unit_lower_inverse(a)
        sol = _bdot(t_inv, rhs, _B_NN, exact=True)
    else:
        sol = _solve_given_inverse(t_inv, a, rhs)
    u_bar, w = sol[..., :HEAD_DIM], sol[..., HEAD_DIM:]
    qk = _bdot(qh, kh, _B_NT) * decay
    q_dec = qh * egc
    k_dec = kh * jnp.exp(gc_last - gc)
    u = u_bar - _bdot(w, S, _B_NN)
    o = _bdot(q_dec, S, _B_NN) + _bdot(qk, u, _B_NN)
    S_new = jnp.exp(gc_last) * S + _bdot(k_dec, u, _B_TN)
    o2 = jnp.concatenate([lax.index_in_dim(o, h, 0, keepdims=False) for h in range(H)], axis=1)
    return o2, S_new, t_inv


def _delta_fwd(name, q, k, v, bg, T):
    nc = T // CHUNK
    H = N_HEADS_B
    G, R = CHUNKS_FWD, CHUNKS_FWD * CHUNK

    def body(q_ref, k_ref, v_ref, bg_ref, o_ref, s_out_ref, t_out_ref, s_ref):
        @pl.when(pl.program_id(0) == 0)
        def _():
            s_ref[...] = jnp.zeros_like(s_ref)

        S = s_ref[...]
        for c in range(G):
            rows = slice(c * CHUNK, (c + 1) * CHUNK)
            s_out_ref[c] = S
            o, S, t_inv = _delta_chunk(S, q_ref[rows, :], k_ref[rows, :], v_ref[rows, :], bg_ref[rows, :])
            o_ref[rows, :] = o.astype(o_ref.dtype)
            t_out_ref[c] = t_inv
        s_ref[...] = S

    return pl.pallas_call(
        body, name=name, grid=(nc // G,),
        in_specs=[_row_spec(R, D_KEY_B, 0)] * 3 + [_row_spec(R, 128, 0)],
        out_specs=[_row_spec(R, D_KEY_B, 0), pl.BlockSpec((G, H, HEAD_DIM, HEAD_DIM), lambda i: (i, 0, 0, 0)),
                   pl.BlockSpec((G, H, CHUNK, CHUNK), lambda i: (i, 0, 0, 0))],
        out_shape=[jax.ShapeDtypeStruct((T, D_KEY_B), BF16), jax.ShapeDtypeStruct((nc, H, HEAD_DIM, HEAD_DIM), F32),
                   jax.ShapeDtypeStruct((nc, H, CHUNK, CHUNK), F32)],
        scratch_shapes=[pltpu.VMEM((H, HEAD_DIM, HEAD_DIM), F32)],
        compiler_params=_ARB,
    )(q, k, v, bg)


def _delta_bwd(name, q, k, v, bg, s_all, t_all, do, T):
    nc = T // CHUNK
    H = N_HEADS_B
    G, R = CHUNKS_BWD, CHUNKS_BWD * CHUNK
    ns = nc // G

    def body(q_ref, k_ref, v_ref, bg_ref, s_in_ref, t_ref, do_ref, dq_ref, dk_ref, dv_ref, dbg_ref, ds_ref):
        @pl.when(pl.program_id(0) == 0)
        def _():
            ds_ref[...] = jnp.zeros_like(ds_ref)

        dS = ds_ref[...]
        for c in reversed(range(G)):
            rows = slice(c * CHUNK, (c + 1) * CHUNK)
            t_inv = t_ref[c]

            def f(S, qq, kk, vv, bb, t_inv=t_inv):
                return _delta_chunk(S, qq, kk, vv, bb, t_inv)[:2]

            _, vjp = jax.vjp(f, s_in_ref[c], q_ref[rows, :], k_ref[rows, :], v_ref[rows, :], bg_ref[rows, :])
            dS, dq, dk, dv, dbg = vjp((do_ref[rows, :].astype(F32), dS))
            dq_ref[rows, :] = dq.astype(dq_ref.dtype)
            dk_ref[rows, :] = dk.astype(dk_ref.dtype)
            dv_ref[rows, :] = dv.astype(dv_ref.dtype)
            dbg_ref[rows, :] = dbg
        ds_ref[...] = dS

    def rev(w):
        return pl.BlockSpec((R, w), lambda s: (ns - 1 - s, 0))

    def rev4(n):
        return pl.BlockSpec((G, H, n, n), lambda s: (ns - 1 - s, 0, 0, 0))

    return pl.pallas_call(
        body, name=name, grid=(ns,),
        in_specs=[rev(D_KEY_B)] * 3 + [rev(128), rev4(HEAD_DIM), rev4(CHUNK), rev(D_KEY_B)],
        out_specs=[rev(D_KEY_B)] * 3 + [rev(128)],
        out_shape=[jax.ShapeDtypeStruct((T, D_KEY_B), F32)] * 3 + [jax.ShapeDtypeStruct((T, 128), F32)],
        scratch_shapes=[pltpu.VMEM((H, HEAD_DIM, HEAD_DIM), F32)],
        compiler_params=_ARB,
    )(q, k, v, bg, s_all, t_all, do)


def _loss_fb(name, h2, w, tgt, tm, T):
    def tile_loss(h, wv, t):
        y = _f_rms(h, wv)[0]
        e = y - t
        return 0.5 * jnp.sum(jnp.mean(e * e, axis=-1))

    def body(h_ref, w_ref, t_ref, dh_ref, loss_ref, dw_ref):
        @pl.when(pl.program_id(0) == 0)
        def _():
            loss_ref[...] = jnp.zeros_like(loss_ref)
            dw_ref[...] = jnp.zeros_like(dw_ref)

        val, (dh, dw) = jax.value_and_grad(tile_loss, argnums=(0, 1))(h_ref[...], w_ref[...], t_ref[...])
        dh_ref[...] = dh
        dw_ref[...] += dw
        loss_ref[...] += jnp.reshape(val, (1, 1))

    return pl.pallas_call(
        body, name=name, grid=(T // tm,),
        in_specs=[_row_spec(tm, D_MODEL, 0), _full_spec((1, D_MODEL)), _row_spec(tm, D_MODEL, 0)],
        out_specs=[_row_spec(tm, D_MODEL, 0), _full_spec((1, 1)), _full_spec((1, D_MODEL))],
        out_shape=[jax.ShapeDtypeStruct((T, D_MODEL), F32), jax.ShapeDtypeStruct((1, 1), F32),
                   jax.ShapeDtypeStruct((1, D_MODEL), F32)],
        compiler_params=_ARB,
    )(h2, w, tgt)


def _adamw(name, w, g, m, v, tile):
    R = w.shape[0]

    def body(w_ref, g_ref, m_ref, v_ref, d_out, m_out, v_out):
        g = g_ref[...]
        mm = ADAM_B1 * m_ref[...] + (1.0 - ADAM_B1) * g
        vv = ADAM_B2 * v_ref[...] + (1.0 - ADAM_B2) * jnp.square(g)
        m_hat = mm / (1.0 - ADAM_B1 ** ADAM_STEP)
        v_hat = vv / (1.0 - ADAM_B2 ** ADAM_STEP)
        d_out[...] = -ADAM_LR * (m_hat / (jnp.sqrt(v_hat) + ADAM_EPS) + ADAM_WD * w_ref[...])
        m_out[...] = mm
        v_out[...] = vv

    assert R % tile == 0
    spec = pl.BlockSpec((tile, w.shape[1]), lambda i: (i, 0))
    return pl.pallas_call(
        body, name=name, grid=(R // tile,), in_specs=[spec] * 4, out_specs=[spec] * 3,
        out_shape=[jax.ShapeDtypeStruct(w.shape, F32)] * 3, compiler_params=_ARB,
    )(w, g, m, v)


def _sum4(name, r, tile):
    _, R, C = r.shape
    assert R % tile == 0

    def body(r_ref, o_ref):
        o_ref[...] = ((r_ref[0].astype(F32) + r_ref[1].astype(F32)) + r_ref[2].astype(F32)) + r_ref[3].astype(F32)

    return pl.pallas_call(
        body, name=name, grid=(R // tile,),
        in_specs=[pl.BlockSpec((4, tile, C), lambda i: (0, i, 0))],
        out_specs=pl.BlockSpec((tile, C), lambda i: (i, 0)),
        out_shape=jax.ShapeDtypeStruct((R, C), F32), compiler_params=_ARB,
    )(r)


def _add2(name, a, b, tile):
    R, C = a.shape
    assert R % tile == 0

    def body(a_ref, b_ref, o_ref):
        o_ref[...] = (a_ref[...].astype(F32) + b_ref[...].astype(F32)).astype(BF16)

    spec = pl.BlockSpec((tile, C), lambda i: (i, 0))
    return pl.pallas_call(
        body, name=name, grid=(R // tile,), in_specs=[spec, spec], out_specs=spec,
        out_shape=jax.ShapeDtypeStruct((R, C), BF16), compiler_params=_ARB,
    )(a, b)


_ANY = pl.BlockSpec(memory_space=pl.ANY)


def _place():
    return lax.axis_index("x"), lax.axis_index("y"), lax.axis_index("c")


def _chip_exchange(in_refs, out_refs, send_sems, recv_sems, local_sems, gather):
    n = len(in_refs)
    x, y, c = _place()
    me = 2 * x + y
    chips = [(1 - x, y), (x, 1 - y), (1 - x, 1 - y)]

    def src(a, slot):
        return in_refs[a] if gather else in_refs[a].at[slot]

    def copy(a, k, src_slot, dst_slot, px, py):
        return pltpu.make_async_remote_copy(src_ref=src(a, src_slot), dst_ref=out_refs[a].at[dst_slot],
                                            send_sem=send_sems.at[3 * a + k], recv_sem=recv_sems.at[3 * a + k],
                                            device_id=(px, py, c), device_id_type=MESH)

    def mine():
        return [pltpu.make_async_copy(src(a, me), out_refs[a].at[me], local_sems.at[a]) for a in range(n)]

    def sends():
        return [copy(a, k, 2 * px + py, me, px, py) for a in range(n) for k, (px, py) in enumerate(chips)]

    def start():
        for cp in mine() + sends():
            cp.start()

    def finish():
        for a in range(n):
            for k, (px, py) in enumerate(chips):
                copy(a, k, me, 2 * px + py, px, py).wait_recv()
        for cp in sends():
            cp.wait_send()
        for cp in mine():
            cp.wait()

    return start, finish


def _chip_exchange_shapes(arrays, gather):
    return [jax.ShapeDtypeStruct(((4,) + a.shape) if gather else a.shape, a.dtype) for a in arrays]


def _chip_exchange_sems(n):
    return [pltpu.SemaphoreType.DMA((3 * n,)), pltpu.SemaphoreType.DMA((3 * n,)), pltpu.SemaphoreType.DMA((n,))]


def _exchange_chips(name, arrays, gather):
    n = len(arrays)

    def body(*refs):
        start, finish = _chip_exchange(refs[:n], refs[n:2 * n], *refs[2 * n:], gather=gather)
        start()
        finish()

    return pl.pallas_call(
        body, name=name, in_specs=[_ANY] * n, out_specs=[_ANY] * n,
        out_shape=_chip_exchange_shapes(arrays, gather), scratch_shapes=_chip_exchange_sems(n),
    )(*arrays)


def _swap_sibling(name, arrays):
    n = len(arrays)

    def body(*refs):
        in_refs, out_refs, send_sems, recv_sems = refs[:n], refs[n:2 * n], refs[2 * n], refs[2 * n + 1]
        x, y, c = _place()
        cps = [pltpu.make_async_remote_copy(src_ref=in_refs[a], dst_ref=out_refs[a], send_sem=send_sems.at[a],
                                            recv_sem=recv_sems.at[a], device_id=(x, y, 1 - c), device_id_type=MESH)
               for a in range(n)]
        for cp in cps:
            cp.start()
        for cp in cps:
            cp.wait()

    return pl.pallas_call(
        body, name=name, in_specs=[_ANY] * n, out_specs=[_ANY] * n,
        out_shape=[jax.ShapeDtypeStruct(a.shape, a.dtype) for a in arrays],
        scratch_shapes=[pltpu.SemaphoreType.DMA((n,)), pltpu.SemaphoreType.DMA((n,))],
    )(*arrays)


def _allsum_small(name, pack):
    def body(p_ref, out_ref, slots, send_sems, recv_sems):
        x, y, c = _place()
        me = 4 * x + 2 * y + c
        slots[me] = p_ref[...]
        sends = []
        for k in range(1, 8):
            px = 1 - x if k & 4 else x
            py = 1 - y if k & 2 else y
            pc = 1 - c if k & 1 else c
            cp = pltpu.make_async_remote_copy(src_ref=p_ref, dst_ref=slots.at[me], send_sem=send_sems.at[k - 1],
                                              recv_sem=recv_sems.at[k - 1], device_id=(px, py, pc), device_id_type=MESH)
            cp.start()
            sends.append(cp)
        for k in range(1, 8):
            px = 1 - x if k & 4 else x
            py = 1 - y if k & 2 else y
            pc = 1 - c if k & 1 else c
            pltpu.make_async_remote_copy(src_ref=p_ref, dst_ref=slots.at[4 * px + 2 * py + pc],
                                         send_sem=send_sems.at[k - 1], recv_sem=recv_sems.at[k - 1],
                                         device_id=(px, py, pc), device_id_type=MESH).wait_recv()
        for cp in sends:
            cp.wait_send()
        acc = slots[0]
        for s in range(1, 8):
            acc = acc + slots[s]
        out_ref[...] = acc

    vm = pl.BlockSpec(memory_space=pltpu.VMEM)
    return pl.pallas_call(
        body, name=name, in_specs=[vm], out_specs=vm,
        out_shape=jax.ShapeDtypeStruct(pack.shape, pack.dtype),
        scratch_shapes=[pltpu.VMEM((8,) + pack.shape, pack.dtype), pltpu.SemaphoreType.DMA((7,)),
                        pltpu.SemaphoreType.DMA((7,))],
    )(pack)


def _pack_shards(parts, rows_total):
    flat = [p.reshape(-1, D_MODEL) for p in parts]
    used = sum(f.shape[0] for f in flat)
    return jnp.concatenate(flat + [jnp.zeros((rows_total - used, D_MODEL), flat[0].dtype)], axis=0)


def _unpack_shards(buf, shapes):
    out, r0 = [], 0
    for shp in shapes:
        n = math.prod(shp) // D_MODEL
        out.append(buf[r0:r0 + n].reshape(shp))
        r0 += n
    return out


def kernel(x, norm_mix, w_in, conv_w, a_log, dt_bias, dn_norm, w_proj_attn, w_proj_delta, w_out, norm_ffn, w_gate, w_up, w_down, norm_final, loss_target, m_norm_mix, m_w_in, m_conv_w, m_a_log, m_dt_bias, m_dn_norm, m_w_proj_attn, m_w_proj_delta, m_w_out, m_norm_ffn, m_w_gate, m_w_up, m_w_down, m_norm_final, v_norm_mix, v_w_in, v_conv_w, v_a_log, v_dt_bias, v_dn_norm, v_w_proj_attn, v_w_proj_delta, v_w_out, v_norm_ffn, v_w_gate, v_w_up, v_w_down, v_norm_final):
    T = x.shape[1]
    D = D_MODEL
    tm = 512
    tw = 256
    mt = 1024 if T % 1024 == 0 else 512
    x2 = x.reshape(T, D)
    tgt = loss_target.reshape(T, D)

    packed = (w_proj_attn, w_proj_delta, w_out, w_gate, w_up, w_down)
    conv_bits = lax.bitcast_convert_type(conv_w[0], BF16)
    w16 = _pack_shards([w[0].astype(BF16) for w in packed] + [conv_bits], PACK_R)
    win16 = w_in[0].astype(BF16)
    my_c = lax.axis_index("c")

    def both_halves(mine, other, axis):
        return jnp.concatenate([jnp.where(my_c == 0, mine, other), jnp.where(my_c == 0, other, mine)], axis=axis)

    pack_half = lax.dynamic_slice_in_dim(w16, my_c * HALF, HALF, axis=0)
    win_half = lax.dynamic_slice_in_dim(win16, my_c * W_IN_HALF, W_IN_HALF, axis=0)
    u, win_mine = _rowwise("rms_mix", _f_rms, [(x2, 0, D)], [norm_mix], [(D, BF16)], tm, T, side=([win_half], True))
    (win_other,) = _swap_sibling("swap_w_in", [win_mine])
    win4 = both_halves(win_mine, win_other, 1)

    def win_cols(g0, g1):
        out = []
        for j in range(4):
            lo, hi = max(g0, W_IN_COLS * j), min(g1, W_IN_COLS * (j + 1))
            if lo < hi:
                out.append(win4[j][:, lo - W_IN_COLS * j:hi - W_IN_COLS * j])
        return out

    Wa = jnp.concatenate([p for c0, w in _att_columns() for p in win_cols(c0, c0 + w)], axis=1)
    Wb = jnp.concatenate(win_cols(4608, 8704) + win_cols(8720, D_IN), axis=1)
    Wba = jnp.pad(jnp.concatenate(win_cols(8704, 8720), axis=1), ((0, 0), (0, 112)))
    alog_v = jnp.pad(a_log, ((0, 0), (8, 112)))
    dtb_v = jnp.pad(dt_bias, ((0, 0), (8, 112)))

    proj_a, pack_mine = _matmul("mm_in_a", u, Wa, "nn", T, D_PA, D, mt, 2304, D, F32, side=([pack_half], True))
    (pack_other,) = _swap_sibling("swap_pack", [pack_mine])
    wg = both_halves(pack_mine, pack_other, 1)

    def seg(r0, n):
        return wg[:, r0:r0 + n, :]

    r = 0
    Wpa = seg(r, 128).reshape(4, D_ATTN_OUT, 256).transpose(1, 0, 2).reshape(D_ATTN_OUT, D); r += 128
    Wpd = seg(r, 256).reshape(D, D); r += 256
    Wout = seg(r, 256).reshape(D, D); r += 256
    Wg = seg(r, 704).reshape(4, D, 704).transpose(1, 0, 2).reshape(D, D_FF); r += 704
    Wu = seg(r, 704).reshape(4, D, 704).transpose(1, 0, 2).reshape(D, D_FF); r += 704
    Wd = seg(r, 704).reshape(D_FF, D); r += 704
    convw = lax.bitcast_convert_type(seg(r, 6).reshape(4, CONV_WIDTH, 768, 2), F32)
    convw = convw.transpose(1, 0, 2).reshape(CONV_WIDTH, 3 * D_KEY_B)
    Wgu = jnp.concatenate([Wg, Wu], axis=1)
    proj_b = _matmul("mm_in_b", u, Wb, "nn", T, D_PB, D, mt, 3072, D, BF16)
    ba = _matmul("mm_ba", u, Wba, "nn", T, 128, D, mt, 128, D, F32)

    att = [_attn_fwd(f"attn_fwd{g}", proj_a, g, T) for g in range(3)]
    (ya_m,) = _rowwise("merge", _f_merge, [(att[g][0], 0, 512) for g in range(3)] + [(att[g][1], 0, 512) for g in range(3)],
                       [], [(512, BF16)], tm, T)
    qn, kn, vd, bg = _dn_pre_fwd("dn_pre_fwd", proj_b, ba, convw, alog_v, dtb_v, tw, T)
    o_d, s_all, t_all = _delta_fwd("delta_fwd", qn, kn, vd, bg, T)
    (od,) = _rowwise("post_dn", _f_postdn, [(o_d, 0, D), (proj_b, C_Z // D, D)], [dn_norm], [(D, BF16)], tm, T)
    y_a = _matmul("mm_pa", ya_m, Wpa, "nn", T, D, 512, mt, D, 512, BF16)
    y_b = _matmul("mm_pd", od, Wpd, "nn", T, D, D, 2 * mt, D, D, BF16)
    (mix,) = _rowwise("mix", _f_mix, [(y_a, 0, D), (y_b, 0, D), (proj_b, C_GATE // 2048, 2048)], [], [(D, BF16)], tm, T)
    h1 = _matmul("mm_out", mix, Wout, "nn", T, D, D, mt, D, D, F32, residual=x2)
    (hn,) = _rowwise("rms_ffn", _f_rms, [(h1, 0, D)], [norm_ffn], [(D, BF16)], tm, T)
    gu = _matmul("mm_gu", hn, Wgu, "nn", T, 2 * D_FF, D, mt, 2816, D, BF16)
    h2 = _matmul("mm_down", gu, Wd, "nn", T, D, D_FF, mt, D, 1408, F32, residual=h1, a_swiglu=D_FF)
    dh2, loss, g_norm_final = _loss_fb("loss", h2, norm_final.reshape(1, D), tgt, tm, T)

    dact = _matmul("mm_dact", dh2, Wd, "nt", T, D_FF, D, mt, D_FF, D, BF16)
    gWd = _matmul("mm_gwd", gu, dh2, "tn", D_FF, D, T, 1408, D, mt, BF16, a_swiglu=D_FF)
    (dgu,) = _rowwise_vjp("act_bwd", _f_act, [(gu, 0, 2 * D_FF)], [], [(dact, 0, D_FF)], [BF16], tw, T)
    dhn = _matmul("mm_dhn", dgu, Wgu, "nt", T, D, 2 * D_FF, mt, D, 2816, F32)
    gWgu = _matmul("mm_gwgu", hn, dgu, "tn", D, 2 * D_FF, T, D, 1408, 2 * mt, BF16)
    dh1, g_norm_ffn = _rowwise_vjp("rms_ffn_bwd", _f_rms, [(h1, 0, D)], [norm_ffn], [(dhn, 0, D)], [F32], tm, T,
                                   residual=(dh2, 0, D))
    dmix = _matmul("mm_dmix", dh1, Wout, "nt", T, D, D, mt, D, D, BF16)
    gWout = _matmul("mm_gwout", mix, dh1, "tn", D, D, T, D, D, mt, BF16)
    dya, dyb, dgate = _rowwise_vjp("mix_bwd", _f_mix, [(y_a, 0, D), (y_b, 0, D), (proj_b, C_GATE // 2048, 2048)], [],
                                   [(dmix, 0, D)], [BF16, BF16, BF16], tm, T)
    dya_m = _matmul("mm_dyam", dya, Wpa, "nt", T, 512, D, mt, 512, D, BF16)
    gWpa = _matmul("mm_gwpa", ya_m, dya, "tn", 512, D, T, 512, D, mt, BF16)
    dod = _matmul("mm_dod", dyb, Wpd, "nt", T, D, D, 2 * mt, D, D, BF16)
    gWpd = _matmul("mm_gwpd", od, dyb, "tn", D, D, T, D, D, mt, BF16)
    do_d, dz, g_dn_norm = _rowwise_vjp("post_dn_bwd", _f_postdn, [(o_d, 0, D), (proj_b, C_Z // D, D)], [dn_norm],
                                       [(dod, 0, D)], [BF16, BF16], tm, T)
    dqn, dkn, dvd, dbg = _delta_bwd("delta_bwd", qn, kn, vd, bg, s_all, t_all, do_d, T)
    dqkvd, dba, g_conv8, g_alog_v, g_dtb_v = _dn_pre_bwd("dn_pre_bwd", proj_b, ba, convw, alog_v, dtb_v,
                                                         dqn, dkn, dvd, dbg, tw, T)
    def pair_sum(tag, g, h):
        keep = lax.dynamic_slice_in_dim(g, my_c * h, h, axis=1)
        give = lax.dynamic_slice_in_dim(g, (1 - my_c) * h, h, axis=1)
        (got,) = _swap_sibling("swap_grads_" + tag, [give])
        return _add2("add_cores_" + tag, keep.reshape(-1, g.shape[2]), got.reshape(-1, g.shape[2]), HALF_TILE).reshape(keep.shape)

    def cols4(g, n):
        return g.reshape(g.shape[0], 4, n).transpose(1, 0, 2).reshape(4, -1, D)

    parts = [cols4(gWpa, 256), gWpd.reshape(4, 256, D), gWout.reshape(4, 256, D), cols4(gWgu[:, :D_FF], 704),
             cols4(gWgu[:, D_FF:], 704), gWd.reshape(4, 704, D), cols4(g_conv8[:CONV_WIDTH], 768)]
    used = sum(p.shape[1] for p in parts)
    g_pack = jnp.concatenate([p.astype(BF16) for p in parts] + [jnp.zeros((4, PACK_R - used, D), BF16)], axis=1)
    pair_pack = pair_sum("pack", g_pack, HALF)

    mg = _rowwise_vjp("merge_bwd", _f_merge,
                      [(att[g][0], 0, 512) for g in range(3)] + [(att[g][1], 0, 512) for g in range(3)], [],
                      [(dya_m, 0, 512)], [F32] * 6, tm, T)
    dproj_a = None
    for g in range(3):
        dproj_a = _attn_bwd(f"attn_bwd{g}", proj_a, att[g][0], att[g][1], mg[g], mg[3 + g], g, T, dproj_a)
    dproj_b = [dqkvd, dz, dgate]
    gWa, recv_pack = _matmul("mm_gwa", u, dproj_a, "tn", D, D_PA, T, D, 1536, 2 * mt, BF16, side=([pair_pack], False))
    gWb = _matmul_pieces("mm_gwb", "tn", dproj_b, u, D, D_PB, T, D, 1024, 2 * mt, BF16)
    gWba = _matmul("mm_gwba", u, dba, "tn", D, 128, T, D, 128, mt, BF16)

    pieces, off = [(gWb[:, :C_GATE], D_PA), (gWba[:, :16], 8704), (gWb[:, C_GATE:], 8720)], 0
    for c0, w in _att_columns():
        pieces.append((gWa[:, off:off + w], c0))
        off += w
    pieces.sort(key=lambda t: t[1])

    def win_shard(j):
        out = []
        for arr, g0 in pieces:
            lo, hi = max(g0, W_IN_COLS * j), min(g0 + arr.shape[1], W_IN_COLS * (j + 1))
            if lo < hi:
                out.append(arr[:, lo - g0:hi - g0])
        return jnp.concatenate(out, axis=1)

    g_win4 = jnp.concatenate([win_shard(j)[None] for j in range(4)], axis=0)
    pair_win = pair_sum("w_in", g_win4, W_IN_HALF)

    du0 = _matmul("mm_du_ba", dba, Wba, "nt", T, D, 128, mt, D, 128, F32)
    du1 = _matmul_pieces("mm_du_b", "nt", dproj_b, Wb, T, D, D_PB, mt, D, 1024, F32, residual=du0)
    du, recv_win = _matmul("mm_du_a", dproj_a, Wa, "nt", T, D, D_PA, mt, D, 2304, F32, residual=du1, side=([pair_win], False))
    dx, g_norm_mix = _rowwise_vjp("rms_mix_bwd", _f_rms, [(x2, 0, D)], [norm_mix], [(du, 0, D)], [F32], tm, T,
                                  residual=(dh1, 0, D))

    g_mine = [_sum4("sum_chips_pack", recv_pack, HALF_TILE), _sum4("sum_chips_w_in", recv_win, HALF_TILE)]
    g_other = _swap_sibling("swap_sums", g_mine)
    g_all = both_halves(g_mine[0], g_other[0], 0)
    g_win = both_halves(g_mine[1], g_other[1], 0)

    locals_ = (w_proj_attn, w_proj_delta, w_out, w_gate, w_up, w_down, conv_w, w_in)
    ms = (m_w_proj_attn, m_w_proj_delta, m_w_out, m_w_gate, m_w_up, m_w_down, m_conv_w, m_w_in)
    vs = (v_w_proj_attn, v_w_proj_delta, v_w_out, v_w_gate, v_w_up, v_w_down, v_conv_w, v_w_in)
    big_names = ("w_proj_attn", "w_proj_delta", "w_out", "w_gate", "w_up", "w_down", "conv_w", "w_in")
    adam_tiles = (512, 256, 256, 256, 256, 352, CONV_WIDTH, 128)
    g_shards = _unpack_shards(g_all, [w.shape for w in locals_[:-1]]) + [g_win.reshape(w_in.shape)]
    big = [g_shards, [], [], []]
    for nm, w, g, m_, v_, tl in zip(big_names, locals_, g_shards, ms, vs, adam_tiles):
        res = _adamw("adamw_" + nm, w[0], g[0], m_[0], v_[0], tl)
        for kind in range(3):
            big[kind + 1].append(res[kind].reshape(w.shape))

    def small_pack(nm, nf, nfin, dn, al, dt, extra):
        row3 = jnp.concatenate([dn.reshape(1, 128), al.reshape(1, 8), dt.reshape(1, 8), extra.reshape(1, 1),
                                jnp.zeros((1, D - 145), F32)], axis=1)
        return jnp.concatenate([nm.reshape(1, D), nf.reshape(1, D), nfin.reshape(1, D), row3, jnp.zeros((4, D), F32)], axis=0)

    zero1 = jnp.zeros((1, 1), F32)
    sp = small_pack(g_norm_mix, g_norm_ffn, g_norm_final, g_dn_norm, g_alog_v[:, 8:16], g_dtb_v[:, 8:16], loss)
    tot = _allsum_small("allsum_small", sp)
    sm = _adamw("adamw_small", small_pack(norm_mix, norm_ffn, norm_final, dn_norm, a_log, dt_bias, zero1), tot,
                small_pack(m_norm_mix, m_norm_ffn, m_norm_final, m_dn_norm, m_a_log, m_dt_bias, zero1),
                small_pack(v_norm_mix, v_norm_ffn, v_norm_final, v_dn_norm, v_a_log, v_dt_bias, zero1), 8)
    sm = [tot] + list(sm)

    def small_unpack(p):
        return {"norm_mix": p[0:1], "norm_ffn": p[1:2], "norm_final": p[2], "dn_norm": p[3:4, :128],
                "a_log": p[3:4, 128:136], "dt_bias": p[3:4, 136:144]}

    sm = [small_unpack(p) for p in sm]
    loss_out = tot[3, 144]

    order = ("norm_mix", "w_in", "conv_w", "a_log", "dt_bias", "dn_norm", "w_proj_attn", "w_proj_delta", "w_out",
             "norm_ffn", "w_gate", "w_up", "w_down", "norm_final")
    outs = [loss_out, dx.reshape(x.shape)]
    for kind in range(4):
        for n in order:
            outs.append(big[kind][big_names.index(n)] if n in big_names else sm[kind][n])
    return tuple(outs)
```

```python
import functools
import math

import jax
import jax.numpy as jnp
from jax import lax
from jax.experimental import pallas as pl
from jax.experimental.pallas import tpu as pltpu

F32 = jnp.float32
BF16 = jnp.bfloat16
MESH = pl.DeviceIdType.MESH

D_MODEL = 1024
N_HEADS_A = 12
HEAD_DIM = 128
BLOCK_A = 128
DILATIONS = (1, 4, 16)
W_SUB = 128
D_ATTN_OUT = 512
N_HEADS_B = 8
D_KEY_B = 1024
CONV_WIDTH = 4
CHUNK = 64
CHUNKS_FWD, CHUNKS_BWD = 4, 1
D_FF = 2816
EPS = 1e-6
D_IN = 10768
D_PA = 4608
D_PB = 6144
C_Z, C_GATE = 3072, 4096

ADAM_LR, ADAM_B1, ADAM_B2, ADAM_EPS, ADAM_WD, ADAM_STEP = 0.001, 0.9, 0.999, 1e-08, 0.01, 10

PACK_R = 2816
HALF = PACK_R // 2
HALF_TILE = 128
W_IN_COLS = 2692
W_IN_HALF = D_MODEL // 2

VMEM_LIMIT = 48 * 1024 * 1024


def _cparams(n):
    return pltpu.CompilerParams(dimension_semantics=("arbitrary",) * n, vmem_limit_bytes=VMEM_LIMIT)


_ARB = _cparams(1)


def _row_spec(tm, w, cb):
    return pl.BlockSpec((tm, w), lambda i: (i, cb))


def _full_spec(shape):
    nd = len(shape)
    return pl.BlockSpec(shape, lambda i: (0,) * nd)


def _rowwise(name, fn, ins, params, outs, tm, T, side=None):
    n_in, n_p, n_out = len(ins), len(params), len(outs)
    n_side = 0 if side is None else len(side[0])
    steps = T // tm

    def body(*refs):
        k0 = n_in + n_p + n_side
        if n_side:
            start, finish = _chip_exchange(refs[n_in + n_p:k0], refs[k0 + n_out:k0 + n_out + n_side],
                                           *refs[k0 + n_out + n_side:], gather=side[1])
            pl.when(pl.program_id(0) == 0)(start)
        xs = [r[...].astype(F32) for r in refs[:n_in]] + [r[...] for r in refs[n_in:n_in + n_p]]
        res = fn(*xs)
        for o_ref, v in zip(refs[k0:k0 + n_out], res):
            o_ref[...] = v.astype(o_ref.dtype)
        if n_side:
            pl.when(pl.program_id(0) == steps - 1)(finish)

    side_arrays = [] if side is None else list(side[0])
    return pl.pallas_call(
        body, name=name, grid=(steps,),
        in_specs=([_row_spec(tm, w, cb) for (_, cb, w) in ins] + [_full_spec(p.shape) for p in params] + [_ANY] * n_side),
        out_specs=[_row_spec(tm, w, 0) for (w, _) in outs] + [_ANY] * n_side,
        out_shape=([jax.ShapeDtypeStruct((T, w), dt) for (w, dt) in outs]
                   + ([] if side is None else _chip_exchange_shapes(*side))),
        scratch_shapes=_chip_exchange_sems(n_side) if n_side else [],
        compiler_params=_ARB,
    )(*[a for (a, _, _) in ins], *params, *side_arrays)


def _rowwise_vjp(name, fn, ins, params, cts, gdtypes, tm, T, residual=None):
    n_in, n_p, n_ct = len(ins), len(params), len(cts)
    n_res = 0 if residual is None else 1

    def body(*refs):
        i = pl.program_id(0)
        k = 0
        xs = [r[...].astype(F32) for r in refs[k:k + n_in]]
        k += n_in
        ps = [r[...] for r in refs[k:k + n_p]]
        k += n_p
        gs = tuple(r[...].astype(F32) for r in refs[k:k + n_ct])
        k += n_ct
        res_refs = refs[k:k + n_res]
        k += n_res
        gi_refs = refs[k:k + n_in]
        gp_refs = refs[k + n_in:]
        _, vjp = jax.vjp(fn, *xs, *ps)
        g = vjp(gs)
        for j, (ref, v) in enumerate(zip(gi_refs, g[:n_in])):
            if j == 0 and n_res:
                v = v + res_refs[0][...].astype(F32)
            ref[...] = v.astype(ref.dtype)

        @pl.when(i == 0)
        def _():
            for ref in gp_refs:
                ref[...] = jnp.zeros_like(ref)

        for ref, v in zip(gp_refs, g[n_in:]):
            ref[...] += v

    res_in = [] if residual is None else [residual]
    return pl.pallas_call(
        body, name=name, grid=(T // tm,),
        in_specs=([_row_spec(tm, w, cb) for (_, cb, w) in ins] + [_full_spec(p.shape) for p in params]
                  + [_row_spec(tm, w, cb) for (_, cb, w) in cts] + [_row_spec(tm, w, cb) for (_, cb, w) in res_in]),
        out_specs=[_row_spec(tm, w, 0) for (_, _, w) in ins] + [_full_spec(p.shape) for p in params],
        out_shape=([jax.ShapeDtypeStruct((T, w), dt) for (_, _, w), dt in zip(ins, gdtypes)]
                   + [jax.ShapeDtypeStruct(p.shape, F32) for p in params]),
        compiler_params=_ARB,
    )(*[a for (a, _, _) in ins], *params, *[a for (a, _, _) in cts], *[a for (a, _, _) in res_in])


def _matmul(name, a, b, mode, M, N, K, tm, tn, tk, out_dtype, a_off=0, b_off=0, residual=None, side=None, a_swiglu=None):
    nk = K // tk
    n_side = 0 if side is None else len(side[0])
    assert M % tm == 0 and N % tn == 0 and K % tk == 0
    if mode == "nn":
        assert a_off % tk == 0 and b_off % tn == 0
        a_spec = pl.BlockSpec((tm, tk), lambda i, j, k: (i, a_off // tk + k))
        b_spec = pl.BlockSpec((tk, tn), lambda i, j, k: (k, b_off // tn + j))
        dims = (((1,), (0,)), ((), ()))
    elif mode == "nt":
        assert a_off % tk == 0 and b_off == 0
        a_spec = pl.BlockSpec((tm, tk), lambda i, j, k: (i, a_off // tk + k))
        b_spec = pl.BlockSpec((tn, tk), lambda i, j, k: (j, k))
        dims = (((1,), (1,)), ((), ()))
    else:
        assert a_off % tm == 0 and b_off % tn == 0
        a_spec = pl.BlockSpec((tk, tm), lambda i, j, k: (k, a_off // tm + i))
        b_spec = pl.BlockSpec((tk, tn), lambda i, j, k: (k, b_off // tn + j))
        dims = (((0,), (0,)), ((), ()))
    has_res = residual is not None
    has_pair = a_swiglu is not None
    if has_pair:
        blk = tm if mode == "tn" else tk
        assert mode in ("nn", "tn") and a_swiglu % blk == 0
        shift = a_swiglu // blk
        a2_spec = (pl.BlockSpec((tm, tk), lambda i, j, k: (i, a_off // tk + shift + k)) if mode == "nn" else
                   pl.BlockSpec((tk, tm), lambda i, j, k: (k, a_off // tm + shift + i)))

    n_main = 2 + has_res + has_pair
    grid = (M // tm, N // tn, nk)

    def a_tile(refs):
        if not has_pair:
            return refs[0][...].astype(BF16)
        g, up = refs[0][...].astype(F32), refs[n_main - 1][...].astype(F32)
        return (_silu(g) * up).astype(BF16)

    def body(*refs):
        b_ref = refs[1]
        r_ref = refs[2] if has_res else None
        o_ref = refs[n_main + n_side]
        acc_ref = refs[n_main + 2 * n_side + 1]
        i, j, k = pl.program_id(0), pl.program_id(1), pl.program_id(2)
        if n_side:
            start, finish = _chip_exchange(refs[n_main:n_main + n_side], refs[n_main + n_side + 1:n_main + 2 * n_side + 1],
                                           *refs[n_main + 2 * n_side + 2:], gather=side[1])
            pl.when((i == 0) & (j == 0) & (k == 0))(start)

        @pl.when(k == 0)
        def _():
            acc_ref[...] = jnp.zeros_like(acc_ref)

        acc_ref[...] += lax.dot_general(a_tile(refs), b_ref[...].astype(BF16), dims, preferred_element_type=F32)

        @pl.when(k == nk - 1)
        def _():
            acc = acc_ref[...]
            if has_res:
                acc = acc + r_ref[...].astype(F32)
            o_ref[...] = acc.astype(o_ref.dtype)

        if n_side:
            pl.when((i == grid[0] - 1) & (j == grid[1] - 1) & (k == nk - 1))(finish)

    in_specs = [a_spec, b_spec]
    args = [a, b]
    if has_res:
        in_specs.append(pl.BlockSpec((tm, tn), lambda i, j, k: (i, j)))
        args.append(residual)
    if has_pair:
        in_specs.append(a2_spec)
        args.append(a)
    out_specs = pl.BlockSpec((tm, tn), lambda i, j, k: (i, j))
    out_shape = jax.ShapeDtypeStruct((M, N), out_dtype)
    scratch = [pltpu.VMEM((tm, tn), F32)]
    semantics = ("parallel", "parallel", "arbitrary")
    if n_side:
        in_specs += [_ANY] * n_side
        args += list(side[0])
        out_specs = [out_specs] + [_ANY] * n_side
        out_shape = [out_shape] + _chip_exchange_shapes(*side)
        scratch += _chip_exchange_sems(n_side)
        semantics = ("arbitrary",) * 3
    return pl.pallas_call(
        body, name=name, grid=grid, in_specs=in_specs, out_specs=out_specs, out_shape=out_shape, scratch_shapes=scratch,
        compiler_params=pltpu.CompilerParams(dimension_semantics=semantics, vmem_limit_bytes=VMEM_LIMIT),
    )(*args)


def _matmul_pieces(name, mode, pieces, other, M, N, K, tm, tn, tk, out_dtype, residual=None):
    nk = K // tk
    blk = tn if mode == "tn" else tk
    counts = [p.shape[1] // blk for p in pieces]
    assert all(p.shape[1] % blk == 0 for p in pieces) and sum(counts) == (N if mode == "tn" else K) // blk
    starts = [sum(counts[:p]) for p in range(len(pieces))]
    n_p = len(pieces)
    has_res = residual is not None

    def piece_spec(s, n):
        if mode == "tn":
            return pl.BlockSpec((tk, tn), lambda i, j, k: (jnp.where((j >= s) & (j < s + n), k, 0), jnp.clip(j - s, 0, n - 1)))
        return pl.BlockSpec((tm, tk), lambda i, j, k: (i, jnp.clip(k - s, 0, n - 1)))

    if mode == "tn":
        other_spec = pl.BlockSpec((tk, tm), lambda i, j, k: (k, i))
    else:
        other_spec = pl.BlockSpec((tn, tk), lambda i, j, k: (j, k))

    def body(*refs):
        other_ref, piece_refs = refs[0], refs[1:1 + n_p]
        r_ref = refs[1 + n_p] if has_res else None
        o_ref, acc_ref = refs[-2], refs[-1]
        k = pl.program_id(2)
        sel = pl.program_id(1) if mode == "tn" else k

        @pl.when(k == 0)
        def _():
            acc_ref[...] = jnp.zeros_like(acc_ref)

        for p_ref, s, n in zip(piece_refs, starts, counts):
            @pl.when((sel >= s) & (sel < s + n))
            def _(p_ref=p_ref):
                if mode == "tn":
                    acc_ref[...] += lax.dot_general(other_ref[...].astype(BF16), p_ref[...].astype(BF16),
                                                    (((0,), (0,)), ((), ())), preferred_element_type=F32)
                else:
                    acc_ref[...] += lax.dot_general(p_ref[...].astype(BF16), other_ref[...].astype(BF16),
                                                    (((1,), (1,)), ((), ())), preferred_element_type=F32)

        @pl.when(k == nk - 1)
        def _():
            acc = acc_ref[...]
            if has_res:
                acc = acc + r_ref[...].astype(F32)
            o_ref[...] = acc.astype(o_ref.dtype)

    in_specs = [other_spec] + [piece_spec(s, n) for s, n in zip(starts, counts)]
    args = [other] + list(pieces)
    if has_res:
        in_specs.append(pl.BlockSpec((tm, tn), lambda i, j, k: (i, j)))
        args.append(residual)
    return pl.pallas_call(
        body, name=name, grid=(M // tm, N // tn, nk),
        in_specs=in_specs,
        out_specs=pl.BlockSpec((tm, tn), lambda i, j, k: (i, j)),
        out_shape=jax.ShapeDtypeStruct((M, N), out_dtype),
        scratch_shapes=[pltpu.VMEM((tm, tn), F32)],
        compiler_params=pltpu.CompilerParams(dimension_semantics=("parallel", "parallel", "arbitrary"),
                                             vmem_limit_bytes=VMEM_LIMIT),
    )(*args)


def _f_rms(x, w):
    return (x * lax.rsqrt(jnp.mean(x * x, axis=-1, keepdims=True) + EPS) * w,)


def _silu(x):
    return x * jax.nn.sigmoid(x)


def _softplus(x):
    return jnp.maximum(x, 0.0) + jnp.log1p(jnp.exp(-jnp.abs(x)))


def _heads(x, n):
    return [x[:, HEAD_DIM * h:HEAD_DIM * (h + 1)] for h in range(n)]


def _f_dnpoint(c, ba, alog_v, dtb_v):
    cs = _silu(c)
    q, k, v = cs[:, :D_KEY_B], cs[:, D_KEY_B:2 * D_KEY_B], cs[:, 2 * D_KEY_B:]
    qn = jnp.concatenate([t * lax.rsqrt(jnp.sum(t * t, axis=-1, keepdims=True) + EPS) * (HEAD_DIM ** -0.5)
                          for t in _heads(q, N_HEADS_B)], axis=1)
    kn = jnp.concatenate([t * lax.rsqrt(jnp.sum(t * t, axis=-1, keepdims=True) + EPS)
                          for t in _heads(k, N_HEADS_B)], axis=1)
    lane = lax.broadcasted_iota(jnp.int32, ba.shape, 1)
    beta = jax.nn.sigmoid(ba)
    g = -jnp.exp(alog_v) * _softplus(ba + dtb_v)
    bg = jnp.where(lane < N_HEADS_B, beta, jnp.where(lane < 2 * N_HEADS_B, g, 0.0))
    return qn, kn, v, bg


def _dnpoint_bwd(c, ba, alog_v, dtb_v, dq, dk, dv, dbg):
    sig = jax.nn.sigmoid(c)
    cs = c * sig

    def l2_bwd(t, dy, scale):
        outs = []
        for th, dh in zip(_heads(t, N_HEADS_B), _heads(dy, N_HEADS_B)):
            r = lax.rsqrt(jnp.sum(th * th, axis=-1, keepdims=True) + EPS)
            y = th * r
            outs.append((scale * r) * (dh - y * jnp.sum(dh * y, axis=-1, keepdims=True)))
        return outs

    dcs = jnp.concatenate(l2_bwd(cs[:, :D_KEY_B], dq, HEAD_DIM ** -0.5) + l2_bwd(cs[:, D_KEY_B:2 * D_KEY_B], dk, 1.0) + [dv],
                          axis=1)
    dc = dcs * (sig * (1.0 + c * (1.0 - sig)))
    lane = lax.broadcasted_iota(jnp.int32, ba.shape, 1)
    is_beta, is_g = lane < N_HEADS_B, (lane >= N_HEADS_B) & (lane < 2 * N_HEADS_B)
    beta = jax.nn.sigmoid(ba)
    z = ba + dtb_v
    minus_ea = -jnp.exp(alog_v)
    dz = jnp.where(is_g, dbg * minus_ea * jax.nn.sigmoid(z), 0.0)
    dba = jnp.where(is_beta, dbg * beta * (1.0 - beta), dz)
    dal = jnp.sum(jnp.where(is_g, dbg * minus_ea * _softplus(z), 0.0), axis=0, keepdims=True)
    ddt = jnp.sum(dz, axis=0, keepdims=True)
    return dc, dba, dal, ddt


def _f_postdn(o, z, w):
    outs = []
    for oh, zh in zip(_heads(o, N_HEADS_B), _heads(z, N_HEADS_B)):
        outs.append(oh * lax.rsqrt(jnp.mean(oh * oh, axis=-1, keepdims=True) + EPS) * w * _silu(zh))
    return (jnp.concatenate(outs, axis=1),)


def _f_merge(o0, o1, o2, l0, l1, l2):
    m = lax.stop_gradient(jnp.maximum(jnp.maximum(l0, l1), l2))
    e0, e1, e2 = jnp.exp(l0 - m), jnp.exp(l1 - m), jnp.exp(l2 - m)
    return ((e0 * o0 + e1 * o1 + e2 * o2) / (e0 + e1 + e2),)


def _f_mix(ya, yb, gr):
    return (jax.nn.sigmoid(gr[:, :D_MODEL]) * ya + jax.nn.sigmoid(gr[:, D_MODEL:]) * yb,)


def _f_act(gu):
    return (_silu(gu[:, :D_FF]) * gu[:, D_FF:],)


def _attn_block(q, kp, kc, vp, vc, slope_d, first_key):
    k2 = jnp.concatenate([kp, kc], axis=0).astype(BF16)
    v2 = jnp.concatenate([vp, vc], axis=0).astype(BF16)
    s = lax.dot_general(q.astype(BF16), k2, (((1,), (1,)), ((), ())), preferred_element_type=F32)
    s = s * (HEAD_DIM ** -0.5)
    i = lax.broadcasted_iota(jnp.int32, (BLOCK_A, 2 * BLOCK_A), 0)
    j = lax.broadcasted_iota(jnp.int32, (BLOCK_A, 2 * BLOCK_A), 1)
    delta = BLOCK_A + i - j
    valid = (delta >= 0) & (delta <= W_SUB) & (j >= first_key)
    s = jnp.where(valid, s - slope_d * delta.astype(F32), -jnp.inf)
    m = lax.stop_gradient(jnp.max(s, axis=-1, keepdims=True))
    p = jnp.exp(s - m)
    den = jnp.sum(p, axis=-1, keepdims=True)
    o = jnp.dot(p.astype(BF16), v2, preferred_element_type=F32) / den
    lse = jnp.broadcast_to(m + jnp.log(den), (BLOCK_A, HEAD_DIM))
    return o, lse


def _slope_d(group, h, d):
    hg = (group * 4 + h + 1).astype(F32)
    return jnp.exp(jnp.full((1, 1), -8.0 * math.log(2.0) / N_HEADS_A, F32) * hg) * float(d)


ATT_CFG = ((1, 4, 4), (4, 2, 1), (16, 1, 1))
D_GROUP = 1536


def _att_columns():
    out = []
    for g, (_, _, hp) in enumerate(ATT_CFG):
        lanes = HEAD_DIM * hp
        for hb in range(D_ATTN_OUT // lanes):
            for which in range(3):
                out.append((N_HEADS_A * HEAD_DIM * which + D_ATTN_OUT * g + lanes * hb, lanes))
    return out
R_UNROLL = 4


def _attn_geometry(group, T):
    d, m, hp = ATT_CFG[group]
    span = BLOCK_A * d
    tile = m * span
    assert T % tile == 0 and 4 % hp == 0
    return d, m, hp, span, tile, T // tile, HEAD_DIM * hp


def _sub_rows(b, span, r, d):
    return pl.ds(b * span + r, BLOCK_A, stride=d) if d > 1 else pl.ds(b * span, BLOCK_A)


def _for_each_r(d, fn):
    if d <= R_UNROLL:
        for r in range(d):
            fn(r)
    else:
        def step(r, carry):
            fn(r)
            return carry

        lax.fori_loop(0, d, step, 0, unroll=R_UNROLL)


def _attn_fwd(name, proj, group, T):
    d, m, hp, span, tile, nt, lanes = _attn_geometry(group, T)
    qb, kb, vb = [(D_GROUP * group) // lanes + which for which in range(3)]

    def body(q_ref, kc_ref, kp_ref, vc_ref, vp_ref, o_ref, l_ref):
        hb, n = pl.program_id(0), pl.program_id(1)
        first_key = jnp.where(n > 0, 0, BLOCK_A)
        for hh in range(hp):
            ls = slice(HEAD_DIM * hh, HEAD_DIM * (hh + 1))
            sl = _slope_d(group, hb * hp + hh, d)
            for b in range(m):
                def one(r, b=b, ls=ls, sl=sl):
                    rq = _sub_rows(b, span, r, d)
                    if b == 0:
                        rp = _sub_rows(0, span, r, d)
                        kp, vp, fk = kp_ref[rp, ls], vp_ref[rp, ls], first_key
                    else:
                        rp = _sub_rows(b - 1, span, r, d)
                        kp, vp, fk = kc_ref[rp, ls], vc_ref[rp, ls], 0
                    o, lse = _attn_block(q_ref[rq, ls], kp, kc_ref[rq, ls], vp, vc_ref[rq, ls], sl, fk)
                    o_ref[rq, ls] = o
                    l_ref[rq, ls] = lse

                _for_each_r(d, one)

    def cur(cb):
        return pl.BlockSpec((tile, lanes), lambda hb, n: (n, cb + 3 * hb))

    def prev(cb):
        return pl.BlockSpec((span, lanes), lambda hb, n: (jnp.maximum(m * n - 1, 0), cb + 3 * hb))

    return pl.pallas_call(
        body, name=name, grid=(4 // hp, nt),
        in_specs=[cur(qb), cur(kb), prev(kb), cur(vb), prev(vb)],
        out_specs=[pl.BlockSpec((tile, lanes), lambda hb, n: (n, hb))] * 2,
        out_shape=[jax.ShapeDtypeStruct((T, D_ATTN_OUT), F32)] * 2,
        compiler_params=_cparams(2),
    )(proj, proj, proj, proj, proj)


def _attn_block_bwd(q, kp, kc, vp, vc, o, lse, do, dl, slope_d, first_key):
    scale = HEAD_DIM ** -0.5
    nt_dims = (((1,), (1,)), ((), ()))
    tn_dims = (((0,), (0,)), ((), ()))
    qb = q.astype(BF16)
    k2 = jnp.concatenate([kp, kc], axis=0).astype(BF16)
    v2 = jnp.concatenate([vp, vc], axis=0).astype(BF16)
    dob = do.astype(BF16)
    s = lax.dot_general(qb, k2, nt_dims, preferred_element_type=F32) * scale
    i = lax.broadcasted_iota(jnp.int32, (BLOCK_A, 2 * BLOCK_A), 0)
    j = lax.broadcasted_iota(jnp.int32, (BLOCK_A, 2 * BLOCK_A), 1)
    delta = BLOCK_A + i - j
    valid = (delta >= 0) & (delta <= W_SUB) & (j >= first_key)
    lse_col = jnp.max(lse, axis=-1, keepdims=True)
    p = jnp.exp(jnp.where(valid, s - slope_d * delta.astype(F32), -jnp.inf) - lse_col)
    dp = lax.dot_general(dob, v2, nt_dims, preferred_element_type=F32)
    row = jnp.sum(dl, axis=-1, keepdims=True) - jnp.sum(do * o, axis=-1, keepdims=True)
    ds = (p * (dp + row)).astype(BF16)
    dv2 = lax.dot_general(p.astype(BF16), dob, tn_dims, preferred_element_type=F32)
    dq = jnp.dot(ds, k2, preferred_element_type=F32) * scale
    dk2 = lax.dot_general(ds, qb, tn_dims, preferred_element_type=F32) * scale
    return dq, dk2[:BLOCK_A], dk2[BLOCK_A:], dv2[:BLOCK_A], dv2[BLOCK_A:]


def _attn_bwd(name, proj, o, lse, do, dl, group, T, dest):
    d, m, hp, span, tile, nt, lanes = _attn_geometry(group, T)
    qb, kb, vb = [(D_GROUP * group) // lanes + which for which in range(3)]
    n_in = 9 if dest is None else 10

    def body(*refs):
        q_ref, kc_ref, kp_ref, vc_ref, vp_ref, o_ref, l_ref, do_ref, dl_ref = refs[:9]
        out_ref, aq_ref, ak_ref, av_ref = refs[n_in:]
        hb, n = pl.program_id(0), pl.program_id(1)
        first_key = jnp.where(n > 0, 0, BLOCK_A)
        cur_slot = n % 2
        old_slot = 1 - cur_slot

        @pl.when(n == 0)
        def _():
            aq_ref[...] = jnp.zeros_like(aq_ref)
            ak_ref[...] = jnp.zeros_like(ak_ref)
            av_ref[...] = jnp.zeros_like(av_ref)

        @pl.when(n < nt)
        def _():
            for hh in range(hp):
                ls = slice(HEAD_DIM * hh, HEAD_DIM * (hh + 1))
                sl = _slope_d(group, hb * hp + hh, d)
                for b in range(m):
                    def one(r, b=b, ls=ls, sl=sl):
                        rq = _sub_rows(b, span, r, d)
                        if b == 0:
                            rp = _sub_rows(0, span, r, d)
                            kp, vp, fk = kp_ref[rp, ls], vp_ref[rp, ls], first_key
                        else:
                            rp = _sub_rows(b - 1, span, r, d)
                            kp, vp, fk = kc_ref[rp, ls], vc_ref[rp, ls], 0
                        dq, dkp, dkc, dvp, dvc = _attn_block_bwd(
                            q_ref[rq, ls], kp, kc_ref[rq, ls], vp, vc_ref[rq, ls], o_ref[rq, ls], l_ref[rq, ls],
                            do_ref[rq, ls], dl_ref[rq, ls], sl, fk)
                        aq_ref[cur_slot, rq, ls] = dq
                        ak_ref[cur_slot, rq, ls] = dkc
                        av_ref[cur_slot, rq, ls] = dvc
                        if b == 0:
                            last = _sub_rows(m - 1, span, r, d)
                            ak_ref[old_slot, last, ls] += dkp
                            av_ref[old_slot, last, ls] += dvp
                        else:
                            ak_ref[cur_slot, rp, ls] += dkp
                            av_ref[cur_slot, rp, ls] += dvp

                    _for_each_r(d, one)

        out_ref[:, 0:lanes] = aq_ref[old_slot].astype(BF16)
        out_ref[:, lanes:2 * lanes] = ak_ref[old_slot].astype(BF16)
        out_ref[:, 2 * lanes:3 * lanes] = av_ref[old_slot].astype(BF16)

    def cur(cb):
        return pl.BlockSpec((tile, lanes), lambda hb, n: (jnp.minimum(n, nt - 1), cb + 3 * hb))

    def prev(cb):
        return pl.BlockSpec((span, lanes), lambda hb, n: (jnp.maximum(m * jnp.minimum(n, nt - 1) - 1, 0), cb + 3 * hb))

    own = pl.BlockSpec((tile, lanes), lambda hb, n: (jnp.minimum(n, nt - 1), hb))
    late = pl.BlockSpec((tile, 3 * lanes), lambda hb, n: (jnp.maximum(n - 1, 0), (D_GROUP * group) // (3 * lanes) + hb))
    acc = pltpu.VMEM((2, tile, lanes), F32)
    return pl.pallas_call(
        body, name=name, grid=(4 // hp, nt + 1),
        in_specs=[cur(qb), cur(kb), prev(kb), cur(vb), prev(vb), own, own, own, own] + ([] if dest is None else [_ANY]),
        out_specs=late,
        out_shape=jax.ShapeDtypeStruct((T, D_PA), BF16),
        input_output_aliases={} if dest is None else {9: 0},
        scratch_shapes=[acc, acc, acc],
        compiler_params=_cparams(2),
    )(proj, proj, proj, proj, proj, o, lse, do, dl, *([] if dest is None else [dest]))


HALO = 8
HALO_IN = 16


def _shifted(ext):
    return [ext[HALO:, :]] + [pltpu.roll(ext, k, 0)[HALO:, :] for k in range(1, CONV_WIDTH)]


def _conv_taps(sh, w):
    c = sh[0] * w[CONV_WIDTH - 1]
    for k in range(1, CONV_WIDTH):
        c = c + sh[k] * w[CONV_WIDTH - 1 - k]
    return c


def _taps(w_ref):
    return [w_ref[j:j + 1, :] for j in range(CONV_WIDTH)]


def _dn_pre_fwd(name, proj, ba, conv_w, alog_v, dtb_v, tm, T):
    C = 3 * D_KEY_B

    def body(x_ref, ba_ref, w_ref, al_ref, dt_ref, q_ref, k_ref, v_ref, bg_ref, halo_ref):
        i = pl.program_id(0)

        @pl.when(i == 0)
        def _():
            halo_ref[...] = jnp.zeros_like(halo_ref)

        x = x_ref[...].astype(F32)
        ext = jnp.concatenate([halo_ref[...], x], axis=0)
        c = _conv_taps(_shifted(ext), _taps(w_ref))
        halo_ref[...] = x[tm - HALO:, :]
        qn, kn, v, bg = _f_dnpoint(c, ba_ref[...], al_ref[...], dt_ref[...])
        q_ref[...] = qn
        k_ref[...] = kn
        v_ref[...] = v
        bg_ref[...] = bg

    return pl.pallas_call(
        body, name=name, grid=(T // tm,),
        in_specs=[_row_spec(tm, C, 0), _row_spec(tm, 128, 0), _full_spec(conv_w.shape), _full_spec((1, 128)),
                  _full_spec((1, 128))],
        out_specs=[_row_spec(tm, D_KEY_B, 0)] * 3 + [_row_spec(tm, 128, 0)],
        out_shape=[jax.ShapeDtypeStruct((T, D_KEY_B), F32)] * 3 + [jax.ShapeDtypeStruct((T, 128), F32)],
        scratch_shapes=[pltpu.VMEM((HALO, C), F32)],
        compiler_params=_ARB,
    )(proj, ba, conv_w, alog_v, dtb_v)


def _dn_pre_bwd(name, proj, ba, conv_w, alog_v, dtb_v, dq, dk, dv, dbg, tm, T):
    C = 3 * D_KEY_B
    nt = T // tm
    hb = tm // HALO_IN

    def body(x_ref, xh_ref, ba_ref, w_ref, al_ref, dt_ref, dq_ref, dk_ref, dv_ref, dbg_ref,
             dx_ref, dba_ref, dw_ref, dal_ref, ddt_ref, nxt_ref):
        s = pl.program_id(0)
        i = nt - 1 - s

        @pl.when(s == 0)
        def _():
            nxt_ref[...] = jnp.zeros_like(nxt_ref)
            dw_ref[...] = jnp.zeros_like(dw_ref)
            dal_ref[...] = jnp.zeros_like(dal_ref)
            ddt_ref[...] = jnp.zeros_like(ddt_ref)

        w = _taps(w_ref)
        halo = jnp.where(i > 0, xh_ref[...].astype(F32)[HALO_IN - HALO:, :], 0.0)
        ext = jnp.concatenate([halo, x_ref[...].astype(F32)], axis=0)
        sh = _shifted(ext)
        c = _conv_taps(sh, w)
        dc, dba, dal, ddt = _dnpoint_bwd(c, ba_ref[...], al_ref[...], dt_ref[...],
                                         dq_ref[...].astype(F32), dk_ref[...].astype(F32), dv_ref[...].astype(F32), dbg_ref[...])
        dba_ref[...] = dba
        dal_ref[...] += dal
        ddt_ref[...] += ddt
        ext2 = jnp.concatenate([dc, nxt_ref[...]], axis=0)
        dx = dc * w[CONV_WIDTH - 1]
        for m in range(1, CONV_WIDTH):
            dx = dx + pltpu.roll(ext2, tm + HALO - m, 0)[:tm, :] * w[CONV_WIDTH - 1 - m]
        dx_ref[...] = dx.astype(dx_ref.dtype)
        nxt_ref[...] = dc[:HALO, :]
        for j in range(CONV_WIDTH):
            dw_ref[j:j + 1, :] += jnp.sum(sh[CONV_WIDTH - 1 - j] * dc, axis=0, keepdims=True)

    def rev(w):
        return pl.BlockSpec((tm, w), lambda s: (nt - 1 - s, 0))

    return pl.pallas_call(
        body, name=name, grid=(nt,),
        in_specs=[rev(C), pl.BlockSpec((HALO_IN, C), lambda s: (jnp.maximum((nt - 1 - s) * hb - 1, 0), 0)), rev(128),
                  _full_spec(conv_w.shape), _full_spec((1, 128)), _full_spec((1, 128)),
                  rev(D_KEY_B), rev(D_KEY_B), rev(D_KEY_B), rev(128)],
        out_specs=[rev(C), rev(128), _full_spec((8, C)), _full_spec((1, 128)), _full_spec((1, 128))],
        out_shape=[jax.ShapeDtypeStruct((T, C), BF16), jax.ShapeDtypeStruct((T, 128), F32),
                   jax.ShapeDtypeStruct((8, C), F32), jax.ShapeDtypeStruct((1, 128), F32),
                   jax.ShapeDtypeStruct((1, 128), F32)],
        scratch_shapes=[pltpu.VMEM((HALO, C), F32)],
        compiler_params=_ARB,
    )(proj, proj, ba, conv_w, alog_v, dtb_v, dq, dk, dv, dbg)


def _bdot(a, b, dims, exact=False):
    if exact:
        return lax.dot_general(a, b, dims, preferred_element_type=F32, precision=lax.Precision.HIGH)
    return lax.dot_general(a.astype(BF16), b.astype(BF16), dims, preferred_element_type=F32)


_B_NN = (((2,), (1,)), ((0,), (0,)))
_B_NT = (((2,), (2,)), ((0,), (0,)))
_B_TN = (((1,), (1,)), ((0,), (0,)))


def _unit_lower_inverse(a):
    C = a.shape[-1]
    r = lax.broadcasted_iota(jnp.int32, (C, C), 0)
    c = lax.broadcasted_iota(jnp.int32, (C, C), 1)
    pw = -a
    t = jnp.where((r == c)[None], 1.0, 0.0) + pw
    for _ in range(int(math.log2(C)) - 1):
        pw = _bdot(pw, pw, _B_NN, exact=True)
        t = t + _bdot(t, pw, _B_NN, exact=True)
    return t


@jax.custom_vjp
def _solve_given_inverse(t, a, rhs):
    return _bdot(t, rhs, _B_NN, exact=True)


def _solve_fwd(t, a, rhs):
    sol = _bdot(t, rhs, _B_NN, exact=True)
    return sol, (t, sol)


def _solve_bwd(res, dsol):
    t, sol = res
    drhs = _bdot(t, dsol, _B_TN, exact=True)
    return jnp.zeros_like(t), -_bdot(drhs, sol, _B_NT, exact=True), drhs


_solve_given_inverse.defvjp(_solve_fwd, _solve_bwd)


def _stack(xs):
    return jnp.concatenate([x[None] for x in xs], axis=0)


def _delta_chunk(S, q, k, v, bg, t_inv=None):
    H, C = N_HEADS_B, CHUNK
    r = lax.broadcasted_iota(jnp.int32, (C, C), 0)
    c = lax.broadcasted_iota(jnp.int32, (C, C), 1)
    incl, strict, eye = r >= c, r > c, r == c
    gc_all = jnp.dot(incl.astype(F32), bg, preferred_element_type=F32, precision=lax.Precision.HIGHEST)
    lane = lax.broadcasted_iota(jnp.int32, (C, 128), 1)

    def col(x, l):
        return jnp.sum(jnp.where(lane == l, x, 0.0), axis=1, keepdims=True)

    beta = _stack([col(bg, h) for h in range(H)])
    gc = _stack([col(gc_all, H + h) for h in range(H)])
    gc_row = jnp.sum(jnp.where(eye[None], jnp.broadcast_to(gc, (H, C, C)), 0.0), axis=1, keepdims=True)
    row = lax.broadcasted_iota(jnp.int32, (H, C, 1), 1)
    gc_last = jnp.sum(jnp.where(row == C - 1, gc, 0.0), axis=1, keepdims=True)
    decay = jnp.exp(jnp.where(incl[None], gc - gc_row, -jnp.inf))
    qh = _stack(_heads(q, H))
    kh = _stack(_heads(k, H))
    vh = _stack(_heads(v, H))
    kk = _bdot(kh, kh, _B_NT, exact=True)
    a = jnp.where(strict[None], beta * kk * decay, 0.0)
    egc = jnp.exp(gc)
    rhs = jnp.concatenate([beta * vh, (beta * egc) * kh], axis=-1)
    if t_inv is None:
        t_inv = _unit_lower_inverse(a)
        sol = _bdot(t_inv, rhs, _B_NN, exact=True)
    else:
        sol = _solve_given_inverse(t_inv, a, rhs)
    u_bar, w = sol[..., :HEAD_DIM], sol[..., HEAD_DIM:]
    qk = _bdot(qh, kh, _B_NT) * decay
    q_dec = qh * egc
    k_dec = kh * jnp.exp(gc_last - gc)
    u = u_bar - _bdot(w, S, _B_NN)
    o = _bdot(q_dec, S, _B_NN) + _bdot(qk, u, _B_NN)
    S_new = jnp.exp(gc_last) * S + _bdot(k_dec, u, _B_TN)
    o2 = jnp.concatenate([lax.index_in_dim(o, h, 0, keepdims=False) for h in range(H)], axis=1)
    return o2, S_new, t_inv


def _delta_fwd(name, q, k, v, bg, T):
    nc = T // CHUNK
    H = N_HEADS_B
    G, R = CHUNKS_FWD, CHUNKS_FWD * CHUNK

    def body(q_ref, k_ref, v_ref, bg_ref, o_ref, s_out_ref, t_out_ref, s_ref):
        @pl.when(pl.program_id(0) == 0)
        def _():
            s_ref[...] = jnp.zeros_like(s_ref)

        S = s_ref[...]
        for c in range(G):
            rows = slice(c * CHUNK, (c + 1) * CHUNK)
            s_out_ref[c] = S
            o, S, t_inv = _delta_chunk(S, q_ref[rows, :], k_ref[rows, :], v_ref[rows, :], bg_ref[rows, :])
            o_ref[rows, :] = o.astype(o_ref.dtype)
            t_out_ref[c] = t_inv
        s_ref[...] = S

    return pl.pallas_call(
        body, name=name, grid=(nc // G,),
        in_specs=[_row_spec(R, D_KEY_B, 0)] * 3 + [_row_spec(R, 128, 0)],
        out_specs=[_row_spec(R, D_KEY_B, 0), pl.BlockSpec((G, H, HEAD_DIM, HEAD_DIM), lambda i: (i, 0, 0, 0)),
                   pl.BlockSpec((G, H, CHUNK, CHUNK), lambda i: (i, 0, 0, 0))],
        out_shape=[jax.ShapeDtypeStruct((T, D_KEY_B), BF16), jax.ShapeDtypeStruct((nc, H, HEAD_DIM, HEAD_DIM), F32),
                   jax.ShapeDtypeStruct((nc, H, CHUNK, CHUNK), F32)],
        scratch_shapes=[pltpu.VMEM((H, HEAD_DIM, HEAD_DIM), F32)],
        compiler_params=_ARB,
    )(q, k, v, bg)


def _delta_bwd(name, q, k, v, bg, s_all, t_all, do, T):
    nc = T // CHUNK
    H = N_HEADS_B
    G, R = CHUNKS_BWD, CHUNKS_BWD * CHUNK
    ns = nc // G

    def body(q_ref, k_ref, v_ref, bg_ref, s_in_ref, t_ref, do_ref, dq_ref, dk_ref, dv_ref, dbg_ref, ds_ref):
        @pl.when(pl.program_id(0) == 0)
        def _():
            ds_ref[...] = jnp.zeros_like(ds_ref)

        dS = ds_ref[...]
        for c in reversed(range(G)):
            rows = slice(c * CHUNK, (c + 1) * CHUNK)
            t_inv = t_ref[c]

            def f(S, qq, kk, vv, bb, t_inv=t_inv):
                return _delta_chunk(S, qq, kk, vv, bb, t_inv)[:2]

            _, vjp = jax.vjp(f, s_in_ref[c], q_ref[rows, :], k_ref[rows, :], v_ref[rows, :], bg_ref[rows, :])
            dS, dq, dk, dv, dbg = vjp((do_ref[rows, :].astype(F32), dS))
            dq_ref[rows, :] = dq.astype(dq_ref.dtype)
            dk_ref[rows, :] = dk.astype(dk_ref.dtype)
            dv_ref[rows, :] = dv.astype(dv_ref.dtype)
            dbg_ref[rows, :] = dbg
        ds_ref[...] = dS

    def rev(w):
        return pl.BlockSpec((R, w), lambda s: (ns - 1 - s, 0))

    def rev4(n):
        return pl.BlockSpec((G, H, n, n), lambda s: (ns - 1 - s, 0, 0, 0))

    return pl.pallas_call(
        body, name=name, grid=(ns,),
        in_specs=[rev(D_KEY_B)] * 3 + [rev(128), rev4(HEAD_DIM), rev4(CHUNK), rev(D_KEY_B)],
        out_specs=[rev(D_KEY_B)] * 3 + [rev(128)],
        out_shape=[jax.ShapeDtypeStruct((T, D_KEY_B), F32)] * 3 + [jax.ShapeDtypeStruct((T, 128), F32)],
        scratch_shapes=[pltpu.VMEM((H, HEAD_DIM, HEAD_DIM), F32)],
        compiler_params=_ARB,
    )(q, k, v, bg, s_all, t_all, do)


def _loss_fb(name, h2, w, tgt, tm, T):
    def tile_loss(h, wv, t):
        y = _f_rms(h, wv)[0]
        e = y - t
        return 0.5 * jnp.sum(jnp.mean(e * e, axis=-1))

    def body(h_ref, w_ref, t_ref, dh_ref, loss_ref, dw_ref):
        @pl.when(pl.program_id(0) == 0)
        def _():
            loss_ref[...] = jnp.zeros_like(loss_ref)
            dw_ref[...] = jnp.zeros_like(dw_ref)

        val, (dh, dw) = jax.value_and_grad(tile_loss, argnums=(0, 1))(h_ref[...], w_ref[...], t_ref[...])
        dh_ref[...] = dh
        dw_ref[...] += dw
        loss_ref[...] += jnp.reshape(val, (1, 1))

    return pl.pallas_call(
        body, name=name, grid=(T // tm,),
        in_specs=[_row_spec(tm, D_MODEL, 0), _full_spec((1, D_MODEL)), _row_spec(tm, D_MODEL, 0)],
        out_specs=[_row_spec(tm, D_MODEL, 0), _full_spec((1, 1)), _full_spec((1, D_MODEL))],
        out_shape=[jax.ShapeDtypeStruct((T, D_MODEL), F32), jax.ShapeDtypeStruct((1, 1), F32),
                   jax.ShapeDtypeStruct((1, D_MODEL), F32)],
        compiler_params=_ARB,
    )(h2, w, tgt)


def _adamw(name, w, g, m, v, tile):
    R = w.shape[0]

    def body(w_ref, g_ref, m_ref, v_ref, d_out, m_out, v_out):
        g = g_ref[...]
        mm = ADAM_B1 * m_ref[...] + (1.0 - ADAM_B1) * g
        vv = ADAM_B2 * v_ref[...] + (1.0 - ADAM_B2) * jnp.square(g)
        m_hat = mm / (1.0 - ADAM_B1 ** ADAM_STEP)
        v_hat = vv / (1.0 - ADAM_B2 ** ADAM_STEP)
        d_out[...] = -ADAM_LR * (m_hat / (jnp.sqrt(v_hat) + ADAM_EPS) + ADAM_WD * w_ref[...])
        m_out[...] = mm
        v_out[...] = vv

    assert R % tile == 0
    spec = pl.BlockSpec((tile, w.shape[1]), lambda i: (i, 0))
    return pl.pallas_call(
        body, name=name, grid=(R // tile,), in_specs=[spec] * 4, out_specs=[spec] * 3,
        out_shape=[jax.ShapeDtypeStruct(w.shape, F32)] * 3, compiler_params=_ARB,
    )(w, g, m, v)


def _sum4(name, r, tile):
    _, R, C = r.shape
    assert R % tile == 0

    def body(r_ref, o_ref):
        o_ref[...] = ((r_ref[0].astype(F32) + r_ref[1].astype(F32)) + r_ref[2].astype(F32)) + r_ref[3].astype(F32)

    return pl.pallas_call(
        body, name=name, grid=(R // tile,),
        in_specs=[pl.BlockSpec((4, tile, C), lambda i: (0, i, 0))],
        out_specs=pl.BlockSpec((tile, C), lambda i: (i, 0)),
        out_shape=jax.ShapeDtypeStruct((R, C), F32), compiler_params=_ARB,
    )(r)


def _add2(name, a, b, tile):
    R, C = a.shape
    assert R % tile == 0

    def body(a_ref, b_ref, o_ref):
        o_ref[...] = (a_ref[...].astype(F32) + b_ref[...].astype(F32)).astype(BF16)

    spec = pl.BlockSpec((tile, C), lambda i: (i, 0))
    return pl.pallas_call(
        body, name=name, grid=(R // tile,), in_specs=[spec, spec], out_specs=spec,
        out_shape=jax.ShapeDtypeStruct((R, C), BF16), compiler_params=_ARB,
    )(a, b)


_ANY = pl.BlockSpec(memory_space=pl.ANY)


def _place():
    return lax.axis_index("x"), lax.axis_index("y"), lax.axis_index("c")


def _chip_exchange(in_refs, out_refs, send_sems, recv_sems, local_sems, gather):
    n = len(in_refs)
    x, y, c = _place()
    me = 2 * x + y
    chips = [(1 - x, y), (x, 1 - y), (1 - x, 1 - y)]

    def src(a, slot):
        return in_refs[a] if gather else in_refs[a].at[slot]

    def copy(a, k, src_slot, dst_slot, px, py):
        return pltpu.make_async_remote_copy(src_ref=src(a, src_slot), dst_ref=out_refs[a].at[dst_slot],
                                            send_sem=send_sems.at[3 * a + k], recv_sem=recv_sems.at[3 * a + k],
                                            device_id=(px, py, c), device_id_type=MESH)

    def mine():
        return [pltpu.make_async_copy(src(a, me), out_refs[a].at[me], local_sems.at[a]) for a in range(n)]

    def sends():
        return [copy(a, k, 2 * px + py, me, px, py) for a in range(n) for k, (px, py) in enumerate(chips)]

    def start():
        for cp in mine() + sends():
            cp.start()

    def finish():
        for a in range(n):
            for k, (px, py) in enumerate(chips):
                copy(a, k, me, 2 * px + py, px, py).wait_recv()
        for cp in sends():
            cp.wait_send()
        for cp in mine():
            cp.wait()

    return start, finish


def _chip_exchange_shapes(arrays, gather):
    return [jax.ShapeDtypeStruct(((4,) + a.shape) if gather else a.shape, a.dtype) for a in arrays]


def _chip_exchange_sems(n):
    return [pltpu.SemaphoreType.DMA((3 * n,)), pltpu.SemaphoreType.DMA((3 * n,)), pltpu.SemaphoreType.DMA((n,))]


def _exchange_chips(name, arrays, gather):
    n = len(arrays)

    def body(*refs):
        start, finish = _chip_exchange(refs[:n], refs[n:2 * n], *refs[2 * n:], gather=gather)
        start()
        finish()

    return pl.pallas_call(
        body, name=name, in_specs=[_ANY] * n, out_specs=[_ANY] * n,
        out_shape=_chip_exchange_shapes(arrays, gather), scratch_shapes=_chip_exchange_sems(n),
    )(*arrays)


def _swap_sibling(name, arrays):
    n = len(arrays)

    def body(*refs):
        in_refs, out_refs, send_sems, recv_sems = refs[:n], refs[n:2 * n], refs[2 * n], refs[2 * n + 1]
        x, y, c = _place()
        cps = [pltpu.make_async_remote_copy(src_ref=in_refs[a], dst_ref=out_refs[a], send_sem=send_sems.at[a],
                                            recv_sem=recv_sems.at[a], device_id=(x, y, 1 - c), device_id_type=MESH)
               for a in range(n)]
        for cp in cps:
            cp.start()
        for cp in cps:
            cp.wait()

    return pl.pallas_call(
        body, name=name, in_specs=[_ANY] * n, out_specs=[_ANY] * n,
        out_shape=[jax.ShapeDtypeStruct(a.shape, a.dtype) for a in arrays],
        scratch_shapes=[pltpu.SemaphoreType.DMA((n,)), pltpu.SemaphoreType.DMA((n,))],
    )(*arrays)


def _allsum_small(name, pack):
    def body(p_ref, out_ref, slots, send_sems, recv_sems):
        x, y, c = _place()
        me = 4 * x + 2 * y + c
        slots[me] = p_ref[...]
        sends = []
        for k in range(1, 8):
            px = 1 - x if k & 4 else x
            py = 1 - y if k & 2 else y
            pc = 1 - c if k & 1 else c
            cp = pltpu.make_async_remote_copy(src_ref=p_ref, dst_ref=slots.at[me], send_sem=send_sems.at[k - 1],
                                              recv_sem=recv_sems.at[k - 1], device_id=(px, py, pc), device_id_type=MESH)
            cp.start()
            sends.append(cp)
        for k in range(1, 8):
            px = 1 - x if k & 4 else x
            py = 1 - y if k & 2 else y
            pc = 1 - c if k & 1 else c
            pltpu.make_async_remote_copy(src_ref=p_ref, dst_ref=slots.at[4 * px + 2 * py + pc],
                                         send_sem=send_sems.at[k - 1], recv_sem=recv_sems.at[k - 1],
                                         device_id=(px, py, pc), device_id_type=MESH).wait_recv()
        for cp in sends:
            cp.wait_send()
        acc = slots[0]
        for s in range(1, 8):
            acc = acc + slots[s]
        out_ref[...] = acc

    vm = pl.BlockSpec(memory_space=pltpu.VMEM)
    return pl.pallas_call(
        body, name=name, in_specs=[vm], out_specs=vm,
        out_shape=jax.ShapeDtypeStruct(pack.shape, pack.dtype),
        scratch_shapes=[pltpu.VMEM((8,) + pack.shape, pack.dtype), pltpu.SemaphoreType.DMA((7,)),
                        pltpu.SemaphoreType.DMA((7,))],
    )(pack)


def _pack_shards(parts, rows_total):
    flat = [p.reshape(-1, D_MODEL) for p in parts]
    used = sum(f.shape[0] for f in flat)
    return jnp.concatenate(flat + [jnp.zeros((rows_total - used, D_MODEL), flat[0].dtype)], axis=0)


def _unpack_shards(buf, shapes):
    out, r0 = [], 0
    for shp in shapes:
        n = math.prod(shp) // D_MODEL
        out.append(buf[r0:r0 + n].reshape(shp))
        r0 += n
    return out


def kernel(x, norm_mix, w_in, conv_w, a_log, dt_bias, dn_norm, w_proj_attn, w_proj_delta, w_out, norm_ffn, w_gate, w_up, w_down, norm_final, loss_target, m_norm_mix, m_w_in, m_conv_w, m_a_log, m_dt_bias, m_dn_norm, m_w_proj_attn, m_w_proj_delta, m_w_out, m_norm_ffn, m_w_gate, m_w_up, m_w_down, m_norm_final, v_norm_mix, v_w_in, v_conv_w, v_a_log, v_dt_bias, v_dn_norm, v_w_proj_attn, v_w_proj_delta, v_w_out, v_norm_ffn, v_w_gate, v_w_up, v_w_down, v_norm_final):
    T = x.shape[1]
    D = D_MODEL
    tm = 512
    tw = 256
    mt = 1024 if T % 1024 == 0 else 512
    x2 = x.reshape(T, D)
    tgt = loss_target.reshape(T, D)

    packed = (w_proj_attn, w_proj_delta, w_out, w_gate, w_up, w_down)
    conv_bits = lax.bitcast_convert_type(conv_w[0], BF16)
    w16 = _pack_shards([w[0].astype(BF16) for w in packed] + [conv_bits], PACK_R)
    win16 = w_in[0].astype(BF16)
    my_c = lax.axis_index("c")

    def both_halves(mine, other, axis):
        return jnp.concatenate([jnp.where(my_c == 0, mine, other), jnp.where(my_c == 0, other, mine)], axis=axis)

    pack_half = lax.dynamic_slice_in_dim(w16, my_c * HALF, HALF, axis=0)
    win_half = lax.dynamic_slice_in_dim(win16, my_c * W_IN_HALF, W_IN_HALF, axis=0)
    u, win_mine = _rowwise("rms_mix", _f_rms, [(x2, 0, D)], [norm_mix], [(D, BF16)], tm, T, side=([win_half], True))
    (win_other,) = _swap_sibling("swap_w_in", [win_mine])
    win4 = both_halves(win_mine, win_other, 1)

    def win_cols(g0, g1):
        out = []
        for j in range(4):
            lo, hi = max(g0, W_IN_COLS * j), min(g1, W_IN_COLS * (j + 1))
            if lo < hi:
                out.append(win4[j][:, lo - W_IN_COLS * j:hi - W_IN_COLS * j])
        return out

    Wa = jnp.concatenate([p for c0, w in _att_columns() for p in win_cols(c0, c0 + w)], axis=1)
    Wb = jnp.concatenate(win_cols(4608, 8704) + win_cols(8720, D_IN), axis=1)
    Wba = jnp.pad(jnp.concatenate(win_cols(8704, 8720), axis=1), ((0, 0), (0, 112)))
    alog_v = jnp.pad(a_log, ((0, 0), (8, 112)))
    dtb_v = jnp.pad(dt_bias, ((0, 0), (8, 112)))

    proj_a, pack_mine = _matmul("mm_in_a", u, Wa, "nn", T, D_PA, D, mt, 2304, D, F32, side=([pack_half], True))
    (pack_other,) = _swap_sibling("swap_pack", [pack_mine])
    wg = both_halves(pack_mine, pack_other, 1)

    def seg(r0, n):
        return wg[:, r0:r0 + n, :]

    r = 0
    Wpa = seg(r, 128).reshape(4, D_ATTN_OUT, 256).transpose(1, 0, 2).reshape(D_ATTN_OUT, D); r += 128
    Wpd = seg(r, 256).reshape(D, D); r += 256
    Wout = seg(r, 256).reshape(D, D); r += 256
    Wg = seg(r, 704).reshape(4, D, 704).transpose(1, 0, 2).reshape(D, D_FF); r += 704
    Wu = seg(r, 704).reshape(4, D, 704).transpose(1, 0, 2).reshape(D, D_FF); r += 704
    Wd = seg(r, 704).reshape(D_FF, D); r += 704
    convw = lax.bitcast_convert_type(seg(r, 6).reshape(4, CONV_WIDTH, 768, 2), F32)
    convw = convw.transpose(1, 0, 2).reshape(CONV_WIDTH, 3 * D_KEY_B)
    Wgu = jnp.concatenate([Wg, Wu], axis=1)
    proj_b = _matmul("mm_in_b", u, Wb, "nn", T, D_PB, D, mt, 3072, D, BF16)
    ba = _matmul("mm_ba", u, Wba, "nn", T, 128, D, mt, 128, D, F32)

    att = [_attn_fwd(f"attn_fwd{g}", proj_a, g, T) for g in range(3)]
    (ya_m,) = _rowwise("merge", _f_merge, [(att[g][0], 0, 512) for g in range(3)] + [(att[g][1], 0, 512) for g in range(3)],
                       [], [(512, BF16)], tm, T)
    qn, kn, vd, bg = _dn_pre_fwd("dn_pre_fwd", proj_b, ba, convw, alog_v, dtb_v, tw, T)
    o_d, s_all, t_all = _delta_fwd("delta_fwd", qn, kn, vd, bg, T)
    (od,) = _rowwise("post_dn", _f_postdn, [(o_d, 0, D), (proj_b, C_Z // D, D)], [dn_norm], [(D, BF16)], tm, T)
    y_a = _matmul("mm_pa", ya_m, Wpa, "nn", T, D, 512, mt, D, 512, BF16)
    y_b = _matmul("mm_pd", od, Wpd, "nn", T, D, D, 2 * mt, D, D, BF16)
    (mix,) = _rowwise("mix", _f_mix, [(y_a, 0, D), (y_b, 0, D), (proj_b, C_GATE // 2048, 2048)], [], [(D, BF16)], tm, T)
    h1 = _matmul("mm_out", mix, Wout, "nn", T, D, D, mt, D, D, F32, residual=x2)
    (hn,) = _rowwise("rms_ffn", _f_rms, [(h1, 0, D)], [norm_ffn], [(D, BF16)], tm, T)
    gu = _matmul("mm_gu", hn, Wgu, "nn", T, 2 * D_FF, D, mt, 2816, D, BF16)
    h2 = _matmul("mm_down", gu, Wd, "nn", T, D, D_FF, mt, D, 1408, F32, residual=h1, a_swiglu=D_FF)
    dh2, loss, g_norm_final = _loss_fb("loss", h2, norm_final.reshape(1, D), tgt, tm, T)

    dact = _matmul("mm_dact", dh2, Wd, "nt", T, D_FF, D, mt, D_FF, D, BF16)
    gWd = _matmul("mm_gwd", gu, dh2, "tn", D_FF, D, T, 1408, D, mt, BF16, a_swiglu=D_FF)
    (dgu,) = _rowwise_vjp("act_bwd", _f_act, [(gu, 0, 2 * D_FF)], [], [(dact, 0, D_FF)], [BF16], tw, T)
    dhn = _matmul("mm_dhn", dgu, Wgu, "nt", T, D, 2 * D_FF, mt, D, 2816, F32)
    gWgu = _matmul("mm_gwgu", hn, dgu, "tn", D, 2 * D_FF, T, D, 1408, 2 * mt, BF16)
    dh1, g_norm_ffn = _rowwise_vjp("rms_ffn_bwd", _f_rms, [(h1, 0, D)], [norm_ffn], [(dhn, 0, D)], [F32], tm, T,
                                   residual=(dh2, 0, D))
    dmix = _matmul("mm_dmix", dh1, Wout, "nt", T, D, D, mt, D, D, BF16)
    gWout = _matmul("mm_gwout", mix, dh1, "tn", D, D, T, D, D, mt, BF16)
    dya, dyb, dgate = _rowwise_vjp("mix_bwd", _f_mix, [(y_a, 0, D), (y_b, 0, D), (proj_b, C_GATE // 2048, 2048)], [],
                                   [(dmix, 0, D)], [BF16, BF16, BF16], tm, T)
    dya_m = _matmul("mm_dyam", dya, Wpa, "nt", T, 512, D, mt, 512, D, BF16)
    gWpa = _matmul("mm_gwpa", ya_m, dya, "tn", 512, D, T, 512, D, mt, BF16)
    dod = _matmul("mm_dod", dyb, Wpd, "nt", T, D, D, 2 * mt, D, D, BF16)
    gWpd = _matmul("mm_gwpd", od, dyb, "tn", D, D, T, D, D, mt, BF16)
    do_d, dz, g_dn_norm = _rowwise_vjp("post_dn_bwd", _f_postdn, [(o_d, 0, D), (proj_b, C_Z // D, D)], [dn_norm],
                                       [(dod, 0, D)], [BF16, BF16], tm, T)
    dqn, dkn, dvd, dbg = _delta_bwd("delta_bwd", qn, kn, vd, bg, s_all, t_all, do_d, T)
    dqkvd, dba, g_conv8, g_alog_v, g_dtb_v = _dn_pre_bwd("dn_pre_bwd", proj_b, ba, convw, alog_v, dtb_v,
                                                         dqn, dkn, dvd, dbg, tw, T)
    def pair_sum(tag, g, h):
        keep = lax.dynamic_slice_in_dim(g, my_c * h, h, axis=1)
        give = lax.dynamic_slice_in_dim(g, (1 - my_c) * h, h, axis=1)
        (got,) = _swap_sibling("swap_grads_" + tag, [give])
        return _add2("add_cores_" + tag, keep.reshape(-1, g.shape[2]), got.reshape(-1, g.shape[2]), HALF_TILE).reshape(keep.shape)

    def cols4(g, n):
        return g.reshape(g.shape[0], 4, n).transpose(1, 0, 2).reshape(4, -1, D)

    parts = [cols4(gWpa, 256), gWpd.reshape(4, 256, D), gWout.reshape(4, 256, D), cols4(gWgu[:, :D_FF], 704),
             cols4(gWgu[:, D_FF:], 704), gWd.reshape(4, 704, D), cols4(g_conv8[:CONV_WIDTH], 768)]
    used = sum(p.shape[1] for p in parts)
    g_pack = jnp.concatenate([p.astype(BF16) for p in parts] + [jnp.zeros((4, PACK_R - used, D), BF16)], axis=1)
    pair_pack = pair_sum("pack", g_pack, HALF)

    mg = _rowwise_vjp("merge_bwd", _f_merge,
                      [(att[g][0], 0, 512) for g in range(3)] + [(att[g][1], 0, 512) for g in range(3)], [],
                      [(dya_m, 0, 512)], [F32] * 6, tm, T)
    dproj_a = None
    for g in range(3):
        dproj_a = _attn_bwd(f"attn_bwd{g}", proj_a, att[g][0], att[g][1], mg[g], mg[3 + g], g, T, dproj_a)
    dproj_b = [dqkvd, dz, dgate]
    gWa, recv_pack = _matmul("mm_gwa", u, dproj_a, "tn", D, D_PA, T, D, 1536, 2 * mt, BF16, side=([pair_pack], False))
    gWb = _matmul_pieces("mm_gwb", "tn", dproj_b, u, D, D_PB, T, D, 1024, 2 * mt, BF16)
    gWba = _matmul("mm_gwba", u, dba, "tn", D, 128, T, D, 128, mt, BF16)

    pieces, off = [(gWb[:, :C_GATE], D_PA), (gWba[:, :16], 8704), (gWb[:, C_GATE:], 8720)], 0
    for c0, w in _att_columns():
        pieces.append((gWa[:, off:off + w], c0))
        off += w
    pieces.sort(key=lambda t: t[1])

    def win_shard(j):
        out = []
        for arr, g0 in pieces:
            lo, hi = max(g0, W_IN_COLS * j), min(g0 + arr.shape[1], W_IN_COLS * (j + 1))
            if lo < hi:
                out.append(arr[:, lo - g0:hi - g0])
        return jnp.concatenate(out, axis=1)

    g_win4 = jnp.concatenate([win_shard(j)[None] for j in range(4)], axis=0)
    pair_win = pair_sum("w_in", g_win4, W_IN_HALF)

    du0 = _matmul("mm_du_ba", dba, Wba, "nt", T, D, 128, mt, D, 128, F32)
    du1 = _matmul_pieces("mm_du_b", "nt", dproj_b, Wb, T, D, D_PB, mt, D, 1024, F32, residual=du0)
    du, recv_win = _matmul("mm_du_a", dproj_a, Wa, "nt", T, D, D_PA, mt, D, 2304, F32, residual=du1, side=([pair_win], False))
    dx, g_norm_mix = _rowwise_vjp("rms_mix_bwd", _f_rms, [(x2, 0, D)], [norm_mix], [(du, 0, D)], [F32], tm, T,
                                  residual=(dh1, 0, D))

    g_mine = [_sum4("sum_chips_pack", recv_pack, HALF_TILE), _sum4("sum_chips_w_in", recv_win, HALF_TILE)]
    g_other = _swap_sibling("swap_sums", g_mine)
    g_all = both_halves(g_mine[0], g_other[0], 0)
    g_win = both_halves(g_mine[1], g_other[1], 0)

    locals_ = (w_proj_attn, w_proj_delta, w_out, w_gate, w_up, w_down, conv_w, w_in)
    ms = (m_w_proj_attn, m_w_proj_delta, m_w_out, m_w_gate, m_w_up, m_w_down, m_conv_w, m_w_in)
    vs = (v_w_proj_attn, v_w_proj_delta, v_w_out, v_w_gate, v_w_up, v_w_down, v_conv_w, v_w_in)
    big_names = ("w_proj_attn", "w_proj_delta", "w_out", "w_gate", "w_up", "w_down", "conv_w", "w_in")
    adam_tiles = (512, 256, 256, 256, 256, 352, CONV_WIDTH, 128)
    g_shards = _unpack_shards(g_all, [w.shape for w in locals_[:-1]]) + [g_win.reshape(w_in.shape)]
    big = [g_shards, [], [], []]
    for nm, w, g, m_, v_, tl in zip(big_names, locals_, g_shards, ms, vs, adam_tiles):
        res = _adamw("adamw_" + nm, w[0], g[0], m_[0], v_[0], tl)
        for kind in range(3):
            big[kind + 1].append(res[kind].reshape(w.shape))

    def small_pack(nm, nf, nfin, dn, al, dt, extra):
        row3 = jnp.concatenate([dn.reshape(1, 128), al.reshape(1, 8), dt.reshape(1, 8), extra.reshape(1, 1),
                                jnp.zeros((1, D - 145), F32)], axis=1)
        return jnp.concatenate([nm.reshape(1, D), nf.reshape(1, D), nfin.reshape(1, D), row3, jnp.zeros((4, D), F32)], axis=0)

    zero1 = jnp.zeros((1, 1), F32)
    sp = small_pack(g_norm_mix, g_norm_ffn, g_norm_final, g_dn_norm, g_alog_v[:, 8:16], g_dtb_v[:, 8:16], loss)
    tot = _allsum_small("allsum_small", sp)
    sm = _adamw("adamw_small", small_pack(norm_mix, norm_ffn, norm_final, dn_norm, a_log, dt_bias, zero1), tot,
                small_pack(m_norm_mix, m_norm_ffn, m_norm_final, m_dn_norm, m_a_log, m_dt_bias, zero1),
                small_pack(v_norm_mix, v_norm_ffn, v_norm_final, v_dn_norm, v_a_log, v_dt_bias, zero1), 8)
    sm = [tot] + list(sm)

    def small_unpack(p):
        return {"norm_mix": p[0:1], "norm_ffn": p[1:2], "norm_final": p[2], "dn_norm": p[3:4, :128],
                "a_log": p[3:4, 128:136], "dt_bias": p[3:4, 136:144]}

    sm = [small_unpack(p) for p in sm]
    loss_out = tot[3, 144]

    order = ("norm_mix", "w_in", "conv_w", "a_log", "dt_bias", "dn_norm", "w_proj_attn", "w_proj_delta", "w_out",
             "norm_ffn", "w_gate", "w_up", "w_down", "norm_final")
    outs = [loss_out, dx.reshape(x.shape)]
    for kind in range(4):
        for n in order:
            outs.append(big[kind][big_names.index(n)] if n in big_names else sm[kind][n])
    return tuple(outs)
```

```python
import functools
import math

import jax
import jax.numpy as jnp
from jax import lax
from jax.experimental import pallas as pl
from jax.experimental.pallas import tpu as pltpu

F32 = jnp.float32
BF16 = jnp.bfloat16
MESH = pl.DeviceIdType.MESH

D_MODEL = 1024
N_HEADS_A = 12
HEAD_DIM = 128
BLOCK_A = 128
DILATIONS = (1, 4, 16)
W_SUB = 128
D_ATTN_OUT = 512
N_HEADS_B = 8
D_KEY_B = 1024
CONV_WIDTH = 4
CHUNK = 64
CHUNKS_FWD, CHUNKS_BWD = 2, 1
D_FF = 2816
EPS = 1e-6
D_IN = 10768
D_PA = 4608
D_PB = 6144
C_Z, C_GATE = 3072, 4096

ADAM_LR, ADAM_B1, ADAM_B2, ADAM_EPS, ADAM_WD, ADAM_STEP = 0.001, 0.9, 0.999, 1e-08, 0.01, 10

PACK_R = 2816
HALF = PACK_R // 2
HALF_TILE = 128
W_IN_COLS = 2692
W_IN_HALF = D_MODEL // 2

VMEM_LIMIT = 48 * 1024 * 1024


def _cparams(n):
    return pltpu.CompilerParams(dimension_semantics=("arbitrary",) * n, vmem_limit_bytes=VMEM_LIMIT)


_ARB = _cparams(1)


def _row_spec(tm, w, cb):
    return pl.BlockSpec((tm, w), lambda i: (i, cb))


def _full_spec(shape):
    nd = len(shape)
    return pl.BlockSpec(shape, lambda i: (0,) * nd)


def _rowwise(name, fn, ins, params, outs, tm, T, side=None):
    n_in, n_p, n_out = len(ins), len(params), len(outs)
    n_side = 0 if side is None else len(side[0])
    steps = T // tm

    def body(*refs):
        k0 = n_in + n_p + n_side
        if n_side:
            start, finish = _chip_exchange(refs[n_in + n_p:k0], refs[k0 + n_out:k0 + n_out + n_side],
                                           *refs[k0 + n_out + n_side:], gather=side[1])
            pl.when(pl.program_id(0) == 0)(start)
        xs = [r[...].astype(F32) for r in refs[:n_in]] + [r[...] for r in refs[n_in:n_in + n_p]]
        res = fn(*xs)
        for o_ref, v in zip(refs[k0:k0 + n_out], res):
            o_ref[...] = v.astype(o_ref.dtype)
        if n_side:
            pl.when(pl.program_id(0) == steps - 1)(finish)

    side_arrays = [] if side is None else list(side[0])
    return pl.pallas_call(
        body, name=name, grid=(steps,),
        in_specs=([_row_spec(tm, w, cb) for (_, cb, w) in ins] + [_full_spec(p.shape) for p in params] + [_ANY] * n_side),
        out_specs=[_row_spec(tm, w, 0) for (w, _) in outs] + [_ANY] * n_side,
        out_shape=([jax.ShapeDtypeStruct((T, w), dt) for (w, dt) in outs]
                   + ([] if side is None else _chip_exchange_shapes(*side))),
        scratch_shapes=_chip_exchange_sems(n_side) if n_side else [],
        compiler_params=_ARB,
    )(*[a for (a, _, _) in ins], *params, *side_arrays)


def _rowwise_vjp(name, fn, ins, params, cts, gdtypes, tm, T, residual=None):
    n_in, n_p, n_ct = len(ins), len(params), len(cts)
    n_res = 0 if residual is None else 1

    def body(*refs):
        i = pl.program_id(0)
        k = 0
        xs = [r[...].astype(F32) for r in refs[k:k + n_in]]
        k += n_in
        ps = [r[...] for r in refs[k:k + n_p]]
        k += n_p
        gs = tuple(r[...].astype(F32) for r in refs[k:k + n_ct])
        k += n_ct
        res_refs = refs[k:k + n_res]
        k += n_res
        gi_refs = refs[k:k + n_in]
        gp_refs = refs[k + n_in:]
        _, vjp = jax.vjp(fn, *xs, *ps)
        g = vjp(gs)
        for j, (ref, v) in enumerate(zip(gi_refs, g[:n_in])):
            if j == 0 and n_res:
                v = v + res_refs[0][...].astype(F32)
            ref[...] = v.astype(ref.dtype)

        @pl.when(i == 0)
        def _():
            for ref in gp_refs:
                ref[...] = jnp.zeros_like(ref)

        for ref, v in zip(gp_refs, g[n_in:]):
            ref[...] += v

    res_in = [] if residual is None else [residual]
    return pl.pallas_call(
        body, name=name, grid=(T // tm,),
        in_specs=([_row_spec(tm, w, cb) for (_, cb, w) in ins] + [_full_spec(p.shape) for p in params]
                  + [_row_spec(tm, w, cb) for (_, cb, w) in cts] + [_row_spec(tm, w, cb) for (_, cb, w) in res_in]),
        out_specs=[_row_spec(tm, w, 0) for (_, _, w) in ins] + [_full_spec(p.shape) for p in params],
        out_shape=([jax.ShapeDtypeStruct((T, w), dt) for (_, _, w), dt in zip(ins, gdtypes)]
                   + [jax.ShapeDtypeStruct(p.shape, F32) for p in params]),
        compiler_params=_ARB,
    )(*[a for (a, _, _) in ins], *params, *[a for (a, _, _) in cts], *[a for (a, _, _) in res_in])


def _matmul(name, a, b, mode, M, N, K, tm, tn, tk, out_dtype, a_off=0, b_off=0, residual=None, side=None, a_swiglu=None):
    nk = K // tk
    n_side = 0 if side is None else len(side[0])
    assert M % tm == 0 and N % tn == 0 and K % tk == 0
    if mode == "nn":
        assert a_off % tk == 0 and b_off % tn == 0
        a_spec = pl.BlockSpec((tm, tk), lambda i, j, k: (i, a_off // tk + k))
        b_spec = pl.BlockSpec((tk, tn), lambda i, j, k: (k, b_off // tn + j))
        dims = (((1,), (0,)), ((), ()))
    elif mode == "nt":
        assert a_off % tk == 0 and b_off == 0
        a_spec = pl.BlockSpec((tm, tk), lambda i, j, k: (i, a_off // tk + k))
        b_spec = pl.BlockSpec((tn, tk), lambda i, j, k: (j, k))
        dims = (((1,), (1,)), ((), ()))
    else:
        assert a_off % tm == 0 and b_off % tn == 0
        a_spec = pl.BlockSpec((tk, tm), lambda i, j, k: (k, a_off // tm + i))
        b_spec = pl.BlockSpec((tk, tn), lambda i, j, k: (k, b_off // tn + j))
        dims = (((0,), (0,)), ((), ()))
    has_res = residual is not None
    has_pair = a_swiglu is not None
    if has_pair:
        blk = tm if mode == "tn" else tk
        assert mode in ("nn", "tn") and a_swiglu % blk == 0
        shift = a_swiglu // blk
        a2_spec = (pl.BlockSpec((tm, tk), lambda i, j, k: (i, a_off // tk + shift + k)) if mode == "nn" else
                   pl.BlockSpec((tk, tm), lambda i, j, k: (k, a_off // tm + shift + i)))

    n_main = 2 + has_res + has_pair
    grid = (M // tm, N // tn, nk)

    def a_tile(refs):
        if not has_pair:
            return refs[0][...].astype(BF16)
        g, up = refs[0][...].astype(F32), refs[n_main - 1][...].astype(F32)
        return (_silu(g) * up).astype(BF16)

    def body(*refs):
        b_ref = refs[1]
        r_ref = refs[2] if has_res else None
        o_ref = refs[n_main + n_side]
        acc_ref = refs[n_main + 2 * n_side + 1]
        i, j, k = pl.program_id(0), pl.program_id(1), pl.program_id(2)
        if n_side:
            start, finish = _chip_exchange(refs[n_main:n_main + n_side], refs[n_main + n_side + 1:n_main + 2 * n_side + 1],
                                           *refs[n_main + 2 * n_side + 2:], gather=side[1])
            pl.when((i == 0) & (j == 0) & (k == 0))(start)

        @pl.when(k == 0)
        def _():
            acc_ref[...] = jnp.zeros_like(acc_ref)

        acc_ref[...] += lax.dot_general(a_tile(refs), b_ref[...].astype(BF16), dims, preferred_element_type=F32)

        @pl.when(k == nk - 1)
        def _():
            acc = acc_ref[...]
            if has_res:
                acc = acc + r_ref[...].astype(F32)
            o_ref[...] = acc.astype(o_ref.dtype)

        if n_side:
            pl.when((i == grid[0] - 1) & (j == grid[1] - 1) & (k == nk - 1))(finish)

    in_specs = [a_spec, b_spec]
    args = [a, b]
    if has_res:
        in_specs.append(pl.BlockSpec((tm, tn), lambda i, j, k: (i, j)))
        args.append(residual)
    if has_pair:
        in_specs.append(a2_spec)
        args.append(a)
    out_specs = pl.BlockSpec((tm, tn), lambda i, j, k: (i, j))
    out_shape = jax.ShapeDtypeStruct((M, N), out_dtype)
    scratch = [pltpu.VMEM((tm, tn), F32)]
    semantics = ("parallel", "parallel", "arbitrary")
    if n_side:
        in_specs += [_ANY] * n_side
        args += list(side[0])
        out_specs = [out_specs] + [_ANY] * n_side
        out_shape = [out_shape] + _chip_exchange_shapes(*side)
        scratch += _chip_exchange_sems(n_side)
        semantics = ("arbitrary",) * 3
    return pl.pallas_call(
        body, name=name, grid=grid, in_specs=in_specs, out_specs=out_specs, out_shape=out_shape, scratch_shapes=scratch,
        compiler_params=pltpu.CompilerParams(dimension_semantics=semantics, vmem_limit_bytes=VMEM_LIMIT),
    )(*args)


def _matmul_pieces(name, mode, pieces, other, M, N, K, tm, tn, tk, out_dtype, residual=None):
    nk = K // tk
    blk = tn if mode == "tn" else tk
    counts = [p.shape[1] // blk for p in pieces]
    assert all(p.shape[1] % blk == 0 for p in pieces) and sum(counts) == (N if mode == "tn" else K) // blk
    starts = [sum(counts[:p]) for p in range(len(pieces))]
    n_p = len(pieces)
    has_res = residual is not None

    def piece_spec(s, n):
        if mode == "tn":
            return pl.BlockSpec((tk, tn), lambda i, j, k: (jnp.where((j >= s) & (j < s + n), k, 0), jnp.clip(j - s, 0, n - 1)))
        return pl.BlockSpec((tm, tk), lambda i, j, k: (i, jnp.clip(k - s, 0, n - 1)))

    if mode == "tn":
        other_spec = pl.BlockSpec((tk, tm), lambda i, j, k: (k, i))
    else:
        other_spec = pl.BlockSpec((tn, tk), lambda i, j, k: (j, k))

    def body(*refs):
        other_ref, piece_refs = refs[0], refs[1:1 + n_p]
        r_ref = refs[1 + n_p] if has_res else None
        o_ref, acc_ref = refs[-2], refs[-1]
        k = pl.program_id(2)
        sel = pl.program_id(1) if mode == "tn" else k

        @pl.when(k == 0)
        def _():
            acc_ref[...] = jnp.zeros_like(acc_ref)

        for p_ref, s, n in zip(piece_refs, starts, counts):
            @pl.when((sel >= s) & (sel < s + n))
            def _(p_ref=p_ref):
                if mode == "tn":
                    acc_ref[...] += lax.dot_general(other_ref[...].astype(BF16), p_ref[...].astype(BF16),
                                                    (((0,), (0,)), ((), ())), preferred_element_type=F32)
                else:
                    acc_ref[...] += lax.dot_general(p_ref[...].astype(BF16), other_ref[...].astype(BF16),
                                                    (((1,), (1,)), ((), ())), preferred_element_type=F32)

        @pl.when(k == nk - 1)
        def _():
            acc = acc_ref[...]
            if has_res:
                acc = acc + r_ref[...].astype(F32)
            o_ref[...] = acc.astype(o_ref.dtype)

    in_specs = [other_spec] + [piece_spec(s, n) for s, n in zip(starts, counts)]
    args = [other] + list(pieces)
    if has_res:
        in_specs.append(pl.BlockSpec((tm, tn), lambda i, j, k: (i, j)))
        args.append(residual)
    return pl.pallas_call(
        body, name=name, grid=(M // tm, N // tn, nk),
        in_specs=in_specs,
        out_specs=pl.BlockSpec((tm, tn), lambda i, j, k: (i, j)),
        out_shape=jax.ShapeDtypeStruct((M, N), out_dtype),
        scratch_shapes=[pltpu.VMEM((tm, tn), F32)],
        compiler_params=pltpu.CompilerParams(dimension_semantics=("parallel", "parallel", "arbitrary"),
                                             vmem_limit_bytes=VMEM_LIMIT),
    )(*args)


def _f_rms(x, w):
    return (x * lax.rsqrt(jnp.mean(x * x, axis=-1, keepdims=True) + EPS) * w,)


def _silu(x):
    return x * jax.nn.sigmoid(x)


def _softplus(x):
    return jnp.maximum(x, 0.0) + jnp.log1p(jnp.exp(-jnp.abs(x)))


def _heads(x, n):
    return [x[:, HEAD_DIM * h:HEAD_DIM * (h + 1)] for h in range(n)]


def _f_dnpoint(c, ba, alog_v, dtb_v):
    cs = _silu(c)
    q, k, v = cs[:, :D_KEY_B], cs[:, D_KEY_B:2 * D_KEY_B], cs[:, 2 * D_KEY_B:]
    qn = jnp.concatenate([t * lax.rsqrt(jnp.sum(t * t, axis=-1, keepdims=True) + EPS) * (HEAD_DIM ** -0.5)
                          for t in _heads(q, N_HEADS_B)], axis=1)
    kn = jnp.concatenate([t * lax.rsqrt(jnp.sum(t * t, axis=-1, keepdims=True) + EPS)
                          for t in _heads(k, N_HEADS_B)], axis=1)
    lane = lax.broadcasted_iota(jnp.int32, ba.shape, 1)
    beta = jax.nn.sigmoid(ba)
    g = -jnp.exp(alog_v) * _softplus(ba + dtb_v)
    bg = jnp.where(lane < N_HEADS_B, beta, jnp.where(lane < 2 * N_HEADS_B, g, 0.0))
    return qn, kn, v, bg


def _dnpoint_bwd(c, ba, alog_v, dtb_v, dq, dk, dv, dbg):
    sig = jax.nn.sigmoid(c)
    cs = c * sig

    def l2_bwd(t, dy, scale):
        outs = []
        for th, dh in zip(_heads(t, N_HEADS_B), _heads(dy, N_HEADS_B)):
            r = lax.rsqrt(jnp.sum(th * th, axis=-1, keepdims=True) + EPS)
            y = th * r
            outs.append((scale * r) * (dh - y * jnp.sum(dh * y, axis=-1, keepdims=True)))
        return outs

    dcs = jnp.concatenate(l2_bwd(cs[:, :D_KEY_B], dq, HEAD_DIM ** -0.5) + l2_bwd(cs[:, D_KEY_B:2 * D_KEY_B], dk, 1.0) + [dv],
                          axis=1)
    dc = dcs * (sig * (1.0 + c * (1.0 - sig)))
    lane = lax.broadcasted_iota(jnp.int32, ba.shape, 1)
    is_beta, is_g = lane < N_HEADS_B, (lane >= N_HEADS_B) & (lane < 2 * N_HEADS_B)
    beta = jax.nn.sigmoid(ba)
    z = ba + dtb_v
    minus_ea = -jnp.exp(alog_v)
    dz = jnp.where(is_g, dbg * minus_ea * jax.nn.sigmoid(z), 0.0)
    dba = jnp.where(is_beta, dbg * beta * (1.0 - beta), dz)
    dal = jnp.sum(jnp.where(is_g, dbg * minus_ea * _softplus(z), 0.0), axis=0, keepdims=True)
    ddt = jnp.sum(dz, axis=0, keepdims=True)
    return dc, dba, dal, ddt


def _f_postdn(o, z, w):
    outs = []
    for oh, zh in zip(_heads(o, N_HEADS_B), _heads(z, N_HEADS_B)):
        outs.append(oh * lax.rsqrt(jnp.mean(oh * oh, axis=-1, keepdims=True) + EPS) * w * _silu(zh))
    return (jnp.concatenate(outs, axis=1),)


def _f_merge(o0, o1, o2, l0, l1, l2):
    m = lax.stop_gradient(jnp.maximum(jnp.maximum(l0, l1), l2))
    e0, e1, e2 = jnp.exp(l0 - m), jnp.exp(l1 - m), jnp.exp(l2 - m)
    return ((e0 * o0 + e1 * o1 + e2 * o2) / (e0 + e1 + e2),)


def _f_mix(ya, yb, gr):
    return (jax.nn.sigmoid(gr[:, :D_MODEL]) * ya + jax.nn.sigmoid(gr[:, D_MODEL:]) * yb,)


def _f_act(gu):
    return (_silu(gu[:, :D_FF]) * gu[:, D_FF:],)


def _attn_block(q, kp, kc, vp, vc, slope_d, first_key):
    k2 = jnp.concatenate([kp, kc], axis=0).astype(BF16)
    v2 = jnp.concatenate([vp, vc], axis=0).astype(BF16)
    s = lax.dot_general(q.astype(BF16), k2, (((1,), (1,)), ((), ())), preferred_element_type=F32)
    s = s * (HEAD_DIM ** -0.5)
    i = lax.broadcasted_iota(jnp.int32, (BLOCK_A, 2 * BLOCK_A), 0)
    j = lax.broadcasted_iota(jnp.int32, (BLOCK_A, 2 * BLOCK_A), 1)
    delta = BLOCK_A + i - j
    valid = (delta >= 0) & (delta <= W_SUB) & (j >= first_key)
    s = jnp.where(valid, s - slope_d * delta.astype(F32), -jnp.inf)
    m = lax.stop_gradient(jnp.max(s, axis=-1, keepdims=True))
    p = jnp.exp(s - m)
    den = jnp.sum(p, axis=-1, keepdims=True)
    o = jnp.dot(p.astype(BF16), v2, preferred_element_type=F32) / den
    lse = jnp.broadcast_to(m + jnp.log(den), (BLOCK_A, HEAD_DIM))
    return o, lse


def _slope_d(group, h, d):
    hg = (group * 4 + h + 1).astype(F32)
    return jnp.exp(jnp.full((1, 1), -8.0 * math.log(2.0) / N_HEADS_A, F32) * hg) * float(d)


ATT_CFG = ((1, 4, 4), (4, 2, 1), (16, 1, 1))
D_GROUP = 1536


def _att_columns():
    out = []
    for g, (_, _, hp) in enumerate(ATT_CFG):
        lanes = HEAD_DIM * hp
        for hb in range(D_ATTN_OUT // lanes):
            for which in range(3):
                out.append((N_HEADS_A * HEAD_DIM * which + D_ATTN_OUT * g + lanes * hb, lanes))
    return out
R_UNROLL = 4


def _attn_geometry(group, T):
    d, m, hp = ATT_CFG[group]
    span = BLOCK_A * d
    tile = m * span
    assert T % tile == 0 and 4 % hp == 0
    return d, m, hp, span, tile, T // tile, HEAD_DIM * hp


def _sub_rows(b, span, r, d):
    return pl.ds(b * span + r, BLOCK_A, stride=d) if d > 1 else pl.ds(b * span, BLOCK_A)


def _for_each_r(d, fn):
    if d <= R_UNROLL:
        for r in range(d):
            fn(r)
    else:
        def step(r, carry):
            fn(r)
            return carry

        lax.fori_loop(0, d, step, 0, unroll=R_UNROLL)


def _attn_fwd(name, proj, group, T):
    d, m, hp, span, tile, nt, lanes = _attn_geometry(group, T)
    qb, kb, vb = [(D_GROUP * group) // lanes + which for which in range(3)]

    def body(q_ref, kc_ref, kp_ref, vc_ref, vp_ref, o_ref, l_ref):
        hb, n = pl.program_id(0), pl.program_id(1)
        first_key = jnp.where(n > 0, 0, BLOCK_A)
        for hh in range(hp):
            ls = slice(HEAD_DIM * hh, HEAD_DIM * (hh + 1))
            sl = _slope_d(group, hb * hp + hh, d)
            for b in range(m):
                def one(r, b=b, ls=ls, sl=sl):
                    rq = _sub_rows(b, span, r, d)
                    if b == 0:
                        rp = _sub_rows(0, span, r, d)
                        kp, vp, fk = kp_ref[rp, ls], vp_ref[rp, ls], first_key
                    else:
                        rp = _sub_rows(b - 1, span, r, d)
                        kp, vp, fk = kc_ref[rp, ls], vc_ref[rp, ls], 0
                    o, lse = _attn_block(q_ref[rq, ls], kp, kc_ref[rq, ls], vp, vc_ref[rq, ls], sl, fk)
                    o_ref[rq, ls] = o
                    l_ref[rq, ls] = lse

                _for_each_r(d, one)

    def cur(cb):
        return pl.BlockSpec((tile, lanes), lambda hb, n: (n, cb + 3 * hb))

    def prev(cb):
        return pl.BlockSpec((span, lanes), lambda hb, n: (jnp.maximum(m * n - 1, 0), cb + 3 * hb))

    return pl.pallas_call(
        body, name=name, grid=(4 // hp, nt),
        in_specs=[cur(qb), cur(kb), prev(kb), cur(vb), prev(vb)],
        out_specs=[pl.BlockSpec((tile, lanes), lambda hb, n: (n, hb))] * 2,
        out_shape=[jax.ShapeDtypeStruct((T, D_ATTN_OUT), F32)] * 2,
        compiler_params=_cparams(2),
    )(proj, proj, proj, proj, proj)


def _attn_block_bwd(q, kp, kc, vp, vc, o, lse, do, dl, slope_d, first_key):
    scale = HEAD_DIM ** -0.5
    nt_dims = (((1,), (1,)), ((), ()))
    tn_dims = (((0,), (0,)), ((), ()))
    qb = q.astype(BF16)
    k2 = jnp.concatenate([kp, kc], axis=0).astype(BF16)
    v2 = jnp.concatenate([vp, vc], axis=0).astype(BF16)
    dob = do.astype(BF16)
    s = lax.dot_general(qb, k2, nt_dims, preferred_element_type=F32) * scale
    i = lax.broadcasted_iota(jnp.int32, (BLOCK_A, 2 * BLOCK_A), 0)
    j = lax.broadcasted_iota(jnp.int32, (BLOCK_A, 2 * BLOCK_A), 1)
    delta = BLOCK_A + i - j
    valid = (delta >= 0) & (delta <= W_SUB) & (j >= first_key)
    lse_col = jnp.max(lse, axis=-1, keepdims=True)
    p = jnp.exp(jnp.where(valid, s - slope_d * delta.astype(F32), -jnp.inf) - lse_col)
    dp = lax.dot_general(dob, v2, nt_dims, preferred_element_type=F32)
    row = jnp.sum(dl, axis=-1, keepdims=True) - jnp.sum(do * o, axis=-1, keepdims=True)
    ds = (p * (dp + row)).astype(BF16)
    dv2 = lax.dot_general(p.astype(BF16), dob, tn_dims, preferred_element_type=F32)
    dq = jnp.dot(ds, k2, preferred_element_type=F32) * scale
    dk2 = lax.dot_general(ds, qb, tn_dims, preferred_element_type=F32) * scale
    return dq, dk2[:BLOCK_A], dk2[BLOCK_A:], dv2[:BLOCK_A], dv2[BLOCK_A:]


def _attn_bwd(name, proj, o, lse, do, dl, group, T, dest):
    d, m, hp, span, tile, nt, lanes = _attn_geometry(group, T)
    qb, kb, vb = [(D_GROUP * group) // lanes + which for which in range(3)]
    n_in = 9 if dest is None else 10

    def body(*refs):
        q_ref, kc_ref, kp_ref, vc_ref, vp_ref, o_ref, l_ref, do_ref, dl_ref = refs[:9]
        out_ref, aq_ref, ak_ref, av_ref = refs[n_in:]
        hb, n = pl.program_id(0), pl.program_id(1)
        first_key = jnp.where(n > 0, 0, BLOCK_A)
        cur_slot = n % 2
        old_slot = 1 - cur_slot

        @pl.when(n == 0)
        def _():
            aq_ref[...] = jnp.zeros_like(aq_ref)
            ak_ref[...] = jnp.zeros_like(ak_ref)
            av_ref[...] = jnp.zeros_like(av_ref)

        @pl.when(n < nt)
        def _():
            for hh in range(hp):
                ls = slice(HEAD_DIM * hh, HEAD_DIM * (hh + 1))
                sl = _slope_d(group, hb * hp + hh, d)
                for b in range(m):
                    def one(r, b=b, ls=ls, sl=sl):
                        rq = _sub_rows(b, span, r, d)
                        if b == 0:
                            rp = _sub_rows(0, span, r, d)
                            kp, vp, fk = kp_ref[rp, ls], vp_ref[rp, ls], first_key
                        else:
                            rp = _sub_rows(b - 1, span, r, d)
                            kp, vp, fk = kc_ref[rp, ls], vc_ref[rp, ls], 0
                        dq, dkp, dkc, dvp, dvc = _attn_block_bwd(
                            q_ref[rq, ls], kp, kc_ref[rq, ls], vp, vc_ref[rq, ls], o_ref[rq, ls], l_ref[rq, ls],
                            do_ref[rq, ls], dl_ref[rq, ls], sl, fk)
                        aq_ref[cur_slot, rq, ls] = dq
                        ak_ref[cur_slot, rq, ls] = dkc
                        av_ref[cur_slot, rq, ls] = dvc
                        if b == 0:
                            last = _sub_rows(m - 1, span, r, d)
                            ak_ref[old_slot, last, ls] += dkp
                            av_ref[old_slot, last, ls] += dvp
                        else:
                            ak_ref[cur_slot, rp, ls] += dkp
                            av_ref[cur_slot, rp, ls] += dvp

                    _for_each_r(d, one)

        out_ref[:, 0:lanes] = aq_ref[old_slot].astype(BF16)
        out_ref[:, lanes:2 * lanes] = ak_ref[old_slot].astype(BF16)
        out_ref[:, 2 * lanes:3 * lanes] = av_ref[old_slot].astype(BF16)

    def cur(cb):
        return pl.BlockSpec((tile, lanes), lambda hb, n: (jnp.minimum(n, nt - 1), cb + 3 * hb))

    def prev(cb):
        return pl.BlockSpec((span, lanes), lambda hb, n: (jnp.maximum(m * jnp.minimum(n, nt - 1) - 1, 0), cb + 3 * hb))

    own = pl.BlockSpec((tile, lanes), lambda hb, n: (jnp.minimum(n, nt - 1), hb))
    late = pl.BlockSpec((tile, 3 * lanes), lambda hb, n: (jnp.maximum(n - 1, 0), (D_GROUP * group) // (3 * lanes) + hb))
    acc = pltpu.VMEM((2, tile, lanes), F32)
    return pl.pallas_call(
        body, name=name, grid=(4 // hp, nt + 1),
        in_specs=[cur(qb), cur(kb), prev(kb), cur(vb), prev(vb), own, own, own, own] + ([] if dest is None else [_ANY]),
        out_specs=late,
        out_shape=jax.ShapeDtypeStruct((T, D_PA), BF16),
        input_output_aliases={} if dest is None else {9: 0},
        scratch_shapes=[acc, acc, acc],
        compiler_params=_cparams(2),
    )(proj, proj, proj, proj, proj, o, lse, do, dl, *([] if dest is None else [dest]))


HALO = 8
HALO_IN = 16


def _shifted(ext):
    return [ext[HALO:, :]] + [pltpu.roll(ext, k, 0)[HALO:, :] for k in range(1, CONV_WIDTH)]


def _conv_taps(sh, w):
    c = sh[0] * w[CONV_WIDTH - 1]
    for k in range(1, CONV_WIDTH):
        c = c + sh[k] * w[CONV_WIDTH - 1 - k]
    return c


def _taps(w_ref):
    return [w_ref[j:j + 1, :] for j in range(CONV_WIDTH)]


def _dn_pre_fwd(name, proj, ba, conv_w, alog_v, dtb_v, tm, T):
    C = 3 * D_KEY_B

    def body(x_ref, ba_ref, w_ref, al_ref, dt_ref, q_ref, k_ref, v_ref, bg_ref, halo_ref):
        i = pl.program_id(0)

        @pl.when(i == 0)
        def _():
            halo_ref[...] = jnp.zeros_like(halo_ref)

        x = x_ref[...].astype(F32)
        ext = jnp.concatenate([halo_ref[...], x], axis=0)
        c = _conv_taps(_shifted(ext), _taps(w_ref))
        halo_ref[...] = x[tm - HALO:, :]
        qn, kn, v, bg = _f_dnpoint(c, ba_ref[...], al_ref[...], dt_ref[...])
        q_ref[...] = qn
        k_ref[...] = kn
        v_ref[...] = v
        bg_ref[...] = bg

    return pl.pallas_call(
        body, name=name, grid=(T // tm,),
        in_specs=[_row_spec(tm, C, 0), _row_spec(tm, 128, 0), _full_spec(conv_w.shape), _full_spec((1, 128)),
                  _full_spec((1, 128))],
        out_specs=[_row_spec(tm, D_KEY_B, 0)] * 3 + [_row_spec(tm, 128, 0)],
        out_shape=[jax.ShapeDtypeStruct((T, D_KEY_B), F32)] * 3 + [jax.ShapeDtypeStruct((T, 128), F32)],
        scratch_shapes=[pltpu.VMEM((HALO, C), F32)],
        compiler_params=_ARB,
    )(proj, ba, conv_w, alog_v, dtb_v)


def _dn_pre_bwd(name, proj, ba, conv_w, alog_v, dtb_v, dq, dk, dv, dbg, tm, T):
    C = 3 * D_KEY_B
    nt = T // tm
    hb = tm // HALO_IN

    def body(x_ref, xh_ref, ba_ref, w_ref, al_ref, dt_ref, dq_ref, dk_ref, dv_ref, dbg_ref,
             dx_ref, dba_ref, dw_ref, dal_ref, ddt_ref, nxt_ref):
        s = pl.program_id(0)
        i = nt - 1 - s

        @pl.when(s == 0)
        def _():
            nxt_ref[...] = jnp.zeros_like(nxt_ref)
            dw_ref[...] = jnp.zeros_like(dw_ref)
            dal_ref[...] = jnp.zeros_like(dal_ref)
            ddt_ref[...] = jnp.zeros_like(ddt_ref)

        w = _taps(w_ref)
        halo = jnp.where(i > 0, xh_ref[...].astype(F32)[HALO_IN - HALO:, :], 0.0)
        ext = jnp.concatenate([halo, x_ref[...].astype(F32)], axis=0)
        sh = _shifted(ext)
        c = _conv_taps(sh, w)
        dc, dba, dal, ddt = _dnpoint_bwd(c, ba_ref[...], al_ref[...], dt_ref[...],
                                         dq_ref[...].astype(F32), dk_ref[...].astype(F32), dv_ref[...].astype(F32), dbg_ref[...])
        dba_ref[...] = dba
        dal_ref[...] += dal
        ddt_ref[...] += ddt
        ext2 = jnp.concatenate([dc, nxt_ref[...]], axis=0)
        dx = dc * w[CONV_WIDTH - 1]
        for m in range(1, CONV_WIDTH):
            dx = dx + pltpu.roll(ext2, tm + HALO - m, 0)[:tm, :] * w[CONV_WIDTH - 1 - m]
        dx_ref[...] = dx.astype(dx_ref.dtype)
        nxt_ref[...] = dc[:HALO, :]
        for j in range(CONV_WIDTH):
            dw_ref[j:j + 1, :] += jnp.sum(sh[CONV_WIDTH - 1 - j] * dc, axis=0, keepdims=True)

    def rev(w):
        return pl.BlockSpec((tm, w), lambda s: (nt - 1 - s, 0))

    return pl.pallas_call(
        body, name=name, grid=(nt,),
        in_specs=[rev(C), pl.BlockSpec((HALO_IN, C), lambda s: (jnp.maximum((nt - 1 - s) * hb - 1, 0), 0)), rev(128),
                  _full_spec(conv_w.shape), _full_spec((1, 128)), _full_spec((1, 128)),
                  rev(D_KEY_B), rev(D_KEY_B), rev(D_KEY_B), rev(128)],
        out_specs=[rev(C), rev(128), _full_spec((8, C)), _full_spec((1, 128)), _full_spec((1, 128))],
        out_shape=[jax.ShapeDtypeStruct((T, C), BF16), jax.ShapeDtypeStruct((T, 128), F32),
                   jax.ShapeDtypeStruct((8, C), F32), jax.ShapeDtypeStruct((1, 128), F32),
                   jax.ShapeDtypeStruct((1, 128), F32)],
        scratch_shapes=[pltpu.VMEM((HALO, C), F32)],
        compiler_params=_ARB,
    )(proj, proj, ba, conv_w, alog_v, dtb_v, dq, dk, dv, dbg)


def _bdot(a, b, dims, exact=False):
    if exact:
        return lax.dot_general(a, b, dims, preferred_element_type=F32, precision=lax.Precision.HIGH)
    return lax.dot_general(a.astype(BF16), b.astype(BF16), dims, preferred_element_type=F32)


_B_NN = (((2,), (1,)), ((0,), (0,)))
_B_NT = (((2,), (2,)), ((0,), (0,)))
_B_TN = (((1,), (1,)), ((0,), (0,)))


def _unit_lower_inverse(a):
    C = a.shape[-1]
    r = lax.broadcasted_iota(jnp.int32, (C, C), 0)
    c = lax.broadcasted_iota(jnp.int32, (C, C), 1)
    pw = -a
    t = jnp.where((r == c)[None], 1.0, 0.0) + pw
    for _ in range(int(math.log2(C)) - 1):
        pw = _bdot(pw, pw, _B_NN, exact=True)
        t = t + _bdot(t, pw, _B_NN, exact=True)
    return t


@jax.custom_vjp
def _solve_given_inverse(t, a, rhs):
    return _bdot(t, rhs, _B_NN, exact=True)


def _solve_fwd(t, a, rhs):
    sol = _bdot(t, rhs, _B_NN, exact=True)
    return sol, (t, sol)


def _solve_bwd(res, dsol):
    t, sol = res
    drhs = _bdot(t, dsol, _B_TN, exact=True)
    return jnp.zeros_like(t), -_bdot(drhs, sol, _B_NT, exact=True), drhs


_solve_given_inverse.defvjp(_solve_fwd, _solve_bwd)


def _stack(xs):
    return jnp.concatenate([x[None] for x in xs], axis=0)


def _delta_chunk(S, q, k, v, bg, t_inv=None):
    H, C = N_HEADS_B, CHUNK
    r = lax.broadcasted_iota(jnp.int32, (C, C), 0)
    c = lax.broadcasted_iota(jnp.int32, (C, C), 1)
    incl, strict, eye = r >= c, r > c, r == c
    gc_all = jnp.dot(incl.astype(F32), bg, preferred_element_type=F32, precision=lax.Precision.HIGHEST)
    lane = lax.broadcasted_iota(jnp.int32, (C, 128), 1)

    def col(x, l):
        return jnp.sum(jnp.where(lane == l, x, 0.0), axis=1, keepdims=True)

    beta = _stack([col(bg, h) for h in range(H)])
    gc = _stack([col(gc_all, H + h) for h in range(H)])
    gc_row = jnp.sum(jnp.where(eye[None], jnp.broadcast_to(gc, (H, C, C)), 0.0), axis=1, keepdims=True)
    row = lax.broadcasted_iota(jnp.int32, (H, C, 1), 1)
    gc_last = jnp.sum(jnp.where(row == C - 1, gc, 0.0), axis=1, keepdims=True)
    decay = jnp.exp(jnp.where(incl[None], gc - gc_row, -jnp.inf))
    qh = _stack(_heads(q, H))
    kh = _stack(_heads(k, H))
    vh = _stack(_heads(v, H))
    kk = _bdot(kh, kh, _B_NT, exact=True)
    a = jnp.where(strict[None], beta * kk * decay, 0.0)
    egc = jnp.exp(gc)
    rhs = jnp.concatenate([beta * vh, (beta * egc) * kh], axis=-1)
    if t_inv is None:
        t_inv = _unit_lower_inverse(a)
        sol = _bdot(t_inv, rhs, _B_NN, exact=True)
    else:
        sol = _solve_given_inverse(t_inv, a, rhs)
    u_bar, w = sol[..., :HEAD_DIM], sol[..., HEAD_DIM:]
    qk = _bdot(qh, kh, _B_NT) * decay
    q_dec = qh * egc
    k_dec = kh * jnp.exp(gc_last - gc)
    u = u_bar - _bdot(w, S, _B_NN)
    o = _bdot(q_dec, S, _B_NN) + _bdot(qk, u, _B_NN)
    S_new = jnp.exp(gc_last) * S + _bdot(k_dec, u, _B_TN)
    o2 = jnp.concatenate([lax.index_in_dim(o, h, 0, keepdims=False) for h in range(H)], axis=1)
    return o2, S_new, t_inv


def _delta_fwd(name, q, k, v, bg, T):
    nc = T // CHUNK
    H = N_HEADS_B
    G, R = CHUNKS_FWD, CHUNKS_FWD * CHUNK

    def body(q_ref, k_ref, v_ref, bg_ref, o_ref, s_out_ref, t_out_ref, s_ref):
        @pl.when(pl.program_id(0) == 0)
        def _():
            s_ref[...] = jnp.zeros_like(s_ref)

        S = s_ref[...]
        for c in range(G):
            rows = slice(c * CHUNK, (c + 1) * CHUNK)
            s_out_ref[c] = S
            o, S, t_inv = _delta_chunk(S, q_ref[rows, :], k_ref[rows, :], v_ref[rows, :], bg_ref[rows, :])
            o_ref[rows, :] = o.astype(o_ref.dtype)
            t_out_ref[c] = t_inv
        s_ref[...] = S

    return pl.pallas_call(
        body, name=name, grid=(nc // G,),
        in_specs=[_row_spec(R, D_KEY_B, 0)] * 3 + [_row_spec(R, 128, 0)],
        out_specs=[_row_spec(R, D_KEY_B, 0), pl.BlockSpec((G, H, HEAD_DIM, HEAD_DIM), lambda i: (i, 0, 0, 0)),
                   pl.BlockSpec((G, H, CHUNK, CHUNK), lambda i: (i, 0, 0, 0))],
        out_shape=[jax.ShapeDtypeStruct((T, D_KEY_B), BF16), jax.ShapeDtypeStruct((nc, H, HEAD_DIM, HEAD_DIM), F32),
                   jax.ShapeDtypeStruct((nc, H, CHUNK, CHUNK), F32)],
        scratch_shapes=[pltpu.VMEM((H, HEAD_DIM, HEAD_DIM), F32)],
        compiler_params=_ARB,
    )(q, k, v, bg)


def _delta_bwd(name, q, k, v, bg, s_all, t_all, do, T):
    nc = T // CHUNK
    H = N_HEADS_B
    G, R = CHUNKS_BWD, CHUNKS_BWD * CHUNK
    ns = nc // G

    def body(q_ref, k_ref, v_ref, bg_ref, s_in_ref, t_ref, do_ref, dq_ref, dk_ref, dv_ref, dbg_ref, ds_ref):
        @pl.when(pl.program_id(0) == 0)
        def _():
            ds_ref[...] = jnp.zeros_like(ds_ref)

        dS = ds_ref[...]
        for c in reversed(range(G)):
            rows = slice(c * CHUNK, (c + 1) * CHUNK)
            t_inv = t_ref[c]

            def f(S, qq, kk, vv, bb, t_inv=t_inv):
                return _delta_chunk(S, qq, kk, vv, bb, t_inv)[:2]

            _, vjp = jax.vjp(f, s_in_ref[c], q_ref[rows, :], k_ref[rows, :], v_ref[rows, :], bg_ref[rows, :])
            dS, dq, dk, dv, dbg = vjp((do_ref[rows, :].astype(F32), dS))
            dq_ref[rows, :] = dq.astype(dq_ref.dtype)
            dk_ref[rows, :] = dk.astype(dk_ref.dtype)
            dv_ref[rows, :] = dv.astype(dv_ref.dtype)
            dbg_ref[rows, :] = dbg
        ds_ref[...] = dS

    def rev(w):
        return pl.BlockSpec((R, w), lambda s: (ns - 1 - s, 0))

    def rev4(n):
        return pl.BlockSpec((G, H, n, n), lambda s: (ns - 1 - s, 0, 0, 0))

    return pl.pallas_call(
        body, name=name, grid=(ns,),
        in_specs=[rev(D_KEY_B)] * 3 + [rev(128), rev4(HEAD_DIM), rev4(CHUNK), rev(D_KEY_B)],
        out_specs=[rev(D_KEY_B)] * 3 + [rev(128)],
        out_shape=[jax.ShapeDtypeStruct((T, D_KEY_B), F32)] * 3 + [jax.ShapeDtypeStruct((T, 128), F32)],
        scratch_shapes=[pltpu.VMEM((H, HEAD_DIM, HEAD_DIM), F32)],
        compiler_params=_ARB,
    )(q, k, v, bg, s_all, t_all, do)


def _loss_fb(name, h2, w, tgt, tm, T):
    def tile_loss(h, wv, t):
        y = _f_rms(h, wv)[0]
        e = y - t
        return 0.5 * jnp.sum(jnp.mean(e * e, axis=-1))

    def body(h_ref, w_ref, t_ref, dh_ref, loss_ref, dw_ref):
        @pl.when(pl.program_id(0) == 0)
        def _():
            loss_ref[...] = jnp.zeros_like(loss_ref)
            dw_ref[...] = jnp.zeros_like(dw_ref)

        val, (dh, dw) = jax.value_and_grad(tile_loss, argnums=(0, 1))(h_ref[...], w_ref[...], t_ref[...])
        dh_ref[...] = dh
        dw_ref[...] += dw
        loss_ref[...] += jnp.reshape(val, (1, 1))

    return pl.pallas_call(
        body, name=name, grid=(T // tm,),
        in_specs=[_row_spec(tm, D_MODEL, 0), _full_spec((1, D_MODEL)), _row_spec(tm, D_MODEL, 0)],
        out_specs=[_row_spec(tm, D_MODEL, 0), _full_spec((1, 1)), _full_spec((1, D_MODEL))],
        out_shape=[jax.ShapeDtypeStruct((T, D_MODEL), F32), jax.ShapeDtypeStruct((1, 1), F32),
                   jax.ShapeDtypeStruct((1, D_MODEL), F32)],
        compiler_params=_ARB,
    )(h2, w, tgt)


def _adamw(name, w, g, m, v, tile):
    R = w.shape[0]

    def body(w_ref, g_ref, m_ref, v_ref, d_out, m_out, v_out):
        g = g_ref[...]
        mm = ADAM_B1 * m_ref[...] + (1.0 - ADAM_B1) * g
        vv = ADAM_B2 * v_ref[...] + (1.0 - ADAM_B2) * jnp.square(g)
        m_hat = mm / (1.0 - ADAM_B1 ** ADAM_STEP)
        v_hat = vv / (1.0 - ADAM_B2 ** ADAM_STEP)
        d_out[...] = -ADAM_LR * (m_hat / (jnp.sqrt(v_hat) + ADAM_EPS) + ADAM_WD * w_ref[...])
        m_out[...] = mm
        v_out[...] = vv

    assert R % tile == 0
    spec = pl.BlockSpec((tile, w.shape[1]), lambda i: (i, 0))
    return pl.pallas_call(
        body, name=name, grid=(R // tile,), in_specs=[spec] * 4, out_specs=[spec] * 3,
        out_shape=[jax.ShapeDtypeStruct(w.shape, F32)] * 3, compiler_params=_ARB,
    )(w, g, m, v)


def _sum4(name, r, tile):
    _, R, C = r.shape
    assert R % tile == 0

    def body(r_ref, o_ref):
        o_ref[...] = ((r_ref[0].astype(F32) + r_ref[1].astype(F32)) + r_ref[2].astype(F32)) + r_ref[3].astype(F32)

    return pl.pallas_call(
        body, name=name, grid=(R // tile,),
        in_specs=[pl.BlockSpec((4, tile, C), lambda i: (0, i, 0))],
        out_specs=pl.BlockSpec((tile, C), lambda i: (i, 0)),
        out_shape=jax.ShapeDtypeStruct((R, C), F32), compiler_params=_ARB,
    )(r)


def _add2(name, a, b, tile):
    R, C = a.shape
    assert R % tile == 0

    def body(a_ref, b_ref, o_ref):
        o_ref[...] = (a_ref[...].astype(F32) + b_ref[...].astype(F32)).astype(BF16)

    spec = pl.BlockSpec((tile, C), lambda i: (i, 0))
    return pl.pallas_call(
        body, name=name, grid=(R // tile,), in_specs=[spec, spec], out_specs=spec,
        out_shape=jax.ShapeDtypeStruct((R, C), BF16), compiler_params=_ARB,
    )(a, b)


_ANY = pl.BlockSpec(memory_space=pl.ANY)


def _place():
    return lax.axis_index("x"), lax.axis_index("y"), lax.axis_index("c")


def _chip_exchange(in_refs, out_refs, send_sems, recv_sems, local_sems, gather):
    n = len(in_refs)
    x, y, c = _place()
    if gather == "swap":
        def swaps():
            return [pltpu.make_async_remote_copy(src_ref=in_refs[a], dst_ref=out_refs[a], send_sem=send_sems.at[a],
                                                 recv_sem=recv_sems.at[a], device_id=(x, y, 1 - c), device_id_type=MESH)
                    for a in range(n)]

        def start_swaps():
            for cp in swaps():
                cp.start()

        def finish_swaps():
            for cp in swaps():
                cp.wait()

        return start_swaps, finish_swaps
    me = 2 * x + y
    chips = [(1 - x, y), (x, 1 - y), (1 - x, 1 - y)]

    def src(a, slot):
        return in_refs[a] if gather else in_refs[a].at[slot]

    def copy(a, k, src_slot, dst_slot, px, py):
        return pltpu.make_async_remote_copy(src_ref=src(a, src_slot), dst_ref=out_refs[a].at[dst_slot],
                                            send_sem=send_sems.at[3 * a + k], recv_sem=recv_sems.at[3 * a + k],
                                            device_id=(px, py, c), device_id_type=MESH)

    def mine():
        return [pltpu.make_async_copy(src(a, me), out_refs[a].at[me], local_sems.at[a]) for a in range(n)]

    def sends():
        return [copy(a, k, 2 * px + py, me, px, py) for a in range(n) for k, (px, py) in enumerate(chips)]

    def start():
        for cp in mine() + sends():
            cp.start()

    def finish():
        for a in range(n):
            for k, (px, py) in enumerate(chips):
                copy(a, k, me, 2 * px + py, px, py).wait_recv()
        for cp in sends():
            cp.wait_send()
        for cp in mine():
            cp.wait()

    return start, finish


def _chip_exchange_shapes(arrays, gather):
    return [jax.ShapeDtypeStruct(((4,) + a.shape) if gather is True else a.shape, a.dtype) for a in arrays]


def _chip_exchange_sems(n):
    return [pltpu.SemaphoreType.DMA((3 * n,)), pltpu.SemaphoreType.DMA((3 * n,)), pltpu.SemaphoreType.DMA((n,))]


def _exchange_chips(name, arrays, gather):
    n = len(arrays)

    def body(*refs):
        start, finish = _chip_exchange(refs[:n], refs[n:2 * n], *refs[2 * n:], gather=gather)
        start()
        finish()

    return pl.pallas_call(
        body, name=name, in_specs=[_ANY] * n, out_specs=[_ANY] * n,
        out_shape=_chip_exchange_shapes(arrays, gather), scratch_shapes=_chip_exchange_sems(n),
    )(*arrays)


def _swap_sibling(name, arrays):
    n = len(arrays)

    def body(*refs):
        in_refs, out_refs, send_sems, recv_sems = refs[:n], refs[n:2 * n], refs[2 * n], refs[2 * n + 1]
        x, y, c = _place()
        cps = [pltpu.make_async_remote_copy(src_ref=in_refs[a], dst_ref=out_refs[a], send_sem=send_sems.at[a],
                                            recv_sem=recv_sems.at[a], device_id=(x, y, 1 - c), device_id_type=MESH)
               for a in range(n)]
        for cp in cps:
            cp.start()
        for cp in cps:
            cp.wait()

    return pl.pallas_call(
        body, name=name, in_specs=[_ANY] * n, out_specs=[_ANY] * n,
        out_shape=[jax.ShapeDtypeStruct(a.shape, a.dtype) for a in arrays],
        scratch_shapes=[pltpu.SemaphoreType.DMA((n,)), pltpu.SemaphoreType.DMA((n,))],
    )(*arrays)


def _allsum_small(name, pack):
    def body(p_ref, out_ref, slots, send_sems, recv_sems):
        x, y, c = _place()
        me = 4 * x + 2 * y + c
        slots[me] = p_ref[...]
        sends = []
        for k in range(1, 8):
            px = 1 - x if k & 4 else x
            py = 1 - y if k & 2 else y
            pc = 1 - c if k & 1 else c
            cp = pltpu.make_async_remote_copy(src_ref=p_ref, dst_ref=slots.at[me], send_sem=send_sems.at[k - 1],
                                              recv_sem=recv_sems.at[k - 1], device_id=(px, py, pc), device_id_type=MESH)
            cp.start()
            sends.append(cp)
        for k in range(1, 8):
            px = 1 - x if k & 4 else x
            py = 1 - y if k & 2 else y
            pc = 1 - c if k & 1 else c
            pltpu.make_async_remote_copy(src_ref=p_ref, dst_ref=slots.at[4 * px + 2 * py + pc],
                                         send_sem=send_sems.at[k - 1], recv_sem=recv_sems.at[k - 1],
                                         device_id=(px, py, pc), device_id_type=MESH).wait_recv()
        for cp in sends:
            cp.wait_send()
        acc = slots[0]
        for s in range(1, 8):
            acc = acc + slots[s]
        out_ref[...] = acc

    vm = pl.BlockSpec(memory_space=pltpu.VMEM)
    return pl.pallas_call(
        body, name=name, in_specs=[vm], out_specs=vm,
        out_shape=jax.ShapeDtypeStruct(pack.shape, pack.dtype),
        scratch_shapes=[pltpu.VMEM((8,) + pack.shape, pack.dtype), pltpu.SemaphoreType.DMA((7,)),
                        pltpu.SemaphoreType.DMA((7,))],
    )(pack)


def _pack_shards(parts, rows_total):
    flat = [p.reshape(-1, D_MODEL) for p in parts]
    used = sum(f.shape[0] for f in flat)
    return jnp.concatenate(flat + [jnp.zeros((rows_total - used, D_MODEL), flat[0].dtype)], axis=0)


def _unpack_shards(buf, shapes):
    out, r0 = [], 0
    for shp in shapes:
        n = math.prod(shp) // D_MODEL
        out.append(buf[r0:r0 + n].reshape(shp))
        r0 += n
    return out


def kernel(x, norm_mix, w_in, conv_w, a_log, dt_bias, dn_norm, w_proj_attn, w_proj_delta, w_out, norm_ffn, w_gate, w_up, w_down, norm_final, loss_target, m_norm_mix, m_w_in, m_conv_w, m_a_log, m_dt_bias, m_dn_norm, m_w_proj_attn, m_w_proj_delta, m_w_out, m_norm_ffn, m_w_gate, m_w_up, m_w_down, m_norm_final, v_norm_mix, v_w_in, v_conv_w, v_a_log, v_dt_bias, v_dn_norm, v_w_proj_attn, v_w_proj_delta, v_w_out, v_norm_ffn, v_w_gate, v_w_up, v_w_down, v_norm_final):
    T = x.shape[1]
    D = D_MODEL
    tm = 512
    tw = 256
    mt = 1024 if T % 1024 == 0 else 512
    x2 = x.reshape(T, D)
    tgt = loss_target.reshape(T, D)

    packed = (w_proj_attn, w_proj_delta, w_out, w_gate, w_up, w_down)
    conv_bits = lax.bitcast_convert_type(conv_w[0], BF16)
    w16 = _pack_shards([w[0].astype(BF16) for w in packed] + [conv_bits], PACK_R)
    win16 = w_in[0].astype(BF16)
    my_c = lax.axis_index("c")

    def both_halves(mine, other, axis):
        return jnp.concatenate([jnp.where(my_c == 0, mine, other), jnp.where(my_c == 0, other, mine)], axis=axis)

    pack_half = lax.dynamic_slice_in_dim(w16, my_c * HALF, HALF, axis=0)
    win_half = lax.dynamic_slice_in_dim(win16, my_c * W_IN_HALF, W_IN_HALF, axis=0)
    u, win_mine = _rowwise("rms_mix", _f_rms, [(x2, 0, D)], [norm_mix], [(D, BF16)], tm, T, side=([win_half], True))
    (win_other,) = _swap_sibling("swap_w_in", [win_mine])
    win4 = both_halves(win_mine, win_other, 1)

    def win_cols(g0, g1):
        out = []
        for j in range(4):
            lo, hi = max(g0, W_IN_COLS * j), min(g1, W_IN_COLS * (j + 1))
            if lo < hi:
                out.append(win4[j][:, lo - W_IN_COLS * j:hi - W_IN_COLS * j])
        return out

    Wa = jnp.concatenate([p for c0, w in _att_columns() for p in win_cols(c0, c0 + w)], axis=1)
    Wb = jnp.concatenate(win_cols(4608, 8704) + win_cols(8720, D_IN), axis=1)
    Wba = jnp.pad(jnp.concatenate(win_cols(8704, 8720), axis=1), ((0, 0), (0, 112)))
    alog_v = jnp.pad(a_log, ((0, 0), (8, 112)))
    dtb_v = jnp.pad(dt_bias, ((0, 0), (8, 112)))

    proj_a, pack_mine = _matmul("mm_in_a", u, Wa, "nn", T, D_PA, D, mt, 2304, D, F32, side=([pack_half], True))
    proj_b, pack_other = _matmul("mm_in_b", u, Wb, "nn", T, D_PB, D, mt, 3072, D, BF16, side=([pack_mine], "swap"))
    wg = both_halves(pack_mine, pack_other, 1)

    def seg(r0, n):
        return wg[:, r0:r0 + n, :]

    r = 0
    Wpa = seg(r, 128).reshape(4, D_ATTN_OUT, 256).transpose(1, 0, 2).reshape(D_ATTN_OUT, D); r += 128
    Wpd = seg(r, 256).reshape(D, D); r += 256
    Wout = seg(r, 256).reshape(D, D); r += 256
    Wg = seg(r, 704).reshape(4, D, 704).transpose(1, 0, 2).reshape(D, D_FF); r += 704
    Wu = seg(r, 704).reshape(4, D, 704).transpose(1, 0, 2).reshape(D, D_FF); r += 704
    Wd = seg(r, 704).reshape(D_FF, D); r += 704
    convw = lax.bitcast_convert_type(seg(r, 6).reshape(4, CONV_WIDTH, 768, 2), F32)
    convw = convw.transpose(1, 0, 2).reshape(CONV_WIDTH, 3 * D_KEY_B)
    Wgu = jnp.concatenate([Wg, Wu], axis=1)
    ba = _matmul("mm_ba", u, Wba, "nn", T, 128, D, mt, 128, D, F32)

    att = [_attn_fwd(f"attn_fwd{g}", proj_a, g, T) for g in range(3)]
    (ya_m,) = _rowwise("merge", _f_merge, [(att[g][0], 0, 512) for g in range(3)] + [(att[g][1], 0, 512) for g in range(3)],
                       [], [(512, BF16)], tm, T)
    qn, kn, vd, bg = _dn_pre_fwd("dn_pre_fwd", proj_b, ba, convw, alog_v, dtb_v, tw, T)
    o_d, s_all, t_all = _delta_fwd("delta_fwd", qn, kn, vd, bg, T)
    (od,) = _rowwise("post_dn", _f_postdn, [(o_d, 0, D), (proj_b, C_Z // D, D)], [dn_norm], [(D, BF16)], tm, T)
    y_a = _matmul("mm_pa", ya_m, Wpa, "nn", T, D, 512, mt, D, 512, BF16)
    y_b = _matmul("mm_pd", od, Wpd, "nn", T, D, D, 2 * mt, D, D, BF16)
    (mix,) = _rowwise("mix", _f_mix, [(y_a, 0, D), (y_b, 0, D), (proj_b, C_GATE // 2048, 2048)], [], [(D, BF16)], tm, T)
    h1 = _matmul("mm_out", mix, Wout, "nn", T, D, D, mt, D, D, F32, residual=x2)
    (hn,) = _rowwise("rms_ffn", _f_rms, [(h1, 0, D)], [norm_ffn], [(D, BF16)], tm, T)
    gu = _matmul("mm_gu", hn, Wgu, "nn", T, 2 * D_FF, D, mt, 2816, D, BF16)
    h2 = _matmul("mm_down", gu, Wd, "nn", T, D, D_FF, mt, D, 1408, F32, residual=h1, a_swiglu=D_FF)
    dh2, loss, g_norm_final = _loss_fb("loss", h2, norm_final.reshape(1, D), tgt, tm, T)

    dact = _matmul("mm_dact", dh2, Wd, "nt", T, D_FF, D, mt, D_FF, D, BF16)
    gWd = _matmul("mm_gwd", gu, dh2, "tn", D_FF, D, T, 1408, D, mt, BF16, a_swiglu=D_FF)
    (dgu,) = _rowwise_vjp("act_bwd", _f_act, [(gu, 0, 2 * D_FF)], [], [(dact, 0, D_FF)], [BF16], tw, T)
    dhn = _matmul("mm_dhn", dgu, Wgu, "nt", T, D, 2 * D_FF, mt, D, 2816, F32)
    gWgu = _matmul("mm_gwgu", hn, dgu, "tn", D, 2 * D_FF, T, D, 1408, 2 * mt, BF16)
    dh1, g_norm_ffn = _rowwise_vjp("rms_ffn_bwd", _f_rms, [(h1, 0, D)], [norm_ffn], [(dhn, 0, D)], [F32], tm, T,
                                   residual=(dh2, 0, D))
    dmix = _matmul("mm_dmix", dh1, Wout, "nt", T, D, D, mt, D, D, BF16)
    gWout = _matmul("mm_gwout", mix, dh1, "tn", D, D, T, D, D, mt, BF16)
    dya, dyb, dgate = _rowwise_vjp("mix_bwd", _f_mix, [(y_a, 0, D), (y_b, 0, D), (proj_b, C_GATE // 2048, 2048)], [],
                                   [(dmix, 0, D)], [BF16, BF16, BF16], tm, T)
    dya_m = _matmul("mm_dyam", dya, Wpa, "nt", T, 512, D, mt, 512, D, BF16)
    gWpa = _matmul("mm_gwpa", ya_m, dya, "tn", 512, D, T, 512, D, mt, BF16)
    dod = _matmul("mm_dod", dyb, Wpd, "nt", T, D, D, 2 * mt, D, D, BF16)
    gWpd = _matmul("mm_gwpd", od, dyb, "tn", D, D, T, D, D, mt, BF16)
    do_d, dz, g_dn_norm = _rowwise_vjp("post_dn_bwd", _f_postdn, [(o_d, 0, D), (proj_b, C_Z // D, D)], [dn_norm],
                                       [(dod, 0, D)], [BF16, BF16], tm, T)
    dqn, dkn, dvd, dbg = _delta_bwd("delta_bwd", qn, kn, vd, bg, s_all, t_all, do_d, T)
    dqkvd, dba, g_conv8, g_alog_v, g_dtb_v = _dn_pre_bwd("dn_pre_bwd", proj_b, ba, convw, alog_v, dtb_v,
                                                         dqn, dkn, dvd, dbg, tw, T)
    def pair_sum(tag, g, h):
        keep = lax.dynamic_slice_in_dim(g, my_c * h, h, axis=1)
        give = lax.dynamic_slice_in_dim(g, (1 - my_c) * h, h, axis=1)
        (got,) = _swap_sibling("swap_grads_" + tag, [give])
        return _add2("add_cores_" + tag, keep.reshape(-1, g.shape[2]), got.reshape(-1, g.shape[2]), HALF_TILE).reshape(keep.shape)

    def cols4(g, n):
        return g.reshape(g.shape[0], 4, n).transpose(1, 0, 2).reshape(4, -1, D)

    parts = [cols4(gWpa, 256), gWpd.reshape(4, 256, D), gWout.reshape(4, 256, D), cols4(gWgu[:, :D_FF], 704),
             cols4(gWgu[:, D_FF:], 704), gWd.reshape(4, 704, D), cols4(g_conv8[:CONV_WIDTH], 768)]
    used = sum(p.shape[1] for p in parts)
    g_pack = jnp.concatenate([p.astype(BF16) for p in parts] + [jnp.zeros((4, PACK_R - used, D), BF16)], axis=1)
    pair_pack = pair_sum("pack", g_pack, HALF)

    mg = _rowwise_vjp("merge_bwd", _f_merge,
                      [(att[g][0], 0, 512) for g in range(3)] + [(att[g][1], 0, 512) for g in range(3)], [],
                      [(dya_m, 0, 512)], [F32] * 6, tm, T)
    dproj_a = None
    for g in range(3):
        dproj_a = _attn_bwd(f"attn_bwd{g}", proj_a, att[g][0], att[g][1], mg[g], mg[3 + g], g, T, dproj_a)
    dproj_b = [dqkvd, dz, dgate]
    gWa, recv_pack = _matmul("mm_gwa", u, dproj_a, "tn", D, D_PA, T, D, 1536, 2 * mt, BF16, side=([pair_pack], False))
    gWb = _matmul_pieces("mm_gwb", "tn", dproj_b, u, D, D_PB, T, D, 1024, 2 * mt, BF16)
    gWba = _matmul("mm_gwba", u, dba, "tn", D, 128, T, D, 128, mt, BF16)

    pieces, off = [(gWb[:, :C_GATE], D_PA), (gWba[:, :16], 8704), (gWb[:, C_GATE:], 8720)], 0
    for c0, w in _att_columns():
        pieces.append((gWa[:, off:off + w], c0))
        off += w
    pieces.sort(key=lambda t: t[1])

    def win_shard(j):
        out = []
        for arr, g0 in pieces:
            lo, hi = max(g0, W_IN_COLS * j), min(g0 + arr.shape[1], W_IN_COLS * (j + 1))
            if lo < hi:
                out.append(arr[:, lo - g0:hi - g0])
        return jnp.concatenate(out, axis=1)

    g_win4 = jnp.concatenate([win_shard(j)[None] for j in range(4)], axis=0)
    pair_win = pair_sum("w_in", g_win4, W_IN_HALF)

    du0 = _matmul("mm_du_ba", dba, Wba, "nt", T, D, 128, mt, D, 128, F32)
    du1 = _matmul_pieces("mm_du_b", "nt", dproj_b, Wb, T, D, D_PB, mt, D, 1024, F32, residual=du0)
    du, recv_win = _matmul("mm_du_a", dproj_a, Wa, "nt", T, D, D_PA, mt, D, 2304, F32, residual=du1, side=([pair_win], False))
    dx, g_norm_mix = _rowwise_vjp("rms_mix_bwd", _f_rms, [(x2, 0, D)], [norm_mix], [(du, 0, D)], [F32], tm, T,
                                  residual=(dh1, 0, D))

    g_mine = [_sum4("sum_chips_pack", recv_pack, HALF_TILE), _sum4("sum_chips_w_in", recv_win, HALF_TILE)]
    g_other = _swap_sibling("swap_sums", g_mine)
    g_all = both_halves(g_mine[0], g_other[0], 0)
    g_win = both_halves(g_mine[1], g_other[1], 0)

    locals_ = (w_proj_attn, w_proj_delta, w_out, w_gate, w_up, w_down, conv_w, w_in)
    ms = (m_w_proj_attn, m_w_proj_delta, m_w_out, m_w_gate, m_w_up, m_w_down, m_conv_w, m_w_in)
    vs = (v_w_proj_attn, v_w_proj_delta, v_w_out, v_w_gate, v_w_up, v_w_down, v_conv_w, v_w_in)
    big_names = ("w_proj_attn", "w_proj_delta", "w_out", "w_gate", "w_up", "w_down", "conv_w", "w_in")
    adam_tiles = (512, 256, 256, 256, 256, 352, CONV_WIDTH, 128)
    g_shards = _unpack_shards(g_all, [w.shape for w in locals_[:-1]]) + [g_win.reshape(w_in.shape)]
    big = [g_shards, [], [], []]
    for nm, w, g, m_, v_, tl in zip(big_names, locals_, g_shards, ms, vs, adam_tiles):
        res = _adamw("adamw_" + nm, w[0], g[0], m_[0], v_[0], tl)
        for kind in range(3):
            big[kind + 1].append(res[kind].reshape(w.shape))

    def small_pack(nm, nf, nfin, dn, al, dt, extra):
        row3 = jnp.concatenate([dn.reshape(1, 128), al.reshape(1, 8), dt.reshape(1, 8), extra.reshape(1, 1),
                                jnp.zeros((1, D - 145), F32)], axis=1)
        return jnp.concatenate([nm.reshape(1, D), nf.reshape(1, D), nfin.reshape(1, D), row3, jnp.zeros((4, D), F32)], axis=0)

    zero1 = jnp.zeros((1, 1), F32)
    sp = small_pack(g_norm_mix, g_norm_ffn, g_norm_final, g_dn_norm, g_alog_v[:, 8:16], g_dtb_v[:, 8:16], loss)
    tot = _allsum_small("allsum_small", sp)
    sm = _adamw("adamw_small", small_pack(norm_mix, norm_ffn, norm_final, dn_norm, a_log, dt_bias, zero1), tot,
                small_pack(m_norm_mix, m_norm_ffn, m_norm_final, m_dn_norm, m_a_log, m_dt_bias, zero1),
                small_pack(v_norm_mix, v_norm_ffn, v_norm_final, v_dn_norm, v_a_log, v_dt_bias, zero1), 8)
    sm = [tot] + list(sm)

    def small_unpack(p):
        return {"norm_mix": p[0:1], "norm_ffn": p[1:2], "norm_final": p[2], "dn_norm": p[3:4, :128],
                "a_log": p[3:4, 128:136], "dt_bias": p[3:4, 136:144]}

    sm = [small_unpack(p) for p in sm]
    loss_out = tot[3, 144]

    order = ("norm_mix", "w_in", "conv_w", "a_log", "dt_bias", "dn_norm", "w_proj_attn", "w_proj_delta", "w_out",
             "norm_ffn", "w_gate", "w_up", "w_down", "norm_final")
    outs = [loss_out, dx.reshape(x.shape)]
    for kind in range(4):
        for n in order:
            outs.append(big[kind][big_names.index(n)] if n in big_names else sm[kind][n])
    return tuple(outs)
```
